```python
import jax, jax.numpy as jnp
from jax import lax
import numpy as np

D_MODEL = 1024
BATCH = 8
SEQ = 16384
DEPTH = 4

CHUNK = 64
N_MIXERS = 3
N_LAYERS_A = (DEPTH + 2) // 3
N_LAYERS_B = (DEPTH + 1) // 3
N_LAYERS_C = DEPTH // 3
SHORT_CONV_WIDTH = 3
POOL_WINDOWS = (2, 4, 8, 16)
N_POOL_GROUPS = len(POOL_WINDOWS)
POOL_GROUP_DIM = D_MODEL // N_POOL_GROUPS
CONFORMER_CONV_WIDTH = 31
FFN_DIM = 2816
FFN_CONV_WIDTH = 3
RMS_EPS = 1e-6
LN_EPS = 1e-5

kernel_name = "hybrid_conv_pool_conformer_trunk"


def rmsnorm(x, g):
    xf = x.astype(jnp.float32)
    y = xf * lax.rsqrt(jnp.mean(xf * xf, axis=-1, keepdims=True) + RMS_EPS)
    return (y * g.astype(jnp.float32)).astype(x.dtype)


def layernorm(x, g, b):
    xf = x.astype(jnp.float32)
    mu = jnp.mean(xf, axis=-1, keepdims=True)
    var = jnp.mean(jnp.square(xf - mu), axis=-1, keepdims=True)
    y = (xf - mu) * lax.rsqrt(var + LN_EPS)
    return (y * g.astype(jnp.float32) + b.astype(jnp.float32)).astype(x.dtype)


def causal_depthwise_conv(x, w, b=None):
    k, ch = w.shape
    xp = jnp.pad(x, ((0, 0), (k - 1, 0), (0, 0)))
    y = lax.conv_general_dilated(
        xp, w[:, None, :].astype(x.dtype), window_strides=(1,), padding="VALID",
        dimension_numbers=("NWC", "WIO", "NWC"), feature_group_count=ch)
    if b is not None:
        y = y + b
    return y


def short_conv_mixer(h, w_in, w_conv, w_out):
    bgate, cgate, hin = jnp.split(h @ w_in, 3, axis=-1)
    u = causal_depthwise_conv(cgate * hin, w_conv)
    return (bgate * u) @ w_out


def trailing_mean(xf, window):
    s = xf.shape[1]
    csum = jnp.cumsum(xf, axis=1)
    shifted = jnp.pad(csum, ((0, 0), (window, 0), (0, 0)))[:, :s]
    count = jnp.minimum(jnp.arange(1, s + 1), window).astype(jnp.float32)[None, :, None]
    return (csum - shifted) / count


def pooling_mixer(h, w_group, b_group, scale):
    bsz, s, d = h.shape
    hf = h.astype(jnp.float32).reshape(bsz, s, N_POOL_GROUPS, POOL_GROUP_DIM)
    pooled = jnp.stack(
        [trailing_mean(hf[:, :, g], w) - hf[:, :, g] for g, w in enumerate(POOL_WINDOWS)],
        axis=2).astype(h.dtype)
    y = jnp.einsum("bsgc,gcd->bsgd", pooled, w_group).reshape(bsz, s, d)
    return (y + b_group) * scale


def conformer_conv_module(h, w_pw1, b_pw1, w_dw, b_dw, ln_g, ln_b, w_pw2, b_pw2):
    a, g = jnp.split(h @ w_pw1 + b_pw1, 2, axis=-1)
    u = a * jax.nn.sigmoid(g)
    u = causal_depthwise_conv(u, w_dw, b_dw)
    u = jax.nn.silu(layernorm(u, ln_g, ln_b))
    return u @ w_pw2 + b_pw2


def conv_ffn(h, w_up, w_conv, b_conv, w_down):
    gate, val = jnp.split(h @ w_up, 2, axis=-1)
    gate = causal_depthwise_conv(gate, w_conv, b_conv)
    return (jax.nn.silu(gate) * val) @ w_down


def _fwd_setup_inputs(seed: int = 0) -> dict:
    key = jax.random.key(seed)
    ks = iter(jax.random.split(key, 32))
    D = D_MODEL

    def nrm(shape, scale):
        return jax.random.normal(next(ks), shape, jnp.float32) * scale

    return {
        "x": nrm((BATCH, SEQ, D), 1.0),
        "c": nrm((BATCH, D), 1.0),
        "w_mod": nrm((DEPTH, D, 6 * D), 0.5 * D ** -0.5),
        "b_mod": nrm((DEPTH, 6 * D), 0.02),
        "norm_g": 1.0 + nrm((DEPTH, 4, D), 0.05),
        "sc_w_in": nrm((N_LAYERS_A, D, 3 * D), D ** -0.5),
        "sc_conv": nrm((N_LAYERS_A, SHORT_CONV_WIDTH, D), SHORT_CONV_WIDTH ** -0.5),
        "sc_w_out": nrm((N_LAYERS_A, D, D), D ** -0.5),
        "pool_w": nrm((N_LAYERS_B, N_POOL_GROUPS, POOL_GROUP_DIM, POOL_GROUP_DIM), POOL_GROUP_DIM ** -0.5),
        "pool_b": nrm((N_LAYERS_B, D), 0.02),
        "pool_scale": 1.0 + nrm((N_LAYERS_B, D), 0.05),
        "cf_w_pw1": nrm((N_LAYERS_C, D, 2 * D), D ** -0.5),
        "cf_b_pw1": nrm((N_LAYERS_C, 2 * D), 0.02),
        "cf_w_dw": nrm((N_LAYERS_C, CONFORMER_CONV_WIDTH, D), CONFORMER_CONV_WIDTH ** -0.5),
        "cf_b_dw": nrm((N_LAYERS_C, D), 0.02),
        "cf_ln_g": 1.0 + nrm((N_LAYERS_C, D), 0.05),
        "cf_ln_b": nrm((N_LAYERS_C, D), 0.02),
        "cf_w_pw2": nrm((N_LAYERS_C, D, D), D ** -0.5),
        "cf_b_pw2": nrm((N_LAYERS_C, D), 0.02),
        "ffn_w_up": nrm((DEPTH, D, 2 * FFN_DIM), D ** -0.5),
        "ffn_conv": nrm((DEPTH, FFN_CONV_WIDTH, FFN_DIM), FFN_CONV_WIDTH ** -0.5),
        "ffn_b_conv": nrm((DEPTH, FFN_DIM), 0.02),
        "ffn_w_down": nrm((DEPTH, FFN_DIM, D), FFN_DIM ** -0.5),
    }


def _fwd_reference(x, c, w_mod, b_mod, norm_g, sc_w_in, sc_conv, sc_w_out, pool_w, pool_b, pool_scale,
              cf_w_pw1, cf_b_pw1, cf_w_dw, cf_b_dw, cf_ln_g, cf_ln_b, cf_w_pw2, cf_b_pw2,
              ffn_w_up, ffn_conv, ffn_b_conv, ffn_w_down):
    c_act = jax.nn.silu(c)
    for i in range(DEPTH):
        mod = (c_act @ w_mod[i] + b_mod[i])[:, None, :]
        sh1, sc1, g1, sh2, sc2, g2 = jnp.split(mod, 6, axis=-1)

        h = rmsnorm(x, norm_g[i, 0]) * (1.0 + sc1) + sh1
        kind, j = i % N_MIXERS, i // N_MIXERS
        if kind == 0:
            m = short_conv_mixer(h, sc_w_in[j], sc_conv[j], sc_w_out[j])
        elif kind == 1:
            m = pooling_mixer(h, pool_w[j], pool_b[j], pool_scale[j])
        else:
            m = conformer_conv_module(h, cf_w_pw1[j], cf_b_pw1[j], cf_w_dw[j], cf_b_dw[j],
                                      cf_ln_g[j], cf_ln_b[j], cf_w_pw2[j], cf_b_pw2[j])
        x = x + g1 * rmsnorm(m, norm_g[i, 1])

        h = rmsnorm(x, norm_g[i, 2]) * (1.0 + sc2) + sh2
        f = conv_ffn(h, ffn_w_up[i], ffn_conv[i], ffn_b_conv[i], ffn_w_down[i])
        x = x + g2 * rmsnorm(f, norm_g[i, 3])
    return x


import jax as _jax
import jax.numpy as _jnp

TWIN_FORMAT = 'train_step'
FWD_PARAMS = ['x', 'c', 'w_mod', 'b_mod', 'norm_g', 'sc_w_in', 'sc_conv', 'sc_w_out', 'pool_w', 'pool_b', 'pool_scale', 'cf_w_pw1', 'cf_b_pw1', 'cf_w_dw', 'cf_b_dw', 'cf_ln_g', 'cf_ln_b', 'cf_w_pw2', 'cf_b_pw2', 'ffn_w_up', 'ffn_conv', 'ffn_b_conv', 'ffn_w_down']
TWIN_WEIGHTS = ['w_mod', 'b_mod', 'norm_g', 'sc_w_in', 'sc_conv', 'sc_w_out', 'pool_w', 'pool_b', 'pool_scale', 'cf_w_pw1', 'cf_b_pw1', 'cf_w_dw', 'cf_b_dw', 'cf_ln_g', 'cf_ln_b', 'cf_w_pw2', 'cf_b_pw2', 'ffn_w_up', 'ffn_conv', 'ffn_b_conv', 'ffn_w_down']
TWIN_DIFF_INPUT = 'x'
TWIN_INPUTS = ['x', 'c', 'w_mod', 'b_mod', 'norm_g', 'sc_w_in', 'sc_conv', 'sc_w_out', 'pool_w', 'pool_b', 'pool_scale', 'cf_w_pw1', 'cf_b_pw1', 'cf_w_dw', 'cf_b_dw', 'cf_ln_g', 'cf_ln_b', 'cf_w_pw2', 'cf_b_pw2', 'ffn_w_up', 'ffn_conv', 'ffn_b_conv', 'ffn_w_down', 'loss_target', 'm_w_mod', 'm_b_mod', 'm_norm_g', 'm_sc_w_in', 'm_sc_conv', 'm_sc_w_out', 'm_pool_w', 'm_pool_b', 'm_pool_scale', 'm_cf_w_pw1', 'm_cf_b_pw1', 'm_cf_w_dw', 'm_cf_b_dw', 'm_cf_ln_g', 'm_cf_ln_b', 'm_cf_w_pw2', 'm_cf_b_pw2', 'm_ffn_w_up', 'm_ffn_conv', 'm_ffn_b_conv', 'm_ffn_w_down', 'v_w_mod', 'v_b_mod', 'v_norm_g', 'v_sc_w_in', 'v_sc_conv', 'v_sc_w_out', 'v_pool_w', 'v_pool_b', 'v_pool_scale', 'v_cf_w_pw1', 'v_cf_b_pw1', 'v_cf_w_dw', 'v_cf_b_dw', 'v_cf_ln_g', 'v_cf_ln_b', 'v_cf_w_pw2', 'v_cf_b_pw2', 'v_ffn_w_up', 'v_ffn_conv', 'v_ffn_b_conv', 'v_ffn_w_down']
TWIN_OUTPUTS = ['loss', 'grad_x', 'grad_w_mod', 'grad_b_mod', 'grad_norm_g', 'grad_sc_w_in', 'grad_sc_conv', 'grad_sc_w_out', 'grad_pool_w', 'grad_pool_b', 'grad_pool_scale', 'grad_cf_w_pw1', 'grad_cf_b_pw1', 'grad_cf_w_dw', 'grad_cf_b_dw', 'grad_cf_ln_g', 'grad_cf_ln_b', 'grad_cf_w_pw2', 'grad_cf_b_pw2', 'grad_ffn_w_up', 'grad_ffn_conv', 'grad_ffn_b_conv', 'grad_ffn_w_down', 'delta_w_mod', 'delta_b_mod', 'delta_norm_g', 'delta_sc_w_in', 'delta_sc_conv', 'delta_sc_w_out', 'delta_pool_w', 'delta_pool_b', 'delta_pool_scale', 'delta_cf_w_pw1', 'delta_cf_b_pw1', 'delta_cf_w_dw', 'delta_cf_b_dw', 'delta_cf_ln_g', 'delta_cf_ln_b', 'delta_cf_w_pw2', 'delta_cf_b_pw2', 'delta_ffn_w_up', 'delta_ffn_conv', 'delta_ffn_b_conv', 'delta_ffn_w_down', 'new_m_w_mod', 'new_m_b_mod', 'new_m_norm_g', 'new_m_sc_w_in', 'new_m_sc_conv', 'new_m_sc_w_out', 'new_m_pool_w', 'new_m_pool_b', 'new_m_pool_scale', 'new_m_cf_w_pw1', 'new_m_cf_b_pw1', 'new_m_cf_w_dw', 'new_m_cf_b_dw', 'new_m_cf_ln_g', 'new_m_cf_ln_b', 'new_m_cf_w_pw2', 'new_m_cf_b_pw2', 'new_m_ffn_w_up', 'new_m_ffn_conv', 'new_m_ffn_b_conv', 'new_m_ffn_w_down', 'new_v_w_mod', 'new_v_b_mod', 'new_v_norm_g', 'new_v_sc_w_in', 'new_v_sc_conv', 'new_v_sc_w_out', 'new_v_pool_w', 'new_v_pool_b', 'new_v_pool_scale', 'new_v_cf_w_pw1', 'new_v_cf_b_pw1', 'new_v_cf_w_dw', 'new_v_cf_b_dw', 'new_v_cf_ln_g', 'new_v_cf_ln_b', 'new_v_cf_w_pw2', 'new_v_cf_b_pw2', 'new_v_ffn_w_up', 'new_v_ffn_conv', 'new_v_ffn_b_conv', 'new_v_ffn_w_down']
TWIN_LEAF_KINDS = {'loss': 'loss', 'grad_x': 'grad_x', 'grad_w_mod': 'grad_w', 'grad_b_mod': 'grad_w', 'grad_norm_g': 'grad_w', 'grad_sc_w_in': 'grad_w', 'grad_sc_conv': 'grad_w', 'grad_sc_w_out': 'grad_w', 'grad_pool_w': 'grad_w', 'grad_pool_b': 'grad_w', 'grad_pool_scale': 'grad_w', 'grad_cf_w_pw1': 'grad_w', 'grad_cf_b_pw1': 'grad_w', 'grad_cf_w_dw': 'grad_w', 'grad_cf_b_dw': 'grad_w', 'grad_cf_ln_g': 'grad_w', 'grad_cf_ln_b': 'grad_w', 'grad_cf_w_pw2': 'grad_w', 'grad_cf_b_pw2': 'grad_w', 'grad_ffn_w_up': 'grad_w', 'grad_ffn_conv': 'grad_w', 'grad_ffn_b_conv': 'grad_w', 'grad_ffn_w_down': 'grad_w', 'delta_w_mod': 'delta_w', 'delta_b_mod': 'delta_w', 'delta_norm_g': 'delta_w', 'delta_sc_w_in': 'delta_w', 'delta_sc_conv': 'delta_w', 'delta_sc_w_out': 'delta_w', 'delta_pool_w': 'delta_w', 'delta_pool_b': 'delta_w', 'delta_pool_scale': 'delta_w', 'delta_cf_w_pw1': 'delta_w', 'delta_cf_b_pw1': 'delta_w', 'delta_cf_w_dw': 'delta_w', 'delta_cf_b_dw': 'delta_w', 'delta_cf_ln_g': 'delta_w', 'delta_cf_ln_b': 'delta_w', 'delta_cf_w_pw2': 'delta_w', 'delta_cf_b_pw2': 'delta_w', 'delta_ffn_w_up': 'delta_w', 'delta_ffn_conv': 'delta_w', 'delta_ffn_b_conv': 'delta_w', 'delta_ffn_w_down': 'delta_w', 'new_m_w_mod': 'new_m', 'new_m_b_mod': 'new_m', 'new_m_norm_g': 'new_m', 'new_m_sc_w_in': 'new_m', 'new_m_sc_conv': 'new_m', 'new_m_sc_w_out': 'new_m', 'new_m_pool_w': 'new_m', 'new_m_pool_b': 'new_m', 'new_m_pool_scale': 'new_m', 'new_m_cf_w_pw1': 'new_m', 'new_m_cf_b_pw1': 'new_m', 'new_m_cf_w_dw': 'new_m', 'new_m_cf_b_dw': 'new_m', 'new_m_cf_ln_g': 'new_m', 'new_m_cf_ln_b': 'new_m', 'new_m_cf_w_pw2': 'new_m', 'new_m_cf_b_pw2': 'new_m', 'new_m_ffn_w_up': 'new_m', 'new_m_ffn_conv': 'new_m', 'new_m_ffn_b_conv': 'new_m', 'new_m_ffn_w_down': 'new_m', 'new_v_w_mod': 'new_v', 'new_v_b_mod': 'new_v', 'new_v_norm_g': 'new_v', 'new_v_sc_w_in': 'new_v', 'new_v_sc_conv': 'new_v', 'new_v_sc_w_out': 'new_v', 'new_v_pool_w': 'new_v', 'new_v_pool_b': 'new_v', 'new_v_pool_scale': 'new_v', 'new_v_cf_w_pw1': 'new_v', 'new_v_cf_b_pw1': 'new_v', 'new_v_cf_w_dw': 'new_v', 'new_v_cf_b_dw': 'new_v', 'new_v_cf_ln_g': 'new_v', 'new_v_cf_ln_b': 'new_v', 'new_v_cf_w_pw2': 'new_v', 'new_v_cf_b_pw2': 'new_v', 'new_v_ffn_w_up': 'new_v', 'new_v_ffn_conv': 'new_v', 'new_v_ffn_b_conv': 'new_v', 'new_v_ffn_w_down': 'new_v'}


def _forward(args):
    return _fwd_reference(*[args[k] for k in FWD_PARAMS])


def _output_shape():
    def fwd():
        inp = _fwd_setup_inputs(0)
        return _fwd_reference(*[inp[k] for k in FWD_PARAMS])
    out = _jax.eval_shape(fwd)
    return out.shape, out.dtype

N_MICROBATCH = 1
ADAM_LR = 0.001
ADAM_B1 = 0.9
ADAM_B2 = 0.999
ADAM_EPS = 1e-08
ADAM_WD = 0.01
ADAM_STEP = 10
PER_EXAMPLE_BATCH_AXIS = {'x': 0, 'c': 0, 'loss_target': 0}
SHARED_INPUTS = []
_WEIGHT_DTYPES = {'w_mod': _jnp.float32, 'b_mod': _jnp.float32, 'norm_g': _jnp.float32, 'sc_w_in': _jnp.float32, 'sc_conv': _jnp.float32, 'sc_w_out': _jnp.float32, 'pool_w': _jnp.float32, 'pool_b': _jnp.float32, 'pool_scale': _jnp.float32, 'cf_w_pw1': _jnp.float32, 'cf_b_pw1': _jnp.float32, 'cf_w_dw': _jnp.float32, 'cf_b_dw': _jnp.float32, 'cf_ln_g': _jnp.float32, 'cf_ln_b': _jnp.float32, 'cf_w_pw2': _jnp.float32, 'cf_b_pw2': _jnp.float32, 'ffn_w_up': _jnp.float32, 'ffn_conv': _jnp.float32, 'ffn_b_conv': _jnp.float32, 'ffn_w_down': _jnp.float32}
MOMENT_SCALE = {'w_mod': 4.664524e+00, 'b_mod': 1.043338e+01, 'norm_g': 9.249386e+00, 'sc_w_in': 2.395689e-01, 'sc_conv': 2.435092e-01, 'sc_w_out': 2.626555e-01, 'pool_w': 7.052551e-01, 'pool_b': 1.004274e+00, 'pool_scale': 6.165917e+00, 'cf_w_pw1': 2.640708e-01, 'cf_b_pw1': 1.089986e+00, 'cf_w_dw': 3.831585e-01, 'cf_b_dw': 2.575480e+00, 'cf_ln_g': 1.119811e+00, 'cf_ln_b': 1.818340e+00, 'cf_w_pw2': 6.957041e-01, 'cf_b_pw2': 3.801167e+00, 'ffn_w_up': 1.615097e-01, 'ffn_conv': 1.692361e-01, 'ffn_b_conv': 2.479922e-01, 'ffn_w_down': 3.027822e-01}


def _to_microbatches(a, axis):
    t = _jnp.moveaxis(a, axis, 0)
    t = t.reshape((N_MICROBATCH, t.shape[0] // N_MICROBATCH) + t.shape[1:])
    return _jnp.moveaxis(t, 1, axis + 1)


def setup_inputs(seed: int = 0) -> dict:
    inp = _fwd_setup_inputs(seed)
    key = _jax.random.fold_in(_jax.random.key(seed), 7919)
    shape, _ = _output_shape()
    out = dict(inp)
    out["loss_target"] = _jax.random.normal(_jax.random.fold_in(key, 0), shape, _jnp.float32)
    for i, name in enumerate(TWIN_WEIGHTS):
        w = inp[name].astype(_jnp.float32)
        if MOMENT_SCALE is None:
            s = _jnp.sqrt(_jnp.mean(_jnp.square(w)) + 1e-30)
        else:
            s = MOMENT_SCALE[name]
        km, kv = _jax.random.split(_jax.random.fold_in(key, i + 1))
        out[name] = w
        out["m_" + name] = s * _jax.random.normal(km, w.shape, _jnp.float32)
        out["v_" + name] = (s * s) * _jax.random.uniform(kv, w.shape, _jnp.float32, 0.5, 1.5)
    if N_MICROBATCH > 1:
        for name, axis in PER_EXAMPLE_BATCH_AXIS.items():
            out[name] = _to_microbatches(out[name], axis)
    return {'x': out['x'], 'c': out['c'], 'w_mod': out['w_mod'], 'b_mod': out['b_mod'], 'norm_g': out['norm_g'], 'sc_w_in': out['sc_w_in'], 'sc_conv': out['sc_conv'], 'sc_w_out': out['sc_w_out'], 'pool_w': out['pool_w'], 'pool_b': out['pool_b'], 'pool_scale': out['pool_scale'], 'cf_w_pw1': out['cf_w_pw1'], 'cf_b_pw1': out['cf_b_pw1'], 'cf_w_dw': out['cf_w_dw'], 'cf_b_dw': out['cf_b_dw'], 'cf_ln_g': out['cf_ln_g'], 'cf_ln_b': out['cf_ln_b'], 'cf_w_pw2': out['cf_w_pw2'], 'cf_b_pw2': out['cf_b_pw2'], 'ffn_w_up': out['ffn_w_up'], 'ffn_conv': out['ffn_conv'], 'ffn_b_conv': out['ffn_b_conv'], 'ffn_w_down': out['ffn_w_down'], 'loss_target': out['loss_target'], 'm_w_mod': out['m_w_mod'], 'm_b_mod': out['m_b_mod'], 'm_norm_g': out['m_norm_g'], 'm_sc_w_in': out['m_sc_w_in'], 'm_sc_conv': out['m_sc_conv'], 'm_sc_w_out': out['m_sc_w_out'], 'm_pool_w': out['m_pool_w'], 'm_pool_b': out['m_pool_b'], 'm_pool_scale': out['m_pool_scale'], 'm_cf_w_pw1': out['m_cf_w_pw1'], 'm_cf_b_pw1': out['m_cf_b_pw1'], 'm_cf_w_dw': out['m_cf_w_dw'], 'm_cf_b_dw': out['m_cf_b_dw'], 'm_cf_ln_g': out['m_cf_ln_g'], 'm_cf_ln_b': out['m_cf_ln_b'], 'm_cf_w_pw2': out['m_cf_w_pw2'], 'm_cf_b_pw2': out['m_cf_b_pw2'], 'm_ffn_w_up': out['m_ffn_w_up'], 'm_ffn_conv': out['m_ffn_conv'], 'm_ffn_b_conv': out['m_ffn_b_conv'], 'm_ffn_w_down': out['m_ffn_w_down'], 'v_w_mod': out['v_w_mod'], 'v_b_mod': out['v_b_mod'], 'v_norm_g': out['v_norm_g'], 'v_sc_w_in': out['v_sc_w_in'], 'v_sc_conv': out['v_sc_conv'], 'v_sc_w_out': out['v_sc_w_out'], 'v_pool_w': out['v_pool_w'], 'v_pool_b': out['v_pool_b'], 'v_pool_scale': out['v_pool_scale'], 'v_cf_w_pw1': out['v_cf_w_pw1'], 'v_cf_b_pw1': out['v_cf_b_pw1'], 'v_cf_w_dw': out['v_cf_w_dw'], 'v_cf_b_dw': out['v_cf_b_dw'], 'v_cf_ln_g': out['v_cf_ln_g'], 'v_cf_ln_b': out['v_cf_ln_b'], 'v_cf_w_pw2': out['v_cf_w_pw2'], 'v_cf_b_pw2': out['v_cf_b_pw2'], 'v_ffn_w_up': out['v_ffn_w_up'], 'v_ffn_conv': out['v_ffn_conv'], 'v_ffn_b_conv': out['v_ffn_b_conv'], 'v_ffn_w_down': out['v_ffn_w_down']}


def _loss(weights, diff, rest, loss_target):
    with _jax.named_scope("forward"):
        args = {**rest, TWIN_DIFF_INPUT: diff, **{k: w.astype(_WEIGHT_DTYPES[k]) for k, w in weights.items()}}
        y = _forward(args)
    with _jax.named_scope("loss_head"):
        err = _jnp.square(y.astype(_jnp.float32) - loss_target)
        return 0.5 * _jnp.sum(_jnp.mean(err, axis=-1)) if err.ndim else 0.5 * err


def _adamw(w, g, m, v):
    m = ADAM_B1 * m + (1.0 - ADAM_B1) * g
    v = ADAM_B2 * v + (1.0 - ADAM_B2) * _jnp.square(g)
    m_hat = m / (1.0 - ADAM_B1 ** ADAM_STEP)
    v_hat = v / (1.0 - ADAM_B2 ** ADAM_STEP)
    delta = -ADAM_LR * (m_hat / (_jnp.sqrt(v_hat) + ADAM_EPS) + ADAM_WD * w)
    return delta, m, v


def reference(x, c, w_mod, b_mod, norm_g, sc_w_in, sc_conv, sc_w_out, pool_w, pool_b, pool_scale, cf_w_pw1, cf_b_pw1, cf_w_dw, cf_b_dw, cf_ln_g, cf_ln_b, cf_w_pw2, cf_b_pw2, ffn_w_up, ffn_conv, ffn_b_conv, ffn_w_down, loss_target, m_w_mod, m_b_mod, m_norm_g, m_sc_w_in, m_sc_conv, m_sc_w_out, m_pool_w, m_pool_b, m_pool_scale, m_cf_w_pw1, m_cf_b_pw1, m_cf_w_dw, m_cf_b_dw, m_cf_ln_g, m_cf_ln_b, m_cf_w_pw2, m_cf_b_pw2, m_ffn_w_up, m_ffn_conv, m_ffn_b_conv, m_ffn_w_down, v_w_mod, v_b_mod, v_norm_g, v_sc_w_in, v_sc_conv, v_sc_w_out, v_pool_w, v_pool_b, v_pool_scale, v_cf_w_pw1, v_cf_b_pw1, v_cf_w_dw, v_cf_b_dw, v_cf_ln_g, v_cf_ln_b, v_cf_w_pw2, v_cf_b_pw2, v_ffn_w_up, v_ffn_conv, v_ffn_b_conv, v_ffn_w_down):
    given = dict(x=x, c=c, w_mod=w_mod, b_mod=b_mod, norm_g=norm_g, sc_w_in=sc_w_in, sc_conv=sc_conv, sc_w_out=sc_w_out, pool_w=pool_w, pool_b=pool_b, pool_scale=pool_scale, cf_w_pw1=cf_w_pw1, cf_b_pw1=cf_b_pw1, cf_w_dw=cf_w_dw, cf_b_dw=cf_b_dw, cf_ln_g=cf_ln_g, cf_ln_b=cf_ln_b, cf_w_pw2=cf_w_pw2, cf_b_pw2=cf_b_pw2, ffn_w_up=ffn_w_up, ffn_conv=ffn_conv, ffn_b_conv=ffn_b_conv, ffn_w_down=ffn_w_down, loss_target=loss_target, m_w_mod=m_w_mod, m_b_mod=m_b_mod, m_norm_g=m_norm_g, m_sc_w_in=m_sc_w_in, m_sc_conv=m_sc_conv, m_sc_w_out=m_sc_w_out, m_pool_w=m_pool_w, m_pool_b=m_pool_b, m_pool_scale=m_pool_scale, m_cf_w_pw1=m_cf_w_pw1, m_cf_b_pw1=m_cf_b_pw1, m_cf_w_dw=m_cf_w_dw, m_cf_b_dw=m_cf_b_dw, m_cf_ln_g=m_cf_ln_g, m_cf_ln_b=m_cf_ln_b, m_cf_w_pw2=m_cf_w_pw2, m_cf_b_pw2=m_cf_b_pw2, m_ffn_w_up=m_ffn_w_up, m_ffn_conv=m_ffn_conv, m_ffn_b_conv=m_ffn_b_conv, m_ffn_w_down=m_ffn_w_down, v_w_mod=v_w_mod, v_b_mod=v_b_mod, v_norm_g=v_norm_g, v_sc_w_in=v_sc_w_in, v_sc_conv=v_sc_conv, v_sc_w_out=v_sc_w_out, v_pool_w=v_pool_w, v_pool_b=v_pool_b, v_pool_scale=v_pool_scale, v_cf_w_pw1=v_cf_w_pw1, v_cf_b_pw1=v_cf_b_pw1, v_cf_w_dw=v_cf_w_dw, v_cf_b_dw=v_cf_b_dw, v_cf_ln_g=v_cf_ln_g, v_cf_ln_b=v_cf_ln_b, v_cf_w_pw2=v_cf_w_pw2, v_cf_b_pw2=v_cf_b_pw2, v_ffn_w_up=v_ffn_w_up, v_ffn_conv=v_ffn_conv, v_ffn_b_conv=v_ffn_b_conv, v_ffn_w_down=v_ffn_w_down)
    weights = {n: given[n] for n in TWIN_WEIGHTS}
    shared = {n: given[n] for n in SHARED_INPUTS}
    per_example = {n: given[n] for n in ['x', 'c']}
    grad_fn = _jax.value_and_grad(_loss, argnums=(0, 1))

    def one_microbatch(ex, loss_target):
        ex = dict(ex)
        diff = ex.pop(TWIN_DIFF_INPUT)
        return grad_fn(weights, diff, {**shared, **ex}, loss_target)

    if N_MICROBATCH == 1:
        loss, (grad_w, grad_x) = one_microbatch(per_example, given["loss_target"])
    else:
        def body(carry, xs):
            loss_sum, grad_sum = carry
            l_k, (gw_k, gx_k) = one_microbatch(xs[0], xs[1])
            with _jax.named_scope("update"):
                return (loss_sum + l_k, _jax.tree.map(_jnp.add, grad_sum, gw_k)), gx_k

        init = (_jnp.zeros((), _jnp.float32), _jax.tree.map(_jnp.zeros_like, weights))
        (loss, grad_w), grad_x = _jax.lax.scan(body, init, (per_example, given["loss_target"]))
    with _jax.named_scope("update"):
        delta_w, new_m, new_v = {}, {}, {}
        for n in TWIN_WEIGHTS:
            delta_w[n], new_m[n], new_v[n] = _adamw(weights[n], grad_w[n], given["m_" + n], given["v_" + n])
    return (loss, grad_x, *[grad_w[n] for n in TWIN_WEIGHTS], *[delta_w[n] for n in TWIN_WEIGHTS],
            *[new_m[n] for n in TWIN_WEIGHTS], *[new_v[n] for n in TWIN_WEIGHTS])
```

```python
import functools

import jax
import jax.numpy as jnp
from jax import lax
from jax.experimental import pallas as pl
from jax.experimental.pallas import tpu as pltpu

D = 1024
F = 2816
NDEV = 8
FB = F // 4
DEPTH = 4
RMS_EPS = 1e-6
LN_EPS = 1e-5
CFW = 31
POOL_WINDOWS = (2, 4, 8, 16)
PG = D // 4
LR, B1, B2, ADAM_EPS, WD, STEP = 0.001, 0.9, 0.999, 1e-08, 0.01, 10

BF = jnp.bfloat16
F32 = jnp.float32
VMEM_LIMIT_V7X = 56 * 1024 * 1024
MESH = pl.DeviceIdType.MESH
ANY = pl.BlockSpec(memory_space=pl.ANY)


def _params(n_axes):
    return pltpu.CompilerParams(dimension_semantics=("arbitrary",) * n_axes, vmem_limit_bytes=VMEM_LIMIT_V7X)


def _const(shape, single=True):
    nd = len(shape)
    if single:
        return pl.BlockSpec(shape, lambda *_: (0,) * nd, pipeline_mode=pl.Buffered(1))
    return pl.BlockSpec(shape, lambda *_: (0,) * nd)


def _rows(tm, c):
    return pl.BlockSpec((tm, c), lambda i: (i, 0))


def _brows(nb, tm, c, b0=0):
    return pl.BlockSpec((nb, tm, c), lambda i: (b0, i, 0))


def _prev(hb, c, tm):
    return pl.BlockSpec((hb, c), lambda i: (jnp.maximum(i * (tm // hb) - 1, 0), 0))


def _next(hb, c, tm, t):
    return pl.BlockSpec((hb, c), lambda i: (jnp.minimum((i + 1) * (tm // hb), t // hb - 1), 0))


def _bprev(nb, hb, c, tm, b0=0):
    return pl.BlockSpec((nb, hb, c), lambda i: (b0, jnp.maximum(i * (tm // hb) - 1, 0), 0))


def _bnext(nb, hb, c, tm, t, b0=0):
    return pl.BlockSpec((nb, hb, c), lambda i: (b0, jnp.minimum((i + 1) * (tm // hb), t // hb - 1), 0))


def _sigmoid(v):
    return 1.0 / (1.0 + jnp.exp(-v))


def _dot(a, b):
    return jnp.dot(a, b, preferred_element_type=F32)


def _dot_nt(a, b):
    return lax.dot_general(a, b, (((1,), (1,)), ((), ())), preferred_element_type=F32)


def _dot_tn(a, b):
    return lax.dot_general(a, b, (((0,), (0,)), ((), ())), preferred_element_type=F32)


def _rsum(v):
    return jnp.sum(v, axis=0, keepdims=True)


def _adaln(x, g, sc, sh):
    r = lax.rsqrt(jnp.mean(x * x, axis=-1, keepdims=True) + RMS_EPS)
    return (x * r * g) * (1.0 + sc) + sh


def _gated_res(x, m, gn, gt):
    r = lax.rsqrt(jnp.mean(m * m, axis=-1, keepdims=True) + RMS_EPS)
    return x + gt * (m * r * gn)


def _gated_res_bwd(dxo, m, gn, gt):
    r = lax.rsqrt(jnp.mean(m * m, axis=-1, keepdims=True) + RMS_EPS)
    mh = m * r
    dgt = _rsum(dxo * (mh * gn))
    dn = dxo * gt
    dgn = _rsum(dn * mh)
    dmh = dn * gn
    dm = r * (dmh - mh * jnp.mean(dmh * mh, axis=-1, keepdims=True))
    return dm, dgt, dgn


def fwd_in(x, g, sc, sh, w, bias, *, blocked, tm, name):
    t = x.shape[0]
    nb, _, bw = w.shape

    def body(*refs):
        if bias is None:
            x_ref, g_ref, sc_ref, sh_ref, w_ref, h_ref, p_ref = refs
        else:
            x_ref, g_ref, sc_ref, sh_ref, w_ref, b_ref, h_ref, p_ref = refs
        hb = _adaln(x_ref[...], g_ref[...], sc_ref[...], sh_ref[...]).astype(BF)
        h_ref[...] = hb
        for d in range(nb):
            y = _dot(hb, w_ref[d])
            if bias is not None:
                y = y + b_ref[d]
            if blocked:
                p_ref[d] = y.astype(BF)
            else:
                p_ref[:, d * bw:(d + 1) * bw] = y.astype(BF)

    vec = _const((1, D))
    in_specs = [_rows(tm, D), vec, vec, vec, _const((nb, D, bw))]
    args = [x, g, sc, sh, w]
    if bias is not None:
        in_specs.append(_const((nb, 1, bw)))
        args.append(bias)
    if blocked:
        p_spec, p_shape = _brows(nb, tm, bw), jax.ShapeDtypeStruct((nb, t, bw), BF)
    else:
        p_spec, p_shape = _rows(tm, nb * bw), jax.ShapeDtypeStruct((t, nb * bw), BF)
    return pl.pallas_call(
        body, name=name, grid=(t // tm,), in_specs=in_specs, out_specs=[_rows(tm, D), p_spec],
        out_shape=[jax.ShapeDtypeStruct((t, D), BF), p_shape], compiler_params=_params(1))(*args)


def _conv3_from(s_ref, w, tm, lo):
    acc = w[0:1, :] * s_ref[pl.ds(lo, tm), :]
    for k in (1, 2):
        acc = acc + w[k:k + 1, :] * s_ref[pl.ds(lo + k, tm), :]
    return acc


def sc_fwd_out(p, convw, w_out, x, gn, gt, *, tm, name):
    t = x.shape[0]

    def body(p_ref, ph_ref, cw_ref, w_ref, x_ref, gn_ref, gt_ref, x1_ref, m_ref, q_ref, s_ref):
        i = pl.program_id(0)
        zh = ph_ref[8:16, D:2 * D].astype(F32) * ph_ref[8:16, 2 * D:3 * D].astype(F32)
        s_ref[0:8, :] = jnp.where(i == 0, 0.0, zh)
        s_ref[8:8 + tm, :] = p_ref[:, D:2 * D].astype(F32) * p_ref[:, 2 * D:3 * D].astype(F32)
        u = _conv3_from(s_ref, cw_ref[...], tm, 6)
        qb = (p_ref[:, 0:D].astype(F32) * u).astype(BF)
        q_ref[...] = qb
        m = _dot(qb, w_ref[...])
        m_ref[...] = m.astype(BF)
        x1_ref[...] = _gated_res(x_ref[...], m, gn_ref[...], gt_ref[...])

    vec = _const((1, D))
    return pl.pallas_call(
        body, name=name, grid=(t // tm,),
        in_specs=[_rows(tm, 3 * D), _prev(16, 3 * D, tm), _const((3, D)), _const((D, D)), _rows(tm, D), vec, vec],
        out_specs=[_rows(tm, D)] * 3,
        out_shape=[jax.ShapeDtypeStruct((t, D), F32), jax.ShapeDtypeStruct((t, D), BF), jax.ShapeDtypeStruct((t, D), BF)],
        scratch_shapes=[pltpu.VMEM((tm + 8, D), F32)], compiler_params=_params(1))(p, p, convw, w_out, x, gn, gt)


def _layernorm_parts(u2):
    mu = jnp.mean(u2, axis=-1, keepdims=True)
    cen = u2 - mu
    rstd = lax.rsqrt(jnp.mean(cen * cen, axis=-1, keepdims=True) + LN_EPS)
    return cen * rstd, rstd


def cf_fwd_out(a, w_dw, b_dw, ln_g, ln_b, w_pw2, b_pw2, x, gn, gt, *, tm, name):
    t = x.shape[0]
    hb = 32

    def body(a_ref, ah_ref, wd_ref, bd_ref, lg_ref, lb_ref, w_ref, b2_ref, x_ref, gn_ref, gt_ref,
             x1_ref, m_ref, s_out_ref, u2_ref, s_ref):
        i = pl.program_id(0)
        uh = ah_ref[:, 0:D].astype(F32) * _sigmoid(ah_ref[:, D:2 * D].astype(F32))
        s_ref[0:hb, :] = jnp.where(i == 0, 0.0, uh)
        s_ref[hb:hb + tm, :] = a_ref[:, 0:D].astype(F32) * _sigmoid(a_ref[:, D:2 * D].astype(F32))
        acc = bd_ref[...] + wd_ref[0:1, :] * s_ref[pl.ds(hb - CFW + 1, tm), :]
        for k in range(1, CFW):
            acc = acc + wd_ref[k:k + 1, :] * s_ref[pl.ds(hb - CFW + 1 + k, tm), :]
        u2_ref[...] = acc.astype(BF)
        xh, _ = _layernorm_parts(acc)
        l = xh * lg_ref[...] + lb_ref[...]
        sb = (l * _sigmoid(l)).astype(BF)
        s_out_ref[...] = sb
        m = _dot(sb, w_ref[...]) + b2_ref[...]
        m_ref[...] = m.astype(BF)
        x1_ref[...] = _gated_res(x_ref[...], m, gn_ref[...], gt_ref[...])

    vec = _const((1, D))
    return pl.pallas_call(
        body, name=name, grid=(t // tm,),
        in_specs=[_rows(tm, 2 * D), _prev(hb, 2 * D, tm), _const((CFW, D)), vec, vec, vec, _const((D, D)), vec,
                  _rows(tm, D), vec, vec],
        out_specs=[_rows(tm, D)] * 4,
        out_shape=[jax.ShapeDtypeStruct((t, D), F32)] + [jax.ShapeDtypeStruct((t, D), BF)] * 3,
        scratch_shapes=[pltpu.VMEM((tm + hb, D), F32)], compiler_params=_params(1),
    )(a, a, w_dw, b_dw, ln_g, ln_b, w_pw2, b_pw2, x, gn, gt)


def _pool_counts(i, tm, w):
    row = lax.broadcasted_iota(jnp.int32, (tm, 1), 0) + i * tm
    return jnp.minimum(row + 1, w).astype(F32)


def pool_fwd(x, g, sc, sh, pw, pb, pscale, gn, gt, *, tm, name):
    t = x.shape[0]
    pad, hb = 8, 16
    base = pad + hb

    def body(x_ref, xh_ref, g_ref, sc_ref, sh_ref, pw_ref, pb_ref, ps_ref, gn_ref, gt_ref,
             x1_ref, m_ref, yp_ref, po_ref, sa_ref, sb_ref):
        i = pl.program_id(0)
        hh = _adaln(xh_ref[...], g_ref[...], sc_ref[...], sh_ref[...])
        h = _adaln(x_ref[...], g_ref[...], sc_ref[...], sh_ref[...])
        zero = jnp.zeros((pad, D), F32)
        sa_ref[0:pad, :] = zero
        sb_ref[0:pad, :] = zero
        sa_ref[pad:base, :] = jnp.where(i == 0, 0.0, hh)
        sa_ref[base:base + tm, :] = h
        n = hb + tm
        src, dst = sa_ref, sb_ref
        ys = []
        for gi, w in enumerate(POOL_WINDOWS):
            c0 = gi * PG
            step = w // 2
            dst[pl.ds(pad, n), c0:D] = src[pl.ds(pad, n), c0:D] + src[pl.ds(pad - step, n), c0:D]
            mean = dst[pl.ds(base, tm), c0:c0 + PG] / _pool_counts(i, tm, w)
            pooled = (mean - h[:, c0:c0 + PG]).astype(BF)
            po_ref[:, c0:c0 + PG] = pooled
            ys.append(_dot(pooled, pw_ref[gi]))
            src, dst = dst, src
        ypre = jnp.concatenate(ys, axis=1) + pb_ref[...]
        yp_ref[...] = ypre.astype(BF)
        m = ypre * ps_ref[...]
        m_ref[...] = m.astype(BF)
        x1_ref[...] = _gated_res(x_ref[...], m, gn_ref[...], gt_ref[...])

    vec = _const((1, D))
    return pl.pallas_call(
        body, name=name, grid=(t // tm,),
        in_specs=[_rows(tm, D), _prev(hb, D, tm), vec, vec, vec, _const((4, PG, PG)), vec, vec, vec, vec],
        out_specs=[_rows(tm, D)] * 4,
        out_shape=[jax.ShapeDtypeStruct((t, D), F32)] + [jax.ShapeDtypeStruct((t, D), BF)] * 3,
        scratch_shapes=[pltpu.VMEM((tm + base, D), F32)] * 2, compiler_params=_params(1),
    )(x, x, g, sc, sh, pw, pb, pscale, gn, gt)


def ffn_fwd_out(up, convw, convb, w_down, x, gn, gt, *, tm, name):
    t = x.shape[0]

    def body(gate_ref, gh_ref, val_ref, cw_ref, cb_ref, w_ref, x_ref, gn_ref, gt_ref,
             x2_ref, f_ref, gc_ref, a_ref, s_ref):
        i = pl.program_id(0)
        acc = jnp.zeros((tm, D), F32)
        for j in range(4):
            sj = s_ref.at[j]
            sj[0:8, :] = jnp.where(i == 0, 0.0, gh_ref[j, 8:16, :].astype(F32))
            sj[8:8 + tm, :] = gate_ref[j].astype(F32)
            gc = cb_ref[j] + _conv3_from(sj, cw_ref[j], tm, 6)
            gc_ref[j] = gc.astype(BF)
            ab = (gc * _sigmoid(gc) * val_ref[j].astype(F32)).astype(BF)
            a_ref[j] = ab
            acc = acc + _dot(ab, w_ref[j * FB:(j + 1) * FB, :])
        f_ref[...] = acc.astype(BF)
        x2_ref[...] = _gated_res(x_ref[...], acc, gn_ref[...], gt_ref[...])

    vec = _const((1, D))
    blk = jax.ShapeDtypeStruct((4, t, FB), BF)
    return pl.pallas_call(
        body, name=name, grid=(t // tm,),
        in_specs=[_brows(4, tm, FB, 0), _bprev(4, 16, FB, tm, 0), _brows(4, tm, FB, 1), _const((4, 3, FB)),
                  _const((4, 1, FB)), _const((F, D)), _rows(tm, D), vec, vec],
        out_specs=[_rows(tm, D), _rows(tm, D), _brows(4, tm, FB), _brows(4, tm, FB)],
        out_shape=[jax.ShapeDtypeStruct((t, D), F32), jax.ShapeDtypeStruct((t, D), BF), blk, blk],
        scratch_shapes=[pltpu.VMEM((4, tm + 8, FB), F32)], compiler_params=_params(1),
    )(up, up, up, convw, convb, w_down, x, gn, gt)


def loss_head(y, target, *, tm, name):
    t = y.shape[0]

    def body(y_ref, t_ref, dy_ref, l_ref, acc_ref):
        i = pl.program_id(0)

        @pl.when(i == 0)
        def _():
            acc_ref[...] = jnp.zeros_like(acc_ref)

        e = y_ref[...] - t_ref[...]
        dy_ref[...] = e * (1.0 / D)
        acc_ref[...] += _rsum(e * e)

        @pl.when(i == pl.num_programs(0) - 1)
        def _():
            l_ref[...] = jnp.sum(acc_ref[...], axis=1, keepdims=True) * (0.5 / D)

    return pl.pallas_call(
        body, name=name, grid=(t // tm,), in_specs=[_rows(tm, D), _rows(tm, D)],
        out_specs=[_rows(tm, D), pl.BlockSpec((1, 1), lambda i: (0, 0))],
        out_shape=[jax.ShapeDtypeStruct((t, D), F32), jax.ShapeDtypeStruct((1, 1), F32)],
        scratch_shapes=[pltpu.VMEM((1, D), F32)], compiler_params=_params(1))(y, target)


def _init_stats(ref):
    @pl.when(pl.program_id(0) == 0)
    def _():
        ref[...] = jnp.zeros_like(ref)


def bwd_out(dxo, m, gn, gt, w, *, blocked, tm, name):
    t = dxo.shape[0]
    k = w.shape[0]

    def body(dx_ref, m_ref, gn_ref, gt_ref, w_ref, dm_ref, da_ref, st_ref):
        _init_stats(st_ref)
        dm, dgt, dgn = _gated_res_bwd(dx_ref[...], m_ref[...].astype(F32), gn_ref[...], gt_ref[...])
        st_ref[0:1, :] += dgt
        st_ref[1:2, :] += dgn
        st_ref[2:3, :] += _rsum(dm)
        dmb = dm.astype(BF)
        dm_ref[...] = dmb
        if blocked:
            for j in range(4):
                da_ref[j] = _dot_nt(dmb, w_ref[j * FB:(j + 1) * FB, :]).astype(BF)
        else:
            da_ref[...] = _dot_nt(dmb, w_ref[...]).astype(BF)

    vec = _const((1, D))
    if blocked:
        da_spec, da_shape = _brows(4, tm, FB), jax.ShapeDtypeStruct((4, t, FB), BF)
    else:
        da_spec, da_shape = _rows(tm, k), jax.ShapeDtypeStruct((t, k), BF)
    return pl.pallas_call(
        body, name=name, grid=(t // tm,), in_specs=[_rows(tm, D), _rows(tm, D), vec, vec, _const((k, D))],
        out_specs=[_rows(tm, D), da_spec, _const((8, D), single=False)],
        out_shape=[jax.ShapeDtypeStruct((t, D), BF), da_shape, jax.ShapeDtypeStruct((8, D), F32)],
        compiler_params=_params(1))(dxo, m, gn, gt, w)


def bwd_in(dps, w, x, g, sc, dxo, *, tm, name):
    t = x.shape[0]
    direct = w is None
    if not direct:
        nb, _, bw = w.shape
    natural = (not direct) and dps[0].ndim == 2

    def body(*refs):
        n = len(dps)
        dp_refs = refs[:n]
        if direct:
            x_ref, g_ref, sc_ref, dxo_ref, dx_ref, st_ref = refs[n:]
            dh = dp_refs[0][...]
        else:
            w_ref, x_ref, g_ref, sc_ref, dxo_ref, dx_ref, st_ref = refs[n:]
            dh = jnp.zeros((tm, D), F32)
            if natural:
                for d in range(nb):
                    dh = dh + _dot_nt(dp_refs[0][:, d * bw:(d + 1) * bw], w_ref[d])
            else:
                d = 0
                for r in dp_refs:
                    for j in range(r.shape[0]):
                        dh = dh + _dot_nt(r[j], w_ref[d])
                        d += 1
        _init_stats(st_ref)
        x = x_ref[...]
        r = lax.rsqrt(jnp.mean(x * x, axis=-1, keepdims=True) + RMS_EPS)
        xh = x * r
        gv = g_ref[...]
        st_ref[0:1, :] += _rsum(dh)
        st_ref[1:2, :] += _rsum(dh * (xh * gv))
        dn = dh * (1.0 + sc_ref[...])
        st_ref[2:3, :] += _rsum(dn * xh)
        dy = dn * gv
        dx_ref[...] = dxo_ref[...] + r * (dy - xh * jnp.mean(dy * xh, axis=-1, keepdims=True))

    vec = _const((1, D))
    if direct:
        dp_specs = [_rows(tm, D)]
    elif natural:
        dp_specs = [_rows(tm, nb * bw)]
    else:
        dp_specs = [_brows(a.shape[0], tm, bw) for a in dps]
    w_specs, w_args = ([], []) if direct else ([_const((nb, D, bw))], [w])
    return pl.pallas_call(
        body, name=name, grid=(t // tm,),
        in_specs=dp_specs + w_specs + [_rows(tm, D), vec, vec, _rows(tm, D)],
        out_specs=[_rows(tm, D), _const((8, D), single=False)],
        out_shape=[jax.ShapeDtypeStruct((t, D), F32), jax.ShapeDtypeStruct((8, D), F32)],
        compiler_params=_params(1))(*dps, *w_args, x, g, sc, dxo)


def sc_bwd_mid(dq, p, convw, *, tm, name):
    t = dq.shape[0]

    def body(dq_ref, dqn_ref, p_ref, pp_ref, pn_ref, cw_ref, dp_ref, st_ref, s1_ref, s2_ref):
        i = pl.program_id(0)
        last = i == pl.num_programs(0) - 1
        _init_stats(st_ref)
        cw = cw_ref[...]
        cg = p_ref[:, D:2 * D].astype(F32)
        hi = p_ref[:, 2 * D:3 * D].astype(F32)
        bg = p_ref[:, 0:D].astype(F32)
        zp = pp_ref[8:16, D:2 * D].astype(F32) * pp_ref[8:16, 2 * D:3 * D].astype(F32)
        s2_ref[0:8, :] = jnp.where(i == 0, 0.0, zp)
        s2_ref[8:8 + tm, :] = cg * hi
        u = _conv3_from(s2_ref, cw, tm, 6)
        dqf = dq_ref[...].astype(F32)
        dp_ref[:, 0:D] = (dqf * u).astype(BF)
        du = dqf * bg
        duh = dqn_ref[0:8, :].astype(F32) * pn_ref[0:8, 0:D].astype(F32)
        s1_ref[0:tm, :] = du
        s1_ref[tm:tm + 8, :] = jnp.where(last, 0.0, duh)
        dz = cw[2:3, :] * du + cw[1:2, :] * s1_ref[pl.ds(1, tm), :] + cw[0:1, :] * s1_ref[pl.ds(2, tm), :]
        dp_ref[:, D:2 * D] = (dz * hi).astype(BF)
        dp_ref[:, 2 * D:3 * D] = (dz * cg).astype(BF)
        for k in range(3):
            st_ref[k:k + 1, :] += _rsum(du * s2_ref[pl.ds(6 + k, tm), :])

    return pl.pallas_call(
        body, name=name, grid=(t // tm,),
        in_specs=[_rows(tm, D), _next(16, D, tm, t), _rows(tm, 3 * D), _prev(16, 3 * D, tm), _next(16, 3 * D, tm, t),
                  _const((3, D))],
        out_specs=[_rows(tm, 3 * D), _const((8, D), single=False)],
        out_shape=[jax.ShapeDtypeStruct((t, 3 * D), BF), jax.ShapeDtypeStruct((8, D), F32)],
        scratch_shapes=[pltpu.VMEM((tm + 8, D), F32)] * 2, compiler_params=_params(1))(dq, dq, p, p, p, convw)


def cf_bwd_mid(ds, u2, a, w_dw, ln_g, ln_b, *, tm, name):
    t = ds.shape[0]
    hb = 32

    def du2_of(dsv, u2v, lg, lb):
        xh, rstd = _layernorm_parts(u2v)
        l = xh * lg + lb
        sg = _sigmoid(l)
        dl = dsv * (sg * (1.0 + l * (1.0 - sg)))
        dxh = dl * lg
        du2 = rstd * (dxh - jnp.mean(dxh, axis=-1, keepdims=True) - xh * jnp.mean(dxh * xh, axis=-1, keepdims=True))
        return du2, dl, xh

    def body(ds_ref, dsn_ref, u2_ref, u2n_ref, a_ref, ap_ref, wd_ref, lg_ref, lb_ref, da_ref, st_ref, s1_ref, s2_ref):
        i = pl.program_id(0)
        last = i == pl.num_programs(0) - 1
        _init_stats(st_ref)
        lg, lb = lg_ref[...], lb_ref[...]
        du2, dl, xh = du2_of(ds_ref[...].astype(F32), u2_ref[...].astype(F32), lg, lb)
        st_ref[32:33, :] += _rsum(dl * xh)
        st_ref[33:34, :] += _rsum(dl)
        st_ref[31:32, :] += _rsum(du2)
        du2n, _, _ = du2_of(dsn_ref[...].astype(F32), u2n_ref[...].astype(F32), lg, lb)
        s1_ref[0:tm, :] = du2
        s1_ref[tm:tm + hb, :] = jnp.where(last, 0.0, du2n)
        av = a_ref[:, 0:D].astype(F32)
        sg = _sigmoid(a_ref[:, D:2 * D].astype(F32))
        uh = ap_ref[:, 0:D].astype(F32) * _sigmoid(ap_ref[:, D:2 * D].astype(F32))
        s2_ref[0:hb, :] = jnp.where(i == 0, 0.0, uh)
        s2_ref[hb:hb + tm, :] = av * sg
        du = wd_ref[CFW - 1:CFW, :] * du2
        for k in range(CFW - 1):
            du = du + wd_ref[k:k + 1, :] * s1_ref[pl.ds(CFW - 1 - k, tm), :]
        for k in range(CFW):
            st_ref[k:k + 1, :] += _rsum(du2 * s2_ref[pl.ds(hb - CFW + 1 + k, tm), :])
        dav = du * sg
        dgv = du * av * sg * (1.0 - sg)
        st_ref[34:35, :] += _rsum(dav)
        st_ref[35:36, :] += _rsum(dgv)
        da_ref[:, 0:D] = dav.astype(BF)
        da_ref[:, D:2 * D] = dgv.astype(BF)

    vec = _const((1, D))
    return pl.pallas_call(
        body, name=name, grid=(t // tm,),
        in_specs=[_rows(tm, D), _next(hb, D, tm, t), _rows(tm, D), _next(hb, D, tm, t), _rows(tm, 2 * D),
                  _prev(hb, 2 * D, tm), _const((CFW, D)), vec, vec],
        out_specs=[_rows(tm, 2 * D), _const((40, D), single=False)],
        out_shape=[jax.ShapeDtypeStruct((t, 2 * D), BF), jax.ShapeDtypeStruct((40, D), F32)],
        scratch_shapes=[pltpu.VMEM((tm + hb, D), F32)] * 2, compiler_params=_params(1),
    )(ds, ds, u2, u2, a, a, w_dw, ln_g, ln_b)


def pool_bwd(dxo, m, ypre, pw, pscale, gn, gt, *, tm, name):
    t = dxo.shape[0]
    hb = 16

    def dyp_of(dxv, mv, ypv, ps, gnv, gtv):
        dm, dgt, dgn = _gated_res_bwd(dxv, mv, gnv, gtv)
        return dm * ps, dgt, dgn, _rsum(dm * ypv)

    def body(dx_ref, dxn_ref, m_ref, mn_ref, yp_ref, ypn_ref, pw_ref, ps_ref, gn_ref, gt_ref,
             dh_ref, dyp_ref, st_ref, sa_ref, sb_ref):
        i = pl.program_id(0)
        last = i == pl.num_programs(0) - 1
        _init_stats(st_ref)
        ps, gnv, gtv = ps_ref[...], gn_ref[...], gt_ref[...]
        dyp, dgt, dgn, dps = dyp_of(dx_ref[...], m_ref[...].astype(F32), yp_ref[...].astype(F32), ps, gnv, gtv)
        st_ref[0:1, :] += dgt
        st_ref[1:2, :] += dgn
        st_ref[2:3, :] += dps
        st_ref[3:4, :] += _rsum(dyp)
        dypb = dyp.astype(BF)
        dyp_ref[...] = dypb
        dypn, _, _, _ = dyp_of(dxn_ref[...], mn_ref[...].astype(F32), ypn_ref[...].astype(F32), ps, gnv, gtv)
        dypnb = jnp.where(last, 0.0, dypn).astype(BF)
        dpo = []
        for gi, w in enumerate(POOL_WINDOWS):
            c0 = gi * PG
            dp_main = _dot_nt(dypb[:, c0:c0 + PG], pw_ref[gi])
            dp_next = _dot_nt(dypnb[:, c0:c0 + PG], pw_ref[gi])
            dpo.append(dp_main)
            sa_ref[0:tm, c0:c0 + PG] = dp_main / _pool_counts(i, tm, w)
            sa_ref[tm:tm + hb, c0:c0 + PG] = dp_next / float(w)
        zero = jnp.zeros((8, D), F32)
        sa_ref[tm + hb:tm + hb + 8, :] = zero
        sb_ref[tm + hb:tm + hb + 8, :] = zero
        n = tm + hb
        src, dst = sa_ref, sb_ref
        for gi, w in enumerate(POOL_WINDOWS):
            c0 = gi * PG
            step = w // 2
            dst[pl.ds(0, n), c0:D] = src[pl.ds(0, n), c0:D] + src[pl.ds(step, n), c0:D]
            dh_ref[:, c0:c0 + PG] = dst[pl.ds(0, tm), c0:c0 + PG] - dpo[gi]
            src, dst = dst, src

    vec = _const((1, D))
    return pl.pallas_call(
        body, name=name, grid=(t // tm,),
        in_specs=[_rows(tm, D), _next(hb, D, tm, t), _rows(tm, D), _next(hb, D, tm, t), _rows(tm, D),
                  _next(hb, D, tm, t), _const((4, PG, PG)), vec, vec, vec],
        out_specs=[_rows(tm, D), _rows(tm, D), _const((8, D), single=False)],
        out_shape=[jax.ShapeDtypeStruct((t, D), F32), jax.ShapeDtypeStruct((t, D), BF), jax.ShapeDtypeStruct((8, D), F32)],
        scratch_shapes=[pltpu.VMEM((tm + hb + 8, D), F32)] * 2, compiler_params=_params(1),
    )(dxo, dxo, m, m, ypre, ypre, pw, pscale, gn, gt)


def ffn_bwd_mid(da, gc, up, convw, *, tm, name):
    t = da.shape[1]

    def dgc_of(dav, gcv, valv):
        sg = _sigmoid(gcv)
        return dav * valv * (sg * (1.0 + gcv * (1.0 - sg))), dav * (gcv * sg)

    def body(da_ref, dan_ref, gc_ref, gcn_ref, val_ref, valn_ref, gate_ref, gatep_ref, cw_ref,
             dup_ref, st_ref, s1_ref, s2_ref):
        i = pl.program_id(0)
        last = i == pl.num_programs(0) - 1
        _init_stats(st_ref)
        for j in range(4):
            cw = cw_ref[j]
            dgc, dval = dgc_of(da_ref[j].astype(F32), gc_ref[j].astype(F32), val_ref[j].astype(F32))
            dup_ref[4 + j] = dval.astype(BF)
            dgcn, _ = dgc_of(dan_ref[j, 0:8, :].astype(F32), gcn_ref[j, 0:8, :].astype(F32), valn_ref[j, 0:8, :].astype(F32))
            s1 = s1_ref.at[j]
            s1[0:tm, :] = dgc
            s1[tm:tm + 8, :] = jnp.where(last, 0.0, dgcn)
            dgate = cw[2:3, :] * dgc + cw[1:2, :] * s1[pl.ds(1, tm), :] + cw[0:1, :] * s1[pl.ds(2, tm), :]
            dup_ref[j] = dgate.astype(BF)
            s2 = s2_ref.at[j]
            s2[0:8, :] = jnp.where(i == 0, 0.0, gatep_ref[j, 8:16, :].astype(F32))
            s2[8:8 + tm, :] = gate_ref[j].astype(F32)
            st_ref[j, 0:1, :] += _rsum(dgc)
            for k in range(3):
                st_ref[j, k + 1:k + 2, :] += _rsum(dgc * s2[pl.ds(6 + k, tm), :])

    return pl.pallas_call(
        body, name=name, grid=(t // tm,),
        in_specs=[_brows(4, tm, FB), _bnext(4, 16, FB, tm, t), _brows(4, tm, FB), _bnext(4, 16, FB, tm, t),
                  _brows(4, tm, FB, 1), _bnext(4, 16, FB, tm, t, 1), _brows(4, tm, FB, 0), _bprev(4, 16, FB, tm, 0),
                  _const((4, 3, FB))],
        out_specs=[_brows(8, tm, FB), _const((4, 8, FB), single=False)],
        out_shape=[jax.ShapeDtypeStruct((8, t, FB), BF), jax.ShapeDtypeStruct((4, 8, FB), F32)],
        scratch_shapes=[pltpu.VMEM((4, tm + 8, FB), F32)] * 2, compiler_params=_params(1),
    )(da, da, gc, gc, up, up, up, up, convw)


def wgrad(a, b, *, nblk, a_blocked, b_blocked, bk, bn, tt, name):
    t = a.shape[1] if a.ndim == 3 else a.shape[0]
    nt = t // tt

    def body(a_ref, b_ref, o_ref, acc_ref):
        s = pl.program_id(1)

        @pl.when(s == 0)
        def _():
            acc_ref[...] = jnp.zeros_like(acc_ref)

        av = a_ref[0] if a.ndim == 3 else a_ref[...]
        bv = b_ref[0] if b.ndim == 3 else b_ref[...]
        acc_ref[...] += _dot_tn(av, bv)

        @pl.when(s == nt - 1)
        def _():
            o_ref[0] = acc_ref[...].astype(BF)

    def spec(arr, blocked, width):
        if arr.ndim == 3:
            return pl.BlockSpec((1, tt, width), lambda j, s: (j, s, 0))
        if blocked:
            return pl.BlockSpec((tt, width), lambda j, s: (s, j))
        return pl.BlockSpec((tt, width), lambda j, s: (s, 0))

    return pl.pallas_call(
        body, name=name, grid=(nblk, nt), in_specs=[spec(a, a_blocked, bk), spec(b, b_blocked, bn)],
        out_specs=pl.BlockSpec((1, bk, bn), lambda j, s: (j, 0, 0)),
        out_shape=jax.ShapeDtypeStruct((nblk, bk, bn), BF),
        scratch_shapes=[pltpu.VMEM((bk, bn), F32)], compiler_params=_params(2))(a, b)


def mod_partial(c_all, w_mod):
    cols = w_mod.shape[2]

    def body(c_ref, w_ref, o_ref):
        c = c_ref[...]
        ca = c * _sigmoid(c)
        o_ref[0] = jnp.dot(ca, w_ref[0], preferred_element_type=F32, precision=lax.Precision.HIGHEST)

    return pl.pallas_call(
        body, name="mod_partial", grid=(DEPTH,),
        in_specs=[pl.BlockSpec((NDEV, D), lambda l: (0, 0)), pl.BlockSpec((1, D, cols), lambda l: (l, 0, 0))],
        out_specs=pl.BlockSpec((1, NDEV, cols), lambda l: (l, 0, 0)),
        out_shape=jax.ShapeDtypeStruct((DEPTH, NDEV, cols), F32), compiler_params=_params(1))(c_all, w_mod)


def mod_finish(parts, b_mod):
    cols = parts.shape[2]

    def body(p_ref, b_ref, o_ref):
        for e in range(NDEV):
            o_ref[:, e * cols:(e + 1) * cols] = p_ref[e] + b_ref[:, e * cols:(e + 1) * cols]

    return pl.pallas_call(
        body, name="mod_finish", out_shape=jax.ShapeDtypeStruct((DEPTH, NDEV * cols), F32))(parts, b_mod)


def sum_parts(parts):
    n, r, c = parts.shape

    def body(p_ref, o_ref):
        acc = p_ref[0]
        for j in range(1, n):
            acc = acc + p_ref[j]
        o_ref[...] = acc

    return pl.pallas_call(body, name="sum_parts", out_shape=jax.ShapeDtypeStruct((r, c), F32))(parts)


def mod_wgrad(c_all_t, gmod_cols):
    cols = gmod_cols.shape[2]

    def body(c_ref, g_ref, o_ref):
        c = c_ref[...]
        ca = c * _sigmoid(c)
        acc = ca[:, 0:1] * g_ref[0, 0:1, :]
        for b in range(1, NDEV):
            acc = acc + ca[:, b:b + 1] * g_ref[0, b:b + 1, :]
        o_ref[0] = acc

    return pl.pallas_call(
        body, name="mod_wgrad", grid=(DEPTH,),
        in_specs=[pl.BlockSpec((D, NDEV), lambda l: (0, 0)), pl.BlockSpec((1, NDEV, cols), lambda l: (l, 0, 0))],
        out_specs=pl.BlockSpec((1, D, cols), lambda l: (l, 0, 0)),
        out_shape=jax.ShapeDtypeStruct((DEPTH, D, cols), F32), compiler_params=_params(1))(c_all_t, gmod_cols)


def _adamw_math(g, w, m, v):
    m2 = B1 * m + (1.0 - B1) * g
    v2 = B2 * v + (1.0 - B2) * (g * g)
    m_hat = m2 / (1.0 - B1 ** STEP)
    v_hat = v2 / (1.0 - B2 ** STEP)
    delta = -LR * (m_hat / (jnp.sqrt(v_hat) + ADAM_EPS) + WD * w)
    return delta, m2, v2


def _row_tile(r, c, budget=1 << 18):
    if r * c <= budget or r % 8:
        return r
    best = 8
    for cand in range(8, r + 1, 8):
        if r % cand == 0 and cand * c <= budget:
            best = cand
    return best


def adamw_sum(parts, w, m, v, *, name):
    n, r, c = parts.shape
    tr = _row_tile(r, c)

    def body(p_ref, w_ref, m_ref, v_ref, g_ref, d_ref, m2_ref, v2_ref):
        g = p_ref[0].astype(F32)
        for j in range(1, n):
            g = g + p_ref[j].astype(F32)
        d, m2, v2 = _adamw_math(g, w_ref[...], m_ref[...], v_ref[...])
        g_ref[...] = g
        d_ref[...] = d
        m2_ref[...] = m2
        v2_ref[...] = v2

    blk = pl.BlockSpec((tr, c), lambda i: (i, 0))
    out = jax.ShapeDtypeStruct((r, c), F32)
    return pl.pallas_call(
        body, name=name, grid=(r // tr,), in_specs=[pl.BlockSpec((n, tr, c), lambda i: (0, i, 0)), blk, blk, blk],
        out_specs=[blk] * 4, out_shape=[out] * 4, compiler_params=_params(1))(parts, w, m, v)


def _my_id():
    return 4 * lax.axis_index("x") + 2 * lax.axis_index("y") + lax.axis_index("c")


def _peer(k):
    x, y, c = lax.axis_index("x"), lax.axis_index("y"), lax.axis_index("c")
    px = 1 - x if k & 4 else x
    py = 1 - y if k & 2 else y
    pc = 1 - c if k & 1 else c
    return (px, py, pc), 4 * px + 2 * py + pc


def all_gather(shards, *, name):
    n = len(shards)

    def body(*refs):
        src, dst = refs[:n], refs[n:2 * n]
        send_sems, recv_sems, local_sems = refs[2 * n:]
        me = _my_id()
        copies = []
        for a in range(n):
            loc = pltpu.make_async_copy(src[a], dst[a].at[me], local_sems.at[a])
            loc.start()
            copies.append(loc)
        sends = []
        for k in range(1, NDEV):
            to, _ = _peer(k)
            for a in range(n):
                cp = pltpu.make_async_remote_copy(src_ref=src[a], dst_ref=dst[a].at[me], send_sem=send_sems.at[a * NDEV + k],
                                                  recv_sem=recv_sems.at[a * NDEV + k], device_id=to, device_id_type=MESH)
                cp.start()
                sends.append(cp)
        for k in range(1, NDEV):
            to, pid = _peer(k)
            for a in range(n):
                pltpu.make_async_remote_copy(src_ref=src[a], dst_ref=dst[a].at[pid], send_sem=send_sems.at[a * NDEV + k],
                                             recv_sem=recv_sems.at[a * NDEV + k], device_id=to, device_id_type=MESH).wait_recv()
        for cp in sends:
            cp.wait_send()
        for cp in copies:
            cp.wait()

    return pl.pallas_call(
        body, name=name, in_specs=[ANY] * n, out_specs=[ANY] * n,
        out_shape=[jax.ShapeDtypeStruct((NDEV,) + s.shape, s.dtype) for s in shards],
        scratch_shapes=[pltpu.SemaphoreType.DMA((n * NDEV,)), pltpu.SemaphoreType.DMA((n * NDEV,)), pltpu.SemaphoreType.DMA((n,))],
    )(*shards)


def all_to_all(groups, *, name):
    flat = [(gi, li, a) for gi, g in enumerate(groups) for li, a in enumerate(g)]
    n = len(flat)
    ng = len(groups)

    def body(*refs):
        src, dst = refs[:n], refs[n:n + ng]
        send_sems, recv_sems, local_sems = refs[n + ng:]
        me = _my_id()
        copies = []
        for a, (gi, li, _) in enumerate(flat):
            loc = pltpu.make_async_copy(src[a].at[me], dst[gi].at[me, li], local_sems.at[a])
            loc.start()
            copies.append(loc)
        sends = []
        for k in range(1, NDEV):
            to, pid = _peer(k)
            for a, (gi, li, _) in enumerate(flat):
                cp = pltpu.make_async_remote_copy(src_ref=src[a].at[pid], dst_ref=dst[gi].at[me, li],
                                                  send_sem=send_sems.at[a * NDEV + k], recv_sem=recv_sems.at[a * NDEV + k],
                                                  device_id=to, device_id_type=MESH)
                cp.start()
                sends.append(cp)
        for k in range(1, NDEV):
            to, pid = _peer(k)
            for a, (gi, li, _) in enumerate(flat):
                pltpu.make_async_remote_copy(src_ref=src[a].at[pid], dst_ref=dst[gi].at[pid, li],
                                             send_sem=send_sems.at[a * NDEV + k], recv_sem=recv_sems.at[a * NDEV + k],
                                             device_id=to, device_id_type=MESH).wait_recv()
        for cp in sends:
            cp.wait_send()
        for cp in copies:
            cp.wait()

    return pl.pallas_call(
        body, name=name, in_specs=[ANY] * n, out_specs=[ANY] * ng,
        out_shape=[jax.ShapeDtypeStruct((NDEV, len(g)) + g[0].shape[1:], g[0].dtype) for g in groups],
        scratch_shapes=[pltpu.SemaphoreType.DMA((n * NDEV,)), pltpu.SemaphoreType.DMA((n * NDEV,)), pltpu.SemaphoreType.DMA((n,))],
    )(*[a for _, _, a in flat])


def _pack(arrays):
    flat, layout, off = [], [], 0
    for a in arrays:
        flat.append(a.reshape(-1))
        layout.append((off, a.shape))
        off += a.size
    pad = (-off) % 1024
    if pad:
        flat.append(jnp.zeros((pad,), F32))
    return jnp.concatenate(flat).reshape(-1, 128), layout


def _unpack(packed, layout, lead=()):
    flat = packed.reshape(lead + (-1,))
    return [flat[..., off:off + _size(shape)].reshape(lead + tuple(shape)) for off, shape in layout]


def _size(shape):
    n = 1
    for s in shape:
        n *= s
    return n


def _join_last(g):
    g = jnp.moveaxis(g, 0, -2)
    return g.reshape(g.shape[:-2] + (g.shape[-2] * g.shape[-1],))


def _my_cols(a, width):
    return lax.dynamic_slice_in_dim(a, _my_id() * width, width, axis=a.ndim - 1)


def _tile(t, pref):
    return min(pref, t)


def kernel(x, c, w_mod, b_mod, norm_g, sc_w_in, sc_conv, sc_w_out, pool_w, pool_b, pool_scale, cf_w_pw1, cf_b_pw1, cf_w_dw, cf_b_dw, cf_ln_g, cf_ln_b, cf_w_pw2, cf_b_pw2, ffn_w_up, ffn_conv, ffn_b_conv, ffn_w_down, loss_target, m_w_mod, m_b_mod, m_norm_g, m_sc_w_in, m_sc_conv, m_sc_w_out, m_pool_w, m_pool_b, m_pool_scale, m_cf_w_pw1, m_cf_b_pw1, m_cf_w_dw, m_cf_b_dw, m_cf_ln_g, m_cf_ln_b, m_cf_w_pw2, m_cf_b_pw2, m_ffn_w_up, m_ffn_conv, m_ffn_b_conv, m_ffn_w_down, v_w_mod, v_b_mod, v_norm_g, v_sc_w_in, v_sc_conv, v_sc_w_out, v_pool_w, v_pool_b, v_pool_scale, v_cf_w_pw1, v_cf_b_pw1, v_cf_w_dw, v_cf_b_dw, v_cf_ln_g, v_cf_ln_b, v_cf_w_pw2, v_cf_b_pw2, v_ffn_w_up, v_ffn_conv, v_ffn_b_conv, v_ffn_w_down):
    env = dict(locals())
    names = ["w_mod", "b_mod", "norm_g", "sc_w_in", "sc_conv", "sc_w_out", "pool_w", "pool_b", "pool_scale", "cf_w_pw1",
             "cf_b_pw1", "cf_w_dw", "cf_b_dw", "cf_ln_g", "cf_ln_b", "cf_w_pw2", "cf_b_pw2", "ffn_w_up", "ffn_conv",
             "ffn_b_conv", "ffn_w_down"]
    t = x.shape[1]
    tm = _tile(t, 512)
    tm_ffn = _tile(t, 256)
    tt = _tile(t, 2048)
    x0, target = x[0], loss_target[0]

    small_names = ["norm_g", "sc_conv", "cf_b_pw1", "cf_w_dw", "cf_b_dw", "cf_ln_g", "cf_ln_b", "cf_b_pw2", "ffn_conv"]
    packed, layout = _pack([c] + [env[n] for n in small_names])
    (gathered,) = all_gather([packed], name="gather_small")
    parts = _unpack(gathered, layout, lead=(NDEV,))
    c_all = parts[0].reshape(NDEV, D)
    full = {n: _join_last(p) for n, p in zip(small_names, parts[1:])}

    shards = ([sc_w_in[j].astype(BF) for j in range(2)] + [sc_w_out[j].astype(BF) for j in range(2)]
              + [pool_w[0].astype(BF), cf_w_pw1[0].astype(BF), cf_w_pw2[0].astype(BF)]
              + [ffn_w_up[l].astype(BF) for l in range(DEPTH)] + [ffn_w_down[l].astype(BF) for l in range(DEPTH)])
    got = all_gather(shards, name="gather_weights")
    w_in_g = got[0:2]
    w_out_f = [g.reshape(D, D) for g in got[2:4]]
    pool_w_f = jnp.swapaxes(got[4], 0, 1).reshape(4, PG, PG)
    w_pw1_g = got[5]
    w_pw2_f = got[6].reshape(D, D)
    w_up_g = got[7:11]
    w_down_f = [g.reshape(F, D) for g in got[11:15]]

    mp = mod_partial(c_all, w_mod)
    (mod_parts,) = all_to_all([[jnp.swapaxes(mp, 0, 1)]], name="exchange_mod")
    mod = mod_finish(mod_parts[:, 0], b_mod)

    def vec(a):
        return a.reshape(1, -1)

    def ffn_blocks(a):
        return jnp.swapaxes(a.reshape(a.shape[0], 4, FB), 0, 1)

    saved = []
    xs = x0
    for l in range(DEPTH):
        sh1, sc1, g1, sh2, sc2, g2 = [mod[l:l + 1, k * D:(k + 1) * D] for k in range(6)]
        ng = [full["norm_g"][l, k:k + 1] for k in range(4)]
        kind, j = l % 3, l // 3
        s = dict(x_in=xs, sc1=sc1, g1=g1, sc2=sc2, g2=g2, ng=ng)
        if kind == 0:
            s["h"], s["p"] = fwd_in(xs, ng[0], sc1, sh1, w_in_g[j], None, blocked=False, tm=tm, name=f"sc_in_{l}")
            x1, s["m"], s["q"] = sc_fwd_out(s["p"], full["sc_conv"][j], w_out_f[j], xs, ng[1], g1, tm=tm, name=f"sc_out_{l}")
        elif kind == 1:
            x1, s["m"], s["ypre"], s["pooled"] = pool_fwd(xs, ng[0], sc1, sh1, pool_w_f, pool_b, pool_scale, ng[1], g1,
                                                          tm=tm, name=f"pool_{l}")
        else:
            s["h"], s["a"] = fwd_in(xs, ng[0], sc1, sh1, w_pw1_g, full["cf_b_pw1"].reshape(NDEV, 1, 2 * D // NDEV),
                                    blocked=False, tm=tm, name=f"cf_in_{l}")
            x1, s["m"], s["s"], s["u2"] = cf_fwd_out(s["a"], full["cf_w_dw"][0], full["cf_b_dw"], full["cf_ln_g"],
                                                     full["cf_ln_b"], w_pw2_f, full["cf_b_pw2"], xs, ng[1], g1,
                                                     tm=tm, name=f"cf_out_{l}")
        s["x1"] = x1
        s["cw"] = ffn_blocks(full["ffn_conv"][l])
        s["h2"], s["up"] = fwd_in(x1, ng[2], sc2, sh2, w_up_g[l], None, blocked=True, tm=tm, name=f"ffn_in_{l}")
        xs, s["f"], s["gc"], s["fa"] = ffn_fwd_out(s["up"], s["cw"], ffn_blocks(ffn_b_conv[l:l + 1]), w_down_f[l], x1,
                                                   ng[3], g2, tm=tm_ffn, name=f"ffn_out_{l}")
        saved.append(s)

    dx, loss_part = loss_head(xs, target, tm=tm, name="loss_head")
    loss = lax.psum(loss_part[0, 0], ("x", "y", "c"))

    gmod = [None] * DEPTH
    d_norm_g = [None] * DEPTH
    d_ffn_conv = [None] * DEPTH
    d_ffn_b_conv = [None] * DEPTH
    d_sc_conv = [None] * 2
    big = {}
    small_g = {}
    for l in reversed(range(DEPTH)):
        s = saved[l]
        ng = s["ng"]
        kind, j = l % 3, l // 3
        df, da, st_o = bwd_out(dx, s["f"], ng[3], s["g2"], w_down_f[l], blocked=True, tm=tm, name=f"ffn_bout_{l}")
        dup, st_c = ffn_bwd_mid(da, s["gc"], s["up"], s["cw"], tm=tm_ffn, name=f"ffn_bmid_{l}")
        dx1, st_i = bwd_in([dup], w_up_g[l], s["x1"], ng[2], s["sc2"], dx, tm=tm, name=f"ffn_bin_{l}")
        big[f"up{l}"] = wgrad(s["h2"], dup, nblk=NDEV, a_blocked=False, b_blocked=True, bk=D, bn=FB, tt=tt, name=f"ffn_wup_{l}")
        big[f"down{l}"] = wgrad(s["fa"], df, nblk=4, a_blocked=True, b_blocked=False, bk=FB, bn=D, tt=tt,
                                name=f"ffn_wdown_{l}").reshape(NDEV, F // NDEV, D)
        d_ffn_b_conv[l] = st_c[:, 0, :].reshape(F)
        d_ffn_conv[l] = jnp.swapaxes(st_c[:, 1:4, :], 0, 1).reshape(3, F)
        g_ffn = [st_i[0], st_i[1], st_o[0]]
        dn3, dn2 = st_o[1], st_i[2]
        if kind == 0:
            dm, dq, st_o = bwd_out(dx1, s["m"], ng[1], s["g1"], w_out_f[j], blocked=False, tm=tm, name=f"sc_bout_{l}")
            dp, st_c = sc_bwd_mid(dq, s["p"], full["sc_conv"][j], tm=tm, name=f"sc_bmid_{l}")
            dx, st_i = bwd_in([dp], w_in_g[j], s["x_in"], ng[0], s["sc1"], dx1, tm=tm, name=f"sc_bin_{l}")
            big[f"in{j}"] = wgrad(s["h"], dp, nblk=NDEV, a_blocked=False, b_blocked=True, bk=D, bn=3 * D // NDEV, tt=tt,
                                  name=f"sc_win_{l}")
            big[f"out{j}"] = wgrad(s["q"], dm, nblk=1, a_blocked=False, b_blocked=False, bk=D, bn=D, tt=tt,
                                   name=f"sc_wout_{l}").reshape(NDEV, D // NDEV, D)
            d_sc_conv[j] = st_c[0:3]
        elif kind == 1:
            dh, dyp, st_o = pool_bwd(dx1, s["m"], s["ypre"], pool_w_f, pool_scale, ng[1], s["g1"], tm=tm, name=f"pool_b_{l}")
            dx, st_i = bwd_in([dh], None, s["x_in"], ng[0], s["sc1"], dx1, tm=tm, name=f"pool_bin_{l}")
            dpw = wgrad(s["pooled"], dyp, nblk=4, a_blocked=True, b_blocked=True, bk=PG, bn=PG, tt=tt, name=f"pool_w_{l}")
            big["pool"] = jnp.swapaxes(dpw.reshape(4, NDEV, PG // NDEV, PG), 0, 1).reshape(NDEV, 4 * PG // NDEV, PG)
            small_g["pool_scale"], small_g["pool_b"] = st_o[2:3], st_o[3:4]
        else:
            dm, ds, st_o = bwd_out(dx1, s["m"], ng[1], s["g1"], w_pw2_f, blocked=False, tm=tm, name=f"cf_bout_{l}")
            dA, st_c = cf_bwd_mid(ds, s["u2"], s["a"], full["cf_w_dw"][0], full["cf_ln_g"], full["cf_ln_b"], tm=tm,
                                  name=f"cf_bmid_{l}")
            dx, st_i = bwd_in([dA], w_pw1_g, s["x_in"], ng[0], s["sc1"], dx1, tm=tm, name=f"cf_bin_{l}")
            big["pw1"] = wgrad(s["h"], dA, nblk=NDEV, a_blocked=False, b_blocked=True, bk=D, bn=2 * D // NDEV, tt=tt,
                               name=f"cf_wpw1_{l}")
            big["pw2"] = wgrad(s["s"], dm, nblk=1, a_blocked=False, b_blocked=False, bk=D, bn=D, tt=tt,
                               name=f"cf_wpw2_{l}").reshape(NDEV, D // NDEV, D)
            small_g["cf_w_dw"] = st_c[0:CFW][None]
            small_g["cf_b_dw"], small_g["cf_ln_g"], small_g["cf_ln_b"] = st_c[31:32], st_c[32:33], st_c[33:34]
            small_g["cf_b_pw1"] = st_c[34:36].reshape(1, 2 * D)
            small_g["cf_b_pw2"] = st_o[2:3]
        gmod[l] = jnp.concatenate([st_i[0], st_i[1], st_o[0]] + g_ffn)
        d_norm_g[l] = jnp.stack([st_i[2], st_o[1], dn2, dn3])

    small_g["gmod"] = jnp.stack(gmod)
    small_g["norm_g"] = jnp.stack(d_norm_g)
    small_g["sc_conv"] = jnp.stack(d_sc_conv)
    small_g["ffn_conv"] = jnp.stack(d_ffn_conv)
    small_g["ffn_b_conv"] = jnp.stack(d_ffn_b_conv)
    sg_names = ["gmod", "norm_g", "sc_conv", "pool_b", "pool_scale", "cf_b_pw1", "cf_w_dw", "cf_b_dw", "cf_ln_g", "cf_ln_b",
                "cf_b_pw2", "ffn_conv", "ffn_b_conv"]
    gpacked, glayout = _pack([small_g[n] for n in sg_names])
    (ggath,) = all_gather([gpacked], name="gather_small_grads")
    gsum = dict(zip(sg_names, _unpack(sum_parts(ggath), glayout)))
    gmod_all = _unpack(ggath, glayout[:1], lead=(NDEV,))[0]
    grads = {"b_mod": gsum["gmod"], "pool_b": gsum["pool_b"], "pool_scale": gsum["pool_scale"],
             "ffn_b_conv": gsum["ffn_b_conv"]}
    for n in ["norm_g", "sc_conv", "cf_b_pw1", "cf_w_dw", "cf_b_dw", "cf_ln_g", "cf_ln_b", "cf_b_pw2", "ffn_conv"]:
        grads[n] = _my_cols(gsum[n], env[n].shape[-1])
    grads["w_mod"] = mod_wgrad(c_all.T, jnp.swapaxes(_my_cols(gmod_all, w_mod.shape[2]), 0, 1))

    deltas, new_m, new_v = {}, {}, {}
    sp_names = ["b_mod", "norm_g", "sc_conv", "pool_b", "pool_scale", "cf_b_pw1", "cf_w_dw", "cf_b_dw", "cf_ln_g", "cf_ln_b",
                "cf_b_pw2", "ffn_conv", "ffn_b_conv"]
    pg, playout = _pack([grads[n] for n in sp_names])
    pw_, _ = _pack([env[n] for n in sp_names])
    pm_, _ = _pack([env["m_" + n] for n in sp_names])
    pv_, _ = _pack([env["v_" + n] for n in sp_names])
    _, sd, sm, sv = adamw_sum(pg[None], pw_, pm_, pv_, name="adamw_small")
    for n, d_, m_, v_ in zip(sp_names, _unpack(sd, playout), _unpack(sm, playout), _unpack(sv, playout)):
        deltas[n], new_m[n], new_v[n] = d_, m_, v_
    gw = grads["w_mod"].reshape(1, DEPTH * D, -1)
    _, d_, m_, v_ = adamw_sum(gw, w_mod.reshape(gw.shape[1:]), m_w_mod.reshape(gw.shape[1:]), v_w_mod.reshape(gw.shape[1:]),
                              name="adamw_w_mod")
    deltas["w_mod"], new_m["w_mod"], new_v["w_mod"] = [a.reshape(w_mod.shape) for a in (d_, m_, v_)]

    groups = {"sc_w_in": [big["in0"], big["in1"]], "sc_w_out": [big["out0"], big["out1"]], "pool_w": [big["pool"]],
              "cf_w_pw1": [big["pw1"]], "cf_w_pw2": [big["pw2"]], "ffn_w_up": [big[f"up{l}"] for l in range(DEPTH)],
              "ffn_w_down": [big[f"down{l}"] for l in range(DEPTH)]}
    gnames = list(groups)
    recv = all_to_all([groups[n] for n in gnames], name="scatter_grads")
    for n, r in zip(gnames, recv):
        rows, cols = r.shape[1] * r.shape[2], r.shape[3]
        w2 = [env[p + n].reshape(rows, cols) for p in ("", "m_", "v_")]
        outs = adamw_sum(r.reshape(NDEV, rows, cols), *w2, name=f"adamw_{n}")
        grads[n], deltas[n], new_m[n], new_v[n] = [a.reshape(env[n].shape) for a in outs]

    return (loss, dx[None], *[grads[n] for n in names], *[deltas[n] for n in names], *[new_m[n] for n in names],
            *[new_v[n] for n in names])
```

```python
import functools

import jax
import jax.numpy as jnp
from jax import lax
from jax.experimental import pallas as pl
from jax.experimental.pallas import tpu as pltpu

D = 1024
F = 2816
NDEV = 8
FB = F // 4
DEPTH = 4
RMS_EPS = 1e-6
LN_EPS = 1e-5
CFW = 31
POOL_WINDOWS = (2, 4, 8, 16)
PG = D // 4
LR, B1, B2, ADAM_EPS, WD, STEP = 0.001, 0.9, 0.999, 1e-08, 0.01, 10

BF = jnp.bfloat16
F32 = jnp.float32
VMEM_LIMIT_V7X = 56 * 1024 * 1024
MESH = pl.DeviceIdType.MESH
ANY = pl.BlockSpec(memory_space=pl.ANY)


def _params(n_axes):
    return pltpu.CompilerParams(dimension_semantics=("arbitrary",) * n_axes, vmem_limit_bytes=VMEM_LIMIT_V7X)


def _const(shape, single=True):
    nd = len(shape)
    if single:
        return pl.BlockSpec(shape, lambda *_: (0,) * nd, pipeline_mode=pl.Buffered(1))
    return pl.BlockSpec(shape, lambda *_: (0,) * nd)


def _rows(tm, c):
    return pl.BlockSpec((tm, c), lambda i: (i, 0))


def _brows(nb, tm, c, b0=0):
    return pl.BlockSpec((nb, tm, c), lambda i: (b0, i, 0))


def _prev(hb, c, tm):
    return pl.BlockSpec((hb, c), lambda i: (jnp.maximum(i * (tm // hb) - 1, 0), 0))


def _next(hb, c, tm, t):
    return pl.BlockSpec((hb, c), lambda i: (jnp.minimum((i + 1) * (tm // hb), t // hb - 1), 0))


def _bprev(nb, hb, c, tm, b0=0):
    return pl.BlockSpec((nb, hb, c), lambda i: (b0, jnp.maximum(i * (tm // hb) - 1, 0), 0))


def _bnext(nb, hb, c, tm, t, b0=0):
    return pl.BlockSpec((nb, hb, c), lambda i: (b0, jnp.minimum((i + 1) * (tm // hb), t // hb - 1), 0))


def _sigmoid(v):
    return 0.5 * jnp.tanh(0.5 * v) + 0.5


def _fold8(v):
    r, c = v.shape
    return jnp.sum(v.reshape(r // 8, 8, c), axis=0)


def _chunks(n_rows, rc, step, init=0, reverse=False):
    n = n_rows // rc

    def it(c, carry):
        idx = (n - 1 - c) if reverse else c
        return step(pl.multiple_of(idx * rc, rc), carry)

    return lax.fori_loop(0, n, it, init)


def _row_shifted_copies(s_ref, n):
    for b in range(1, 8):
        s_ref[b, 0:n, :] = s_ref[0, pl.ds(b, n), :]


def _shifted(s_ref, o, tm):
    return s_ref[o % 8, pl.ds(8 * (o // 8), tm), :]


def _dot(a, b):
    return jnp.dot(a, b, preferred_element_type=F32)


def _dot_nt(a, b):
    return lax.dot_general(a, b, (((1,), (1,)), ((), ())), preferred_element_type=F32)


def _dot_tn(a, b):
    return lax.dot_general(a, b, (((0,), (0,)), ((), ())), preferred_element_type=F32)


def _rsum(v):
    return jnp.sum(v, axis=0, keepdims=True)


def _adaln(x, g, sc, sh):
    r = lax.rsqrt(jnp.mean(x * x, axis=-1, keepdims=True) + RMS_EPS)
    return (x * r * g) * (1.0 + sc) + sh


def _gated_res(x, m, gn, gt):
    r = lax.rsqrt(jnp.mean(m * m, axis=-1, keepdims=True) + RMS_EPS)
    return x + gt * (m * r * gn)


def _gated_res_bwd(dxo, m, gn, gt):
    r = lax.rsqrt(jnp.mean(m * m, axis=-1, keepdims=True) + RMS_EPS)
    mh = m * r
    dgt = _rsum(dxo * (mh * gn))
    dn = dxo * gt
    dgn = _rsum(dn * mh)
    dmh = dn * gn
    dm = r * (dmh - mh * jnp.mean(dmh * mh, axis=-1, keepdims=True))
    return dm, dgt, dgn


def fwd_in(x, g, sc, sh, w, bias, *, blocked, tm, name):
    t = x.shape[0]
    nb, _, bw = w.shape

    def body(*refs):
        if bias is None:
            x_ref, g_ref, sc_ref, sh_ref, w_ref, h_ref, p_ref = refs
        else:
            x_ref, g_ref, sc_ref, sh_ref, w_ref, b_ref, h_ref, p_ref = refs
        hb = _adaln(x_ref[...], g_ref[...], sc_ref[...], sh_ref[...]).astype(BF)
        h_ref[...] = hb
        for d in range(nb):
            y = _dot(hb, w_ref[d])
            if bias is not None:
                y = y + b_ref[d]
            if blocked:
                p_ref[d] = y.astype(BF)
            else:
                p_ref[:, d * bw:(d + 1) * bw] = y.astype(BF)

    vec = _const((1, D))
    in_specs = [_rows(tm, D), vec, vec, vec, _const((nb, D, bw))]
    args = [x, g, sc, sh, w]
    if bias is not None:
        in_specs.append(_const((nb, 1, bw)))
        args.append(bias)
    if blocked:
        p_spec, p_shape = _brows(nb, tm, bw), jax.ShapeDtypeStruct((nb, t, bw), BF)
    else:
        p_spec, p_shape = _rows(tm, nb * bw), jax.ShapeDtypeStruct((t, nb * bw), BF)
    return pl.pallas_call(
        body, name=name, grid=(t // tm,), in_specs=in_specs, out_specs=[_rows(tm, D), p_spec],
        out_shape=[jax.ShapeDtypeStruct((t, D), BF), p_shape], compiler_params=_params(1))(*args)


def _conv3_from(s_ref, w, tm, lo):
    acc = w[0:1, :] * s_ref[pl.ds(lo, tm), :]
    for k in (1, 2):
        acc = acc + w[k:k + 1, :] * s_ref[pl.ds(lo + k, tm), :]
    return acc


def sc_fwd_out(p, convw, w_out, x, gn, gt, *, tm, name):
    t = x.shape[0]

    def body(p_ref, ph_ref, cw_ref, w_ref, x_ref, gn_ref, gt_ref, x1_ref, m_ref, q_ref, s_ref):
        i = pl.program_id(0)
        zh = ph_ref[8:16, D:2 * D].astype(F32) * ph_ref[8:16, 2 * D:3 * D].astype(F32)
        s_ref[0:8, :] = jnp.where(i == 0, 0.0, zh)
        s_ref[8:8 + tm, :] = p_ref[:, D:2 * D].astype(F32) * p_ref[:, 2 * D:3 * D].astype(F32)
        u = _conv3_from(s_ref, cw_ref[...], tm, 6)
        qb = (p_ref[:, 0:D].astype(F32) * u).astype(BF)
        q_ref[...] = qb
        m = _dot(qb, w_ref[...])
        m_ref[...] = m.astype(BF)
        x1_ref[...] = _gated_res(x_ref[...], m, gn_ref[...], gt_ref[...])

    vec = _const((1, D))
    return pl.pallas_call(
        body, name=name, grid=(t // tm,),
        in_specs=[_rows(tm, 3 * D), _prev(16, 3 * D, tm), _const((3, D)), _const((D, D)), _rows(tm, D), vec, vec],
        out_specs=[_rows(tm, D)] * 3,
        out_shape=[jax.ShapeDtypeStruct((t, D), F32), jax.ShapeDtypeStruct((t, D), BF), jax.ShapeDtypeStruct((t, D), BF)],
        scratch_shapes=[pltpu.VMEM((tm + 8, D), F32)], compiler_params=_params(1))(p, p, convw, w_out, x, gn, gt)


def _layernorm_parts(u2):
    mu = jnp.mean(u2, axis=-1, keepdims=True)
    cen = u2 - mu
    rstd = lax.rsqrt(jnp.mean(cen * cen, axis=-1, keepdims=True) + LN_EPS)
    return cen * rstd, rstd


def cf_fwd_out(a, w_dw, b_dw, ln_g, ln_b, w_pw2, b_pw2, x, gn, gt, *, tm, name):
    t = x.shape[0]
    hb = 32

    def body(a_ref, ah_ref, wd_ref, bd_ref, lg_ref, lb_ref, w_ref, b2_ref, x_ref, gn_ref, gt_ref,
             x1_ref, m_ref, s_out_ref, u2_ref, s_ref):
        i = pl.program_id(0)
        uh = ah_ref[:, 0:D].astype(F32) * _sigmoid(ah_ref[:, D:2 * D].astype(F32))
        s_ref[0, 0:hb, :] = jnp.where(i == 0, 0.0, uh)
        s_ref[0, hb:hb + tm, :] = a_ref[:, 0:D].astype(F32) * _sigmoid(a_ref[:, D:2 * D].astype(F32))
        _row_shifted_copies(s_ref, tm + hb - 8)
        acc = bd_ref[...] + wd_ref[0:1, :] * _shifted(s_ref, hb - CFW + 1, tm)
        for k in range(1, CFW):
            acc = acc + wd_ref[k:k + 1, :] * _shifted(s_ref, hb - CFW + 1 + k, tm)
        u2_ref[...] = acc.astype(BF)
        xh, _ = _layernorm_parts(acc)
        l = xh * lg_ref[...] + lb_ref[...]
        sb = (l * _sigmoid(l)).astype(BF)
        s_out_ref[...] = sb
        m = _dot(sb, w_ref[...]) + b2_ref[...]
        m_ref[...] = m.astype(BF)
        x1_ref[...] = _gated_res(x_ref[...], m, gn_ref[...], gt_ref[...])

    vec = _const((1, D))
    return pl.pallas_call(
        body, name=name, grid=(t // tm,),
        in_specs=[_rows(tm, 2 * D), _prev(hb, 2 * D, tm), _const((CFW, D)), vec, vec, vec, _const((D, D)), vec,
                  _rows(tm, D), vec, vec],
        out_specs=[_rows(tm, D)] * 4,
        out_shape=[jax.ShapeDtypeStruct((t, D), F32)] + [jax.ShapeDtypeStruct((t, D), BF)] * 3,
        scratch_shapes=[pltpu.VMEM((8, tm + hb, D), F32)], compiler_params=_params(1),
    )(a, a, w_dw, b_dw, ln_g, ln_b, w_pw2, b_pw2, x, gn, gt)


def _pool_counts(i, tm, w):
    row = lax.broadcasted_iota(jnp.int32, (tm, 1), 0) + i * tm
    return jnp.minimum(row + 1, w).astype(F32)


def pool_fwd(x, g, sc, sh, pw, pb, pscale, gn, gt, *, tm, name):
    t = x.shape[0]
    pad, hb = 8, 16
    base = pad + hb

    def body(x_ref, xh_ref, g_ref, sc_ref, sh_ref, pw_ref, pb_ref, ps_ref, gn_ref, gt_ref,
             x1_ref, m_ref, yp_ref, po_ref, sa_ref, sb_ref):
        i = pl.program_id(0)
        hh = _adaln(xh_ref[...], g_ref[...], sc_ref[...], sh_ref[...])
        h = _adaln(x_ref[...], g_ref[...], sc_ref[...], sh_ref[...])
        zero = jnp.zeros((pad, D), F32)
        sa_ref[0:pad, :] = zero
        sb_ref[0:pad, :] = zero
        sa_ref[pad:base, :] = jnp.where(i == 0, 0.0, hh)
        sa_ref[base:base + tm, :] = h
        n = hb + tm
        src, dst = sa_ref, sb_ref
        ys = []
        for gi, w in enumerate(POOL_WINDOWS):
            c0 = gi * PG
            step = w // 2
            dst[pl.ds(pad, n), c0:D] = src[pl.ds(pad, n), c0:D] + src[pl.ds(pad - step, n), c0:D]
            mean = dst[pl.ds(base, tm), c0:c0 + PG] / _pool_counts(i, tm, w)
            pooled = (mean - h[:, c0:c0 + PG]).astype(BF)
            po_ref[:, c0:c0 + PG] = pooled
            ys.append(_dot(pooled, pw_ref[gi]))
            src, dst = dst, src
        ypre = jnp.concatenate(ys, axis=1) + pb_ref[...]
        yp_ref[...] = ypre.astype(BF)
        m = ypre * ps_ref[...]
        m_ref[...] = m.astype(BF)
        x1_ref[...] = _gated_res(x_ref[...], m, gn_ref[...], gt_ref[...])

    vec = _const((1, D))
    return pl.pallas_call(
        body, name=name, grid=(t // tm,),
        in_specs=[_rows(tm, D), _prev(hb, D, tm), vec, vec, vec, _const((4, PG, PG)), vec, vec, vec, vec],
        out_specs=[_rows(tm, D)] * 4,
        out_shape=[jax.ShapeDtypeStruct((t, D), F32)] + [jax.ShapeDtypeStruct((t, D), BF)] * 3,
        scratch_shapes=[pltpu.VMEM((tm + base, D), F32)] * 2, compiler_params=_params(1),
    )(x, x, g, sc, sh, pw, pb, pscale, gn, gt)


def ffn_fwd_out(up, convw, convb, w_down, x, gn, gt, *, tm, name):
    t = x.shape[0]

    def body(gate_ref, gh_ref, val_ref, cw_ref, cb_ref, w_ref, x_ref, gn_ref, gt_ref,
             x2_ref, f_ref, gc_ref, a_ref, s_ref):
        i = pl.program_id(0)
        acc = jnp.zeros((tm, D), F32)
        for j in range(4):
            sj = s_ref.at[j]
            sj[0:8, :] = jnp.where(i == 0, 0.0, gh_ref[j, 8:16, :].astype(F32))
            sj[8:8 + tm, :] = gate_ref[j].astype(F32)
            gc = cb_ref[j] + _conv3_from(sj, cw_ref[j], tm, 6)
            gc_ref[j] = gc.astype(BF)
            ab = (gc * _sigmoid(gc) * val_ref[j].astype(F32)).astype(BF)
            a_ref[j] = ab
            acc = acc + _dot(ab, w_ref[j * FB:(j + 1) * FB, :])
        f_ref[...] = acc.astype(BF)
        x2_ref[...] = _gated_res(x_ref[...], acc, gn_ref[...], gt_ref[...])

    vec = _const((1, D))
    blk = jax.ShapeDtypeStruct((4, t, FB), BF)
    return pl.pallas_call(
        body, name=name, grid=(t // tm,),
        in_specs=[_brows(4, tm, FB, 0), _bprev(4, 16, FB, tm, 0), _brows(4, tm, FB, 1), _const((4, 3, FB)),
                  _const((4, 1, FB)), _const((F, D)), _rows(tm, D), vec, vec],
        out_specs=[_rows(tm, D), _rows(tm, D), _brows(4, tm, FB), _brows(4, tm, FB)],
        out_shape=[jax.ShapeDtypeStruct((t, D), F32), jax.ShapeDtypeStruct((t, D), BF), blk, blk],
        scratch_shapes=[pltpu.VMEM((4, tm + 8, FB), F32)], compiler_params=_params(1),
    )(up, up, up, convw, convb, w_down, x, gn, gt)


def loss_head(y, target, *, tm, name):
    t = y.shape[0]

    def body(y_ref, t_ref, dy_ref, l_ref, acc_ref):
        i = pl.program_id(0)

        @pl.when(i == 0)
        def _():
            acc_ref[...] = jnp.zeros_like(acc_ref)

        e = y_ref[...] - t_ref[...]
        dy_ref[...] = e * (1.0 / D)
        acc_ref[...] += _rsum(e * e)

        @pl.when(i == pl.num_programs(0) - 1)
        def _():
            l_ref[...] = jnp.sum(acc_ref[...], axis=1, keepdims=True) * (0.5 / D)

    return pl.pallas_call(
        body, name=name, grid=(t // tm,), in_specs=[_rows(tm, D), _rows(tm, D)],
        out_specs=[_rows(tm, D), pl.BlockSpec((1, 1), lambda i: (0, 0))],
        out_shape=[jax.ShapeDtypeStruct((t, D), F32), jax.ShapeDtypeStruct((1, 1), F32)],
        scratch_shapes=[pltpu.VMEM((1, D), F32)], compiler_params=_params(1))(y, target)


def _init_stats(ref):
    @pl.when(pl.program_id(0) == 0)
    def _():
        ref[...] = jnp.zeros_like(ref)


def bwd_out(dxo, m, gn, gt, w, *, blocked, tm, name):
    t = dxo.shape[0]
    k = w.shape[0]

    def body(dx_ref, m_ref, gn_ref, gt_ref, w_ref, dm_ref, da_ref, st_ref):
        _init_stats(st_ref)
        dm, dgt, dgn = _gated_res_bwd(dx_ref[...], m_ref[...].astype(F32), gn_ref[...], gt_ref[...])
        st_ref[0:1, :] += dgt
        st_ref[1:2, :] += dgn
        st_ref[2:3, :] += _rsum(dm)
        dmb = dm.astype(BF)
        dm_ref[...] = dmb
        if blocked:
            for j in range(4):
                da_ref[j] = _dot_nt(dmb, w_ref[j * FB:(j + 1) * FB, :]).astype(BF)
        else:
            da_ref[...] = _dot_nt(dmb, w_ref[...]).astype(BF)

    vec = _const((1, D))
    if blocked:
        da_spec, da_shape = _brows(4, tm, FB), jax.ShapeDtypeStruct((4, t, FB), BF)
    else:
        da_spec, da_shape = _rows(tm, k), jax.ShapeDtypeStruct((t, k), BF)
    return pl.pallas_call(
        body, name=name, grid=(t // tm,), in_specs=[_rows(tm, D), _rows(tm, D), vec, vec, _const((k, D))],
        out_specs=[_rows(tm, D), da_spec, _const((8, D), single=False)],
        out_shape=[jax.ShapeDtypeStruct((t, D), BF), da_shape, jax.ShapeDtypeStruct((8, D), F32)],
        compiler_params=_params(1))(dxo, m, gn, gt, w)


def bwd_in(dps, w, x, g, sc, dxo, *, tm, name):
    t = x.shape[0]
    direct = w is None
    if not direct:
        nb, _, bw = w.shape
    natural = (not direct) and dps[0].ndim == 2

    def body(*refs):
        n = len(dps)
        dp_refs = refs[:n]
        if direct:
            x_ref, g_ref, sc_ref, dxo_ref, dx_ref, st_ref = refs[n:]
            dh = dp_refs[0][...]
        else:
            w_ref, x_ref, g_ref, sc_ref, dxo_ref, dx_ref, st_ref = refs[n:]
            dh = jnp.zeros((tm, D), F32)
            if natural:
                for d in range(nb):
                    dh = dh + _dot_nt(dp_refs[0][:, d * bw:(d + 1) * bw], w_ref[d])
            else:
                d = 0
                for r in dp_refs:
                    for j in range(r.shape[0]):
                        dh = dh + _dot_nt(r[j], w_ref[d])
                        d += 1
        _init_stats(st_ref)
        x = x_ref[...]
        r = lax.rsqrt(jnp.mean(x * x, axis=-1, keepdims=True) + RMS_EPS)
        xh = x * r
        gv = g_ref[...]
        st_ref[0:1, :] += _rsum(dh)
        st_ref[1:2, :] += _rsum(dh * (xh * gv))
        dn = dh * (1.0 + sc_ref[...])
        st_ref[2:3, :] += _rsum(dn * xh)
        dy = dn * gv
        dx_ref[...] = dxo_ref[...] + r * (dy - xh * jnp.mean(dy * xh, axis=-1, keepdims=True))

    vec = _const((1, D))
    if direct:
        dp_specs = [_rows(tm, D)]
    elif natural:
        dp_specs = [_rows(tm, nb * bw)]
    else:
        dp_specs = [_brows(a.shape[0], tm, bw) for a in dps]
    w_specs, w_args = ([], []) if direct else ([_const((nb, D, bw))], [w])
    return pl.pallas_call(
        body, name=name, grid=(t // tm,),
        in_specs=dp_specs + w_specs + [_rows(tm, D), vec, vec, _rows(tm, D)],
        out_specs=[_rows(tm, D), _const((8, D), single=False)],
        out_shape=[jax.ShapeDtypeStruct((t, D), F32), jax.ShapeDtypeStruct((8, D), F32)],
        compiler_params=_params(1))(*dps, *w_args, x, g, sc, dxo)


def sc_bwd_mid(dq, p, convw, *, tm, name):
    t = dq.shape[0]

    def body(dq_ref, dqn_ref, p_ref, pp_ref, pn_ref, cw_ref, dp_ref, st_ref, s1_ref, s2_ref):
        i = pl.program_id(0)
        last = i == pl.num_programs(0) - 1
        _init_stats(st_ref)
        cw = cw_ref[...]
        cg = p_ref[:, D:2 * D].astype(F32)
        hi = p_ref[:, 2 * D:3 * D].astype(F32)
        bg = p_ref[:, 0:D].astype(F32)
        zp = pp_ref[8:16, D:2 * D].astype(F32) * pp_ref[8:16, 2 * D:3 * D].astype(F32)
        s2_ref[0:8, :] = jnp.where(i == 0, 0.0, zp)
        s2_ref[8:8 + tm, :] = cg * hi
        u = _conv3_from(s2_ref, cw, tm, 6)
        dqf = dq_ref[...].astype(F32)
        dp_ref[:, 0:D] = (dqf * u).astype(BF)
        du = dqf * bg
        duh = dqn_ref[0:8, :].astype(F32) * pn_ref[0:8, 0:D].astype(F32)
        s1_ref[0:tm, :] = du
        s1_ref[tm:tm + 8, :] = jnp.where(last, 0.0, duh)
        dz = cw[2:3, :] * du + cw[1:2, :] * s1_ref[pl.ds(1, tm), :] + cw[0:1, :] * s1_ref[pl.ds(2, tm), :]
        dp_ref[:, D:2 * D] = (dz * hi).astype(BF)
        dp_ref[:, 2 * D:3 * D] = (dz * cg).astype(BF)
        for k in range(3):
            st_ref[k:k + 1, :] += _rsum(du * s2_ref[pl.ds(6 + k, tm), :])

    return pl.pallas_call(
        body, name=name, grid=(t // tm,),
        in_specs=[_rows(tm, D), _next(16, D, tm, t), _rows(tm, 3 * D), _prev(16, 3 * D, tm), _next(16, 3 * D, tm, t),
                  _const((3, D))],
        out_specs=[_rows(tm, 3 * D), _const((8, D), single=False)],
        out_shape=[jax.ShapeDtypeStruct((t, 3 * D), BF), jax.ShapeDtypeStruct((8, D), F32)],
        scratch_shapes=[pltpu.VMEM((tm + 8, D), F32)] * 2, compiler_params=_params(1))(dq, dq, p, p, p, convw)


def cf_bwd_mid(ds, u2, a, w_dw, ln_g, ln_b, *, tm, name):
    t = ds.shape[0]
    hb = 32

    def du2_of(dsv, u2v, lg, lb):
        xh, rstd = _layernorm_parts(u2v)
        l = xh * lg + lb
        sg = _sigmoid(l)
        dl = dsv * (sg * (1.0 + l * (1.0 - sg)))
        dxh = dl * lg
        du2 = rstd * (dxh - jnp.mean(dxh, axis=-1, keepdims=True) - xh * jnp.mean(dxh * xh, axis=-1, keepdims=True))
        return du2, dl, xh

    def body(ds_ref, dsn_ref, u2_ref, u2n_ref, a_ref, wd_ref, lg_ref, lb_ref, da_ref, st_ref, s1_ref):
        i = pl.program_id(0)
        last = i == pl.num_programs(0) - 1
        _init_stats(st_ref)
        lg, lb = lg_ref[...], lb_ref[...]
        du2, dl, xh = du2_of(ds_ref[...].astype(F32), u2_ref[...].astype(F32), lg, lb)
        st_ref[32:33, :] += _rsum(dl * xh)
        st_ref[33:34, :] += _rsum(dl)
        st_ref[31:32, :] += _rsum(du2)
        du2n, _, _ = du2_of(dsn_ref[...].astype(F32), u2n_ref[...].astype(F32), lg, lb)
        s1_ref[0, 0:tm, :] = du2
        s1_ref[0, tm:tm + hb, :] = jnp.where(last, 0.0, du2n)
        _row_shifted_copies(s1_ref, tm + hb - 8)
        av = a_ref[:, 0:D].astype(F32)
        sg = _sigmoid(a_ref[:, D:2 * D].astype(F32))
        u = av * sg
        du = jnp.zeros((tm, D), F32)
        for k in range(CFW):
            sh = _shifted(s1_ref, CFW - 1 - k, tm)
            du = du + wd_ref[k:k + 1, :] * sh
            st_ref[k:k + 1, :] += _rsum(u * sh)
        dav = du * sg
        dgv = du * u * (1.0 - sg)
        st_ref[34:35, :] += _rsum(dav)
        st_ref[35:36, :] += _rsum(dgv)
        da_ref[:, 0:D] = dav.astype(BF)
        da_ref[:, D:2 * D] = dgv.astype(BF)

    vec = _const((1, D))
    return pl.pallas_call(
        body, name=name, grid=(t // tm,),
        in_specs=[_rows(tm, D), _next(hb, D, tm, t), _rows(tm, D), _next(hb, D, tm, t), _rows(tm, 2 * D),
                  _const((CFW, D)), vec, vec],
        out_specs=[_rows(tm, 2 * D), _const((40, D), single=False)],
        out_shape=[jax.ShapeDtypeStruct((t, 2 * D), BF), jax.ShapeDtypeStruct((40, D), F32)],
        scratch_shapes=[pltpu.VMEM((8, tm + hb, D), F32)], compiler_params=_params(1),
    )(ds, ds, u2, u2, a, w_dw, ln_g, ln_b)


def pool_bwd(dxo, m, ypre, pw, pscale, gn, gt, *, tm, name):
    t = dxo.shape[0]
    hb = 16

    def dyp_of(dxv, mv, ypv, ps, gnv, gtv):
        dm, dgt, dgn = _gated_res_bwd(dxv, mv, gnv, gtv)
        return dm * ps, dgt, dgn, _rsum(dm * ypv)

    def body(dx_ref, dxn_ref, m_ref, mn_ref, yp_ref, ypn_ref, pw_ref, ps_ref, gn_ref, gt_ref,
             dh_ref, dyp_ref, st_ref, sa_ref, sb_ref):
        i = pl.program_id(0)
        last = i == pl.num_programs(0) - 1
        _init_stats(st_ref)
        ps, gnv, gtv = ps_ref[...], gn_ref[...], gt_ref[...]
        dyp, dgt, dgn, dps = dyp_of(dx_ref[...], m_ref[...].astype(F32), yp_ref[...].astype(F32), ps, gnv, gtv)
        st_ref[0:1, :] += dgt
        st_ref[1:2, :] += dgn
        st_ref[2:3, :] += dps
        st_ref[3:4, :] += _rsum(dyp)
        dypb = dyp.astype(BF)
        dyp_ref[...] = dypb
        dypn, _, _, _ = dyp_of(dxn_ref[...], mn_ref[...].astype(F32), ypn_ref[...].astype(F32), ps, gnv, gtv)
        dypnb = jnp.where(last, 0.0, dypn).astype(BF)
        dpo = []
        for gi, w in enumerate(POOL_WINDOWS):
            c0 = gi * PG
            dp_main = _dot_nt(dypb[:, c0:c0 + PG], pw_ref[gi])
            dp_next = _dot_nt(dypnb[:, c0:c0 + PG], pw_ref[gi])
            dpo.append(dp_main)
            sa_ref[0:tm, c0:c0 + PG] = dp_main / _pool_counts(i, tm, w)
            sa_ref[tm:tm + hb, c0:c0 + PG] = dp_next / float(w)
        zero = jnp.zeros((8, D), F32)
        sa_ref[tm + hb:tm + hb + 8, :] = zero
        sb_ref[tm + hb:tm + hb + 8, :] = zero
        n = tm + hb
        src, dst = sa_ref, sb_ref
        for gi, w in enumerate(POOL_WINDOWS):
            c0 = gi * PG
            step = w // 2
            dst[pl.ds(0, n), c0:D] = src[pl.ds(0, n), c0:D] + src[pl.ds(step, n), c0:D]
            dh_ref[:, c0:c0 + PG] = dst[pl.ds(0, tm), c0:c0 + PG] - dpo[gi]
            src, dst = dst, src

    vec = _const((1, D))
    return pl.pallas_call(
        body, name=name, grid=(t // tm,),
        in_specs=[_rows(tm, D), _next(hb, D, tm, t), _rows(tm, D), _next(hb, D, tm, t), _rows(tm, D),
                  _next(hb, D, tm, t), _const((4, PG, PG)), vec, vec, vec],
        out_specs=[_rows(tm, D), _rows(tm, D), _const((8, D), single=False)],
        out_shape=[jax.ShapeDtypeStruct((t, D), F32), jax.ShapeDtypeStruct((t, D), BF), jax.ShapeDtypeStruct((8, D), F32)],
        scratch_shapes=[pltpu.VMEM((tm + hb + 8, D), F32)] * 2, compiler_params=_params(1),
    )(dxo, dxo, m, m, ypre, ypre, pw, pscale, gn, gt)


def ffn_bwd_mid(da, gc, up, convw, *, tm, name):
    t = da.shape[1]

    def dgc_of(dav, gcv, valv):
        sg = _sigmoid(gcv)
        return dav * valv * (sg * (1.0 + gcv * (1.0 - sg))), dav * (gcv * sg)

    rc = 16

    def body(da_ref, dan_ref, gc_ref, gcn_ref, val_ref, valn_ref, gate_ref, cw_ref, dup_ref, st_ref, acc_ref):
        i = pl.program_id(0)
        last = i == pl.num_programs(0) - 1
        _init_stats(acc_ref)
        for j in range(4):
            w0, w1, w2 = cw_ref[j, 0:1, :], cw_ref[j, 1:2, :], cw_ref[j, 2:3, :]
            dgcn, _ = dgc_of(dan_ref[j, 0:8, :].astype(F32), gcn_ref[j, 0:8, :].astype(F32), valn_ref[j, 0:8, :].astype(F32))
            below = jnp.where(last, 0.0, dgcn)

            def step(r0, carry):
                rows = pl.ds(r0, rc)
                dgc, dval = dgc_of(da_ref[j, rows, :].astype(F32), gc_ref[j, rows, :].astype(F32), val_ref[j, rows, :].astype(F32))
                dup_ref[4 + j, rows, :] = dval.astype(BF)
                ext = jnp.concatenate([dgc, carry], axis=0)
                e1, e2 = ext[1:rc + 1], ext[2:rc + 2]
                dup_ref[j, rows, :] = (w2 * dgc + w1 * e1 + w0 * e2).astype(BF)
                gate = gate_ref[j, rows, :].astype(F32)
                acc_ref[j, 0:8, :] += _fold8(dgc)
                acc_ref[j, 8:16, :] += _fold8(gate * e2)
                acc_ref[j, 16:24, :] += _fold8(gate * e1)
                acc_ref[j, 24:32, :] += _fold8(gate * dgc)
                return dgc[0:8]

            _chunks(tm, rc, step, below, reverse=True)

        @pl.when(last)
        def _():
            for j in range(4):
                for q in range(4):
                    st_ref[j, q:q + 1, :] = jnp.sum(acc_ref[j, 8 * q:8 * q + 8, :], axis=0, keepdims=True)
                st_ref[j, 4:8, :] = jnp.zeros((4, FB), F32)

    return pl.pallas_call(
        body, name=name, grid=(t // tm,),
        in_specs=[_brows(4, tm, FB), _bnext(4, 16, FB, tm, t), _brows(4, tm, FB), _bnext(4, 16, FB, tm, t),
                  _brows(4, tm, FB, 1), _bnext(4, 16, FB, tm, t, 1), _brows(4, tm, FB, 0), _const((4, 3, FB))],
        out_specs=[_brows(8, tm, FB), _const((4, 8, FB), single=False)],
        out_shape=[jax.ShapeDtypeStruct((8, t, FB), BF), jax.ShapeDtypeStruct((4, 8, FB), F32)],
        scratch_shapes=[pltpu.VMEM((4, 32, FB), F32)], compiler_params=_params(1),
    )(da, da, gc, gc, up, up, up, convw)


def wgrad(a, b, *, nblk, a_blocked, b_blocked, bk, bn, tt, name):
    t = a.shape[1] if a.ndim == 3 else a.shape[0]
    nt = t // tt

    def body(a_ref, b_ref, o_ref, acc_ref):
        s = pl.program_id(1)

        @pl.when(s == 0)
        def _():
            acc_ref[...] = jnp.zeros_like(acc_ref)

        av = a_ref[0] if a.ndim == 3 else a_ref[...]
        bv = b_ref[0] if b.ndim == 3 else b_ref[...]
        acc_ref[...] += _dot_tn(av, bv)

        @pl.when(s == nt - 1)
        def _():
            o_ref[0] = acc_ref[...].astype(BF)

    def spec(arr, blocked, width):
        if arr.ndim == 3:
            return pl.BlockSpec((1, tt, width), lambda j, s: (j, s, 0))
        if blocked:
            return pl.BlockSpec((tt, width), lambda j, s: (s, j))
        return pl.BlockSpec((tt, width), lambda j, s: (s, 0))

    return pl.pallas_call(
        body, name=name, grid=(nblk, nt), in_specs=[spec(a, a_blocked, bk), spec(b, b_blocked, bn)],
        out_specs=pl.BlockSpec((1, bk, bn), lambda j, s: (j, 0, 0)),
        out_shape=jax.ShapeDtypeStruct((nblk, bk, bn), BF),
        scratch_shapes=[pltpu.VMEM((bk, bn), F32)], compiler_params=_params(2))(a, b)


def mod_partial(c_all, w_mod):
    cols = w_mod.shape[2]

    def body(c_ref, w_ref, o_ref):
        c = c_ref[...]
        ca = c * _sigmoid(c)
        o_ref[0] = jnp.dot(ca, w_ref[0], preferred_element_type=F32, precision=lax.Precision.HIGHEST)

    return pl.pallas_call(
        body, name="mod_partial", grid=(DEPTH,),
        in_specs=[pl.BlockSpec((NDEV, D), lambda l: (0, 0)), pl.BlockSpec((1, D, cols), lambda l: (l, 0, 0))],
        out_specs=pl.BlockSpec((1, NDEV, cols), lambda l: (l, 0, 0)),
        out_shape=jax.ShapeDtypeStruct((DEPTH, NDEV, cols), F32), compiler_params=_params(1))(c_all, w_mod)


def mod_finish(parts, b_mod):
    cols = parts.shape[2]

    def body(p_ref, b_ref, o_ref):
        for e in range(NDEV):
            o_ref[:, e * cols:(e + 1) * cols] = p_ref[e] + b_ref[:, e * cols:(e + 1) * cols]

    return pl.pallas_call(
        body, name="mod_finish", out_shape=jax.ShapeDtypeStruct((DEPTH, NDEV * cols), F32))(parts, b_mod)


def sum_parts(parts):
    n, r, c = parts.shape

    def body(p_ref, o_ref):
        acc = p_ref[0]
        for j in range(1, n):
            acc = acc + p_ref[j]
        o_ref[...] = acc

    return pl.pallas_call(body, name="sum_parts", out_shape=jax.ShapeDtypeStruct((r, c), F32))(parts)


def mod_wgrad(c_all_t, gmod_cols):
    cols = gmod_cols.shape[2]

    def body(c_ref, g_ref, o_ref):
        c = c_ref[...]
        ca = c * _sigmoid(c)
        acc = ca[:, 0:1] * g_ref[0, 0:1, :]
        for b in range(1, NDEV):
            acc = acc + ca[:, b:b + 1] * g_ref[0, b:b + 1, :]
        o_ref[0] = acc

    return pl.pallas_call(
        body, name="mod_wgrad", grid=(DEPTH,),
        in_specs=[pl.BlockSpec((D, NDEV), lambda l: (0, 0)), pl.BlockSpec((1, NDEV, cols), lambda l: (l, 0, 0))],
        out_specs=pl.BlockSpec((1, D, cols), lambda l: (l, 0, 0)),
        out_shape=jax.ShapeDtypeStruct((DEPTH, D, cols), F32), compiler_params=_params(1))(c_all_t, gmod_cols)


def _adamw_math(g, w, m, v):
    m2 = B1 * m + (1.0 - B1) * g
    v2 = B2 * v + (1.0 - B2) * (g * g)
    m_hat = m2 / (1.0 - B1 ** STEP)
    v_hat = v2 / (1.0 - B2 ** STEP)
    delta = -LR * (m_hat / (jnp.sqrt(v_hat) + ADAM_EPS) + WD * w)
    return delta, m2, v2


def _row_tile(r, c, budget=1 << 18):
    if r * c <= budget or r % 8:
        return r
    best = 8
    for cand in range(8, r + 1, 8):
        if r % cand == 0 and cand * c <= budget:
            best = cand
    return best


def adamw_sum(parts, w, m, v, *, name):
    n, r, c = parts.shape
    tr = _row_tile(r, c)

    def body(p_ref, w_ref, m_ref, v_ref, g_ref, d_ref, m2_ref, v2_ref):
        g = p_ref[0].astype(F32)
        for j in range(1, n):
            g = g + p_ref[j].astype(F32)
        d, m2, v2 = _adamw_math(g, w_ref[...], m_ref[...], v_ref[...])
        g_ref[...] = g
        d_ref[...] = d
        m2_ref[...] = m2
        v2_ref[...] = v2

    blk = pl.BlockSpec((tr, c), lambda i: (i, 0))
    out = jax.ShapeDtypeStruct((r, c), F32)
    return pl.pallas_call(
        body, name=name, grid=(r // tr,), in_specs=[pl.BlockSpec((n, tr, c), lambda i: (0, i, 0)), blk, blk, blk],
        out_specs=[blk] * 4, out_shape=[out] * 4, compiler_params=_params(1))(parts, w, m, v)


def _my_id():
    return 4 * lax.axis_index("x") + 2 * lax.axis_index("y") + lax.axis_index("c")


def _peer(k):
    x, y, c = lax.axis_index("x"), lax.axis_index("y"), lax.axis_index("c")
    px = 1 - x if k & 4 else x
    py = 1 - y if k & 2 else y
    pc = 1 - c if k & 1 else c
    return (px, py, pc), 4 * px + 2 * py + pc


def all_gather(shards, *, name):
    n = len(shards)

    def body(*refs):
        src, dst = refs[:n], refs[n:2 * n]
        send_sems, recv_sems, local_sems = refs[2 * n:]
        me = _my_id()
        copies = []
        for a in range(n):
            loc = pltpu.make_async_copy(src[a], dst[a].at[me], local_sems.at[a])
            loc.start()
            copies.append(loc)
        sends = []
        for k in range(1, NDEV):
            to, _ = _peer(k)
            for a in range(n):
                cp = pltpu.make_async_remote_copy(src_ref=src[a], dst_ref=dst[a].at[me], send_sem=send_sems.at[a * NDEV + k],
                                                  recv_sem=recv_sems.at[a * NDEV + k], device_id=to, device_id_type=MESH)
                cp.start()
                sends.append(cp)
        for k in range(1, NDEV):
            to, pid = _peer(k)
            for a in range(n):
                pltpu.make_async_remote_copy(src_ref=src[a], dst_ref=dst[a].at[pid], send_sem=send_sems.at[a * NDEV + k],
                                             recv_sem=recv_sems.at[a * NDEV + k], device_id=to, device_id_type=MESH).wait_recv()
        for cp in sends:
            cp.wait_send()
        for cp in copies:
            cp.wait()

    return pl.pallas_call(
        body, name=name, in_specs=[ANY] * n, out_specs=[ANY] * n,
        out_shape=[jax.ShapeDtypeStruct((NDEV,) + s.shape, s.dtype) for s in shards],
        scratch_shapes=[pltpu.SemaphoreType.DMA((n * NDEV,)), pltpu.SemaphoreType.DMA((n * NDEV,)), pltpu.SemaphoreType.DMA((n,))],
    )(*shards)


def all_to_all(groups, *, name):
    flat = [(gi, li, a) for gi, g in enumerate(groups) for li, a in enumerate(g)]
    n = len(flat)
    ng = len(groups)

    def body(*refs):
        src, dst = refs[:n], refs[n:n + ng]
        send_sems, recv_sems, local_sems = refs[n + ng:]
        me = _my_id()
        copies = []
        for a, (gi, li, _) in enumerate(flat):
            loc = pltpu.make_async_copy(src[a].at[me], dst[gi].at[me, li], local_sems.at[a])
            loc.start()
            copies.append(loc)
        sends = []
        for k in range(1, NDEV):
            to, pid = _peer(k)
            for a, (gi, li, _) in enumerate(flat):
                cp = pltpu.make_async_remote_copy(src_ref=src[a].at[pid], dst_ref=dst[gi].at[me, li],
                                                  send_sem=send_sems.at[a * NDEV + k], recv_sem=recv_sems.at[a * NDEV + k],
                                                  device_id=to, device_id_type=MESH)
                cp.start()
                sends.append(cp)
        for k in range(1, NDEV):
            to, pid = _peer(k)
            for a, (gi, li, _) in enumerate(flat):
                pltpu.make_async_remote_copy(src_ref=src[a].at[pid], dst_ref=dst[gi].at[pid, li],
                                             send_sem=send_sems.at[a * NDEV + k], recv_sem=recv_sems.at[a * NDEV + k],
                                             device_id=to, device_id_type=MESH).wait_recv()
        for cp in sends:
            cp.wait_send()
        for cp in copies:
            cp.wait()

    return pl.pallas_call(
        body, name=name, in_specs=[ANY] * n, out_specs=[ANY] * ng,
        out_shape=[jax.ShapeDtypeStruct((NDEV, len(g)) + g[0].shape[1:], g[0].dtype) for g in groups],
        scratch_shapes=[pltpu.SemaphoreType.DMA((n * NDEV,)), pltpu.SemaphoreType.DMA((n * NDEV,)), pltpu.SemaphoreType.DMA((n,))],
    )(*[a for _, _, a in flat])


def _pack(arrays):
    flat, layout, off = [], [], 0
    for a in arrays:
        flat.append(a.reshape(-1))
        layout.append((off, a.shape))
        off += a.size
    pad = (-off) % 1024
    if pad:
        flat.append(jnp.zeros((pad,), F32))
    return jnp.concatenate(flat).reshape(-1, 128), layout


def _unpack(packed, layout, lead=()):
    flat = packed.reshape(lead + (-1,))
    return [flat[..., off:off + _size(shape)].reshape(lead + tuple(shape)) for off, shape in layout]


def _size(shape):
    n = 1
    for s in shape:
        n *= s
    return n


def _join_last(g):
    g = jnp.moveaxis(g, 0, -2)
    return g.reshape(g.shape[:-2] + (g.shape[-2] * g.shape[-1],))


def _my_cols(a, width):
    return lax.dynamic_slice_in_dim(a, _my_id() * width, width, axis=a.ndim - 1)


def _tile(t, pref):
    return min(pref, t)


def kernel(x, c, w_mod, b_mod, norm_g, sc_w_in, sc_conv, sc_w_out, pool_w, pool_b, pool_scale, cf_w_pw1, cf_b_pw1, cf_w_dw, cf_b_dw, cf_ln_g, cf_ln_b, cf_w_pw2, cf_b_pw2, ffn_w_up, ffn_conv, ffn_b_conv, ffn_w_down, loss_target, m_w_mod, m_b_mod, m_norm_g, m_sc_w_in, m_sc_conv, m_sc_w_out, m_pool_w, m_pool_b, m_pool_scale, m_cf_w_pw1, m_cf_b_pw1, m_cf_w_dw, m_cf_b_dw, m_cf_ln_g, m_cf_ln_b, m_cf_w_pw2, m_cf_b_pw2, m_ffn_w_up, m_ffn_conv, m_ffn_b_conv, m_ffn_w_down, v_w_mod, v_b_mod, v_norm_g, v_sc_w_in, v_sc_conv, v_sc_w_out, v_pool_w, v_pool_b, v_pool_scale, v_cf_w_pw1, v_cf_b_pw1, v_cf_w_dw, v_cf_b_dw, v_cf_ln_g, v_cf_ln_b, v_cf_w_pw2, v_cf_b_pw2, v_ffn_w_up, v_ffn_conv, v_ffn_b_conv, v_ffn_w_down):
    env = dict(locals())
    names = ["w_mod", "b_mod", "norm_g", "sc_w_in", "sc_conv", "sc_w_out", "pool_w", "pool_b", "pool_scale", "cf_w_pw1",
             "cf_b_pw1", "cf_w_dw", "cf_b_dw", "cf_ln_g", "cf_ln_b", "cf_w_pw2", "cf_b_pw2", "ffn_w_up", "ffn_conv",
             "ffn_b_conv", "ffn_w_down"]
    t = x.shape[1]
    tm = _tile(t, 512)
    tm_ffn = _tile(t, 256)
    tt = _tile(t, 2048)
    x0, target = x[0], loss_target[0]

    small_names = ["norm_g", "sc_conv", "cf_b_pw1", "cf_w_dw", "cf_b_dw", "cf_ln_g", "cf_ln_b", "cf_b_pw2", "ffn_conv"]
    packed, layout = _pack([c] + [env[n] for n in small_names])
    (gathered,) = all_gather([packed], name="gather_small")
    parts = _unpack(gathered, layout, lead=(NDEV,))
    c_all = parts[0].reshape(NDEV, D)
    full = {n: _join_last(p) for n, p in zip(small_names, parts[1:])}

    shards = ([sc_w_in[j].astype(BF) for j in range(2)] + [sc_w_out[j].astype(BF) for j in range(2)]
              + [pool_w[0].astype(BF), cf_w_pw1[0].astype(BF), cf_w_pw2[0].astype(BF)]
              + [ffn_w_up[l].astype(BF) for l in range(DEPTH)] + [ffn_w_down[l].astype(BF) for l in range(DEPTH)])
    got = all_gather(shards, name="gather_weights")
    w_in_g = got[0:2]
    w_out_f = [g.reshape(D, D) for g in got[2:4]]
    pool_w_f = jnp.swapaxes(got[4], 0, 1).reshape(4, PG, PG)
    w_pw1_g = got[5]
    w_pw2_f = got[6].reshape(D, D)
    w_up_g = got[7:11]
    w_down_f = [g.reshape(F, D) for g in got[11:15]]

    mp = mod_partial(c_all, w_mod)
    (mod_parts,) = all_to_all([[jnp.swapaxes(mp, 0, 1)]], name="exchange_mod")
    mod = mod_finish(mod_parts[:, 0], b_mod)

    def vec(a):
        return a.reshape(1, -1)

    def ffn_blocks(a):
        return jnp.swapaxes(a.reshape(a.shape[0], 4, FB), 0, 1)

    saved = []
    xs = x0
    for l in range(DEPTH):
        sh1, sc1, g1, sh2, sc2, g2 = [mod[l:l + 1, k * D:(k + 1) * D] for k in range(6)]
        ng = [full["norm_g"][l, k:k + 1] for k in range(4)]
        kind, j = l % 3, l // 3
        s = dict(x_in=xs, sc1=sc1, g1=g1, sc2=sc2, g2=g2, ng=ng)
        if kind == 0:
            s["h"], s["p"] = fwd_in(xs, ng[0], sc1, sh1, w_in_g[j], None, blocked=False, tm=tm, name=f"sc_in_{l}")
            x1, s["m"], s["q"] = sc_fwd_out(s["p"], full["sc_conv"][j], w_out_f[j], xs, ng[1], g1, tm=tm, name=f"sc_out_{l}")
        elif kind == 1:
            x1, s["m"], s["ypre"], s["pooled"] = pool_fwd(xs, ng[0], sc1, sh1, pool_w_f, pool_b, pool_scale, ng[1], g1,
                                                          tm=tm, name=f"pool_{l}")
        else:
            s["h"], s["a"] = fwd_in(xs, ng[0], sc1, sh1, w_pw1_g, full["cf_b_pw1"].reshape(NDEV, 1, 2 * D // NDEV),
                                    blocked=False, tm=tm, name=f"cf_in_{l}")
            x1, s["m"], s["s"], s["u2"] = cf_fwd_out(s["a"], full["cf_w_dw"][0], full["cf_b_dw"], full["cf_ln_g"],
                                                     full["cf_ln_b"], w_pw2_f, full["cf_b_pw2"], xs, ng[1], g1,
                                                     tm=tm, name=f"cf_out_{l}")
        s["x1"] = x1
        s["cw"] = ffn_blocks(full["ffn_conv"][l])
        s["h2"], s["up"] = fwd_in(x1, ng[2], sc2, sh2, w_up_g[l], None, blocked=True, tm=tm, name=f"ffn_in_{l}")
        xs, s["f"], s["gc"], s["fa"] = ffn_fwd_out(s["up"], s["cw"], ffn_blocks(ffn_b_conv[l:l + 1]), w_down_f[l], x1,
                                                   ng[3], g2, tm=tm_ffn, name=f"ffn_out_{l}")
        saved.append(s)

    dx, loss_part = loss_head(xs, target, tm=tm, name="loss_head")
    loss = lax.psum(loss_part[0, 0], ("x", "y", "c"))

    gmod = [None] * DEPTH
    d_norm_g = [None] * DEPTH
    d_ffn_conv = [None] * DEPTH
    d_ffn_b_conv = [None] * DEPTH
    d_sc_conv = [None] * 2
    big = {}
    small_g = {}
    for l in reversed(range(DEPTH)):
        s = saved[l]
        ng = s["ng"]
        kind, j = l % 3, l // 3
        df, da, st_o = bwd_out(dx, s["f"], ng[3], s["g2"], w_down_f[l], blocked=True, tm=tm, name=f"ffn_bout_{l}")
        dup, st_c = ffn_bwd_mid(da, s["gc"], s["up"], s["cw"], tm=tm_ffn, name=f"ffn_bmid_{l}")
        dx1, st_i = bwd_in([dup], w_up_g[l], s["x1"], ng[2], s["sc2"], dx, tm=tm, name=f"ffn_bin_{l}")
        big[f"up{l}"] = wgrad(s["h2"], dup, nblk=NDEV, a_blocked=False, b_blocked=True, bk=D, bn=FB, tt=tt, name=f"ffn_wup_{l}")
        big[f"down{l}"] = wgrad(s["fa"], df, nblk=4, a_blocked=True, b_blocked=False, bk=FB, bn=D, tt=tt,
                                name=f"ffn_wdown_{l}").reshape(NDEV, F // NDEV, D)
        d_ffn_b_conv[l] = st_c[:, 0, :].reshape(F)
        d_ffn_conv[l] = jnp.swapaxes(st_c[:, 1:4, :], 0, 1).reshape(3, F)
        g_ffn = [st_i[0], st_i[1], st_o[0]]
        dn3, dn2 = st_o[1], st_i[2]
        if kind == 0:
            dm, dq, st_o = bwd_out(dx1, s["m"], ng[1], s["g1"], w_out_f[j], blocked=False, tm=tm, name=f"sc_bout_{l}")
            dp, st_c = sc_bwd_mid(dq, s["p"], full["sc_conv"][j], tm=tm, name=f"sc_bmid_{l}")
            dx, st_i = bwd_in([dp], w_in_g[j], s["x_in"], ng[0], s["sc1"], dx1, tm=tm, name=f"sc_bin_{l}")
            big[f"in{j}"] = wgrad(s["h"], dp, nblk=NDEV, a_blocked=False, b_blocked=True, bk=D, bn=3 * D // NDEV, tt=tt,
                                  name=f"sc_win_{l}")
            big[f"out{j}"] = wgrad(s["q"], dm, nblk=1, a_blocked=False, b_blocked=False, bk=D, bn=D, tt=tt,
                                   name=f"sc_wout_{l}").reshape(NDEV, D // NDEV, D)
            d_sc_conv[j] = st_c[0:3]
        elif kind == 1:
            dh, dyp, st_o = pool_bwd(dx1, s["m"], s["ypre"], pool_w_f, pool_scale, ng[1], s["g1"], tm=tm, name=f"pool_b_{l}")
            dx, st_i = bwd_in([dh], None, s["x_in"], ng[0], s["sc1"], dx1, tm=tm, name=f"pool_bin_{l}")
            dpw = wgrad(s["pooled"], dyp, nblk=4, a_blocked=True, b_blocked=True, bk=PG, bn=PG, tt=tt, name=f"pool_w_{l}")
            big["pool"] = jnp.swapaxes(dpw.reshape(4, NDEV, PG // NDEV, PG), 0, 1).reshape(NDEV, 4 * PG // NDEV, PG)
            small_g["pool_scale"], small_g["pool_b"] = st_o[2:3], st_o[3:4]
        else:
            dm, ds, st_o = bwd_out(dx1, s["m"], ng[1], s["g1"], w_pw2_f, blocked=False, tm=tm, name=f"cf_bout_{l}")
            dA, st_c = cf_bwd_mid(ds, s["u2"], s["a"], full["cf_w_dw"][0], full["cf_ln_g"], full["cf_ln_b"], tm=tm,
                                  name=f"cf_bmid_{l}")
            dx, st_i = bwd_in([dA], w_pw1_g, s["x_in"], ng[0], s["sc1"], dx1, tm=tm, name=f"cf_bin_{l}")
            big["pw1"] = wgrad(s["h"], dA, nblk=NDEV, a_blocked=False, b_blocked=True, bk=D, bn=2 * D // NDEV, tt=tt,
                               name=f"cf_wpw1_{l}")
            big["pw2"] = wgrad(s["s"], dm, nblk=1, a_blocked=False, b_blocked=False, bk=D, bn=D, tt=tt,
                               name=f"cf_wpw2_{l}").reshape(NDEV, D // NDEV, D)
            small_g["cf_w_dw"] = st_c[0:CFW][None]
            small_g["cf_b_dw"], small_g["cf_ln_g"], small_g["cf_ln_b"] = st_c[31:32], st_c[32:33], st_c[33:34]
            small_g["cf_b_pw1"] = st_c[34:36].reshape(1, 2 * D)
            small_g["cf_b_pw2"] = st_o[2:3]
        gmod[l] = jnp.concatenate([st_i[0], st_i[1], st_o[0]] + g_ffn)
        d_norm_g[l] = jnp.stack([st_i[2], st_o[1], dn2, dn3])

    small_g["gmod"] = jnp.stack(gmod)
    small_g["norm_g"] = jnp.stack(d_norm_g)
    small_g["sc_conv"] = jnp.stack(d_sc_conv)
    small_g["ffn_conv"] = jnp.stack(d_ffn_conv)
    small_g["ffn_b_conv"] = jnp.stack(d_ffn_b_conv)
    sg_names = ["gmod", "norm_g", "sc_conv", "pool_b", "pool_scale", "cf_b_pw1", "cf_w_dw", "cf_b_dw", "cf_ln_g", "cf_ln_b",
                "cf_b_pw2", "ffn_conv", "ffn_b_conv"]
    gpacked, glayout = _pack([small_g[n] for n in sg_names])
    (ggath,) = all_gather([gpacked], name="gather_small_grads")
    gsum = dict(zip(sg_names, _unpack(sum_parts(ggath), glayout)))
    gmod_all = _unpack(ggath, glayout[:1], lead=(NDEV,))[0]
    grads = {"b_mod": gsum["gmod"], "pool_b": gsum["pool_b"], "pool_scale": gsum["pool_scale"],
             "ffn_b_conv": gsum["ffn_b_conv"]}
    for n in ["norm_g", "sc_conv", "cf_b_pw1", "cf_w_dw", "cf_b_dw", "cf_ln_g", "cf_ln_b", "cf_b_pw2", "ffn_conv"]:
        grads[n] = _my_cols(gsum[n], env[n].shape[-1])
    grads["w_mod"] = mod_wgrad(c_all.T, jnp.swapaxes(_my_cols(gmod_all, w_mod.shape[2]), 0, 1))

    deltas, new_m, new_v = {}, {}, {}
    sp_names = ["b_mod", "norm_g", "sc_conv", "pool_b", "pool_scale", "cf_b_pw1", "cf_w_dw", "cf_b_dw", "cf_ln_g", "cf_ln_b",
                "cf_b_pw2", "ffn_conv", "ffn_b_conv"]
    pg, playout = _pack([grads[n] for n in sp_names])
    pw_, _ = _pack([env[n] for n in sp_names])
    pm_, _ = _pack([env["m_" + n] for n in sp_names])
    pv_, _ = _pack([env["v_" + n] for n in sp_names])
    _, sd, sm, sv = adamw_sum(pg[None], pw_, pm_, pv_, name="adamw_small")
    for n, d_, m_, v_ in zip(sp_names, _unpack(sd, playout), _unpack(sm, playout), _unpack(sv, playout)):
        deltas[n], new_m[n], new_v[n] = d_, m_, v_
    gw = grads["w_mod"].reshape(1, DEPTH * D, -1)
    _, d_, m_, v_ = adamw_sum(gw, w_mod.reshape(gw.shape[1:]), m_w_mod.reshape(gw.shape[1:]), v_w_mod.reshape(gw.shape[1:]),
                              name="adamw_w_mod")
    deltas["w_mod"], new_m["w_mod"], new_v["w_mod"] = [a.reshape(w_mod.shape) for a in (d_, m_, v_)]

    groups = {"sc_w_in": [big["in0"], big["in1"]], "sc_w_out": [big["out0"], big["out1"]], "pool_w": [big["pool"]],
              "cf_w_pw1": [big["pw1"]], "cf_w_pw2": [big["pw2"]], "ffn_w_up": [big[f"up{l}"] for l in range(DEPTH)],
              "ffn_w_down": [big[f"down{l}"] for l in range(DEPTH)]}
    gnames = list(groups)
    recv = all_to_all([groups[n] for n in gnames], name="scatter_grads")
    for n, r in zip(gnames, recv):
        rows, cols = r.shape[1] * r.shape[2], r.shape[3]
        w2 = [env[p + n].reshape(rows, cols) for p in ("", "m_", "v_")]
        outs = adamw_sum(r.reshape(NDEV, rows, cols), *w2, name=f"adamw_{n}")
        grads[n], deltas[n], new_m[n], new_v[n] = [a.reshape(env[n].shape) for a in outs]

    return (loss, dx[None], *[grads[n] for n in names], *[deltas[n] for n in names], *[new_m[n] for n in names],
            *[new_v[n] for n in names])
```

```python
import functools

import jax
import jax.numpy as jnp
from jax import lax
from jax.experimental import pallas as pl
from jax.experimental.pallas import tpu as pltpu

D = 1024
F = 2816
NDEV = 8
FB = F // 4
DEPTH = 4
RMS_EPS = 1e-6
LN_EPS = 1e-5
CFW = 31
POOL_WINDOWS = (2, 4, 8, 16)
PG = D // 4
LR, B1, B2, ADAM_EPS, WD, STEP = 0.001, 0.9, 0.999, 1e-08, 0.01, 10

BF = jnp.bfloat16
F32 = jnp.float32
VMEM_LIMIT_V7X = 56 * 1024 * 1024
MESH = pl.DeviceIdType.MESH
ANY = pl.BlockSpec(memory_space=pl.ANY)


def _params(n_axes):
    return pltpu.CompilerParams(dimension_semantics=("arbitrary",) * n_axes, vmem_limit_bytes=VMEM_LIMIT_V7X)


def _const(shape, single=True):
    nd = len(shape)
    if single:
        return pl.BlockSpec(shape, lambda *_: (0,) * nd, pipeline_mode=pl.Buffered(1))
    return pl.BlockSpec(shape, lambda *_: (0,) * nd)


def _rows(tm, c):
    return pl.BlockSpec((tm, c), lambda i: (i, 0))


def _brows(nb, tm, c, b0=0):
    return pl.BlockSpec((nb, tm, c), lambda i: (b0, i, 0))


def _prev(hb, c, tm):
    return pl.BlockSpec((hb, c), lambda i: (jnp.maximum(i * (tm // hb) - 1, 0), 0))


def _next(hb, c, tm, t):
    return pl.BlockSpec((hb, c), lambda i: (jnp.minimum((i + 1) * (tm // hb), t // hb - 1), 0))


def _bprev(nb, hb, c, tm, b0=0):
    return pl.BlockSpec((nb, hb, c), lambda i: (b0, jnp.maximum(i * (tm // hb) - 1, 0), 0))


def _bnext(nb, hb, c, tm, t, b0=0):
    return pl.BlockSpec((nb, hb, c), lambda i: (b0, jnp.minimum((i + 1) * (tm // hb), t // hb - 1), 0))


def _sigmoid(v):
    return 0.5 * jnp.tanh(0.5 * v) + 0.5


def _fold8(v):
    r, c = v.shape
    return jnp.sum(v.reshape(r // 8, 8, c), axis=0)


def _chunks(n_rows, rc, step, init=0, reverse=False):
    n = n_rows // rc

    def it(c, carry):
        idx = (n - 1 - c) if reverse else c
        return step(pl.multiple_of(idx * rc, rc), carry)

    return lax.fori_loop(0, n, it, init)


def _row_shifted_copies(s_ref, n):
    for b in range(1, 8):
        s_ref[b, 0:n, :] = s_ref[0, pl.ds(b, n), :]


def _shifted(s_ref, o, tm):
    return s_ref[o % 8, pl.ds(8 * (o // 8), tm), :]


def _dot(a, b):
    return jnp.dot(a, b, preferred_element_type=F32)


def _dot_nt(a, b):
    return lax.dot_general(a, b, (((1,), (1,)), ((), ())), preferred_element_type=F32)


def _dot_tn(a, b):
    return lax.dot_general(a, b, (((0,), (0,)), ((), ())), preferred_element_type=F32)


def _rsum(v):
    return jnp.sum(v, axis=0, keepdims=True)


def _adaln(x, g, sc, sh):
    r = lax.rsqrt(jnp.mean(x * x, axis=-1, keepdims=True) + RMS_EPS)
    return (x * r * g) * (1.0 + sc) + sh


def _gated_res(x, m, gn, gt):
    r = lax.rsqrt(jnp.mean(m * m, axis=-1, keepdims=True) + RMS_EPS)
    return x + gt * (m * r * gn)


def _gated_res_bwd(dxo, m, gn, gt):
    r = lax.rsqrt(jnp.mean(m * m, axis=-1, keepdims=True) + RMS_EPS)
    mh = m * r
    dgt = _rsum(dxo * (mh * gn))
    dn = dxo * gt
    dgn = _rsum(dn * mh)
    dmh = dn * gn
    dm = r * (dmh - mh * jnp.mean(dmh * mh, axis=-1, keepdims=True))
    return dm, dgt, dgn


def _my_id():
    return 4 * lax.axis_index("x") + 2 * lax.axis_index("y") + lax.axis_index("c")


def _peer(k):
    x, y, c = lax.axis_index("x"), lax.axis_index("y"), lax.axis_index("c")
    px = 1 - x if k & 4 else x
    py = 1 - y if k & 2 else y
    pc = 1 - c if k & 1 else c
    return (px, py, pc), 4 * px + 2 * py + pc


class _Exchange:
    def __init__(self, kind, arrays):
        self.gather = kind == "gather"
        self.arrays = list(arrays)
        n = len(self.arrays)
        if self.gather:
            self.out_shape = [jax.ShapeDtypeStruct((NDEV,) + a.shape, a.dtype) for a in self.arrays]
        else:
            self.out_shape = [jax.ShapeDtypeStruct(a.shape, a.dtype) for a in self.arrays]
        self.scratch = [pltpu.SemaphoreType.DMA((n * NDEV,)), pltpu.SemaphoreType.DMA((n * NDEV,)),
                        pltpu.SemaphoreType.DMA((n,))]

    def _local(self, a, src, dst, sems):
        me = _my_id()
        return pltpu.make_async_copy(src[a] if self.gather else src[a].at[me], dst[a].at[me], sems[2].at[a])

    def _remote(self, a, k, src, dst, sems, incoming):
        to, pid = _peer(k)
        me = _my_id()
        return pltpu.make_async_remote_copy(
            src_ref=src[a] if self.gather else src[a].at[pid], dst_ref=dst[a].at[pid if incoming else me],
            send_sem=sems[0].at[a * NDEV + k], recv_sem=sems[1].at[a * NDEV + k], device_id=to, device_id_type=MESH)

    def start(self, src, dst, sems):
        for a in range(len(self.arrays)):
            self._local(a, src, dst, sems).start()
        for k in range(1, NDEV):
            for a in range(len(self.arrays)):
                self._remote(a, k, src, dst, sems, False).start()

    def wait(self, src, dst, sems):
        for k in range(1, NDEV):
            for a in range(len(self.arrays)):
                self._remote(a, k, src, dst, sems, True).wait_recv()
        for k in range(1, NDEV):
            for a in range(len(self.arrays)):
                self._remote(a, k, src, dst, sems, False).wait_send()
        for a in range(len(self.arrays)):
            self._local(a, src, dst, sems).wait()

    def run(self, name):
        n = len(self.arrays)

        def body(*refs):
            src, dst, sems = refs[:n], refs[n:2 * n], refs[2 * n:]
            self.start(src, dst, sems)
            self.wait(src, dst, sems)

        return pl.pallas_call(body, name=name, in_specs=[ANY] * n, out_specs=[ANY] * n, out_shape=self.out_shape,
                              scratch_shapes=self.scratch)(*self.arrays)


def _call(body, *, name, grid, in_specs, out_specs, out_shape, args, scratch_shapes=(), carry=None):
    cp = _params(len(grid))
    if carry is None:
        return tuple(pl.pallas_call(body, name=name, grid=grid, in_specs=in_specs, out_specs=out_specs, out_shape=out_shape,
                                    scratch_shapes=list(scratch_shapes), compiler_params=cp)(*args))
    n_in, n_out, n_sc, n_c = len(in_specs), len(out_specs), len(scratch_shapes), len(carry.arrays)

    def wrapped(*refs):
        ins, src = refs[:n_in], refs[n_in:n_in + n_c]
        outs = refs[n_in + n_c:n_in + n_c + n_out]
        dst = refs[n_in + n_c + n_out:n_in + 2 * n_c + n_out]
        rest = refs[n_in + 2 * n_c + n_out:]
        scr, sems = rest[:n_sc], rest[n_sc:]
        first = pl.program_id(0) == 0
        last = pl.program_id(0) == grid[0] - 1
        for ax in range(1, len(grid)):
            first = jnp.logical_and(first, pl.program_id(ax) == 0)
            last = jnp.logical_and(last, pl.program_id(ax) == grid[ax] - 1)

        @pl.when(first)
        def _():
            carry.start(src, dst, sems)

        body(*ins, *outs, *scr)

        @pl.when(last)
        def _():
            carry.wait(src, dst, sems)

    res = pl.pallas_call(
        wrapped, name=name, grid=grid, in_specs=list(in_specs) + [ANY] * n_c, out_specs=list(out_specs) + [ANY] * n_c,
        out_shape=list(out_shape) + carry.out_shape, scratch_shapes=list(scratch_shapes) + carry.scratch,
        compiler_params=cp)(*args, *carry.arrays)
    return tuple(res[:n_out]) + (list(res[n_out:]),)


def fwd_in(x, g, sc, sh, w, bias, *, blocked, tm, name, carry=None):
    t = x.shape[0]
    nb, _, bw = w.shape

    def body(*refs):
        if bias is None:
            x_ref, g_ref, sc_ref, sh_ref, w_ref, h_ref, p_ref = refs
        else:
            x_ref, g_ref, sc_ref, sh_ref, w_ref, b_ref, h_ref, p_ref = refs
        hb = _adaln(x_ref[...], g_ref[...], sc_ref[...], sh_ref[...]).astype(BF)
        h_ref[...] = hb
        for d in range(nb):
            y = _dot(hb, w_ref[d])
            if bias is not None:
                y = y + b_ref[d]
            if blocked:
                p_ref[d] = y.astype(BF)
            else:
                p_ref[:, d * bw:(d + 1) * bw] = y.astype(BF)

    vec = _const((1, D))
    in_specs = [_rows(tm, D), vec, vec, vec, _const((nb, D, bw))]
    args = [x, g, sc, sh, w]
    if bias is not None:
        in_specs.append(_const((nb, 1, bw)))
        args.append(bias)
    if blocked:
        p_spec, p_shape = _brows(nb, tm, bw), jax.ShapeDtypeStruct((nb, t, bw), BF)
    else:
        p_spec, p_shape = _rows(tm, nb * bw), jax.ShapeDtypeStruct((t, nb * bw), BF)
    return _call(body, name=name, grid=(t // tm,), in_specs=in_specs, out_specs=[_rows(tm, D), p_spec],
                 out_shape=[jax.ShapeDtypeStruct((t, D), BF), p_shape], args=args, carry=carry)


def _conv3_from(s_ref, w, tm, lo):
    acc = w[0:1, :] * s_ref[pl.ds(lo, tm), :]
    for k in (1, 2):
        acc = acc + w[k:k + 1, :] * s_ref[pl.ds(lo + k, tm), :]
    return acc


def sc_fwd_out(p, convw, w_out, x, gn, gt, *, tm, name, carry=None):
    t = x.shape[0]

    def body(p_ref, ph_ref, cw_ref, w_ref, x_ref, gn_ref, gt_ref, x1_ref, m_ref, q_ref, s_ref):
        i = pl.program_id(0)
        zh = ph_ref[8:16, D:2 * D].astype(F32) * ph_ref[8:16, 2 * D:3 * D].astype(F32)
        s_ref[0:8, :] = jnp.where(i == 0, 0.0, zh)
        s_ref[8:8 + tm, :] = p_ref[:, D:2 * D].astype(F32) * p_ref[:, 2 * D:3 * D].astype(F32)
        u = _conv3_from(s_ref, cw_ref[...], tm, 6)
        qb = (p_ref[:, 0:D].astype(F32) * u).astype(BF)
        q_ref[...] = qb
        m = _dot(qb, w_ref[...])
        m_ref[...] = m.astype(BF)
        x1_ref[...] = _gated_res(x_ref[...], m, gn_ref[...], gt_ref[...])

    vec = _const((1, D))
    return _call(
        body, name=name, grid=(t // tm,),
        in_specs=[_rows(tm, 3 * D), _prev(16, 3 * D, tm), _const((3, D)), _const((D, D)), _rows(tm, D), vec, vec],
        out_specs=[_rows(tm, D)] * 3,
        out_shape=[jax.ShapeDtypeStruct((t, D), F32), jax.ShapeDtypeStruct((t, D), BF), jax.ShapeDtypeStruct((t, D), BF)],
        scratch_shapes=[pltpu.VMEM((tm + 8, D), F32)], args=[p, p, convw, w_out, x, gn, gt], carry=carry)


def _layernorm_parts(u2):
    mu = jnp.mean(u2, axis=-1, keepdims=True)
    cen = u2 - mu
    rstd = lax.rsqrt(jnp.mean(cen * cen, axis=-1, keepdims=True) + LN_EPS)
    return cen * rstd, rstd


def cf_fwd_out(a, w_dw, b_dw, ln_g, ln_b, w_pw2, b_pw2, x, gn, gt, *, tm, name):
    t = x.shape[0]
    hb = 32

    def body(a_ref, ah_ref, wd_ref, bd_ref, lg_ref, lb_ref, w_ref, b2_ref, x_ref, gn_ref, gt_ref,
             x1_ref, m_ref, s_out_ref, u2_ref, s_ref):
        i = pl.program_id(0)
        uh = ah_ref[:, 0:D].astype(F32) * _sigmoid(ah_ref[:, D:2 * D].astype(F32))
        s_ref[0, 0:hb, :] = jnp.where(i == 0, 0.0, uh)
        s_ref[0, hb:hb + tm, :] = a_ref[:, 0:D].astype(F32) * _sigmoid(a_ref[:, D:2 * D].astype(F32))
        _row_shifted_copies(s_ref, tm + hb - 8)
        acc = bd_ref[...] + wd_ref[0:1, :] * _shifted(s_ref, hb - CFW + 1, tm)
        for k in range(1, CFW):
            acc = acc + wd_ref[k:k + 1, :] * _shifted(s_ref, hb - CFW + 1 + k, tm)
        u2_ref[...] = acc.astype(BF)
        xh, _ = _layernorm_parts(acc)
        l = xh * lg_ref[...] + lb_ref[...]
        sb = (l * _sigmoid(l)).astype(BF)
        s_out_ref[...] = sb
        m = _dot(sb, w_ref[...]) + b2_ref[...]
        m_ref[...] = m.astype(BF)
        x1_ref[...] = _gated_res(x_ref[...], m, gn_ref[...], gt_ref[...])

    vec = _const((1, D))
    return pl.pallas_call(
        body, name=name, grid=(t // tm,),
        in_specs=[_rows(tm, 2 * D), _prev(hb, 2 * D, tm), _const((CFW, D)), vec, vec, vec, _const((D, D)), vec,
                  _rows(tm, D), vec, vec],
        out_specs=[_rows(tm, D)] * 4,
        out_shape=[jax.ShapeDtypeStruct((t, D), F32)] + [jax.ShapeDtypeStruct((t, D), BF)] * 3,
        scratch_shapes=[pltpu.VMEM((8, tm + hb, D), F32)], compiler_params=_params(1),
    )(a, a, w_dw, b_dw, ln_g, ln_b, w_pw2, b_pw2, x, gn, gt)


def _pool_counts(i, tm, w):
    row = lax.broadcasted_iota(jnp.int32, (tm, 1), 0) + i * tm
    return jnp.minimum(row + 1, w).astype(F32)


def pool_fwd(x, g, sc, sh, pw, pb, pscale, gn, gt, *, tm, name):
    t = x.shape[0]
    pad, hb = 8, 16
    base = pad + hb

    def body(x_ref, xh_ref, g_ref, sc_ref, sh_ref, pw_ref, pb_ref, ps_ref, gn_ref, gt_ref,
             x1_ref, m_ref, yp_ref, po_ref, sa_ref, sb_ref):
        i = pl.program_id(0)
        hh = _adaln(xh_ref[...], g_ref[...], sc_ref[...], sh_ref[...])
        h = _adaln(x_ref[...], g_ref[...], sc_ref[...], sh_ref[...])
        zero = jnp.zeros((pad, D), F32)
        sa_ref[0:pad, :] = zero
        sb_ref[0:pad, :] = zero
        sa_ref[pad:base, :] = jnp.where(i == 0, 0.0, hh)
        sa_ref[base:base + tm, :] = h
        n = hb + tm
        src, dst = sa_ref, sb_ref
        ys = []
        for gi, w in enumerate(POOL_WINDOWS):
            c0 = gi * PG
            step = w // 2
            dst[pl.ds(pad, n), c0:D] = src[pl.ds(pad, n), c0:D] + src[pl.ds(pad - step, n), c0:D]
            mean = dst[pl.ds(base, tm), c0:c0 + PG] / _pool_counts(i, tm, w)
            pooled = (mean - h[:, c0:c0 + PG]).astype(BF)
            po_ref[:, c0:c0 + PG] = pooled
            ys.append(_dot(pooled, pw_ref[gi]))
            src, dst = dst, src
        ypre = jnp.concatenate(ys, axis=1) + pb_ref[...]
        yp_ref[...] = ypre.astype(BF)
        m = ypre * ps_ref[...]
        m_ref[...] = m.astype(BF)
        x1_ref[...] = _gated_res(x_ref[...], m, gn_ref[...], gt_ref[...])

    vec = _const((1, D))
    return pl.pallas_call(
        body, name=name, grid=(t // tm,),
        in_specs=[_rows(tm, D), _prev(hb, D, tm), vec, vec, vec, _const((4, PG, PG)), vec, vec, vec, vec],
        out_specs=[_rows(tm, D)] * 4,
        out_shape=[jax.ShapeDtypeStruct((t, D), F32)] + [jax.ShapeDtypeStruct((t, D), BF)] * 3,
        scratch_shapes=[pltpu.VMEM((tm + base, D), F32)] * 2, compiler_params=_params(1),
    )(x, x, g, sc, sh, pw, pb, pscale, gn, gt)


def ffn_fwd_out(up, convw, convb, w_down, x, gn, gt, *, tm, name, carry=None):
    t = x.shape[0]

    def body(gate_ref, gh_ref, val_ref, cw_ref, cb_ref, w_ref, x_ref, gn_ref, gt_ref,
             x2_ref, f_ref, gc_ref, a_ref, s_ref):
        i = pl.program_id(0)
        acc = jnp.zeros((tm, D), F32)
        for j in range(4):
            sj = s_ref.at[j]
            sj[0:8, :] = jnp.where(i == 0, 0.0, gh_ref[j, 8:16, :].astype(F32))
            sj[8:8 + tm, :] = gate_ref[j].astype(F32)
            gc = cb_ref[j] + _conv3_from(sj, cw_ref[j], tm, 6)
            gc_ref[j] = gc.astype(BF)
            ab = (gc * _sigmoid(gc) * val_ref[j].astype(F32)).astype(BF)
            a_ref[j] = ab
            acc = acc + _dot(ab, w_ref[j * FB:(j + 1) * FB, :])
        f_ref[...] = acc.astype(BF)
        x2_ref[...] = _gated_res(x_ref[...], acc, gn_ref[...], gt_ref[...])

    vec = _const((1, D))
    blk = jax.ShapeDtypeStruct((4, t, FB), BF)
    return _call(
        body, name=name, grid=(t // tm,),
        in_specs=[_brows(4, tm, FB, 0), _bprev(4, 16, FB, tm, 0), _brows(4, tm, FB, 1), _const((4, 3, FB)),
                  _const((4, 1, FB)), _const((F, D)), _rows(tm, D), vec, vec],
        out_specs=[_rows(tm, D), _rows(tm, D), _brows(4, tm, FB), _brows(4, tm, FB)],
        out_shape=[jax.ShapeDtypeStruct((t, D), F32), jax.ShapeDtypeStruct((t, D), BF), blk, blk],
        scratch_shapes=[pltpu.VMEM((4, tm + 8, FB), F32)], args=[up, up, up, convw, convb, w_down, x, gn, gt], carry=carry)


def loss_head(y, target, *, tm, name):
    t = y.shape[0]

    def body(y_ref, t_ref, dy_ref, l_ref, acc_ref):
        i = pl.program_id(0)

        @pl.when(i == 0)
        def _():
            acc_ref[...] = jnp.zeros_like(acc_ref)

        e = y_ref[...] - t_ref[...]
        dy_ref[...] = e * (1.0 / D)
        acc_ref[...] += _rsum(e * e)

        @pl.when(i == pl.num_programs(0) - 1)
        def _():
            l_ref[...] = jnp.sum(acc_ref[...], axis=1, keepdims=True) * (0.5 / D)

    return pl.pallas_call(
        body, name=name, grid=(t // tm,), in_specs=[_rows(tm, D), _rows(tm, D)],
        out_specs=[_rows(tm, D), pl.BlockSpec((1, 1), lambda i: (0, 0))],
        out_shape=[jax.ShapeDtypeStruct((t, D), F32), jax.ShapeDtypeStruct((1, 1), F32)],
        scratch_shapes=[pltpu.VMEM((1, D), F32)], compiler_params=_params(1))(y, target)


def _init_stats(ref):
    @pl.when(pl.program_id(0) == 0)
    def _():
        ref[...] = jnp.zeros_like(ref)


def bwd_out(dxo, m, gn, gt, w, *, blocked, tm, name, carry=None):
    t = dxo.shape[0]
    k = w.shape[0]

    def body(dx_ref, m_ref, gn_ref, gt_ref, w_ref, dm_ref, da_ref, st_ref):
        _init_stats(st_ref)
        dm, dgt, dgn = _gated_res_bwd(dx_ref[...], m_ref[...].astype(F32), gn_ref[...], gt_ref[...])
        st_ref[0:1, :] += dgt
        st_ref[1:2, :] += dgn
        st_ref[2:3, :] += _rsum(dm)
        dmb = dm.astype(BF)
        dm_ref[...] = dmb
        if blocked:
            for j in range(4):
                da_ref[j] = _dot_nt(dmb, w_ref[j * FB:(j + 1) * FB, :]).astype(BF)
        else:
            da_ref[...] = _dot_nt(dmb, w_ref[...]).astype(BF)

    vec = _const((1, D))
    if blocked:
        da_spec, da_shape = _brows(4, tm, FB), jax.ShapeDtypeStruct((4, t, FB), BF)
    else:
        da_spec, da_shape = _rows(tm, k), jax.ShapeDtypeStruct((t, k), BF)
    return _call(
        body, name=name, grid=(t // tm,), in_specs=[_rows(tm, D), _rows(tm, D), vec, vec, _const((k, D))],
        out_specs=[_rows(tm, D), da_spec, _const((8, D), single=False)],
        out_shape=[jax.ShapeDtypeStruct((t, D), BF), da_shape, jax.ShapeDtypeStruct((8, D), F32)],
        args=[dxo, m, gn, gt, w], carry=carry)


def bwd_in(dps, w, x, g, sc, dxo, *, tm, name):
    t = x.shape[0]
    direct = w is None
    if not direct:
        nb, _, bw = w.shape
    natural = (not direct) and dps[0].ndim == 2

    def body(*refs):
        n = len(dps)
        dp_refs = refs[:n]
        if direct:
            x_ref, g_ref, sc_ref, dxo_ref, dx_ref, st_ref = refs[n:]
            dh = dp_refs[0][...]
        else:
            w_ref, x_ref, g_ref, sc_ref, dxo_ref, dx_ref, st_ref = refs[n:]
            dh = jnp.zeros((tm, D), F32)
            if natural:
                for d in range(nb):
                    dh = dh + _dot_nt(dp_refs[0][:, d * bw:(d + 1) * bw], w_ref[d])
            else:
                d = 0
                for r in dp_refs:
                    for j in range(r.shape[0]):
                        dh = dh + _dot_nt(r[j], w_ref[d])
                        d += 1
        _init_stats(st_ref)
        x = x_ref[...]
        r = lax.rsqrt(jnp.mean(x * x, axis=-1, keepdims=True) + RMS_EPS)
        xh = x * r
        gv = g_ref[...]
        st_ref[0:1, :] += _rsum(dh)
        st_ref[1:2, :] += _rsum(dh * (xh * gv))
        dn = dh * (1.0 + sc_ref[...])
        st_ref[2:3, :] += _rsum(dn * xh)
        dy = dn * gv
        dx_ref[...] = dxo_ref[...] + r * (dy - xh * jnp.mean(dy * xh, axis=-1, keepdims=True))

    vec = _const((1, D))
    if direct:
        dp_specs = [_rows(tm, D)]
    elif natural:
        dp_specs = [_rows(tm, nb * bw)]
    else:
        dp_specs = [_brows(a.shape[0], tm, bw) for a in dps]
    w_specs, w_args = ([], []) if direct else ([_const((nb, D, bw))], [w])
    return pl.pallas_call(
        body, name=name, grid=(t // tm,),
        in_specs=dp_specs + w_specs + [_rows(tm, D), vec, vec, _rows(tm, D)],
        out_specs=[_rows(tm, D), _const((8, D), single=False)],
        out_shape=[jax.ShapeDtypeStruct((t, D), F32), jax.ShapeDtypeStruct((8, D), F32)],
        compiler_params=_params(1))(*dps, *w_args, x, g, sc, dxo)


def sc_bwd_mid(dq, p, convw, *, tm, name, carry=None):
    t = dq.shape[0]

    def body(dq_ref, dqn_ref, p_ref, pp_ref, pn_ref, cw_ref, dp_ref, st_ref, s1_ref, s2_ref):
        i = pl.program_id(0)
        last = i == pl.num_programs(0) - 1
        _init_stats(st_ref)
        cw = cw_ref[...]
        cg = p_ref[:, D:2 * D].astype(F32)
        hi = p_ref[:, 2 * D:3 * D].astype(F32)
        bg = p_ref[:, 0:D].astype(F32)
        zp = pp_ref[8:16, D:2 * D].astype(F32) * pp_ref[8:16, 2 * D:3 * D].astype(F32)
        s2_ref[0:8, :] = jnp.where(i == 0, 0.0, zp)
        s2_ref[8:8 + tm, :] = cg * hi
        u = _conv3_from(s2_ref, cw, tm, 6)
        dqf = dq_ref[...].astype(F32)
        dp_ref[:, 0:D] = (dqf * u).astype(BF)
        du = dqf * bg
        duh = dqn_ref[0:8, :].astype(F32) * pn_ref[0:8, 0:D].astype(F32)
        s1_ref[0:tm, :] = du
        s1_ref[tm:tm + 8, :] = jnp.where(last, 0.0, duh)
        dz = cw[2:3, :] * du + cw[1:2, :] * s1_ref[pl.ds(1, tm), :] + cw[0:1, :] * s1_ref[pl.ds(2, tm), :]
        dp_ref[:, D:2 * D] = (dz * hi).astype(BF)
        dp_ref[:, 2 * D:3 * D] = (dz * cg).astype(BF)
        for k in range(3):
            st_ref[k:k + 1, :] += _rsum(du * s2_ref[pl.ds(6 + k, tm), :])

    return _call(
        body, name=name, grid=(t // tm,),
        in_specs=[_rows(tm, D), _next(16, D, tm, t), _rows(tm, 3 * D), _prev(16, 3 * D, tm), _next(16, 3 * D, tm, t),
                  _const((3, D))],
        out_specs=[_rows(tm, 3 * D), _const((8, D), single=False)],
        out_shape=[jax.ShapeDtypeStruct((t, 3 * D), BF), jax.ShapeDtypeStruct((8, D), F32)],
        scratch_shapes=[pltpu.VMEM((tm + 8, D), F32)] * 2, args=[dq, dq, p, p, p, convw], carry=carry)


def cf_bwd_mid(ds, u2, a, w_dw, ln_g, ln_b, *, tm, name, carry=None):
    t = ds.shape[0]
    hb = 32

    def du2_of(dsv, u2v, lg, lb):
        xh, rstd = _layernorm_parts(u2v)
        l = xh * lg + lb
        sg = _sigmoid(l)
        dl = dsv * (sg * (1.0 + l * (1.0 - sg)))
        dxh = dl * lg
        du2 = rstd * (dxh - jnp.mean(dxh, axis=-1, keepdims=True) - xh * jnp.mean(dxh * xh, axis=-1, keepdims=True))
        return du2, dl, xh

    def body(ds_ref, dsn_ref, u2_ref, u2n_ref, a_ref, wd_ref, lg_ref, lb_ref, da_ref, st_ref, s1_ref):
        i = pl.program_id(0)
        last = i == pl.num_programs(0) - 1
        _init_stats(st_ref)
        lg, lb = lg_ref[...], lb_ref[...]
        du2, dl, xh = du2_of(ds_ref[...].astype(F32), u2_ref[...].astype(F32), lg, lb)
        st_ref[32:33, :] += _rsum(dl * xh)
        st_ref[33:34, :] += _rsum(dl)
        st_ref[31:32, :] += _rsum(du2)
        du2n, _, _ = du2_of(dsn_ref[...].astype(F32), u2n_ref[...].astype(F32), lg, lb)
        s1_ref[0, 0:tm, :] = du2
        s1_ref[0, tm:tm + hb, :] = jnp.where(last, 0.0, du2n)
        _row_shifted_copies(s1_ref, tm + hb - 8)
        av = a_ref[:, 0:D].astype(F32)
        sg = _sigmoid(a_ref[:, D:2 * D].astype(F32))
        u = av * sg
        du = jnp.zeros((tm, D), F32)
        for k in range(CFW):
            sh = _shifted(s1_ref, CFW - 1 - k, tm)
            du = du + wd_ref[k:k + 1, :] * sh
            st_ref[k:k + 1, :] += _rsum(u * sh)
        dav = du * sg
        dgv = du * u * (1.0 - sg)
        st_ref[34:35, :] += _rsum(dav)
        st_ref[35:36, :] += _rsum(dgv)
        da_ref[:, 0:D] = dav.astype(BF)
        da_ref[:, D:2 * D] = dgv.astype(BF)

    vec = _const((1, D))
    return _call(
        body, name=name, grid=(t // tm,),
        in_specs=[_rows(tm, D), _next(hb, D, tm, t), _rows(tm, D), _next(hb, D, tm, t), _rows(tm, 2 * D),
                  _const((CFW, D)), vec, vec],
        out_specs=[_rows(tm, 2 * D), _const((40, D), single=False)],
        out_shape=[jax.ShapeDtypeStruct((t, 2 * D), BF), jax.ShapeDtypeStruct((40, D), F32)],
        scratch_shapes=[pltpu.VMEM((8, tm + hb, D), F32)], args=[ds, ds, u2, u2, a, w_dw, ln_g, ln_b], carry=carry)


def pool_bwd(dxo, m, ypre, pw, pscale, gn, gt, *, tm, name):
    t = dxo.shape[0]
    hb = 16

    def dyp_of(dxv, mv, ypv, ps, gnv, gtv):
        dm, dgt, dgn = _gated_res_bwd(dxv, mv, gnv, gtv)
        return dm * ps, dgt, dgn, _rsum(dm * ypv)

    def body(dx_ref, dxn_ref, m_ref, mn_ref, yp_ref, ypn_ref, pw_ref, ps_ref, gn_ref, gt_ref,
             dh_ref, dyp_ref, st_ref, sa_ref, sb_ref):
        i = pl.program_id(0)
        last = i == pl.num_programs(0) - 1
        _init_stats(st_ref)
        ps, gnv, gtv = ps_ref[...], gn_ref[...], gt_ref[...]
        dyp, dgt, dgn, dps = dyp_of(dx_ref[...], m_ref[...].astype(F32), yp_ref[...].astype(F32), ps, gnv, gtv)
        st_ref[0:1, :] += dgt
        st_ref[1:2, :] += dgn
        st_ref[2:3, :] += dps
        st_ref[3:4, :] += _rsum(dyp)
        dypb = dyp.astype(BF)
        dyp_ref[...] = dypb
        dypn, _, _, _ = dyp_of(dxn_ref[...], mn_ref[...].astype(F32), ypn_ref[...].astype(F32), ps, gnv, gtv)
        dypnb = jnp.where(last, 0.0, dypn).astype(BF)
        dpo = []
        for gi, w in enumerate(POOL_WINDOWS):
            c0 = gi * PG
            dp_main = _dot_nt(dypb[:, c0:c0 + PG], pw_ref[gi])
            dp_next = _dot_nt(dypnb[:, c0:c0 + PG], pw_ref[gi])
            dpo.append(dp_main)
            sa_ref[0:tm, c0:c0 + PG] = dp_main / _pool_counts(i, tm, w)
            sa_ref[tm:tm + hb, c0:c0 + PG] = dp_next / float(w)
        zero = jnp.zeros((8, D), F32)
        sa_ref[tm + hb:tm + hb + 8, :] = zero
        sb_ref[tm + hb:tm + hb + 8, :] = zero
        n = tm + hb
        src, dst = sa_ref, sb_ref
        for gi, w in enumerate(POOL_WINDOWS):
            c0 = gi * PG
            step = w // 2
            dst[pl.ds(0, n), c0:D] = src[pl.ds(0, n), c0:D] + src[pl.ds(step, n), c0:D]
            dh_ref[:, c0:c0 + PG] = dst[pl.ds(0, tm), c0:c0 + PG] - dpo[gi]
            src, dst = dst, src

    vec = _const((1, D))
    return pl.pallas_call(
        body, name=name, grid=(t // tm,),
        in_specs=[_rows(tm, D), _next(hb, D, tm, t), _rows(tm, D), _next(hb, D, tm, t), _rows(tm, D),
                  _next(hb, D, tm, t), _const((4, PG, PG)), vec, vec, vec],
        out_specs=[_rows(tm, D), _rows(tm, D), _const((8, D), single=False)],
        out_shape=[jax.ShapeDtypeStruct((t, D), F32), jax.ShapeDtypeStruct((t, D), BF), jax.ShapeDtypeStruct((8, D), F32)],
        scratch_shapes=[pltpu.VMEM((tm + hb + 8, D), F32)] * 2, compiler_params=_params(1),
    )(dxo, dxo, m, m, ypre, ypre, pw, pscale, gn, gt)


def ffn_bwd_mid(da, gc, up, convw, *, tm, name, carry=None):
    t = da.shape[1]

    def dgc_of(dav, gcv, valv):
        sg = _sigmoid(gcv)
        return dav * valv * (sg * (1.0 + gcv * (1.0 - sg))), dav * (gcv * sg)

    rc = 16

    def body(da_ref, dan_ref, gc_ref, gcn_ref, val_ref, valn_ref, gate_ref, cw_ref, dup_ref, st_ref, acc_ref):
        i = pl.program_id(0)
        last = i == pl.num_programs(0) - 1
        _init_stats(acc_ref)
        for j in range(4):
            w0, w1, w2 = cw_ref[j, 0:1, :], cw_ref[j, 1:2, :], cw_ref[j, 2:3, :]
            dgcn, _ = dgc_of(dan_ref[j, 0:8, :].astype(F32), gcn_ref[j, 0:8, :].astype(F32), valn_ref[j, 0:8, :].astype(F32))
            below = jnp.where(last, 0.0, dgcn)

            def step(r0, carry):
                rows = pl.ds(r0, rc)
                dgc, dval = dgc_of(da_ref[j, rows, :].astype(F32), gc_ref[j, rows, :].astype(F32), val_ref[j, rows, :].astype(F32))
                dup_ref[4 + j, rows, :] = dval.astype(BF)
                ext = jnp.concatenate([dgc, carry], axis=0)
                e1, e2 = ext[1:rc + 1], ext[2:rc + 2]
                dup_ref[j, rows, :] = (w2 * dgc + w1 * e1 + w0 * e2).astype(BF)
                gate = gate_ref[j, rows, :].astype(F32)
                acc_ref[j, 0:8, :] += _fold8(dgc)
                acc_ref[j, 8:16, :] += _fold8(gate * e2)
                acc_ref[j, 16:24, :] += _fold8(gate * e1)
                acc_ref[j, 24:32, :] += _fold8(gate * dgc)
                return dgc[0:8]

            _chunks(tm, rc, step, below, reverse=True)

        @pl.when(last)
        def _():
            for j in range(4):
                for q in range(4):
                    st_ref[j, q:q + 1, :] = jnp.sum(acc_ref[j, 8 * q:8 * q + 8, :], axis=0, keepdims=True)
                st_ref[j, 4:8, :] = jnp.zeros((4, FB), F32)

    return _call(
        body, name=name, grid=(t // tm,),
        in_specs=[_brows(4, tm, FB), _bnext(4, 16, FB, tm, t), _brows(4, tm, FB), _bnext(4, 16, FB, tm, t),
                  _brows(4, tm, FB, 1), _bnext(4, 16, FB, tm, t, 1), _brows(4, tm, FB, 0), _const((4, 3, FB))],
        out_specs=[_brows(8, tm, FB), _const((4, 8, FB), single=False)],
        out_shape=[jax.ShapeDtypeStruct((8, t, FB), BF), jax.ShapeDtypeStruct((4, 8, FB), F32)],
        scratch_shapes=[pltpu.VMEM((4, 32, FB), F32)], args=[da, da, gc, gc, up, up, up, convw], carry=carry)


def wgrad(a, b, *, nblk, a_blocked, b_blocked, bk, bn, tt, name):
    t = a.shape[1] if a.ndim == 3 else a.shape[0]
    nt = t // tt

    def body(a_ref, b_ref, o_ref, acc_ref):
        s = pl.program_id(1)

        @pl.when(s == 0)
        def _():
            acc_ref[...] = jnp.zeros_like(acc_ref)

        av = a_ref[0] if a.ndim == 3 else a_ref[...]
        bv = b_ref[0] if b.ndim == 3 else b_ref[...]
        acc_ref[...] += _dot_tn(av, bv)

        @pl.when(s == nt - 1)
        def _():
            o_ref[0] = acc_ref[...].astype(BF)

    def spec(arr, blocked, width):
        if arr.ndim == 3:
            return pl.BlockSpec((1, tt, width), lambda j, s: (j, s, 0))
        if blocked:
            return pl.BlockSpec((tt, width), lambda j, s: (s, j))
        return pl.BlockSpec((tt, width), lambda j, s: (s, 0))

    return pl.pallas_call(
        body, name=name, grid=(nblk, nt), in_specs=[spec(a, a_blocked, bk), spec(b, b_blocked, bn)],
        out_specs=pl.BlockSpec((1, bk, bn), lambda j, s: (j, 0, 0)),
        out_shape=jax.ShapeDtypeStruct((nblk, bk, bn), BF),
        scratch_shapes=[pltpu.VMEM((bk, bn), F32)], compiler_params=_params(2))(a, b)


def mod_partial(c_all, w_mod):
    cols = w_mod.shape[2]

    def body(c_ref, w_ref, o_ref):
        c = c_ref[...]
        ca = c * _sigmoid(c)
        o_ref[0] = jnp.dot(ca, w_ref[0], preferred_element_type=F32, precision=lax.Precision.HIGHEST)

    return pl.pallas_call(
        body, name="mod_partial", grid=(DEPTH,),
        in_specs=[pl.BlockSpec((NDEV, D), lambda l: (0, 0)), pl.BlockSpec((1, D, cols), lambda l: (l, 0, 0))],
        out_specs=pl.BlockSpec((1, NDEV, cols), lambda l: (l, 0, 0)),
        out_shape=jax.ShapeDtypeStruct((DEPTH, NDEV, cols), F32), compiler_params=_params(1))(c_all, w_mod)


def mod_finish(parts, b_mod):
    cols = parts.shape[2]

    def body(p_ref, b_ref, o_ref):
        for e in range(NDEV):
            o_ref[:, e * cols:(e + 1) * cols] = p_ref[e] + b_ref[:, e * cols:(e + 1) * cols]

    return pl.pallas_call(
        body, name="mod_finish", out_shape=jax.ShapeDtypeStruct((DEPTH, NDEV * cols), F32))(parts, b_mod)


def sum_parts(parts):
    n, r, c = parts.shape

    def body(p_ref, o_ref):
        acc = p_ref[0]
        for j in range(1, n):
            acc = acc + p_ref[j]
        o_ref[...] = acc

    return pl.pallas_call(body, name="sum_parts", out_shape=jax.ShapeDtypeStruct((r, c), F32))(parts)


def mod_wgrad(c_all_t, gmod_cols):
    cols = gmod_cols.shape[2]

    def body(c_ref, g_ref, o_ref):
        c = c_ref[...]
        ca = c * _sigmoid(c)
        acc = ca[:, 0:1] * g_ref[0, 0:1, :]
        for b in range(1, NDEV):
            acc = acc + ca[:, b:b + 1] * g_ref[0, b:b + 1, :]
        o_ref[0] = acc

    return pl.pallas_call(
        body, name="mod_wgrad", grid=(DEPTH,),
        in_specs=[pl.BlockSpec((D, NDEV), lambda l: (0, 0)), pl.BlockSpec((1, NDEV, cols), lambda l: (l, 0, 0))],
        out_specs=pl.BlockSpec((1, D, cols), lambda l: (l, 0, 0)),
        out_shape=jax.ShapeDtypeStruct((DEPTH, D, cols), F32), compiler_params=_params(1))(c_all_t, gmod_cols)


def _adamw_math(g, w, m, v):
    m2 = B1 * m + (1.0 - B1) * g
    v2 = B2 * v + (1.0 - B2) * (g * g)
    m_hat = m2 / (1.0 - B1 ** STEP)
    v_hat = v2 / (1.0 - B2 ** STEP)
    delta = -LR * (m_hat / (jnp.sqrt(v_hat) + ADAM_EPS) + WD * w)
    return delta, m2, v2


def _row_tile(r, c, budget=1 << 18):
    if r * c <= budget or r % 8:
        return r
    best = 8
    for cand in range(8, r + 1, 8):
        if r % cand == 0 and cand * c <= budget:
            best = cand
    return best


def adamw_sum(parts, w, m, v, *, name):
    n, r, c = parts.shape
    tr = _row_tile(r, c)

    def body(p_ref, w_ref, m_ref, v_ref, g_ref, d_ref, m2_ref, v2_ref):
        g = p_ref[0].astype(F32)
        for j in range(1, n):
            g = g + p_ref[j].astype(F32)
        d, m2, v2 = _adamw_math(g, w_ref[...], m_ref[...], v_ref[...])
        g_ref[...] = g
        d_ref[...] = d
        m2_ref[...] = m2
        v2_ref[...] = v2

    blk = pl.BlockSpec((tr, c), lambda i: (i, 0))
    out = jax.ShapeDtypeStruct((r, c), F32)
    return pl.pallas_call(
        body, name=name, grid=(r // tr,), in_specs=[pl.BlockSpec((n, tr, c), lambda i: (0, i, 0)), blk, blk, blk],
        out_specs=[blk] * 4, out_shape=[out] * 4, compiler_params=_params(1))(parts, w, m, v)


def adamw_layer(parts, w, m, v, prev, layer, *, name):
    n, r, c = parts.shape
    nl = w.shape[0]
    tr = _row_tile(r, c)

    def body(p_ref, w_ref, m_ref, v_ref, *rest):
        g_ref, d_ref, m2_ref, v2_ref = rest[-4:]
        g = p_ref[0].astype(F32)
        for j in range(1, n):
            g = g + p_ref[j].astype(F32)
        d, m2, v2 = _adamw_math(g, w_ref[0], m_ref[0], v_ref[0])
        g_ref[0] = g
        d_ref[0] = d
        m2_ref[0] = m2
        v2_ref[0] = v2

    blk = pl.BlockSpec((1, tr, c), lambda i: (layer, i, 0))
    in_specs = [pl.BlockSpec((n, tr, c), lambda i: (0, i, 0)), blk, blk, blk]
    args = [parts, w, m, v]
    aliases = {}
    if prev is not None:
        in_specs += [ANY] * 4
        args += list(prev)
        aliases = {4 + k: k for k in range(4)}
    out = jax.ShapeDtypeStruct((nl, r, c), F32)
    return pl.pallas_call(
        body, name=name, grid=(r // tr,), in_specs=in_specs, out_specs=[blk] * 4, out_shape=[out] * 4,
        input_output_aliases=aliases, compiler_params=_params(1))(*args)


def _pack(arrays):
    flat, layout, off = [], [], 0
    for a in arrays:
        flat.append(a.reshape(-1))
        layout.append((off, a.shape))
        off += a.size
    pad = (-off) % 1024
    if pad:
        flat.append(jnp.zeros((pad,), F32))
    return jnp.concatenate(flat).reshape(-1, 128), layout


def _unpack(packed, layout, lead=()):
    flat = packed.reshape(lead + (-1,))
    return [flat[..., off:off + _size(shape)].reshape(lead + tuple(shape)) for off, shape in layout]


def _size(shape):
    n = 1
    for s in shape:
        n *= s
    return n


def _join_last(g):
    g = jnp.moveaxis(g, 0, -2)
    return g.reshape(g.shape[:-2] + (g.shape[-2] * g.shape[-1],))


def _my_cols(a, width):
    return lax.dynamic_slice_in_dim(a, _my_id() * width, width, axis=a.ndim - 1)


def _tile(t, pref):
    return min(pref, t)


def kernel(x, c, w_mod, b_mod, norm_g, sc_w_in, sc_conv, sc_w_out, pool_w, pool_b, pool_scale, cf_w_pw1, cf_b_pw1, cf_w_dw, cf_b_dw, cf_ln_g, cf_ln_b, cf_w_pw2, cf_b_pw2, ffn_w_up, ffn_conv, ffn_b_conv, ffn_w_down, loss_target, m_w_mod, m_b_mod, m_norm_g, m_sc_w_in, m_sc_conv, m_sc_w_out, m_pool_w, m_pool_b, m_pool_scale, m_cf_w_pw1, m_cf_b_pw1, m_cf_w_dw, m_cf_b_dw, m_cf_ln_g, m_cf_ln_b, m_cf_w_pw2, m_cf_b_pw2, m_ffn_w_up, m_ffn_conv, m_ffn_b_conv, m_ffn_w_down, v_w_mod, v_b_mod, v_norm_g, v_sc_w_in, v_sc_conv, v_sc_w_out, v_pool_w, v_pool_b, v_pool_scale, v_cf_w_pw1, v_cf_b_pw1, v_cf_w_dw, v_cf_b_dw, v_cf_ln_g, v_cf_ln_b, v_cf_w_pw2, v_cf_b_pw2, v_ffn_w_up, v_ffn_conv, v_ffn_b_conv, v_ffn_w_down):
    env = dict(locals())
    names = ["w_mod", "b_mod", "norm_g", "sc_w_in", "sc_conv", "sc_w_out", "pool_w", "pool_b", "pool_scale", "cf_w_pw1",
             "cf_b_pw1", "cf_w_dw", "cf_b_dw", "cf_ln_g", "cf_ln_b", "cf_w_pw2", "cf_b_pw2", "ffn_w_up", "ffn_conv",
             "ffn_b_conv", "ffn_w_down"]
    t = x.shape[1]
    tm = _tile(t, 512)
    tm_ffn = _tile(t, 256)
    tt = _tile(t, 2048)
    x0, target = x[0], loss_target[0]

    small_names = ["norm_g", "sc_conv", "cf_b_pw1", "cf_w_dw", "cf_b_dw", "cf_ln_g", "cf_ln_b", "cf_b_pw2", "ffn_conv"]
    packed, layout = _pack([c] + [env[n] for n in small_names])

    shard = {"pool": pool_w[0].astype(BF), "pw1": cf_w_pw1[0].astype(BF), "pw2": cf_w_pw2[0].astype(BF)}
    for j in range(2):
        shard[f"in{j}"], shard[f"out{j}"] = sc_w_in[j].astype(BF), sc_w_out[j].astype(BF)
    for l in range(DEPTH):
        shard[f"up{l}"], shard[f"down{l}"] = ffn_w_up[l].astype(BF), ffn_w_down[l].astype(BF)
    gathered, g_in0, g_out0 = _Exchange("gather", [packed, shard["in0"], shard["out0"]]).run("gather_first")
    wg = {"in0": g_in0, "out0": g_out0}
    parts = _unpack(gathered, layout, lead=(NDEV,))
    c_all = parts[0].reshape(NDEV, D)
    full = {n: _join_last(p) for n, p in zip(small_names, parts[1:])}
    fwd_plan = {("mix_in", 0): ["up0"], ("mix_out", 0): ["down0"], ("ffn_in", 0): ["pool", "up1"],
                ("ffn_out", 0): ["down1", "pw1", "pw2"], ("ffn_in", 1): ["up2"], ("ffn_out", 1): ["down2", "in1", "out1"],
                ("ffn_in", 2): ["up3"], ("ffn_out", 2): ["down3"]}

    def carrying(plan, kind, store, source, fn, key, *a, **k):
        names = plan.get(key)
        if not names:
            return fn(*a, **k)
        res = fn(*a, carry=_Exchange(kind, [source[n] for n in names]), **k)
        store.update(zip(names, res[-1]))
        return res[:-1]

    fwd = functools.partial(carrying, fwd_plan, "gather", wg, shard)

    mp = mod_partial(c_all, w_mod)
    (mod_parts,) = _Exchange("scatter", [jnp.swapaxes(mp, 0, 1)]).run("exchange_mod")
    mod = mod_finish(mod_parts, b_mod)

    def vec(a):
        return a.reshape(1, -1)

    def ffn_blocks(a):
        return jnp.swapaxes(a.reshape(a.shape[0], 4, FB), 0, 1)

    saved = []
    xs = x0
    for l in range(DEPTH):
        sh1, sc1, g1, sh2, sc2, g2 = [mod[l:l + 1, k * D:(k + 1) * D] for k in range(6)]
        ng = [full["norm_g"][l, k:k + 1] for k in range(4)]
        kind, j = l % 3, l // 3
        s = dict(x_in=xs, sc1=sc1, g1=g1, sc2=sc2, g2=g2, ng=ng)
        if kind == 0:
            s["h"], s["p"] = fwd(fwd_in, ("mix_in", l), xs, ng[0], sc1, sh1, wg[f"in{j}"], None, blocked=False, tm=tm,
                                 name=f"sc_in_{l}")
            x1, s["m"], s["q"] = fwd(sc_fwd_out, ("mix_out", l), s["p"], full["sc_conv"][j], wg[f"out{j}"].reshape(D, D), xs,
                                     ng[1], g1, tm=tm, name=f"sc_out_{l}")
        elif kind == 1:
            pool_w_f = jnp.swapaxes(wg["pool"], 0, 1).reshape(4, PG, PG)
            x1, s["m"], s["ypre"], s["pooled"] = pool_fwd(xs, ng[0], sc1, sh1, pool_w_f, pool_b, pool_scale, ng[1], g1,
                                                          tm=tm, name=f"pool_{l}")
        else:
            s["h"], s["a"] = fwd_in(xs, ng[0], sc1, sh1, wg["pw1"], full["cf_b_pw1"].reshape(NDEV, 1, 2 * D // NDEV),
                                    blocked=False, tm=tm, name=f"cf_in_{l}")
            x1, s["m"], s["s"], s["u2"] = cf_fwd_out(s["a"], full["cf_w_dw"][0], full["cf_b_dw"], full["cf_ln_g"],
                                                     full["cf_ln_b"], wg["pw2"].reshape(D, D), full["cf_b_pw2"], xs, ng[1],
                                                     g1, tm=tm, name=f"cf_out_{l}")
        s["x1"] = x1
        s["cw"] = ffn_blocks(full["ffn_conv"][l])
        s["h2"], s["up"] = fwd(fwd_in, ("ffn_in", l), x1, ng[2], sc2, sh2, wg[f"up{l}"], None, blocked=True, tm=tm,
                               name=f"ffn_in_{l}")
        xs, s["f"], s["gc"], s["fa"] = fwd(ffn_fwd_out, ("ffn_out", l), s["up"], s["cw"], ffn_blocks(ffn_b_conv[l:l + 1]),
                                           wg[f"down{l}"].reshape(F, D), x1, ng[3], g2, tm=tm_ffn, name=f"ffn_out_{l}")
        saved.append(s)

    dx, loss_part = loss_head(xs, target, tm=tm, name="loss_head")
    loss = lax.psum(loss_part[0, 0], ("x", "y", "c"))

    gmod = [None] * DEPTH
    d_norm_g = [None] * DEPTH
    d_ffn_conv = [None] * DEPTH
    d_ffn_b_conv = [None] * DEPTH
    d_sc_conv = [None] * 2
    big = {}
    got = {}
    small_g = {}
    bwd_plan = {("mix_bout", 3): ["down3"], ("mix_bmid", 3): ["up3"], ("ffn_bout", 2): ["in1", "out1"],
                ("mix_bmid", 2): ["up2", "down2"], ("ffn_bout", 1): ["pw1", "pw2"], ("ffn_bout", 0): ["pool", "down1"],
                ("ffn_bmid", 0): ["up1"], ("mix_bout", 0): ["down0"], ("mix_bmid", 0): ["up0"]}
    bwd = functools.partial(carrying, bwd_plan, "scatter", got, big)
    pool_w_f = jnp.swapaxes(wg["pool"], 0, 1).reshape(4, PG, PG)
    for l in reversed(range(DEPTH)):
        s = saved[l]
        ng = s["ng"]
        kind, j = l % 3, l // 3
        df, da, st_o = bwd(bwd_out, ("ffn_bout", l), dx, s["f"], ng[3], s["g2"], wg[f"down{l}"].reshape(F, D), blocked=True,
                           tm=tm, name=f"ffn_bout_{l}")
        dup, st_c = bwd(ffn_bwd_mid, ("ffn_bmid", l), da, s["gc"], s["up"], s["cw"], tm=tm_ffn, name=f"ffn_bmid_{l}")
        dx1, st_i = bwd_in([dup], wg[f"up{l}"], s["x1"], ng[2], s["sc2"], dx, tm=tm, name=f"ffn_bin_{l}")
        big[f"up{l}"] = wgrad(s["h2"], dup, nblk=NDEV, a_blocked=False, b_blocked=True, bk=D, bn=FB, tt=tt, name=f"ffn_wup_{l}")
        big[f"down{l}"] = wgrad(s["fa"], df, nblk=4, a_blocked=True, b_blocked=False, bk=FB, bn=D, tt=tt,
                                name=f"ffn_wdown_{l}").reshape(NDEV, F // NDEV, D)
        d_ffn_b_conv[l] = st_c[:, 0, :].reshape(F)
        d_ffn_conv[l] = jnp.swapaxes(st_c[:, 1:4, :], 0, 1).reshape(3, F)
        g_ffn = [st_i[0], st_i[1], st_o[0]]
        dn3, dn2 = st_o[1], st_i[2]
        if kind == 0:
            dm, dq, st_o = bwd(bwd_out, ("mix_bout", l), dx1, s["m"], ng[1], s["g1"], wg[f"out{j}"].reshape(D, D), blocked=False,
                               tm=tm, name=f"sc_bout_{l}")
            dp, st_c = bwd(sc_bwd_mid, ("mix_bmid", l), dq, s["p"], full["sc_conv"][j], tm=tm, name=f"sc_bmid_{l}")
            dx, st_i = bwd_in([dp], wg[f"in{j}"], s["x_in"], ng[0], s["sc1"], dx1, tm=tm, name=f"sc_bin_{l}")
            big[f"in{j}"] = wgrad(s["h"], dp, nblk=NDEV, a_blocked=False, b_blocked=True, bk=D, bn=3 * D // NDEV, tt=tt,
                                  name=f"sc_win_{l}")
            big[f"out{j}"] = wgrad(s["q"], dm, nblk=1, a_blocked=False, b_blocked=False, bk=D, bn=D, tt=tt,
                                   name=f"sc_wout_{l}").reshape(NDEV, D // NDEV, D)
            d_sc_conv[j] = st_c[0:3]
        elif kind == 1:
            dh, dyp, st_o = pool_bwd(dx1, s["m"], s["ypre"], pool_w_f, pool_scale, ng[1], s["g1"], tm=tm, name=f"pool_b_{l}")
            dx, st_i = bwd_in([dh], None, s["x_in"], ng[0], s["sc1"], dx1, tm=tm, name=f"pool_bin_{l}")
            dpw = wgrad(s["pooled"], dyp, nblk=4, a_blocked=True, b_blocked=True, bk=PG, bn=PG, tt=tt, name=f"pool_w_{l}")
            big["pool"] = jnp.swapaxes(dpw.reshape(4, NDEV, PG // NDEV, PG), 0, 1).reshape(NDEV, 4 * PG // NDEV, PG)
            small_g["pool_scale"], small_g["pool_b"] = st_o[2:3], st_o[3:4]
        else:
            dm, ds, st_o = bwd_out(dx1, s["m"], ng[1], s["g1"], wg["pw2"].reshape(D, D), blocked=False, tm=tm,
                                   name=f"cf_bout_{l}")
            dA, st_c = bwd(cf_bwd_mid, ("mix_bmid", l), ds, s["u2"], s["a"], full["cf_w_dw"][0], full["cf_ln_g"],
                           full["cf_ln_b"], tm=tm, name=f"cf_bmid_{l}")
            dx, st_i = bwd_in([dA], wg["pw1"], s["x_in"], ng[0], s["sc1"], dx1, tm=tm, name=f"cf_bin_{l}")
            big["pw1"] = wgrad(s["h"], dA, nblk=NDEV, a_blocked=False, b_blocked=True, bk=D, bn=2 * D // NDEV, tt=tt,
                               name=f"cf_wpw1_{l}")
            big["pw2"] = wgrad(s["s"], dm, nblk=1, a_blocked=False, b_blocked=False, bk=D, bn=D, tt=tt,
                               name=f"cf_wpw2_{l}").reshape(NDEV, D // NDEV, D)
            small_g["cf_w_dw"] = st_c[0:CFW][None]
            small_g["cf_b_dw"], small_g["cf_ln_g"], small_g["cf_ln_b"] = st_c[31:32], st_c[32:33], st_c[33:34]
            small_g["cf_b_pw1"] = st_c[34:36].reshape(1, 2 * D)
            small_g["cf_b_pw2"] = st_o[2:3]
        gmod[l] = jnp.concatenate([st_i[0], st_i[1], st_o[0]] + g_ffn)
        d_norm_g[l] = jnp.stack([st_i[2], st_o[1], dn2, dn3])

    small_g["gmod"] = jnp.stack(gmod)
    small_g["norm_g"] = jnp.stack(d_norm_g)
    small_g["sc_conv"] = jnp.stack(d_sc_conv)
    small_g["ffn_conv"] = jnp.stack(d_ffn_conv)
    small_g["ffn_b_conv"] = jnp.stack(d_ffn_b_conv)
    sg_names = ["gmod", "norm_g", "sc_conv", "pool_b", "pool_scale", "cf_b_pw1", "cf_w_dw", "cf_b_dw", "cf_ln_g", "cf_ln_b",
                "cf_b_pw2", "ffn_conv", "ffn_b_conv"]
    gpacked, glayout = _pack([small_g[n] for n in sg_names])
    (ggath,) = _Exchange("gather", [gpacked]).run("gather_small_grads")
    gsum = dict(zip(sg_names, _unpack(sum_parts(ggath), glayout)))
    gmod_all = _unpack(ggath, glayout[:1], lead=(NDEV,))[0]
    grads = {"b_mod": gsum["gmod"], "pool_b": gsum["pool_b"], "pool_scale": gsum["pool_scale"],
             "ffn_b_conv": gsum["ffn_b_conv"]}
    for n in ["norm_g", "sc_conv", "cf_b_pw1", "cf_w_dw", "cf_b_dw", "cf_ln_g", "cf_ln_b", "cf_b_pw2", "ffn_conv"]:
        grads[n] = _my_cols(gsum[n], env[n].shape[-1])
    grads["w_mod"] = mod_wgrad(c_all.T, jnp.swapaxes(_my_cols(gmod_all, w_mod.shape[2]), 0, 1))

    deltas, new_m, new_v = {}, {}, {}
    sp_names = ["b_mod", "norm_g", "sc_conv", "pool_b", "pool_scale", "cf_b_pw1", "cf_w_dw", "cf_b_dw", "cf_ln_g", "cf_ln_b",
                "cf_b_pw2", "ffn_conv", "ffn_b_conv"]
    pg, playout = _pack([grads[n] for n in sp_names])
    pw_, _ = _pack([env[n] for n in sp_names])
    pm_, _ = _pack([env["m_" + n] for n in sp_names])
    pv_, _ = _pack([env["v_" + n] for n in sp_names])
    _, sd, sm, sv = adamw_sum(pg[None], pw_, pm_, pv_, name="adamw_small")
    for n, d_, m_, v_ in zip(sp_names, _unpack(sd, playout), _unpack(sm, playout), _unpack(sv, playout)):
        deltas[n], new_m[n], new_v[n] = d_, m_, v_
    gw = grads["w_mod"].reshape(1, DEPTH * D, -1)
    _, d_, m_, v_ = adamw_sum(gw, w_mod.reshape(gw.shape[1:]), m_w_mod.reshape(gw.shape[1:]), v_w_mod.reshape(gw.shape[1:]),
                              name="adamw_w_mod")
    deltas["w_mod"], new_m["w_mod"], new_v["w_mod"] = [a.reshape(w_mod.shape) for a in (d_, m_, v_)]

    got["in0"], got["out0"] = _Exchange("scatter", [big["in0"], big["out0"]]).run("scatter_last")
    groups = {"sc_w_in": ["in0", "in1"], "sc_w_out": ["out0", "out1"], "pool_w": ["pool"], "cf_w_pw1": ["pw1"],
              "cf_w_pw2": ["pw2"], "ffn_w_up": [f"up{l}" for l in range(DEPTH)], "ffn_w_down": [f"down{l}" for l in range(DEPTH)]}
    for n, layers in groups.items():
        stacked = (len(layers),) + got[layers[0]].shape[1:]
        w3 = [env[p + n].reshape(stacked) for p in ("", "m_", "v_")]
        outs = None
        for li, key in enumerate(layers):
            outs = adamw_layer(got[key], *w3, outs, li, name=f"adamw_{n}_{li}")
        grads[n], deltas[n], new_m[n], new_v[n] = [a.reshape(env[n].shape) for a in outs]

    return (loss, dx[None], *[grads[n] for n in names], *[deltas[n] for n in names], *[new_m[n] for n in names],
            *[new_v[n] for n in names])
```

```python
import functools

import jax
import jax.numpy as jnp
from jax import lax
from jax.experimental import pallas as pl
from jax.experimental.pallas import tpu as pltpu

D = 1024
F = 2816
NDEV = 8
FB = F // 4
DEPTH = 4
RMS_EPS = 1e-6
LN_EPS = 1e-5
CFW = 31
POOL_WINDOWS = (2, 4, 8, 16)
PG = D // 4
LR, B1, B2, ADAM_EPS, WD, STEP = 0.001, 0.9, 0.999, 1e-08, 0.01, 10

BF = jnp.bfloat16
F32 = jnp.float32
VMEM_LIMIT_V7X = 56 * 1024 * 1024
MESH = pl.DeviceIdType.MESH
ANY = pl.BlockSpec(memory_space=pl.ANY)


def _params(n_axes):
    return pltpu.CompilerParams(dimension_semantics=("arbitrary",) * n_axes, vmem_limit_bytes=VMEM_LIMIT_V7X)


def _const(shape, single=True):
    nd = len(shape)
    if single:
        return pl.BlockSpec(shape, lambda *_: (0,) * nd, pipeline_mode=pl.Buffered(1))
    return pl.BlockSpec(shape, lambda *_: (0,) * nd)


def _rows(tm, c):
    return pl.BlockSpec((tm, c), lambda i: (i, 0))


def _brows(nb, tm, c, b0=0):
    return pl.BlockSpec((nb, tm, c), lambda i: (b0, i, 0))


def _prev(hb, c, tm):
    return pl.BlockSpec((hb, c), lambda i: (jnp.maximum(i * (tm // hb) - 1, 0), 0))


def _next(hb, c, tm, t):
    return pl.BlockSpec((hb, c), lambda i: (jnp.minimum((i + 1) * (tm // hb), t // hb - 1), 0))


def _bprev(nb, hb, c, tm, b0=0):
    return pl.BlockSpec((nb, hb, c), lambda i: (b0, jnp.maximum(i * (tm // hb) - 1, 0), 0))


def _bnext(nb, hb, c, tm, t, b0=0):
    return pl.BlockSpec((nb, hb, c), lambda i: (b0, jnp.minimum((i + 1) * (tm // hb), t // hb - 1), 0))


def _sigmoid(v):
    return 0.5 * jnp.tanh(0.5 * v) + 0.5


def _fold8(v):
    r, c = v.shape
    return jnp.sum(v.reshape(r // 8, 8, c), axis=0)


def _chunks(n_rows, rc, step, init=0, reverse=False):
    n = n_rows // rc

    def it(c, carry):
        idx = (n - 1 - c) if reverse else c
        return step(pl.multiple_of(idx * rc, rc), carry)

    return lax.fori_loop(0, n, it, init)


def _row_shifted_copies(s_ref, n):
    for b in range(1, 8):
        s_ref[b, 0:n, :] = s_ref[0, pl.ds(b, n), :]


def _shifted(s_ref, o, tm):
    return s_ref[o % 8, pl.ds(8 * (o // 8), tm), :]


def _dot(a, b):
    return jnp.dot(a, b, preferred_element_type=F32)


def _dot_nt(a, b):
    return lax.dot_general(a, b, (((1,), (1,)), ((), ())), preferred_element_type=F32)


def _dot_tn(a, b):
    return lax.dot_general(a, b, (((0,), (0,)), ((), ())), preferred_element_type=F32)


def _rsum(v):
    return jnp.sum(v, axis=0, keepdims=True)


def _adaln(x, g, sc, sh):
    r = lax.rsqrt(jnp.mean(x * x, axis=-1, keepdims=True) + RMS_EPS)
    return (x * r * g) * (1.0 + sc) + sh


def _gated_res(x, m, gn, gt):
    r = lax.rsqrt(jnp.mean(m * m, axis=-1, keepdims=True) + RMS_EPS)
    return x + gt * (m * r * gn)


def _gated_res_bwd(dxo, m, gn, gt):
    r = lax.rsqrt(jnp.mean(m * m, axis=-1, keepdims=True) + RMS_EPS)
    mh = m * r
    dgt = _rsum(dxo * (mh * gn))
    dn = dxo * gt
    dgn = _rsum(dn * mh)
    dmh = dn * gn
    dm = r * (dmh - mh * jnp.mean(dmh * mh, axis=-1, keepdims=True))
    return dm, dgt, dgn


def _my_id():
    return 4 * lax.axis_index("x") + 2 * lax.axis_index("y") + lax.axis_index("c")


def _peer(k):
    x, y, c = lax.axis_index("x"), lax.axis_index("y"), lax.axis_index("c")
    px = 1 - x if k & 4 else x
    py = 1 - y if k & 2 else y
    pc = 1 - c if k & 1 else c
    return (px, py, pc), 4 * px + 2 * py + pc


class _Exchange:
    def __init__(self, kind, arrays):
        self.gather = kind == "gather"
        self.arrays = list(arrays)
        n = len(self.arrays)
        if self.gather:
            self.out_shape = [jax.ShapeDtypeStruct((NDEV,) + a.shape, a.dtype) for a in self.arrays]
        else:
            self.out_shape = [jax.ShapeDtypeStruct(a.shape, a.dtype) for a in self.arrays]
        self.scratch = [pltpu.SemaphoreType.DMA((n * NDEV,)), pltpu.SemaphoreType.DMA((n * NDEV,)),
                        pltpu.SemaphoreType.DMA((n,))]

    def _local(self, a, src, dst, sems):
        me = _my_id()
        return pltpu.make_async_copy(src[a] if self.gather else src[a].at[me], dst[a].at[me], sems[2].at[a])

    def _remote(self, a, k, src, dst, sems, incoming):
        to, pid = _peer(k)
        me = _my_id()
        return pltpu.make_async_remote_copy(
            src_ref=src[a] if self.gather else src[a].at[pid], dst_ref=dst[a].at[pid if incoming else me],
            send_sem=sems[0].at[a * NDEV + k], recv_sem=sems[1].at[a * NDEV + k], device_id=to, device_id_type=MESH)

    def start(self, src, dst, sems):
        for a in range(len(self.arrays)):
            self._local(a, src, dst, sems).start()
        for k in range(1, NDEV):
            for a in range(len(self.arrays)):
                self._remote(a, k, src, dst, sems, False).start()

    def wait(self, src, dst, sems):
        for k in range(1, NDEV):
            for a in range(len(self.arrays)):
                self._remote(a, k, src, dst, sems, True).wait_recv()
        for k in range(1, NDEV):
            for a in range(len(self.arrays)):
                self._remote(a, k, src, dst, sems, False).wait_send()
        for a in range(len(self.arrays)):
            self._local(a, src, dst, sems).wait()

    def run(self, name):
        n = len(self.arrays)

        def body(*refs):
            src, dst, sems = refs[:n], refs[n:2 * n], refs[2 * n:]
            self.start(src, dst, sems)
            self.wait(src, dst, sems)

        return pl.pallas_call(body, name=name, in_specs=[ANY] * n, out_specs=[ANY] * n, out_shape=self.out_shape,
                              scratch_shapes=self.scratch)(*self.arrays)


def _call(body, *, name, grid, in_specs, out_specs, out_shape, args, scratch_shapes=(), carry=None):
    cp = _params(len(grid))
    if carry is None:
        return tuple(pl.pallas_call(body, name=name, grid=grid, in_specs=in_specs, out_specs=out_specs, out_shape=out_shape,
                                    scratch_shapes=list(scratch_shapes), compiler_params=cp)(*args))
    n_in, n_out, n_sc, n_c = len(in_specs), len(out_specs), len(scratch_shapes), len(carry.arrays)

    def wrapped(*refs):
        ins, src = refs[:n_in], refs[n_in:n_in + n_c]
        outs = refs[n_in + n_c:n_in + n_c + n_out]
        dst = refs[n_in + n_c + n_out:n_in + 2 * n_c + n_out]
        rest = refs[n_in + 2 * n_c + n_out:]
        scr, sems = rest[:n_sc], rest[n_sc:]
        first = pl.program_id(0) == 0
        last = pl.program_id(0) == grid[0] - 1
        for ax in range(1, len(grid)):
            first = jnp.logical_and(first, pl.program_id(ax) == 0)
            last = jnp.logical_and(last, pl.program_id(ax) == grid[ax] - 1)

        @pl.when(first)
        def _():
            carry.start(src, dst, sems)

        body(*ins, *outs, *scr)

        @pl.when(last)
        def _():
            carry.wait(src, dst, sems)

    res = pl.pallas_call(
        wrapped, name=name, grid=grid, in_specs=list(in_specs) + [ANY] * n_c, out_specs=list(out_specs) + [ANY] * n_c,
        out_shape=list(out_shape) + carry.out_shape, scratch_shapes=list(scratch_shapes) + carry.scratch,
        compiler_params=cp)(*args, *carry.arrays)
    return tuple(res[:n_out]) + (list(res[n_out:]),)


def fwd_in(x, g, sc, sh, w, bias, *, blocked, tm, name, carry=None):
    t = x.shape[0]
    nb, _, bw = w.shape

    def body(*refs):
        if bias is None:
            x_ref, g_ref, sc_ref, sh_ref, w_ref, h_ref, p_ref = refs
        else:
            x_ref, g_ref, sc_ref, sh_ref, w_ref, b_ref, h_ref, p_ref = refs
        hb = _adaln(x_ref[...], g_ref[...], sc_ref[...], sh_ref[...]).astype(BF)
        h_ref[...] = hb
        for d in range(nb):
            y = _dot(hb, w_ref[d])
            if bias is not None:
                y = y + b_ref[d]
            if blocked:
                p_ref[d] = y.astype(BF)
            else:
                p_ref[:, d * bw:(d + 1) * bw] = y.astype(BF)

    vec = _const((1, D))
    in_specs = [_rows(tm, D), vec, vec, vec, _const((nb, D, bw))]
    args = [x, g, sc, sh, w]
    if bias is not None:
        in_specs.append(_const((nb, 1, bw)))
        args.append(bias)
    if blocked:
        p_spec, p_shape = _brows(nb, tm, bw), jax.ShapeDtypeStruct((nb, t, bw), BF)
    else:
        p_spec, p_shape = _rows(tm, nb * bw), jax.ShapeDtypeStruct((t, nb * bw), BF)
    return _call(body, name=name, grid=(t // tm,), in_specs=in_specs, out_specs=[_rows(tm, D), p_spec],
                 out_shape=[jax.ShapeDtypeStruct((t, D), BF), p_shape], args=args, carry=carry)


def _conv3_from(s_ref, w, tm, lo):
    acc = w[0:1, :] * s_ref[pl.ds(lo, tm), :]
    for k in (1, 2):
        acc = acc + w[k:k + 1, :] * s_ref[pl.ds(lo + k, tm), :]
    return acc


def sc_fwd_out(p, convw, w_out, x, gn, gt, *, tm, name, carry=None):
    t = x.shape[0]

    def body(p_ref, ph_ref, cw_ref, w_ref, x_ref, gn_ref, gt_ref, x1_ref, m_ref, q_ref, s_ref):
        i = pl.program_id(0)
        zh = ph_ref[8:16, D:2 * D].astype(F32) * ph_ref[8:16, 2 * D:3 * D].astype(F32)
        s_ref[0:8, :] = jnp.where(i == 0, 0.0, zh)
        s_ref[8:8 + tm, :] = p_ref[:, D:2 * D].astype(F32) * p_ref[:, 2 * D:3 * D].astype(F32)
        u = _conv3_from(s_ref, cw_ref[...], tm, 6)
        qb = (p_ref[:, 0:D].astype(F32) * u).astype(BF)
        q_ref[...] = qb
        m = _dot(qb, w_ref[...])
        m_ref[...] = m.astype(BF)
        x1_ref[...] = _gated_res(x_ref[...], m, gn_ref[...], gt_ref[...])

    vec = _const((1, D))
    return _call(
        body, name=name, grid=(t // tm,),
        in_specs=[_rows(tm, 3 * D), _prev(16, 3 * D, tm), _const((3, D)), _const((D, D)), _rows(tm, D), vec, vec],
        out_specs=[_rows(tm, D)] * 3,
        out_shape=[jax.ShapeDtypeStruct((t, D), F32), jax.ShapeDtypeStruct((t, D), BF), jax.ShapeDtypeStruct((t, D), BF)],
        scratch_shapes=[pltpu.VMEM((tm + 8, D), F32)], args=[p, p, convw, w_out, x, gn, gt], carry=carry)


def _layernorm_parts(u2):
    mu = jnp.mean(u2, axis=-1, keepdims=True)
    cen = u2 - mu
    rstd = lax.rsqrt(jnp.mean(cen * cen, axis=-1, keepdims=True) + LN_EPS)
    return cen * rstd, rstd


def cf_fwd_out(a, w_dw, b_dw, ln_g, ln_b, w_pw2, b_pw2, x, gn, gt, *, tm, name):
    t = x.shape[0]
    hb = 32

    def body(a_ref, ah_ref, wd_ref, bd_ref, lg_ref, lb_ref, w_ref, b2_ref, x_ref, gn_ref, gt_ref,
             x1_ref, m_ref, s_out_ref, u2_ref, s_ref):
        i = pl.program_id(0)
        uh = ah_ref[:, 0:D].astype(F32) * _sigmoid(ah_ref[:, D:2 * D].astype(F32))
        s_ref[0, 0:hb, :] = jnp.where(i == 0, 0.0, uh)
        s_ref[0, hb:hb + tm, :] = a_ref[:, 0:D].astype(F32) * _sigmoid(a_ref[:, D:2 * D].astype(F32))
        _row_shifted_copies(s_ref, tm + hb - 8)
        acc = bd_ref[...] + wd_ref[0:1, :] * _shifted(s_ref, hb - CFW + 1, tm)
        for k in range(1, CFW):
            acc = acc + wd_ref[k:k + 1, :] * _shifted(s_ref, hb - CFW + 1 + k, tm)
        u2_ref[...] = acc.astype(BF)
        xh, _ = _layernorm_parts(acc)
        l = xh * lg_ref[...] + lb_ref[...]
        sb = (l * _sigmoid(l)).astype(BF)
        s_out_ref[...] = sb
        m = _dot(sb, w_ref[...]) + b2_ref[...]
        m_ref[...] = m.astype(BF)
        x1_ref[...] = _gated_res(x_ref[...], m, gn_ref[...], gt_ref[...])

    vec = _const((1, D))
    return pl.pallas_call(
        body, name=name, grid=(t // tm,),
        in_specs=[_rows(tm, 2 * D), _prev(hb, 2 * D, tm), _const((CFW, D)), vec, vec, vec, _const((D, D)), vec,
                  _rows(tm, D), vec, vec],
        out_specs=[_rows(tm, D)] * 4,
        out_shape=[jax.ShapeDtypeStruct((t, D), F32)] + [jax.ShapeDtypeStruct((t, D), BF)] * 3,
        scratch_shapes=[pltpu.VMEM((8, tm + hb, D), F32)], compiler_params=_params(1),
    )(a, a, w_dw, b_dw, ln_g, ln_b, w_pw2, b_pw2, x, gn, gt)


def _pool_counts(i, tm, w):
    row = lax.broadcasted_iota(jnp.int32, (tm, 1), 0) + i * tm
    return jnp.minimum(row + 1, w).astype(F32)


def pool_fwd(x, g, sc, sh, pw, pb, pscale, gn, gt, *, tm, name):
    t = x.shape[0]
    pad, hb = 8, 16
    base = pad + hb

    def body(x_ref, xh_ref, g_ref, sc_ref, sh_ref, pw_ref, pb_ref, ps_ref, gn_ref, gt_ref,
             x1_ref, m_ref, yp_ref, po_ref, sa_ref, sb_ref):
        i = pl.program_id(0)
        hh = _adaln(xh_ref[...], g_ref[...], sc_ref[...], sh_ref[...])
        h = _adaln(x_ref[...], g_ref[...], sc_ref[...], sh_ref[...])
        zero = jnp.zeros((pad, D), F32)
        sa_ref[0:pad, :] = zero
        sb_ref[0:pad, :] = zero
        sa_ref[pad:base, :] = jnp.where(i == 0, 0.0, hh)
        sa_ref[base:base + tm, :] = h
        n = hb + tm
        src, dst = sa_ref, sb_ref
        ys = []
        for gi, w in enumerate(POOL_WINDOWS):
            c0 = gi * PG
            step = w // 2
            dst[pl.ds(pad, n), c0:D] = src[pl.ds(pad, n), c0:D] + src[pl.ds(pad - step, n), c0:D]
            mean = dst[pl.ds(base, tm), c0:c0 + PG] / _pool_counts(i, tm, w)
            pooled = (mean - h[:, c0:c0 + PG]).astype(BF)
            po_ref[:, c0:c0 + PG] = pooled
            ys.append(_dot(pooled, pw_ref[gi]))
            src, dst = dst, src
        ypre = jnp.concatenate(ys, axis=1) + pb_ref[...]
        yp_ref[...] = ypre.astype(BF)
        m = ypre * ps_ref[...]
        m_ref[...] = m.astype(BF)
        x1_ref[...] = _gated_res(x_ref[...], m, gn_ref[...], gt_ref[...])

    vec = _const((1, D))
    return pl.pallas_call(
        body, name=name, grid=(t // tm,),
        in_specs=[_rows(tm, D), _prev(hb, D, tm), vec, vec, vec, _const((4, PG, PG)), vec, vec, vec, vec],
        out_specs=[_rows(tm, D)] * 4,
        out_shape=[jax.ShapeDtypeStruct((t, D), F32)] + [jax.ShapeDtypeStruct((t, D), BF)] * 3,
        scratch_shapes=[pltpu.VMEM((tm + base, D), F32)] * 2, compiler_params=_params(1),
    )(x, x, g, sc, sh, pw, pb, pscale, gn, gt)


def ffn_fwd_out(up, convw, convb, w_down, x, gn, gt, *, tm, name, carry=None):
    t = x.shape[0]

    def body(gate_ref, gh_ref, val_ref, cw_ref, cb_ref, w_ref, x_ref, gn_ref, gt_ref,
             x2_ref, f_ref, gc_ref, a_ref, s_ref):
        i = pl.program_id(0)
        acc = jnp.zeros((tm, D), F32)
        for j in range(4):
            sj = s_ref.at[j]
            sj[0:8, :] = jnp.where(i == 0, 0.0, gh_ref[j, 8:16, :].astype(F32))
            sj[8:8 + tm, :] = gate_ref[j].astype(F32)
            gc = cb_ref[j] + _conv3_from(sj, cw_ref[j], tm, 6)
            gc_ref[j] = gc.astype(BF)
            ab = (gc * _sigmoid(gc) * val_ref[j].astype(F32)).astype(BF)
            a_ref[j] = ab
            acc = acc + _dot(ab, w_ref[j * FB:(j + 1) * FB, :])
        f_ref[...] = acc.astype(BF)
        x2_ref[...] = _gated_res(x_ref[...], acc, gn_ref[...], gt_ref[...])

    vec = _const((1, D))
    blk = jax.ShapeDtypeStruct((4, t, FB), BF)
    return _call(
        body, name=name, grid=(t // tm,),
        in_specs=[_brows(4, tm, FB, 0), _bprev(4, 16, FB, tm, 0), _brows(4, tm, FB, 1), _const((4, 3, FB)),
                  _const((4, 1, FB)), _const((F, D)), _rows(tm, D), vec, vec],
        out_specs=[_rows(tm, D), _rows(tm, D), _brows(4, tm, FB), _brows(4, tm, FB)],
        out_shape=[jax.ShapeDtypeStruct((t, D), F32), jax.ShapeDtypeStruct((t, D), BF), blk, blk],
        scratch_shapes=[pltpu.VMEM((4, tm + 8, FB), F32)], args=[up, up, up, convw, convb, w_down, x, gn, gt], carry=carry)


def loss_head(y, target, *, tm, name):
    t = y.shape[0]

    def body(y_ref, t_ref, dy_ref, l_ref, acc_ref):
        i = pl.program_id(0)

        @pl.when(i == 0)
        def _():
            acc_ref[...] = jnp.zeros_like(acc_ref)

        e = y_ref[...] - t_ref[...]
        dy_ref[...] = e * (1.0 / D)
        acc_ref[...] += _rsum(e * e)

        @pl.when(i == pl.num_programs(0) - 1)
        def _():
            l_ref[...] = jnp.sum(acc_ref[...], axis=1, keepdims=True) * (0.5 / D)

    return pl.pallas_call(
        body, name=name, grid=(t // tm,), in_specs=[_rows(tm, D), _rows(tm, D)],
        out_specs=[_rows(tm, D), pl.BlockSpec((1, 1), lambda i: (0, 0))],
        out_shape=[jax.ShapeDtypeStruct((t, D), F32), jax.ShapeDtypeStruct((1, 1), F32)],
        scratch_shapes=[pltpu.VMEM((1, D), F32)], compiler_params=_params(1))(y, target)


def _init_stats(ref):
    @pl.when(pl.program_id(0) == 0)
    def _():
        ref[...] = jnp.zeros_like(ref)


def bwd_out(dxo, m, gn, gt, w, *, blocked, tm, name, carry=None):
    t = dxo.shape[0]
    k = w.shape[0]

    def body(dx_ref, m_ref, gn_ref, gt_ref, w_ref, dm_ref, da_ref, st_ref):
        _init_stats(st_ref)
        dm, dgt, dgn = _gated_res_bwd(dx_ref[...], m_ref[...].astype(F32), gn_ref[...], gt_ref[...])
        st_ref[0:1, :] += dgt
        st_ref[1:2, :] += dgn
        st_ref[2:3, :] += _rsum(dm)
        dmb = dm.astype(BF)
        dm_ref[...] = dmb
        if blocked:
            for j in range(4):
                da_ref[j] = _dot_nt(dmb, w_ref[j * FB:(j + 1) * FB, :]).astype(BF)
        else:
            da_ref[...] = _dot_nt(dmb, w_ref[...]).astype(BF)

    vec = _const((1, D))
    if blocked:
        da_spec, da_shape = _brows(4, tm, FB), jax.ShapeDtypeStruct((4, t, FB), BF)
    else:
        da_spec, da_shape = _rows(tm, k), jax.ShapeDtypeStruct((t, k), BF)
    return _call(
        body, name=name, grid=(t // tm,), in_specs=[_rows(tm, D), _rows(tm, D), vec, vec, _const((k, D))],
        out_specs=[_rows(tm, D), da_spec, _const((8, D), single=False)],
        out_shape=[jax.ShapeDtypeStruct((t, D), BF), da_shape, jax.ShapeDtypeStruct((8, D), F32)],
        args=[dxo, m, gn, gt, w], carry=carry)


def bwd_in(dps, w, x, g, sc, dxo, *, tm, name):
    t = x.shape[0]
    direct = w is None
    if not direct:
        nb, _, bw = w.shape
    natural = (not direct) and dps[0].ndim == 2

    def body(*refs):
        n = len(dps)
        dp_refs = refs[:n]
        if direct:
            x_ref, g_ref, sc_ref, dxo_ref, dx_ref, st_ref = refs[n:]
            dh = dp_refs[0][...]
        else:
            w_ref, x_ref, g_ref, sc_ref, dxo_ref, dx_ref, st_ref = refs[n:]
            dh = jnp.zeros((tm, D), F32)
            if natural:
                for d in range(nb):
                    dh = dh + _dot_nt(dp_refs[0][:, d * bw:(d + 1) * bw], w_ref[d])
            else:
                d = 0
                for r in dp_refs:
                    for j in range(r.shape[0]):
                        dh = dh + _dot_nt(r[j], w_ref[d])
                        d += 1
        _init_stats(st_ref)
        x = x_ref[...]
        r = lax.rsqrt(jnp.mean(x * x, axis=-1, keepdims=True) + RMS_EPS)
        xh = x * r
        gv = g_ref[...]
        st_ref[0:1, :] += _rsum(dh)
        st_ref[1:2, :] += _rsum(dh * (xh * gv))
        dn = dh * (1.0 + sc_ref[...])
        st_ref[2:3, :] += _rsum(dn * xh)
        dy = dn * gv
        dx_ref[...] = dxo_ref[...] + r * (dy - xh * jnp.mean(dy * xh, axis=-1, keepdims=True))

    vec = _const((1, D))
    if direct:
        dp_specs = [_rows(tm, D)]
    elif natural:
        dp_specs = [_rows(tm, nb * bw)]
    else:
        dp_specs = [_brows(a.shape[0], tm, bw) for a in dps]
    w_specs, w_args = ([], []) if direct else ([_const((nb, D, bw))], [w])
    return pl.pallas_call(
        body, name=name, grid=(t // tm,),
        in_specs=dp_specs + w_specs + [_rows(tm, D), vec, vec, _rows(tm, D)],
        out_specs=[_rows(tm, D), _const((8, D), single=False)],
        out_shape=[jax.ShapeDtypeStruct((t, D), F32), jax.ShapeDtypeStruct((8, D), F32)],
        compiler_params=_params(1))(*dps, *w_args, x, g, sc, dxo)


def sc_bwd_mid(dq, p, convw, *, tm, name, carry=None):
    t = dq.shape[0]

    def body(dq_ref, dqn_ref, p_ref, pp_ref, pn_ref, cw_ref, dp_ref, st_ref, s1_ref, s2_ref):
        i = pl.program_id(0)
        last = i == pl.num_programs(0) - 1
        _init_stats(st_ref)
        cw = cw_ref[...]
        cg = p_ref[:, D:2 * D].astype(F32)
        hi = p_ref[:, 2 * D:3 * D].astype(F32)
        bg = p_ref[:, 0:D].astype(F32)
        zp = pp_ref[8:16, D:2 * D].astype(F32) * pp_ref[8:16, 2 * D:3 * D].astype(F32)
        s2_ref[0:8, :] = jnp.where(i == 0, 0.0, zp)
        s2_ref[8:8 + tm, :] = cg * hi
        u = _conv3_from(s2_ref, cw, tm, 6)
        dqf = dq_ref[...].astype(F32)
        dp_ref[:, 0:D] = (dqf * u).astype(BF)
        du = dqf * bg
        duh = dqn_ref[0:8, :].astype(F32) * pn_ref[0:8, 0:D].astype(F32)
        s1_ref[0:tm, :] = du
        s1_ref[tm:tm + 8, :] = jnp.where(last, 0.0, duh)
        dz = cw[2:3, :] * du + cw[1:2, :] * s1_ref[pl.ds(1, tm), :] + cw[0:1, :] * s1_ref[pl.ds(2, tm), :]
        dp_ref[:, D:2 * D] = (dz * hi).astype(BF)
        dp_ref[:, 2 * D:3 * D] = (dz * cg).astype(BF)
        for k in range(3):
            st_ref[k:k + 1, :] += _rsum(du * s2_ref[pl.ds(6 + k, tm), :])

    return _call(
        body, name=name, grid=(t // tm,),
        in_specs=[_rows(tm, D), _next(16, D, tm, t), _rows(tm, 3 * D), _prev(16, 3 * D, tm), _next(16, 3 * D, tm, t),
                  _const((3, D))],
        out_specs=[_rows(tm, 3 * D), _const((8, D), single=False)],
        out_shape=[jax.ShapeDtypeStruct((t, 3 * D), BF), jax.ShapeDtypeStruct((8, D), F32)],
        scratch_shapes=[pltpu.VMEM((tm + 8, D), F32)] * 2, args=[dq, dq, p, p, p, convw], carry=carry)


def cf_bwd_mid(ds, u2, a, w_dw, ln_g, ln_b, *, tm, name, carry=None):
    t = ds.shape[0]
    hb = 32

    def du2_of(dsv, u2v, lg, lb):
        xh, rstd = _layernorm_parts(u2v)
        l = xh * lg + lb
        sg = _sigmoid(l)
        dl = dsv * (sg * (1.0 + l * (1.0 - sg)))
        dxh = dl * lg
        du2 = rstd * (dxh - jnp.mean(dxh, axis=-1, keepdims=True) - xh * jnp.mean(dxh * xh, axis=-1, keepdims=True))
        return du2, dl, xh

    rc, cc = 32, 256

    def body(ds_ref, dsn_ref, u2_ref, u2n_ref, a_ref, wd_ref, lg_ref, lb_ref, da_ref, st_ref, s1_ref, acc_ref):
        i = pl.program_id(0)
        last = i == pl.num_programs(0) - 1
        _init_stats(st_ref)
        _init_stats(acc_ref)
        lg, lb = lg_ref[...], lb_ref[...]
        du2, dl, xh = du2_of(ds_ref[...].astype(F32), u2_ref[...].astype(F32), lg, lb)
        st_ref[32:33, :] += _rsum(dl * xh)
        st_ref[33:34, :] += _rsum(dl)
        st_ref[31:32, :] += _rsum(du2)
        du2n, _, _ = du2_of(dsn_ref[...].astype(F32), u2n_ref[...].astype(F32), lg, lb)
        s1_ref[0, 0:tm, :] = du2
        s1_ref[0, tm:tm + hb, :] = jnp.where(last, 0.0, du2n)
        _row_shifted_copies(s1_ref, tm + hb - 8)

        def taps(r0, _):
            rows = pl.ds(r0, rc)
            for c0 in range(0, D, cc):
                sg = _sigmoid(a_ref[rows, D + c0:D + c0 + cc].astype(F32))
                u = a_ref[rows, c0:c0 + cc].astype(F32) * sg
                du = jnp.zeros((rc, cc), F32)
                for k in range(CFW):
                    o = CFW - 1 - k
                    sh = s1_ref[o % 8, pl.ds(pl.multiple_of(r0 + 8 * (o // 8), 8), rc), c0:c0 + cc]
                    du = du + wd_ref[k:k + 1, c0:c0 + cc] * sh
                    acc_ref[8 * k:8 * k + 8, c0:c0 + cc] += _fold8(u * sh)
                dav = du * sg
                dgv = du * u * (1.0 - sg)
                acc_ref[8 * CFW:8 * CFW + 8, c0:c0 + cc] += _fold8(dav)
                acc_ref[8 * CFW + 8:8 * CFW + 16, c0:c0 + cc] += _fold8(dgv)
                da_ref[rows, c0:c0 + cc] = dav.astype(BF)
                da_ref[rows, D + c0:D + c0 + cc] = dgv.astype(BF)
            return 0

        _chunks(tm, rc, taps)

        @pl.when(last)
        def _():
            for k in range(CFW):
                st_ref[k:k + 1, :] = jnp.sum(acc_ref[8 * k:8 * k + 8, :], axis=0, keepdims=True)
            st_ref[34:35, :] = jnp.sum(acc_ref[8 * CFW:8 * CFW + 8, :], axis=0, keepdims=True)
            st_ref[35:36, :] = jnp.sum(acc_ref[8 * CFW + 8:8 * CFW + 16, :], axis=0, keepdims=True)

    vec = _const((1, D))
    return _call(
        body, name=name, grid=(t // tm,),
        in_specs=[_rows(tm, D), _next(hb, D, tm, t), _rows(tm, D), _next(hb, D, tm, t), _rows(tm, 2 * D),
                  _const((CFW, D)), vec, vec],
        out_specs=[_rows(tm, 2 * D), _const((40, D), single=False)],
        out_shape=[jax.ShapeDtypeStruct((t, 2 * D), BF), jax.ShapeDtypeStruct((40, D), F32)],
        scratch_shapes=[pltpu.VMEM((8, tm + hb, D), F32), pltpu.VMEM((8 * (CFW + 2), D), F32)],
        args=[ds, ds, u2, u2, a, w_dw, ln_g, ln_b], carry=carry)


def pool_bwd(dxo, m, ypre, pw, pscale, gn, gt, *, tm, name):
    t = dxo.shape[0]
    hb = 16

    def dyp_of(dxv, mv, ypv, ps, gnv, gtv):
        dm, dgt, dgn = _gated_res_bwd(dxv, mv, gnv, gtv)
        return dm * ps, dgt, dgn, _rsum(dm * ypv)

    def body(dx_ref, dxn_ref, m_ref, mn_ref, yp_ref, ypn_ref, pw_ref, ps_ref, gn_ref, gt_ref,
             dh_ref, dyp_ref, st_ref, sa_ref, sb_ref):
        i = pl.program_id(0)
        last = i == pl.num_programs(0) - 1
        _init_stats(st_ref)
        ps, gnv, gtv = ps_ref[...], gn_ref[...], gt_ref[...]
        dyp, dgt, dgn, dps = dyp_of(dx_ref[...], m_ref[...].astype(F32), yp_ref[...].astype(F32), ps, gnv, gtv)
        st_ref[0:1, :] += dgt
        st_ref[1:2, :] += dgn
        st_ref[2:3, :] += dps
        st_ref[3:4, :] += _rsum(dyp)
        dypb = dyp.astype(BF)
        dyp_ref[...] = dypb
        dypn, _, _, _ = dyp_of(dxn_ref[...], mn_ref[...].astype(F32), ypn_ref[...].astype(F32), ps, gnv, gtv)
        dypnb = jnp.where(last, 0.0, dypn).astype(BF)
        dpo = []
        for gi, w in enumerate(POOL_WINDOWS):
            c0 = gi * PG
            dp_main = _dot_nt(dypb[:, c0:c0 + PG], pw_ref[gi])
            dp_next = _dot_nt(dypnb[:, c0:c0 + PG], pw_ref[gi])
            dpo.append(dp_main)
            sa_ref[0:tm, c0:c0 + PG] = dp_main / _pool_counts(i, tm, w)
            sa_ref[tm:tm + hb, c0:c0 + PG] = dp_next / float(w)
        zero = jnp.zeros((8, D), F32)
        sa_ref[tm + hb:tm + hb + 8, :] = zero
        sb_ref[tm + hb:tm + hb + 8, :] = zero
        n = tm + hb
        src, dst = sa_ref, sb_ref
        for gi, w in enumerate(POOL_WINDOWS):
            c0 = gi * PG
            step = w // 2
            dst[pl.ds(0, n), c0:D] = src[pl.ds(0, n), c0:D] + src[pl.ds(step, n), c0:D]
            dh_ref[:, c0:c0 + PG] = dst[pl.ds(0, tm), c0:c0 + PG] - dpo[gi]
            src, dst = dst, src

    vec = _const((1, D))
    return pl.pallas_call(
        body, name=name, grid=(t // tm,),
        in_specs=[_rows(tm, D), _next(hb, D, tm, t), _rows(tm, D), _next(hb, D, tm, t), _rows(tm, D),
                  _next(hb, D, tm, t), _const((4, PG, PG)), vec, vec, vec],
        out_specs=[_rows(tm, D), _rows(tm, D), _const((8, D), single=False)],
        out_shape=[jax.ShapeDtypeStruct((t, D), F32), jax.ShapeDtypeStruct((t, D), BF), jax.ShapeDtypeStruct((8, D), F32)],
        scratch_shapes=[pltpu.VMEM((tm + hb + 8, D), F32)] * 2, compiler_params=_params(1),
    )(dxo, dxo, m, m, ypre, ypre, pw, pscale, gn, gt)


def ffn_bwd_mid(da, gc, up, convw, *, tm, name, carry=None):
    t = da.shape[1]

    def dgc_of(dav, gcv, valv):
        sg = _sigmoid(gcv)
        return dav * valv * (sg * (1.0 + gcv * (1.0 - sg))), dav * (gcv * sg)

    rc = 16

    def body(da_ref, dan_ref, gc_ref, gcn_ref, val_ref, valn_ref, gate_ref, cw_ref, dup_ref, st_ref, acc_ref):
        i = pl.program_id(0)
        last = i == pl.num_programs(0) - 1
        _init_stats(acc_ref)
        for j in range(4):
            w0, w1, w2 = cw_ref[j, 0:1, :], cw_ref[j, 1:2, :], cw_ref[j, 2:3, :]
            dgcn, _ = dgc_of(dan_ref[j, 0:8, :].astype(F32), gcn_ref[j, 0:8, :].astype(F32), valn_ref[j, 0:8, :].astype(F32))
            below = jnp.where(last, 0.0, dgcn)

            def step(r0, carry):
                rows = pl.ds(r0, rc)
                dgc, dval = dgc_of(da_ref[j, rows, :].astype(F32), gc_ref[j, rows, :].astype(F32), val_ref[j, rows, :].astype(F32))
                dup_ref[4 + j, rows, :] = dval.astype(BF)
                ext = jnp.concatenate([dgc, carry], axis=0)
                e1, e2 = ext[1:rc + 1], ext[2:rc + 2]
                dup_ref[j, rows, :] = (w2 * dgc + w1 * e1 + w0 * e2).astype(BF)
                gate = gate_ref[j, rows, :].astype(F32)
                acc_ref[j, 0:8, :] += _fold8(dgc)
                acc_ref[j, 8:16, :] += _fold8(gate * e2)
                acc_ref[j, 16:24, :] += _fold8(gate * e1)
                acc_ref[j, 24:32, :] += _fold8(gate * dgc)
                return dgc[0:8]

            _chunks(tm, rc, step, below, reverse=True)

        @pl.when(last)
        def _():
            for j in range(4):
                for q in range(4):
                    st_ref[j, q:q + 1, :] = jnp.sum(acc_ref[j, 8 * q:8 * q + 8, :], axis=0, keepdims=True)
                st_ref[j, 4:8, :] = jnp.zeros((4, FB), F32)

    return _call(
        body, name=name, grid=(t // tm,),
        in_specs=[_brows(4, tm, FB), _bnext(4, 16, FB, tm, t), _brows(4, tm, FB), _bnext(4, 16, FB, tm, t),
                  _brows(4, tm, FB, 1), _bnext(4, 16, FB, tm, t, 1), _brows(4, tm, FB, 0), _const((4, 3, FB))],
        out_specs=[_brows(8, tm, FB), _const((4, 8, FB), single=False)],
        out_shape=[jax.ShapeDtypeStruct((8, t, FB), BF), jax.ShapeDtypeStruct((4, 8, FB), F32)],
        scratch_shapes=[pltpu.VMEM((4, 32, FB), F32)], args=[da, da, gc, gc, up, up, up, convw], carry=carry)


def wgrad(a, b, *, nblk, a_blocked, b_blocked, bk, bn, tt, name):
    t = a.shape[1] if a.ndim == 3 else a.shape[0]
    nt = t // tt

    def body(a_ref, b_ref, o_ref, acc_ref):
        s = pl.program_id(1)

        @pl.when(s == 0)
        def _():
            acc_ref[...] = jnp.zeros_like(acc_ref)

        av = a_ref[0] if a.ndim == 3 else a_ref[...]
        bv = b_ref[0] if b.ndim == 3 else b_ref[...]
        acc_ref[...] += _dot_tn(av, bv)

        @pl.when(s == nt - 1)
        def _():
            o_ref[0] = acc_ref[...].astype(BF)

    def spec(arr, blocked, width):
        if arr.ndim == 3:
            return pl.BlockSpec((1, tt, width), lambda j, s: (j, s, 0))
        if blocked:
            return pl.BlockSpec((tt, width), lambda j, s: (s, j))
        return pl.BlockSpec((tt, width), lambda j, s: (s, 0))

    return pl.pallas_call(
        body, name=name, grid=(nblk, nt), in_specs=[spec(a, a_blocked, bk), spec(b, b_blocked, bn)],
        out_specs=pl.BlockSpec((1, bk, bn), lambda j, s: (j, 0, 0)),
        out_shape=jax.ShapeDtypeStruct((nblk, bk, bn), BF),
        scratch_shapes=[pltpu.VMEM((bk, bn), F32)], compiler_params=_params(2))(a, b)


def mod_partial(c_all, w_mod):
    cols = w_mod.shape[2]

    def body(c_ref, w_ref, o_ref):
        c = c_ref[...]
        ca = c * _sigmoid(c)
        o_ref[0] = jnp.dot(ca, w_ref[0], preferred_element_type=F32, precision=lax.Precision.HIGHEST)

    return pl.pallas_call(
        body, name="mod_partial", grid=(DEPTH,),
        in_specs=[pl.BlockSpec((NDEV, D), lambda l: (0, 0)), pl.BlockSpec((1, D, cols), lambda l: (l, 0, 0))],
        out_specs=pl.BlockSpec((1, NDEV, cols), lambda l: (l, 0, 0)),
        out_shape=jax.ShapeDtypeStruct((DEPTH, NDEV, cols), F32), compiler_params=_params(1))(c_all, w_mod)


def mod_finish(parts, b_mod):
    cols = parts.shape[2]

    def body(p_ref, b_ref, o_ref):
        for e in range(NDEV):
            o_ref[:, e * cols:(e + 1) * cols] = p_ref[e] + b_ref[:, e * cols:(e + 1) * cols]

    return pl.pallas_call(
        body, name="mod_finish", out_shape=jax.ShapeDtypeStruct((DEPTH, NDEV * cols), F32))(parts, b_mod)


def sum_parts(parts):
    n, r, c = parts.shape

    def body(p_ref, o_ref):
        acc = p_ref[0]
        for j in range(1, n):
            acc = acc + p_ref[j]
        o_ref[...] = acc

    return pl.pallas_call(body, name="sum_parts", out_shape=jax.ShapeDtypeStruct((r, c), F32))(parts)


def mod_wgrad(c_all_t, gmod_cols):
    cols = gmod_cols.shape[2]

    def body(c_ref, g_ref, o_ref):
        c = c_ref[...]
        ca = c * _sigmoid(c)
        acc = ca[:, 0:1] * g_ref[0, 0:1, :]
        for b in range(1, NDEV):
            acc = acc + ca[:, b:b + 1] * g_ref[0, b:b + 1, :]
        o_ref[0] = acc

    return pl.pallas_call(
        body, name="mod_wgrad", grid=(DEPTH,),
        in_specs=[pl.BlockSpec((D, NDEV), lambda l: (0, 0)), pl.BlockSpec((1, NDEV, cols), lambda l: (l, 0, 0))],
        out_specs=pl.BlockSpec((1, D, cols), lambda l: (l, 0, 0)),
        out_shape=jax.ShapeDtypeStruct((DEPTH, D, cols), F32), compiler_params=_params(1))(c_all_t, gmod_cols)


def _adamw_math(g, w, m, v):
    m2 = B1 * m + (1.0 - B1) * g
    v2 = B2 * v + (1.0 - B2) * (g * g)
    m_hat = m2 / (1.0 - B1 ** STEP)
    v_hat = v2 / (1.0 - B2 ** STEP)
    delta = -LR * (m_hat / (jnp.sqrt(v_hat) + ADAM_EPS) + WD * w)
    return delta, m2, v2


def _row_tile(r, c, budget=1 << 18):
    if r * c <= budget or r % 8:
        return r
    best = 8
    for cand in range(8, r + 1, 8):
        if r % cand == 0 and cand * c <= budget:
            best = cand
    return best


def adamw_sum(parts, w, m, v, *, name):
    n, r, c = parts.shape
    tr = _row_tile(r, c)

    def body(p_ref, w_ref, m_ref, v_ref, g_ref, d_ref, m2_ref, v2_ref):
        g = p_ref[0].astype(F32)
        for j in range(1, n):
            g = g + p_ref[j].astype(F32)
        d, m2, v2 = _adamw_math(g, w_ref[...], m_ref[...], v_ref[...])
        g_ref[...] = g
        d_ref[...] = d
        m2_ref[...] = m2
        v2_ref[...] = v2

    blk = pl.BlockSpec((tr, c), lambda i: (i, 0))
    out = jax.ShapeDtypeStruct((r, c), F32)
    return pl.pallas_call(
        body, name=name, grid=(r // tr,), in_specs=[pl.BlockSpec((n, tr, c), lambda i: (0, i, 0)), blk, blk, blk],
        out_specs=[blk] * 4, out_shape=[out] * 4, compiler_params=_params(1))(parts, w, m, v)


def adamw_layer(parts, w, m, v, prev, layer, *, name):
    n, r, c = parts.shape
    nl = w.shape[0]
    tr = _row_tile(r, c)

    def body(p_ref, w_ref, m_ref, v_ref, *rest):
        g_ref, d_ref, m2_ref, v2_ref = rest[-4:]
        g = p_ref[0].astype(F32)
        for j in range(1, n):
            g = g + p_ref[j].astype(F32)
        d, m2, v2 = _adamw_math(g, w_ref[0], m_ref[0], v_ref[0])
        g_ref[0] = g
        d_ref[0] = d
        m2_ref[0] = m2
        v2_ref[0] = v2

    blk = pl.BlockSpec((1, tr, c), lambda i: (layer, i, 0))
    in_specs = [pl.BlockSpec((n, tr, c), lambda i: (0, i, 0)), blk, blk, blk]
    args = [parts, w, m, v]
    aliases = {}
    if prev is not None:
        in_specs += [ANY] * 4
        args += list(prev)
        aliases = {4 + k: k for k in range(4)}
    out = jax.ShapeDtypeStruct((nl, r, c), F32)
    return pl.pallas_call(
        body, name=name, grid=(r // tr,), in_specs=in_specs, out_specs=[blk] * 4, out_shape=[out] * 4,
        input_output_aliases=aliases, compiler_params=_params(1))(*args)


def _pack(arrays):
    flat, layout, off = [], [], 0
    for a in arrays:
        flat.append(a.reshape(-1))
        layout.append((off, a.shape))
        off += a.size
    pad = (-off) % 1024
    if pad:
        flat.append(jnp.zeros((pad,), F32))
    return jnp.concatenate(flat).reshape(-1, 128), layout


def _unpack(packed, layout, lead=()):
    flat = packed.reshape(lead + (-1,))
    return [flat[..., off:off + _size(shape)].reshape(lead + tuple(shape)) for off, shape in layout]


def _size(shape):
    n = 1
    for s in shape:
        n *= s
    return n


def _join_last(g):
    g = jnp.moveaxis(g, 0, -2)
    return g.reshape(g.shape[:-2] + (g.shape[-2] * g.shape[-1],))


def _my_cols(a, width):
    return lax.dynamic_slice_in_dim(a, _my_id() * width, width, axis=a.ndim - 1)


def _tile(t, pref):
    return min(pref, t)


def kernel(x, c, w_mod, b_mod, norm_g, sc_w_in, sc_conv, sc_w_out, pool_w, pool_b, pool_scale, cf_w_pw1, cf_b_pw1, cf_w_dw, cf_b_dw, cf_ln_g, cf_ln_b, cf_w_pw2, cf_b_pw2, ffn_w_up, ffn_conv, ffn_b_conv, ffn_w_down, loss_target, m_w_mod, m_b_mod, m_norm_g, m_sc_w_in, m_sc_conv, m_sc_w_out, m_pool_w, m_pool_b, m_pool_scale, m_cf_w_pw1, m_cf_b_pw1, m_cf_w_dw, m_cf_b_dw, m_cf_ln_g, m_cf_ln_b, m_cf_w_pw2, m_cf_b_pw2, m_ffn_w_up, m_ffn_conv, m_ffn_b_conv, m_ffn_w_down, v_w_mod, v_b_mod, v_norm_g, v_sc_w_in, v_sc_conv, v_sc_w_out, v_pool_w, v_pool_b, v_pool_scale, v_cf_w_pw1, v_cf_b_pw1, v_cf_w_dw, v_cf_b_dw, v_cf_ln_g, v_cf_ln_b, v_cf_w_pw2, v_cf_b_pw2, v_ffn_w_up, v_ffn_conv, v_ffn_b_conv, v_ffn_w_down):
    env = dict(locals())
    names = ["w_mod", "b_mod", "norm_g", "sc_w_in", "sc_conv", "sc_w_out", "pool_w", "pool_b", "pool_scale", "cf_w_pw1",
             "cf_b_pw1", "cf_w_dw", "cf_b_dw", "cf_ln_g", "cf_ln_b", "cf_w_pw2", "cf_b_pw2", "ffn_w_up", "ffn_conv",
             "ffn_b_conv", "ffn_w_down"]
    t = x.shape[1]
    tm = _tile(t, 512)
    tm_ffn = _tile(t, 256)
    tt = _tile(t, 2048)
    x0, target = x[0], loss_target[0]

    small_names = ["norm_g", "sc_conv", "cf_b_pw1", "cf_w_dw", "cf_b_dw", "cf_ln_g", "cf_ln_b", "cf_b_pw2", "ffn_conv"]
    packed, layout = _pack([c] + [env[n] for n in small_names])

    shard = {"pool": pool_w[0].astype(BF), "pw1": cf_w_pw1[0].astype(BF), "pw2": cf_w_pw2[0].astype(BF)}
    for j in range(2):
        shard[f"in{j}"], shard[f"out{j}"] = sc_w_in[j].astype(BF), sc_w_out[j].astype(BF)
    for l in range(DEPTH):
        shard[f"up{l}"], shard[f"down{l}"] = ffn_w_up[l].astype(BF), ffn_w_down[l].astype(BF)
    gathered, g_in0, g_out0 = _Exchange("gather", [packed, shard["in0"], shard["out0"]]).run("gather_first")
    wg = {"in0": g_in0, "out0": g_out0}
    parts = _unpack(gathered, layout, lead=(NDEV,))
    c_all = parts[0].reshape(NDEV, D)
    full = {n: _join_last(p) for n, p in zip(small_names, parts[1:])}
    fwd_plan = {("mix_in", 0): ["up0"], ("mix_out", 0): ["down0"], ("ffn_in", 0): ["pool", "up1"],
                ("ffn_out", 0): ["down1", "pw1", "pw2"], ("ffn_in", 1): ["up2"], ("ffn_out", 1): ["down2", "in1", "out1"],
                ("ffn_in", 2): ["up3"], ("ffn_out", 2): ["down3"]}

    def carrying(plan, kind, store, source, fn, key, *a, **k):
        names = plan.get(key)
        if not names:
            return fn(*a, **k)
        res = fn(*a, carry=_Exchange(kind, [source[n] for n in names]), **k)
        store.update(zip(names, res[-1]))
        return res[:-1]

    fwd = functools.partial(carrying, fwd_plan, "gather", wg, shard)

    mp = mod_partial(c_all, w_mod)
    (mod_parts,) = _Exchange("scatter", [jnp.swapaxes(mp, 0, 1)]).run("exchange_mod")
    mod = mod_finish(mod_parts, b_mod)

    def vec(a):
        return a.reshape(1, -1)

    def col_blocks(g):
        w = jnp.swapaxes(g, 0, 1).reshape(D, -1)
        return jnp.swapaxes(w.reshape(D, -1, D), 0, 1)

    def ffn_blocks(a):
        return jnp.swapaxes(a.reshape(a.shape[0], 4, FB), 0, 1)

    saved = []
    xs = x0
    for l in range(DEPTH):
        sh1, sc1, g1, sh2, sc2, g2 = [mod[l:l + 1, k * D:(k + 1) * D] for k in range(6)]
        ng = [full["norm_g"][l, k:k + 1] for k in range(4)]
        kind, j = l % 3, l // 3
        s = dict(x_in=xs, sc1=sc1, g1=g1, sc2=sc2, g2=g2, ng=ng)
        if kind == 0:
            s["w_in"] = col_blocks(wg[f"in{j}"])
            s["h"], s["p"] = fwd(fwd_in, ("mix_in", l), xs, ng[0], sc1, sh1, s["w_in"], None, blocked=False, tm=tm,
                                 name=f"sc_in_{l}")
            x1, s["m"], s["q"] = fwd(sc_fwd_out, ("mix_out", l), s["p"], full["sc_conv"][j], wg[f"out{j}"].reshape(D, D), xs,
                                     ng[1], g1, tm=tm, name=f"sc_out_{l}")
        elif kind == 1:
            pool_w_f = jnp.swapaxes(wg["pool"], 0, 1).reshape(4, PG, PG)
            x1, s["m"], s["ypre"], s["pooled"] = pool_fwd(xs, ng[0], sc1, sh1, pool_w_f, pool_b, pool_scale, ng[1], g1,
                                                          tm=tm, name=f"pool_{l}")
        else:
            s["w_in"] = col_blocks(wg["pw1"])
            s["h"], s["a"] = fwd_in(xs, ng[0], sc1, sh1, s["w_in"], full["cf_b_pw1"].reshape(2, 1, D), blocked=False, tm=tm,
                                    name=f"cf_in_{l}")
            x1, s["m"], s["s"], s["u2"] = cf_fwd_out(s["a"], full["cf_w_dw"][0], full["cf_b_dw"], full["cf_ln_g"],
                                                     full["cf_ln_b"], wg["pw2"].reshape(D, D), full["cf_b_pw2"], xs, ng[1],
                                                     g1, tm=tm, name=f"cf_out_{l}")
        s["x1"] = x1
        s["cw"] = ffn_blocks(full["ffn_conv"][l])
        s["h2"], s["up"] = fwd(fwd_in, ("ffn_in", l), x1, ng[2], sc2, sh2, wg[f"up{l}"], None, blocked=True, tm=tm,
                               name=f"ffn_in_{l}")
        xs, s["f"], s["gc"], s["fa"] = fwd(ffn_fwd_out, ("ffn_out", l), s["up"], s["cw"], ffn_blocks(ffn_b_conv[l:l + 1]),
                                           wg[f"down{l}"].reshape(F, D), x1, ng[3], g2, tm=tm_ffn, name=f"ffn_out_{l}")
        saved.append(s)

    dx, loss_part = loss_head(xs, target, tm=tm, name="loss_head")
    loss = lax.psum(loss_part[0, 0], ("x", "y", "c"))

    gmod = [None] * DEPTH
    d_norm_g = [None] * DEPTH
    d_ffn_conv = [None] * DEPTH
    d_ffn_b_conv = [None] * DEPTH
    d_sc_conv = [None] * 2
    big = {}
    got = {}
    small_g = {}
    bwd_plan = {("mix_bout", 3): ["down3"], ("mix_bmid", 3): ["up3"], ("ffn_bout", 2): ["in1", "out1"],
                ("mix_bmid", 2): ["up2", "down2"], ("ffn_bout", 1): ["pw1", "pw2"], ("ffn_bout", 0): ["pool", "down1"],
                ("ffn_bmid", 0): ["up1"], ("mix_bout", 0): ["down0"], ("mix_bmid", 0): ["up0"]}
    bwd = functools.partial(carrying, bwd_plan, "scatter", got, big)
    pool_w_f = jnp.swapaxes(wg["pool"], 0, 1).reshape(4, PG, PG)
    for l in reversed(range(DEPTH)):
        s = saved[l]
        ng = s["ng"]
        kind, j = l % 3, l // 3
        df, da, st_o = bwd(bwd_out, ("ffn_bout", l), dx, s["f"], ng[3], s["g2"], wg[f"down{l}"].reshape(F, D), blocked=True,
                           tm=tm, name=f"ffn_bout_{l}")
        dup, st_c = bwd(ffn_bwd_mid, ("ffn_bmid", l), da, s["gc"], s["up"], s["cw"], tm=tm, name=f"ffn_bmid_{l}")
        dx1, st_i = bwd_in([dup], wg[f"up{l}"], s["x1"], ng[2], s["sc2"], dx, tm=tm, name=f"ffn_bin_{l}")
        big[f"up{l}"] = wgrad(s["h2"], dup, nblk=NDEV, a_blocked=False, b_blocked=True, bk=D, bn=FB, tt=tt, name=f"ffn_wup_{l}")
        big[f"down{l}"] = wgrad(s["fa"], df, nblk=4, a_blocked=True, b_blocked=False, bk=FB, bn=D, tt=tt,
                                name=f"ffn_wdown_{l}").reshape(NDEV, F // NDEV, D)
        d_ffn_b_conv[l] = st_c[:, 0, :].reshape(F)
        d_ffn_conv[l] = jnp.swapaxes(st_c[:, 1:4, :], 0, 1).reshape(3, F)
        g_ffn = [st_i[0], st_i[1], st_o[0]]
        dn3, dn2 = st_o[1], st_i[2]
        if kind == 0:
            dm, dq, st_o = bwd(bwd_out, ("mix_bout", l), dx1, s["m"], ng[1], s["g1"], wg[f"out{j}"].reshape(D, D), blocked=False,
                               tm=tm, name=f"sc_bout_{l}")
            dp, st_c = bwd(sc_bwd_mid, ("mix_bmid", l), dq, s["p"], full["sc_conv"][j], tm=tm, name=f"sc_bmid_{l}")
            dx, st_i = bwd_in([dp], s["w_in"], s["x_in"], ng[0], s["sc1"], dx1, tm=tm, name=f"sc_bin_{l}")
            big[f"in{j}"] = wgrad(s["h"], dp, nblk=NDEV, a_blocked=False, b_blocked=True, bk=D, bn=3 * D // NDEV, tt=tt,
                                  name=f"sc_win_{l}")
            big[f"out{j}"] = wgrad(s["q"], dm, nblk=1, a_blocked=False, b_blocked=False, bk=D, bn=D, tt=tt,
                                   name=f"sc_wout_{l}").reshape(NDEV, D // NDEV, D)
            d_sc_conv[j] = st_c[0:3]
        elif kind == 1:
            dh, dyp, st_o = pool_bwd(dx1, s["m"], s["ypre"], pool_w_f, pool_scale, ng[1], s["g1"], tm=tm, name=f"pool_b_{l}")
            dx, st_i = bwd_in([dh], None, s["x_in"], ng[0], s["sc1"], dx1, tm=tm, name=f"pool_bin_{l}")
            dpw = wgrad(s["pooled"], dyp, nblk=4, a_blocked=True, b_blocked=True, bk=PG, bn=PG, tt=tt, name=f"pool_w_{l}")
            big["pool"] = jnp.swapaxes(dpw.reshape(4, NDEV, PG // NDEV, PG), 0, 1).reshape(NDEV, 4 * PG // NDEV, PG)
            small_g["pool_scale"], small_g["pool_b"] = st_o[2:3], st_o[3:4]
        else:
            dm, ds, st_o = bwd_out(dx1, s["m"], ng[1], s["g1"], wg["pw2"].reshape(D, D), blocked=False, tm=tm,
                                   name=f"cf_bout_{l}")
            dA, st_c = bwd(cf_bwd_mid, ("mix_bmid", l), ds, s["u2"], s["a"], full["cf_w_dw"][0], full["cf_ln_g"],
                           full["cf_ln_b"], tm=tm, name=f"cf_bmid_{l}")
            dx, st_i = bwd_in([dA], s["w_in"], s["x_in"], ng[0], s["sc1"], dx1, tm=tm, name=f"cf_bin_{l}")
            big["pw1"] = wgrad(s["h"], dA, nblk=NDEV, a_blocked=False, b_blocked=True, bk=D, bn=2 * D // NDEV, tt=tt,
                               name=f"cf_wpw1_{l}")
            big["pw2"] = wgrad(s["s"], dm, nblk=1, a_blocked=False, b_blocked=False, bk=D, bn=D, tt=tt,
                               name=f"cf_wpw2_{l}").reshape(NDEV, D // NDEV, D)
            small_g["cf_w_dw"] = st_c[0:CFW][None]
            small_g["cf_b_dw"], small_g["cf_ln_g"], small_g["cf_ln_b"] = st_c[31:32], st_c[32:33], st_c[33:34]
            small_g["cf_b_pw1"] = st_c[34:36].reshape(1, 2 * D)
            small_g["cf_b_pw2"] = st_o[2:3]
        gmod[l] = jnp.concatenate([st_i[0], st_i[1], st_o[0]] + g_ffn)
        d_norm_g[l] = jnp.stack([st_i[2], st_o[1], dn2, dn3])

    small_g["gmod"] = jnp.stack(gmod)
    small_g["norm_g"] = jnp.stack(d_norm_g)
    small_g["sc_conv"] = jnp.stack(d_sc_conv)
    small_g["ffn_conv"] = jnp.stack(d_ffn_conv)
    small_g["ffn_b_conv"] = jnp.stack(d_ffn_b_conv)
    sg_names = ["gmod", "norm_g", "sc_conv", "pool_b", "pool_scale", "cf_b_pw1", "cf_w_dw", "cf_b_dw", "cf_ln_g", "cf_ln_b",
                "cf_b_pw2", "ffn_conv", "ffn_b_conv"]
    gpacked, glayout = _pack([small_g[n] for n in sg_names])
    (ggath,) = _Exchange("gather", [gpacked]).run("gather_small_grads")
    gsum = dict(zip(sg_names, _unpack(sum_parts(ggath), glayout)))
    gmod_all = _unpack(ggath, glayout[:1], lead=(NDEV,))[0]
    grads = {"b_mod": gsum["gmod"], "pool_b": gsum["pool_b"], "pool_scale": gsum["pool_scale"],
             "ffn_b_conv": gsum["ffn_b_conv"]}
    for n in ["norm_g", "sc_conv", "cf_b_pw1", "cf_w_dw", "cf_b_dw", "cf_ln_g", "cf_ln_b", "cf_b_pw2", "ffn_conv"]:
        grads[n] = _my_cols(gsum[n], env[n].shape[-1])
    grads["w_mod"] = mod_wgrad(c_all.T, jnp.swapaxes(_my_cols(gmod_all, w_mod.shape[2]), 0, 1))

    deltas, new_m, new_v = {}, {}, {}
    sp_names = ["b_mod", "norm_g", "sc_conv", "pool_b", "pool_scale", "cf_b_pw1", "cf_w_dw", "cf_b_dw", "cf_ln_g", "cf_ln_b",
                "cf_b_pw2", "ffn_conv", "ffn_b_conv"]
    pg, playout = _pack([grads[n] for n in sp_names])
    pw_, _ = _pack([env[n] for n in sp_names])
    pm_, _ = _pack([env["m_" + n] for n in sp_names])
    pv_, _ = _pack([env["v_" + n] for n in sp_names])
    _, sd, sm, sv = adamw_sum(pg[None], pw_, pm_, pv_, name="adamw_small")
    for n, d_, m_, v_ in zip(sp_names, _unpack(sd, playout), _unpack(sm, playout), _unpack(sv, playout)):
        deltas[n], new_m[n], new_v[n] = d_, m_, v_
    gw = grads["w_mod"].reshape(1, DEPTH * D, -1)
    _, d_, m_, v_ = adamw_sum(gw, w_mod.reshape(gw.shape[1:]), m_w_mod.reshape(gw.shape[1:]), v_w_mod.reshape(gw.shape[1:]),
                              name="adamw_w_mod")
    deltas["w_mod"], new_m["w_mod"], new_v["w_mod"] = [a.reshape(w_mod.shape) for a in (d_, m_, v_)]

    got["in0"], got["out0"] = _Exchange("scatter", [big["in0"], big["out0"]]).run("scatter_last")
    groups = {"sc_w_in": ["in0", "in1"], "sc_w_out": ["out0", "out1"], "pool_w": ["pool"], "cf_w_pw1": ["pw1"],
              "cf_w_pw2": ["pw2"], "ffn_w_up": [f"up{l}" for l in range(DEPTH)], "ffn_w_down": [f"down{l}" for l in range(DEPTH)]}
    for n, layers in groups.items():
        stacked = (len(layers),) + got[layers[0]].shape[1:]
        w3 = [env[p + n].reshape(stacked) for p in ("", "m_", "v_")]
        outs = None
        for li, key in enumerate(layers):
            outs = adamw_layer(got[key], *w3, outs, li, name=f"adamw_{n}_{li}")
        grads[n], deltas[n], new_m[n], new_v[n] = [a.reshape(env[n].shape) for a in outs]

    return (loss, dx[None], *[grads[n] for n in names], *[deltas[n] for n in names], *[new_m[n] for n in names],
            *[new_v[n] for n in names])
```

```python
import functools

import jax
import jax.numpy as jnp
from jax import lax
from jax.experimental import pallas as pl
from jax.experimental.pallas import tpu as pltpu

D = 1024
F = 2816
NDEV = 8
FB = F // 4
DEPTH = 4
RMS_EPS = 1e-6
LN_EPS = 1e-5
CFW = 31
POOL_WINDOWS = (2, 4, 8, 16)
PG = D // 4
LR, B1, B2, ADAM_EPS, WD, STEP = 0.001, 0.9, 0.999, 1e-08, 0.01, 10

BF = jnp.bfloat16
F32 = jnp.float32
VMEM_LIMIT_V7X = 56 * 1024 * 1024
MESH = pl.DeviceIdType.MESH
ANY = pl.BlockSpec(memory_space=pl.ANY)


def _params(n_axes):
    return pltpu.CompilerParams(dimension_semantics=("arbitrary",) * n_axes, vmem_limit_bytes=VMEM_LIMIT_V7X)


def _const(shape, single=True):
    nd = len(shape)
    if single:
        return pl.BlockSpec(shape, lambda *_: (0,) * nd, pipeline_mode=pl.Buffered(1))
    return pl.BlockSpec(shape, lambda *_: (0,) * nd)


def _rows(tm, c):
    return pl.BlockSpec((tm, c), lambda i: (i, 0))


def _brows(nb, tm, c, b0=0):
    return pl.BlockSpec((nb, tm, c), lambda i: (b0, i, 0))


def _prev(hb, c, tm):
    return pl.BlockSpec((hb, c), lambda i: (jnp.maximum(i * (tm // hb) - 1, 0), 0))


def _next(hb, c, tm, t):
    return pl.BlockSpec((hb, c), lambda i: (jnp.minimum((i + 1) * (tm // hb), t // hb - 1), 0))


def _bprev(nb, hb, c, tm, b0=0):
    return pl.BlockSpec((nb, hb, c), lambda i: (b0, jnp.maximum(i * (tm // hb) - 1, 0), 0))


def _bnext(nb, hb, c, tm, t, b0=0):
    return pl.BlockSpec((nb, hb, c), lambda i: (b0, jnp.minimum((i + 1) * (tm // hb), t // hb - 1), 0))


def _sigmoid(v):
    return 0.5 * jnp.tanh(0.5 * v) + 0.5


def _fold8(v):
    r, c = v.shape
    return jnp.sum(v.reshape(r // 8, 8, c), axis=0)


def _chunks(n_rows, rc, step, init=0, reverse=False):
    n = n_rows // rc

    def it(c, carry):
        idx = (n - 1 - c) if reverse else c
        return step(pl.multiple_of(idx * rc, rc), carry)

    return lax.fori_loop(0, n, it, init)


def _row_shifted_copies(s_ref, n):
    for b in range(1, 8):
        s_ref[b, 0:n, :] = s_ref[0, pl.ds(b, n), :]


def _shifted(s_ref, o, tm):
    return s_ref[o % 8, pl.ds(8 * (o // 8), tm), :]


def _dot(a, b):
    return jnp.dot(a, b, preferred_element_type=F32)


def _dot_nt(a, b):
    return lax.dot_general(a, b, (((1,), (1,)), ((), ())), preferred_element_type=F32)


def _dot_tn(a, b):
    return lax.dot_general(a, b, (((0,), (0,)), ((), ())), preferred_element_type=F32)


def _rsum(v):
    return jnp.sum(v, axis=0, keepdims=True)


def _adaln(x, g, sc, sh):
    r = lax.rsqrt(jnp.mean(x * x, axis=-1, keepdims=True) + RMS_EPS)
    return (x * r * g) * (1.0 + sc) + sh


def _gated_res(x, m, gn, gt):
    r = lax.rsqrt(jnp.mean(m * m, axis=-1, keepdims=True) + RMS_EPS)
    return x + gt * (m * r * gn)


def _gated_res_bwd(dxo, m, gn, gt):
    r = lax.rsqrt(jnp.mean(m * m, axis=-1, keepdims=True) + RMS_EPS)
    mh = m * r
    dgt = _rsum(dxo * (mh * gn))
    dn = dxo * gt
    dgn = _rsum(dn * mh)
    dmh = dn * gn
    dm = r * (dmh - mh * jnp.mean(dmh * mh, axis=-1, keepdims=True))
    return dm, dgt, dgn


def _my_id():
    return 4 * lax.axis_index("x") + 2 * lax.axis_index("y") + lax.axis_index("c")


def _peer(k):
    x, y, c = lax.axis_index("x"), lax.axis_index("y"), lax.axis_index("c")
    px = 1 - x if k & 4 else x
    py = 1 - y if k & 2 else y
    pc = 1 - c if k & 1 else c
    return (px, py, pc), 4 * px + 2 * py + pc


class _Exchange:
    def __init__(self, kind, arrays):
        self.gather = kind == "gather"
        self.arrays = list(arrays)
        n = len(self.arrays)
        if self.gather:
            self.out_shape = [jax.ShapeDtypeStruct((NDEV,) + a.shape, a.dtype) for a in self.arrays]
        else:
            self.out_shape = [jax.ShapeDtypeStruct(a.shape, a.dtype) for a in self.arrays]
        self.scratch = [pltpu.SemaphoreType.DMA((n * NDEV,)), pltpu.SemaphoreType.DMA((n * NDEV,)),
                        pltpu.SemaphoreType.DMA((n,))]

    def _local(self, a, src, dst, sems):
        me = _my_id()
        return pltpu.make_async_copy(src[a] if self.gather else src[a].at[me], dst[a].at[me], sems[2].at[a])

    def _remote(self, a, k, src, dst, sems, incoming):
        to, pid = _peer(k)
        me = _my_id()
        return pltpu.make_async_remote_copy(
            src_ref=src[a] if self.gather else src[a].at[pid], dst_ref=dst[a].at[pid if incoming else me],
            send_sem=sems[0].at[a * NDEV + k], recv_sem=sems[1].at[a * NDEV + k], device_id=to, device_id_type=MESH)

    def start(self, src, dst, sems):
        for a in range(len(self.arrays)):
            self._local(a, src, dst, sems).start()
        for k in range(1, NDEV):
            for a in range(len(self.arrays)):
                self._remote(a, k, src, dst, sems, False).start()

    def wait(self, src, dst, sems):
        for k in range(1, NDEV):
            for a in range(len(self.arrays)):
                self._remote(a, k, src, dst, sems, True).wait_recv()
        for k in range(1, NDEV):
            for a in range(len(self.arrays)):
                self._remote(a, k, src, dst, sems, False).wait_send()
        for a in range(len(self.arrays)):
            self._local(a, src, dst, sems).wait()

    def run(self, name):
        n = len(self.arrays)

        def body(*refs):
            src, dst, sems = refs[:n], refs[n:2 * n], refs[2 * n:]
            self.start(src, dst, sems)
            self.wait(src, dst, sems)

        return pl.pallas_call(body, name=name, in_specs=[ANY] * n, out_specs=[ANY] * n, out_shape=self.out_shape,
                              scratch_shapes=self.scratch)(*self.arrays)


def _run_exchanges(exchanges, name):
    counts = [len(e.arrays) for e in exchanges]
    n = sum(counts)

    def body(*refs):
        src, dst, sems = refs[:n], refs[n:2 * n], refs[2 * n:]
        parts, lo = [], 0
        for ei, (e, c) in enumerate(zip(exchanges, counts)):
            parts.append((e, src[lo:lo + c], dst[lo:lo + c], sems[3 * ei:3 * ei + 3]))
            lo += c
        for e, s, d, m in parts:
            e.start(s, d, m)
        for e, s, d, m in parts:
            e.wait(s, d, m)

    res = pl.pallas_call(
        body, name=name, in_specs=[ANY] * n, out_specs=[ANY] * n, out_shape=[s for e in exchanges for s in e.out_shape],
        scratch_shapes=[s for e in exchanges for s in e.scratch])(*[a for e in exchanges for a in e.arrays])
    out, lo = [], 0
    for c in counts:
        out.append(list(res[lo:lo + c]))
        lo += c
    return out


def _call(body, *, name, grid, in_specs, out_specs, out_shape, args, scratch_shapes=(), carry=None):
    cp = _params(len(grid))
    if carry is None:
        return tuple(pl.pallas_call(body, name=name, grid=grid, in_specs=in_specs, out_specs=out_specs, out_shape=out_shape,
                                    scratch_shapes=list(scratch_shapes), compiler_params=cp)(*args))
    n_in, n_out, n_sc, n_c = len(in_specs), len(out_specs), len(scratch_shapes), len(carry.arrays)

    def wrapped(*refs):
        ins, src = refs[:n_in], refs[n_in:n_in + n_c]
        outs = refs[n_in + n_c:n_in + n_c + n_out]
        dst = refs[n_in + n_c + n_out:n_in + 2 * n_c + n_out]
        rest = refs[n_in + 2 * n_c + n_out:]
        scr, sems = rest[:n_sc], rest[n_sc:]
        first = pl.program_id(0) == 0
        last = pl.program_id(0) == grid[0] - 1
        for ax in range(1, len(grid)):
            first = jnp.logical_and(first, pl.program_id(ax) == 0)
            last = jnp.logical_and(last, pl.program_id(ax) == grid[ax] - 1)

        @pl.when(first)
        def _():
            carry.start(src, dst, sems)

        body(*ins, *outs, *scr)

        @pl.when(last)
        def _():
            carry.wait(src, dst, sems)

    res = pl.pallas_call(
        wrapped, name=name, grid=grid, in_specs=list(in_specs) + [ANY] * n_c, out_specs=list(out_specs) + [ANY] * n_c,
        out_shape=list(out_shape) + carry.out_shape, scratch_shapes=list(scratch_shapes) + carry.scratch,
        compiler_params=cp)(*args, *carry.arrays)
    return tuple(res[:n_out]) + (list(res[n_out:]),)


def fwd_in(x, g, sc, sh, w, bias, *, blocked, tm, name, carry=None, wt=False):
    t = x.shape[0]
    nb, bw = (w.shape[0], w.shape[1]) if wt else (w.shape[0], w.shape[2])

    def body(*refs):
        if bias is None:
            x_ref, g_ref, sc_ref, sh_ref, w_ref, h_ref, p_ref = refs
        else:
            x_ref, g_ref, sc_ref, sh_ref, w_ref, b_ref, h_ref, p_ref = refs
        hb = _adaln(x_ref[...], g_ref[...], sc_ref[...], sh_ref[...]).astype(BF)
        h_ref[...] = hb
        for d in range(nb):
            y = _dot_nt(hb, w_ref[d]) if wt else _dot(hb, w_ref[d])
            if bias is not None:
                y = y + b_ref[d]
            if blocked:
                p_ref[d] = y.astype(BF)
            else:
                p_ref[:, d * bw:(d + 1) * bw] = y.astype(BF)

    vec = _const((1, D))
    in_specs = [_rows(tm, D), vec, vec, vec, _const(w.shape)]
    args = [x, g, sc, sh, w]
    if bias is not None:
        in_specs.append(_const((nb, 1, bw)))
        args.append(bias)
    if blocked:
        p_spec, p_shape = _brows(nb, tm, bw), jax.ShapeDtypeStruct((nb, t, bw), BF)
    else:
        p_spec, p_shape = _rows(tm, nb * bw), jax.ShapeDtypeStruct((t, nb * bw), BF)
    return _call(body, name=name, grid=(t // tm,), in_specs=in_specs, out_specs=[_rows(tm, D), p_spec],
                 out_shape=[jax.ShapeDtypeStruct((t, D), BF), p_shape], args=args, carry=carry)


def _conv3_from(s_ref, w, tm, lo):
    acc = w[0:1, :] * s_ref[pl.ds(lo, tm), :]
    for k in (1, 2):
        acc = acc + w[k:k + 1, :] * s_ref[pl.ds(lo + k, tm), :]
    return acc


def sc_fwd_out(p, convw, w_out, x, gn, gt, *, tm, name, carry=None):
    t = x.shape[0]

    def body(p_ref, ph_ref, cw_ref, w_ref, x_ref, gn_ref, gt_ref, x1_ref, m_ref, q_ref, s_ref):
        i = pl.program_id(0)
        zh = ph_ref[8:16, D:2 * D].astype(F32) * ph_ref[8:16, 2 * D:3 * D].astype(F32)
        s_ref[0:8, :] = jnp.where(i == 0, 0.0, zh)
        s_ref[8:8 + tm, :] = p_ref[:, D:2 * D].astype(F32) * p_ref[:, 2 * D:3 * D].astype(F32)
        u = _conv3_from(s_ref, cw_ref[...], tm, 6)
        qb = (p_ref[:, 0:D].astype(F32) * u).astype(BF)
        q_ref[...] = qb
        m = _dot(qb, w_ref[...])
        m_ref[...] = m.astype(BF)
        x1_ref[...] = _gated_res(x_ref[...], m, gn_ref[...], gt_ref[...])

    vec = _const((1, D))
    return _call(
        body, name=name, grid=(t // tm,),
        in_specs=[_rows(tm, 3 * D), _prev(16, 3 * D, tm), _const((3, D)), _const((D, D)), _rows(tm, D), vec, vec],
        out_specs=[_rows(tm, D)] * 3,
        out_shape=[jax.ShapeDtypeStruct((t, D), F32), jax.ShapeDtypeStruct((t, D), BF), jax.ShapeDtypeStruct((t, D), BF)],
        scratch_shapes=[pltpu.VMEM((tm + 8, D), F32)], args=[p, p, convw, w_out, x, gn, gt], carry=carry)


def _layernorm_parts(u2):
    mu = jnp.mean(u2, axis=-1, keepdims=True)
    cen = u2 - mu
    rstd = lax.rsqrt(jnp.mean(cen * cen, axis=-1, keepdims=True) + LN_EPS)
    return cen * rstd, rstd


def cf_fwd_out(a, w_dw, b_dw, ln_g, ln_b, w_pw2, b_pw2, x, gn, gt, *, tm, name):
    t = x.shape[0]
    hb = 32

    def body(a_ref, ah_ref, wd_ref, bd_ref, lg_ref, lb_ref, w_ref, b2_ref, x_ref, gn_ref, gt_ref,
             x1_ref, m_ref, s_out_ref, u2_ref, s_ref):
        i = pl.program_id(0)
        uh = ah_ref[:, 0:D].astype(F32) * _sigmoid(ah_ref[:, D:2 * D].astype(F32))
        s_ref[0, 0:hb, :] = jnp.where(i == 0, 0.0, uh)
        s_ref[0, hb:hb + tm, :] = a_ref[:, 0:D].astype(F32) * _sigmoid(a_ref[:, D:2 * D].astype(F32))
        _row_shifted_copies(s_ref, tm + hb - 8)
        acc = bd_ref[...] + wd_ref[0:1, :] * _shifted(s_ref, hb - CFW + 1, tm)
        for k in range(1, CFW):
            acc = acc + wd_ref[k:k + 1, :] * _shifted(s_ref, hb - CFW + 1 + k, tm)
        u2_ref[...] = acc.astype(BF)
        xh, _ = _layernorm_parts(acc)
        l = xh * lg_ref[...] + lb_ref[...]
        sb = (l * _sigmoid(l)).astype(BF)
        s_out_ref[...] = sb
        m = _dot(sb, w_ref[...]) + b2_ref[...]
        m_ref[...] = m.astype(BF)
        x1_ref[...] = _gated_res(x_ref[...], m, gn_ref[...], gt_ref[...])

    vec = _const((1, D))
    return pl.pallas_call(
        body, name=name, grid=(t // tm,),
        in_specs=[_rows(tm, 2 * D), _prev(hb, 2 * D, tm), _const((CFW, D)), vec, vec, vec, _const((D, D)), vec,
                  _rows(tm, D), vec, vec],
        out_specs=[_rows(tm, D)] * 4,
        out_shape=[jax.ShapeDtypeStruct((t, D), F32)] + [jax.ShapeDtypeStruct((t, D), BF)] * 3,
        scratch_shapes=[pltpu.VMEM((8, tm + hb, D), F32)], compiler_params=_params(1),
    )(a, a, w_dw, b_dw, ln_g, ln_b, w_pw2, b_pw2, x, gn, gt)


def _pool_counts(i, tm, w):
    row = lax.broadcasted_iota(jnp.int32, (tm, 1), 0) + i * tm
    return jnp.minimum(row + 1, w).astype(F32)


def pool_fwd(x, g, sc, sh, pw, pb, pscale, gn, gt, *, tm, name):
    t = x.shape[0]
    pad, hb = 8, 16
    base = pad + hb

    def body(x_ref, xh_ref, g_ref, sc_ref, sh_ref, pw_ref, pb_ref, ps_ref, gn_ref, gt_ref,
             x1_ref, m_ref, yp_ref, po_ref, sa_ref, sb_ref):
        i = pl.program_id(0)
        hh = _adaln(xh_ref[...], g_ref[...], sc_ref[...], sh_ref[...])
        h = _adaln(x_ref[...], g_ref[...], sc_ref[...], sh_ref[...])
        zero = jnp.zeros((pad, D), F32)
        sa_ref[0:pad, :] = zero
        sb_ref[0:pad, :] = zero
        sa_ref[pad:base, :] = jnp.where(i == 0, 0.0, hh)
        sa_ref[base:base + tm, :] = h
        n = hb + tm
        src, dst = sa_ref, sb_ref
        ys = []
        for gi, w in enumerate(POOL_WINDOWS):
            c0 = gi * PG
            step = w // 2
            dst[pl.ds(pad, n), c0:D] = src[pl.ds(pad, n), c0:D] + src[pl.ds(pad - step, n), c0:D]
            mean = dst[pl.ds(base, tm), c0:c0 + PG] / _pool_counts(i, tm, w)
            pooled = (mean - h[:, c0:c0 + PG]).astype(BF)
            po_ref[:, c0:c0 + PG] = pooled
            ys.append(_dot(pooled, pw_ref[gi]))
            src, dst = dst, src
        ypre = jnp.concatenate(ys, axis=1) + pb_ref[...]
        yp_ref[...] = ypre.astype(BF)
        m = ypre * ps_ref[...]
        m_ref[...] = m.astype(BF)
        x1_ref[...] = _gated_res(x_ref[...], m, gn_ref[...], gt_ref[...])

    vec = _const((1, D))
    return pl.pallas_call(
        body, name=name, grid=(t // tm,),
        in_specs=[_rows(tm, D), _prev(hb, D, tm), vec, vec, vec, _const((4, PG, PG)), vec, vec, vec, vec],
        out_specs=[_rows(tm, D)] * 4,
        out_shape=[jax.ShapeDtypeStruct((t, D), F32)] + [jax.ShapeDtypeStruct((t, D), BF)] * 3,
        scratch_shapes=[pltpu.VMEM((tm + base, D), F32)] * 2, compiler_params=_params(1),
    )(x, x, g, sc, sh, pw, pb, pscale, gn, gt)


def ffn_fwd_out(up, convw, convb, w_down, x, gn, gt, *, tm, name, carry=None):
    t = x.shape[0]

    def body(gate_ref, gh_ref, val_ref, cw_ref, cb_ref, w_ref, x_ref, gn_ref, gt_ref,
             x2_ref, f_ref, gc_ref, a_ref, s_ref):
        i = pl.program_id(0)
        acc = jnp.zeros((tm, D), F32)
        for j in range(4):
            sj = s_ref.at[j]
            sj[0:8, :] = jnp.where(i == 0, 0.0, gh_ref[j, 8:16, :].astype(F32))
            sj[8:8 + tm, :] = gate_ref[j].astype(F32)
            gc = cb_ref[j] + _conv3_from(sj, cw_ref[j], tm, 6)
            gc_ref[j] = gc.astype(BF)
            ab = (gc * _sigmoid(gc) * val_ref[j].astype(F32)).astype(BF)
            a_ref[j] = ab
            acc = acc + _dot(ab, w_ref[j * FB:(j + 1) * FB, :])
        f_ref[...] = acc.astype(BF)
        x2_ref[...] = _gated_res(x_ref[...], acc, gn_ref[...], gt_ref[...])

    vec = _const((1, D))
    blk = jax.ShapeDtypeStruct((4, t, FB), BF)
    return _call(
        body, name=name, grid=(t // tm,),
        in_specs=[_brows(4, tm, FB, 0), _bprev(4, 16, FB, tm, 0), _brows(4, tm, FB, 1), _const((4, 3, FB)),
                  _const((4, 1, FB)), _const((F, D)), _rows(tm, D), vec, vec],
        out_specs=[_rows(tm, D), _rows(tm, D), _brows(4, tm, FB), _brows(4, tm, FB)],
        out_shape=[jax.ShapeDtypeStruct((t, D), F32), jax.ShapeDtypeStruct((t, D), BF), blk, blk],
        scratch_shapes=[pltpu.VMEM((4, tm + 8, FB), F32)], args=[up, up, up, convw, convb, w_down, x, gn, gt], carry=carry)


def loss_head(y, target, *, tm, name):
    t = y.shape[0]

    def body(y_ref, t_ref, dy_ref, l_ref, acc_ref):
        i = pl.program_id(0)

        @pl.when(i == 0)
        def _():
            acc_ref[...] = jnp.zeros_like(acc_ref)

        e = y_ref[...] - t_ref[...]
        dy_ref[...] = e * (1.0 / D)
        acc_ref[...] += _rsum(e * e)

        @pl.when(i == pl.num_programs(0) - 1)
        def _():
            l_ref[...] = jnp.sum(acc_ref[...], axis=1, keepdims=True) * (0.5 / D)

    return pl.pallas_call(
        body, name=name, grid=(t // tm,), in_specs=[_rows(tm, D), _rows(tm, D)],
        out_specs=[_rows(tm, D), pl.BlockSpec((1, 1), lambda i: (0, 0))],
        out_shape=[jax.ShapeDtypeStruct((t, D), F32), jax.ShapeDtypeStruct((1, 1), F32)],
        scratch_shapes=[pltpu.VMEM((1, D), F32)], compiler_params=_params(1))(y, target)


def _init_stats(ref):
    @pl.when(pl.program_id(0) == 0)
    def _():
        ref[...] = jnp.zeros_like(ref)


def bwd_out(dxo, m, gn, gt, w, *, blocked, tm, name, carry=None):
    t = dxo.shape[0]
    k = w.shape[0]

    def body(dx_ref, m_ref, gn_ref, gt_ref, w_ref, dm_ref, da_ref, st_ref):
        _init_stats(st_ref)
        dm, dgt, dgn = _gated_res_bwd(dx_ref[...], m_ref[...].astype(F32), gn_ref[...], gt_ref[...])
        st_ref[0:1, :] += dgt
        st_ref[1:2, :] += dgn
        st_ref[2:3, :] += _rsum(dm)
        dmb = dm.astype(BF)
        dm_ref[...] = dmb
        if blocked:
            for j in range(4):
                da_ref[j] = _dot_nt(dmb, w_ref[j * FB:(j + 1) * FB, :]).astype(BF)
        else:
            da_ref[...] = _dot_nt(dmb, w_ref[...]).astype(BF)

    vec = _const((1, D))
    if blocked:
        da_spec, da_shape = _brows(4, tm, FB), jax.ShapeDtypeStruct((4, t, FB), BF)
    else:
        da_spec, da_shape = _rows(tm, k), jax.ShapeDtypeStruct((t, k), BF)
    return _call(
        body, name=name, grid=(t // tm,), in_specs=[_rows(tm, D), _rows(tm, D), vec, vec, _const((k, D))],
        out_specs=[_rows(tm, D), da_spec, _const((8, D), single=False)],
        out_shape=[jax.ShapeDtypeStruct((t, D), BF), da_shape, jax.ShapeDtypeStruct((8, D), F32)],
        args=[dxo, m, gn, gt, w], carry=carry)


def bwd_in(dps, w, x, g, sc, dxo, *, tm, name, wt=False):
    t = x.shape[0]
    direct = w is None
    if not direct:
        nb, bw = (w.shape[0], w.shape[1]) if wt else (w.shape[0], w.shape[2])
    natural = (not direct) and dps[0].ndim == 2

    def body(*refs):
        n = len(dps)
        dp_refs = refs[:n]
        if direct:
            x_ref, g_ref, sc_ref, dxo_ref, dx_ref, st_ref = refs[n:]
            dh = dp_refs[0][...]
        else:
            w_ref, x_ref, g_ref, sc_ref, dxo_ref, dx_ref, st_ref = refs[n:]
            dh = jnp.zeros((tm, D), F32)
            if natural:
                for d in range(nb):
                    dh = dh + _dot_nt(dp_refs[0][:, d * bw:(d + 1) * bw], w_ref[d])
            else:
                d = 0
                for r in dp_refs:
                    for j in range(r.shape[0]):
                        dh = dh + (_dot(r[j], w_ref[d]) if wt else _dot_nt(r[j], w_ref[d]))
                        d += 1
        _init_stats(st_ref)
        x = x_ref[...]
        r = lax.rsqrt(jnp.mean(x * x, axis=-1, keepdims=True) + RMS_EPS)
        xh = x * r
        gv = g_ref[...]
        st_ref[0:1, :] += _rsum(dh)
        st_ref[1:2, :] += _rsum(dh * (xh * gv))
        dn = dh * (1.0 + sc_ref[...])
        st_ref[2:3, :] += _rsum(dn * xh)
        dy = dn * gv
        dx_ref[...] = dxo_ref[...] + r * (dy - xh * jnp.mean(dy * xh, axis=-1, keepdims=True))

    vec = _const((1, D))
    if direct:
        dp_specs = [_rows(tm, D)]
    elif natural:
        dp_specs = [_rows(tm, nb * bw)]
    else:
        dp_specs = [_brows(a.shape[0], tm, bw) for a in dps]
    w_specs, w_args = ([], []) if direct else ([_const(w.shape)], [w])
    return pl.pallas_call(
        body, name=name, grid=(t // tm,),
        in_specs=dp_specs + w_specs + [_rows(tm, D), vec, vec, _rows(tm, D)],
        out_specs=[_rows(tm, D), _const((8, D), single=False)],
        out_shape=[jax.ShapeDtypeStruct((t, D), F32), jax.ShapeDtypeStruct((8, D), F32)],
        compiler_params=_params(1))(*dps, *w_args, x, g, sc, dxo)


def sc_bwd_mid(dq, p, convw, *, tm, name, carry=None):
    t = dq.shape[0]

    rc, cc = 16, 256

    def body(dq_ref, dqn_ref, p_ref, pp_ref, pn_ref, cw_ref, dp_ref, st_ref, acc_ref):
        i = pl.program_id(0)
        last = i == pl.num_programs(0) - 1
        _init_stats(acc_ref)
        for c0 in range(0, D, cc):
            b_c, c_c, h_c = slice(c0, c0 + cc), slice(D + c0, D + c0 + cc), slice(2 * D + c0, 2 * D + c0 + cc)
            w0, w1, w2 = cw_ref[0:1, b_c], cw_ref[1:2, b_c], cw_ref[2:3, b_c]
            below = jnp.where(last, 0.0, dqn_ref[0:8, b_c].astype(F32) * pn_ref[0:8, b_c].astype(F32))
            above = jnp.where(i == 0, 0.0, pp_ref[8:16, c_c].astype(F32) * pp_ref[8:16, h_c].astype(F32))

            def step(r0, carry):
                rows = pl.ds(r0, rc)
                cg, hi = p_ref[rows, c_c].astype(F32), p_ref[rows, h_c].astype(F32)
                z = cg * hi
                up16 = pl.ds(pl.multiple_of(jnp.maximum(r0 - 16, 0), 16), 16)
                zprev = jnp.where(r0 == 0, above, (p_ref[up16, c_c].astype(F32) * p_ref[up16, h_c].astype(F32))[8:16])
                zext = jnp.concatenate([zprev, z], axis=0)
                u = w0 * zext[6:6 + rc] + w1 * zext[7:7 + rc] + w2 * z
                dqf = dq_ref[rows, b_c].astype(F32)
                dp_ref[rows, b_c] = (dqf * u).astype(BF)
                du = dqf * p_ref[rows, b_c].astype(F32)
                ext = jnp.concatenate([du, carry], axis=0)
                e1, e2 = ext[1:rc + 1], ext[2:rc + 2]
                dz = w2 * du + w1 * e1 + w0 * e2
                dp_ref[rows, c_c] = (dz * hi).astype(BF)
                dp_ref[rows, h_c] = (dz * cg).astype(BF)
                acc_ref[0:8, b_c] += _fold8(z * e2)
                acc_ref[8:16, b_c] += _fold8(z * e1)
                acc_ref[16:24, b_c] += _fold8(z * du)
                return du[0:8]

            _chunks(tm, rc, step, below, reverse=True)

        @pl.when(last)
        def _():
            for k in range(3):
                st_ref[k:k + 1, :] = jnp.sum(acc_ref[8 * k:8 * k + 8, :], axis=0, keepdims=True)
            st_ref[3:8, :] = jnp.zeros((5, D), F32)

    return _call(
        body, name=name, grid=(t // tm,),
        in_specs=[_rows(tm, D), _next(16, D, tm, t), _rows(tm, 3 * D), _prev(16, 3 * D, tm), _next(16, 3 * D, tm, t),
                  _const((3, D))],
        out_specs=[_rows(tm, 3 * D), _const((8, D), single=False)],
        out_shape=[jax.ShapeDtypeStruct((t, 3 * D), BF), jax.ShapeDtypeStruct((8, D), F32)],
        scratch_shapes=[pltpu.VMEM((24, D), F32)], args=[dq, dq, p, p, p, convw], carry=carry)


def cf_bwd_mid(ds, u2, a, w_dw, ln_g, ln_b, *, tm, name, carry=None):
    t = ds.shape[0]
    hb = 32

    def du2_of(dsv, u2v, lg, lb):
        xh, rstd = _layernorm_parts(u2v)
        l = xh * lg + lb
        sg = _sigmoid(l)
        dl = dsv * (sg * (1.0 + l * (1.0 - sg)))
        dxh = dl * lg
        du2 = rstd * (dxh - jnp.mean(dxh, axis=-1, keepdims=True) - xh * jnp.mean(dxh * xh, axis=-1, keepdims=True))
        return du2, dl, xh

    rc, cc = 32, 256

    def body(ds_ref, dsn_ref, u2_ref, u2n_ref, a_ref, wd_ref, lg_ref, lb_ref, da_ref, st_ref, s1_ref, acc_ref):
        i = pl.program_id(0)
        last = i == pl.num_programs(0) - 1
        _init_stats(st_ref)
        _init_stats(acc_ref)
        lg, lb = lg_ref[...], lb_ref[...]
        du2, dl, xh = du2_of(ds_ref[...].astype(F32), u2_ref[...].astype(F32), lg, lb)
        st_ref[32:33, :] += _rsum(dl * xh)
        st_ref[33:34, :] += _rsum(dl)
        st_ref[31:32, :] += _rsum(du2)
        du2n, _, _ = du2_of(dsn_ref[...].astype(F32), u2n_ref[...].astype(F32), lg, lb)
        s1_ref[0, 0:tm, :] = du2
        s1_ref[0, tm:tm + hb, :] = jnp.where(last, 0.0, du2n)
        _row_shifted_copies(s1_ref, tm + hb - 8)

        def taps(r0, _):
            rows = pl.ds(r0, rc)
            for c0 in range(0, D, cc):
                sg = _sigmoid(a_ref[rows, D + c0:D + c0 + cc].astype(F32))
                u = a_ref[rows, c0:c0 + cc].astype(F32) * sg
                du = jnp.zeros((rc, cc), F32)
                for k in range(CFW):
                    o = CFW - 1 - k
                    sh = s1_ref[o % 8, pl.ds(pl.multiple_of(r0 + 8 * (o // 8), 8), rc), c0:c0 + cc]
                    du = du + wd_ref[k:k + 1, c0:c0 + cc] * sh
                    acc_ref[8 * k:8 * k + 8, c0:c0 + cc] += _fold8(u * sh)
                dav = du * sg
                dgv = du * u * (1.0 - sg)
                acc_ref[8 * CFW:8 * CFW + 8, c0:c0 + cc] += _fold8(dav)
                acc_ref[8 * CFW + 8:8 * CFW + 16, c0:c0 + cc] += _fold8(dgv)
                da_ref[rows, c0:c0 + cc] = dav.astype(BF)
                da_ref[rows, D + c0:D + c0 + cc] = dgv.astype(BF)
            return 0

        _chunks(tm, rc, taps)

        @pl.when(last)
        def _():
            for k in range(CFW):
                st_ref[k:k + 1, :] = jnp.sum(acc_ref[8 * k:8 * k + 8, :], axis=0, keepdims=True)
            st_ref[34:35, :] = jnp.sum(acc_ref[8 * CFW:8 * CFW + 8, :], axis=0, keepdims=True)
            st_ref[35:36, :] = jnp.sum(acc_ref[8 * CFW + 8:8 * CFW + 16, :], axis=0, keepdims=True)

    vec = _const((1, D))
    return _call(
        body, name=name, grid=(t // tm,),
        in_specs=[_rows(tm, D), _next(hb, D, tm, t), _rows(tm, D), _next(hb, D, tm, t), _rows(tm, 2 * D),
                  _const((CFW, D)), vec, vec],
        out_specs=[_rows(tm, 2 * D), _const((40, D), single=False)],
        out_shape=[jax.ShapeDtypeStruct((t, 2 * D), BF), jax.ShapeDtypeStruct((40, D), F32)],
        scratch_shapes=[pltpu.VMEM((8, tm + hb, D), F32), pltpu.VMEM((8 * (CFW + 2), D), F32)],
        args=[ds, ds, u2, u2, a, w_dw, ln_g, ln_b], carry=carry)


def pool_bwd(dxo, m, ypre, pw, pscale, gn, gt, *, tm, name):
    t = dxo.shape[0]
    hb = 16

    def dyp_of(dxv, mv, ypv, ps, gnv, gtv):
        dm, dgt, dgn = _gated_res_bwd(dxv, mv, gnv, gtv)
        return dm * ps, dgt, dgn, _rsum(dm * ypv)

    def body(dx_ref, dxn_ref, m_ref, mn_ref, yp_ref, ypn_ref, pw_ref, ps_ref, gn_ref, gt_ref,
             dh_ref, dyp_ref, st_ref, sa_ref, sb_ref):
        i = pl.program_id(0)
        last = i == pl.num_programs(0) - 1
        _init_stats(st_ref)
        ps, gnv, gtv = ps_ref[...], gn_ref[...], gt_ref[...]
        dyp, dgt, dgn, dps = dyp_of(dx_ref[...], m_ref[...].astype(F32), yp_ref[...].astype(F32), ps, gnv, gtv)
        st_ref[0:1, :] += dgt
        st_ref[1:2, :] += dgn
        st_ref[2:3, :] += dps
        st_ref[3:4, :] += _rsum(dyp)
        dypb = dyp.astype(BF)
        dyp_ref[...] = dypb
        dypn, _, _, _ = dyp_of(dxn_ref[...], mn_ref[...].astype(F32), ypn_ref[...].astype(F32), ps, gnv, gtv)
        dypnb = jnp.where(last, 0.0, dypn).astype(BF)
        dpo = []
        for gi, w in enumerate(POOL_WINDOWS):
            c0 = gi * PG
            dp_main = _dot_nt(dypb[:, c0:c0 + PG], pw_ref[gi])
            dp_next = _dot_nt(dypnb[:, c0:c0 + PG], pw_ref[gi])
            dpo.append(dp_main)
            sa_ref[0:tm, c0:c0 + PG] = dp_main / _pool_counts(i, tm, w)
            sa_ref[tm:tm + hb, c0:c0 + PG] = dp_next / float(w)
        zero = jnp.zeros((8, D), F32)
        sa_ref[tm + hb:tm + hb + 8, :] = zero
        sb_ref[tm + hb:tm + hb + 8, :] = zero
        n = tm + hb
        src, dst = sa_ref, sb_ref
        for gi, w in enumerate(POOL_WINDOWS):
            c0 = gi * PG
            step = w // 2
            dst[pl.ds(0, n), c0:D] = src[pl.ds(0, n), c0:D] + src[pl.ds(step, n), c0:D]
            dh_ref[:, c0:c0 + PG] = dst[pl.ds(0, tm), c0:c0 + PG] - dpo[gi]
            src, dst = dst, src

    vec = _const((1, D))
    return pl.pallas_call(
        body, name=name, grid=(t // tm,),
        in_specs=[_rows(tm, D), _next(hb, D, tm, t), _rows(tm, D), _next(hb, D, tm, t), _rows(tm, D),
                  _next(hb, D, tm, t), _const((4, PG, PG)), vec, vec, vec],
        out_specs=[_rows(tm, D), _rows(tm, D), _const((8, D), single=False)],
        out_shape=[jax.ShapeDtypeStruct((t, D), F32), jax.ShapeDtypeStruct((t, D), BF), jax.ShapeDtypeStruct((8, D), F32)],
        scratch_shapes=[pltpu.VMEM((tm + hb + 8, D), F32)] * 2, compiler_params=_params(1),
    )(dxo, dxo, m, m, ypre, ypre, pw, pscale, gn, gt)


def ffn_bwd_mid(da, gc, up, convw, *, tm, name, carry=None):
    t = da.shape[1]

    def dgc_of(dav, gcv, valv):
        sg = _sigmoid(gcv)
        return dav * valv * (sg * (1.0 + gcv * (1.0 - sg))), dav * (gcv * sg)

    rc = 16

    def body(da_ref, dan_ref, gc_ref, gcn_ref, val_ref, valn_ref, gate_ref, cw_ref, dup_ref, st_ref, acc_ref):
        i = pl.program_id(0)
        last = i == pl.num_programs(0) - 1
        _init_stats(acc_ref)
        for j in range(4):
            w0, w1, w2 = cw_ref[j, 0:1, :], cw_ref[j, 1:2, :], cw_ref[j, 2:3, :]
            dgcn, _ = dgc_of(dan_ref[j, 0:8, :].astype(F32), gcn_ref[j, 0:8, :].astype(F32), valn_ref[j, 0:8, :].astype(F32))
            below = jnp.where(last, 0.0, dgcn)

            def step(r0, carry):
                rows = pl.ds(r0, rc)
                dgc, dval = dgc_of(da_ref[j, rows, :].astype(F32), gc_ref[j, rows, :].astype(F32), val_ref[j, rows, :].astype(F32))
                dup_ref[4 + j, rows, :] = dval.astype(BF)
                ext = jnp.concatenate([dgc, carry], axis=0)
                e1, e2 = ext[1:rc + 1], ext[2:rc + 2]
                dup_ref[j, rows, :] = (w2 * dgc + w1 * e1 + w0 * e2).astype(BF)
                gate = gate_ref[j, rows, :].astype(F32)
                acc_ref[j, 0:8, :] += _fold8(dgc)
                acc_ref[j, 8:16, :] += _fold8(gate * e2)
                acc_ref[j, 16:24, :] += _fold8(gate * e1)
                acc_ref[j, 24:32, :] += _fold8(gate * dgc)
                return dgc[0:8]

            _chunks(tm, rc, step, below, reverse=True)

        @pl.when(last)
        def _():
            for j in range(4):
                for q in range(4):
                    st_ref[j, q:q + 1, :] = jnp.sum(acc_ref[j, 8 * q:8 * q + 8, :], axis=0, keepdims=True)
                st_ref[j, 4:8, :] = jnp.zeros((4, FB), F32)

    return _call(
        body, name=name, grid=(t // tm,),
        in_specs=[_brows(4, tm, FB), _bnext(4, 16, FB, tm, t), _brows(4, tm, FB), _bnext(4, 16, FB, tm, t),
                  _brows(4, tm, FB, 1), _bnext(4, 16, FB, tm, t, 1), _brows(4, tm, FB, 0), _const((4, 3, FB))],
        out_specs=[_brows(8, tm, FB), _const((4, 8, FB), single=False)],
        out_shape=[jax.ShapeDtypeStruct((8, t, FB), BF), jax.ShapeDtypeStruct((4, 8, FB), F32)],
        scratch_shapes=[pltpu.VMEM((4, 32, FB), F32)], args=[da, da, gc, gc, up, up, up, convw], carry=carry)


def wgrad(a, b, *, nblk, a_blocked, b_blocked, bk, bn, tt, name):
    t = a.shape[1] if a.ndim == 3 else a.shape[0]
    nt = t // tt

    def body(a_ref, b_ref, o_ref, acc_ref):
        s = pl.program_id(1)

        @pl.when(s == 0)
        def _():
            acc_ref[...] = jnp.zeros_like(acc_ref)

        av = a_ref[0] if a.ndim == 3 else a_ref[...]
        bv = b_ref[0] if b.ndim == 3 else b_ref[...]
        acc_ref[...] += _dot_tn(av, bv)

        @pl.when(s == nt - 1)
        def _():
            o_ref[0] = acc_ref[...].astype(BF)

    def spec(arr, blocked, width):
        if arr.ndim == 3:
            return pl.BlockSpec((1, tt, width), lambda j, s: (j, s, 0))
        if blocked:
            return pl.BlockSpec((tt, width), lambda j, s: (s, j))
        return pl.BlockSpec((tt, width), lambda j, s: (s, 0))

    return pl.pallas_call(
        body, name=name, grid=(nblk, nt), in_specs=[spec(a, a_blocked, bk), spec(b, b_blocked, bn)],
        out_specs=pl.BlockSpec((1, bk, bn), lambda j, s: (j, 0, 0)),
        out_shape=jax.ShapeDtypeStruct((nblk, bk, bn), BF),
        scratch_shapes=[pltpu.VMEM((bk, bn), F32)], compiler_params=_params(2))(a, b)


def mod_partial(c_all, w_mod):
    cols = w_mod.shape[2]

    def body(c_ref, w_ref, o_ref):
        c = c_ref[...]
        ca = c * _sigmoid(c)
        o_ref[0] = jnp.dot(ca, w_ref[0], preferred_element_type=F32, precision=lax.Precision.HIGHEST)

    return pl.pallas_call(
        body, name="mod_partial", grid=(DEPTH,),
        in_specs=[pl.BlockSpec((NDEV, D), lambda l: (0, 0)), pl.BlockSpec((1, D, cols), lambda l: (l, 0, 0))],
        out_specs=pl.BlockSpec((1, NDEV, cols), lambda l: (l, 0, 0)),
        out_shape=jax.ShapeDtypeStruct((DEPTH, NDEV, cols), F32), compiler_params=_params(1))(c_all, w_mod)


def mod_finish(parts, b_mod):
    cols = parts.shape[2]

    def body(p_ref, b_ref, o_ref):
        for e in range(NDEV):
            o_ref[:, e * cols:(e + 1) * cols] = p_ref[e] + b_ref[:, e * cols:(e + 1) * cols]

    return pl.pallas_call(
        body, name="mod_finish", out_shape=jax.ShapeDtypeStruct((DEPTH, NDEV * cols), F32))(parts, b_mod)


def sum_parts(parts):
    n, r, c = parts.shape

    def body(p_ref, o_ref):
        acc = p_ref[0]
        for j in range(1, n):
            acc = acc + p_ref[j]
        o_ref[...] = acc

    return pl.pallas_call(body, name="sum_parts", out_shape=jax.ShapeDtypeStruct((r, c), F32))(parts)


def mod_wgrad(c_all_t, gmod_cols):
    cols = gmod_cols.shape[2]

    def body(c_ref, g_ref, o_ref):
        c = c_ref[...]
        ca = c * _sigmoid(c)
        acc = ca[:, 0:1] * g_ref[0, 0:1, :]
        for b in range(1, NDEV):
            acc = acc + ca[:, b:b + 1] * g_ref[0, b:b + 1, :]
        o_ref[0] = acc

    return pl.pallas_call(
        body, name="mod_wgrad", grid=(DEPTH,),
        in_specs=[pl.BlockSpec((D, NDEV), lambda l: (0, 0)), pl.BlockSpec((1, NDEV, cols), lambda l: (l, 0, 0))],
        out_specs=pl.BlockSpec((1, D, cols), lambda l: (l, 0, 0)),
        out_shape=jax.ShapeDtypeStruct((DEPTH, D, cols), F32), compiler_params=_params(1))(c_all_t, gmod_cols)


def _adamw_math(g, w, m, v):
    m2 = B1 * m + (1.0 - B1) * g
    v2 = B2 * v + (1.0 - B2) * (g * g)
    m_hat = m2 / (1.0 - B1 ** STEP)
    v_hat = v2 / (1.0 - B2 ** STEP)
    delta = -LR * (m_hat / (jnp.sqrt(v_hat) + ADAM_EPS) + WD * w)
    return delta, m2, v2


def _row_tile(r, c, budget=1 << 18):
    if r * c <= budget or r % 8:
        return r
    best = 8
    for cand in range(8, r + 1, 8):
        if r % cand == 0 and cand * c <= budget:
            best = cand
    return best


def adamw_sum(parts, w, m, v, *, name):
    n, r, c = parts.shape
    tr = _row_tile(r, c)

    def body(p_ref, w_ref, m_ref, v_ref, g_ref, d_ref, m2_ref, v2_ref):
        g = p_ref[0].astype(F32)
        for j in range(1, n):
            g = g + p_ref[j].astype(F32)
        d, m2, v2 = _adamw_math(g, w_ref[...], m_ref[...], v_ref[...])
        g_ref[...] = g
        d_ref[...] = d
        m2_ref[...] = m2
        v2_ref[...] = v2

    blk = pl.BlockSpec((tr, c), lambda i: (i, 0))
    out = jax.ShapeDtypeStruct((r, c), F32)
    return pl.pallas_call(
        body, name=name, grid=(r // tr,), in_specs=[pl.BlockSpec((n, tr, c), lambda i: (0, i, 0)), blk, blk, blk],
        out_specs=[blk] * 4, out_shape=[out] * 4, compiler_params=_params(1))(parts, w, m, v)


def adamw_layer(parts, w, m, v, prev, layer, *, name):
    n, r, c = parts.shape
    nl = w.shape[0]
    tr = _row_tile(r, c)

    def body(p_ref, w_ref, m_ref, v_ref, *rest):
        g_ref, d_ref, m2_ref, v2_ref = rest[-4:]
        g = p_ref[0].astype(F32)
        for j in range(1, n):
            g = g + p_ref[j].astype(F32)
        d, m2, v2 = _adamw_math(g, w_ref[0], m_ref[0], v_ref[0])
        g_ref[0] = g
        d_ref[0] = d
        m2_ref[0] = m2
        v2_ref[0] = v2

    blk = pl.BlockSpec((1, tr, c), lambda i: (layer, i, 0))
    in_specs = [pl.BlockSpec((n, tr, c), lambda i: (0, i, 0)), blk, blk, blk]
    args = [parts, w, m, v]
    aliases = {}
    if prev is not None:
        in_specs += [ANY] * 4
        args += list(prev)
        aliases = {4 + k: k for k in range(4)}
    out = jax.ShapeDtypeStruct((nl, r, c), F32)
    return pl.pallas_call(
        body, name=name, grid=(r // tr,), in_specs=in_specs, out_specs=[blk] * 4, out_shape=[out] * 4,
        input_output_aliases=aliases, compiler_params=_params(1))(*args)


def _pack(arrays):
    flat, layout, off = [], [], 0
    for a in arrays:
        flat.append(a.reshape(-1))
        layout.append((off, a.shape))
        off += a.size
    pad = (-off) % 1024
    if pad:
        flat.append(jnp.zeros((pad,), F32))
    return jnp.concatenate(flat).reshape(-1, 128), layout


def _unpack(packed, layout, lead=()):
    flat = packed.reshape(lead + (-1,))
    return [flat[..., off:off + _size(shape)].reshape(lead + tuple(shape)) for off, shape in layout]


def _size(shape):
    n = 1
    for s in shape:
        n *= s
    return n


def _join_last(g):
    g = jnp.moveaxis(g, 0, -2)
    return g.reshape(g.shape[:-2] + (g.shape[-2] * g.shape[-1],))


def _my_cols(a, width):
    return lax.dynamic_slice_in_dim(a, _my_id() * width, width, axis=a.ndim - 1)


def _tile(t, pref):
    return min(pref, t)


def kernel(x, c, w_mod, b_mod, norm_g, sc_w_in, sc_conv, sc_w_out, pool_w, pool_b, pool_scale, cf_w_pw1, cf_b_pw1, cf_w_dw, cf_b_dw, cf_ln_g, cf_ln_b, cf_w_pw2, cf_b_pw2, ffn_w_up, ffn_conv, ffn_b_conv, ffn_w_down, loss_target, m_w_mod, m_b_mod, m_norm_g, m_sc_w_in, m_sc_conv, m_sc_w_out, m_pool_w, m_pool_b, m_pool_scale, m_cf_w_pw1, m_cf_b_pw1, m_cf_w_dw, m_cf_b_dw, m_cf_ln_g, m_cf_ln_b, m_cf_w_pw2, m_cf_b_pw2, m_ffn_w_up, m_ffn_conv, m_ffn_b_conv, m_ffn_w_down, v_w_mod, v_b_mod, v_norm_g, v_sc_w_in, v_sc_conv, v_sc_w_out, v_pool_w, v_pool_b, v_pool_scale, v_cf_w_pw1, v_cf_b_pw1, v_cf_w_dw, v_cf_b_dw, v_cf_ln_g, v_cf_ln_b, v_cf_w_pw2, v_cf_b_pw2, v_ffn_w_up, v_ffn_conv, v_ffn_b_conv, v_ffn_w_down):
    env = dict(locals())
    names = ["w_mod", "b_mod", "norm_g", "sc_w_in", "sc_conv", "sc_w_out", "pool_w", "pool_b", "pool_scale", "cf_w_pw1",
             "cf_b_pw1", "cf_w_dw", "cf_b_dw", "cf_ln_g", "cf_ln_b", "cf_w_pw2", "cf_b_pw2", "ffn_w_up", "ffn_conv",
             "ffn_b_conv", "ffn_w_down"]
    t = x.shape[1]
    tm = _tile(t, 512)
    tm_ffn = _tile(t, 256)
    tt = _tile(t, 2048)
    x0, target = x[0], loss_target[0]

    small_names = ["norm_g", "sc_conv", "cf_b_pw1", "cf_w_dw", "cf_b_dw", "cf_ln_g", "cf_ln_b", "cf_b_pw2", "ffn_conv"]
    packed, layout = _pack([c] + [env[n] for n in small_names])

    shard = {"pool": pool_w[0].astype(BF), "pw1": cf_w_pw1[0].astype(BF), "pw2": cf_w_pw2[0].astype(BF)}
    for j in range(2):
        shard[f"in{j}"], shard[f"out{j}"] = sc_w_in[j].astype(BF), sc_w_out[j].astype(BF)
    for l in range(DEPTH):
        shard[f"up{l}"], shard[f"down{l}"] = ffn_w_up[l].T.astype(BF), ffn_w_down[l].astype(BF)
    gathered, g_in0, g_out0 = _Exchange("gather", [packed, shard["in0"], shard["out0"]]).run("gather_first")
    wg = {"in0": g_in0, "out0": g_out0}
    parts = _unpack(gathered, layout, lead=(NDEV,))
    c_all = parts[0].reshape(NDEV, D)
    full = {n: _join_last(p) for n, p in zip(small_names, parts[1:])}
    fwd_plan = {("mix_in", 0): ["up0"], ("mix_out", 0): ["down0"], ("ffn_in", 0): ["pool", "up1"],
                ("ffn_out", 0): ["down1", "pw1", "pw2"], ("ffn_in", 1): ["up2"], ("ffn_out", 1): ["down2", "in1", "out1"],
                ("ffn_in", 2): ["up3"], ("ffn_out", 2): ["down3"]}

    def carrying(plan, kind, store, source, fn, key, *a, **k):
        names = plan.get(key)
        if not names:
            return fn(*a, **k)
        res = fn(*a, carry=_Exchange(kind, [source[n] for n in names]), **k)
        store.update(zip(names, res[-1]))
        return res[:-1]

    fwd = functools.partial(carrying, fwd_plan, "gather", wg, shard)

    mp = mod_partial(c_all, w_mod)
    (mod_parts,) = _Exchange("scatter", [jnp.swapaxes(mp, 0, 1)]).run("exchange_mod")
    mod = mod_finish(mod_parts, b_mod)

    def vec(a):
        return a.reshape(1, -1)

    def col_blocks(g):
        w = jnp.swapaxes(g, 0, 1).reshape(D, -1)
        return jnp.swapaxes(w.reshape(D, -1, D), 0, 1)

    def ffn_blocks(a):
        return jnp.swapaxes(a.reshape(a.shape[0], 4, FB), 0, 1)

    saved = []
    xs = x0
    for l in range(DEPTH):
        sh1, sc1, g1, sh2, sc2, g2 = [mod[l:l + 1, k * D:(k + 1) * D] for k in range(6)]
        ng = [full["norm_g"][l, k:k + 1] for k in range(4)]
        kind, j = l % 3, l // 3
        s = dict(x_in=xs, sc1=sc1, g1=g1, sc2=sc2, g2=g2, ng=ng)
        if kind == 0:
            s["w_in"] = col_blocks(wg[f"in{j}"])
            s["h"], s["p"] = fwd(fwd_in, ("mix_in", l), xs, ng[0], sc1, sh1, s["w_in"], None, blocked=False, tm=tm,
                                 name=f"sc_in_{l}")
            x1, s["m"], s["q"] = fwd(sc_fwd_out, ("mix_out", l), s["p"], full["sc_conv"][j], wg[f"out{j}"].reshape(D, D), xs,
                                     ng[1], g1, tm=tm, name=f"sc_out_{l}")
        elif kind == 1:
            pool_w_f = jnp.swapaxes(wg["pool"], 0, 1).reshape(4, PG, PG)
            x1, s["m"], s["ypre"], s["pooled"] = pool_fwd(xs, ng[0], sc1, sh1, pool_w_f, pool_b, pool_scale, ng[1], g1,
                                                          tm=tm, name=f"pool_{l}")
        else:
            s["w_in"] = col_blocks(wg["pw1"])
            s["h"], s["a"] = fwd_in(xs, ng[0], sc1, sh1, s["w_in"], full["cf_b_pw1"].reshape(2, 1, D), blocked=False, tm=tm,
                                    name=f"cf_in_{l}")
            x1, s["m"], s["s"], s["u2"] = cf_fwd_out(s["a"], full["cf_w_dw"][0], full["cf_b_dw"], full["cf_ln_g"],
                                                     full["cf_ln_b"], wg["pw2"].reshape(D, D), full["cf_b_pw2"], xs, ng[1],
                                                     g1, tm=tm, name=f"cf_out_{l}")
        s["x1"] = x1
        s["cw"] = ffn_blocks(full["ffn_conv"][l])
        s["h2"], s["up"] = fwd(fwd_in, ("ffn_in", l), x1, ng[2], sc2, sh2, wg[f"up{l}"], None, blocked=True, wt=True, tm=tm,
                               name=f"ffn_in_{l}")
        xs, s["f"], s["gc"], s["fa"] = fwd(ffn_fwd_out, ("ffn_out", l), s["up"], s["cw"], ffn_blocks(ffn_b_conv[l:l + 1]),
                                           wg[f"down{l}"].reshape(F, D), x1, ng[3], g2, tm=tm_ffn, name=f"ffn_out_{l}")
        saved.append(s)

    dx, loss_part = loss_head(xs, target, tm=tm, name="loss_head")
    loss = lax.psum(loss_part[0, 0], ("x", "y", "c"))

    gmod = [None] * DEPTH
    d_norm_g = [None] * DEPTH
    d_ffn_conv = [None] * DEPTH
    d_ffn_b_conv = [None] * DEPTH
    d_sc_conv = [None] * 2
    big = {}
    got = {}
    small_g = {}
    bwd_plan = {("mix_bout", 3): ["down3"], ("mix_bmid", 3): ["up3"], ("ffn_bout", 2): ["in1", "out1"],
                ("mix_bmid", 2): ["up2", "down2"], ("ffn_bout", 1): ["pw1", "pw2"], ("ffn_bout", 0): ["pool", "down1"],
                ("ffn_bmid", 0): ["up1"], ("mix_bout", 0): ["down0"], ("mix_bmid", 0): ["up0"]}
    bwd = functools.partial(carrying, bwd_plan, "scatter", got, big)
    pool_w_f = jnp.swapaxes(wg["pool"], 0, 1).reshape(4, PG, PG)
    for l in reversed(range(DEPTH)):
        s = saved[l]
        ng = s["ng"]
        kind, j = l % 3, l // 3
        df, da, st_o = bwd(bwd_out, ("ffn_bout", l), dx, s["f"], ng[3], s["g2"], wg[f"down{l}"].reshape(F, D), blocked=True,
                           tm=tm, name=f"ffn_bout_{l}")
        dup, st_c = bwd(ffn_bwd_mid, ("ffn_bmid", l), da, s["gc"], s["up"], s["cw"], tm=tm, name=f"ffn_bmid_{l}")
        dx1, st_i = bwd_in([dup], wg[f"up{l}"], s["x1"], ng[2], s["sc2"], dx, wt=True, tm=tm, name=f"ffn_bin_{l}")
        big[f"up{l}"] = wgrad(dup, s["h2"], nblk=NDEV, a_blocked=True, b_blocked=False, bk=FB, bn=D, tt=tt, name=f"ffn_wup_{l}")
        big[f"down{l}"] = wgrad(s["fa"], df, nblk=4, a_blocked=True, b_blocked=False, bk=FB, bn=D, tt=tt,
                                name=f"ffn_wdown_{l}").reshape(NDEV, F // NDEV, D)
        d_ffn_b_conv[l] = st_c[:, 0, :].reshape(F)
        d_ffn_conv[l] = jnp.swapaxes(st_c[:, 1:4, :], 0, 1).reshape(3, F)
        g_ffn = [st_i[0], st_i[1], st_o[0]]
        dn3, dn2 = st_o[1], st_i[2]
        if kind == 0:
            dm, dq, st_o = bwd(bwd_out, ("mix_bout", l), dx1, s["m"], ng[1], s["g1"], wg[f"out{j}"].reshape(D, D), blocked=False,
                               tm=tm, name=f"sc_bout_{l}")
            dp, st_c = bwd(sc_bwd_mid, ("mix_bmid", l), dq, s["p"], full["sc_conv"][j], tm=tm, name=f"sc_bmid_{l}")
            dx, st_i = bwd_in([dp], s["w_in"], s["x_in"], ng[0], s["sc1"], dx1, tm=tm, name=f"sc_bin_{l}")
            big[f"in{j}"] = wgrad(s["h"], dp, nblk=NDEV, a_blocked=False, b_blocked=True, bk=D, bn=3 * D // NDEV, tt=tt,
                                  name=f"sc_win_{l}")
            big[f"out{j}"] = wgrad(s["q"], dm, nblk=1, a_blocked=False, b_blocked=False, bk=D, bn=D, tt=tt,
                                   name=f"sc_wout_{l}").reshape(NDEV, D // NDEV, D)
            d_sc_conv[j] = st_c[0:3]
        elif kind == 1:
            dh, dyp, st_o = pool_bwd(dx1, s["m"], s["ypre"], pool_w_f, pool_scale, ng[1], s["g1"], tm=tm, name=f"pool_b_{l}")
            dx, st_i = bwd_in([dh], None, s["x_in"], ng[0], s["sc1"], dx1, tm=tm, name=f"pool_bin_{l}")
            dpw = wgrad(s["pooled"], dyp, nblk=4, a_blocked=True, b_blocked=True, bk=PG, bn=PG, tt=tt, name=f"pool_w_{l}")
            big["pool"] = jnp.swapaxes(dpw.reshape(4, NDEV, PG // NDEV, PG), 0, 1).reshape(NDEV, 4 * PG // NDEV, PG)
            small_g["pool_scale"], small_g["pool_b"] = st_o[2:3], st_o[3:4]
        else:
            dm, ds, st_o = bwd_out(dx1, s["m"], ng[1], s["g1"], wg["pw2"].reshape(D, D), blocked=False, tm=tm,
                                   name=f"cf_bout_{l}")
            dA, st_c = bwd(cf_bwd_mid, ("mix_bmid", l), ds, s["u2"], s["a"], full["cf_w_dw"][0], full["cf_ln_g"],
                           full["cf_ln_b"], tm=tm, name=f"cf_bmid_{l}")
            dx, st_i = bwd_in([dA], s["w_in"], s["x_in"], ng[0], s["sc1"], dx1, tm=tm, name=f"cf_bin_{l}")
            big["pw1"] = wgrad(s["h"], dA, nblk=NDEV, a_blocked=False, b_blocked=True, bk=D, bn=2 * D // NDEV, tt=tt,
                               name=f"cf_wpw1_{l}")
            big["pw2"] = wgrad(s["s"], dm, nblk=1, a_blocked=False, b_blocked=False, bk=D, bn=D, tt=tt,
                               name=f"cf_wpw2_{l}").reshape(NDEV, D // NDEV, D)
            small_g["cf_w_dw"] = st_c[0:CFW][None]
            small_g["cf_b_dw"], small_g["cf_ln_g"], small_g["cf_ln_b"] = st_c[31:32], st_c[32:33], st_c[33:34]
            small_g["cf_b_pw1"] = st_c[34:36].reshape(1, 2 * D)
            small_g["cf_b_pw2"] = st_o[2:3]
        gmod[l] = jnp.concatenate([st_i[0], st_i[1], st_o[0]] + g_ffn)
        d_norm_g[l] = jnp.stack([st_i[2], st_o[1], dn2, dn3])

    small_g["gmod"] = jnp.stack(gmod)
    small_g["norm_g"] = jnp.stack(d_norm_g)
    small_g["sc_conv"] = jnp.stack(d_sc_conv)
    small_g["ffn_conv"] = jnp.stack(d_ffn_conv)
    small_g["ffn_b_conv"] = jnp.stack(d_ffn_b_conv)
    sg_names = ["gmod", "norm_g", "sc_conv", "pool_b", "pool_scale", "cf_b_pw1", "cf_w_dw", "cf_b_dw", "cf_ln_g", "cf_ln_b",
                "cf_b_pw2", "ffn_conv", "ffn_b_conv"]
    gpacked, glayout = _pack([small_g[n] for n in sg_names])
    (ggath,), (got["in0"], got["out0"]) = _run_exchanges(
        [_Exchange("gather", [gpacked]), _Exchange("scatter", [big["in0"], big["out0"]])], "exchange_last")
    gsum = dict(zip(sg_names, _unpack(sum_parts(ggath), glayout)))
    gmod_all = _unpack(ggath, glayout[:1], lead=(NDEV,))[0]
    grads = {"b_mod": gsum["gmod"], "pool_b": gsum["pool_b"], "pool_scale": gsum["pool_scale"],
             "ffn_b_conv": gsum["ffn_b_conv"]}
    for n in ["norm_g", "sc_conv", "cf_b_pw1", "cf_w_dw", "cf_b_dw", "cf_ln_g", "cf_ln_b", "cf_b_pw2", "ffn_conv"]:
        grads[n] = _my_cols(gsum[n], env[n].shape[-1])
    grads["w_mod"] = mod_wgrad(c_all.T, jnp.swapaxes(_my_cols(gmod_all, w_mod.shape[2]), 0, 1))

    deltas, new_m, new_v = {}, {}, {}
    sp_names = ["b_mod", "norm_g", "sc_conv", "pool_b", "pool_scale", "cf_b_pw1", "cf_w_dw", "cf_b_dw", "cf_ln_g", "cf_ln_b",
                "cf_b_pw2", "ffn_conv", "ffn_b_conv"]
    pg, playout = _pack([grads[n] for n in sp_names])
    pw_, _ = _pack([env[n] for n in sp_names])
    pm_, _ = _pack([env["m_" + n] for n in sp_names])
    pv_, _ = _pack([env["v_" + n] for n in sp_names])
    _, sd, sm, sv = adamw_sum(pg[None], pw_, pm_, pv_, name="adamw_small")
    for n, d_, m_, v_ in zip(sp_names, _unpack(sd, playout), _unpack(sm, playout), _unpack(sv, playout)):
        deltas[n], new_m[n], new_v[n] = d_, m_, v_
    gw = grads["w_mod"].reshape(1, DEPTH * D, -1)
    _, d_, m_, v_ = adamw_sum(gw, w_mod.reshape(gw.shape[1:]), m_w_mod.reshape(gw.shape[1:]), v_w_mod.reshape(gw.shape[1:]),
                              name="adamw_w_mod")
    deltas["w_mod"], new_m["w_mod"], new_v["w_mod"] = [a.reshape(w_mod.shape) for a in (d_, m_, v_)]

    groups = {"sc_w_in": ["in0", "in1"], "sc_w_out": ["out0", "out1"], "pool_w": ["pool"], "cf_w_pw1": ["pw1"],
              "cf_w_pw2": ["pw2"], "ffn_w_up": [f"up{l}" for l in range(DEPTH)], "ffn_w_down": [f"down{l}" for l in range(DEPTH)]}
    for n, layers in groups.items():
        stacked = (len(layers),) + got[layers[0]].shape[1:]
        flip = n == "ffn_w_up"
        w3 = [(jnp.swapaxes(env[p + n], 1, 2) if flip else env[p + n]).reshape(stacked) for p in ("", "m_", "v_")]
        outs = None
        for li, key in enumerate(layers):
            outs = adamw_layer(got[key], *w3, outs, li, name=f"adamw_{n}_{li}")
        grads[n], deltas[n], new_m[n], new_v[n] = [(jnp.swapaxes(a, 1, 2) if flip else a).reshape(env[n].shape) for a in outs]

    return (loss, dx[None], *[grads[n] for n in names], *[deltas[n] for n in names], *[new_m[n] for n in names],
            *[new_v[n] for n in names])
```

```python
import functools

import jax
import jax.numpy as jnp
from jax import lax
from jax.experimental import pallas as pl
from jax.experimental.pallas import tpu as pltpu

D = 1024
F = 2816
NDEV = 8
FB = F // 4
DEPTH = 4
RMS_EPS = 1e-6
LN_EPS = 1e-5
CFW = 31
POOL_WINDOWS = (2, 4, 8, 16)
PG = D // 4
LR, B1, B2, ADAM_EPS, WD, STEP = 0.001, 0.9, 0.999, 1e-08, 0.01, 10

BF = jnp.bfloat16
F32 = jnp.float32
VMEM_LIMIT_V7X = 56 * 1024 * 1024
MESH = pl.DeviceIdType.MESH
ANY = pl.BlockSpec(memory_space=pl.ANY)


def _params(n_axes):
    return pltpu.CompilerParams(dimension_semantics=("arbitrary",) * n_axes, vmem_limit_bytes=VMEM_LIMIT_V7X)


def _const(shape, single=True):
    nd = len(shape)
    if single:
        return pl.BlockSpec(shape, lambda *_: (0,) * nd, pipeline_mode=pl.Buffered(1))
    return pl.BlockSpec(shape, lambda *_: (0,) * nd)


def _rows(tm, c):
    return pl.BlockSpec((tm, c), lambda i: (i, 0))


def _brows(nb, tm, c, b0=0):
    return pl.BlockSpec((nb, tm, c), lambda i: (b0, i, 0))


def _prev(hb, c, tm):
    return pl.BlockSpec((hb, c), lambda i: (jnp.maximum(i * (tm // hb) - 1, 0), 0))


def _next(hb, c, tm, t):
    return pl.BlockSpec((hb, c), lambda i: (jnp.minimum((i + 1) * (tm // hb), t // hb - 1), 0))


def _bprev(nb, hb, c, tm, b0=0):
    return pl.BlockSpec((nb, hb, c), lambda i: (b0, jnp.maximum(i * (tm // hb) - 1, 0), 0))


def _bnext(nb, hb, c, tm, t, b0=0):
    return pl.BlockSpec((nb, hb, c), lambda i: (b0, jnp.minimum((i + 1) * (tm // hb), t // hb - 1), 0))


def _sigmoid(v):
    return 0.5 * jnp.tanh(0.5 * v) + 0.5


def _fold8(v):
    r, c = v.shape
    return jnp.sum(v.reshape(r // 8, 8, c), axis=0)


def _chunks(n_rows, rc, step, init=0, reverse=False):
    n = n_rows // rc

    def it(c, carry):
        idx = (n - 1 - c) if reverse else c
        return step(pl.multiple_of(idx * rc, rc), carry)

    return lax.fori_loop(0, n, it, init)


def _row_shifted_copies(s_ref, n):
    for b in range(1, 8):
        s_ref[b, 0:n, :] = s_ref[0, pl.ds(b, n), :]


def _shifted(s_ref, o, tm):
    return s_ref[o % 8, pl.ds(8 * (o // 8), tm), :]


def _dot(a, b):
    return jnp.dot(a, b, preferred_element_type=F32)


def _dot_nt(a, b):
    return lax.dot_general(a, b, (((1,), (1,)), ((), ())), preferred_element_type=F32)


def _dot_tn(a, b):
    return lax.dot_general(a, b, (((0,), (0,)), ((), ())), preferred_element_type=F32)


def _rsum(v):
    return jnp.sum(v, axis=0, keepdims=True)


def _adaln(x, g, sc, sh):
    r = lax.rsqrt(jnp.mean(x * x, axis=-1, keepdims=True) + RMS_EPS)
    return (x * r * g) * (1.0 + sc) + sh


def _gated_res(x, m, gn, gt):
    r = lax.rsqrt(jnp.mean(m * m, axis=-1, keepdims=True) + RMS_EPS)
    return x + gt * (m * r * gn)


def _gated_res_bwd(dxo, m, gn, gt):
    r = lax.rsqrt(jnp.mean(m * m, axis=-1, keepdims=True) + RMS_EPS)
    mh = m * r
    dgt = _rsum(dxo * (mh * gn))
    dn = dxo * gt
    dgn = _rsum(dn * mh)
    dmh = dn * gn
    dm = r * (dmh - mh * jnp.mean(dmh * mh, axis=-1, keepdims=True))
    return dm, dgt, dgn


def _my_id():
    return 4 * lax.axis_index("x") + 2 * lax.axis_index("y") + lax.axis_index("c")


def _peer(k):
    x, y, c = lax.axis_index("x"), lax.axis_index("y"), lax.axis_index("c")
    px = 1 - x if k & 4 else x
    py = 1 - y if k & 2 else y
    pc = 1 - c if k & 1 else c
    return (px, py, pc), 4 * px + 2 * py + pc


class _Exchange:
    def __init__(self, kind, arrays):
        self.gather = kind == "gather"
        self.arrays = list(arrays)
        n = len(self.arrays)
        if self.gather:
            self.out_shape = [jax.ShapeDtypeStruct((NDEV,) + a.shape, a.dtype) for a in self.arrays]
        else:
            self.out_shape = [jax.ShapeDtypeStruct(a.shape, a.dtype) for a in self.arrays]
        self.scratch = [pltpu.SemaphoreType.DMA((n * NDEV,)), pltpu.SemaphoreType.DMA((n * NDEV,)),
                        pltpu.SemaphoreType.DMA((n,))]

    def _local(self, a, src, dst, sems):
        me = _my_id()
        return pltpu.make_async_copy(src[a] if self.gather else src[a].at[me], dst[a].at[me], sems[2].at[a])

    def _remote(self, a, k, src, dst, sems, incoming):
        to, pid = _peer(k)
        me = _my_id()
        return pltpu.make_async_remote_copy(
            src_ref=src[a] if self.gather else src[a].at[pid], dst_ref=dst[a].at[pid if incoming else me],
            send_sem=sems[0].at[a * NDEV + k], recv_sem=sems[1].at[a * NDEV + k], device_id=to, device_id_type=MESH)

    def start(self, src, dst, sems):
        for a in range(len(self.arrays)):
            self._local(a, src, dst, sems).start()
        for k in range(1, NDEV):
            for a in range(len(self.arrays)):
                self._remote(a, k, src, dst, sems, False).start()

    def wait(self, src, dst, sems):
        for k in range(1, NDEV):
            for a in range(len(self.arrays)):
                self._remote(a, k, src, dst, sems, True).wait_recv()
        for k in range(1, NDEV):
            for a in range(len(self.arrays)):
                self._remote(a, k, src, dst, sems, False).wait_send()
        for a in range(len(self.arrays)):
            self._local(a, src, dst, sems).wait()

    def run(self, name):
        n = len(self.arrays)

        def body(*refs):
            src, dst, sems = refs[:n], refs[n:2 * n], refs[2 * n:]
            self.start(src, dst, sems)
            self.wait(src, dst, sems)

        return pl.pallas_call(body, name=name, in_specs=[ANY] * n, out_specs=[ANY] * n, out_shape=self.out_shape,
                              scratch_shapes=self.scratch)(*self.arrays)


def _run_exchanges(exchanges, name):
    counts = [len(e.arrays) for e in exchanges]
    n = sum(counts)

    def body(*refs):
        src, dst, sems = refs[:n], refs[n:2 * n], refs[2 * n:]
        parts, lo = [], 0
        for ei, (e, c) in enumerate(zip(exchanges, counts)):
            parts.append((e, src[lo:lo + c], dst[lo:lo + c], sems[3 * ei:3 * ei + 3]))
            lo += c
        for e, s, d, m in parts:
            e.start(s, d, m)
        for e, s, d, m in parts:
            e.wait(s, d, m)

    res = pl.pallas_call(
        body, name=name, in_specs=[ANY] * n, out_specs=[ANY] * n, out_shape=[s for e in exchanges for s in e.out_shape],
        scratch_shapes=[s for e in exchanges for s in e.scratch])(*[a for e in exchanges for a in e.arrays])
    out, lo = [], 0
    for c in counts:
        out.append(list(res[lo:lo + c]))
        lo += c
    return out


def _call(body, *, name, grid, in_specs, out_specs, out_shape, args, scratch_shapes=(), carry=None, aliases=None):
    cp = _params(len(grid))
    aliases = aliases or {}
    if carry is None:
        return tuple(pl.pallas_call(body, name=name, grid=grid, in_specs=in_specs, out_specs=out_specs, out_shape=out_shape,
                                    scratch_shapes=list(scratch_shapes), input_output_aliases=aliases,
                                    compiler_params=cp)(*args))
    n_in, n_out, n_sc, n_c = len(in_specs), len(out_specs), len(scratch_shapes), len(carry.arrays)

    def wrapped(*refs):
        ins, src = refs[:n_in], refs[n_in:n_in + n_c]
        outs = refs[n_in + n_c:n_in + n_c + n_out]
        dst = refs[n_in + n_c + n_out:n_in + 2 * n_c + n_out]
        rest = refs[n_in + 2 * n_c + n_out:]
        scr, sems = rest[:n_sc], rest[n_sc:]
        first = pl.program_id(0) == 0
        last = pl.program_id(0) == grid[0] - 1
        for ax in range(1, len(grid)):
            first = jnp.logical_and(first, pl.program_id(ax) == 0)
            last = jnp.logical_and(last, pl.program_id(ax) == grid[ax] - 1)

        @pl.when(first)
        def _():
            carry.start(src, dst, sems)

        body(*ins, *outs, *scr)

        @pl.when(last)
        def _():
            carry.wait(src, dst, sems)

    res = pl.pallas_call(
        wrapped, name=name, grid=grid, in_specs=list(in_specs) + [ANY] * n_c, out_specs=list(out_specs) + [ANY] * n_c,
        out_shape=list(out_shape) + carry.out_shape, scratch_shapes=list(scratch_shapes) + carry.scratch,
        input_output_aliases=aliases, compiler_params=cp)(*args, *carry.arrays)
    return tuple(res[:n_out]) + (list(res[n_out:]),)


def fwd_in(x, g, sc, sh, w, bias, *, blocked, tm, name, carry=None, wt=False):
    t = x.shape[0]
    nb, bw = (w.shape[0], w.shape[1]) if wt else (w.shape[0], w.shape[2])

    def body(*refs):
        if bias is None:
            x_ref, g_ref, sc_ref, sh_ref, w_ref, h_ref, p_ref = refs
        else:
            x_ref, g_ref, sc_ref, sh_ref, w_ref, b_ref, h_ref, p_ref = refs
        hb = _adaln(x_ref[...], g_ref[...], sc_ref[...], sh_ref[...]).astype(BF)
        h_ref[...] = hb
        for d in range(nb):
            y = _dot_nt(hb, w_ref[d]) if wt else _dot(hb, w_ref[d])
            if bias is not None:
                y = y + b_ref[d]
            if blocked:
                p_ref[d] = y.astype(BF)
            else:
                p_ref[:, d * bw:(d + 1) * bw] = y.astype(BF)

    vec = _const((1, D))
    in_specs = [_rows(tm, D), vec, vec, vec, _const(w.shape)]
    args = [x, g, sc, sh, w]
    if bias is not None:
        in_specs.append(_const((nb, 1, bw)))
        args.append(bias)
    if blocked:
        p_spec, p_shape = _brows(nb, tm, bw), jax.ShapeDtypeStruct((nb, t, bw), BF)
    else:
        p_spec, p_shape = _rows(tm, nb * bw), jax.ShapeDtypeStruct((t, nb * bw), BF)
    return _call(body, name=name, grid=(t // tm,), in_specs=in_specs, out_specs=[_rows(tm, D), p_spec],
                 out_shape=[jax.ShapeDtypeStruct((t, D), BF), p_shape], args=args, carry=carry)


def _conv3_from(s_ref, w, tm, lo):
    acc = w[0:1, :] * s_ref[pl.ds(lo, tm), :]
    for k in (1, 2):
        acc = acc + w[k:k + 1, :] * s_ref[pl.ds(lo + k, tm), :]
    return acc


def sc_fwd_out(p, convw, w_out, x, gn, gt, *, tm, name, carry=None):
    t = x.shape[0]

    def body(p_ref, ph_ref, cw_ref, w_ref, x_ref, gn_ref, gt_ref, x1_ref, m_ref, q_ref, s_ref):
        i = pl.program_id(0)
        zh = ph_ref[8:16, D:2 * D].astype(F32) * ph_ref[8:16, 2 * D:3 * D].astype(F32)
        s_ref[0:8, :] = jnp.where(i == 0, 0.0, zh)
        s_ref[8:8 + tm, :] = p_ref[:, D:2 * D].astype(F32) * p_ref[:, 2 * D:3 * D].astype(F32)
        u = _conv3_from(s_ref, cw_ref[...], tm, 6)
        qb = (p_ref[:, 0:D].astype(F32) * u).astype(BF)
        q_ref[...] = qb
        m = _dot(qb, w_ref[...])
        m_ref[...] = m.astype(BF)
        x1_ref[...] = _gated_res(x_ref[...], m, gn_ref[...], gt_ref[...])

    vec = _const((1, D))
    return _call(
        body, name=name, grid=(t // tm,),
        in_specs=[_rows(tm, 3 * D), _prev(16, 3 * D, tm), _const((3, D)), _const((D, D)), _rows(tm, D), vec, vec],
        out_specs=[_rows(tm, D)] * 3,
        out_shape=[jax.ShapeDtypeStruct((t, D), F32), jax.ShapeDtypeStruct((t, D), BF), jax.ShapeDtypeStruct((t, D), BF)],
        scratch_shapes=[pltpu.VMEM((tm + 8, D), F32)], args=[p, p, convw, w_out, x, gn, gt], carry=carry)


def _layernorm_parts(u2):
    mu = jnp.mean(u2, axis=-1, keepdims=True)
    cen = u2 - mu
    rstd = lax.rsqrt(jnp.mean(cen * cen, axis=-1, keepdims=True) + LN_EPS)
    return cen * rstd, rstd


def cf_fwd_out(a, w_dw, b_dw, ln_g, ln_b, w_pw2, b_pw2, x, gn, gt, *, tm, name):
    t = x.shape[0]
    hb = 32

    def body(a_ref, ah_ref, wd_ref, bd_ref, lg_ref, lb_ref, w_ref, b2_ref, x_ref, gn_ref, gt_ref,
             x1_ref, m_ref, s_out_ref, u2_ref, s_ref):
        i = pl.program_id(0)
        uh = ah_ref[:, 0:D].astype(F32) * _sigmoid(ah_ref[:, D:2 * D].astype(F32))
        s_ref[0, 0:hb, :] = jnp.where(i == 0, 0.0, uh)
        s_ref[0, hb:hb + tm, :] = a_ref[:, 0:D].astype(F32) * _sigmoid(a_ref[:, D:2 * D].astype(F32))
        _row_shifted_copies(s_ref, tm + hb - 8)
        acc = bd_ref[...] + wd_ref[0:1, :] * _shifted(s_ref, hb - CFW + 1, tm)
        for k in range(1, CFW):
            acc = acc + wd_ref[k:k + 1, :] * _shifted(s_ref, hb - CFW + 1 + k, tm)
        u2_ref[...] = acc.astype(BF)
        xh, _ = _layernorm_parts(acc)
        l = xh * lg_ref[...] + lb_ref[...]
        sb = (l * _sigmoid(l)).astype(BF)
        s_out_ref[...] = sb
        m = _dot(sb, w_ref[...]) + b2_ref[...]
        m_ref[...] = m.astype(BF)
        x1_ref[...] = _gated_res(x_ref[...], m, gn_ref[...], gt_ref[...])

    vec = _const((1, D))
    return pl.pallas_call(
        body, name=name, grid=(t // tm,),
        in_specs=[_rows(tm, 2 * D), _prev(hb, 2 * D, tm), _const((CFW, D)), vec, vec, vec, _const((D, D)), vec,
                  _rows(tm, D), vec, vec],
        out_specs=[_rows(tm, D)] * 4,
        out_shape=[jax.ShapeDtypeStruct((t, D), F32)] + [jax.ShapeDtypeStruct((t, D), BF)] * 3,
        scratch_shapes=[pltpu.VMEM((8, tm + hb, D), F32)], compiler_params=_params(1),
    )(a, a, w_dw, b_dw, ln_g, ln_b, w_pw2, b_pw2, x, gn, gt)


def _pool_counts(i, tm, w):
    row = lax.broadcasted_iota(jnp.int32, (tm, 1), 0) + i * tm
    return jnp.minimum(row + 1, w).astype(F32)


def pool_fwd(x, g, sc, sh, pw, pb, pscale, gn, gt, *, tm, name):
    t = x.shape[0]
    pad, hb = 8, 16
    base = pad + hb

    def body(x_ref, xh_ref, g_ref, sc_ref, sh_ref, pw_ref, pb_ref, ps_ref, gn_ref, gt_ref,
             x1_ref, m_ref, yp_ref, po_ref, sa_ref, sb_ref):
        i = pl.program_id(0)
        hh = _adaln(xh_ref[...], g_ref[...], sc_ref[...], sh_ref[...])
        h = _adaln(x_ref[...], g_ref[...], sc_ref[...], sh_ref[...])
        zero = jnp.zeros((pad, D), F32)
        sa_ref[0:pad, :] = zero
        sb_ref[0:pad, :] = zero
        sa_ref[pad:base, :] = jnp.where(i == 0, 0.0, hh)
        sa_ref[base:base + tm, :] = h
        n = hb + tm
        src, dst = sa_ref, sb_ref
        ys = []
        for gi, w in enumerate(POOL_WINDOWS):
            c0 = gi * PG
            step = w // 2
            dst[pl.ds(pad, n), c0:D] = src[pl.ds(pad, n), c0:D] + src[pl.ds(pad - step, n), c0:D]
            mean = dst[pl.ds(base, tm), c0:c0 + PG] / _pool_counts(i, tm, w)
            pooled = (mean - h[:, c0:c0 + PG]).astype(BF)
            po_ref[:, c0:c0 + PG] = pooled
            ys.append(_dot(pooled, pw_ref[gi]))
            src, dst = dst, src
        ypre = jnp.concatenate(ys, axis=1) + pb_ref[...]
        yp_ref[...] = ypre.astype(BF)
        m = ypre * ps_ref[...]
        m_ref[...] = m.astype(BF)
        x1_ref[...] = _gated_res(x_ref[...], m, gn_ref[...], gt_ref[...])

    vec = _const((1, D))
    return pl.pallas_call(
        body, name=name, grid=(t // tm,),
        in_specs=[_rows(tm, D), _prev(hb, D, tm), vec, vec, vec, _const((4, PG, PG)), vec, vec, vec, vec],
        out_specs=[_rows(tm, D)] * 4,
        out_shape=[jax.ShapeDtypeStruct((t, D), F32)] + [jax.ShapeDtypeStruct((t, D), BF)] * 3,
        scratch_shapes=[pltpu.VMEM((tm + base, D), F32)] * 2, compiler_params=_params(1),
    )(x, x, g, sc, sh, pw, pb, pscale, gn, gt)


def ffn_fwd_out(up, convw, convb, w_down, x, gn, gt, *, tm, name, carry=None):
    t = x.shape[0]

    def body(gate_ref, gh_ref, val_ref, cw_ref, cb_ref, w_ref, x_ref, gn_ref, gt_ref, x2_ref, f_ref, gc_ref, a_ref):
        i = pl.program_id(0)
        acc = jnp.zeros((tm, D), F32)
        for j in range(4):
            gate = gate_ref[j].astype(F32)
            ext = jnp.concatenate([jnp.where(i == 0, 0.0, gh_ref[j, 8:16, :].astype(F32)), gate], axis=0)
            gc = cb_ref[j] + cw_ref[j, 0:1, :] * ext[6:6 + tm] + cw_ref[j, 1:2, :] * ext[7:7 + tm] + cw_ref[j, 2:3, :] * gate
            gc_ref[j] = gc.astype(BF)
            ab = (gc * _sigmoid(gc) * val_ref[j].astype(F32)).astype(BF)
            a_ref[j] = ab
            acc = acc + _dot(ab, w_ref[j * FB:(j + 1) * FB, :])
        f_ref[...] = acc.astype(BF)
        x2_ref[...] = _gated_res(x_ref[...], acc, gn_ref[...], gt_ref[...])

    vec = _const((1, D))
    blk = jax.ShapeDtypeStruct((4, t, FB), BF)
    return _call(
        body, name=name, grid=(t // tm,),
        in_specs=[_brows(4, tm, FB, 0), _bprev(4, 16, FB, tm, 0), _brows(4, tm, FB, 1), _const((4, 3, FB)),
                  _const((4, 1, FB)), _const((F, D)), _rows(tm, D), vec, vec],
        out_specs=[_rows(tm, D), _rows(tm, D), _brows(4, tm, FB), _brows(4, tm, FB)],
        out_shape=[jax.ShapeDtypeStruct((t, D), F32), jax.ShapeDtypeStruct((t, D), BF), blk, blk],
        args=[up, up, up, convw, convb, w_down, x, gn, gt], carry=carry)


def loss_head(y, target, *, tm, name):
    t = y.shape[0]

    def body(y_ref, t_ref, dy_ref, l_ref, acc_ref):
        i = pl.program_id(0)

        @pl.when(i == 0)
        def _():
            acc_ref[...] = jnp.zeros_like(acc_ref)

        e = y_ref[...] - t_ref[...]
        dy_ref[...] = e * (1.0 / D)
        acc_ref[...] += _rsum(e * e)

        @pl.when(i == pl.num_programs(0) - 1)
        def _():
            l_ref[...] = jnp.sum(acc_ref[...], axis=1, keepdims=True) * (0.5 / D)

    return pl.pallas_call(
        body, name=name, grid=(t // tm,), in_specs=[_rows(tm, D), _rows(tm, D)],
        out_specs=[_rows(tm, D), pl.BlockSpec((1, 1), lambda i: (0, 0))],
        out_shape=[jax.ShapeDtypeStruct((t, D), F32), jax.ShapeDtypeStruct((1, 1), F32)],
        scratch_shapes=[pltpu.VMEM((1, D), F32)], compiler_params=_params(1))(y, target)


def _init_stats(ref):
    @pl.when(pl.program_id(0) == 0)
    def _():
        ref[...] = jnp.zeros_like(ref)


def bwd_out(dxo, m, gn, gt, w, *, blocked, tm, name, carry=None):
    t = dxo.shape[0]
    k = w.shape[0]

    def body(dx_ref, m_ref, gn_ref, gt_ref, w_ref, dm_ref, da_ref, st_ref):
        _init_stats(st_ref)
        dm, dgt, dgn = _gated_res_bwd(dx_ref[...], m_ref[...].astype(F32), gn_ref[...], gt_ref[...])
        st_ref[0:1, :] += dgt
        st_ref[1:2, :] += dgn
        st_ref[2:3, :] += _rsum(dm)
        dmb = dm.astype(BF)
        dm_ref[...] = dmb
        if blocked:
            for j in range(4):
                da_ref[j] = _dot_nt(dmb, w_ref[j * FB:(j + 1) * FB, :]).astype(BF)
        else:
            da_ref[...] = _dot_nt(dmb, w_ref[...]).astype(BF)

    vec = _const((1, D))
    if blocked:
        da_spec, da_shape = _brows(4, tm, FB), jax.ShapeDtypeStruct((4, t, FB), BF)
    else:
        da_spec, da_shape = _rows(tm, k), jax.ShapeDtypeStruct((t, k), BF)
    return _call(
        body, name=name, grid=(t // tm,), in_specs=[_rows(tm, D), _rows(tm, D), vec, vec, _const((k, D))],
        out_specs=[_rows(tm, D), da_spec, _const((8, D), single=False)],
        out_shape=[jax.ShapeDtypeStruct((t, D), BF), da_shape, jax.ShapeDtypeStruct((8, D), F32)],
        args=[dxo, m, gn, gt, w], carry=carry)


def bwd_in(dps, w, x, g, sc, dxo, *, tm, name, wt=False):
    t = x.shape[0]
    direct = w is None
    if not direct:
        nb, bw = (w.shape[0], w.shape[1]) if wt else (w.shape[0], w.shape[2])
    natural = (not direct) and dps[0].ndim == 2

    def body(*refs):
        n = len(dps)
        dp_refs = refs[:n]
        if direct:
            x_ref, g_ref, sc_ref, dxo_ref, dx_ref, st_ref = refs[n:]
            dh = dp_refs[0][...]
        else:
            w_ref, x_ref, g_ref, sc_ref, dxo_ref, dx_ref, st_ref = refs[n:]
            dh = jnp.zeros((tm, D), F32)
            if natural:
                for d in range(nb):
                    dh = dh + _dot_nt(dp_refs[0][:, d * bw:(d + 1) * bw], w_ref[d])
            else:
                d = 0
                for r in dp_refs:
                    for j in range(r.shape[0]):
                        dh = dh + (_dot(r[j], w_ref[d]) if wt else _dot_nt(r[j], w_ref[d]))
                        d += 1
        _init_stats(st_ref)
        x = x_ref[...]
        r = lax.rsqrt(jnp.mean(x * x, axis=-1, keepdims=True) + RMS_EPS)
        xh = x * r
        gv = g_ref[...]
        st_ref[0:1, :] += _rsum(dh)
        st_ref[1:2, :] += _rsum(dh * (xh * gv))
        dn = dh * (1.0 + sc_ref[...])
        st_ref[2:3, :] += _rsum(dn * xh)
        dy = dn * gv
        dx_ref[...] = dxo_ref[...] + r * (dy - xh * jnp.mean(dy * xh, axis=-1, keepdims=True))

    vec = _const((1, D))
    if direct:
        dp_specs = [_rows(tm, D)]
    elif natural:
        dp_specs = [_rows(tm, nb * bw)]
    else:
        dp_specs = [_brows(a.shape[0], tm, bw) for a in dps]
    w_specs, w_args = ([], []) if direct else ([_const(w.shape)], [w])
    return pl.pallas_call(
        body, name=name, grid=(t // tm,),
        in_specs=dp_specs + w_specs + [_rows(tm, D), vec, vec, _rows(tm, D)],
        out_specs=[_rows(tm, D), _const((8, D), single=False)],
        out_shape=[jax.ShapeDtypeStruct((t, D), F32), jax.ShapeDtypeStruct((8, D), F32)],
        compiler_params=_params(1))(*dps, *w_args, x, g, sc, dxo)


def sc_bwd_mid(dq, p, convw, *, tm, name, carry=None):
    t = dq.shape[0]

    rc, cc = 16, 256

    def body(dq_ref, dqn_ref, p_ref, pp_ref, pn_ref, cw_ref, dp_ref, st_ref, acc_ref):
        i = pl.program_id(0)
        last = i == pl.num_programs(0) - 1
        _init_stats(acc_ref)
        for c0 in range(0, D, cc):
            b_c, c_c, h_c = slice(c0, c0 + cc), slice(D + c0, D + c0 + cc), slice(2 * D + c0, 2 * D + c0 + cc)
            w0, w1, w2 = cw_ref[0:1, b_c], cw_ref[1:2, b_c], cw_ref[2:3, b_c]
            below = jnp.where(last, 0.0, dqn_ref[0:8, b_c].astype(F32) * pn_ref[0:8, b_c].astype(F32))
            above = jnp.where(i == 0, 0.0, pp_ref[8:16, c_c].astype(F32) * pp_ref[8:16, h_c].astype(F32))

            def step(r0, carry):
                rows = pl.ds(r0, rc)
                cg, hi = p_ref[rows, c_c].astype(F32), p_ref[rows, h_c].astype(F32)
                z = cg * hi
                up16 = pl.ds(pl.multiple_of(jnp.maximum(r0 - 16, 0), 16), 16)
                zprev = jnp.where(r0 == 0, above, (p_ref[up16, c_c].astype(F32) * p_ref[up16, h_c].astype(F32))[8:16])
                zext = jnp.concatenate([zprev, z], axis=0)
                u = w0 * zext[6:6 + rc] + w1 * zext[7:7 + rc] + w2 * z
                dqf = dq_ref[rows, b_c].astype(F32)
                dp_ref[rows, b_c] = (dqf * u).astype(BF)
                du = dqf * p_ref[rows, b_c].astype(F32)
                ext = jnp.concatenate([du, carry], axis=0)
                e1, e2 = ext[1:rc + 1], ext[2:rc + 2]
                dz = w2 * du + w1 * e1 + w0 * e2
                dp_ref[rows, c_c] = (dz * hi).astype(BF)
                dp_ref[rows, h_c] = (dz * cg).astype(BF)
                acc_ref[0:8, b_c] += _fold8(z * e2)
                acc_ref[8:16, b_c] += _fold8(z * e1)
                acc_ref[16:24, b_c] += _fold8(z * du)
                return du[0:8]

            _chunks(tm, rc, step, below, reverse=True)

        @pl.when(last)
        def _():
            for k in range(3):
                st_ref[k:k + 1, :] = jnp.sum(acc_ref[8 * k:8 * k + 8, :], axis=0, keepdims=True)
            st_ref[3:8, :] = jnp.zeros((5, D), F32)

    return _call(
        body, name=name, grid=(t // tm,),
        in_specs=[_rows(tm, D), _next(16, D, tm, t), _rows(tm, 3 * D), _prev(16, 3 * D, tm), _next(16, 3 * D, tm, t),
                  _const((3, D))],
        out_specs=[_rows(tm, 3 * D), _const((8, D), single=False)],
        out_shape=[jax.ShapeDtypeStruct((t, 3 * D), BF), jax.ShapeDtypeStruct((8, D), F32)],
        scratch_shapes=[pltpu.VMEM((24, D), F32)], args=[dq, dq, p, p, p, convw], carry=carry)


def cf_bwd_mid(ds, u2, a, w_dw, ln_g, ln_b, *, tm, name, carry=None):
    t = ds.shape[0]
    hb = 32

    def du2_of(dsv, u2v, lg, lb):
        xh, rstd = _layernorm_parts(u2v)
        l = xh * lg + lb
        sg = _sigmoid(l)
        dl = dsv * (sg * (1.0 + l * (1.0 - sg)))
        dxh = dl * lg
        du2 = rstd * (dxh - jnp.mean(dxh, axis=-1, keepdims=True) - xh * jnp.mean(dxh * xh, axis=-1, keepdims=True))
        return du2, dl, xh

    rc, cc = 32, 256

    def body(ds_ref, dsn_ref, u2_ref, u2n_ref, a_ref, wd_ref, lg_ref, lb_ref, da_ref, st_ref, s1_ref, acc_ref):
        i = pl.program_id(0)
        last = i == pl.num_programs(0) - 1
        _init_stats(st_ref)
        _init_stats(acc_ref)
        lg, lb = lg_ref[...], lb_ref[...]
        du2, dl, xh = du2_of(ds_ref[...].astype(F32), u2_ref[...].astype(F32), lg, lb)
        st_ref[32:33, :] += _rsum(dl * xh)
        st_ref[33:34, :] += _rsum(dl)
        st_ref[31:32, :] += _rsum(du2)
        du2n, _, _ = du2_of(dsn_ref[...].astype(F32), u2n_ref[...].astype(F32), lg, lb)
        s1_ref[0, 0:tm, :] = du2
        s1_ref[0, tm:tm + hb, :] = jnp.where(last, 0.0, du2n)
        _row_shifted_copies(s1_ref, tm + hb - 8)

        def taps(r0, _):
            rows = pl.ds(r0, rc)
            for c0 in range(0, D, cc):
                sg = _sigmoid(a_ref[rows, D + c0:D + c0 + cc].astype(F32))
                u = a_ref[rows, c0:c0 + cc].astype(F32) * sg
                du = jnp.zeros((rc, cc), F32)
                for k in range(CFW):
                    o = CFW - 1 - k
                    sh = s1_ref[o % 8, pl.ds(pl.multiple_of(r0 + 8 * (o // 8), 8), rc), c0:c0 + cc]
                    du = du + wd_ref[k:k + 1, c0:c0 + cc] * sh
                    acc_ref[8 * k:8 * k + 8, c0:c0 + cc] += _fold8(u * sh)
                dav = du * sg
                dgv = du * u * (1.0 - sg)
                acc_ref[8 * CFW:8 * CFW + 8, c0:c0 + cc] += _fold8(dav)
                acc_ref[8 * CFW + 8:8 * CFW + 16, c0:c0 + cc] += _fold8(dgv)
                da_ref[rows, c0:c0 + cc] = dav.astype(BF)
                da_ref[rows, D + c0:D + c0 + cc] = dgv.astype(BF)
            return 0

        _chunks(tm, rc, taps)

        @pl.when(last)
        def _():
            for k in range(CFW):
                st_ref[k:k + 1, :] = jnp.sum(acc_ref[8 * k:8 * k + 8, :], axis=0, keepdims=True)
            st_ref[34:35, :] = jnp.sum(acc_ref[8 * CFW:8 * CFW + 8, :], axis=0, keepdims=True)
            st_ref[35:36, :] = jnp.sum(acc_ref[8 * CFW + 8:8 * CFW + 16, :], axis=0, keepdims=True)

    vec = _const((1, D))
    return _call(
        body, name=name, grid=(t // tm,),
        in_specs=[_rows(tm, D), _next(hb, D, tm, t), _rows(tm, D), _next(hb, D, tm, t), _rows(tm, 2 * D),
                  _const((CFW, D)), vec, vec],
        out_specs=[_rows(tm, 2 * D), _const((40, D), single=False)],
        out_shape=[jax.ShapeDtypeStruct((t, 2 * D), BF), jax.ShapeDtypeStruct((40, D), F32)],
        scratch_shapes=[pltpu.VMEM((8, tm + hb, D), F32), pltpu.VMEM((8 * (CFW + 2), D), F32)],
        args=[ds, ds, u2, u2, a, w_dw, ln_g, ln_b], carry=carry)


def pool_bwd(dxo, m, ypre, pw, pscale, gn, gt, *, tm, name):
    t = dxo.shape[0]
    hb = 16

    def dyp_of(dxv, mv, ypv, ps, gnv, gtv):
        dm, dgt, dgn = _gated_res_bwd(dxv, mv, gnv, gtv)
        return dm * ps, dgt, dgn, _rsum(dm * ypv)

    def body(dx_ref, dxn_ref, m_ref, mn_ref, yp_ref, ypn_ref, pw_ref, ps_ref, gn_ref, gt_ref,
             dh_ref, dyp_ref, st_ref, sa_ref, sb_ref):
        i = pl.program_id(0)
        last = i == pl.num_programs(0) - 1
        _init_stats(st_ref)
        ps, gnv, gtv = ps_ref[...], gn_ref[...], gt_ref[...]
        dyp, dgt, dgn, dps = dyp_of(dx_ref[...], m_ref[...].astype(F32), yp_ref[...].astype(F32), ps, gnv, gtv)
        st_ref[0:1, :] += dgt
        st_ref[1:2, :] += dgn
        st_ref[2:3, :] += dps
        st_ref[3:4, :] += _rsum(dyp)
        dypb = dyp.astype(BF)
        dyp_ref[...] = dypb
        dypn, _, _, _ = dyp_of(dxn_ref[...], mn_ref[...].astype(F32), ypn_ref[...].astype(F32), ps, gnv, gtv)
        dypnb = jnp.where(last, 0.0, dypn).astype(BF)
        dpo = []
        for gi, w in enumerate(POOL_WINDOWS):
            c0 = gi * PG
            dp_main = _dot_nt(dypb[:, c0:c0 + PG], pw_ref[gi])
            dp_next = _dot_nt(dypnb[:, c0:c0 + PG], pw_ref[gi])
            dpo.append(dp_main)
            sa_ref[0:tm, c0:c0 + PG] = dp_main / _pool_counts(i, tm, w)
            sa_ref[tm:tm + hb, c0:c0 + PG] = dp_next / float(w)
        zero = jnp.zeros((8, D), F32)
        sa_ref[tm + hb:tm + hb + 8, :] = zero
        sb_ref[tm + hb:tm + hb + 8, :] = zero
        n = tm + hb
        src, dst = sa_ref, sb_ref
        for gi, w in enumerate(POOL_WINDOWS):
            c0 = gi * PG
            step = w // 2
            dst[pl.ds(0, n), c0:D] = src[pl.ds(0, n), c0:D] + src[pl.ds(step, n), c0:D]
            dh_ref[:, c0:c0 + PG] = dst[pl.ds(0, tm), c0:c0 + PG] - dpo[gi]
            src, dst = dst, src

    vec = _const((1, D))
    return pl.pallas_call(
        body, name=name, grid=(t // tm,),
        in_specs=[_rows(tm, D), _next(hb, D, tm, t), _rows(tm, D), _next(hb, D, tm, t), _rows(tm, D),
                  _next(hb, D, tm, t), _const((4, PG, PG)), vec, vec, vec],
        out_specs=[_rows(tm, D), _rows(tm, D), _const((8, D), single=False)],
        out_shape=[jax.ShapeDtypeStruct((t, D), F32), jax.ShapeDtypeStruct((t, D), BF), jax.ShapeDtypeStruct((8, D), F32)],
        scratch_shapes=[pltpu.VMEM((tm + hb + 8, D), F32)] * 2, compiler_params=_params(1),
    )(dxo, dxo, m, m, ypre, ypre, pw, pscale, gn, gt)


def ffn_bwd_mid(da, gc, up, convw, *, tm, name, carry=None):
    t = da.shape[1]

    def dgc_of(dav, gcv, valv):
        sg = _sigmoid(gcv)
        return dav * valv * (sg * (1.0 + gcv * (1.0 - sg))), dav * (gcv * sg)

    rc = 16

    def body(da_ref, dan_ref, gc_ref, gcn_ref, val_ref, valn_ref, gate_ref, cw_ref, dup_ref, st_ref, acc_ref):
        i = pl.program_id(0)
        last = i == pl.num_programs(0) - 1
        _init_stats(acc_ref)
        for j in range(4):
            w0, w1, w2 = cw_ref[j, 0:1, :], cw_ref[j, 1:2, :], cw_ref[j, 2:3, :]
            dgcn, _ = dgc_of(dan_ref[j, 0:8, :].astype(F32), gcn_ref[j, 0:8, :].astype(F32), valn_ref[j, 0:8, :].astype(F32))
            below = jnp.where(last, 0.0, dgcn)

            def step(r0, carry):
                rows = pl.ds(r0, rc)
                dgc, dval = dgc_of(da_ref[j, rows, :].astype(F32), gc_ref[j, rows, :].astype(F32), val_ref[j, rows, :].astype(F32))
                dup_ref[4 + j, rows, :] = dval.astype(BF)
                ext = jnp.concatenate([dgc, carry], axis=0)
                e1, e2 = ext[1:rc + 1], ext[2:rc + 2]
                dup_ref[j, rows, :] = (w2 * dgc + w1 * e1 + w0 * e2).astype(BF)
                gate = gate_ref[j, rows, :].astype(F32)
                acc_ref[j, 0:8, :] += _fold8(dgc)
                acc_ref[j, 8:16, :] += _fold8(gate * e2)
                acc_ref[j, 16:24, :] += _fold8(gate * e1)
                acc_ref[j, 24:32, :] += _fold8(gate * dgc)
                return dgc[0:8]

            _chunks(tm, rc, step, below, reverse=True)

        @pl.when(last)
        def _():
            for j in range(4):
                for q in range(4):
                    st_ref[j, q:q + 1, :] = jnp.sum(acc_ref[j, 8 * q:8 * q + 8, :], axis=0, keepdims=True)
                st_ref[j, 4:8, :] = jnp.zeros((4, FB), F32)

    return _call(
        body, name=name, grid=(t // tm,),
        in_specs=[_brows(4, tm, FB), _bnext(4, 16, FB, tm, t), _brows(4, tm, FB), _bnext(4, 16, FB, tm, t),
                  _brows(4, tm, FB, 1), _bnext(4, 16, FB, tm, t, 1), _brows(4, tm, FB, 0), _const((4, 3, FB))],
        out_specs=[_brows(8, tm, FB), _const((4, 8, FB), single=False)],
        out_shape=[jax.ShapeDtypeStruct((8, t, FB), BF), jax.ShapeDtypeStruct((4, 8, FB), F32)],
        scratch_shapes=[pltpu.VMEM((4, 32, FB), F32)], args=[da, da, gc, gc, up, up, up, convw], carry=carry)


def ffn_bwd_out(dxo, f, gn, gt, w_down, gc, up, *, tm, name, carry=None):
    t = dxo.shape[0]

    def body(dx_ref, f_ref, gn_ref, gt_ref, w_ref, gc_ref, val_ref, df_ref, dgc_ref, dval_ref, st_ref, sc_ref):
        _init_stats(st_ref)
        _init_stats(sc_ref)
        dm, dgt, dgn = _gated_res_bwd(dx_ref[...], f_ref[...].astype(F32), gn_ref[...], gt_ref[...])
        st_ref[0:1, :] += dgt
        st_ref[1:2, :] += dgn
        dmb = dm.astype(BF)
        df_ref[...] = dmb
        for j in range(4):
            da = _dot_nt(dmb, w_ref[j * FB:(j + 1) * FB, :])
            gcv = gc_ref[j].astype(F32)
            sg = _sigmoid(gcv)
            dval_ref[j] = (da * (gcv * sg)).astype(BF)
            dgc = da * val_ref[j].astype(F32) * (sg * (1.0 + gcv * (1.0 - sg)))
            dgc_ref[j] = dgc.astype(BF)
            sc_ref[j, 0:1, :] += _rsum(dgc)

    vec = _const((1, D))
    return _call(
        body, name=name, grid=(t // tm,),
        in_specs=[_rows(tm, D), _rows(tm, D), vec, vec, _const((F, D)), _brows(4, tm, FB), _brows(4, tm, FB, 1)],
        out_specs=[_rows(tm, D), _brows(4, tm, FB), _brows(4, tm, FB, 1), _const((8, D), single=False),
                   _const((4, 8, FB), single=False)],
        out_shape=[jax.ShapeDtypeStruct((t, D), BF), jax.ShapeDtypeStruct((4, t, FB), BF), jax.ShapeDtypeStruct((8, t, FB), BF),
                   jax.ShapeDtypeStruct((8, D), F32), jax.ShapeDtypeStruct((4, 8, FB), F32)],
        args=[dxo, f, gn, gt, w_down, gc, up], carry=carry)


def ffn_bwd_in(dgc, dup, up, convw, w, x, g, sc, dxo, *, tm, name, carry=None):
    t = x.shape[0]

    def body(dgc_ref, dgcn_ref, dval_ref, gate_ref, cw_ref, w_ref, x_ref, g_ref, sc_ref, dxo_ref,
             dx_ref, dgate_ref, st_ref, sc2_ref):
        last = pl.program_id(0) == pl.num_programs(0) - 1
        _init_stats(st_ref)
        _init_stats(sc2_ref)
        dh = jnp.zeros((tm, D), F32)
        for j in range(4):
            dgc = dgc_ref[j].astype(F32)
            ext = jnp.concatenate([dgc, jnp.where(last, 0.0, dgcn_ref[j, 0:8, :].astype(F32))], axis=0)
            e1, e2 = ext[1:tm + 1], ext[2:tm + 2]
            dgate = (cw_ref[j, 2:3, :] * dgc + cw_ref[j, 1:2, :] * e1 + cw_ref[j, 0:1, :] * e2).astype(BF)
            dgate_ref[j] = dgate
            dh = dh + _dot(dgate, w_ref[j])
            gate = gate_ref[j].astype(F32)
            sc2_ref[j, 1:2, :] += _rsum(gate * e2)
            sc2_ref[j, 2:3, :] += _rsum(gate * e1)
            sc2_ref[j, 3:4, :] += _rsum(gate * dgc)
        for j in range(4):
            dh = dh + _dot(dval_ref[j], w_ref[4 + j])
        x = x_ref[...]
        r = lax.rsqrt(jnp.mean(x * x, axis=-1, keepdims=True) + RMS_EPS)
        xh = x * r
        gv = g_ref[...]
        st_ref[0:1, :] += _rsum(dh)
        st_ref[1:2, :] += _rsum(dh * (xh * gv))
        dn = dh * (1.0 + sc_ref[...])
        st_ref[2:3, :] += _rsum(dn * xh)
        dy = dn * gv
        dx_ref[...] = dxo_ref[...] + r * (dy - xh * jnp.mean(dy * xh, axis=-1, keepdims=True))

    vec = _const((1, D))
    return _call(
        body, name=name, grid=(t // tm,),
        in_specs=[_brows(4, tm, FB), _bnext(4, 16, FB, tm, t), _brows(4, tm, FB, 1), _brows(4, tm, FB, 0), _const((4, 3, FB)),
                  _const(w.shape), _rows(tm, D), vec, vec, _rows(tm, D)],
        out_specs=[_rows(tm, D), _brows(4, tm, FB, 0), _const((8, D), single=False), _const((4, 8, FB), single=False)],
        out_shape=[jax.ShapeDtypeStruct((t, D), F32), jax.ShapeDtypeStruct((8, t, FB), BF), jax.ShapeDtypeStruct((8, D), F32),
                   jax.ShapeDtypeStruct((4, 8, FB), F32)],
        args=[dgc, dgc, dup, up, convw, w, x, g, sc, dxo], aliases={2: 1}, carry=carry)


def wgrad(a, b, *, nblk, a_blocked, b_blocked, bk, bn, tt, name):
    t = a.shape[1] if a.ndim == 3 else a.shape[0]
    nt = t // tt

    def body(a_ref, b_ref, o_ref, acc_ref):
        s = pl.program_id(1)

        @pl.when(s == 0)
        def _():
            acc_ref[...] = jnp.zeros_like(acc_ref)

        av = a_ref[0] if a.ndim == 3 else a_ref[...]
        bv = b_ref[0] if b.ndim == 3 else b_ref[...]
        acc_ref[...] += _dot_tn(av, bv)

        @pl.when(s == nt - 1)
        def _():
            o_ref[0] = acc_ref[...].astype(BF)

    def spec(arr, blocked, width):
        if arr.ndim == 3:
            return pl.BlockSpec((1, tt, width), lambda j, s: (j, s, 0))
        if blocked:
            return pl.BlockSpec((tt, width), lambda j, s: (s, j))
        return pl.BlockSpec((tt, width), lambda j, s: (s, 0))

    return pl.pallas_call(
        body, name=name, grid=(nblk, nt), in_specs=[spec(a, a_blocked, bk), spec(b, b_blocked, bn)],
        out_specs=pl.BlockSpec((1, bk, bn), lambda j, s: (j, 0, 0)),
        out_shape=jax.ShapeDtypeStruct((nblk, bk, bn), BF),
        scratch_shapes=[pltpu.VMEM((bk, bn), F32)], compiler_params=_params(2))(a, b)


def mod_partial(c_all, w_mod):
    cols = w_mod.shape[2]

    def body(c_ref, w_ref, o_ref):
        c = c_ref[...]
        ca = c * _sigmoid(c)
        o_ref[0] = jnp.dot(ca, w_ref[0], preferred_element_type=F32, precision=lax.Precision.HIGHEST)

    return pl.pallas_call(
        body, name="mod_partial", grid=(DEPTH,),
        in_specs=[pl.BlockSpec((NDEV, D), lambda l: (0, 0)), pl.BlockSpec((1, D, cols), lambda l: (l, 0, 0))],
        out_specs=pl.BlockSpec((1, NDEV, cols), lambda l: (l, 0, 0)),
        out_shape=jax.ShapeDtypeStruct((DEPTH, NDEV, cols), F32), compiler_params=_params(1))(c_all, w_mod)


def mod_finish(parts, b_mod):
    cols = parts.shape[2]

    def body(p_ref, b_ref, o_ref):
        for e in range(NDEV):
            o_ref[:, e * cols:(e + 1) * cols] = p_ref[e] + b_ref[:, e * cols:(e + 1) * cols]

    return pl.pallas_call(
        body, name="mod_finish", out_shape=jax.ShapeDtypeStruct((DEPTH, NDEV * cols), F32))(parts, b_mod)


def sum_parts(parts):
    n, r, c = parts.shape

    def body(p_ref, o_ref):
        acc = p_ref[0]
        for j in range(1, n):
            acc = acc + p_ref[j]
        o_ref[...] = acc

    return pl.pallas_call(body, name="sum_parts", out_shape=jax.ShapeDtypeStruct((r, c), F32))(parts)


def mod_wgrad(c_all_t, gmod_cols):
    cols = gmod_cols.shape[2]

    def body(c_ref, g_ref, o_ref):
        c = c_ref[...]
        ca = c * _sigmoid(c)
        acc = ca[:, 0:1] * g_ref[0, 0:1, :]
        for b in range(1, NDEV):
            acc = acc + ca[:, b:b + 1] * g_ref[0, b:b + 1, :]
        o_ref[0] = acc

    return pl.pallas_call(
        body, name="mod_wgrad", grid=(DEPTH,),
        in_specs=[pl.BlockSpec((D, NDEV), lambda l: (0, 0)), pl.BlockSpec((1, NDEV, cols), lambda l: (l, 0, 0))],
        out_specs=pl.BlockSpec((1, D, cols), lambda l: (l, 0, 0)),
        out_shape=jax.ShapeDtypeStruct((DEPTH, D, cols), F32), compiler_params=_params(1))(c_all_t, gmod_cols)


def _adamw_math(g, w, m, v):
    m2 = B1 * m + (1.0 - B1) * g
    v2 = B2 * v + (1.0 - B2) * (g * g)
    m_hat = m2 / (1.0 - B1 ** STEP)
    v_hat = v2 / (1.0 - B2 ** STEP)
    delta = -LR * (m_hat / (jnp.sqrt(v_hat) + ADAM_EPS) + WD * w)
    return delta, m2, v2


def _row_tile(r, c, budget=1 << 18):
    if r * c <= budget or r % 8:
        return r
    best = 8
    for cand in range(8, r + 1, 8):
        if r % cand == 0 and cand * c <= budget:
            best = cand
    return best


def adamw_sum(parts, w, m, v, *, name):
    n, r, c = parts.shape
    tr = _row_tile(r, c)

    def body(p_ref, w_ref, m_ref, v_ref, g_ref, d_ref, m2_ref, v2_ref):
        g = p_ref[0].astype(F32)
        for j in range(1, n):
            g = g + p_ref[j].astype(F32)
        d, m2, v2 = _adamw_math(g, w_ref[...], m_ref[...], v_ref[...])
        g_ref[...] = g
        d_ref[...] = d
        m2_ref[...] = m2
        v2_ref[...] = v2

    blk = pl.BlockSpec((tr, c), lambda i: (i, 0))
    out = jax.ShapeDtypeStruct((r, c), F32)
    return pl.pallas_call(
        body, name=name, grid=(r // tr,), in_specs=[pl.BlockSpec((n, tr, c), lambda i: (0, i, 0)), blk, blk, blk],
        out_specs=[blk] * 4, out_shape=[out] * 4, compiler_params=_params(1))(parts, w, m, v)


def adamw_layer(parts, w, m, v, prev, layer, *, name):
    n, r, c = parts.shape
    nl = w.shape[0]
    tr = _row_tile(r, c)

    def body(p_ref, w_ref, m_ref, v_ref, *rest):
        g_ref, d_ref, m2_ref, v2_ref = rest[-4:]
        g = p_ref[0].astype(F32)
        for j in range(1, n):
            g = g + p_ref[j].astype(F32)
        d, m2, v2 = _adamw_math(g, w_ref[0], m_ref[0], v_ref[0])
        g_ref[0] = g
        d_ref[0] = d
        m2_ref[0] = m2
        v2_ref[0] = v2

    blk = pl.BlockSpec((1, tr, c), lambda i: (layer, i, 0))
    in_specs = [pl.BlockSpec((n, tr, c), lambda i: (0, i, 0)), blk, blk, blk]
    args = [parts, w, m, v]
    aliases = {}
    if prev is not None:
        in_specs += [ANY] * 4
        args += list(prev)
        aliases = {4 + k: k for k in range(4)}
    out = jax.ShapeDtypeStruct((nl, r, c), F32)
    return pl.pallas_call(
        body, name=name, grid=(r // tr,), in_specs=in_specs, out_specs=[blk] * 4, out_shape=[out] * 4,
        input_output_aliases=aliases, compiler_params=_params(1))(*args)


def _pack(arrays):
    flat, layout, off = [], [], 0
    for a in arrays:
        flat.append(a.reshape(-1))
        layout.append((off, a.shape))
        off += a.size
    pad = (-off) % 1024
    if pad:
        flat.append(jnp.zeros((pad,), F32))
    return jnp.concatenate(flat).reshape(-1, 128), layout


def _unpack(packed, layout, lead=()):
    flat = packed.reshape(lead + (-1,))
    return [flat[..., off:off + _size(shape)].reshape(lead + tuple(shape)) for off, shape in layout]


def _size(shape):
    n = 1
    for s in shape:
        n *= s
    return n


def _join_last(g):
    g = jnp.moveaxis(g, 0, -2)
    return g.reshape(g.shape[:-2] + (g.shape[-2] * g.shape[-1],))


def _my_cols(a, width):
    return lax.dynamic_slice_in_dim(a, _my_id() * width, width, axis=a.ndim - 1)


def _tile(t, pref):
    return min(pref, t)


def kernel(x, c, w_mod, b_mod, norm_g, sc_w_in, sc_conv, sc_w_out, pool_w, pool_b, pool_scale, cf_w_pw1, cf_b_pw1, cf_w_dw, cf_b_dw, cf_ln_g, cf_ln_b, cf_w_pw2, cf_b_pw2, ffn_w_up, ffn_conv, ffn_b_conv, ffn_w_down, loss_target, m_w_mod, m_b_mod, m_norm_g, m_sc_w_in, m_sc_conv, m_sc_w_out, m_pool_w, m_pool_b, m_pool_scale, m_cf_w_pw1, m_cf_b_pw1, m_cf_w_dw, m_cf_b_dw, m_cf_ln_g, m_cf_ln_b, m_cf_w_pw2, m_cf_b_pw2, m_ffn_w_up, m_ffn_conv, m_ffn_b_conv, m_ffn_w_down, v_w_mod, v_b_mod, v_norm_g, v_sc_w_in, v_sc_conv, v_sc_w_out, v_pool_w, v_pool_b, v_pool_scale, v_cf_w_pw1, v_cf_b_pw1, v_cf_w_dw, v_cf_b_dw, v_cf_ln_g, v_cf_ln_b, v_cf_w_pw2, v_cf_b_pw2, v_ffn_w_up, v_ffn_conv, v_ffn_b_conv, v_ffn_w_down):
    env = dict(locals())
    names = ["w_mod", "b_mod", "norm_g", "sc_w_in", "sc_conv", "sc_w_out", "pool_w", "pool_b", "pool_scale", "cf_w_pw1",
             "cf_b_pw1", "cf_w_dw", "cf_b_dw", "cf_ln_g", "cf_ln_b", "cf_w_pw2", "cf_b_pw2", "ffn_w_up", "ffn_conv",
             "ffn_b_conv", "ffn_w_down"]
    t = x.shape[1]
    tm = _tile(t, 512)
    tm_ffn = _tile(t, 256)
    tt = _tile(t, 2048)
    x0, target = x[0], loss_target[0]

    small_names = ["norm_g", "sc_conv", "cf_b_pw1", "cf_w_dw", "cf_b_dw", "cf_ln_g", "cf_ln_b", "cf_b_pw2", "ffn_conv"]
    packed, layout = _pack([c] + [env[n] for n in small_names])

    shard = {"pool": pool_w[0].astype(BF), "pw1": cf_w_pw1[0].astype(BF), "pw2": cf_w_pw2[0].astype(BF)}
    for j in range(2):
        shard[f"in{j}"], shard[f"out{j}"] = sc_w_in[j].astype(BF), sc_w_out[j].astype(BF)
    for l in range(DEPTH):
        shard[f"up{l}"], shard[f"down{l}"] = ffn_w_up[l].T.astype(BF), ffn_w_down[l].astype(BF)
    gathered, g_in0, g_out0 = _Exchange("gather", [packed, shard["in0"], shard["out0"]]).run("gather_first")
    wg = {"in0": g_in0, "out0": g_out0}
    parts = _unpack(gathered, layout, lead=(NDEV,))
    c_all = parts[0].reshape(NDEV, D)
    full = {n: _join_last(p) for n, p in zip(small_names, parts[1:])}
    fwd_plan = {("mix_in", 0): ["up0"], ("mix_out", 0): ["down0"], ("ffn_in", 0): ["pool", "up1"],
                ("ffn_out", 0): ["down1", "pw1", "pw2"], ("ffn_in", 1): ["up2"], ("ffn_out", 1): ["down2", "in1", "out1"],
                ("ffn_in", 2): ["up3"], ("ffn_out", 2): ["down3"]}

    def carrying(plan, kind, store, source, fn, key, *a, **k):
        names = plan.get(key)
        if not names:
            return fn(*a, **k)
        res = fn(*a, carry=_Exchange(kind, [source[n] for n in names]), **k)
        store.update(zip(names, res[-1]))
        return res[:-1]

    fwd = functools.partial(carrying, fwd_plan, "gather", wg, shard)

    mp = mod_partial(c_all, w_mod)
    (mod_parts,) = _Exchange("scatter", [jnp.swapaxes(mp, 0, 1)]).run("exchange_mod")
    mod = mod_finish(mod_parts, b_mod)

    def vec(a):
        return a.reshape(1, -1)

    def col_blocks(g):
        w = jnp.swapaxes(g, 0, 1).reshape(D, -1)
        return jnp.swapaxes(w.reshape(D, -1, D), 0, 1)

    def ffn_blocks(a):
        return jnp.swapaxes(a.reshape(a.shape[0], 4, FB), 0, 1)

    saved = []
    xs = x0
    for l in range(DEPTH):
        sh1, sc1, g1, sh2, sc2, g2 = [mod[l:l + 1, k * D:(k + 1) * D] for k in range(6)]
        ng = [full["norm_g"][l, k:k + 1] for k in range(4)]
        kind, j = l % 3, l // 3
        s = dict(x_in=xs, sc1=sc1, g1=g1, sc2=sc2, g2=g2, ng=ng)
        if kind == 0:
            s["w_in"] = col_blocks(wg[f"in{j}"])
            s["h"], s["p"] = fwd(fwd_in, ("mix_in", l), xs, ng[0], sc1, sh1, s["w_in"], None, blocked=False, tm=tm,
                                 name=f"sc_in_{l}")
            x1, s["m"], s["q"] = fwd(sc_fwd_out, ("mix_out", l), s["p"], full["sc_conv"][j], wg[f"out{j}"].reshape(D, D), xs,
                                     ng[1], g1, tm=tm, name=f"sc_out_{l}")
        elif kind == 1:
            pool_w_f = jnp.swapaxes(wg["pool"], 0, 1).reshape(4, PG, PG)
            x1, s["m"], s["ypre"], s["pooled"] = pool_fwd(xs, ng[0], sc1, sh1, pool_w_f, pool_b, pool_scale, ng[1], g1,
                                                          tm=tm, name=f"pool_{l}")
        else:
            s["w_in"] = col_blocks(wg["pw1"])
            s["h"], s["a"] = fwd_in(xs, ng[0], sc1, sh1, s["w_in"], full["cf_b_pw1"].reshape(2, 1, D), blocked=False, tm=tm,
                                    name=f"cf_in_{l}")
            x1, s["m"], s["s"], s["u2"] = cf_fwd_out(s["a"], full["cf_w_dw"][0], full["cf_b_dw"], full["cf_ln_g"],
                                                     full["cf_ln_b"], wg["pw2"].reshape(D, D), full["cf_b_pw2"], xs, ng[1],
                                                     g1, tm=tm, name=f"cf_out_{l}")
        s["x1"] = x1
        s["cw"] = ffn_blocks(full["ffn_conv"][l])
        s["h2"], s["up"] = fwd(fwd_in, ("ffn_in", l), x1, ng[2], sc2, sh2, wg[f"up{l}"], None, blocked=True, wt=True, tm=tm,
                               name=f"ffn_in_{l}")
        xs, s["f"], s["gc"], s["fa"] = fwd(ffn_fwd_out, ("ffn_out", l), s["up"], s["cw"], ffn_blocks(ffn_b_conv[l:l + 1]),
                                           wg[f"down{l}"].reshape(F, D), x1, ng[3], g2, tm=tm_ffn, name=f"ffn_out_{l}")
        saved.append(s)

    dx, loss_part = loss_head(xs, target, tm=tm, name="loss_head")
    loss = lax.psum(loss_part[0, 0], ("x", "y", "c"))

    gmod = [None] * DEPTH
    d_norm_g = [None] * DEPTH
    d_ffn_conv = [None] * DEPTH
    d_ffn_b_conv = [None] * DEPTH
    d_sc_conv = [None] * 2
    big = {}
    got = {}
    small_g = {}
    bwd_plan = {("mix_bout", 3): ["down3"], ("mix_bmid", 3): ["up3"], ("ffn_bout", 2): ["in1", "out1"],
                ("mix_bmid", 2): ["up2", "down2"], ("ffn_bout", 1): ["pw1", "pw2"], ("ffn_bout", 0): ["pool", "down1"],
                ("ffn_bin", 0): ["up1"], ("mix_bout", 0): ["down0"], ("mix_bmid", 0): ["up0"]}
    bwd = functools.partial(carrying, bwd_plan, "scatter", got, big)
    pool_w_f = jnp.swapaxes(wg["pool"], 0, 1).reshape(4, PG, PG)
    for l in reversed(range(DEPTH)):
        s = saved[l]
        ng = s["ng"]
        kind, j = l % 3, l // 3
        df, dgc, dup, st_o, st_b = bwd(ffn_bwd_out, ("ffn_bout", l), dx, s["f"], ng[3], s["g2"], wg[f"down{l}"].reshape(F, D),
                                       s["gc"], s["up"], tm=tm, name=f"ffn_bout_{l}")
        dx1, dup, st_i, st_c = bwd(ffn_bwd_in, ("ffn_bin", l), dgc, dup, s["up"], s["cw"], wg[f"up{l}"], s["x1"], ng[2],
                                   s["sc2"], dx, tm=tm, name=f"ffn_bin_{l}")
        big[f"up{l}"] = wgrad(dup, s["h2"], nblk=NDEV, a_blocked=True, b_blocked=False, bk=FB, bn=D, tt=tt, name=f"ffn_wup_{l}")
        big[f"down{l}"] = wgrad(s["fa"], df, nblk=4, a_blocked=True, b_blocked=False, bk=FB, bn=D, tt=tt,
                                name=f"ffn_wdown_{l}").reshape(NDEV, F // NDEV, D)
        d_ffn_b_conv[l] = st_b[:, 0, :].reshape(F)
        d_ffn_conv[l] = jnp.swapaxes(st_c[:, 1:4, :], 0, 1).reshape(3, F)
        g_ffn = [st_i[0], st_i[1], st_o[0]]
        dn3, dn2 = st_o[1], st_i[2]
        if kind == 0:
            dm, dq, st_o = bwd(bwd_out, ("mix_bout", l), dx1, s["m"], ng[1], s["g1"], wg[f"out{j}"].reshape(D, D), blocked=False,
                               tm=tm, name=f"sc_bout_{l}")
            dp, st_c = bwd(sc_bwd_mid, ("mix_bmid", l), dq, s["p"], full["sc_conv"][j], tm=tm, name=f"sc_bmid_{l}")
            dx, st_i = bwd_in([dp], s["w_in"], s["x_in"], ng[0], s["sc1"], dx1, tm=tm, name=f"sc_bin_{l}")
            big[f"in{j}"] = wgrad(s["h"], dp, nblk=NDEV, a_blocked=False, b_blocked=True, bk=D, bn=3 * D // NDEV, tt=tt,
                                  name=f"sc_win_{l}")
            big[f"out{j}"] = wgrad(s["q"], dm, nblk=1, a_blocked=False, b_blocked=False, bk=D, bn=D, tt=tt,
                                   name=f"sc_wout_{l}").reshape(NDEV, D // NDEV, D)
            d_sc_conv[j] = st_c[0:3]
        elif kind == 1:
            dh, dyp, st_o = pool_bwd(dx1, s["m"], s["ypre"], pool_w_f, pool_scale, ng[1], s["g1"], tm=tm, name=f"pool_b_{l}")
            dx, st_i = bwd_in([dh], None, s["x_in"], ng[0], s["sc1"], dx1, tm=tm, name=f"pool_bin_{l}")
            dpw = wgrad(s["pooled"], dyp, nblk=4, a_blocked=True, b_blocked=True, bk=PG, bn=PG, tt=tt, name=f"pool_w_{l}")
            big["pool"] = jnp.swapaxes(dpw.reshape(4, NDEV, PG // NDEV, PG), 0, 1).reshape(NDEV, 4 * PG // NDEV, PG)
            small_g["pool_scale"], small_g["pool_b"] = st_o[2:3], st_o[3:4]
        else:
            dm, ds, st_o = bwd_out(dx1, s["m"], ng[1], s["g1"], wg["pw2"].reshape(D, D), blocked=False, tm=tm,
                                   name=f"cf_bout_{l}")
            dA, st_c = bwd(cf_bwd_mid, ("mix_bmid", l), ds, s["u2"], s["a"], full["cf_w_dw"][0], full["cf_ln_g"],
                           full["cf_ln_b"], tm=tm, name=f"cf_bmid_{l}")
            dx, st_i = bwd_in([dA], s["w_in"], s["x_in"], ng[0], s["sc1"], dx1, tm=tm, name=f"cf_bin_{l}")
            big["pw1"] = wgrad(s["h"], dA, nblk=NDEV, a_blocked=False, b_blocked=True, bk=D, bn=2 * D // NDEV, tt=tt,
                               name=f"cf_wpw1_{l}")
            big["pw2"] = wgrad(s["s"], dm, nblk=1, a_blocked=False, b_blocked=False, bk=D, bn=D, tt=tt,
                               name=f"cf_wpw2_{l}").reshape(NDEV, D // NDEV, D)
            small_g["cf_w_dw"] = st_c[0:CFW][None]
            small_g["cf_b_dw"], small_g["cf_ln_g"], small_g["cf_ln_b"] = st_c[31:32], st_c[32:33], st_c[33:34]
            small_g["cf_b_pw1"] = st_c[34:36].reshape(1, 2 * D)
            small_g["cf_b_pw2"] = st_o[2:3]
        gmod[l] = jnp.concatenate([st_i[0], st_i[1], st_o[0]] + g_ffn)
        d_norm_g[l] = jnp.stack([st_i[2], st_o[1], dn2, dn3])

    small_g["gmod"] = jnp.stack(gmod)
    small_g["norm_g"] = jnp.stack(d_norm_g)
    small_g["sc_conv"] = jnp.stack(d_sc_conv)
    small_g["ffn_conv"] = jnp.stack(d_ffn_conv)
    small_g["ffn_b_conv"] = jnp.stack(d_ffn_b_conv)
    sg_names = ["gmod", "norm_g", "sc_conv", "pool_b", "pool_scale", "cf_b_pw1", "cf_w_dw", "cf_b_dw", "cf_ln_g", "cf_ln_b",
                "cf_b_pw2", "ffn_conv", "ffn_b_conv"]
    gpacked, glayout = _pack([small_g[n] for n in sg_names])
    (ggath,), (got["in0"], got["out0"]) = _run_exchanges(
        [_Exchange("gather", [gpacked]), _Exchange("scatter", [big["in0"], big["out0"]])], "exchange_last")
    gsum = dict(zip(sg_names, _unpack(sum_parts(ggath), glayout)))
    gmod_all = _unpack(ggath, glayout[:1], lead=(NDEV,))[0]
    grads = {"b_mod": gsum["gmod"], "pool_b": gsum["pool_b"], "pool_scale": gsum["pool_scale"],
             "ffn_b_conv": gsum["ffn_b_conv"]}
    for n in ["norm_g", "sc_conv", "cf_b_pw1", "cf_w_dw", "cf_b_dw", "cf_ln_g", "cf_ln_b", "cf_b_pw2", "ffn_conv"]:
        grads[n] = _my_cols(gsum[n], env[n].shape[-1])
    grads["w_mod"] = mod_wgrad(c_all.T, jnp.swapaxes(_my_cols(gmod_all, w_mod.shape[2]), 0, 1))

    deltas, new_m, new_v = {}, {}, {}
    sp_names = ["b_mod", "norm_g", "sc_conv", "pool_b", "pool_scale", "cf_b_pw1", "cf_w_dw", "cf_b_dw", "cf_ln_g", "cf_ln_b",
                "cf_b_pw2", "ffn_conv", "ffn_b_conv"]
    pg, playout = _pack([grads[n] for n in sp_names])
    pw_, _ = _pack([env[n] for n in sp_names])
    pm_, _ = _pack([env["m_" + n] for n in sp_names])
    pv_, _ = _pack([env["v_" + n] for n in sp_names])
    _, sd, sm, sv = adamw_sum(pg[None], pw_, pm_, pv_, name="adamw_small")
    for n, d_, m_, v_ in zip(sp_names, _unpack(sd, playout), _unpack(sm, playout), _unpack(sv, playout)):
        deltas[n], new_m[n], new_v[n] = d_, m_, v_
    gw = grads["w_mod"].reshape(1, DEPTH * D, -1)
    _, d_, m_, v_ = adamw_sum(gw, w_mod.reshape(gw.shape[1:]), m_w_mod.reshape(gw.shape[1:]), v_w_mod.reshape(gw.shape[1:]),
                              name="adamw_w_mod")
    deltas["w_mod"], new_m["w_mod"], new_v["w_mod"] = [a.reshape(w_mod.shape) for a in (d_, m_, v_)]

    groups = {"sc_w_in": ["in0", "in1"], "sc_w_out": ["out0", "out1"], "pool_w": ["pool"], "cf_w_pw1": ["pw1"],
              "cf_w_pw2": ["pw2"], "ffn_w_up": [f"up{l}" for l in range(DEPTH)], "ffn_w_down": [f"down{l}" for l in range(DEPTH)]}
    for n, layers in groups.items():
        stacked = (len(layers),) + got[layers[0]].shape[1:]
        flip = n == "ffn_w_up"
        w3 = [(jnp.swapaxes(env[p + n], 1, 2) if flip else env[p + n]).reshape(stacked) for p in ("", "m_", "v_")]
        outs = None
        for li, key in enumerate(layers):
            outs = adamw_layer(got[key], *w3, outs, li, name=f"adamw_{n}_{li}")
        grads[n], deltas[n], new_m[n], new_v[n] = [(jnp.swapaxes(a, 1, 2) if flip else a).reshape(env[n].shape) for a in outs]

    return (loss, dx[None], *[grads[n] for n in names], *[deltas[n] for n in names], *[new_m[n] for n in names],
            *[new_v[n] for n in names])
```

```python
import functools

import jax
import jax.numpy as jnp
from jax import lax
from jax.experimental import pallas as pl
from jax.experimental.pallas import tpu as pltpu

D = 1024
F = 2816
NDEV = 8
FB = F // 4
DEPTH = 4
RMS_EPS = 1e-6
LN_EPS = 1e-5
CFW = 31
POOL_WINDOWS = (2, 4, 8, 16)
PG = D // 4
LR, B1, B2, ADAM_EPS, WD, STEP = 0.001, 0.9, 0.999, 1e-08, 0.01, 10

BF = jnp.bfloat16
F32 = jnp.float32
VMEM_LIMIT_V7X = 56 * 1024 * 1024
MESH = pl.DeviceIdType.MESH
ANY = pl.BlockSpec(memory_space=pl.ANY)


def _params(n_axes):
    return pltpu.CompilerParams(dimension_semantics=("arbitrary",) * n_axes, vmem_limit_bytes=VMEM_LIMIT_V7X)


def _const(shape, single=True):
    nd = len(shape)
    if single:
        return pl.BlockSpec(shape, lambda *_: (0,) * nd, pipeline_mode=pl.Buffered(1))
    return pl.BlockSpec(shape, lambda *_: (0,) * nd)


def _rows(tm, c):
    return pl.BlockSpec((tm, c), lambda i: (i, 0))


def _brows(nb, tm, c, b0=0):
    return pl.BlockSpec((nb, tm, c), lambda i: (b0, i, 0))


def _prev(hb, c, tm):
    return pl.BlockSpec((hb, c), lambda i: (jnp.maximum(i * (tm // hb) - 1, 0), 0))


def _next(hb, c, tm, t):
    return pl.BlockSpec((hb, c), lambda i: (jnp.minimum((i + 1) * (tm // hb), t // hb - 1), 0))


def _bprev(nb, hb, c, tm, b0=0):
    return pl.BlockSpec((nb, hb, c), lambda i: (b0, jnp.maximum(i * (tm // hb) - 1, 0), 0))


def _bnext(nb, hb, c, tm, t, b0=0):
    return pl.BlockSpec((nb, hb, c), lambda i: (b0, jnp.minimum((i + 1) * (tm // hb), t // hb - 1), 0))


def _sigmoid(v):
    return 0.5 * jnp.tanh(0.5 * v) + 0.5


def _fold8(v):
    r, c = v.shape
    return jnp.sum(v.reshape(r // 8, 8, c), axis=0)


def _chunks(n_rows, rc, step, init=0, reverse=False):
    n = n_rows // rc

    def it(c, carry):
        idx = (n - 1 - c) if reverse else c
        return step(pl.multiple_of(idx * rc, rc), carry)

    return lax.fori_loop(0, n, it, init)


def _row_shifted_copies(s_ref, n):
    for b in range(1, 8):
        s_ref[b, 0:n, :] = s_ref[0, pl.ds(b, n), :]


def _shifted(s_ref, o, tm):
    return s_ref[o % 8, pl.ds(8 * (o // 8), tm), :]


def _dot(a, b):
    return jnp.dot(a, b, preferred_element_type=F32)


def _dot_nt(a, b):
    return lax.dot_general(a, b, (((1,), (1,)), ((), ())), preferred_element_type=F32)


def _dot_tn(a, b):
    return lax.dot_general(a, b, (((0,), (0,)), ((), ())), preferred_element_type=F32)


def _rsum(v):
    return jnp.sum(v, axis=0, keepdims=True)


def _adaln(x, g, sc, sh):
    r = lax.rsqrt(jnp.mean(x * x, axis=-1, keepdims=True) + RMS_EPS)
    return (x * r * g) * (1.0 + sc) + sh


def _gated_res(x, m, gn, gt):
    r = lax.rsqrt(jnp.mean(m * m, axis=-1, keepdims=True) + RMS_EPS)
    return x + gt * (m * r * gn)


def _gated_res_bwd(dxo, m, gn, gt):
    r = lax.rsqrt(jnp.mean(m * m, axis=-1, keepdims=True) + RMS_EPS)
    mh = m * r
    dgt = _rsum(dxo * (mh * gn))
    dn = dxo * gt
    dgn = _rsum(dn * mh)
    dmh = dn * gn
    dm = r * (dmh - mh * jnp.mean(dmh * mh, axis=-1, keepdims=True))
    return dm, dgt, dgn


def _my_id():
    return 4 * lax.axis_index("x") + 2 * lax.axis_index("y") + lax.axis_index("c")


def _peer(k):
    x, y, c = lax.axis_index("x"), lax.axis_index("y"), lax.axis_index("c")
    px = 1 - x if k & 4 else x
    py = 1 - y if k & 2 else y
    pc = 1 - c if k & 1 else c
    return (px, py, pc), 4 * px + 2 * py + pc


class _Exchange:
    def __init__(self, kind, arrays):
        self.gather = kind == "gather"
        self.arrays = list(arrays)
        n = len(self.arrays)
        if self.gather:
            self.out_shape = [jax.ShapeDtypeStruct((NDEV,) + a.shape, a.dtype) for a in self.arrays]
        else:
            self.out_shape = [jax.ShapeDtypeStruct(a.shape, a.dtype) for a in self.arrays]
        self.scratch = [pltpu.SemaphoreType.DMA((n * NDEV,)), pltpu.SemaphoreType.DMA((n * NDEV,)),
                        pltpu.SemaphoreType.DMA((n,))]

    def _local(self, a, src, dst, sems):
        me = _my_id()
        return pltpu.make_async_copy(src[a] if self.gather else src[a].at[me], dst[a].at[me], sems[2].at[a])

    def _remote(self, a, k, src, dst, sems, incoming):
        to, pid = _peer(k)
        me = _my_id()
        return pltpu.make_async_remote_copy(
            src_ref=src[a] if self.gather else src[a].at[pid], dst_ref=dst[a].at[pid if incoming else me],
            send_sem=sems[0].at[a * NDEV + k], recv_sem=sems[1].at[a * NDEV + k], device_id=to, device_id_type=MESH)

    def start(self, src, dst, sems):
        for a in range(len(self.arrays)):
            self._local(a, src, dst, sems).start()
        for k in range(1, NDEV):
            for a in range(len(self.arrays)):
                self._remote(a, k, src, dst, sems, False).start()

    def wait(self, src, dst, sems):
        for k in range(1, NDEV):
            for a in range(len(self.arrays)):
                self._remote(a, k, src, dst, sems, True).wait_recv()
        for k in range(1, NDEV):
            for a in range(len(self.arrays)):
                self._remote(a, k, src, dst, sems, False).wait_send()
        for a in range(len(self.arrays)):
            self._local(a, src, dst, sems).wait()

    def run(self, name):
        n = len(self.arrays)

        def body(*refs):
            src, dst, sems = refs[:n], refs[n:2 * n], refs[2 * n:]
            self.start(src, dst, sems)
            self.wait(src, dst, sems)

        return pl.pallas_call(body, name=name, in_specs=[ANY] * n, out_specs=[ANY] * n, out_shape=self.out_shape,
                              scratch_shapes=self.scratch)(*self.arrays)


def _run_exchanges(exchanges, name):
    counts = [len(e.arrays) for e in exchanges]
    n = sum(counts)

    def body(*refs):
        src, dst, sems = refs[:n], refs[n:2 * n], refs[2 * n:]
        parts, lo = [], 0
        for ei, (e, c) in enumerate(zip(exchanges, counts)):
            parts.append((e, src[lo:lo + c], dst[lo:lo + c], sems[3 * ei:3 * ei + 3]))
            lo += c
        for e, s, d, m in parts:
            e.start(s, d, m)
        for e, s, d, m in parts:
            e.wait(s, d, m)

    res = pl.pallas_call(
        body, name=name, in_specs=[ANY] * n, out_specs=[ANY] * n, out_shape=[s for e in exchanges for s in e.out_shape],
        scratch_shapes=[s for e in exchanges for s in e.scratch])(*[a for e in exchanges for a in e.arrays])
    out, lo = [], 0
    for c in counts:
        out.append(list(res[lo:lo + c]))
        lo += c
    return out


def _call(body, *, name, grid, in_specs, out_specs, out_shape, args, scratch_shapes=(), carry=None, aliases=None):
    cp = _params(len(grid))
    aliases = aliases or {}
    if carry is None:
        return tuple(pl.pallas_call(body, name=name, grid=grid, in_specs=in_specs, out_specs=out_specs, out_shape=out_shape,
                                    scratch_shapes=list(scratch_shapes), input_output_aliases=aliases,
                                    compiler_params=cp)(*args))
    n_in, n_out, n_sc, n_c = len(in_specs), len(out_specs), len(scratch_shapes), len(carry.arrays)

    def wrapped(*refs):
        ins, src = refs[:n_in], refs[n_in:n_in + n_c]
        outs = refs[n_in + n_c:n_in + n_c + n_out]
        dst = refs[n_in + n_c + n_out:n_in + 2 * n_c + n_out]
        rest = refs[n_in + 2 * n_c + n_out:]
        scr, sems = rest[:n_sc], rest[n_sc:]
        first = pl.program_id(0) == 0
        last = pl.program_id(0) == grid[0] - 1
        for ax in range(1, len(grid)):
            first = jnp.logical_and(first, pl.program_id(ax) == 0)
            last = jnp.logical_and(last, pl.program_id(ax) == grid[ax] - 1)

        @pl.when(first)
        def _():
            carry.start(src, dst, sems)

        body(*ins, *outs, *scr)

        @pl.when(last)
        def _():
            carry.wait(src, dst, sems)

    res = pl.pallas_call(
        wrapped, name=name, grid=grid, in_specs=list(in_specs) + [ANY] * n_c, out_specs=list(out_specs) + [ANY] * n_c,
        out_shape=list(out_shape) + carry.out_shape, scratch_shapes=list(scratch_shapes) + carry.scratch,
        input_output_aliases=aliases, compiler_params=cp)(*args, *carry.arrays)
    return tuple(res[:n_out]) + (list(res[n_out:]),)


def fwd_in(x, g, sc, sh, w, bias, *, blocked, tm, name, carry=None, wt=False):
    t = x.shape[0]
    nb, bw = (w.shape[0], w.shape[1]) if wt else (w.shape[0], w.shape[2])

    def body(*refs):
        if bias is None:
            x_ref, g_ref, sc_ref, sh_ref, w_ref, h_ref, p_ref = refs
        else:
            x_ref, g_ref, sc_ref, sh_ref, w_ref, b_ref, h_ref, p_ref = refs
        hb = _adaln(x_ref[...], g_ref[...], sc_ref[...], sh_ref[...]).astype(BF)
        h_ref[...] = hb
        for d in range(nb):
            y = _dot_nt(hb, w_ref[d]) if wt else _dot(hb, w_ref[d])
            if bias is not None:
                y = y + b_ref[d]
            if blocked:
                p_ref[d] = y.astype(BF)
            else:
                p_ref[:, d * bw:(d + 1) * bw] = y.astype(BF)

    vec = _const((1, D))
    in_specs = [_rows(tm, D), vec, vec, vec, _const(w.shape)]
    args = [x, g, sc, sh, w]
    if bias is not None:
        in_specs.append(_const((nb, 1, bw)))
        args.append(bias)
    if blocked:
        p_spec, p_shape = _brows(nb, tm, bw), jax.ShapeDtypeStruct((nb, t, bw), BF)
    else:
        p_spec, p_shape = _rows(tm, nb * bw), jax.ShapeDtypeStruct((t, nb * bw), BF)
    return _call(body, name=name, grid=(t // tm,), in_specs=in_specs, out_specs=[_rows(tm, D), p_spec],
                 out_shape=[jax.ShapeDtypeStruct((t, D), BF), p_shape], args=args, carry=carry)


def _conv3_from(s_ref, w, tm, lo):
    acc = w[0:1, :] * s_ref[pl.ds(lo, tm), :]
    for k in (1, 2):
        acc = acc + w[k:k + 1, :] * s_ref[pl.ds(lo + k, tm), :]
    return acc


def _sc_conv(p_ref, ph_ref, cw_ref, first, tm):
    z = p_ref[:, D:2 * D].astype(F32) * p_ref[:, 2 * D:3 * D].astype(F32)
    zp = jnp.where(first, 0.0, ph_ref[8:16, D:2 * D].astype(F32) * ph_ref[8:16, 2 * D:3 * D].astype(F32))
    ext = jnp.concatenate([zp, z], axis=0)
    return cw_ref[0:1, :] * ext[6:6 + tm] + cw_ref[1:2, :] * ext[7:7 + tm] + cw_ref[2:3, :] * z


def sc_fwd_out(p, convw, w_out, x, gn, gt, *, tm, name, carry=None):
    t = x.shape[0]

    def body(p_ref, ph_ref, cw_ref, w_ref, x_ref, gn_ref, gt_ref, x1_ref, m_ref, q_ref):
        u = _sc_conv(p_ref, ph_ref, cw_ref, pl.program_id(0) == 0, tm)
        qb = (p_ref[:, 0:D].astype(F32) * u).astype(BF)
        q_ref[...] = qb
        m = _dot(qb, w_ref[...])
        m_ref[...] = m.astype(BF)
        x1_ref[...] = _gated_res(x_ref[...], m, gn_ref[...], gt_ref[...])

    vec = _const((1, D))
    return _call(
        body, name=name, grid=(t // tm,),
        in_specs=[_rows(tm, 3 * D), _prev(16, 3 * D, tm), _const((3, D)), _const((D, D)), _rows(tm, D), vec, vec],
        out_specs=[_rows(tm, D)] * 3,
        out_shape=[jax.ShapeDtypeStruct((t, D), F32), jax.ShapeDtypeStruct((t, D), BF), jax.ShapeDtypeStruct((t, D), BF)],
        args=[p, p, convw, w_out, x, gn, gt], carry=carry)


def _layernorm_parts(u2):
    mu = jnp.mean(u2, axis=-1, keepdims=True)
    cen = u2 - mu
    rstd = lax.rsqrt(jnp.mean(cen * cen, axis=-1, keepdims=True) + LN_EPS)
    return cen * rstd, rstd


def cf_fwd_out(a, w_dw, b_dw, ln_g, ln_b, w_pw2, b_pw2, x, gn, gt, *, tm, name):
    t = x.shape[0]
    hb = 32

    def body(a_ref, ah_ref, wd_ref, bd_ref, lg_ref, lb_ref, w_ref, b2_ref, x_ref, gn_ref, gt_ref,
             x1_ref, m_ref, s_out_ref, u2_ref, s_ref):
        i = pl.program_id(0)
        uh = ah_ref[:, 0:D].astype(F32) * _sigmoid(ah_ref[:, D:2 * D].astype(F32))
        s_ref[0, 0:hb, :] = jnp.where(i == 0, 0.0, uh)
        s_ref[0, hb:hb + tm, :] = a_ref[:, 0:D].astype(F32) * _sigmoid(a_ref[:, D:2 * D].astype(F32))
        _row_shifted_copies(s_ref, tm + hb - 8)
        acc = bd_ref[...] + wd_ref[0:1, :] * _shifted(s_ref, hb - CFW + 1, tm)
        for k in range(1, CFW):
            acc = acc + wd_ref[k:k + 1, :] * _shifted(s_ref, hb - CFW + 1 + k, tm)
        u2_ref[...] = acc.astype(BF)
        xh, _ = _layernorm_parts(acc)
        l = xh * lg_ref[...] + lb_ref[...]
        sb = (l * _sigmoid(l)).astype(BF)
        s_out_ref[...] = sb
        m = _dot(sb, w_ref[...]) + b2_ref[...]
        m_ref[...] = m.astype(BF)
        x1_ref[...] = _gated_res(x_ref[...], m, gn_ref[...], gt_ref[...])

    vec = _const((1, D))
    return pl.pallas_call(
        body, name=name, grid=(t // tm,),
        in_specs=[_rows(tm, 2 * D), _prev(hb, 2 * D, tm), _const((CFW, D)), vec, vec, vec, _const((D, D)), vec,
                  _rows(tm, D), vec, vec],
        out_specs=[_rows(tm, D)] * 4,
        out_shape=[jax.ShapeDtypeStruct((t, D), F32)] + [jax.ShapeDtypeStruct((t, D), BF)] * 3,
        scratch_shapes=[pltpu.VMEM((8, tm + hb, D), F32)], compiler_params=_params(1),
    )(a, a, w_dw, b_dw, ln_g, ln_b, w_pw2, b_pw2, x, gn, gt)


def _pool_counts(i, tm, w):
    row = lax.broadcasted_iota(jnp.int32, (tm, 1), 0) + i * tm
    return jnp.minimum(row + 1, w).astype(F32)


def pool_fwd(x, g, sc, sh, pw, pb, pscale, gn, gt, *, tm, name):
    t = x.shape[0]
    pad, hb = 8, 16
    base = pad + hb

    def body(x_ref, xh_ref, g_ref, sc_ref, sh_ref, pw_ref, pb_ref, ps_ref, gn_ref, gt_ref,
             x1_ref, m_ref, yp_ref, po_ref, sa_ref, sb_ref):
        i = pl.program_id(0)
        hh = _adaln(xh_ref[...], g_ref[...], sc_ref[...], sh_ref[...])
        h = _adaln(x_ref[...], g_ref[...], sc_ref[...], sh_ref[...])
        zero = jnp.zeros((pad, D), F32)
        sa_ref[0:pad, :] = zero
        sb_ref[0:pad, :] = zero
        sa_ref[pad:base, :] = jnp.where(i == 0, 0.0, hh)
        sa_ref[base:base + tm, :] = h
        n = hb + tm
        src, dst = sa_ref, sb_ref
        ys = []
        for gi, w in enumerate(POOL_WINDOWS):
            c0 = gi * PG
            step = w // 2
            dst[pl.ds(pad, n), c0:D] = src[pl.ds(pad, n), c0:D] + src[pl.ds(pad - step, n), c0:D]
            mean = dst[pl.ds(base, tm), c0:c0 + PG] / _pool_counts(i, tm, w)
            pooled = (mean - h[:, c0:c0 + PG]).astype(BF)
            po_ref[:, c0:c0 + PG] = pooled
            ys.append(_dot(pooled, pw_ref[gi]))
            src, dst = dst, src
        ypre = jnp.concatenate(ys, axis=1) + pb_ref[...]
        yp_ref[...] = ypre.astype(BF)
        m = ypre * ps_ref[...]
        m_ref[...] = m.astype(BF)
        x1_ref[...] = _gated_res(x_ref[...], m, gn_ref[...], gt_ref[...])

    vec = _const((1, D))
    return pl.pallas_call(
        body, name=name, grid=(t // tm,),
        in_specs=[_rows(tm, D), _prev(hb, D, tm), vec, vec, vec, _const((4, PG, PG)), vec, vec, vec, vec],
        out_specs=[_rows(tm, D)] * 4,
        out_shape=[jax.ShapeDtypeStruct((t, D), F32)] + [jax.ShapeDtypeStruct((t, D), BF)] * 3,
        scratch_shapes=[pltpu.VMEM((tm + base, D), F32)] * 2, compiler_params=_params(1),
    )(x, x, g, sc, sh, pw, pb, pscale, gn, gt)


def ffn_fwd_out(up, convw, convb, w_down, x, gn, gt, *, tm, name, carry=None):
    t = x.shape[0]

    def body(gate_ref, gh_ref, val_ref, cw_ref, cb_ref, w_ref, x_ref, gn_ref, gt_ref, x2_ref, f_ref, gc_ref, a_ref):
        i = pl.program_id(0)
        acc = jnp.zeros((tm, D), F32)
        for j in range(4):
            gate = gate_ref[j].astype(F32)
            ext = jnp.concatenate([jnp.where(i == 0, 0.0, gh_ref[j, 8:16, :].astype(F32)), gate], axis=0)
            gc = cb_ref[j] + cw_ref[j, 0:1, :] * ext[6:6 + tm] + cw_ref[j, 1:2, :] * ext[7:7 + tm] + cw_ref[j, 2:3, :] * gate
            gc_ref[j] = gc.astype(BF)
            ab = (gc * _sigmoid(gc) * val_ref[j].astype(F32)).astype(BF)
            a_ref[j] = ab
            acc = acc + _dot(ab, w_ref[j * FB:(j + 1) * FB, :])
        f_ref[...] = acc.astype(BF)
        x2_ref[...] = _gated_res(x_ref[...], acc, gn_ref[...], gt_ref[...])

    vec = _const((1, D))
    blk = jax.ShapeDtypeStruct((4, t, FB), BF)
    return _call(
        body, name=name, grid=(t // tm,),
        in_specs=[_brows(4, tm, FB, 0), _bprev(4, 16, FB, tm, 0), _brows(4, tm, FB, 1), _const((4, 3, FB)),
                  _const((4, 1, FB)), _const((F, D)), _rows(tm, D), vec, vec],
        out_specs=[_rows(tm, D), _rows(tm, D), _brows(4, tm, FB), _brows(4, tm, FB)],
        out_shape=[jax.ShapeDtypeStruct((t, D), F32), jax.ShapeDtypeStruct((t, D), BF), blk, blk],
        args=[up, up, up, convw, convb, w_down, x, gn, gt], carry=carry)


def loss_head(y, target, *, tm, name):
    t = y.shape[0]

    def body(y_ref, t_ref, dy_ref, l_ref, acc_ref):
        i = pl.program_id(0)

        @pl.when(i == 0)
        def _():
            acc_ref[...] = jnp.zeros_like(acc_ref)

        e = y_ref[...] - t_ref[...]
        dy_ref[...] = e * (1.0 / D)
        acc_ref[...] += _rsum(e * e)

        @pl.when(i == pl.num_programs(0) - 1)
        def _():
            l_ref[...] = jnp.sum(acc_ref[...], axis=1, keepdims=True) * (0.5 / D)

    return pl.pallas_call(
        body, name=name, grid=(t // tm,), in_specs=[_rows(tm, D), _rows(tm, D)],
        out_specs=[_rows(tm, D), pl.BlockSpec((1, 1), lambda i: (0, 0))],
        out_shape=[jax.ShapeDtypeStruct((t, D), F32), jax.ShapeDtypeStruct((1, 1), F32)],
        scratch_shapes=[pltpu.VMEM((1, D), F32)], compiler_params=_params(1))(y, target)


def _init_stats(ref):
    @pl.when(pl.program_id(0) == 0)
    def _():
        ref[...] = jnp.zeros_like(ref)


def bwd_out(dxo, m, gn, gt, w, *, blocked, tm, name, carry=None):
    t = dxo.shape[0]
    k = w.shape[0]

    def body(dx_ref, m_ref, gn_ref, gt_ref, w_ref, dm_ref, da_ref, st_ref):
        _init_stats(st_ref)
        dm, dgt, dgn = _gated_res_bwd(dx_ref[...], m_ref[...].astype(F32), gn_ref[...], gt_ref[...])
        st_ref[0:1, :] += dgt
        st_ref[1:2, :] += dgn
        st_ref[2:3, :] += _rsum(dm)
        dmb = dm.astype(BF)
        dm_ref[...] = dmb
        if blocked:
            for j in range(4):
                da_ref[j] = _dot_nt(dmb, w_ref[j * FB:(j + 1) * FB, :]).astype(BF)
        else:
            da_ref[...] = _dot_nt(dmb, w_ref[...]).astype(BF)

    vec = _const((1, D))
    if blocked:
        da_spec, da_shape = _brows(4, tm, FB), jax.ShapeDtypeStruct((4, t, FB), BF)
    else:
        da_spec, da_shape = _rows(tm, k), jax.ShapeDtypeStruct((t, k), BF)
    return _call(
        body, name=name, grid=(t // tm,), in_specs=[_rows(tm, D), _rows(tm, D), vec, vec, _const((k, D))],
        out_specs=[_rows(tm, D), da_spec, _const((8, D), single=False)],
        out_shape=[jax.ShapeDtypeStruct((t, D), BF), da_shape, jax.ShapeDtypeStruct((8, D), F32)],
        args=[dxo, m, gn, gt, w], carry=carry)


def bwd_in(dps, w, x, g, sc, dxo, *, tm, name, wt=False):
    t = x.shape[0]
    direct = w is None
    if not direct:
        nb, bw = (w.shape[0], w.shape[1]) if wt else (w.shape[0], w.shape[2])
    natural = (not direct) and dps[0].ndim == 2

    def body(*refs):
        n = len(dps)
        dp_refs = refs[:n]
        if direct:
            x_ref, g_ref, sc_ref, dxo_ref, dx_ref, st_ref = refs[n:]
            dh = dp_refs[0][...]
        else:
            w_ref, x_ref, g_ref, sc_ref, dxo_ref, dx_ref, st_ref = refs[n:]
            dh = jnp.zeros((tm, D), F32)
            if natural:
                for d in range(nb):
                    dh = dh + _dot_nt(dp_refs[0][:, d * bw:(d + 1) * bw], w_ref[d])
            else:
                d = 0
                for r in dp_refs:
                    for j in range(r.shape[0]):
                        dh = dh + (_dot(r[j], w_ref[d]) if wt else _dot_nt(r[j], w_ref[d]))
                        d += 1
        _init_stats(st_ref)
        x = x_ref[...]
        r = lax.rsqrt(jnp.mean(x * x, axis=-1, keepdims=True) + RMS_EPS)
        xh = x * r
        gv = g_ref[...]
        st_ref[0:1, :] += _rsum(dh)
        st_ref[1:2, :] += _rsum(dh * (xh * gv))
        dn = dh * (1.0 + sc_ref[...])
        st_ref[2:3, :] += _rsum(dn * xh)
        dy = dn * gv
        dx_ref[...] = dxo_ref[...] + r * (dy - xh * jnp.mean(dy * xh, axis=-1, keepdims=True))

    vec = _const((1, D))
    if direct:
        dp_specs = [_rows(tm, D)]
    elif natural:
        dp_specs = [_rows(tm, nb * bw)]
    else:
        dp_specs = [_brows(a.shape[0], tm, bw) for a in dps]
    w_specs, w_args = ([], []) if direct else ([_const(w.shape)], [w])
    return pl.pallas_call(
        body, name=name, grid=(t // tm,),
        in_specs=dp_specs + w_specs + [_rows(tm, D), vec, vec, _rows(tm, D)],
        out_specs=[_rows(tm, D), _const((8, D), single=False)],
        out_shape=[jax.ShapeDtypeStruct((t, D), F32), jax.ShapeDtypeStruct((8, D), F32)],
        compiler_params=_params(1))(*dps, *w_args, x, g, sc, dxo)


def sc_bwd_mid(dq, p, convw, *, tm, name, carry=None):
    t = dq.shape[0]

    rc, cc = 16, 256

    def body(dq_ref, dqn_ref, p_ref, pp_ref, pn_ref, cw_ref, dp_ref, st_ref, acc_ref):
        i = pl.program_id(0)
        last = i == pl.num_programs(0) - 1
        _init_stats(acc_ref)
        for c0 in range(0, D, cc):
            b_c, c_c, h_c = slice(c0, c0 + cc), slice(D + c0, D + c0 + cc), slice(2 * D + c0, 2 * D + c0 + cc)
            w0, w1, w2 = cw_ref[0:1, b_c], cw_ref[1:2, b_c], cw_ref[2:3, b_c]
            below = jnp.where(last, 0.0, dqn_ref[0:8, b_c].astype(F32) * pn_ref[0:8, b_c].astype(F32))
            above = jnp.where(i == 0, 0.0, pp_ref[8:16, c_c].astype(F32) * pp_ref[8:16, h_c].astype(F32))

            def step(r0, carry):
                rows = pl.ds(r0, rc)
                cg, hi = p_ref[rows, c_c].astype(F32), p_ref[rows, h_c].astype(F32)
                z = cg * hi
                up16 = pl.ds(pl.multiple_of(jnp.maximum(r0 - 16, 0), 16), 16)
                zprev = jnp.where(r0 == 0, above, (p_ref[up16, c_c].astype(F32) * p_ref[up16, h_c].astype(F32))[8:16])
                zext = jnp.concatenate([zprev, z], axis=0)
                u = w0 * zext[6:6 + rc] + w1 * zext[7:7 + rc] + w2 * z
                dqf = dq_ref[rows, b_c].astype(F32)
                dp_ref[rows, b_c] = (dqf * u).astype(BF)
                du = dqf * p_ref[rows, b_c].astype(F32)
                ext = jnp.concatenate([du, carry], axis=0)
                e1, e2 = ext[1:rc + 1], ext[2:rc + 2]
                dz = w2 * du + w1 * e1 + w0 * e2
                dp_ref[rows, c_c] = (dz * hi).astype(BF)
                dp_ref[rows, h_c] = (dz * cg).astype(BF)
                acc_ref[0:8, b_c] += _fold8(z * e2)
                acc_ref[8:16, b_c] += _fold8(z * e1)
                acc_ref[16:24, b_c] += _fold8(z * du)
                return du[0:8]

            _chunks(tm, rc, step, below, reverse=True)

        @pl.when(last)
        def _():
            for k in range(3):
                st_ref[k:k + 1, :] = jnp.sum(acc_ref[8 * k:8 * k + 8, :], axis=0, keepdims=True)
            st_ref[3:8, :] = jnp.zeros((5, D), F32)

    return _call(
        body, name=name, grid=(t // tm,),
        in_specs=[_rows(tm, D), _next(16, D, tm, t), _rows(tm, 3 * D), _prev(16, 3 * D, tm), _next(16, 3 * D, tm, t),
                  _const((3, D))],
        out_specs=[_rows(tm, 3 * D), _const((8, D), single=False)],
        out_shape=[jax.ShapeDtypeStruct((t, 3 * D), BF), jax.ShapeDtypeStruct((8, D), F32)],
        scratch_shapes=[pltpu.VMEM((24, D), F32)], args=[dq, dq, p, p, p, convw], carry=carry)


def sc_bwd_out(dxo, m, gn, gt, w_out, p, convw, *, tm, name, carry=None):
    t = dxo.shape[0]

    def body(dx_ref, m_ref, gn_ref, gt_ref, w_ref, p_ref, ph_ref, cw_ref, dm_ref, dbg_ref, du_ref, st_ref):
        _init_stats(st_ref)
        dm, dgt, dgn = _gated_res_bwd(dx_ref[...], m_ref[...].astype(F32), gn_ref[...], gt_ref[...])
        st_ref[0:1, :] += dgt
        st_ref[1:2, :] += dgn
        dmb = dm.astype(BF)
        dm_ref[...] = dmb
        dq = _dot_nt(dmb, w_ref[...])
        dbg_ref[...] = (dq * _sc_conv(p_ref, ph_ref, cw_ref, pl.program_id(0) == 0, tm)).astype(BF)
        du_ref[...] = (dq * p_ref[:, 0:D].astype(F32)).astype(BF)

    vec = _const((1, D))
    out = jax.ShapeDtypeStruct((t, D), BF)
    return _call(
        body, name=name, grid=(t // tm,),
        in_specs=[_rows(tm, D), _rows(tm, D), vec, vec, _const((D, D)), _rows(tm, 3 * D), _prev(16, 3 * D, tm), _const((3, D))],
        out_specs=[_rows(tm, D)] * 3 + [_const((8, D), single=False)], out_shape=[out, out, out, jax.ShapeDtypeStruct((8, D), F32)],
        args=[dxo, m, gn, gt, w_out, p, p, convw], carry=carry)


def sc_bwd_in(du, dbg, p, convw, w, x, g, sc, dxo, *, tm, name, carry=None):
    t = x.shape[0]

    def body(du_ref, dun_ref, dbg_ref, p_ref, cw_ref, w_ref, x_ref, g_ref, sc_ref, dxo_ref, dx_ref, dp_ref, st_ref, sc2_ref):
        last = pl.program_id(0) == pl.num_programs(0) - 1
        _init_stats(st_ref)
        _init_stats(sc2_ref)
        du = du_ref[...].astype(F32)
        ext = jnp.concatenate([du, jnp.where(last, 0.0, dun_ref[0:8, :].astype(F32))], axis=0)
        e1, e2 = ext[1:tm + 1], ext[2:tm + 2]
        dz = cw_ref[2:3, :] * du + cw_ref[1:2, :] * e1 + cw_ref[0:1, :] * e2
        cg, hi = p_ref[:, D:2 * D].astype(F32), p_ref[:, 2 * D:3 * D].astype(F32)
        dbg, dcg, dhi = dbg_ref[...], (dz * hi).astype(BF), (dz * cg).astype(BF)
        dp_ref[:, 0:D] = dbg
        dp_ref[:, D:2 * D] = dcg
        dp_ref[:, 2 * D:3 * D] = dhi
        dh = _dot_nt(dbg, w_ref[0]) + _dot_nt(dcg, w_ref[1]) + _dot_nt(dhi, w_ref[2])
        z = cg * hi
        sc2_ref[0:1, :] += _rsum(z * e2)
        sc2_ref[1:2, :] += _rsum(z * e1)
        sc2_ref[2:3, :] += _rsum(z * du)
        x = x_ref[...]
        r = lax.rsqrt(jnp.mean(x * x, axis=-1, keepdims=True) + RMS_EPS)
        xh = x * r
        gv = g_ref[...]
        st_ref[0:1, :] += _rsum(dh)
        st_ref[1:2, :] += _rsum(dh * (xh * gv))
        dn = dh * (1.0 + sc_ref[...])
        st_ref[2:3, :] += _rsum(dn * xh)
        dy = dn * gv
        dx_ref[...] = dxo_ref[...] + r * (dy - xh * jnp.mean(dy * xh, axis=-1, keepdims=True))

    vec = _const((1, D))
    stat = jax.ShapeDtypeStruct((8, D), F32)
    return _call(
        body, name=name, grid=(t // tm,),
        in_specs=[_rows(tm, D), _next(16, D, tm, t), _rows(tm, D), _rows(tm, 3 * D), _const((3, D)), _const(w.shape),
                  _rows(tm, D), vec, vec, _rows(tm, D)],
        out_specs=[_rows(tm, D), _rows(tm, 3 * D), _const((8, D), single=False), _const((8, D), single=False)],
        out_shape=[jax.ShapeDtypeStruct((t, D), F32), jax.ShapeDtypeStruct((t, 3 * D), BF), stat, stat],
        args=[du, du, dbg, p, convw, w, x, g, sc, dxo], carry=carry)


def cf_bwd_mid(ds, u2, a, w_dw, ln_g, ln_b, *, tm, name, carry=None):
    t = ds.shape[0]
    hb = 32

    def du2_of(dsv, u2v, lg, lb):
        xh, rstd = _layernorm_parts(u2v)
        l = xh * lg + lb
        sg = _sigmoid(l)
        dl = dsv * (sg * (1.0 + l * (1.0 - sg)))
        dxh = dl * lg
        du2 = rstd * (dxh - jnp.mean(dxh, axis=-1, keepdims=True) - xh * jnp.mean(dxh * xh, axis=-1, keepdims=True))
        return du2, dl, xh

    rc, cc = 32, 256

    def body(ds_ref, dsn_ref, u2_ref, u2n_ref, a_ref, wd_ref, lg_ref, lb_ref, da_ref, st_ref, s1_ref, acc_ref):
        i = pl.program_id(0)
        last = i == pl.num_programs(0) - 1
        _init_stats(st_ref)
        _init_stats(acc_ref)
        lg, lb = lg_ref[...], lb_ref[...]
        du2, dl, xh = du2_of(ds_ref[...].astype(F32), u2_ref[...].astype(F32), lg, lb)
        st_ref[32:33, :] += _rsum(dl * xh)
        st_ref[33:34, :] += _rsum(dl)
        st_ref[31:32, :] += _rsum(du2)
        du2n, _, _ = du2_of(dsn_ref[...].astype(F32), u2n_ref[...].astype(F32), lg, lb)
        s1_ref[0, 0:tm, :] = du2
        s1_ref[0, tm:tm + hb, :] = jnp.where(last, 0.0, du2n)
        _row_shifted_copies(s1_ref, tm + hb - 8)

        def taps(r0, _):
            rows = pl.ds(r0, rc)
            for c0 in range(0, D, cc):
                sg = _sigmoid(a_ref[rows, D + c0:D + c0 + cc].astype(F32))
                u = a_ref[rows, c0:c0 + cc].astype(F32) * sg
                du = jnp.zeros((rc, cc), F32)
                for k in range(CFW):
                    o = CFW - 1 - k
                    sh = s1_ref[o % 8, pl.ds(pl.multiple_of(r0 + 8 * (o // 8), 8), rc), c0:c0 + cc]
                    du = du + wd_ref[k:k + 1, c0:c0 + cc] * sh
                    acc_ref[8 * k:8 * k + 8, c0:c0 + cc] += _fold8(u * sh)
                dav = du * sg
                dgv = du * u * (1.0 - sg)
                acc_ref[8 * CFW:8 * CFW + 8, c0:c0 + cc] += _fold8(dav)
                acc_ref[8 * CFW + 8:8 * CFW + 16, c0:c0 + cc] += _fold8(dgv)
                da_ref[rows, c0:c0 + cc] = dav.astype(BF)
                da_ref[rows, D + c0:D + c0 + cc] = dgv.astype(BF)
            return 0

        _chunks(tm, rc, taps)

        @pl.when(last)
        def _():
            for k in range(CFW):
                st_ref[k:k + 1, :] = jnp.sum(acc_ref[8 * k:8 * k + 8, :], axis=0, keepdims=True)
            st_ref[34:35, :] = jnp.sum(acc_ref[8 * CFW:8 * CFW + 8, :], axis=0, keepdims=True)
            st_ref[35:36, :] = jnp.sum(acc_ref[8 * CFW + 8:8 * CFW + 16, :], axis=0, keepdims=True)

    vec = _const((1, D))
    return _call(
        body, name=name, grid=(t // tm,),
        in_specs=[_rows(tm, D), _next(hb, D, tm, t), _rows(tm, D), _next(hb, D, tm, t), _rows(tm, 2 * D),
                  _const((CFW, D)), vec, vec],
        out_specs=[_rows(tm, 2 * D), _const((40, D), single=False)],
        out_shape=[jax.ShapeDtypeStruct((t, 2 * D), BF), jax.ShapeDtypeStruct((40, D), F32)],
        scratch_shapes=[pltpu.VMEM((8, tm + hb, D), F32), pltpu.VMEM((8 * (CFW + 2), D), F32)],
        args=[ds, ds, u2, u2, a, w_dw, ln_g, ln_b], carry=carry)


def pool_bwd(dxo, m, ypre, pw, pscale, gn, gt, *, tm, name):
    t = dxo.shape[0]
    hb = 16

    def dyp_of(dxv, mv, ypv, ps, gnv, gtv):
        dm, dgt, dgn = _gated_res_bwd(dxv, mv, gnv, gtv)
        return dm * ps, dgt, dgn, _rsum(dm * ypv)

    def body(dx_ref, dxn_ref, m_ref, mn_ref, yp_ref, ypn_ref, pw_ref, ps_ref, gn_ref, gt_ref,
             dh_ref, dyp_ref, st_ref, sa_ref, sb_ref):
        i = pl.program_id(0)
        last = i == pl.num_programs(0) - 1
        _init_stats(st_ref)
        ps, gnv, gtv = ps_ref[...], gn_ref[...], gt_ref[...]
        dyp, dgt, dgn, dps = dyp_of(dx_ref[...], m_ref[...].astype(F32), yp_ref[...].astype(F32), ps, gnv, gtv)
        st_ref[0:1, :] += dgt
        st_ref[1:2, :] += dgn
        st_ref[2:3, :] += dps
        st_ref[3:4, :] += _rsum(dyp)
        dypb = dyp.astype(BF)
        dyp_ref[...] = dypb
        dypn, _, _, _ = dyp_of(dxn_ref[...], mn_ref[...].astype(F32), ypn_ref[...].astype(F32), ps, gnv, gtv)
        dypnb = jnp.where(last, 0.0, dypn).astype(BF)
        dpo = []
        for gi, w in enumerate(POOL_WINDOWS):
            c0 = gi * PG
            dp_main = _dot_nt(dypb[:, c0:c0 + PG], pw_ref[gi])
            dp_next = _dot_nt(dypnb[:, c0:c0 + PG], pw_ref[gi])
            dpo.append(dp_main)
            sa_ref[0:tm, c0:c0 + PG] = dp_main / _pool_counts(i, tm, w)
            sa_ref[tm:tm + hb, c0:c0 + PG] = dp_next / float(w)
        zero = jnp.zeros((8, D), F32)
        sa_ref[tm + hb:tm + hb + 8, :] = zero
        sb_ref[tm + hb:tm + hb + 8, :] = zero
        n = tm + hb
        src, dst = sa_ref, sb_ref
        for gi, w in enumerate(POOL_WINDOWS):
            c0 = gi * PG
            step = w // 2
            dst[pl.ds(0, n), c0:D] = src[pl.ds(0, n), c0:D] + src[pl.ds(step, n), c0:D]
            dh_ref[:, c0:c0 + PG] = dst[pl.ds(0, tm), c0:c0 + PG] - dpo[gi]
            src, dst = dst, src

    vec = _const((1, D))
    return pl.pallas_call(
        body, name=name, grid=(t // tm,),
        in_specs=[_rows(tm, D), _next(hb, D, tm, t), _rows(tm, D), _next(hb, D, tm, t), _rows(tm, D),
                  _next(hb, D, tm, t), _const((4, PG, PG)), vec, vec, vec],
        out_specs=[_rows(tm, D), _rows(tm, D), _const((8, D), single=False)],
        out_shape=[jax.ShapeDtypeStruct((t, D), F32), jax.ShapeDtypeStruct((t, D), BF), jax.ShapeDtypeStruct((8, D), F32)],
        scratch_shapes=[pltpu.VMEM((tm + hb + 8, D), F32)] * 2, compiler_params=_params(1),
    )(dxo, dxo, m, m, ypre, ypre, pw, pscale, gn, gt)


def ffn_bwd_mid(da, gc, up, convw, *, tm, name, carry=None):
    t = da.shape[1]

    def dgc_of(dav, gcv, valv):
        sg = _sigmoid(gcv)
        return dav * valv * (sg * (1.0 + gcv * (1.0 - sg))), dav * (gcv * sg)

    rc = 16

    def body(da_ref, dan_ref, gc_ref, gcn_ref, val_ref, valn_ref, gate_ref, cw_ref, dup_ref, st_ref, acc_ref):
        i = pl.program_id(0)
        last = i == pl.num_programs(0) - 1
        _init_stats(acc_ref)
        for j in range(4):
            w0, w1, w2 = cw_ref[j, 0:1, :], cw_ref[j, 1:2, :], cw_ref[j, 2:3, :]
            dgcn, _ = dgc_of(dan_ref[j, 0:8, :].astype(F32), gcn_ref[j, 0:8, :].astype(F32), valn_ref[j, 0:8, :].astype(F32))
            below = jnp.where(last, 0.0, dgcn)

            def step(r0, carry):
                rows = pl.ds(r0, rc)
                dgc, dval = dgc_of(da_ref[j, rows, :].astype(F32), gc_ref[j, rows, :].astype(F32), val_ref[j, rows, :].astype(F32))
                dup_ref[4 + j, rows, :] = dval.astype(BF)
                ext = jnp.concatenate([dgc, carry], axis=0)
                e1, e2 = ext[1:rc + 1], ext[2:rc + 2]
                dup_ref[j, rows, :] = (w2 * dgc + w1 * e1 + w0 * e2).astype(BF)
                gate = gate_ref[j, rows, :].astype(F32)
                acc_ref[j, 0:8, :] += _fold8(dgc)
                acc_ref[j, 8:16, :] += _fold8(gate * e2)
                acc_ref[j, 16:24, :] += _fold8(gate * e1)
                acc_ref[j, 24:32, :] += _fold8(gate * dgc)
                return dgc[0:8]

            _chunks(tm, rc, step, below, reverse=True)

        @pl.when(last)
        def _():
            for j in range(4):
                for q in range(4):
                    st_ref[j, q:q + 1, :] = jnp.sum(acc_ref[j, 8 * q:8 * q + 8, :], axis=0, keepdims=True)
                st_ref[j, 4:8, :] = jnp.zeros((4, FB), F32)

    return _call(
        body, name=name, grid=(t // tm,),
        in_specs=[_brows(4, tm, FB), _bnext(4, 16, FB, tm, t), _brows(4, tm, FB), _bnext(4, 16, FB, tm, t),
                  _brows(4, tm, FB, 1), _bnext(4, 16, FB, tm, t, 1), _brows(4, tm, FB, 0), _const((4, 3, FB))],
        out_specs=[_brows(8, tm, FB), _const((4, 8, FB), single=False)],
        out_shape=[jax.ShapeDtypeStruct((8, t, FB), BF), jax.ShapeDtypeStruct((4, 8, FB), F32)],
        scratch_shapes=[pltpu.VMEM((4, 32, FB), F32)], args=[da, da, gc, gc, up, up, up, convw], carry=carry)


def ffn_bwd_out(dxo, f, gn, gt, w_down, gc, up, *, tm, name, carry=None):
    t = dxo.shape[0]

    def body(dx_ref, f_ref, gn_ref, gt_ref, w_ref, gc_ref, val_ref, df_ref, dgc_ref, dval_ref, st_ref, sc_ref):
        _init_stats(st_ref)
        _init_stats(sc_ref)
        dm, dgt, dgn = _gated_res_bwd(dx_ref[...], f_ref[...].astype(F32), gn_ref[...], gt_ref[...])
        st_ref[0:1, :] += dgt
        st_ref[1:2, :] += dgn
        dmb = dm.astype(BF)
        df_ref[...] = dmb
        for j in range(4):
            da = _dot_nt(dmb, w_ref[j * FB:(j + 1) * FB, :])
            gcv = gc_ref[j].astype(F32)
            sg = _sigmoid(gcv)
            dval_ref[j] = (da * (gcv * sg)).astype(BF)
            dgc = da * val_ref[j].astype(F32) * (sg * (1.0 + gcv * (1.0 - sg)))
            dgc_ref[j] = dgc.astype(BF)
            sc_ref[j, 0:1, :] += _rsum(dgc)

    vec = _const((1, D))
    return _call(
        body, name=name, grid=(t // tm,),
        in_specs=[_rows(tm, D), _rows(tm, D), vec, vec, _const((F, D)), _brows(4, tm, FB), _brows(4, tm, FB, 1)],
        out_specs=[_rows(tm, D), _brows(4, tm, FB), _brows(4, tm, FB, 1), _const((8, D), single=False),
                   _const((4, 8, FB), single=False)],
        out_shape=[jax.ShapeDtypeStruct((t, D), BF), jax.ShapeDtypeStruct((4, t, FB), BF), jax.ShapeDtypeStruct((8, t, FB), BF),
                   jax.ShapeDtypeStruct((8, D), F32), jax.ShapeDtypeStruct((4, 8, FB), F32)],
        args=[dxo, f, gn, gt, w_down, gc, up], carry=carry)


def ffn_bwd_in(dgc, dup, up, convw, w, x, g, sc, dxo, *, tm, name, carry=None):
    t = x.shape[0]

    def body(dgc_ref, dgcn_ref, dval_ref, gate_ref, cw_ref, w_ref, x_ref, g_ref, sc_ref, dxo_ref,
             dx_ref, dgate_ref, st_ref, sc2_ref):
        last = pl.program_id(0) == pl.num_programs(0) - 1
        _init_stats(st_ref)
        _init_stats(sc2_ref)
        dh = jnp.zeros((tm, D), F32)
        for j in range(4):
            dgc = dgc_ref[j].astype(F32)
            ext = jnp.concatenate([dgc, jnp.where(last, 0.0, dgcn_ref[j, 0:8, :].astype(F32))], axis=0)
            e1, e2 = ext[1:tm + 1], ext[2:tm + 2]
            dgate = (cw_ref[j, 2:3, :] * dgc + cw_ref[j, 1:2, :] * e1 + cw_ref[j, 0:1, :] * e2).astype(BF)
            dgate_ref[j] = dgate
            dh = dh + _dot(dgate, w_ref[j])
            gate = gate_ref[j].astype(F32)
            sc2_ref[j, 1:2, :] += _rsum(gate * e2)
            sc2_ref[j, 2:3, :] += _rsum(gate * e1)
            sc2_ref[j, 3:4, :] += _rsum(gate * dgc)
        for j in range(4):
            dh = dh + _dot(dval_ref[j], w_ref[4 + j])
        x = x_ref[...]
        r = lax.rsqrt(jnp.mean(x * x, axis=-1, keepdims=True) + RMS_EPS)
        xh = x * r
        gv = g_ref[...]
        st_ref[0:1, :] += _rsum(dh)
        st_ref[1:2, :] += _rsum(dh * (xh * gv))
        dn = dh * (1.0 + sc_ref[...])
        st_ref[2:3, :] += _rsum(dn * xh)
        dy = dn * gv
        dx_ref[...] = dxo_ref[...] + r * (dy - xh * jnp.mean(dy * xh, axis=-1, keepdims=True))

    vec = _const((1, D))
    return _call(
        body, name=name, grid=(t // tm,),
        in_specs=[_brows(4, tm, FB), _bnext(4, 16, FB, tm, t), _brows(4, tm, FB, 1), _brows(4, tm, FB, 0), _const((4, 3, FB)),
                  _const(w.shape), _rows(tm, D), vec, vec, _rows(tm, D)],
        out_specs=[_rows(tm, D), _brows(4, tm, FB, 0), _const((8, D), single=False), _const((4, 8, FB), single=False)],
        out_shape=[jax.ShapeDtypeStruct((t, D), F32), jax.ShapeDtypeStruct((8, t, FB), BF), jax.ShapeDtypeStruct((8, D), F32),
                   jax.ShapeDtypeStruct((4, 8, FB), F32)],
        args=[dgc, dgc, dup, up, convw, w, x, g, sc, dxo], aliases={2: 1}, carry=carry)


def wgrad(a, b, *, nblk, a_blocked, b_blocked, bk, bn, tt, name, carry=None):
    t = a.shape[1] if a.ndim == 3 else a.shape[0]
    nt = t // tt

    def body(a_ref, b_ref, o_ref, acc_ref):
        s = pl.program_id(1)

        @pl.when(s == 0)
        def _():
            acc_ref[...] = jnp.zeros_like(acc_ref)

        av = a_ref[0] if a.ndim == 3 else a_ref[...]
        bv = b_ref[0] if b.ndim == 3 else b_ref[...]
        acc_ref[...] += _dot_tn(av, bv)

        @pl.when(s == nt - 1)
        def _():
            o_ref[0] = acc_ref[...].astype(BF)

    def spec(arr, blocked, width):
        if arr.ndim == 3:
            return pl.BlockSpec((1, tt, width), lambda j, s: (j, s, 0))
        if blocked:
            return pl.BlockSpec((tt, width), lambda j, s: (s, j))
        return pl.BlockSpec((tt, width), lambda j, s: (s, 0))

    return _call(
        body, name=name, grid=(nblk, nt), in_specs=[spec(a, a_blocked, bk), spec(b, b_blocked, bn)],
        out_specs=[pl.BlockSpec((1, bk, bn), lambda j, s: (j, 0, 0))],
        out_shape=[jax.ShapeDtypeStruct((nblk, bk, bn), BF)],
        scratch_shapes=[pltpu.VMEM((bk, bn), F32)], args=[a, b], carry=carry)


def mod_partial(c_all, w_mod):
    cols = w_mod.shape[2]

    def body(c_ref, w_ref, o_ref):
        c = c_ref[...]
        ca = c * _sigmoid(c)
        o_ref[0] = jnp.dot(ca, w_ref[0], preferred_element_type=F32, precision=lax.Precision.HIGHEST)

    return pl.pallas_call(
        body, name="mod_partial", grid=(DEPTH,),
        in_specs=[pl.BlockSpec((NDEV, D), lambda l: (0, 0)), pl.BlockSpec((1, D, cols), lambda l: (l, 0, 0))],
        out_specs=pl.BlockSpec((1, NDEV, cols), lambda l: (l, 0, 0)),
        out_shape=jax.ShapeDtypeStruct((DEPTH, NDEV, cols), F32), compiler_params=_params(1))(c_all, w_mod)


def mod_finish(parts, b_mod):
    cols = parts.shape[2]

    def body(p_ref, b_ref, o_ref):
        for e in range(NDEV):
            o_ref[:, e * cols:(e + 1) * cols] = p_ref[e] + b_ref[:, e * cols:(e + 1) * cols]

    return pl.pallas_call(
        body, name="mod_finish", out_shape=jax.ShapeDtypeStruct((DEPTH, NDEV * cols), F32))(parts, b_mod)


def sum_parts(parts):
    n, r, c = parts.shape

    def body(p_ref, o_ref):
        acc = p_ref[0]
        for j in range(1, n):
            acc = acc + p_ref[j]
        o_ref[...] = acc

    return pl.pallas_call(body, name="sum_parts", out_shape=jax.ShapeDtypeStruct((r, c), F32))(parts)


def mod_wgrad(c_all_t, gmod_cols):
    cols = gmod_cols.shape[2]

    def body(c_ref, g_ref, o_ref):
        c = c_ref[...]
        ca = c * _sigmoid(c)
        acc = ca[:, 0:1] * g_ref[0, 0:1, :]
        for b in range(1, NDEV):
            acc = acc + ca[:, b:b + 1] * g_ref[0, b:b + 1, :]
        o_ref[0] = acc

    return pl.pallas_call(
        body, name="mod_wgrad", grid=(DEPTH,),
        in_specs=[pl.BlockSpec((D, NDEV), lambda l: (0, 0)), pl.BlockSpec((1, NDEV, cols), lambda l: (l, 0, 0))],
        out_specs=pl.BlockSpec((1, D, cols), lambda l: (l, 0, 0)),
        out_shape=jax.ShapeDtypeStruct((DEPTH, D, cols), F32), compiler_params=_params(1))(c_all_t, gmod_cols)


def _adamw_math(g, w, m, v):
    m2 = B1 * m + (1.0 - B1) * g
    v2 = B2 * v + (1.0 - B2) * (g * g)
    m_hat = m2 / (1.0 - B1 ** STEP)
    v_hat = v2 / (1.0 - B2 ** STEP)
    delta = -LR * (m_hat / (jnp.sqrt(v_hat) + ADAM_EPS) + WD * w)
    return delta, m2, v2


def _row_tile(r, c, budget=1 << 18):
    if r * c <= budget or r % 8:
        return r
    best = 8
    for cand in range(8, r + 1, 8):
        if r % cand == 0 and cand * c <= budget:
            best = cand
    return best


def adamw_sum(parts, w, m, v, *, name):
    n, r, c = parts.shape
    tr = _row_tile(r, c)

    def body(p_ref, w_ref, m_ref, v_ref, g_ref, d_ref, m2_ref, v2_ref):
        g = p_ref[0].astype(F32)
        for j in range(1, n):
            g = g + p_ref[j].astype(F32)
        d, m2, v2 = _adamw_math(g, w_ref[...], m_ref[...], v_ref[...])
        g_ref[...] = g
        d_ref[...] = d
        m2_ref[...] = m2
        v2_ref[...] = v2

    blk = pl.BlockSpec((tr, c), lambda i: (i, 0))
    out = jax.ShapeDtypeStruct((r, c), F32)
    return pl.pallas_call(
        body, name=name, grid=(r // tr,), in_specs=[pl.BlockSpec((n, tr, c), lambda i: (0, i, 0)), blk, blk, blk],
        out_specs=[blk] * 4, out_shape=[out] * 4, compiler_params=_params(1))(parts, w, m, v)


def adamw_layer(parts, w, m, v, prev, layer, *, name):
    n, r, c = parts.shape
    nl = w.shape[0]
    tr = _row_tile(r, c)

    def body(p_ref, w_ref, m_ref, v_ref, *rest):
        g_ref, d_ref, m2_ref, v2_ref = rest[-4:]
        g = p_ref[0].astype(F32)
        for j in range(1, n):
            g = g + p_ref[j].astype(F32)
        d, m2, v2 = _adamw_math(g, w_ref[0], m_ref[0], v_ref[0])
        g_ref[0] = g
        d_ref[0] = d
        m2_ref[0] = m2
        v2_ref[0] = v2

    blk = pl.BlockSpec((1, tr, c), lambda i: (layer, i, 0))
    in_specs = [pl.BlockSpec((n, tr, c), lambda i: (0, i, 0)), blk, blk, blk]
    args = [parts, w, m, v]
    aliases = {}
    if prev is not None:
        in_specs += [ANY] * 4
        args += list(prev)
        aliases = {4 + k: k for k in range(4)}
    out = jax.ShapeDtypeStruct((nl, r, c), F32)
    return pl.pallas_call(
        body, name=name, grid=(r // tr,), in_specs=in_specs, out_specs=[blk] * 4, out_shape=[out] * 4,
        input_output_aliases=aliases, compiler_params=_params(1))(*args)


def _pack(arrays):
    flat, layout, off = [], [], 0
    for a in arrays:
        flat.append(a.reshape(-1))
        layout.append((off, a.shape))
        off += a.size
    pad = (-off) % 1024
    if pad:
        flat.append(jnp.zeros((pad,), F32))
    return jnp.concatenate(flat).reshape(-1, 128), layout


def _unpack(packed, layout, lead=()):
    flat = packed.reshape(lead + (-1,))
    return [flat[..., off:off + _size(shape)].reshape(lead + tuple(shape)) for off, shape in layout]


def _size(shape):
    n = 1
    for s in shape:
        n *= s
    return n


def _join_last(g):
    g = jnp.moveaxis(g, 0, -2)
    return g.reshape(g.shape[:-2] + (g.shape[-2] * g.shape[-1],))


def _my_cols(a, width):
    return lax.dynamic_slice_in_dim(a, _my_id() * width, width, axis=a.ndim - 1)


def _tile(t, pref):
    return min(pref, t)


def kernel(x, c, w_mod, b_mod, norm_g, sc_w_in, sc_conv, sc_w_out, pool_w, pool_b, pool_scale, cf_w_pw1, cf_b_pw1, cf_w_dw, cf_b_dw, cf_ln_g, cf_ln_b, cf_w_pw2, cf_b_pw2, ffn_w_up, ffn_conv, ffn_b_conv, ffn_w_down, loss_target, m_w_mod, m_b_mod, m_norm_g, m_sc_w_in, m_sc_conv, m_sc_w_out, m_pool_w, m_pool_b, m_pool_scale, m_cf_w_pw1, m_cf_b_pw1, m_cf_w_dw, m_cf_b_dw, m_cf_ln_g, m_cf_ln_b, m_cf_w_pw2, m_cf_b_pw2, m_ffn_w_up, m_ffn_conv, m_ffn_b_conv, m_ffn_w_down, v_w_mod, v_b_mod, v_norm_g, v_sc_w_in, v_sc_conv, v_sc_w_out, v_pool_w, v_pool_b, v_pool_scale, v_cf_w_pw1, v_cf_b_pw1, v_cf_w_dw, v_cf_b_dw, v_cf_ln_g, v_cf_ln_b, v_cf_w_pw2, v_cf_b_pw2, v_ffn_w_up, v_ffn_conv, v_ffn_b_conv, v_ffn_w_down):
    env = dict(locals())
    names = ["w_mod", "b_mod", "norm_g", "sc_w_in", "sc_conv", "sc_w_out", "pool_w", "pool_b", "pool_scale", "cf_w_pw1",
             "cf_b_pw1", "cf_w_dw", "cf_b_dw", "cf_ln_g", "cf_ln_b", "cf_w_pw2", "cf_b_pw2", "ffn_w_up", "ffn_conv",
             "ffn_b_conv", "ffn_w_down"]
    t = x.shape[1]
    tm = _tile(t, 512)
    tm_ffn = _tile(t, 256)
    tt = _tile(t, 2048)
    x0, target = x[0], loss_target[0]

    small_names = ["norm_g", "sc_conv", "cf_b_pw1", "cf_w_dw", "cf_b_dw", "cf_ln_g", "cf_ln_b", "cf_b_pw2", "ffn_conv"]
    packed, layout = _pack([c] + [env[n] for n in small_names])

    shard = {"pool": pool_w[0].astype(BF), "pw1": cf_w_pw1[0].astype(BF), "pw2": cf_w_pw2[0].astype(BF)}
    for j in range(2):
        shard[f"in{j}"], shard[f"out{j}"] = sc_w_in[j].astype(BF), sc_w_out[j].astype(BF)
    for l in range(DEPTH):
        shard[f"up{l}"], shard[f"down{l}"] = ffn_w_up[l].T.astype(BF), ffn_w_down[l].astype(BF)
    gathered, g_in0, g_out0 = _Exchange("gather", [packed, shard["in0"], shard["out0"]]).run("gather_first")
    wg = {"in0": g_in0, "out0": g_out0}
    parts = _unpack(gathered, layout, lead=(NDEV,))
    c_all = parts[0].reshape(NDEV, D)
    full = {n: _join_last(p) for n, p in zip(small_names, parts[1:])}
    fwd_plan = {("mix_in", 0): ["up0"], ("mix_out", 0): ["down0"], ("ffn_in", 0): ["pool", "up1"],
                ("ffn_out", 0): ["down1", "pw1", "pw2"], ("ffn_in", 1): ["up2"], ("ffn_out", 1): ["down2", "in1", "out1"],
                ("ffn_in", 2): ["up3"], ("ffn_out", 2): ["down3"]}

    def carrying(plan, kind, store, source, fn, key, *a, **k):
        names = plan.get(key)
        if not names:
            return fn(*a, **k)
        res = fn(*a, carry=_Exchange(kind, [source[n] for n in names]), **k)
        store.update(zip(names, res[-1]))
        return res[:-1]

    fwd = functools.partial(carrying, fwd_plan, "gather", wg, shard)

    mp = mod_partial(c_all, w_mod)
    (mod_parts,) = _Exchange("scatter", [jnp.swapaxes(mp, 0, 1)]).run("exchange_mod")
    mod = mod_finish(mod_parts, b_mod)

    def vec(a):
        return a.reshape(1, -1)

    def col_blocks(g):
        w = jnp.swapaxes(g, 0, 1).reshape(D, -1)
        return jnp.swapaxes(w.reshape(D, -1, D), 0, 1)

    def ffn_blocks(a):
        return jnp.swapaxes(a.reshape(a.shape[0], 4, FB), 0, 1)

    saved = []
    xs = x0
    for l in range(DEPTH):
        sh1, sc1, g1, sh2, sc2, g2 = [mod[l:l + 1, k * D:(k + 1) * D] for k in range(6)]
        ng = [full["norm_g"][l, k:k + 1] for k in range(4)]
        kind, j = l % 3, l // 3
        s = dict(x_in=xs, sc1=sc1, g1=g1, sc2=sc2, g2=g2, ng=ng)
        if kind == 0:
            s["w_in"] = col_blocks(wg[f"in{j}"])
            s["h"], s["p"] = fwd(fwd_in, ("mix_in", l), xs, ng[0], sc1, sh1, s["w_in"], None, blocked=False, tm=tm,
                                 name=f"sc_in_{l}")
            x1, s["m"], s["q"] = fwd(sc_fwd_out, ("mix_out", l), s["p"], full["sc_conv"][j], wg[f"out{j}"].reshape(D, D), xs,
                                     ng[1], g1, tm=tm, name=f"sc_out_{l}")
        elif kind == 1:
            pool_w_f = jnp.swapaxes(wg["pool"], 0, 1).reshape(4, PG, PG)
            x1, s["m"], s["ypre"], s["pooled"] = pool_fwd(xs, ng[0], sc1, sh1, pool_w_f, pool_b, pool_scale, ng[1], g1,
                                                          tm=tm, name=f"pool_{l}")
        else:
            s["w_in"] = col_blocks(wg["pw1"])
            s["h"], s["a"] = fwd_in(xs, ng[0], sc1, sh1, s["w_in"], full["cf_b_pw1"].reshape(2, 1, D), blocked=False, tm=tm,
                                    name=f"cf_in_{l}")
            x1, s["m"], s["s"], s["u2"] = cf_fwd_out(s["a"], full["cf_w_dw"][0], full["cf_b_dw"], full["cf_ln_g"],
                                                     full["cf_ln_b"], wg["pw2"].reshape(D, D), full["cf_b_pw2"], xs, ng[1],
                                                     g1, tm=tm, name=f"cf_out_{l}")
        s["x1"] = x1
        s["cw"] = ffn_blocks(full["ffn_conv"][l])
        s["h2"], s["up"] = fwd(fwd_in, ("ffn_in", l), x1, ng[2], sc2, sh2, wg[f"up{l}"], None, blocked=True, wt=True, tm=tm,
                               name=f"ffn_in_{l}")
        xs, s["f"], s["gc"], s["fa"] = fwd(ffn_fwd_out, ("ffn_out", l), s["up"], s["cw"], ffn_blocks(ffn_b_conv[l:l + 1]),
                                           wg[f"down{l}"].reshape(F, D), x1, ng[3], g2, tm=tm_ffn, name=f"ffn_out_{l}")
        saved.append(s)

    dx, loss_part = loss_head(xs, target, tm=tm, name="loss_head")
    loss = lax.psum(loss_part[0, 0], ("x", "y", "c"))

    gmod = [None] * DEPTH
    d_norm_g = [None] * DEPTH
    d_ffn_conv = [None] * DEPTH
    d_ffn_b_conv = [None] * DEPTH
    d_sc_conv = [None] * 2
    big = {}
    got = {}
    small_g = {}
    bwd_plan = {("mix_bout", 3): ["down3"], ("mix_bin", 3): ["up3"], ("ffn_bout", 2): ["in1", "out1"],
                ("mix_bmid", 2): ["up2", "down2"], ("ffn_bout", 1): ["pw1", "pw2"], ("ffn_bout", 0): ["pool", "down1"],
                ("ffn_bin", 0): ["up1"], ("mix_bout", 0): ["down0"], ("mix_win", 0): ["up0"]}
    bwd = functools.partial(carrying, bwd_plan, "scatter", got, big)
    pool_w_f = jnp.swapaxes(wg["pool"], 0, 1).reshape(4, PG, PG)
    for l in reversed(range(DEPTH)):
        s = saved[l]
        ng = s["ng"]
        kind, j = l % 3, l // 3
        df, dgc, dup, st_o, st_b = bwd(ffn_bwd_out, ("ffn_bout", l), dx, s["f"], ng[3], s["g2"], wg[f"down{l}"].reshape(F, D),
                                       s["gc"], s["up"], tm=tm, name=f"ffn_bout_{l}")
        dx1, dup, st_i, st_c = bwd(ffn_bwd_in, ("ffn_bin", l), dgc, dup, s["up"], s["cw"], wg[f"up{l}"], s["x1"], ng[2],
                                   s["sc2"], dx, tm=tm, name=f"ffn_bin_{l}")
        (big[f"up{l}"],) = wgrad(dup, s["h2"], nblk=NDEV, a_blocked=True, b_blocked=False, bk=FB, bn=D, tt=tt,
                                 name=f"ffn_wup_{l}")
        big[f"down{l}"] = wgrad(s["fa"], df, nblk=4, a_blocked=True, b_blocked=False, bk=FB, bn=D, tt=tt,
                                name=f"ffn_wdown_{l}")[0].reshape(NDEV, F // NDEV, D)
        d_ffn_b_conv[l] = st_b[:, 0, :].reshape(F)
        d_ffn_conv[l] = jnp.swapaxes(st_c[:, 1:4, :], 0, 1).reshape(3, F)
        g_ffn = [st_i[0], st_i[1], st_o[0]]
        dn3, dn2 = st_o[1], st_i[2]
        if kind == 0:
            dm, dbg, du, st_o = bwd(sc_bwd_out, ("mix_bout", l), dx1, s["m"], ng[1], s["g1"], wg[f"out{j}"].reshape(D, D), s["p"],
                                    full["sc_conv"][j], tm=tm, name=f"sc_bout_{l}")
            dx, dp, st_i, st_c = bwd(sc_bwd_in, ("mix_bin", l), du, dbg, s["p"], full["sc_conv"][j], s["w_in"], s["x_in"], ng[0],
                                     s["sc1"], dx1, tm=tm, name=f"sc_bin_{l}")
            (big[f"in{j}"],) = bwd(wgrad, ("mix_win", l), s["h"], dp, nblk=NDEV, a_blocked=False, b_blocked=True, bk=D,
                                   bn=3 * D // NDEV, tt=tt, name=f"sc_win_{l}")
            big[f"out{j}"] = wgrad(s["q"], dm, nblk=1, a_blocked=False, b_blocked=False, bk=D, bn=D, tt=tt,
                                   name=f"sc_wout_{l}")[0].reshape(NDEV, D // NDEV, D)
            d_sc_conv[j] = st_c[0:3]
        elif kind == 1:
            dh, dyp, st_o = pool_bwd(dx1, s["m"], s["ypre"], pool_w_f, pool_scale, ng[1], s["g1"], tm=tm, name=f"pool_b_{l}")
            dx, st_i = bwd_in([dh], None, s["x_in"], ng[0], s["sc1"], dx1, tm=tm, name=f"pool_bin_{l}")
            (dpw,) = wgrad(s["pooled"], dyp, nblk=4, a_blocked=True, b_blocked=True, bk=PG, bn=PG, tt=tt, name=f"pool_w_{l}")
            big["pool"] = jnp.swapaxes(dpw.reshape(4, NDEV, PG // NDEV, PG), 0, 1).reshape(NDEV, 4 * PG // NDEV, PG)
            small_g["pool_scale"], small_g["pool_b"] = st_o[2:3], st_o[3:4]
        else:
            dm, ds, st_o = bwd_out(dx1, s["m"], ng[1], s["g1"], wg["pw2"].reshape(D, D), blocked=False, tm=tm,
                                   name=f"cf_bout_{l}")
            dA, st_c = bwd(cf_bwd_mid, ("mix_bmid", l), ds, s["u2"], s["a"], full["cf_w_dw"][0], full["cf_ln_g"],
                           full["cf_ln_b"], tm=tm, name=f"cf_bmid_{l}")
            dx, st_i = bwd_in([dA], s["w_in"], s["x_in"], ng[0], s["sc1"], dx1, tm=tm, name=f"cf_bin_{l}")
            (big["pw1"],) = wgrad(s["h"], dA, nblk=NDEV, a_blocked=False, b_blocked=True, bk=D, bn=2 * D // NDEV, tt=tt,
                                  name=f"cf_wpw1_{l}")
            big["pw2"] = wgrad(s["s"], dm, nblk=1, a_blocked=False, b_blocked=False, bk=D, bn=D, tt=tt,
                               name=f"cf_wpw2_{l}")[0].reshape(NDEV, D // NDEV, D)
            small_g["cf_w_dw"] = st_c[0:CFW][None]
            small_g["cf_b_dw"], small_g["cf_ln_g"], small_g["cf_ln_b"] = st_c[31:32], st_c[32:33], st_c[33:34]
            small_g["cf_b_pw1"] = st_c[34:36].reshape(1, 2 * D)
            small_g["cf_b_pw2"] = st_o[2:3]
        gmod[l] = jnp.concatenate([st_i[0], st_i[1], st_o[0]] + g_ffn)
        d_norm_g[l] = jnp.stack([st_i[2], st_o[1], dn2, dn3])

    small_g["gmod"] = jnp.stack(gmod)
    small_g["norm_g"] = jnp.stack(d_norm_g)
    small_g["sc_conv"] = jnp.stack(d_sc_conv)
    small_g["ffn_conv"] = jnp.stack(d_ffn_conv)
    small_g["ffn_b_conv"] = jnp.stack(d_ffn_b_conv)
    sg_names = ["gmod", "norm_g", "sc_conv", "pool_b", "pool_scale", "cf_b_pw1", "cf_w_dw", "cf_b_dw", "cf_ln_g", "cf_ln_b",
                "cf_b_pw2", "ffn_conv", "ffn_b_conv"]
    gpacked, glayout = _pack([small_g[n] for n in sg_names])
    (ggath,), (got["in0"], got["out0"]) = _run_exchanges(
        [_Exchange("gather", [gpacked]), _Exchange("scatter", [big["in0"], big["out0"]])], "exchange_last")
    gsum = dict(zip(sg_names, _unpack(sum_parts(ggath), glayout)))
    gmod_all = _unpack(ggath, glayout[:1], lead=(NDEV,))[0]
    grads = {"b_mod": gsum["gmod"], "pool_b": gsum["pool_b"], "pool_scale": gsum["pool_scale"],
             "ffn_b_conv": gsum["ffn_b_conv"]}
    for n in ["norm_g", "sc_conv", "cf_b_pw1", "cf_w_dw", "cf_b_dw", "cf_ln_g", "cf_ln_b", "cf_b_pw2", "ffn_conv"]:
        grads[n] = _my_cols(gsum[n], env[n].shape[-1])
    grads["w_mod"] = mod_wgrad(c_all.T, jnp.swapaxes(_my_cols(gmod_all, w_mod.shape[2]), 0, 1))

    deltas, new_m, new_v = {}, {}, {}
    sp_names = ["b_mod", "norm_g", "sc_conv", "pool_b", "pool_scale", "cf_b_pw1", "cf_w_dw", "cf_b_dw", "cf_ln_g", "cf_ln_b",
                "cf_b_pw2", "ffn_conv", "ffn_b_conv"]
    pg, playout = _pack([grads[n] for n in sp_names])
    pw_, _ = _pack([env[n] for n in sp_names])
    pm_, _ = _pack([env["m_" + n] for n in sp_names])
    pv_, _ = _pack([env["v_" + n] for n in sp_names])
    _, sd, sm, sv = adamw_sum(pg[None], pw_, pm_, pv_, name="adamw_small")
    for n, d_, m_, v_ in zip(sp_names, _unpack(sd, playout), _unpack(sm, playout), _unpack(sv, playout)):
        deltas[n], new_m[n], new_v[n] = d_, m_, v_
    gw = grads["w_mod"].reshape(1, DEPTH * D, -1)
    _, d_, m_, v_ = adamw_sum(gw, w_mod.reshape(gw.shape[1:]), m_w_mod.reshape(gw.shape[1:]), v_w_mod.reshape(gw.shape[1:]),
                              name="adamw_w_mod")
    deltas["w_mod"], new_m["w_mod"], new_v["w_mod"] = [a.reshape(w_mod.shape) for a in (d_, m_, v_)]

    groups = {"sc_w_in": ["in0", "in1"], "sc_w_out": ["out0", "out1"], "pool_w": ["pool"], "cf_w_pw1": ["pw1"],
              "cf_w_pw2": ["pw2"], "ffn_w_up": [f"up{l}" for l in range(DEPTH)], "ffn_w_down": [f"down{l}" for l in range(DEPTH)]}
    for n, layers in groups.items():
        stacked = (len(layers),) + got[layers[0]].shape[1:]
        flip = n == "ffn_w_up"
        w3 = [(jnp.swapaxes(env[p + n], 1, 2) if flip else env[p + n]).reshape(stacked) for p in ("", "m_", "v_")]
        outs = None
        for li, key in enumerate(layers):
            outs = adamw_layer(got[key], *w3, outs, li, name=f"adamw_{n}_{li}")
        grads[n], deltas[n], new_m[n], new_v[n] = [(jnp.swapaxes(a, 1, 2) if flip else a).reshape(env[n].shape) for a in outs]

    return (loss, dx[None], *[grads[n] for n in names], *[deltas[n] for n in names], *[new_m[n] for n in names],
            *[new_v[n] for n in names])
```

```python
import functools

import jax
import jax.numpy as jnp
from jax import lax
from jax.experimental import pallas as pl
from jax.experimental.pallas import tpu as pltpu

D = 1024
F = 2816
NDEV = 8
FB = F // 4
DEPTH = 4
RMS_EPS = 1e-6
LN_EPS = 1e-5
CFW = 31
POOL_WINDOWS = (2, 4, 8, 16)
PG = D // 4
LR, B1, B2, ADAM_EPS, WD, STEP = 0.001, 0.9, 0.999, 1e-08, 0.01, 10

BF = jnp.bfloat16
F32 = jnp.float32
VMEM_LIMIT_V7X = 56 * 1024 * 1024
MESH = pl.DeviceIdType.MESH
ANY = pl.BlockSpec(memory_space=pl.ANY)


def _params(n_axes):
    return pltpu.CompilerParams(dimension_semantics=("arbitrary",) * n_axes, vmem_limit_bytes=VMEM_LIMIT_V7X)


def _const(shape, single=True):
    nd = len(shape)
    if single:
        return pl.BlockSpec(shape, lambda *_: (0,) * nd, pipeline_mode=pl.Buffered(1))
    return pl.BlockSpec(shape, lambda *_: (0,) * nd)


def _rows(tm, c):
    return pl.BlockSpec((tm, c), lambda i: (i, 0))


def _brows(nb, tm, c, b0=0):
    return pl.BlockSpec((nb, tm, c), lambda i: (b0, i, 0))


def _prev(hb, c, tm):
    return pl.BlockSpec((hb, c), lambda i: (jnp.maximum(i * (tm // hb) - 1, 0), 0))


def _next(hb, c, tm, t):
    return pl.BlockSpec((hb, c), lambda i: (jnp.minimum((i + 1) * (tm // hb), t // hb - 1), 0))


def _bprev(nb, hb, c, tm, b0=0):
    return pl.BlockSpec((nb, hb, c), lambda i: (b0, jnp.maximum(i * (tm // hb) - 1, 0), 0))


def _bnext(nb, hb, c, tm, t, b0=0):
    return pl.BlockSpec((nb, hb, c), lambda i: (b0, jnp.minimum((i + 1) * (tm // hb), t // hb - 1), 0))


def _sigmoid(v):
    return 0.5 * jnp.tanh(0.5 * v) + 0.5


def _fold8(v):
    r, c = v.shape
    return jnp.sum(v.reshape(r // 8, 8, c), axis=0)


def _chunks(n_rows, rc, step, init=0, reverse=False):
    n = n_rows // rc

    def it(c, carry):
        idx = (n - 1 - c) if reverse else c
        return step(pl.multiple_of(idx * rc, rc), carry)

    return lax.fori_loop(0, n, it, init)


def _row_shifted_copies(s_ref, n):
    for b in range(1, 8):
        s_ref[b, 0:n, :] = s_ref[0, pl.ds(b, n), :]


def _shifted(s_ref, o, tm):
    return s_ref[o % 8, pl.ds(8 * (o // 8), tm), :]


def _dot(a, b):
    return jnp.dot(a, b, preferred_element_type=F32)


def _dot_nt(a, b):
    return lax.dot_general(a, b, (((1,), (1,)), ((), ())), preferred_element_type=F32)


def _dot_tn(a, b):
    return lax.dot_general(a, b, (((0,), (0,)), ((), ())), preferred_element_type=F32)


def _rsum(v):
    return jnp.sum(v, axis=0, keepdims=True)


def _adaln(x, g, sc, sh):
    r = lax.rsqrt(jnp.mean(x * x, axis=-1, keepdims=True) + RMS_EPS)
    return (x * r * g) * (1.0 + sc) + sh


def _gated_res(x, m, gn, gt):
    r = lax.rsqrt(jnp.mean(m * m, axis=-1, keepdims=True) + RMS_EPS)
    return x + gt * (m * r * gn)


def _gated_res_bwd(dxo, m, gn, gt):
    r = lax.rsqrt(jnp.mean(m * m, axis=-1, keepdims=True) + RMS_EPS)
    mh = m * r
    dgt = _rsum(dxo * (mh * gn))
    dn = dxo * gt
    dgn = _rsum(dn * mh)
    dmh = dn * gn
    dm = r * (dmh - mh * jnp.mean(dmh * mh, axis=-1, keepdims=True))
    return dm, dgt, dgn


def _my_id():
    return 4 * lax.axis_index("x") + 2 * lax.axis_index("y") + lax.axis_index("c")


def _peer(k):
    x, y, c = lax.axis_index("x"), lax.axis_index("y"), lax.axis_index("c")
    px = 1 - x if k & 4 else x
    py = 1 - y if k & 2 else y
    pc = 1 - c if k & 1 else c
    return (px, py, pc), 4 * px + 2 * py + pc


class _Exchange:
    def __init__(self, kind, arrays):
        self.gather = kind == "gather"
        self.arrays = list(arrays)
        n = len(self.arrays)
        if self.gather:
            self.out_shape = [jax.ShapeDtypeStruct((NDEV,) + a.shape, a.dtype) for a in self.arrays]
        else:
            self.out_shape = [jax.ShapeDtypeStruct(a.shape, a.dtype) for a in self.arrays]
        self.scratch = [pltpu.SemaphoreType.DMA((n * NDEV,)), pltpu.SemaphoreType.DMA((n * NDEV,)),
                        pltpu.SemaphoreType.DMA((n,))]

    def _local(self, a, src, dst, sems):
        me = _my_id()
        return pltpu.make_async_copy(src[a] if self.gather else src[a].at[me], dst[a].at[me], sems[2].at[a])

    def _remote(self, a, k, src, dst, sems, incoming):
        to, pid = _peer(k)
        me = _my_id()
        return pltpu.make_async_remote_copy(
            src_ref=src[a] if self.gather else src[a].at[pid], dst_ref=dst[a].at[pid if incoming else me],
            send_sem=sems[0].at[a * NDEV + k], recv_sem=sems[1].at[a * NDEV + k], device_id=to, device_id_type=MESH)

    def start(self, src, dst, sems):
        for a in range(len(self.arrays)):
            self._local(a, src, dst, sems).start()
        for k in range(1, NDEV):
            for a in range(len(self.arrays)):
                self._remote(a, k, src, dst, sems, False).start()

    def wait(self, src, dst, sems):
        for k in range(1, NDEV):
            for a in range(len(self.arrays)):
                self._remote(a, k, src, dst, sems, True).wait_recv()
        for k in range(1, NDEV):
            for a in range(len(self.arrays)):
                self._remote(a, k, src, dst, sems, False).wait_send()
        for a in range(len(self.arrays)):
            self._local(a, src, dst, sems).wait()

    def run(self, name):
        n = len(self.arrays)

        def body(*refs):
            src, dst, sems = refs[:n], refs[n:2 * n], refs[2 * n:]
            self.start(src, dst, sems)
            self.wait(src, dst, sems)

        return pl.pallas_call(body, name=name, in_specs=[ANY] * n, out_specs=[ANY] * n, out_shape=self.out_shape,
                              scratch_shapes=self.scratch)(*self.arrays)


def _run_exchanges(exchanges, name):
    counts = [len(e.arrays) for e in exchanges]
    n = sum(counts)

    def body(*refs):
        src, dst, sems = refs[:n], refs[n:2 * n], refs[2 * n:]
        parts, lo = [], 0
        for ei, (e, c) in enumerate(zip(exchanges, counts)):
            parts.append((e, src[lo:lo + c], dst[lo:lo + c], sems[3 * ei:3 * ei + 3]))
            lo += c
        for e, s, d, m in parts:
            e.start(s, d, m)
        for e, s, d, m in parts:
            e.wait(s, d, m)

    res = pl.pallas_call(
        body, name=name, in_specs=[ANY] * n, out_specs=[ANY] * n, out_shape=[s for e in exchanges for s in e.out_shape],
        scratch_shapes=[s for e in exchanges for s in e.scratch])(*[a for e in exchanges for a in e.arrays])
    out, lo = [], 0
    for c in counts:
        out.append(list(res[lo:lo + c]))
        lo += c
    return out


def _call(body, *, name, grid, in_specs, out_specs, out_shape, args, scratch_shapes=(), carry=None, aliases=None):
    cp = _params(len(grid))
    aliases = aliases or {}
    if carry is None:
        return tuple(pl.pallas_call(body, name=name, grid=grid, in_specs=in_specs, out_specs=out_specs, out_shape=out_shape,
                                    scratch_shapes=list(scratch_shapes), input_output_aliases=aliases,
                                    compiler_params=cp)(*args))
    n_in, n_out, n_sc, n_c = len(in_specs), len(out_specs), len(scratch_shapes), len(carry.arrays)

    def wrapped(*refs):
        ins, src = refs[:n_in], refs[n_in:n_in + n_c]
        outs = refs[n_in + n_c:n_in + n_c + n_out]
        dst = refs[n_in + n_c + n_out:n_in + 2 * n_c + n_out]
        rest = refs[n_in + 2 * n_c + n_out:]
        scr, sems = rest[:n_sc], rest[n_sc:]
        first = pl.program_id(0) == 0
        last = pl.program_id(0) == grid[0] - 1
        for ax in range(1, len(grid)):
            first = jnp.logical_and(first, pl.program_id(ax) == 0)
            last = jnp.logical_and(last, pl.program_id(ax) == grid[ax] - 1)

        @pl.when(first)
        def _():
            carry.start(src, dst, sems)

        body(*ins, *outs, *scr)

        @pl.when(last)
        def _():
            carry.wait(src, dst, sems)

    res = pl.pallas_call(
        wrapped, name=name, grid=grid, in_specs=list(in_specs) + [ANY] * n_c, out_specs=list(out_specs) + [ANY] * n_c,
        out_shape=list(out_shape) + carry.out_shape, scratch_shapes=list(scratch_shapes) + carry.scratch,
        input_output_aliases=aliases, compiler_params=cp)(*args, *carry.arrays)
    return tuple(res[:n_out]) + (list(res[n_out:]),)


def fwd_in(x, g, sc, sh, w, bias, *, blocked, tm, name, carry=None, wt=False):
    t = x.shape[0]
    nb, bw = (w.shape[0], w.shape[1]) if wt else (w.shape[0], w.shape[2])

    def body(*refs):
        if bias is None:
            x_ref, g_ref, sc_ref, sh_ref, w_ref, h_ref, p_ref = refs
        else:
            x_ref, g_ref, sc_ref, sh_ref, w_ref, b_ref, h_ref, p_ref = refs
        hb = _adaln(x_ref[...], g_ref[...], sc_ref[...], sh_ref[...]).astype(BF)
        h_ref[...] = hb
        for d in range(nb):
            y = _dot_nt(hb, w_ref[d]) if wt else _dot(hb, w_ref[d])
            if bias is not None:
                y = y + b_ref[d]
            if blocked:
                p_ref[d] = y.astype(BF)
            else:
                p_ref[:, d * bw:(d + 1) * bw] = y.astype(BF)

    vec = _const((1, D))
    in_specs = [_rows(tm, D), vec, vec, vec, _const(w.shape)]
    args = [x, g, sc, sh, w]
    if bias is not None:
        in_specs.append(_const((nb, 1, bw)))
        args.append(bias)
    if blocked:
        p_spec, p_shape = _brows(nb, tm, bw), jax.ShapeDtypeStruct((nb, t, bw), BF)
    else:
        p_spec, p_shape = _rows(tm, nb * bw), jax.ShapeDtypeStruct((t, nb * bw), BF)
    return _call(body, name=name, grid=(t // tm,), in_specs=in_specs, out_specs=[_rows(tm, D), p_spec],
                 out_shape=[jax.ShapeDtypeStruct((t, D), BF), p_shape], args=args, carry=carry)


def _sc_conv(p_ref, ph_ref, cw_ref, first, tm):
    z = p_ref[:, D:2 * D].astype(F32) * p_ref[:, 2 * D:3 * D].astype(F32)
    zp = jnp.where(first, 0.0, ph_ref[8:16, D:2 * D].astype(F32) * ph_ref[8:16, 2 * D:3 * D].astype(F32))
    ext = jnp.concatenate([zp, z], axis=0)
    return cw_ref[0:1, :] * ext[6:6 + tm] + cw_ref[1:2, :] * ext[7:7 + tm] + cw_ref[2:3, :] * z


def sc_fwd_out(p, convw, w_out, x, gn, gt, *, tm, name, carry=None):
    t = x.shape[0]

    def body(p_ref, ph_ref, cw_ref, w_ref, x_ref, gn_ref, gt_ref, x1_ref, m_ref, q_ref):
        u = _sc_conv(p_ref, ph_ref, cw_ref, pl.program_id(0) == 0, tm)
        qb = (p_ref[:, 0:D].astype(F32) * u).astype(BF)
        q_ref[...] = qb
        m = _dot(qb, w_ref[...])
        m_ref[...] = m.astype(BF)
        x1_ref[...] = _gated_res(x_ref[...], m, gn_ref[...], gt_ref[...])

    vec = _const((1, D))
    return _call(
        body, name=name, grid=(t // tm,),
        in_specs=[_rows(tm, 3 * D), _prev(16, 3 * D, tm), _const((3, D)), _const((D, D)), _rows(tm, D), vec, vec],
        out_specs=[_rows(tm, D)] * 3,
        out_shape=[jax.ShapeDtypeStruct((t, D), F32), jax.ShapeDtypeStruct((t, D), BF), jax.ShapeDtypeStruct((t, D), BF)],
        args=[p, p, convw, w_out, x, gn, gt], carry=carry)


def _layernorm_parts(u2):
    mu = jnp.mean(u2, axis=-1, keepdims=True)
    cen = u2 - mu
    rstd = lax.rsqrt(jnp.mean(cen * cen, axis=-1, keepdims=True) + LN_EPS)
    return cen * rstd, rstd


def cf_fwd_out(a, w_dw, b_dw, ln_g, ln_b, w_pw2, b_pw2, x, gn, gt, *, tm, name):
    t = x.shape[0]
    hb = 32

    def body(a_ref, ah_ref, wd_ref, bd_ref, lg_ref, lb_ref, w_ref, b2_ref, x_ref, gn_ref, gt_ref,
             x1_ref, m_ref, s_out_ref, u2_ref, s_ref):
        i = pl.program_id(0)
        uh = ah_ref[:, 0:D].astype(F32) * _sigmoid(ah_ref[:, D:2 * D].astype(F32))
        s_ref[0, 0:hb, :] = jnp.where(i == 0, 0.0, uh)
        s_ref[0, hb:hb + tm, :] = a_ref[:, 0:D].astype(F32) * _sigmoid(a_ref[:, D:2 * D].astype(F32))
        _row_shifted_copies(s_ref, tm + hb - 8)
        acc = bd_ref[...] + wd_ref[0:1, :] * _shifted(s_ref, hb - CFW + 1, tm)
        for k in range(1, CFW):
            acc = acc + wd_ref[k:k + 1, :] * _shifted(s_ref, hb - CFW + 1 + k, tm)
        u2_ref[...] = acc.astype(BF)
        xh, _ = _layernorm_parts(acc)
        l = xh * lg_ref[...] + lb_ref[...]
        sb = (l * _sigmoid(l)).astype(BF)
        s_out_ref[...] = sb
        m = _dot(sb, w_ref[...]) + b2_ref[...]
        m_ref[...] = m.astype(BF)
        x1_ref[...] = _gated_res(x_ref[...], m, gn_ref[...], gt_ref[...])

    vec = _const((1, D))
    return pl.pallas_call(
        body, name=name, grid=(t // tm,),
        in_specs=[_rows(tm, 2 * D), _prev(hb, 2 * D, tm), _const((CFW, D)), vec, vec, vec, _const((D, D)), vec,
                  _rows(tm, D), vec, vec],
        out_specs=[_rows(tm, D)] * 4,
        out_shape=[jax.ShapeDtypeStruct((t, D), F32)] + [jax.ShapeDtypeStruct((t, D), BF)] * 3,
        scratch_shapes=[pltpu.VMEM((8, tm + hb, D), F32)], compiler_params=_params(1),
    )(a, a, w_dw, b_dw, ln_g, ln_b, w_pw2, b_pw2, x, gn, gt)


def _pool_counts(i, tm, w):
    row = lax.broadcasted_iota(jnp.int32, (tm, 1), 0) + i * tm
    return jnp.minimum(row + 1, w).astype(F32)


def pool_fwd(x, g, sc, sh, pw, pb, pscale, gn, gt, *, tm, name):
    t = x.shape[0]
    pad, hb = 8, 16
    base = pad + hb

    def body(x_ref, xh_ref, g_ref, sc_ref, sh_ref, pw_ref, pb_ref, ps_ref, gn_ref, gt_ref,
             x1_ref, m_ref, yp_ref, po_ref, sa_ref, sb_ref):
        i = pl.program_id(0)
        hh = _adaln(xh_ref[...], g_ref[...], sc_ref[...], sh_ref[...])
        h = _adaln(x_ref[...], g_ref[...], sc_ref[...], sh_ref[...])
        zero = jnp.zeros((pad, D), F32)
        sa_ref[0:pad, :] = zero
        sb_ref[0:pad, :] = zero
        sa_ref[pad:base, :] = jnp.where(i == 0, 0.0, hh)
        sa_ref[base:base + tm, :] = h
        n = hb + tm
        src, dst = sa_ref, sb_ref
        ys = []
        for gi, w in enumerate(POOL_WINDOWS):
            c0 = gi * PG
            step = w // 2
            dst[pl.ds(pad, n), c0:D] = src[pl.ds(pad, n), c0:D] + src[pl.ds(pad - step, n), c0:D]
            mean = dst[pl.ds(base, tm), c0:c0 + PG] / _pool_counts(i, tm, w)
            pooled = (mean - h[:, c0:c0 + PG]).astype(BF)
            po_ref[:, c0:c0 + PG] = pooled
            ys.append(_dot(pooled, pw_ref[gi]))
            src, dst = dst, src
        ypre = jnp.concatenate(ys, axis=1) + pb_ref[...]
        yp_ref[...] = ypre.astype(BF)
        m = ypre * ps_ref[...]
        m_ref[...] = m.astype(BF)
        x1_ref[...] = _gated_res(x_ref[...], m, gn_ref[...], gt_ref[...])

    vec = _const((1, D))
    return pl.pallas_call(
        body, name=name, grid=(t // tm,),
        in_specs=[_rows(tm, D), _prev(hb, D, tm), vec, vec, vec, _const((4, PG, PG)), vec, vec, vec, vec],
        out_specs=[_rows(tm, D)] * 4,
        out_shape=[jax.ShapeDtypeStruct((t, D), F32)] + [jax.ShapeDtypeStruct((t, D), BF)] * 3,
        scratch_shapes=[pltpu.VMEM((tm + base, D), F32)] * 2, compiler_params=_params(1),
    )(x, x, g, sc, sh, pw, pb, pscale, gn, gt)


def ffn_fwd_out(up, convw, convb, w_down, x, gn, gt, *, tm, name, carry=None, target=None):
    t = x.shape[0]
    nt = t // tm

    def body(gate_ref, gh_ref, val_ref, cw_ref, cb_ref, w_ref, x_ref, gn_ref, gt_ref, *rest):
        if target is None:
            x2_ref, f_ref, gc_ref, a_ref = rest
        else:
            tg_ref, x2_ref, f_ref, gc_ref, a_ref, l_ref, lacc_ref = rest
        i = pl.program_id(0)
        acc = jnp.zeros((tm, D), F32)
        for j in range(4):
            gate = gate_ref[j].astype(F32)
            ext = jnp.concatenate([jnp.where(i == 0, 0.0, gh_ref[j, 8:16, :].astype(F32)), gate], axis=0)
            gc = cb_ref[j] + cw_ref[j, 0:1, :] * ext[6:6 + tm] + cw_ref[j, 1:2, :] * ext[7:7 + tm] + cw_ref[j, 2:3, :] * gate
            gc_ref[j] = gc.astype(BF)
            ab = (gc * _sigmoid(gc) * val_ref[j].astype(F32)).astype(BF)
            a_ref[j] = ab
            acc = acc + _dot(ab, w_ref[j * FB:(j + 1) * FB, :])
        f_ref[...] = acc.astype(BF)
        x2 = _gated_res(x_ref[...], acc, gn_ref[...], gt_ref[...])
        if target is None:
            x2_ref[...] = x2
        else:
            @pl.when(i == 0)
            def _():
                lacc_ref[...] = jnp.zeros_like(lacc_ref)

            e = x2 - tg_ref[...]
            x2_ref[...] = e * (1.0 / D)
            lacc_ref[...] += _rsum(e * e)

            @pl.when(i == nt - 1)
            def _():
                l_ref[...] = jnp.sum(lacc_ref[...], axis=1, keepdims=True) * (0.5 / D)

    vec = _const((1, D))
    blk = jax.ShapeDtypeStruct((4, t, FB), BF)
    in_specs = [_brows(4, tm, FB, 0), _bprev(4, 16, FB, tm, 0), _brows(4, tm, FB, 1), _const((4, 3, FB)),
                _const((4, 1, FB)), _const((F, D)), _rows(tm, D), vec, vec]
    out_specs = [_rows(tm, D), _rows(tm, D), _brows(4, tm, FB), _brows(4, tm, FB)]
    out_shape = [jax.ShapeDtypeStruct((t, D), F32), jax.ShapeDtypeStruct((t, D), BF), blk, blk]
    args = [up, up, up, convw, convb, w_down, x, gn, gt]
    scratch = []
    if target is not None:
        in_specs.append(_rows(tm, D))
        args.append(target)
        out_specs.append(pl.BlockSpec((1, 1), lambda i: (0, 0)))
        out_shape.append(jax.ShapeDtypeStruct((1, 1), F32))
        scratch.append(pltpu.VMEM((1, D), F32))
    return _call(body, name=name, grid=(nt,), in_specs=in_specs, out_specs=out_specs, out_shape=out_shape, args=args,
                 scratch_shapes=scratch, carry=carry)


def _init_stats(ref):
    @pl.when(pl.program_id(0) == 0)
    def _():
        ref[...] = jnp.zeros_like(ref)


def bwd_out(dxo, m, gn, gt, w, *, tm, name, carry=None):
    t = dxo.shape[0]
    k = w.shape[0]

    def body(dx_ref, m_ref, gn_ref, gt_ref, w_ref, dm_ref, da_ref, st_ref):
        _init_stats(st_ref)
        dm, dgt, dgn = _gated_res_bwd(dx_ref[...], m_ref[...].astype(F32), gn_ref[...], gt_ref[...])
        st_ref[0:1, :] += dgt
        st_ref[1:2, :] += dgn
        st_ref[2:3, :] += _rsum(dm)
        dmb = dm.astype(BF)
        dm_ref[...] = dmb
        da_ref[...] = _dot_nt(dmb, w_ref[...]).astype(BF)

    vec = _const((1, D))
    da_spec, da_shape = _rows(tm, k), jax.ShapeDtypeStruct((t, k), BF)
    return _call(
        body, name=name, grid=(t // tm,), in_specs=[_rows(tm, D), _rows(tm, D), vec, vec, _const((k, D))],
        out_specs=[_rows(tm, D), da_spec, _const((8, D), single=False)],
        out_shape=[jax.ShapeDtypeStruct((t, D), BF), da_shape, jax.ShapeDtypeStruct((8, D), F32)],
        args=[dxo, m, gn, gt, w], carry=carry)


def bwd_in(dps, w, x, g, sc, dxo, *, tm, name, wt=False):
    t = x.shape[0]
    direct = w is None
    if not direct:
        nb, bw = (w.shape[0], w.shape[1]) if wt else (w.shape[0], w.shape[2])
    natural = (not direct) and dps[0].ndim == 2

    def body(*refs):
        n = len(dps)
        dp_refs = refs[:n]
        if direct:
            x_ref, g_ref, sc_ref, dxo_ref, dx_ref, st_ref = refs[n:]
            dh = dp_refs[0][...]
        else:
            w_ref, x_ref, g_ref, sc_ref, dxo_ref, dx_ref, st_ref = refs[n:]
            dh = jnp.zeros((tm, D), F32)
            if natural:
                for d in range(nb):
                    dh = dh + _dot_nt(dp_refs[0][:, d * bw:(d + 1) * bw], w_ref[d])
            else:
                d = 0
                for r in dp_refs:
                    for j in range(r.shape[0]):
                        dh = dh + (_dot(r[j], w_ref[d]) if wt else _dot_nt(r[j], w_ref[d]))
                        d += 1
        _init_stats(st_ref)
        x = x_ref[...]
        r = lax.rsqrt(jnp.mean(x * x, axis=-1, keepdims=True) + RMS_EPS)
        xh = x * r
        gv = g_ref[...]
        st_ref[0:1, :] += _rsum(dh)
        st_ref[1:2, :] += _rsum(dh * (xh * gv))
        dn = dh * (1.0 + sc_ref[...])
        st_ref[2:3, :] += _rsum(dn * xh)
        dy = dn * gv
        dx_ref[...] = dxo_ref[...] + r * (dy - xh * jnp.mean(dy * xh, axis=-1, keepdims=True))

    vec = _const((1, D))
    if direct:
        dp_specs = [_rows(tm, D)]
    elif natural:
        dp_specs = [_rows(tm, nb * bw)]
    else:
        dp_specs = [_brows(a.shape[0], tm, bw) for a in dps]
    w_specs, w_args = ([], []) if direct else ([_const(w.shape)], [w])
    return pl.pallas_call(
        body, name=name, grid=(t // tm,),
        in_specs=dp_specs + w_specs + [_rows(tm, D), vec, vec, _rows(tm, D)],
        out_specs=[_rows(tm, D), _const((8, D), single=False)],
        out_shape=[jax.ShapeDtypeStruct((t, D), F32), jax.ShapeDtypeStruct((8, D), F32)],
        compiler_params=_params(1))(*dps, *w_args, x, g, sc, dxo)


def sc_bwd_out(dxo, m, gn, gt, w_out, p, convw, *, tm, name, carry=None):
    t = dxo.shape[0]

    def body(dx_ref, m_ref, gn_ref, gt_ref, w_ref, p_ref, ph_ref, cw_ref, dm_ref, dbg_ref, du_ref, st_ref):
        _init_stats(st_ref)
        dm, dgt, dgn = _gated_res_bwd(dx_ref[...], m_ref[...].astype(F32), gn_ref[...], gt_ref[...])
        st_ref[0:1, :] += dgt
        st_ref[1:2, :] += dgn
        dmb = dm.astype(BF)
        dm_ref[...] = dmb
        dq = _dot_nt(dmb, w_ref[...])
        dbg_ref[...] = (dq * _sc_conv(p_ref, ph_ref, cw_ref, pl.program_id(0) == 0, tm)).astype(BF)
        du_ref[...] = (dq * p_ref[:, 0:D].astype(F32)).astype(BF)

    vec = _const((1, D))
    out = jax.ShapeDtypeStruct((t, D), BF)
    return _call(
        body, name=name, grid=(t // tm,),
        in_specs=[_rows(tm, D), _rows(tm, D), vec, vec, _const((D, D)), _rows(tm, 3 * D), _prev(16, 3 * D, tm), _const((3, D))],
        out_specs=[_rows(tm, D)] * 3 + [_const((8, D), single=False)], out_shape=[out, out, out, jax.ShapeDtypeStruct((8, D), F32)],
        args=[dxo, m, gn, gt, w_out, p, p, convw], carry=carry)


def sc_bwd_in(du, dbg, p, convw, w, x, g, sc, dxo, *, tm, name, carry=None):
    t = x.shape[0]

    def body(du_ref, dun_ref, dbg_ref, p_ref, cw_ref, w_ref, x_ref, g_ref, sc_ref, dxo_ref, dx_ref, dp_ref, st_ref, sc2_ref):
        last = pl.program_id(0) == pl.num_programs(0) - 1
        _init_stats(st_ref)
        _init_stats(sc2_ref)
        du = du_ref[...].astype(F32)
        ext = jnp.concatenate([du, jnp.where(last, 0.0, dun_ref[0:8, :].astype(F32))], axis=0)
        e1, e2 = ext[1:tm + 1], ext[2:tm + 2]
        dz = cw_ref[2:3, :] * du + cw_ref[1:2, :] * e1 + cw_ref[0:1, :] * e2
        cg, hi = p_ref[:, D:2 * D].astype(F32), p_ref[:, 2 * D:3 * D].astype(F32)
        dbg, dcg, dhi = dbg_ref[...], (dz * hi).astype(BF), (dz * cg).astype(BF)
        dp_ref[:, 0:D] = dbg
        dp_ref[:, D:2 * D] = dcg
        dp_ref[:, 2 * D:3 * D] = dhi
        dh = _dot_nt(dbg, w_ref[0]) + _dot_nt(dcg, w_ref[1]) + _dot_nt(dhi, w_ref[2])
        z = cg * hi
        sc2_ref[0:1, :] += _rsum(z * e2)
        sc2_ref[1:2, :] += _rsum(z * e1)
        sc2_ref[2:3, :] += _rsum(z * du)
        x = x_ref[...]
        r = lax.rsqrt(jnp.mean(x * x, axis=-1, keepdims=True) + RMS_EPS)
        xh = x * r
        gv = g_ref[...]
        st_ref[0:1, :] += _rsum(dh)
        st_ref[1:2, :] += _rsum(dh * (xh * gv))
        dn = dh * (1.0 + sc_ref[...])
        st_ref[2:3, :] += _rsum(dn * xh)
        dy = dn * gv
        dx_ref[...] = dxo_ref[...] + r * (dy - xh * jnp.mean(dy * xh, axis=-1, keepdims=True))

    vec = _const((1, D))
    stat = jax.ShapeDtypeStruct((8, D), F32)
    return _call(
        body, name=name, grid=(t // tm,),
        in_specs=[_rows(tm, D), _next(16, D, tm, t), _rows(tm, D), _rows(tm, 3 * D), _const((3, D)), _const(w.shape),
                  _rows(tm, D), vec, vec, _rows(tm, D)],
        out_specs=[_rows(tm, D), _rows(tm, 3 * D), _const((8, D), single=False), _const((8, D), single=False)],
        out_shape=[jax.ShapeDtypeStruct((t, D), F32), jax.ShapeDtypeStruct((t, 3 * D), BF), stat, stat],
        args=[du, du, dbg, p, convw, w, x, g, sc, dxo], carry=carry)


def cf_bwd_mid(ds, u2, a, w_dw, ln_g, ln_b, *, tm, name, carry=None):
    t = ds.shape[0]
    hb = 32

    def du2_of(dsv, u2v, lg, lb):
        xh, rstd = _layernorm_parts(u2v)
        l = xh * lg + lb
        sg = _sigmoid(l)
        dl = dsv * (sg * (1.0 + l * (1.0 - sg)))
        dxh = dl * lg
        du2 = rstd * (dxh - jnp.mean(dxh, axis=-1, keepdims=True) - xh * jnp.mean(dxh * xh, axis=-1, keepdims=True))
        return du2, dl, xh

    rc, cc = 32, 256

    def body(ds_ref, dsn_ref, u2_ref, u2n_ref, a_ref, wd_ref, lg_ref, lb_ref, da_ref, st_ref, s1_ref, acc_ref):
        i = pl.program_id(0)
        last = i == pl.num_programs(0) - 1
        _init_stats(st_ref)
        _init_stats(acc_ref)
        lg, lb = lg_ref[...], lb_ref[...]
        du2, dl, xh = du2_of(ds_ref[...].astype(F32), u2_ref[...].astype(F32), lg, lb)
        st_ref[32:33, :] += _rsum(dl * xh)
        st_ref[33:34, :] += _rsum(dl)
        st_ref[31:32, :] += _rsum(du2)
        du2n, _, _ = du2_of(dsn_ref[...].astype(F32), u2n_ref[...].astype(F32), lg, lb)
        s1_ref[0, 0:tm, :] = du2
        s1_ref[0, tm:tm + hb, :] = jnp.where(last, 0.0, du2n)
        _row_shifted_copies(s1_ref, tm + hb - 8)

        def taps(r0, _):
            rows = pl.ds(r0, rc)
            for c0 in range(0, D, cc):
                sg = _sigmoid(a_ref[rows, D + c0:D + c0 + cc].astype(F32))
                u = a_ref[rows, c0:c0 + cc].astype(F32) * sg
                du = jnp.zeros((rc, cc), F32)
                for k in range(CFW):
                    o = CFW - 1 - k
                    sh = s1_ref[o % 8, pl.ds(pl.multiple_of(r0 + 8 * (o // 8), 8), rc), c0:c0 + cc]
                    du = du + wd_ref[k:k + 1, c0:c0 + cc] * sh
                    acc_ref[8 * k:8 * k + 8, c0:c0 + cc] += _fold8(u * sh)
                dav = du * sg
                dgv = du * u * (1.0 - sg)
                acc_ref[8 * CFW:8 * CFW + 8, c0:c0 + cc] += _fold8(dav)
                acc_ref[8 * CFW + 8:8 * CFW + 16, c0:c0 + cc] += _fold8(dgv)
                da_ref[rows, c0:c0 + cc] = dav.astype(BF)
                da_ref[rows, D + c0:D + c0 + cc] = dgv.astype(BF)
            return 0

        _chunks(tm, rc, taps)

        @pl.when(last)
        def _():
            for k in range(CFW):
                st_ref[k:k + 1, :] = jnp.sum(acc_ref[8 * k:8 * k + 8, :], axis=0, keepdims=True)
            st_ref[34:35, :] = jnp.sum(acc_ref[8 * CFW:8 * CFW + 8, :], axis=0, keepdims=True)
            st_ref[35:36, :] = jnp.sum(acc_ref[8 * CFW + 8:8 * CFW + 16, :], axis=0, keepdims=True)

    vec = _const((1, D))
    return _call(
        body, name=name, grid=(t // tm,),
        in_specs=[_rows(tm, D), _next(hb, D, tm, t), _rows(tm, D), _next(hb, D, tm, t), _rows(tm, 2 * D),
                  _const((CFW, D)), vec, vec],
        out_specs=[_rows(tm, 2 * D), _const((40, D), single=False)],
        out_shape=[jax.ShapeDtypeStruct((t, 2 * D), BF), jax.ShapeDtypeStruct((40, D), F32)],
        scratch_shapes=[pltpu.VMEM((8, tm + hb, D), F32), pltpu.VMEM((8 * (CFW + 2), D), F32)],
        args=[ds, ds, u2, u2, a, w_dw, ln_g, ln_b], carry=carry)


def pool_bwd(dxo, m, ypre, pw, pscale, gn, gt, *, tm, name):
    t = dxo.shape[0]
    hb = 16

    def dyp_of(dxv, mv, ypv, ps, gnv, gtv):
        dm, dgt, dgn = _gated_res_bwd(dxv, mv, gnv, gtv)
        return dm * ps, dgt, dgn, _rsum(dm * ypv)

    def body(dx_ref, dxn_ref, m_ref, mn_ref, yp_ref, ypn_ref, pw_ref, ps_ref, gn_ref, gt_ref,
             dh_ref, dyp_ref, st_ref, sa_ref, sb_ref):
        i = pl.program_id(0)
        last = i == pl.num_programs(0) - 1
        _init_stats(st_ref)
        ps, gnv, gtv = ps_ref[...], gn_ref[...], gt_ref[...]
        dyp, dgt, dgn, dps = dyp_of(dx_ref[...], m_ref[...].astype(F32), yp_ref[...].astype(F32), ps, gnv, gtv)
        st_ref[0:1, :] += dgt
        st_ref[1:2, :] += dgn
        st_ref[2:3, :] += dps
        st_ref[3:4, :] += _rsum(dyp)
        dypb = dyp.astype(BF)
        dyp_ref[...] = dypb
        dypn, _, _, _ = dyp_of(dxn_ref[...], mn_ref[...].astype(F32), ypn_ref[...].astype(F32), ps, gnv, gtv)
        dypnb = jnp.where(last, 0.0, dypn).astype(BF)
        dpo = []
        for gi, w in enumerate(POOL_WINDOWS):
            c0 = gi * PG
            dp_main = _dot_nt(dypb[:, c0:c0 + PG], pw_ref[gi])
            dp_next = _dot_nt(dypnb[:, c0:c0 + PG], pw_ref[gi])
            dpo.append(dp_main)
            sa_ref[0:tm, c0:c0 + PG] = dp_main / _pool_counts(i, tm, w)
            sa_ref[tm:tm + hb, c0:c0 + PG] = dp_next / float(w)
        zero = jnp.zeros((8, D), F32)
        sa_ref[tm + hb:tm + hb + 8, :] = zero
        sb_ref[tm + hb:tm + hb + 8, :] = zero
        n = tm + hb
        src, dst = sa_ref, sb_ref
        for gi, w in enumerate(POOL_WINDOWS):
            c0 = gi * PG
            step = w // 2
            dst[pl.ds(0, n), c0:D] = src[pl.ds(0, n), c0:D] + src[pl.ds(step, n), c0:D]
            dh_ref[:, c0:c0 + PG] = dst[pl.ds(0, tm), c0:c0 + PG] - dpo[gi]
            src, dst = dst, src

    vec = _const((1, D))
    return pl.pallas_call(
        body, name=name, grid=(t // tm,),
        in_specs=[_rows(tm, D), _next(hb, D, tm, t), _rows(tm, D), _next(hb, D, tm, t), _rows(tm, D),
                  _next(hb, D, tm, t), _const((4, PG, PG)), vec, vec, vec],
        out_specs=[_rows(tm, D), _rows(tm, D), _const((8, D), single=False)],
        out_shape=[jax.ShapeDtypeStruct((t, D), F32), jax.ShapeDtypeStruct((t, D), BF), jax.ShapeDtypeStruct((8, D), F32)],
        scratch_shapes=[pltpu.VMEM((tm + hb + 8, D), F32)] * 2, compiler_params=_params(1),
    )(dxo, dxo, m, m, ypre, ypre, pw, pscale, gn, gt)


def ffn_bwd_out(dxo, f, gn, gt, w_down, gc, up, *, tm, name, carry=None):
    t = dxo.shape[0]

    def body(dx_ref, f_ref, gn_ref, gt_ref, w_ref, gc_ref, val_ref, df_ref, dgc_ref, dval_ref, st_ref, sc_ref):
        _init_stats(st_ref)
        _init_stats(sc_ref)
        dm, dgt, dgn = _gated_res_bwd(dx_ref[...], f_ref[...].astype(F32), gn_ref[...], gt_ref[...])
        st_ref[0:1, :] += dgt
        st_ref[1:2, :] += dgn
        dmb = dm.astype(BF)
        df_ref[...] = dmb
        for j in range(4):
            da = _dot_nt(dmb, w_ref[j * FB:(j + 1) * FB, :])
            gcv = gc_ref[j].astype(F32)
            sg = _sigmoid(gcv)
            dval_ref[j] = (da * (gcv * sg)).astype(BF)
            dgc = da * val_ref[j].astype(F32) * (sg * (1.0 + gcv * (1.0 - sg)))
            dgc_ref[j] = dgc.astype(BF)
            sc_ref[j, 0:1, :] += _rsum(dgc)

    vec = _const((1, D))
    return _call(
        body, name=name, grid=(t // tm,),
        in_specs=[_rows(tm, D), _rows(tm, D), vec, vec, _const((F, D)), _brows(4, tm, FB), _brows(4, tm, FB, 1)],
        out_specs=[_rows(tm, D), _brows(4, tm, FB), _brows(4, tm, FB, 1), _const((8, D), single=False),
                   _const((4, 8, FB), single=False)],
        out_shape=[jax.ShapeDtypeStruct((t, D), BF), jax.ShapeDtypeStruct((4, t, FB), BF), jax.ShapeDtypeStruct((8, t, FB), BF),
                   jax.ShapeDtypeStruct((8, D), F32), jax.ShapeDtypeStruct((4, 8, FB), F32)],
        args=[dxo, f, gn, gt, w_down, gc, up], carry=carry)


def ffn_bwd_in(dgc, dup, up, convw, w, x, g, sc, dxo, *, tm, name, carry=None):
    t = x.shape[0]

    def body(dgc_ref, dgcn_ref, dval_ref, gate_ref, cw_ref, w_ref, x_ref, g_ref, sc_ref, dxo_ref,
             dx_ref, dgate_ref, st_ref, sc2_ref):
        last = pl.program_id(0) == pl.num_programs(0) - 1
        _init_stats(st_ref)
        _init_stats(sc2_ref)
        dh = jnp.zeros((tm, D), F32)
        for j in range(4):
            dgc = dgc_ref[j].astype(F32)
            ext = jnp.concatenate([dgc, jnp.where(last, 0.0, dgcn_ref[j, 0:8, :].astype(F32))], axis=0)
            e1, e2 = ext[1:tm + 1], ext[2:tm + 2]
            dgate = (cw_ref[j, 2:3, :] * dgc + cw_ref[j, 1:2, :] * e1 + cw_ref[j, 0:1, :] * e2).astype(BF)
            dgate_ref[j] = dgate
            dh = dh + _dot(dgate, w_ref[j])
            gate = gate_ref[j].astype(F32)
            sc2_ref[j, 1:2, :] += _rsum(gate * e2)
            sc2_ref[j, 2:3, :] += _rsum(gate * e1)
            sc2_ref[j, 3:4, :] += _rsum(gate * dgc)
        for j in range(4):
            dh = dh + _dot(dval_ref[j], w_ref[4 + j])
        x = x_ref[...]
        r = lax.rsqrt(jnp.mean(x * x, axis=-1, keepdims=True) + RMS_EPS)
        xh = x * r
        gv = g_ref[...]
        st_ref[0:1, :] += _rsum(dh)
        st_ref[1:2, :] += _rsum(dh * (xh * gv))
        dn = dh * (1.0 + sc_ref[...])
        st_ref[2:3, :] += _rsum(dn * xh)
        dy = dn * gv
        dx_ref[...] = dxo_ref[...] + r * (dy - xh * jnp.mean(dy * xh, axis=-1, keepdims=True))

    vec = _const((1, D))
    return _call(
        body, name=name, grid=(t // tm,),
        in_specs=[_brows(4, tm, FB), _bnext(4, 16, FB, tm, t), _brows(4, tm, FB, 1), _brows(4, tm, FB, 0), _const((4, 3, FB)),
                  _const(w.shape), _rows(tm, D), vec, vec, _rows(tm, D)],
        out_specs=[_rows(tm, D), _brows(4, tm, FB, 0), _const((8, D), single=False), _const((4, 8, FB), single=False)],
        out_shape=[jax.ShapeDtypeStruct((t, D), F32), jax.ShapeDtypeStruct((8, t, FB), BF), jax.ShapeDtypeStruct((8, D), F32),
                   jax.ShapeDtypeStruct((4, 8, FB), F32)],
        args=[dgc, dgc, dup, up, convw, w, x, g, sc, dxo], aliases={2: 1}, carry=carry)


def wgrad(a, b, *, nblk, a_blocked, b_blocked, bk, bn, tt, name, carry=None):
    t = a.shape[1] if a.ndim == 3 else a.shape[0]
    nt = t // tt

    def body(a_ref, b_ref, o_ref, acc_ref):
        s = pl.program_id(1)

        @pl.when(s == 0)
        def _():
            acc_ref[...] = jnp.zeros_like(acc_ref)

        av = a_ref[0] if a.ndim == 3 else a_ref[...]
        bv = b_ref[0] if b.ndim == 3 else b_ref[...]
        acc_ref[...] += _dot_tn(av, bv)

        @pl.when(s == nt - 1)
        def _():
            o_ref[0] = acc_ref[...].astype(BF)

    def spec(arr, blocked, width):
        if arr.ndim == 3:
            return pl.BlockSpec((1, tt, width), lambda j, s: (j, s, 0))
        if blocked:
            return pl.BlockSpec((tt, width), lambda j, s: (s, j))
        return pl.BlockSpec((tt, width), lambda j, s: (s, 0))

    return _call(
        body, name=name, grid=(nblk, nt), in_specs=[spec(a, a_blocked, bk), spec(b, b_blocked, bn)],
        out_specs=[pl.BlockSpec((1, bk, bn), lambda j, s: (j, 0, 0))],
        out_shape=[jax.ShapeDtypeStruct((nblk, bk, bn), BF)],
        scratch_shapes=[pltpu.VMEM((bk, bn), F32)], args=[a, b], carry=carry)


def mod_partial(c_all, w_mod):
    cols = w_mod.shape[2]

    def body(c_ref, w_ref, o_ref):
        c = c_ref[...]
        ca = c * _sigmoid(c)
        o_ref[0] = jnp.dot(ca, w_ref[0], preferred_element_type=F32, precision=lax.Precision.HIGHEST)

    return pl.pallas_call(
        body, name="mod_partial", grid=(DEPTH,),
        in_specs=[pl.BlockSpec((NDEV, D), lambda l: (0, 0)), pl.BlockSpec((1, D, cols), lambda l: (l, 0, 0))],
        out_specs=pl.BlockSpec((1, NDEV, cols), lambda l: (l, 0, 0)),
        out_shape=jax.ShapeDtypeStruct((DEPTH, NDEV, cols), F32), compiler_params=_params(1))(c_all, w_mod)


def mod_finish(parts, b_mod):
    cols = parts.shape[2]

    def body(p_ref, b_ref, o_ref):
        for e in range(NDEV):
            o_ref[:, e * cols:(e + 1) * cols] = p_ref[e] + b_ref[:, e * cols:(e + 1) * cols]

    return pl.pallas_call(
        body, name="mod_finish", out_shape=jax.ShapeDtypeStruct((DEPTH, NDEV * cols), F32))(parts, b_mod)


def sum_parts(parts):
    n, r, c = parts.shape

    def body(p_ref, o_ref):
        acc = p_ref[0]
        for j in range(1, n):
            acc = acc + p_ref[j]
        o_ref[...] = acc

    return pl.pallas_call(body, name="sum_parts", out_shape=jax.ShapeDtypeStruct((r, c), F32))(parts)


def mod_wgrad(c_all_t, gmod_cols):
    cols = gmod_cols.shape[2]

    def body(c_ref, g_ref, o_ref):
        c = c_ref[...]
        ca = c * _sigmoid(c)
        acc = ca[:, 0:1] * g_ref[0, 0:1, :]
        for b in range(1, NDEV):
            acc = acc + ca[:, b:b + 1] * g_ref[0, b:b + 1, :]
        o_ref[0] = acc

    return pl.pallas_call(
        body, name="mod_wgrad", grid=(DEPTH,),
        in_specs=[pl.BlockSpec((D, NDEV), lambda l: (0, 0)), pl.BlockSpec((1, NDEV, cols), lambda l: (l, 0, 0))],
        out_specs=pl.BlockSpec((1, D, cols), lambda l: (l, 0, 0)),
        out_shape=jax.ShapeDtypeStruct((DEPTH, D, cols), F32), compiler_params=_params(1))(c_all_t, gmod_cols)


def _adamw_math(g, w, m, v):
    m2 = B1 * m + (1.0 - B1) * g
    v2 = B2 * v + (1.0 - B2) * (g * g)
    m_hat = m2 / (1.0 - B1 ** STEP)
    v_hat = v2 / (1.0 - B2 ** STEP)
    delta = -LR * (m_hat / (jnp.sqrt(v_hat) + ADAM_EPS) + WD * w)
    return delta, m2, v2


def _row_tile(r, c, budget=1 << 18):
    if r * c <= budget or r % 8:
        return r
    best = 8
    for cand in range(8, r + 1, 8):
        if r % cand == 0 and cand * c <= budget:
            best = cand
    return best


def adamw_sum(parts, w, m, v, *, name):
    n, r, c = parts.shape
    tr = _row_tile(r, c)

    def body(p_ref, w_ref, m_ref, v_ref, g_ref, d_ref, m2_ref, v2_ref):
        g = p_ref[0].astype(F32)
        for j in range(1, n):
            g = g + p_ref[j].astype(F32)
        d, m2, v2 = _adamw_math(g, w_ref[...], m_ref[...], v_ref[...])
        g_ref[...] = g
        d_ref[...] = d
        m2_ref[...] = m2
        v2_ref[...] = v2

    blk = pl.BlockSpec((tr, c), lambda i: (i, 0))
    out = jax.ShapeDtypeStruct((r, c), F32)
    return pl.pallas_call(
        body, name=name, grid=(r // tr,), in_specs=[pl.BlockSpec((n, tr, c), lambda i: (0, i, 0)), blk, blk, blk],
        out_specs=[blk] * 4, out_shape=[out] * 4, compiler_params=_params(1))(parts, w, m, v)


def adamw_layer(parts, w, m, v, prev, layer, *, name):
    n, r, c = parts.shape
    nl = w.shape[0]
    tr = _row_tile(r, c)

    def body(p_ref, w_ref, m_ref, v_ref, *rest):
        g_ref, d_ref, m2_ref, v2_ref = rest[-4:]
        g = p_ref[0].astype(F32)
        for j in range(1, n):
            g = g + p_ref[j].astype(F32)
        d, m2, v2 = _adamw_math(g, w_ref[0], m_ref[0], v_ref[0])
        g_ref[0] = g
        d_ref[0] = d
        m2_ref[0] = m2
        v2_ref[0] = v2

    blk = pl.BlockSpec((1, tr, c), lambda i: (layer, i, 0))
    in_specs = [pl.BlockSpec((n, tr, c), lambda i: (0, i, 0)), blk, blk, blk]
    args = [parts, w, m, v]
    aliases = {}
    if prev is not None:
        in_specs += [ANY] * 4
        args += list(prev)
        aliases = {4 + k: k for k in range(4)}
    out = jax.ShapeDtypeStruct((nl, r, c), F32)
    return pl.pallas_call(
        body, name=name, grid=(r // tr,), in_specs=in_specs, out_specs=[blk] * 4, out_shape=[out] * 4,
        input_output_aliases=aliases, compiler_params=_params(1))(*args)


def _pack(arrays):
    flat, layout, off = [], [], 0
    for a in arrays:
        flat.append(a.reshape(-1))
        layout.append((off, a.shape))
        off += a.size
    pad = (-off) % 1024
    if pad:
        flat.append(jnp.zeros((pad,), F32))
    return jnp.concatenate(flat).reshape(-1, 128), layout


def _unpack(packed, layout, lead=()):
    flat = packed.reshape(lead + (-1,))
    return [flat[..., off:off + _size(shape)].reshape(lead + tuple(shape)) for off, shape in layout]


def _size(shape):
    n = 1
    for s in shape:
        n *= s
    return n


def _join_last(g):
    g = jnp.moveaxis(g, 0, -2)
    return g.reshape(g.shape[:-2] + (g.shape[-2] * g.shape[-1],))


def _my_cols(a, width):
    return lax.dynamic_slice_in_dim(a, _my_id() * width, width, axis=a.ndim - 1)


def _tile(t, pref):
    return min(pref, t)


def kernel(x, c, w_mod, b_mod, norm_g, sc_w_in, sc_conv, sc_w_out, pool_w, pool_b, pool_scale, cf_w_pw1, cf_b_pw1, cf_w_dw, cf_b_dw, cf_ln_g, cf_ln_b, cf_w_pw2, cf_b_pw2, ffn_w_up, ffn_conv, ffn_b_conv, ffn_w_down, loss_target, m_w_mod, m_b_mod, m_norm_g, m_sc_w_in, m_sc_conv, m_sc_w_out, m_pool_w, m_pool_b, m_pool_scale, m_cf_w_pw1, m_cf_b_pw1, m_cf_w_dw, m_cf_b_dw, m_cf_ln_g, m_cf_ln_b, m_cf_w_pw2, m_cf_b_pw2, m_ffn_w_up, m_ffn_conv, m_ffn_b_conv, m_ffn_w_down, v_w_mod, v_b_mod, v_norm_g, v_sc_w_in, v_sc_conv, v_sc_w_out, v_pool_w, v_pool_b, v_pool_scale, v_cf_w_pw1, v_cf_b_pw1, v_cf_w_dw, v_cf_b_dw, v_cf_ln_g, v_cf_ln_b, v_cf_w_pw2, v_cf_b_pw2, v_ffn_w_up, v_ffn_conv, v_ffn_b_conv, v_ffn_w_down):
    env = dict(locals())
    names = ["w_mod", "b_mod", "norm_g", "sc_w_in", "sc_conv", "sc_w_out", "pool_w", "pool_b", "pool_scale", "cf_w_pw1",
             "cf_b_pw1", "cf_w_dw", "cf_b_dw", "cf_ln_g", "cf_ln_b", "cf_w_pw2", "cf_b_pw2", "ffn_w_up", "ffn_conv",
             "ffn_b_conv", "ffn_w_down"]
    t = x.shape[1]
    tm = _tile(t, 512)
    tt = _tile(t, 4096)
    x0, target = x[0], loss_target[0]

    small_names = ["norm_g", "sc_conv", "cf_b_pw1", "cf_w_dw", "cf_b_dw", "cf_ln_g", "cf_ln_b", "cf_b_pw2", "ffn_conv"]
    packed, layout = _pack([c] + [env[n] for n in small_names])

    shard = {"pool": pool_w[0].astype(BF), "pw1": cf_w_pw1[0].astype(BF), "pw2": cf_w_pw2[0].astype(BF)}
    for j in range(2):
        shard[f"in{j}"], shard[f"out{j}"] = sc_w_in[j].astype(BF), sc_w_out[j].astype(BF)
    for l in range(DEPTH):
        shard[f"up{l}"], shard[f"down{l}"] = ffn_w_up[l].T.astype(BF), ffn_w_down[l].astype(BF)
    gathered, g_in0, g_out0 = _Exchange("gather", [packed, shard["in0"], shard["out0"]]).run("gather_first")
    wg = {"in0": g_in0, "out0": g_out0}
    parts = _unpack(gathered, layout, lead=(NDEV,))
    c_all = parts[0].reshape(NDEV, D)
    full = {n: _join_last(p) for n, p in zip(small_names, parts[1:])}
    fwd_plan = {("mix_in", 0): ["up0"], ("mix_out", 0): ["down0"], ("ffn_in", 0): ["pool", "up1"],
                ("ffn_out", 0): ["down1", "pw1", "pw2"], ("ffn_in", 1): ["up2"], ("ffn_out", 1): ["down2", "in1", "out1"],
                ("ffn_in", 2): ["up3"], ("ffn_out", 2): ["down3"]}

    def carrying(plan, kind, store, source, fn, key, *a, **k):
        names = plan.get(key)
        if not names:
            return fn(*a, **k)
        res = fn(*a, carry=_Exchange(kind, [source[n] for n in names]), **k)
        store.update(zip(names, res[-1]))
        return res[:-1]

    fwd = functools.partial(carrying, fwd_plan, "gather", wg, shard)

    mp = mod_partial(c_all, w_mod)
    (mod_parts,) = _Exchange("scatter", [jnp.swapaxes(mp, 0, 1)]).run("exchange_mod")
    mod = mod_finish(mod_parts, b_mod)

    def vec(a):
        return a.reshape(1, -1)

    def col_blocks(g):
        w = jnp.swapaxes(g, 0, 1).reshape(D, -1)
        return jnp.swapaxes(w.reshape(D, -1, D), 0, 1)

    def ffn_blocks(a):
        return jnp.swapaxes(a.reshape(a.shape[0], 4, FB), 0, 1)

    saved = []
    xs = x0
    for l in range(DEPTH):
        sh1, sc1, g1, sh2, sc2, g2 = [mod[l:l + 1, k * D:(k + 1) * D] for k in range(6)]
        ng = [full["norm_g"][l, k:k + 1] for k in range(4)]
        kind, j = l % 3, l // 3
        s = dict(x_in=xs, sc1=sc1, g1=g1, sc2=sc2, g2=g2, ng=ng)
        if kind == 0:
            s["w_in"] = col_blocks(wg[f"in{j}"])
            s["h"], s["p"] = fwd(fwd_in, ("mix_in", l), xs, ng[0], sc1, sh1, s["w_in"], None, blocked=False, tm=tm,
                                 name=f"sc_in_{l}")
            x1, s["m"], s["q"] = fwd(sc_fwd_out, ("mix_out", l), s["p"], full["sc_conv"][j], wg[f"out{j}"].reshape(D, D), xs,
                                     ng[1], g1, tm=tm, name=f"sc_out_{l}")
        elif kind == 1:
            pool_w_f = jnp.swapaxes(wg["pool"], 0, 1).reshape(4, PG, PG)
            x1, s["m"], s["ypre"], s["pooled"] = pool_fwd(xs, ng[0], sc1, sh1, pool_w_f, pool_b, pool_scale, ng[1], g1,
                                                          tm=tm, name=f"pool_{l}")
        else:
            s["w_in"] = col_blocks(wg["pw1"])
            s["h"], s["a"] = fwd_in(xs, ng[0], sc1, sh1, s["w_in"], full["cf_b_pw1"].reshape(2, 1, D), blocked=False, tm=tm,
                                    name=f"cf_in_{l}")
            x1, s["m"], s["s"], s["u2"] = cf_fwd_out(s["a"], full["cf_w_dw"][0], full["cf_b_dw"], full["cf_ln_g"],
                                                     full["cf_ln_b"], wg["pw2"].reshape(D, D), full["cf_b_pw2"], xs, ng[1],
                                                     g1, tm=tm, name=f"cf_out_{l}")
        s["x1"] = x1
        s["cw"] = ffn_blocks(full["ffn_conv"][l])
        s["h2"], s["up"] = fwd(fwd_in, ("ffn_in", l), x1, ng[2], sc2, sh2, wg[f"up{l}"], None, blocked=True, wt=True, tm=tm,
                               name=f"ffn_in_{l}")
        xs, s["f"], s["gc"], s["fa"], *loss_part = fwd(
            ffn_fwd_out, ("ffn_out", l), s["up"], s["cw"], ffn_blocks(ffn_b_conv[l:l + 1]), wg[f"down{l}"].reshape(F, D), x1,
            ng[3], g2, tm=tm, name=f"ffn_out_{l}", target=target if l == DEPTH - 1 else None)
        saved.append(s)

    dx = xs
    loss = lax.psum(loss_part[0][0, 0], ("x", "y", "c"))

    gmod = [None] * DEPTH
    d_norm_g = [None] * DEPTH
    d_ffn_conv = [None] * DEPTH
    d_ffn_b_conv = [None] * DEPTH
    d_sc_conv = [None] * 2
    big = {}
    got = {}
    small_g = {}
    bwd_plan = {("mix_bout", 3): ["down3"], ("mix_bin", 3): ["up3"], ("ffn_bout", 2): ["in1", "out1"],
                ("mix_bmid", 2): ["up2", "down2"], ("ffn_bout", 1): ["pw1", "pw2"], ("ffn_bout", 0): ["pool", "down1"],
                ("ffn_bin", 0): ["up1"], ("mix_bout", 0): ["down0"], ("mix_win", 0): ["up0"]}
    bwd = functools.partial(carrying, bwd_plan, "scatter", got, big)
    pool_w_f = jnp.swapaxes(wg["pool"], 0, 1).reshape(4, PG, PG)
    for l in reversed(range(DEPTH)):
        s = saved[l]
        ng = s["ng"]
        kind, j = l % 3, l // 3
        df, dgc, dup, st_o, st_b = bwd(ffn_bwd_out, ("ffn_bout", l), dx, s["f"], ng[3], s["g2"], wg[f"down{l}"].reshape(F, D),
                                       s["gc"], s["up"], tm=tm, name=f"ffn_bout_{l}")
        dx1, dup, st_i, st_c = bwd(ffn_bwd_in, ("ffn_bin", l), dgc, dup, s["up"], s["cw"], wg[f"up{l}"], s["x1"], ng[2],
                                   s["sc2"], dx, tm=tm, name=f"ffn_bin_{l}")
        (big[f"up{l}"],) = wgrad(dup, s["h2"], nblk=NDEV, a_blocked=True, b_blocked=False, bk=FB, bn=D, tt=tt,
                                 name=f"ffn_wup_{l}")
        big[f"down{l}"] = wgrad(s["fa"], df, nblk=4, a_blocked=True, b_blocked=False, bk=FB, bn=D, tt=tt,
                                name=f"ffn_wdown_{l}")[0].reshape(NDEV, F // NDEV, D)
        d_ffn_b_conv[l] = st_b[:, 0, :].reshape(F)
        d_ffn_conv[l] = jnp.swapaxes(st_c[:, 1:4, :], 0, 1).reshape(3, F)
        g_ffn = [st_i[0], st_i[1], st_o[0]]
        dn3, dn2 = st_o[1], st_i[2]
        if kind == 0:
            dm, dbg, du, st_o = bwd(sc_bwd_out, ("mix_bout", l), dx1, s["m"], ng[1], s["g1"], wg[f"out{j}"].reshape(D, D), s["p"],
                                    full["sc_conv"][j], tm=tm, name=f"sc_bout_{l}")
            dx, dp, st_i, st_c = bwd(sc_bwd_in, ("mix_bin", l), du, dbg, s["p"], full["sc_conv"][j], s["w_in"], s["x_in"], ng[0],
                                     s["sc1"], dx1, tm=tm, name=f"sc_bin_{l}")
            (big[f"in{j}"],) = bwd(wgrad, ("mix_win", l), s["h"], dp, nblk=NDEV, a_blocked=False, b_blocked=True, bk=D,
                                   bn=3 * D // NDEV, tt=tt, name=f"sc_win_{l}")
            big[f"out{j}"] = wgrad(s["q"], dm, nblk=1, a_blocked=False, b_blocked=False, bk=D, bn=D, tt=tt,
                                   name=f"sc_wout_{l}")[0].reshape(NDEV, D // NDEV, D)
            d_sc_conv[j] = st_c[0:3]
        elif kind == 1:
            dh, dyp, st_o = pool_bwd(dx1, s["m"], s["ypre"], pool_w_f, pool_scale, ng[1], s["g1"], tm=tm, name=f"pool_b_{l}")
            dx, st_i = bwd_in([dh], None, s["x_in"], ng[0], s["sc1"], dx1, tm=tm, name=f"pool_bin_{l}")
            (dpw,) = wgrad(s["pooled"], dyp, nblk=4, a_blocked=True, b_blocked=True, bk=PG, bn=PG, tt=tt, name=f"pool_w_{l}")
            big["pool"] = jnp.swapaxes(dpw.reshape(4, NDEV, PG // NDEV, PG), 0, 1).reshape(NDEV, 4 * PG // NDEV, PG)
            small_g["pool_scale"], small_g["pool_b"] = st_o[2:3], st_o[3:4]
        else:
            dm, ds, st_o = bwd_out(dx1, s["m"], ng[1], s["g1"], wg["pw2"].reshape(D, D), tm=tm,
                                   name=f"cf_bout_{l}")
            dA, st_c = bwd(cf_bwd_mid, ("mix_bmid", l), ds, s["u2"], s["a"], full["cf_w_dw"][0], full["cf_ln_g"],
                           full["cf_ln_b"], tm=tm, name=f"cf_bmid_{l}")
            dx, st_i = bwd_in([dA], s["w_in"], s["x_in"], ng[0], s["sc1"], dx1, tm=tm, name=f"cf_bin_{l}")
            (big["pw1"],) = wgrad(s["h"], dA, nblk=NDEV, a_blocked=False, b_blocked=True, bk=D, bn=2 * D // NDEV, tt=tt,
                                  name=f"cf_wpw1_{l}")
            big["pw2"] = wgrad(s["s"], dm, nblk=1, a_blocked=False, b_blocked=False, bk=D, bn=D, tt=tt,
                               name=f"cf_wpw2_{l}")[0].reshape(NDEV, D // NDEV, D)
            small_g["cf_w_dw"] = st_c[0:CFW][None]
            small_g["cf_b_dw"], small_g["cf_ln_g"], small_g["cf_ln_b"] = st_c[31:32], st_c[32:33], st_c[33:34]
            small_g["cf_b_pw1"] = st_c[34:36].reshape(1, 2 * D)
            small_g["cf_b_pw2"] = st_o[2:3]
        gmod[l] = jnp.concatenate([st_i[0], st_i[1], st_o[0]] + g_ffn)
        d_norm_g[l] = jnp.stack([st_i[2], st_o[1], dn2, dn3])

    small_g["gmod"] = jnp.stack(gmod)
    small_g["norm_g"] = jnp.stack(d_norm_g)
    small_g["sc_conv"] = jnp.stack(d_sc_conv)
    small_g["ffn_conv"] = jnp.stack(d_ffn_conv)
    small_g["ffn_b_conv"] = jnp.stack(d_ffn_b_conv)
    sg_names = ["gmod", "norm_g", "sc_conv", "pool_b", "pool_scale", "cf_b_pw1", "cf_w_dw", "cf_b_dw", "cf_ln_g", "cf_ln_b",
                "cf_b_pw2", "ffn_conv", "ffn_b_conv"]
    gpacked, glayout = _pack([small_g[n] for n in sg_names])
    (ggath,), (got["in0"], got["out0"]) = _run_exchanges(
        [_Exchange("gather", [gpacked]), _Exchange("scatter", [big["in0"], big["out0"]])], "exchange_last")
    gsum = dict(zip(sg_names, _unpack(sum_parts(ggath), glayout)))
    gmod_all = _unpack(ggath, glayout[:1], lead=(NDEV,))[0]
    grads = {"b_mod": gsum["gmod"], "pool_b": gsum["pool_b"], "pool_scale": gsum["pool_scale"],
             "ffn_b_conv": gsum["ffn_b_conv"]}
    for n in ["norm_g", "sc_conv", "cf_b_pw1", "cf_w_dw", "cf_b_dw", "cf_ln_g", "cf_ln_b", "cf_b_pw2", "ffn_conv"]:
        grads[n] = _my_cols(gsum[n], env[n].shape[-1])
    grads["w_mod"] = mod_wgrad(c_all.T, jnp.swapaxes(_my_cols(gmod_all, w_mod.shape[2]), 0, 1))

    deltas, new_m, new_v = {}, {}, {}
    sp_names = ["b_mod", "norm_g", "sc_conv", "pool_b", "pool_scale", "cf_b_pw1", "cf_w_dw", "cf_b_dw", "cf_ln_g", "cf_ln_b",
                "cf_b_pw2", "ffn_conv", "ffn_b_conv"]
    pg, playout = _pack([grads[n] for n in sp_names])
    pw_, _ = _pack([env[n] for n in sp_names])
    pm_, _ = _pack([env["m_" + n] for n in sp_names])
    pv_, _ = _pack([env["v_" + n] for n in sp_names])
    _, sd, sm, sv = adamw_sum(pg[None], pw_, pm_, pv_, name="adamw_small")
    for n, d_, m_, v_ in zip(sp_names, _unpack(sd, playout), _unpack(sm, playout), _unpack(sv, playout)):
        deltas[n], new_m[n], new_v[n] = d_, m_, v_
    gw = grads["w_mod"].reshape(1, DEPTH * D, -1)
    _, d_, m_, v_ = adamw_sum(gw, w_mod.reshape(gw.shape[1:]), m_w_mod.reshape(gw.shape[1:]), v_w_mod.reshape(gw.shape[1:]),
                              name="adamw_w_mod")
    deltas["w_mod"], new_m["w_mod"], new_v["w_mod"] = [a.reshape(w_mod.shape) for a in (d_, m_, v_)]

    groups = {"sc_w_in": ["in0", "in1"], "sc_w_out": ["out0", "out1"], "pool_w": ["pool"], "cf_w_pw1": ["pw1"],
              "cf_w_pw2": ["pw2"], "ffn_w_up": [f"up{l}" for l in range(DEPTH)], "ffn_w_down": [f"down{l}" for l in range(DEPTH)]}
    for n, layers in groups.items():
        stacked = (len(layers),) + got[layers[0]].shape[1:]
        flip = n == "ffn_w_up"
        w3 = [(jnp.swapaxes(env[p + n], 1, 2) if flip else env[p + n]).reshape(stacked) for p in ("", "m_", "v_")]
        outs = None
        for li, key in enumerate(layers):
            outs = adamw_layer(got[key], *w3, outs, li, name=f"adamw_{n}_{li}")
        grads[n], deltas[n], new_m[n], new_v[n] = [(jnp.swapaxes(a, 1, 2) if flip else a).reshape(env[n].shape) for a in outs]

    return (loss, dx[None], *[grads[n] for n in names], *[deltas[n] for n in names], *[new_m[n] for n in names],
            *[new_v[n] for n in names])
```

```python
import functools

import jax
import jax.numpy as jnp
from jax import lax
from jax.experimental import pallas as pl
from jax.experimental.pallas import tpu as pltpu

D = 1024
F = 2816
NDEV = 8
FB = F // 4
DEPTH = 4
RMS_EPS = 1e-6
LN_EPS = 1e-5
CFW = 31
POOL_WINDOWS = (2, 4, 8, 16)
PG = D // 4
LR, B1, B2, ADAM_EPS, WD, STEP = 0.001, 0.9, 0.999, 1e-08, 0.01, 10

BF = jnp.bfloat16
F32 = jnp.float32
VMEM_LIMIT_V7X = 56 * 1024 * 1024
MESH = pl.DeviceIdType.MESH
ANY = pl.BlockSpec(memory_space=pl.ANY)


def _params(n_axes):
    return pltpu.CompilerParams(dimension_semantics=("arbitrary",) * n_axes, vmem_limit_bytes=VMEM_LIMIT_V7X)


def _const(shape, single=True):
    nd = len(shape)
    if single:
        return pl.BlockSpec(shape, lambda *_: (0,) * nd, pipeline_mode=pl.Buffered(1))
    return pl.BlockSpec(shape, lambda *_: (0,) * nd)


def _rows(tm, c):
    return pl.BlockSpec((tm, c), lambda i: (i, 0))


def _brows(nb, tm, c, b0=0):
    return pl.BlockSpec((nb, tm, c), lambda i: (b0, i, 0))


def _prev(hb, c, tm):
    return pl.BlockSpec((hb, c), lambda i: (jnp.maximum(i * (tm // hb) - 1, 0), 0))


def _next(hb, c, tm, t):
    return pl.BlockSpec((hb, c), lambda i: (jnp.minimum((i + 1) * (tm // hb), t // hb - 1), 0))


def _bprev(nb, hb, c, tm, b0=0):
    return pl.BlockSpec((nb, hb, c), lambda i: (b0, jnp.maximum(i * (tm // hb) - 1, 0), 0))


def _bnext(nb, hb, c, tm, t, b0=0):
    return pl.BlockSpec((nb, hb, c), lambda i: (b0, jnp.minimum((i + 1) * (tm // hb), t // hb - 1), 0))


def _sigmoid(v):
    return 0.5 * jnp.tanh(0.5 * v) + 0.5


def _fold8(v):
    r, c = v.shape
    return jnp.sum(v.reshape(r // 8, 8, c), axis=0)


def _chunks(n_rows, rc, step, init=0, reverse=False):
    n = n_rows // rc

    def it(c, carry):
        idx = (n - 1 - c) if reverse else c
        return step(pl.multiple_of(idx * rc, rc), carry)

    return lax.fori_loop(0, n, it, init)


def _row_shifted_copies(s_ref, n):
    for b in range(1, 8):
        s_ref[b, 0:n, :] = s_ref[0, pl.ds(b, n), :]


def _shifted(s_ref, o, tm):
    return s_ref[o % 8, pl.ds(8 * (o // 8), tm), :]


def _dot(a, b):
    return jnp.dot(a, b, preferred_element_type=F32)


def _dot_nt(a, b):
    return lax.dot_general(a, b, (((1,), (1,)), ((), ())), preferred_element_type=F32)


def _dot_tn(a, b):
    return lax.dot_general(a, b, (((0,), (0,)), ((), ())), preferred_element_type=F32)


def _rsum(v):
    return jnp.sum(v, axis=0, keepdims=True)


def _adaln(x, g, sc, sh):
    r = lax.rsqrt(jnp.mean(x * x, axis=-1, keepdims=True) + RMS_EPS)
    return (x * r * g) * (1.0 + sc) + sh


def _gated_res(x, m, gn, gt):
    r = lax.rsqrt(jnp.mean(m * m, axis=-1, keepdims=True) + RMS_EPS)
    return x + gt * (m * r * gn)


def _gated_res_bwd(dxo, m, gn, gt):
    r = lax.rsqrt(jnp.mean(m * m, axis=-1, keepdims=True) + RMS_EPS)
    mh = m * r
    dgt = _rsum(dxo * (mh * gn))
    dn = dxo * gt
    dgn = _rsum(dn * mh)
    dmh = dn * gn
    dm = r * (dmh - mh * jnp.mean(dmh * mh, axis=-1, keepdims=True))
    return dm, dgt, dgn


def _my_id():
    return 4 * lax.axis_index("x") + 2 * lax.axis_index("y") + lax.axis_index("c")


def _peer(k):
    x, y, c = lax.axis_index("x"), lax.axis_index("y"), lax.axis_index("c")
    px = 1 - x if k & 4 else x
    py = 1 - y if k & 2 else y
    pc = 1 - c if k & 1 else c
    return (px, py, pc), 4 * px + 2 * py + pc


class _Exchange:
    def __init__(self, kind, arrays):
        self.gather = kind == "gather"
        self.arrays = list(arrays)
        n = len(self.arrays)
        if self.gather:
            self.out_shape = [jax.ShapeDtypeStruct((NDEV,) + a.shape, a.dtype) for a in self.arrays]
        else:
            self.out_shape = [jax.ShapeDtypeStruct(a.shape, a.dtype) for a in self.arrays]
        self.scratch = [pltpu.SemaphoreType.DMA((n * NDEV,)), pltpu.SemaphoreType.DMA((n * NDEV,)),
                        pltpu.SemaphoreType.DMA((n,))]

    def _local(self, a, src, dst, sems):
        me = _my_id()
        return pltpu.make_async_copy(src[a] if self.gather else src[a].at[me], dst[a].at[me], sems[2].at[a])

    def _remote(self, a, k, src, dst, sems, incoming):
        to, pid = _peer(k)
        me = _my_id()
        return pltpu.make_async_remote_copy(
            src_ref=src[a] if self.gather else src[a].at[pid], dst_ref=dst[a].at[pid if incoming else me],
            send_sem=sems[0].at[a * NDEV + k], recv_sem=sems[1].at[a * NDEV + k], device_id=to, device_id_type=MESH)

    def start(self, src, dst, sems):
        for a in range(len(self.arrays)):
            self._local(a, src, dst, sems).start()
        for k in range(1, NDEV):
            for a in range(len(self.arrays)):
                self._remote(a, k, src, dst, sems, False).start()

    def wait(self, src, dst, sems):
        for k in range(1, NDEV):
            for a in range(len(self.arrays)):
                self._remote(a, k, src, dst, sems, True).wait_recv()
        for k in range(1, NDEV):
            for a in range(len(self.arrays)):
                self._remote(a, k, src, dst, sems, False).wait_send()
        for a in range(len(self.arrays)):
            self._local(a, src, dst, sems).wait()

    def run(self, name):
        n = len(self.arrays)

        def body(*refs):
            src, dst, sems = refs[:n], refs[n:2 * n], refs[2 * n:]
            self.start(src, dst, sems)
            self.wait(src, dst, sems)

        return pl.pallas_call(body, name=name, in_specs=[ANY] * n, out_specs=[ANY] * n, out_shape=self.out_shape,
                              scratch_shapes=self.scratch)(*self.arrays)


def _run_exchanges(exchanges, name):
    counts = [len(e.arrays) for e in exchanges]
    n = sum(counts)

    def body(*refs):
        src, dst, sems = refs[:n], refs[n:2 * n], refs[2 * n:]
        parts, lo = [], 0
        for ei, (e, c) in enumerate(zip(exchanges, counts)):
            parts.append((e, src[lo:lo + c], dst[lo:lo + c], sems[3 * ei:3 * ei + 3]))
            lo += c
        for e, s, d, m in parts:
            e.start(s, d, m)
        for e, s, d, m in parts:
            e.wait(s, d, m)

    res = pl.pallas_call(
        body, name=name, in_specs=[ANY] * n, out_specs=[ANY] * n, out_shape=[s for e in exchanges for s in e.out_shape],
        scratch_shapes=[s for e in exchanges for s in e.scratch])(*[a for e in exchanges for a in e.arrays])
    out, lo = [], 0
    for c in counts:
        out.append(list(res[lo:lo + c]))
        lo += c
    return out


def _call(body, *, name, grid, in_specs, out_specs, out_shape, args, scratch_shapes=(), carry=None, aliases=None):
    cp = _params(len(grid))
    aliases = aliases or {}
    if carry is None:
        return tuple(pl.pallas_call(body, name=name, grid=grid, in_specs=in_specs, out_specs=out_specs, out_shape=out_shape,
                                    scratch_shapes=list(scratch_shapes), input_output_aliases=aliases,
                                    compiler_params=cp)(*args))
    n_in, n_out, n_sc, n_c = len(in_specs), len(out_specs), len(scratch_shapes), len(carry.arrays)

    def wrapped(*refs):
        ins, src = refs[:n_in], refs[n_in:n_in + n_c]
        outs = refs[n_in + n_c:n_in + n_c + n_out]
        dst = refs[n_in + n_c + n_out:n_in + 2 * n_c + n_out]
        rest = refs[n_in + 2 * n_c + n_out:]
        scr, sems = rest[:n_sc], rest[n_sc:]
        first = pl.program_id(0) == 0
        last = pl.program_id(0) == grid[0] - 1
        for ax in range(1, len(grid)):
            first = jnp.logical_and(first, pl.program_id(ax) == 0)
            last = jnp.logical_and(last, pl.program_id(ax) == grid[ax] - 1)

        @pl.when(first)
        def _():
            carry.start(src, dst, sems)

        body(*ins, *outs, *scr)

        @pl.when(last)
        def _():
            carry.wait(src, dst, sems)

    res = pl.pallas_call(
        wrapped, name=name, grid=grid, in_specs=list(in_specs) + [ANY] * n_c, out_specs=list(out_specs) + [ANY] * n_c,
        out_shape=list(out_shape) + carry.out_shape, scratch_shapes=list(scratch_shapes) + carry.scratch,
        input_output_aliases=aliases, compiler_params=cp)(*args, *carry.arrays)
    return tuple(res[:n_out]) + (list(res[n_out:]),)


def fwd_in(x, g, sc, sh, w, bias, *, blocked, tm, name, carry=None, wt=False):
    t = x.shape[0]
    nb, bw = (w.shape[0], w.shape[1]) if wt else (w.shape[0], w.shape[2])

    def body(*refs):
        if bias is None:
            x_ref, g_ref, sc_ref, sh_ref, w_ref, h_ref, p_ref = refs
        else:
            x_ref, g_ref, sc_ref, sh_ref, w_ref, b_ref, h_ref, p_ref = refs
        hb = _adaln(x_ref[...], g_ref[...], sc_ref[...], sh_ref[...]).astype(BF)
        h_ref[...] = hb
        for d in range(nb):
            y = _dot_nt(hb, w_ref[d]) if wt else _dot(hb, w_ref[d])
            if bias is not None:
                y = y + b_ref[d]
            if blocked:
                p_ref[d] = y.astype(BF)
            else:
                p_ref[:, d * bw:(d + 1) * bw] = y.astype(BF)

    vec = _const((1, D))
    in_specs = [_rows(tm, D), vec, vec, vec, _const(w.shape)]
    args = [x, g, sc, sh, w]
    if bias is not None:
        in_specs.append(_const((nb, 1, bw)))
        args.append(bias)
    if blocked:
        p_spec, p_shape = _brows(nb, tm, bw), jax.ShapeDtypeStruct((nb, t, bw), BF)
    else:
        p_spec, p_shape = _rows(tm, nb * bw), jax.ShapeDtypeStruct((t, nb * bw), BF)
    return _call(body, name=name, grid=(t // tm,), in_specs=in_specs, out_specs=[_rows(tm, D), p_spec],
                 out_shape=[jax.ShapeDtypeStruct((t, D), BF), p_shape], args=args, carry=carry)


def _sc_conv(p_ref, ph_ref, cw_ref, first, tm):
    z = p_ref[:, D:2 * D].astype(F32) * p_ref[:, 2 * D:3 * D].astype(F32)
    zp = jnp.where(first, 0.0, ph_ref[8:16, D:2 * D].astype(F32) * ph_ref[8:16, 2 * D:3 * D].astype(F32))
    ext = jnp.concatenate([zp, z], axis=0)
    return cw_ref[0:1, :] * ext[6:6 + tm] + cw_ref[1:2, :] * ext[7:7 + tm] + cw_ref[2:3, :] * z


def sc_fwd_out(p, convw, w_out, x, gn, gt, *, tm, name, carry=None):
    t = x.shape[0]

    def body(p_ref, ph_ref, cw_ref, w_ref, x_ref, gn_ref, gt_ref, x1_ref, m_ref, q_ref):
        u = _sc_conv(p_ref, ph_ref, cw_ref, pl.program_id(0) == 0, tm)
        qb = (p_ref[:, 0:D].astype(F32) * u).astype(BF)
        q_ref[...] = qb
        m = _dot(qb, w_ref[...])
        m_ref[...] = m.astype(BF)
        x1_ref[...] = _gated_res(x_ref[...], m, gn_ref[...], gt_ref[...])

    vec = _const((1, D))
    return _call(
        body, name=name, grid=(t // tm,),
        in_specs=[_rows(tm, 3 * D), _prev(16, 3 * D, tm), _const((3, D)), _const((D, D)), _rows(tm, D), vec, vec],
        out_specs=[_rows(tm, D)] * 3,
        out_shape=[jax.ShapeDtypeStruct((t, D), F32), jax.ShapeDtypeStruct((t, D), BF), jax.ShapeDtypeStruct((t, D), BF)],
        args=[p, p, convw, w_out, x, gn, gt], carry=carry)


def _layernorm_parts(u2):
    mu = jnp.mean(u2, axis=-1, keepdims=True)
    cen = u2 - mu
    rstd = lax.rsqrt(jnp.mean(cen * cen, axis=-1, keepdims=True) + LN_EPS)
    return cen * rstd, rstd


def cf_fwd_out(a, w_dw, b_dw, ln_g, ln_b, w_pw2, b_pw2, x, gn, gt, *, tm, name):
    t = x.shape[0]
    hb = 32

    def body(a_ref, ah_ref, wd_ref, bd_ref, lg_ref, lb_ref, w_ref, b2_ref, x_ref, gn_ref, gt_ref,
             x1_ref, m_ref, s_out_ref, u2_ref, s_ref):
        i = pl.program_id(0)
        uh = ah_ref[:, 0:D].astype(F32) * _sigmoid(ah_ref[:, D:2 * D].astype(F32))
        s_ref[0, 0:hb, :] = jnp.where(i == 0, 0.0, uh)
        s_ref[0, hb:hb + tm, :] = a_ref[:, 0:D].astype(F32) * _sigmoid(a_ref[:, D:2 * D].astype(F32))
        _row_shifted_copies(s_ref, tm + hb - 8)
        acc = bd_ref[...] + wd_ref[0:1, :] * _shifted(s_ref, hb - CFW + 1, tm)
        for k in range(1, CFW):
            acc = acc + wd_ref[k:k + 1, :] * _shifted(s_ref, hb - CFW + 1 + k, tm)
        u2_ref[...] = acc.astype(BF)
        xh, _ = _layernorm_parts(acc)
        l = xh * lg_ref[...] + lb_ref[...]
        sb = (l * _sigmoid(l)).astype(BF)
        s_out_ref[...] = sb
        m = _dot(sb, w_ref[...]) + b2_ref[...]
        m_ref[...] = m.astype(BF)
        x1_ref[...] = _gated_res(x_ref[...], m, gn_ref[...], gt_ref[...])

    vec = _const((1, D))
    return pl.pallas_call(
        body, name=name, grid=(t // tm,),
        in_specs=[_rows(tm, 2 * D), _prev(hb, 2 * D, tm), _const((CFW, D)), vec, vec, vec, _const((D, D)), vec,
                  _rows(tm, D), vec, vec],
        out_specs=[_rows(tm, D)] * 4,
        out_shape=[jax.ShapeDtypeStruct((t, D), F32)] + [jax.ShapeDtypeStruct((t, D), BF)] * 3,
        scratch_shapes=[pltpu.VMEM((8, tm + hb, D), F32)], compiler_params=_params(1),
    )(a, a, w_dw, b_dw, ln_g, ln_b, w_pw2, b_pw2, x, gn, gt)


def _pool_counts(i, tm, w):
    row = lax.broadcasted_iota(jnp.int32, (tm, 1), 0) + i * tm
    return jnp.minimum(row + 1, w).astype(F32)


def pool_fwd(x, g, sc, sh, pw, pb, pscale, gn, gt, *, tm, name):
    t = x.shape[0]
    pad, hb = 8, 16
    base = pad + hb

    def body(x_ref, xh_ref, g_ref, sc_ref, sh_ref, pw_ref, pb_ref, ps_ref, gn_ref, gt_ref,
             x1_ref, m_ref, yp_ref, po_ref, sa_ref, sb_ref):
        i = pl.program_id(0)
        hh = _adaln(xh_ref[...], g_ref[...], sc_ref[...], sh_ref[...])
        h = _adaln(x_ref[...], g_ref[...], sc_ref[...], sh_ref[...])
        zero = jnp.zeros((pad, D), F32)
        sa_ref[0:pad, :] = zero
        sb_ref[0:pad, :] = zero
        sa_ref[pad:base, :] = jnp.where(i == 0, 0.0, hh)
        sa_ref[base:base + tm, :] = h
        n = hb + tm
        src, dst = sa_ref, sb_ref
        ys = []
        for gi, w in enumerate(POOL_WINDOWS):
            c0 = gi * PG
            step = w // 2
            dst[pl.ds(pad, n), c0:D] = src[pl.ds(pad, n), c0:D] + src[pl.ds(pad - step, n), c0:D]
            mean = dst[pl.ds(base, tm), c0:c0 + PG] / _pool_counts(i, tm, w)
            pooled = (mean - h[:, c0:c0 + PG]).astype(BF)
            po_ref[:, c0:c0 + PG] = pooled
            ys.append(_dot(pooled, pw_ref[gi]))
            src, dst = dst, src
        ypre = jnp.concatenate(ys, axis=1) + pb_ref[...]
        yp_ref[...] = ypre.astype(BF)
        m = ypre * ps_ref[...]
        m_ref[...] = m.astype(BF)
        x1_ref[...] = _gated_res(x_ref[...], m, gn_ref[...], gt_ref[...])

    vec = _const((1, D))
    return pl.pallas_call(
        body, name=name, grid=(t // tm,),
        in_specs=[_rows(tm, D), _prev(hb, D, tm), vec, vec, vec, _const((4, PG, PG)), vec, vec, vec, vec],
        out_specs=[_rows(tm, D)] * 4,
        out_shape=[jax.ShapeDtypeStruct((t, D), F32)] + [jax.ShapeDtypeStruct((t, D), BF)] * 3,
        scratch_shapes=[pltpu.VMEM((tm + base, D), F32)] * 2, compiler_params=_params(1),
    )(x, x, g, sc, sh, pw, pb, pscale, gn, gt)


def ffn_fwd_out(up, convw, convb, w_down, x, gn, gt, *, tm, name, carry=None, target=None):
    t = x.shape[0]
    nt = t // tm

    def body(gate_ref, gh_ref, val_ref, cw_ref, cb_ref, w_ref, x_ref, gn_ref, gt_ref, *rest):
        if target is None:
            x2_ref, f_ref, gc_ref, a_ref = rest
        else:
            tg_ref, x2_ref, f_ref, gc_ref, a_ref, l_ref, lacc_ref = rest
        i = pl.program_id(0)
        acc = jnp.zeros((tm, D), F32)
        for j in range(4):
            gate = gate_ref[j].astype(F32)
            ext = jnp.concatenate([jnp.where(i == 0, 0.0, gh_ref[j, 8:16, :].astype(F32)), gate], axis=0)
            gc = cb_ref[j] + cw_ref[j, 0:1, :] * ext[6:6 + tm] + cw_ref[j, 1:2, :] * ext[7:7 + tm] + cw_ref[j, 2:3, :] * gate
            gc_ref[j] = gc.astype(BF)
            ab = (gc * _sigmoid(gc) * val_ref[j].astype(F32)).astype(BF)
            a_ref[j] = ab
            acc = acc + _dot(ab, w_ref[j * FB:(j + 1) * FB, :])
        f_ref[...] = acc.astype(BF)
        x2 = _gated_res(x_ref[...], acc, gn_ref[...], gt_ref[...])
        if target is None:
            x2_ref[...] = x2
        else:
            @pl.when(i == 0)
            def _():
                lacc_ref[...] = jnp.zeros_like(lacc_ref)

            e = x2 - tg_ref[...]
            x2_ref[...] = e * (1.0 / D)
            lacc_ref[...] += _rsum(e * e)

            @pl.when(i == nt - 1)
            def _():
                l_ref[...] = jnp.sum(lacc_ref[...], axis=1, keepdims=True) * (0.5 / D)

    vec = _const((1, D))
    blk = jax.ShapeDtypeStruct((4, t, FB), BF)
    in_specs = [_brows(4, tm, FB, 0), _bprev(4, 16, FB, tm, 0), _brows(4, tm, FB, 1), _const((4, 3, FB)),
                _const((4, 1, FB)), _const((F, D)), _rows(tm, D), vec, vec]
    out_specs = [_rows(tm, D), _rows(tm, D), _brows(4, tm, FB), _brows(4, tm, FB)]
    out_shape = [jax.ShapeDtypeStruct((t, D), F32), jax.ShapeDtypeStruct((t, D), BF), blk, blk]
    args = [up, up, up, convw, convb, w_down, x, gn, gt]
    scratch = []
    if target is not None:
        in_specs.append(_rows(tm, D))
        args.append(target)
        out_specs.append(pl.BlockSpec((1, 1), lambda i: (0, 0)))
        out_shape.append(jax.ShapeDtypeStruct((1, 1), F32))
        scratch.append(pltpu.VMEM((1, D), F32))
    return _call(body, name=name, grid=(nt,), in_specs=in_specs, out_specs=out_specs, out_shape=out_shape, args=args,
                 scratch_shapes=scratch, carry=carry)


def _init_stats(ref):
    @pl.when(pl.program_id(0) == 0)
    def _():
        ref[...] = jnp.zeros_like(ref)


def bwd_out(dxo, m, gn, gt, w, *, tm, name, carry=None):
    t = dxo.shape[0]
    k = w.shape[0]

    def body(dx_ref, m_ref, gn_ref, gt_ref, w_ref, dm_ref, da_ref, st_ref):
        _init_stats(st_ref)
        dm, dgt, dgn = _gated_res_bwd(dx_ref[...], m_ref[...].astype(F32), gn_ref[...], gt_ref[...])
        st_ref[0:1, :] += dgt
        st_ref[1:2, :] += dgn
        st_ref[2:3, :] += _rsum(dm)
        dmb = dm.astype(BF)
        dm_ref[...] = dmb
        da_ref[...] = _dot_nt(dmb, w_ref[...]).astype(BF)

    vec = _const((1, D))
    da_spec, da_shape = _rows(tm, k), jax.ShapeDtypeStruct((t, k), BF)
    return _call(
        body, name=name, grid=(t // tm,), in_specs=[_rows(tm, D), _rows(tm, D), vec, vec, _const((k, D))],
        out_specs=[_rows(tm, D), da_spec, _const((8, D), single=False)],
        out_shape=[jax.ShapeDtypeStruct((t, D), BF), da_shape, jax.ShapeDtypeStruct((8, D), F32)],
        args=[dxo, m, gn, gt, w], carry=carry)


def _adaln_bwd(dh, x_ref, g_ref, sc_ref, dxo_ref, st_ref, row=0):
    x = x_ref[...]
    r = lax.rsqrt(jnp.mean(x * x, axis=-1, keepdims=True) + RMS_EPS)
    xh = x * r
    gv = g_ref[...]
    st_ref[row:row + 1, :] += _rsum(dh)
    st_ref[row + 1:row + 2, :] += _rsum(dh * (xh * gv))
    dn = dh * (1.0 + sc_ref[...])
    st_ref[row + 2:row + 3, :] += _rsum(dn * xh)
    dy = dn * gv
    return dxo_ref[...] + r * (dy - xh * jnp.mean(dy * xh, axis=-1, keepdims=True))


def bwd_in(dp, w, x, g, sc, dxo, *, tm, name):
    t = x.shape[0]
    nb, _, bw = w.shape

    def body(dp_ref, w_ref, x_ref, g_ref, sc_ref, dxo_ref, dx_ref, st_ref):
        _init_stats(st_ref)
        dh = jnp.zeros((tm, D), F32)
        for d in range(nb):
            dh = dh + _dot_nt(dp_ref[:, d * bw:(d + 1) * bw], w_ref[d])
        dx_ref[...] = _adaln_bwd(dh, x_ref, g_ref, sc_ref, dxo_ref, st_ref)

    vec = _const((1, D))
    return pl.pallas_call(
        body, name=name, grid=(t // tm,),
        in_specs=[_rows(tm, nb * bw), _const(w.shape), _rows(tm, D), vec, vec, _rows(tm, D)],
        out_specs=[_rows(tm, D), _const((8, D), single=False)],
        out_shape=[jax.ShapeDtypeStruct((t, D), F32), jax.ShapeDtypeStruct((8, D), F32)],
        compiler_params=_params(1))(dp, w, x, g, sc, dxo)


def sc_bwd_out(dxo, m, gn, gt, w_out, p, convw, *, tm, name, carry=None):
    t = dxo.shape[0]

    def body(dx_ref, m_ref, gn_ref, gt_ref, w_ref, p_ref, ph_ref, cw_ref, dm_ref, dbg_ref, du_ref, st_ref):
        _init_stats(st_ref)
        dm, dgt, dgn = _gated_res_bwd(dx_ref[...], m_ref[...].astype(F32), gn_ref[...], gt_ref[...])
        st_ref[0:1, :] += dgt
        st_ref[1:2, :] += dgn
        dmb = dm.astype(BF)
        dm_ref[...] = dmb
        dq = _dot_nt(dmb, w_ref[...])
        dbg_ref[...] = (dq * _sc_conv(p_ref, ph_ref, cw_ref, pl.program_id(0) == 0, tm)).astype(BF)
        du_ref[...] = (dq * p_ref[:, 0:D].astype(F32)).astype(BF)

    vec = _const((1, D))
    out = jax.ShapeDtypeStruct((t, D), BF)
    return _call(
        body, name=name, grid=(t // tm,),
        in_specs=[_rows(tm, D), _rows(tm, D), vec, vec, _const((D, D)), _rows(tm, 3 * D), _prev(16, 3 * D, tm), _const((3, D))],
        out_specs=[_rows(tm, D)] * 3 + [_const((8, D), single=False)], out_shape=[out, out, out, jax.ShapeDtypeStruct((8, D), F32)],
        args=[dxo, m, gn, gt, w_out, p, p, convw], carry=carry)


def sc_bwd_in(du, dbg, p, convw, w, x, g, sc, dxo, *, tm, name, carry=None):
    t = x.shape[0]

    def body(du_ref, dun_ref, dbg_ref, p_ref, cw_ref, w_ref, x_ref, g_ref, sc_ref, dxo_ref, dx_ref, dp_ref, st_ref, sc2_ref):
        last = pl.program_id(0) == pl.num_programs(0) - 1
        _init_stats(st_ref)
        _init_stats(sc2_ref)
        du = du_ref[...].astype(F32)
        ext = jnp.concatenate([du, jnp.where(last, 0.0, dun_ref[0:8, :].astype(F32))], axis=0)
        e1, e2 = ext[1:tm + 1], ext[2:tm + 2]
        dz = cw_ref[2:3, :] * du + cw_ref[1:2, :] * e1 + cw_ref[0:1, :] * e2
        cg, hi = p_ref[:, D:2 * D].astype(F32), p_ref[:, 2 * D:3 * D].astype(F32)
        dbg, dcg, dhi = dbg_ref[...], (dz * hi).astype(BF), (dz * cg).astype(BF)
        dp_ref[:, 0:D] = dbg
        dp_ref[:, D:2 * D] = dcg
        dp_ref[:, 2 * D:3 * D] = dhi
        dh = _dot_nt(dbg, w_ref[0]) + _dot_nt(dcg, w_ref[1]) + _dot_nt(dhi, w_ref[2])
        z = cg * hi
        sc2_ref[0:1, :] += _rsum(z * e2)
        sc2_ref[1:2, :] += _rsum(z * e1)
        sc2_ref[2:3, :] += _rsum(z * du)
        dx_ref[...] = _adaln_bwd(dh, x_ref, g_ref, sc_ref, dxo_ref, st_ref)

    vec = _const((1, D))
    stat = jax.ShapeDtypeStruct((8, D), F32)
    return _call(
        body, name=name, grid=(t // tm,),
        in_specs=[_rows(tm, D), _next(16, D, tm, t), _rows(tm, D), _rows(tm, 3 * D), _const((3, D)), _const(w.shape),
                  _rows(tm, D), vec, vec, _rows(tm, D)],
        out_specs=[_rows(tm, D), _rows(tm, 3 * D), _const((8, D), single=False), _const((8, D), single=False)],
        out_shape=[jax.ShapeDtypeStruct((t, D), F32), jax.ShapeDtypeStruct((t, 3 * D), BF), stat, stat],
        args=[du, du, dbg, p, convw, w, x, g, sc, dxo], carry=carry)


def cf_bwd_mid(ds, u2, a, w_dw, ln_g, ln_b, *, tm, name, carry=None):
    t = ds.shape[0]
    hb = 32

    def du2_of(dsv, u2v, lg, lb):
        xh, rstd = _layernorm_parts(u2v)
        l = xh * lg + lb
        sg = _sigmoid(l)
        dl = dsv * (sg * (1.0 + l * (1.0 - sg)))
        dxh = dl * lg
        du2 = rstd * (dxh - jnp.mean(dxh, axis=-1, keepdims=True) - xh * jnp.mean(dxh * xh, axis=-1, keepdims=True))
        return du2, dl, xh

    rc, cc = 32, 256

    def body(ds_ref, dsn_ref, u2_ref, u2n_ref, a_ref, wd_ref, lg_ref, lb_ref, da_ref, st_ref, s1_ref, acc_ref):
        i = pl.program_id(0)
        last = i == pl.num_programs(0) - 1
        _init_stats(st_ref)
        _init_stats(acc_ref)
        lg, lb = lg_ref[...], lb_ref[...]
        du2, dl, xh = du2_of(ds_ref[...].astype(F32), u2_ref[...].astype(F32), lg, lb)
        st_ref[32:33, :] += _rsum(dl * xh)
        st_ref[33:34, :] += _rsum(dl)
        st_ref[31:32, :] += _rsum(du2)
        du2n, _, _ = du2_of(dsn_ref[...].astype(F32), u2n_ref[...].astype(F32), lg, lb)
        s1_ref[0, 0:tm, :] = du2
        s1_ref[0, tm:tm + hb, :] = jnp.where(last, 0.0, du2n)
        _row_shifted_copies(s1_ref, tm + hb - 8)

        def taps(r0, _):
            rows = pl.ds(r0, rc)
            for c0 in range(0, D, cc):
                sg = _sigmoid(a_ref[rows, D + c0:D + c0 + cc].astype(F32))
                u = a_ref[rows, c0:c0 + cc].astype(F32) * sg
                du = jnp.zeros((rc, cc), F32)
                for k in range(CFW):
                    o = CFW - 1 - k
                    sh = s1_ref[o % 8, pl.ds(pl.multiple_of(r0 + 8 * (o // 8), 8), rc), c0:c0 + cc]
                    du = du + wd_ref[k:k + 1, c0:c0 + cc] * sh
                    acc_ref[8 * k:8 * k + 8, c0:c0 + cc] += _fold8(u * sh)
                dav = du * sg
                dgv = du * u * (1.0 - sg)
                acc_ref[8 * CFW:8 * CFW + 8, c0:c0 + cc] += _fold8(dav)
                acc_ref[8 * CFW + 8:8 * CFW + 16, c0:c0 + cc] += _fold8(dgv)
                da_ref[rows, c0:c0 + cc] = dav.astype(BF)
                da_ref[rows, D + c0:D + c0 + cc] = dgv.astype(BF)
            return 0

        _chunks(tm, rc, taps)

        @pl.when(last)
        def _():
            for k in range(CFW):
                st_ref[k:k + 1, :] = jnp.sum(acc_ref[8 * k:8 * k + 8, :], axis=0, keepdims=True)
            st_ref[34:35, :] = jnp.sum(acc_ref[8 * CFW:8 * CFW + 8, :], axis=0, keepdims=True)
            st_ref[35:36, :] = jnp.sum(acc_ref[8 * CFW + 8:8 * CFW + 16, :], axis=0, keepdims=True)

    vec = _const((1, D))
    return _call(
        body, name=name, grid=(t // tm,),
        in_specs=[_rows(tm, D), _next(hb, D, tm, t), _rows(tm, D), _next(hb, D, tm, t), _rows(tm, 2 * D),
                  _const((CFW, D)), vec, vec],
        out_specs=[_rows(tm, 2 * D), _const((40, D), single=False)],
        out_shape=[jax.ShapeDtypeStruct((t, 2 * D), BF), jax.ShapeDtypeStruct((40, D), F32)],
        scratch_shapes=[pltpu.VMEM((8, tm + hb, D), F32), pltpu.VMEM((8 * (CFW + 2), D), F32)],
        args=[ds, ds, u2, u2, a, w_dw, ln_g, ln_b], carry=carry)


def pool_bwd(dxo, m, ypre, pw, pscale, gn, gt, x, g, sc, *, tm, name):
    t = dxo.shape[0]
    hb = 16

    def dyp_of(dxv, mv, ypv, ps, gnv, gtv):
        dm, dgt, dgn = _gated_res_bwd(dxv, mv, gnv, gtv)
        return dm * ps, dgt, dgn, _rsum(dm * ypv)

    def body(dx_ref, dxn_ref, m_ref, mn_ref, yp_ref, ypn_ref, pw_ref, ps_ref, gn_ref, gt_ref, x_ref, g_ref, sc_ref,
             dxi_ref, dyp_ref, st_ref, sa_ref, sb_ref):
        i = pl.program_id(0)
        last = i == pl.num_programs(0) - 1
        _init_stats(st_ref)
        ps, gnv, gtv = ps_ref[...], gn_ref[...], gt_ref[...]
        dyp, dgt, dgn, dps = dyp_of(dx_ref[...], m_ref[...].astype(F32), yp_ref[...].astype(F32), ps, gnv, gtv)
        st_ref[0:1, :] += dgt
        st_ref[1:2, :] += dgn
        st_ref[2:3, :] += dps
        st_ref[3:4, :] += _rsum(dyp)
        dypb = dyp.astype(BF)
        dyp_ref[...] = dypb
        dypn, _, _, _ = dyp_of(dxn_ref[...], mn_ref[...].astype(F32), ypn_ref[...].astype(F32), ps, gnv, gtv)
        dypnb = jnp.where(last, 0.0, dypn).astype(BF)
        dpo = []
        for gi, w in enumerate(POOL_WINDOWS):
            c0 = gi * PG
            dp_main = _dot_nt(dypb[:, c0:c0 + PG], pw_ref[gi])
            dp_next = _dot_nt(dypnb[:, c0:c0 + PG], pw_ref[gi])
            dpo.append(dp_main)
            sa_ref[0:tm, c0:c0 + PG] = dp_main / _pool_counts(i, tm, w)
            sa_ref[tm:tm + hb, c0:c0 + PG] = dp_next / float(w)
        zero = jnp.zeros((8, D), F32)
        sa_ref[tm + hb:tm + hb + 8, :] = zero
        sb_ref[tm + hb:tm + hb + 8, :] = zero
        n = tm + hb
        src, dst = sa_ref, sb_ref
        dhs = []
        for gi, w in enumerate(POOL_WINDOWS):
            c0 = gi * PG
            step = w // 2
            dst[pl.ds(0, n), c0:D] = src[pl.ds(0, n), c0:D] + src[pl.ds(step, n), c0:D]
            dhs.append(dst[pl.ds(0, tm), c0:c0 + PG] - dpo[gi])
            src, dst = dst, src
        dxi_ref[...] = _adaln_bwd(jnp.concatenate(dhs, axis=1), x_ref, g_ref, sc_ref, dx_ref, st_ref, row=4)

    vec = _const((1, D))
    return pl.pallas_call(
        body, name=name, grid=(t // tm,),
        in_specs=[_rows(tm, D), _next(hb, D, tm, t), _rows(tm, D), _next(hb, D, tm, t), _rows(tm, D),
                  _next(hb, D, tm, t), _const((4, PG, PG)), vec, vec, vec, _rows(tm, D), vec, vec],
        out_specs=[_rows(tm, D), _rows(tm, D), _const((8, D), single=False)],
        out_shape=[jax.ShapeDtypeStruct((t, D), F32), jax.ShapeDtypeStruct((t, D), BF), jax.ShapeDtypeStruct((8, D), F32)],
        scratch_shapes=[pltpu.VMEM((tm + hb + 8, D), F32)] * 2, compiler_params=_params(1),
    )(dxo, dxo, m, m, ypre, ypre, pw, pscale, gn, gt, x, g, sc)


def ffn_bwd_out(dxo, f, gn, gt, w_down, gc, up, *, tm, name, carry=None):
    t = dxo.shape[0]

    def body(dx_ref, f_ref, gn_ref, gt_ref, w_ref, gc_ref, val_ref, df_ref, dgc_ref, dval_ref, st_ref, sc_ref):
        _init_stats(st_ref)
        _init_stats(sc_ref)
        dm, dgt, dgn = _gated_res_bwd(dx_ref[...], f_ref[...].astype(F32), gn_ref[...], gt_ref[...])
        st_ref[0:1, :] += dgt
        st_ref[1:2, :] += dgn
        dmb = dm.astype(BF)
        df_ref[...] = dmb
        for j in range(4):
            da = _dot_nt(dmb, w_ref[j * FB:(j + 1) * FB, :])
            gcv = gc_ref[j].astype(F32)
            sg = _sigmoid(gcv)
            dval_ref[j] = (da * (gcv * sg)).astype(BF)
            dgc = da * val_ref[j].astype(F32) * (sg * (1.0 + gcv * (1.0 - sg)))
            dgc_ref[j] = dgc.astype(BF)
            sc_ref[j, 0:1, :] += _rsum(dgc)

    vec = _const((1, D))
    return _call(
        body, name=name, grid=(t // tm,),
        in_specs=[_rows(tm, D), _rows(tm, D), vec, vec, _const((F, D)), _brows(4, tm, FB), _brows(4, tm, FB, 1)],
        out_specs=[_rows(tm, D), _brows(4, tm, FB), _brows(4, tm, FB, 1), _const((8, D), single=False),
                   _const((4, 8, FB), single=False)],
        out_shape=[jax.ShapeDtypeStruct((t, D), BF), jax.ShapeDtypeStruct((4, t, FB), BF), jax.ShapeDtypeStruct((8, t, FB), BF),
                   jax.ShapeDtypeStruct((8, D), F32), jax.ShapeDtypeStruct((4, 8, FB), F32)],
        args=[dxo, f, gn, gt, w_down, gc, up], carry=carry)


def ffn_bwd_in(dgc, dup, up, convw, w, x, g, sc, dxo, *, tm, name, carry=None):
    t = x.shape[0]

    def body(dgc_ref, dgcn_ref, dval_ref, gate_ref, cw_ref, w_ref, x_ref, g_ref, sc_ref, dxo_ref,
             dx_ref, dgate_ref, st_ref, sc2_ref):
        last = pl.program_id(0) == pl.num_programs(0) - 1
        _init_stats(st_ref)
        _init_stats(sc2_ref)
        dh = jnp.zeros((tm, D), F32)
        for j in range(4):
            dgc = dgc_ref[j].astype(F32)
            ext = jnp.concatenate([dgc, jnp.where(last, 0.0, dgcn_ref[j, 0:8, :].astype(F32))], axis=0)
            e1, e2 = ext[1:tm + 1], ext[2:tm + 2]
            dgate = (cw_ref[j, 2:3, :] * dgc + cw_ref[j, 1:2, :] * e1 + cw_ref[j, 0:1, :] * e2).astype(BF)
            dgate_ref[j] = dgate
            dh = dh + _dot(dgate, w_ref[j])
            gate = gate_ref[j].astype(F32)
            sc2_ref[j, 1:2, :] += _rsum(gate * e2)
            sc2_ref[j, 2:3, :] += _rsum(gate * e1)
            sc2_ref[j, 3:4, :] += _rsum(gate * dgc)
        for j in range(4):
            dh = dh + _dot(dval_ref[j], w_ref[4 + j])
        dx_ref[...] = _adaln_bwd(dh, x_ref, g_ref, sc_ref, dxo_ref, st_ref)

    vec = _const((1, D))
    return _call(
        body, name=name, grid=(t // tm,),
        in_specs=[_brows(4, tm, FB), _bnext(4, 16, FB, tm, t), _brows(4, tm, FB, 1), _brows(4, tm, FB, 0), _const((4, 3, FB)),
                  _const(w.shape), _rows(tm, D), vec, vec, _rows(tm, D)],
        out_specs=[_rows(tm, D), _brows(4, tm, FB, 0), _const((8, D), single=False), _const((4, 8, FB), single=False)],
        out_shape=[jax.ShapeDtypeStruct((t, D), F32), jax.ShapeDtypeStruct((8, t, FB), BF), jax.ShapeDtypeStruct((8, D), F32),
                   jax.ShapeDtypeStruct((4, 8, FB), F32)],
        args=[dgc, dgc, dup, up, convw, w, x, g, sc, dxo], aliases={2: 1}, carry=carry)


def wgrad(a, b, *, nblk, a_blocked, b_blocked, bk, bn, tt, name, carry=None):
    t = a.shape[1] if a.ndim == 3 else a.shape[0]
    nt = t // tt

    def body(a_ref, b_ref, o_ref, acc_ref):
        s = pl.program_id(1)

        @pl.when(s == 0)
        def _():
            acc_ref[...] = jnp.zeros_like(acc_ref)

        av = a_ref[0] if a.ndim == 3 else a_ref[...]
        bv = b_ref[0] if b.ndim == 3 else b_ref[...]
        acc_ref[...] += _dot_tn(av, bv)

        @pl.when(s == nt - 1)
        def _():
            o_ref[0] = acc_ref[...].astype(BF)

    def spec(arr, blocked, width):
        if arr.ndim == 3:
            return pl.BlockSpec((1, tt, width), lambda j, s: (j, s, 0))
        if blocked:
            return pl.BlockSpec((tt, width), lambda j, s: (s, j))
        return pl.BlockSpec((tt, width), lambda j, s: (s, 0))

    return _call(
        body, name=name, grid=(nblk, nt), in_specs=[spec(a, a_blocked, bk), spec(b, b_blocked, bn)],
        out_specs=[pl.BlockSpec((1, bk, bn), lambda j, s: (j, 0, 0))],
        out_shape=[jax.ShapeDtypeStruct((nblk, bk, bn), BF)],
        scratch_shapes=[pltpu.VMEM((bk, bn), F32)], args=[a, b], carry=carry)


def mod_partial(c_all, w_mod):
    cols = w_mod.shape[2]

    def body(c_ref, w_ref, o_ref):
        c = c_ref[...]
        ca = c * _sigmoid(c)
        o_ref[0] = jnp.dot(ca, w_ref[0], preferred_element_type=F32, precision=lax.Precision.HIGHEST)

    return pl.pallas_call(
        body, name="mod_partial", grid=(DEPTH,),
        in_specs=[pl.BlockSpec((NDEV, D), lambda l: (0, 0)), pl.BlockSpec((1, D, cols), lambda l: (l, 0, 0))],
        out_specs=pl.BlockSpec((1, NDEV, cols), lambda l: (l, 0, 0)),
        out_shape=jax.ShapeDtypeStruct((DEPTH, NDEV, cols), F32), compiler_params=_params(1))(c_all, w_mod)


def mod_finish(parts, b_mod):
    cols = parts.shape[2]

    def body(p_ref, b_ref, o_ref):
        for e in range(NDEV):
            o_ref[:, e * cols:(e + 1) * cols] = p_ref[e] + b_ref[:, e * cols:(e + 1) * cols]

    return pl.pallas_call(
        body, name="mod_finish", out_shape=jax.ShapeDtypeStruct((DEPTH, NDEV * cols), F32))(parts, b_mod)


def sum_parts(parts):
    n, r, c = parts.shape

    def body(p_ref, o_ref):
        acc = p_ref[0]
        for j in range(1, n):
            acc = acc + p_ref[j]
        o_ref[...] = acc

    return pl.pallas_call(body, name="sum_parts", out_shape=jax.ShapeDtypeStruct((r, c), F32))(parts)


def mod_wgrad(c_all_t, gmod_cols):
    cols = gmod_cols.shape[2]

    def body(c_ref, g_ref, o_ref):
        c = c_ref[...]
        ca = c * _sigmoid(c)
        acc = ca[:, 0:1] * g_ref[0, 0:1, :]
        for b in range(1, NDEV):
            acc = acc + ca[:, b:b + 1] * g_ref[0, b:b + 1, :]
        o_ref[0] = acc

    return pl.pallas_call(
        body, name="mod_wgrad", grid=(DEPTH,),
        in_specs=[pl.BlockSpec((D, NDEV), lambda l: (0, 0)), pl.BlockSpec((1, NDEV, cols), lambda l: (l, 0, 0))],
        out_specs=pl.BlockSpec((1, D, cols), lambda l: (l, 0, 0)),
        out_shape=jax.ShapeDtypeStruct((DEPTH, D, cols), F32), compiler_params=_params(1))(c_all_t, gmod_cols)


def _adamw_math(g, w, m, v):
    m2 = B1 * m + (1.0 - B1) * g
    v2 = B2 * v + (1.0 - B2) * (g * g)
    m_hat = m2 / (1.0 - B1 ** STEP)
    v_hat = v2 / (1.0 - B2 ** STEP)
    delta = -LR * (m_hat / (jnp.sqrt(v_hat) + ADAM_EPS) + WD * w)
    return delta, m2, v2


def _row_tile(r, c, budget=1 << 18):
    if r * c <= budget or r % 8:
        return r
    best = 8
    for cand in range(8, r + 1, 8):
        if r % cand == 0 and cand * c <= budget:
            best = cand
    return best


def adamw_sum(parts, w, m, v, *, name):
    n, r, c = parts.shape
    tr = _row_tile(r, c)

    def body(p_ref, w_ref, m_ref, v_ref, g_ref, d_ref, m2_ref, v2_ref):
        g = p_ref[0].astype(F32)
        for j in range(1, n):
            g = g + p_ref[j].astype(F32)
        d, m2, v2 = _adamw_math(g, w_ref[...], m_ref[...], v_ref[...])
        g_ref[...] = g
        d_ref[...] = d
        m2_ref[...] = m2
        v2_ref[...] = v2

    blk = pl.BlockSpec((tr, c), lambda i: (i, 0))
    out = jax.ShapeDtypeStruct((r, c), F32)
    return pl.pallas_call(
        body, name=name, grid=(r // tr,), in_specs=[pl.BlockSpec((n, tr, c), lambda i: (0, i, 0)), blk, blk, blk],
        out_specs=[blk] * 4, out_shape=[out] * 4, compiler_params=_params(1))(parts, w, m, v)


def adamw_layer(parts, w, m, v, prev, layer, *, name):
    n, r, c = parts.shape
    nl = w.shape[0]
    tr = _row_tile(r, c)

    def body(p_ref, w_ref, m_ref, v_ref, *rest):
        g_ref, d_ref, m2_ref, v2_ref = rest[-4:]
        g = p_ref[0].astype(F32)
        for j in range(1, n):
            g = g + p_ref[j].astype(F32)
        d, m2, v2 = _adamw_math(g, w_ref[0], m_ref[0], v_ref[0])
        g_ref[0] = g
        d_ref[0] = d
        m2_ref[0] = m2
        v2_ref[0] = v2

    blk = pl.BlockSpec((1, tr, c), lambda i: (layer, i, 0))
    in_specs = [pl.BlockSpec((n, tr, c), lambda i: (0, i, 0)), blk, blk, blk]
    args = [parts, w, m, v]
    aliases = {}
    if prev is not None:
        in_specs += [ANY] * 4
        args += list(prev)
        aliases = {4 + k: k for k in range(4)}
    out = jax.ShapeDtypeStruct((nl, r, c), F32)
    return pl.pallas_call(
        body, name=name, grid=(r // tr,), in_specs=in_specs, out_specs=[blk] * 4, out_shape=[out] * 4,
        input_output_aliases=aliases, compiler_params=_params(1))(*args)


def _pack(arrays):
    flat, layout, off = [], [], 0
    for a in arrays:
        flat.append(a.reshape(-1))
        layout.append((off, a.shape))
        off += a.size
    pad = (-off) % 1024
    if pad:
        flat.append(jnp.zeros((pad,), F32))
    return jnp.concatenate(flat).reshape(-1, 128), layout


def _unpack(packed, layout, lead=()):
    flat = packed.reshape(lead + (-1,))
    return [flat[..., off:off + _size(shape)].reshape(lead + tuple(shape)) for off, shape in layout]


def _size(shape):
    n = 1
    for s in shape:
        n *= s
    return n


def _join_last(g):
    g = jnp.moveaxis(g, 0, -2)
    return g.reshape(g.shape[:-2] + (g.shape[-2] * g.shape[-1],))


def _my_cols(a, width):
    return lax.dynamic_slice_in_dim(a, _my_id() * width, width, axis=a.ndim - 1)


def _tile(t, pref):
    return min(pref, t)


def kernel(x, c, w_mod, b_mod, norm_g, sc_w_in, sc_conv, sc_w_out, pool_w, pool_b, pool_scale, cf_w_pw1, cf_b_pw1, cf_w_dw, cf_b_dw, cf_ln_g, cf_ln_b, cf_w_pw2, cf_b_pw2, ffn_w_up, ffn_conv, ffn_b_conv, ffn_w_down, loss_target, m_w_mod, m_b_mod, m_norm_g, m_sc_w_in, m_sc_conv, m_sc_w_out, m_pool_w, m_pool_b, m_pool_scale, m_cf_w_pw1, m_cf_b_pw1, m_cf_w_dw, m_cf_b_dw, m_cf_ln_g, m_cf_ln_b, m_cf_w_pw2, m_cf_b_pw2, m_ffn_w_up, m_ffn_conv, m_ffn_b_conv, m_ffn_w_down, v_w_mod, v_b_mod, v_norm_g, v_sc_w_in, v_sc_conv, v_sc_w_out, v_pool_w, v_pool_b, v_pool_scale, v_cf_w_pw1, v_cf_b_pw1, v_cf_w_dw, v_cf_b_dw, v_cf_ln_g, v_cf_ln_b, v_cf_w_pw2, v_cf_b_pw2, v_ffn_w_up, v_ffn_conv, v_ffn_b_conv, v_ffn_w_down):
    env = dict(locals())
    names = ["w_mod", "b_mod", "norm_g", "sc_w_in", "sc_conv", "sc_w_out", "pool_w", "pool_b", "pool_scale", "cf_w_pw1",
             "cf_b_pw1", "cf_w_dw", "cf_b_dw", "cf_ln_g", "cf_ln_b", "cf_w_pw2", "cf_b_pw2", "ffn_w_up", "ffn_conv",
             "ffn_b_conv", "ffn_w_down"]
    t = x.shape[1]
    tm = _tile(t, 512)
    tt = _tile(t, 4096)
    x0, target = x[0], loss_target[0]

    small_names = ["norm_g", "sc_conv", "cf_b_pw1", "cf_w_dw", "cf_b_dw", "cf_ln_g", "cf_ln_b", "cf_b_pw2", "ffn_conv"]
    packed, layout = _pack([c] + [env[n] for n in small_names])

    shard = {"pool": pool_w[0].astype(BF), "pw1": cf_w_pw1[0].astype(BF), "pw2": cf_w_pw2[0].astype(BF)}
    for j in range(2):
        shard[f"in{j}"], shard[f"out{j}"] = sc_w_in[j].astype(BF), sc_w_out[j].astype(BF)
    for l in range(DEPTH):
        shard[f"up{l}"], shard[f"down{l}"] = ffn_w_up[l].T.astype(BF), ffn_w_down[l].astype(BF)
    gathered, g_in0, g_out0 = _Exchange("gather", [packed, shard["in0"], shard["out0"]]).run("gather_first")
    wg = {"in0": g_in0, "out0": g_out0}
    parts = _unpack(gathered, layout, lead=(NDEV,))
    c_all = parts[0].reshape(NDEV, D)
    full = {n: _join_last(p) for n, p in zip(small_names, parts[1:])}
    fwd_plan = {("mix_in", 0): ["up0"], ("mix_out", 0): ["down0"], ("ffn_in", 0): ["pool", "up1"],
                ("ffn_out", 0): ["down1", "pw1", "pw2"], ("ffn_in", 1): ["up2"], ("ffn_out", 1): ["down2", "in1", "out1"],
                ("ffn_in", 2): ["up3"], ("ffn_out", 2): ["down3"]}

    def carrying(plan, kind, store, source, fn, key, *a, **k):
        names = plan.get(key)
        if not names:
            return fn(*a, **k)
        res = fn(*a, carry=_Exchange(kind, [source[n] for n in names]), **k)
        store.update(zip(names, res[-1]))
        return res[:-1]

    fwd = functools.partial(carrying, fwd_plan, "gather", wg, shard)

    mp = mod_partial(c_all, w_mod)
    (mod_parts,) = _Exchange("scatter", [jnp.swapaxes(mp, 0, 1)]).run("exchange_mod")
    mod = mod_finish(mod_parts, b_mod)

    def vec(a):
        return a.reshape(1, -1)

    def col_blocks(g):
        w = jnp.swapaxes(g, 0, 1).reshape(D, -1)
        return jnp.swapaxes(w.reshape(D, -1, D), 0, 1)

    def ffn_blocks(a):
        return jnp.swapaxes(a.reshape(a.shape[0], 4, FB), 0, 1)

    saved = []
    xs = x0
    for l in range(DEPTH):
        sh1, sc1, g1, sh2, sc2, g2 = [mod[l:l + 1, k * D:(k + 1) * D] for k in range(6)]
        ng = [full["norm_g"][l, k:k + 1] for k in range(4)]
        kind, j = l % 3, l // 3
        s = dict(x_in=xs, sc1=sc1, g1=g1, sc2=sc2, g2=g2, ng=ng)
        if kind == 0:
            s["w_in"] = col_blocks(wg[f"in{j}"])
            s["h"], s["p"] = fwd(fwd_in, ("mix_in", l), xs, ng[0], sc1, sh1, s["w_in"], None, blocked=False, tm=tm,
                                 name=f"sc_in_{l}")
            x1, s["m"], s["q"] = fwd(sc_fwd_out, ("mix_out", l), s["p"], full["sc_conv"][j], wg[f"out{j}"].reshape(D, D), xs,
                                     ng[1], g1, tm=tm, name=f"sc_out_{l}")
        elif kind == 1:
            pool_w_f = jnp.swapaxes(wg["pool"], 0, 1).reshape(4, PG, PG)
            x1, s["m"], s["ypre"], s["pooled"] = pool_fwd(xs, ng[0], sc1, sh1, pool_w_f, pool_b, pool_scale, ng[1], g1,
                                                          tm=tm, name=f"pool_{l}")
        else:
            s["w_in"] = col_blocks(wg["pw1"])
            s["h"], s["a"] = fwd_in(xs, ng[0], sc1, sh1, s["w_in"], full["cf_b_pw1"].reshape(2, 1, D), blocked=False, tm=tm,
                                    name=f"cf_in_{l}")
            x1, s["m"], s["s"], s["u2"] = cf_fwd_out(s["a"], full["cf_w_dw"][0], full["cf_b_dw"], full["cf_ln_g"],
                                                     full["cf_ln_b"], wg["pw2"].reshape(D, D), full["cf_b_pw2"], xs, ng[1],
                                                     g1, tm=tm, name=f"cf_out_{l}")
        s["x1"] = x1
        s["cw"] = ffn_blocks(full["ffn_conv"][l])
        s["h2"], s["up"] = fwd(fwd_in, ("ffn_in", l), x1, ng[2], sc2, sh2, wg[f"up{l}"], None, blocked=True, wt=True, tm=tm,
                               name=f"ffn_in_{l}")
        xs, s["f"], s["gc"], s["fa"], *loss_part = fwd(
            ffn_fwd_out, ("ffn_out", l), s["up"], s["cw"], ffn_blocks(ffn_b_conv[l:l + 1]), wg[f"down{l}"].reshape(F, D), x1,
            ng[3], g2, tm=tm, name=f"ffn_out_{l}", target=target if l == DEPTH - 1 else None)
        saved.append(s)

    dx = xs
    loss = lax.psum(loss_part[0][0, 0], ("x", "y", "c"))

    gmod = [None] * DEPTH
    d_norm_g = [None] * DEPTH
    d_ffn_conv = [None] * DEPTH
    d_ffn_b_conv = [None] * DEPTH
    d_sc_conv = [None] * 2
    big = {}
    got = {}
    small_g = {}
    bwd_plan = {("mix_bout", 3): ["down3"], ("mix_bin", 3): ["up3"], ("ffn_bout", 2): ["in1", "out1"],
                ("mix_bmid", 2): ["up2", "down2"], ("ffn_bout", 1): ["pw1", "pw2"], ("ffn_bout", 0): ["pool", "down1"],
                ("ffn_bin", 0): ["up1"], ("mix_bout", 0): ["down0"], ("mix_win", 0): ["up0"], ("mix_wout", 0): ["in0"]}
    bwd = functools.partial(carrying, bwd_plan, "scatter", got, big)
    pool_w_f = jnp.swapaxes(wg["pool"], 0, 1).reshape(4, PG, PG)
    for l in reversed(range(DEPTH)):
        s = saved[l]
        ng = s["ng"]
        kind, j = l % 3, l // 3
        df, dgc, dup, st_o, st_b = bwd(ffn_bwd_out, ("ffn_bout", l), dx, s["f"], ng[3], s["g2"], wg[f"down{l}"].reshape(F, D),
                                       s["gc"], s["up"], tm=tm, name=f"ffn_bout_{l}")
        dx1, dup, st_i, st_c = bwd(ffn_bwd_in, ("ffn_bin", l), dgc, dup, s["up"], s["cw"], wg[f"up{l}"], s["x1"], ng[2],
                                   s["sc2"], dx, tm=tm, name=f"ffn_bin_{l}")
        (big[f"up{l}"],) = wgrad(dup, s["h2"], nblk=NDEV, a_blocked=True, b_blocked=False, bk=FB, bn=D, tt=tt,
                                 name=f"ffn_wup_{l}")
        big[f"down{l}"] = wgrad(s["fa"], df, nblk=4, a_blocked=True, b_blocked=False, bk=FB, bn=D, tt=tt,
                                name=f"ffn_wdown_{l}")[0].reshape(NDEV, F // NDEV, D)
        d_ffn_b_conv[l] = st_b[:, 0, :].reshape(F)
        d_ffn_conv[l] = jnp.swapaxes(st_c[:, 1:4, :], 0, 1).reshape(3, F)
        g_ffn = [st_i[0], st_i[1], st_o[0]]
        dn3, dn2 = st_o[1], st_i[2]
        if kind == 0:
            dm, dbg, du, st_o = bwd(sc_bwd_out, ("mix_bout", l), dx1, s["m"], ng[1], s["g1"], wg[f"out{j}"].reshape(D, D), s["p"],
                                    full["sc_conv"][j], tm=tm, name=f"sc_bout_{l}")
            dx, dp, st_i, st_c = bwd(sc_bwd_in, ("mix_bin", l), du, dbg, s["p"], full["sc_conv"][j], s["w_in"], s["x_in"], ng[0],
                                     s["sc1"], dx1, tm=tm, name=f"sc_bin_{l}")
            (big[f"in{j}"],) = bwd(wgrad, ("mix_win", l), s["h"], dp, nblk=NDEV, a_blocked=False, b_blocked=True, bk=D,
                                   bn=3 * D // NDEV, tt=tt, name=f"sc_win_{l}")
            big[f"out{j}"] = bwd(wgrad, ("mix_wout", l), s["q"], dm, nblk=1, a_blocked=False, b_blocked=False, bk=D, bn=D, tt=tt,
                                 name=f"sc_wout_{l}")[0].reshape(NDEV, D // NDEV, D)
            d_sc_conv[j] = st_c[0:3]
        elif kind == 1:
            dx, dyp, st_o = pool_bwd(dx1, s["m"], s["ypre"], pool_w_f, pool_scale, ng[1], s["g1"], s["x_in"], ng[0], s["sc1"],
                                     tm=tm, name=f"pool_b_{l}")
            st_i = st_o[4:7]
            (dpw,) = wgrad(s["pooled"], dyp, nblk=4, a_blocked=True, b_blocked=True, bk=PG, bn=PG, tt=tt, name=f"pool_w_{l}")
            big["pool"] = jnp.swapaxes(dpw.reshape(4, NDEV, PG // NDEV, PG), 0, 1).reshape(NDEV, 4 * PG // NDEV, PG)
            small_g["pool_scale"], small_g["pool_b"] = st_o[2:3], st_o[3:4]
        else:
            dm, ds, st_o = bwd_out(dx1, s["m"], ng[1], s["g1"], wg["pw2"].reshape(D, D), tm=tm,
                                   name=f"cf_bout_{l}")
            dA, st_c = bwd(cf_bwd_mid, ("mix_bmid", l), ds, s["u2"], s["a"], full["cf_w_dw"][0], full["cf_ln_g"],
                           full["cf_ln_b"], tm=tm, name=f"cf_bmid_{l}")
            dx, st_i = bwd_in(dA, s["w_in"], s["x_in"], ng[0], s["sc1"], dx1, tm=tm, name=f"cf_bin_{l}")
            (big["pw1"],) = wgrad(s["h"], dA, nblk=NDEV, a_blocked=False, b_blocked=True, bk=D, bn=2 * D // NDEV, tt=tt,
                                  name=f"cf_wpw1_{l}")
            big["pw2"] = wgrad(s["s"], dm, nblk=1, a_blocked=False, b_blocked=False, bk=D, bn=D, tt=tt,
                               name=f"cf_wpw2_{l}")[0].reshape(NDEV, D // NDEV, D)
            small_g["cf_w_dw"] = st_c[0:CFW][None]
            small_g["cf_b_dw"], small_g["cf_ln_g"], small_g["cf_ln_b"] = st_c[31:32], st_c[32:33], st_c[33:34]
            small_g["cf_b_pw1"] = st_c[34:36].reshape(1, 2 * D)
            small_g["cf_b_pw2"] = st_o[2:3]
        gmod[l] = jnp.concatenate([st_i[0], st_i[1], st_o[0]] + g_ffn)
        d_norm_g[l] = jnp.stack([st_i[2], st_o[1], dn2, dn3])

    small_g["gmod"] = jnp.stack(gmod)
    small_g["norm_g"] = jnp.stack(d_norm_g)
    small_g["sc_conv"] = jnp.stack(d_sc_conv)
    small_g["ffn_conv"] = jnp.stack(d_ffn_conv)
    small_g["ffn_b_conv"] = jnp.stack(d_ffn_b_conv)
    sg_names = ["gmod", "norm_g", "sc_conv", "pool_b", "pool_scale", "cf_b_pw1", "cf_w_dw", "cf_b_dw", "cf_ln_g", "cf_ln_b",
                "cf_b_pw2", "ffn_conv", "ffn_b_conv"]
    gpacked, glayout = _pack([small_g[n] for n in sg_names])
    (ggath,), (got["out0"],) = _run_exchanges(
        [_Exchange("gather", [gpacked]), _Exchange("scatter", [big["out0"]])], "exchange_last")
    gsum = dict(zip(sg_names, _unpack(sum_parts(ggath), glayout)))
    gmod_all = _unpack(ggath, glayout[:1], lead=(NDEV,))[0]
    grads = {"b_mod": gsum["gmod"], "pool_b": gsum["pool_b"], "pool_scale": gsum["pool_scale"],
             "ffn_b_conv": gsum["ffn_b_conv"]}
    for n in ["norm_g", "sc_conv", "cf_b_pw1", "cf_w_dw", "cf_b_dw", "cf_ln_g", "cf_ln_b", "cf_b_pw2", "ffn_conv"]:
        grads[n] = _my_cols(gsum[n], env[n].shape[-1])
    grads["w_mod"] = mod_wgrad(c_all.T, jnp.swapaxes(_my_cols(gmod_all, w_mod.shape[2]), 0, 1))

    deltas, new_m, new_v = {}, {}, {}
    sp_names = ["b_mod", "norm_g", "sc_conv", "pool_b", "pool_scale", "cf_b_pw1", "cf_w_dw", "cf_b_dw", "cf_ln_g", "cf_ln_b",
                "cf_b_pw2", "ffn_conv", "ffn_b_conv"]
    pg, playout = _pack([grads[n] for n in sp_names])
    pw_, _ = _pack([env[n] for n in sp_names])
    pm_, _ = _pack([env["m_" + n] for n in sp_names])
    pv_, _ = _pack([env["v_" + n] for n in sp_names])
    _, sd, sm, sv = adamw_sum(pg[None], pw_, pm_, pv_, name="adamw_small")
    for n, d_, m_, v_ in zip(sp_names, _unpack(sd, playout), _unpack(sm, playout), _unpack(sv, playout)):
        deltas[n], new_m[n], new_v[n] = d_, m_, v_
    gw = grads["w_mod"].reshape(1, DEPTH * D, -1)
    _, d_, m_, v_ = adamw_sum(gw, w_mod.reshape(gw.shape[1:]), m_w_mod.reshape(gw.shape[1:]), v_w_mod.reshape(gw.shape[1:]),
                              name="adamw_w_mod")
    deltas["w_mod"], new_m["w_mod"], new_v["w_mod"] = [a.reshape(w_mod.shape) for a in (d_, m_, v_)]

    groups = {"sc_w_in": ["in0", "in1"], "sc_w_out": ["out0", "out1"], "pool_w": ["pool"], "cf_w_pw1": ["pw1"],
              "cf_w_pw2": ["pw2"], "ffn_w_up": [f"up{l}" for l in range(DEPTH)], "ffn_w_down": [f"down{l}" for l in range(DEPTH)]}
    for n, layers in groups.items():
        stacked = (len(layers),) + got[layers[0]].shape[1:]
        flip = n == "ffn_w_up"
        w3 = [(jnp.swapaxes(env[p + n], 1, 2) if flip else env[p + n]).reshape(stacked) for p in ("", "m_", "v_")]
        outs = None
        for li, key in enumerate(layers):
            outs = adamw_layer(got[key], *w3, outs, li, name=f"adamw_{n}_{li}")
        grads[n], deltas[n], new_m[n], new_v[n] = [(jnp.swapaxes(a, 1, 2) if flip else a).reshape(env[n].shape) for a in outs]

    return (loss, dx[None], *[grads[n] for n in names], *[deltas[n] for n in names], *[new_m[n] for n in names],
            *[new_v[n] for n in names])
```

```python
import functools

import jax
import jax.numpy as jnp
from jax import lax
from jax.experimental import pallas as pl
from jax.experimental.pallas import tpu as pltpu

D = 1024
F = 2816
NDEV = 8
FB = F // 4
DEPTH = 4
RMS_EPS = 1e-6
LN_EPS = 1e-5
CFW = 31
POOL_WINDOWS = (2, 4, 8, 16)
PG = D // 4
LR, B1, B2, ADAM_EPS, WD, STEP = 0.001, 0.9, 0.999, 1e-08, 0.01, 10

BF = jnp.bfloat16
F32 = jnp.float32
VMEM_LIMIT_V7X = 56 * 1024 * 1024
MXU_COLS_V7X = 256
MESH = pl.DeviceIdType.MESH
ANY = pl.BlockSpec(memory_space=pl.ANY)


def _params(n_axes):
    return pltpu.CompilerParams(dimension_semantics=("arbitrary",) * n_axes, vmem_limit_bytes=VMEM_LIMIT_V7X)


def _const(shape, single=True):
    nd = len(shape)
    if single:
        return pl.BlockSpec(shape, lambda *_: (0,) * nd, pipeline_mode=pl.Buffered(1))
    return pl.BlockSpec(shape, lambda *_: (0,) * nd)


def _rows(tm, c):
    return pl.BlockSpec((tm, c), lambda i: (i, 0))


def _brows(nb, tm, c, b0=0):
    return pl.BlockSpec((nb, tm, c), lambda i: (b0, i, 0))


def _prev(hb, c, tm):
    return pl.BlockSpec((hb, c), lambda i: (jnp.maximum(i * (tm // hb) - 1, 0), 0))


def _next(hb, c, tm, t):
    return pl.BlockSpec((hb, c), lambda i: (jnp.minimum((i + 1) * (tm // hb), t // hb - 1), 0))


def _bprev(nb, hb, c, tm, b0=0):
    return pl.BlockSpec((nb, hb, c), lambda i: (b0, jnp.maximum(i * (tm // hb) - 1, 0), 0))


def _bnext(nb, hb, c, tm, t, b0=0):
    return pl.BlockSpec((nb, hb, c), lambda i: (b0, jnp.minimum((i + 1) * (tm // hb), t // hb - 1), 0))


def _sigmoid(v):
    return 0.5 * jnp.tanh(0.5 * v) + 0.5


def _fold8(v):
    r, c = v.shape
    return jnp.sum(v.reshape(r // 8, 8, c), axis=0)


def _chunks(n_rows, rc, step, init=0, reverse=False):
    n = n_rows // rc

    def it(c, carry):
        idx = (n - 1 - c) if reverse else c
        return step(pl.multiple_of(idx * rc, rc), carry)

    return lax.fori_loop(0, n, it, init)


def _row_shifted_copies(s_ref, n):
    for b in range(1, 8):
        s_ref[b, 0:n, :] = s_ref[0, pl.ds(b, n), :]


def _shifted(s_ref, o, tm):
    return s_ref[o % 8, pl.ds(8 * (o // 8), tm), :]


def _dot(a, b):
    return jnp.dot(a, b, preferred_element_type=F32)


def _dot_nt(a, b):
    return lax.dot_general(a, b, (((1,), (1,)), ((), ())), preferred_element_type=F32)


def _dot_tn(a, b):
    return lax.dot_general(a, b, (((0,), (0,)), ((), ())), preferred_element_type=F32)


def _rsum(v):
    return jnp.sum(v, axis=0, keepdims=True)


def _adaln(x, g, sc, sh):
    r = lax.rsqrt(jnp.mean(x * x, axis=-1, keepdims=True) + RMS_EPS)
    return (x * r * g) * (1.0 + sc) + sh


def _gated_res(x, m, gn, gt):
    r = lax.rsqrt(jnp.mean(m * m, axis=-1, keepdims=True) + RMS_EPS)
    return x + gt * (m * r * gn)


def _gated_res_bwd(dxo, m, gn, gt):
    r = lax.rsqrt(jnp.mean(m * m, axis=-1, keepdims=True) + RMS_EPS)
    mh = m * r
    dgt = _rsum(dxo * (mh * gn))
    dn = dxo * gt
    dgn = _rsum(dn * mh)
    dmh = dn * gn
    dm = r * (dmh - mh * jnp.mean(dmh * mh, axis=-1, keepdims=True))
    return dm, dgt, dgn


def _my_id():
    return 4 * lax.axis_index("x") + 2 * lax.axis_index("y") + lax.axis_index("c")


def _peer(k):
    x, y, c = lax.axis_index("x"), lax.axis_index("y"), lax.axis_index("c")
    px = 1 - x if k & 4 else x
    py = 1 - y if k & 2 else y
    pc = 1 - c if k & 1 else c
    return (px, py, pc), 4 * px + 2 * py + pc


class _Exchange:
    def __init__(self, kind, arrays):
        self.gather = kind == "gather"
        self.arrays = list(arrays)
        n = len(self.arrays)
        if self.gather:
            self.out_shape = [jax.ShapeDtypeStruct((NDEV,) + a.shape, a.dtype) for a in self.arrays]
        else:
            self.out_shape = [jax.ShapeDtypeStruct(a.shape, a.dtype) for a in self.arrays]
        self.scratch = [pltpu.SemaphoreType.DMA((n * NDEV,)), pltpu.SemaphoreType.DMA((n * NDEV,)),
                        pltpu.SemaphoreType.DMA((n,))]

    def _local(self, a, src, dst, sems):
        me = _my_id()
        return pltpu.make_async_copy(src[a] if self.gather else src[a].at[me], dst[a].at[me], sems[2].at[a])

    def _remote(self, a, k, src, dst, sems, incoming):
        to, pid = _peer(k)
        me = _my_id()
        return pltpu.make_async_remote_copy(
            src_ref=src[a] if self.gather else src[a].at[pid], dst_ref=dst[a].at[pid if incoming else me],
            send_sem=sems[0].at[a * NDEV + k], recv_sem=sems[1].at[a * NDEV + k], device_id=to, device_id_type=MESH)

    def start(self, src, dst, sems):
        for a in range(len(self.arrays)):
            self._local(a, src, dst, sems).start()
        for k in range(1, NDEV):
            for a in range(len(self.arrays)):
                self._remote(a, k, src, dst, sems, False).start()

    def wait(self, src, dst, sems):
        for k in range(1, NDEV):
            for a in range(len(self.arrays)):
                self._remote(a, k, src, dst, sems, True).wait_recv()
        for k in range(1, NDEV):
            for a in range(len(self.arrays)):
                self._remote(a, k, src, dst, sems, False).wait_send()
        for a in range(len(self.arrays)):
            self._local(a, src, dst, sems).wait()

    def run(self, name):
        n = len(self.arrays)

        def body(*refs):
            src, dst, sems = refs[:n], refs[n:2 * n], refs[2 * n:]
            self.start(src, dst, sems)
            self.wait(src, dst, sems)

        return pl.pallas_call(body, name=name, in_specs=[ANY] * n, out_specs=[ANY] * n, out_shape=self.out_shape,
                              scratch_shapes=self.scratch)(*self.arrays)


def _run_exchanges(exchanges, name):
    counts = [len(e.arrays) for e in exchanges]
    n = sum(counts)

    def body(*refs):
        src, dst, sems = refs[:n], refs[n:2 * n], refs[2 * n:]
        parts, lo = [], 0
        for ei, (e, c) in enumerate(zip(exchanges, counts)):
            parts.append((e, src[lo:lo + c], dst[lo:lo + c], sems[3 * ei:3 * ei + 3]))
            lo += c
        for e, s, d, m in parts:
            e.start(s, d, m)
        for e, s, d, m in parts:
            e.wait(s, d, m)

    res = pl.pallas_call(
        body, name=name, in_specs=[ANY] * n, out_specs=[ANY] * n, out_shape=[s for e in exchanges for s in e.out_shape],
        scratch_shapes=[s for e in exchanges for s in e.scratch])(*[a for e in exchanges for a in e.arrays])
    out, lo = [], 0
    for c in counts:
        out.append(list(res[lo:lo + c]))
        lo += c
    return out


def _call(body, *, name, grid, in_specs, out_specs, out_shape, args, scratch_shapes=(), carry=None, aliases=None):
    cp = _params(len(grid))
    aliases = aliases or {}
    if carry is None:
        return tuple(pl.pallas_call(body, name=name, grid=grid, in_specs=in_specs, out_specs=out_specs, out_shape=out_shape,
                                    scratch_shapes=list(scratch_shapes), input_output_aliases=aliases,
                                    compiler_params=cp)(*args))
    n_in, n_out, n_sc, n_c = len(in_specs), len(out_specs), len(scratch_shapes), len(carry.arrays)

    def wrapped(*refs):
        ins, src = refs[:n_in], refs[n_in:n_in + n_c]
        outs = refs[n_in + n_c:n_in + n_c + n_out]
        dst = refs[n_in + n_c + n_out:n_in + 2 * n_c + n_out]
        rest = refs[n_in + 2 * n_c + n_out:]
        scr, sems = rest[:n_sc], rest[n_sc:]
        first = pl.program_id(0) == 0
        last = pl.program_id(0) == grid[0] - 1
        for ax in range(1, len(grid)):
            first = jnp.logical_and(first, pl.program_id(ax) == 0)
            last = jnp.logical_and(last, pl.program_id(ax) == grid[ax] - 1)

        @pl.when(first)
        def _():
            carry.start(src, dst, sems)

        body(*ins, *outs, *scr)

        @pl.when(last)
        def _():
            carry.wait(src, dst, sems)

    res = pl.pallas_call(
        wrapped, name=name, grid=grid, in_specs=list(in_specs) + [ANY] * n_c, out_specs=list(out_specs) + [ANY] * n_c,
        out_shape=list(out_shape) + carry.out_shape, scratch_shapes=list(scratch_shapes) + carry.scratch,
        input_output_aliases=aliases, compiler_params=cp)(*args, *carry.arrays)
    return tuple(res[:n_out]) + (list(res[n_out:]),)


def fwd_in(x, g, sc, sh, w, bias, *, blocked, tm, name, carry=None, wt=False):
    t = x.shape[0]
    nb, bw = (w.shape[0], w.shape[1]) if wt else (w.shape[0], w.shape[2])
    per = next(k for k in (1, 2, 4, 8) if (k * bw) % MXU_COLS_V7X == 0)
    assert not wt or (blocked and bias is None and nb % per == 0)

    def body(*refs):
        if bias is None:
            x_ref, g_ref, sc_ref, sh_ref, w_ref, h_ref, p_ref = refs
        else:
            x_ref, g_ref, sc_ref, sh_ref, w_ref, b_ref, h_ref, p_ref = refs
        hb = _adaln(x_ref[...], g_ref[...], sc_ref[...], sh_ref[...]).astype(BF)
        h_ref[...] = hb
        if wt:
            for c in range(nb // per):
                y = _dot_nt(hb, w_ref[c * per:(c + 1) * per].reshape(per * bw, D))
                for d in range(per):
                    p_ref[c * per + d] = y[:, d * bw:(d + 1) * bw].astype(BF)
            return
        for d in range(nb):
            y = _dot(hb, w_ref[d])
            if bias is not None:
                y = y + b_ref[d]
            if blocked:
                p_ref[d] = y.astype(BF)
            else:
                p_ref[:, d * bw:(d + 1) * bw] = y.astype(BF)

    vec = _const((1, D))
    in_specs = [_rows(tm, D), vec, vec, vec, _const(w.shape)]
    args = [x, g, sc, sh, w]
    if bias is not None:
        in_specs.append(_const((nb, 1, bw)))
        args.append(bias)
    if blocked:
        p_spec, p_shape = _brows(nb, tm, bw), jax.ShapeDtypeStruct((nb, t, bw), BF)
    else:
        p_spec, p_shape = _rows(tm, nb * bw), jax.ShapeDtypeStruct((t, nb * bw), BF)
    return _call(body, name=name, grid=(t // tm,), in_specs=in_specs, out_specs=[_rows(tm, D), p_spec],
                 out_shape=[jax.ShapeDtypeStruct((t, D), BF), p_shape], args=args, carry=carry)


def _sc_conv(p_ref, ph_ref, cw_ref, first, tm):
    z = p_ref[:, D:2 * D].astype(F32) * p_ref[:, 2 * D:3 * D].astype(F32)
    zp = jnp.where(first, 0.0, ph_ref[8:16, D:2 * D].astype(F32) * ph_ref[8:16, 2 * D:3 * D].astype(F32))
    ext = jnp.concatenate([zp, z], axis=0)
    return cw_ref[0:1, :] * ext[6:6 + tm] + cw_ref[1:2, :] * ext[7:7 + tm] + cw_ref[2:3, :] * z


def sc_fwd_out(p, convw, w_out, x, gn, gt, *, tm, name, carry=None):
    t = x.shape[0]

    def body(p_ref, ph_ref, cw_ref, w_ref, x_ref, gn_ref, gt_ref, x1_ref, m_ref, q_ref):
        u = _sc_conv(p_ref, ph_ref, cw_ref, pl.program_id(0) == 0, tm)
        qb = (p_ref[:, 0:D].astype(F32) * u).astype(BF)
        q_ref[...] = qb
        m = _dot(qb, w_ref[...])
        m_ref[...] = m.astype(BF)
        x1_ref[...] = _gated_res(x_ref[...], m, gn_ref[...], gt_ref[...])

    vec = _const((1, D))
    return _call(
        body, name=name, grid=(t // tm,),
        in_specs=[_rows(tm, 3 * D), _prev(16, 3 * D, tm), _const((3, D)), _const((D, D)), _rows(tm, D), vec, vec],
        out_specs=[_rows(tm, D)] * 3,
        out_shape=[jax.ShapeDtypeStruct((t, D), F32), jax.ShapeDtypeStruct((t, D), BF), jax.ShapeDtypeStruct((t, D), BF)],
        args=[p, p, convw, w_out, x, gn, gt], carry=carry)


def _layernorm_parts(u2):
    mu = jnp.mean(u2, axis=-1, keepdims=True)
    cen = u2 - mu
    rstd = lax.rsqrt(jnp.mean(cen * cen, axis=-1, keepdims=True) + LN_EPS)
    return cen * rstd, rstd


def cf_fwd_out(a, w_dw, b_dw, ln_g, ln_b, w_pw2, b_pw2, x, gn, gt, *, tm, name):
    t = x.shape[0]
    hb = 32

    def body(a_ref, ah_ref, wd_ref, bd_ref, lg_ref, lb_ref, w_ref, b2_ref, x_ref, gn_ref, gt_ref,
             x1_ref, m_ref, s_out_ref, u2_ref, s_ref):
        i = pl.program_id(0)
        uh = ah_ref[:, 0:D].astype(F32) * _sigmoid(ah_ref[:, D:2 * D].astype(F32))
        s_ref[0, 0:hb, :] = jnp.where(i == 0, 0.0, uh)
        s_ref[0, hb:hb + tm, :] = a_ref[:, 0:D].astype(F32) * _sigmoid(a_ref[:, D:2 * D].astype(F32))
        _row_shifted_copies(s_ref, tm + hb - 8)
        acc = bd_ref[...] + wd_ref[0:1, :] * _shifted(s_ref, hb - CFW + 1, tm)
        for k in range(1, CFW):
            acc = acc + wd_ref[k:k + 1, :] * _shifted(s_ref, hb - CFW + 1 + k, tm)
        u2_ref[...] = acc.astype(BF)
        xh, _ = _layernorm_parts(acc)
        l = xh * lg_ref[...] + lb_ref[...]
        sb = (l * _sigmoid(l)).astype(BF)
        s_out_ref[...] = sb
        m = _dot(sb, w_ref[...]) + b2_ref[...]
        m_ref[...] = m.astype(BF)
        x1_ref[...] = _gated_res(x_ref[...], m, gn_ref[...], gt_ref[...])

    vec = _const((1, D))
    return pl.pallas_call(
        body, name=name, grid=(t // tm,),
        in_specs=[_rows(tm, 2 * D), _prev(hb, 2 * D, tm), _const((CFW, D)), vec, vec, vec, _const((D, D)), vec,
                  _rows(tm, D), vec, vec],
        out_specs=[_rows(tm, D)] * 4,
        out_shape=[jax.ShapeDtypeStruct((t, D), F32)] + [jax.ShapeDtypeStruct((t, D), BF)] * 3,
        scratch_shapes=[pltpu.VMEM((8, tm + hb, D), F32)], compiler_params=_params(1),
    )(a, a, w_dw, b_dw, ln_g, ln_b, w_pw2, b_pw2, x, gn, gt)


def _pool_counts(i, tm, w):
    row = lax.broadcasted_iota(jnp.int32, (tm, 1), 0) + i * tm
    return jnp.minimum(row + 1, w).astype(F32)


def pool_fwd(x, g, sc, sh, pw, pb, pscale, gn, gt, *, tm, name):
    t = x.shape[0]
    pad, hb = 8, 16
    base = pad + hb

    def body(x_ref, xh_ref, g_ref, sc_ref, sh_ref, pw_ref, pb_ref, ps_ref, gn_ref, gt_ref,
             x1_ref, m_ref, yp_ref, po_ref, sa_ref, sb_ref):
        i = pl.program_id(0)
        hh = _adaln(xh_ref[...], g_ref[...], sc_ref[...], sh_ref[...])
        h = _adaln(x_ref[...], g_ref[...], sc_ref[...], sh_ref[...])
        zero = jnp.zeros((pad, D), F32)
        sa_ref[0:pad, :] = zero
        sb_ref[0:pad, :] = zero
        sa_ref[pad:base, :] = jnp.where(i == 0, 0.0, hh)
        sa_ref[base:base + tm, :] = h
        n = hb + tm
        src, dst = sa_ref, sb_ref
        ys = []
        for gi, w in enumerate(POOL_WINDOWS):
            c0 = gi * PG
            step = w // 2
            dst[pl.ds(pad, n), c0:D] = src[pl.ds(pad, n), c0:D] + src[pl.ds(pad - step, n), c0:D]
            mean = dst[pl.ds(base, tm), c0:c0 + PG] / _pool_counts(i, tm, w)
            pooled = (mean - h[:, c0:c0 + PG]).astype(BF)
            po_ref[:, c0:c0 + PG] = pooled
            ys.append(_dot(pooled, pw_ref[gi]))
            src, dst = dst, src
        ypre = jnp.concatenate(ys, axis=1) + pb_ref[...]
        yp_ref[...] = ypre.astype(BF)
        m = ypre * ps_ref[...]
        m_ref[...] = m.astype(BF)
        x1_ref[...] = _gated_res(x_ref[...], m, gn_ref[...], gt_ref[...])

    vec = _const((1, D))
    return pl.pallas_call(
        body, name=name, grid=(t // tm,),
        in_specs=[_rows(tm, D), _prev(hb, D, tm), vec, vec, vec, _const((4, PG, PG)), vec, vec, vec, vec],
        out_specs=[_rows(tm, D)] * 4,
        out_shape=[jax.ShapeDtypeStruct((t, D), F32)] + [jax.ShapeDtypeStruct((t, D), BF)] * 3,
        scratch_shapes=[pltpu.VMEM((tm + base, D), F32)] * 2, compiler_params=_params(1),
    )(x, x, g, sc, sh, pw, pb, pscale, gn, gt)


def ffn_fwd_out(up, convw, convb, w_down, x, gn, gt, *, tm, name, carry=None, target=None):
    t = x.shape[0]
    nt = t // tm

    def body(gate_ref, gh_ref, val_ref, cw_ref, cb_ref, w_ref, x_ref, gn_ref, gt_ref, *rest):
        if target is None:
            x2_ref, f_ref, gc_ref, a_ref = rest
        else:
            tg_ref, x2_ref, f_ref, gc_ref, a_ref, l_ref, lacc_ref = rest
        i = pl.program_id(0)
        acc = jnp.zeros((tm, D), F32)
        for j in range(4):
            gate = gate_ref[j].astype(F32)
            ext = jnp.concatenate([jnp.where(i == 0, 0.0, gh_ref[j, 8:16, :].astype(F32)), gate], axis=0)
            gc = cb_ref[j] + cw_ref[j, 0:1, :] * ext[6:6 + tm] + cw_ref[j, 1:2, :] * ext[7:7 + tm] + cw_ref[j, 2:3, :] * gate
            gc_ref[j] = gc.astype(BF)
            ab = (gc * _sigmoid(gc) * val_ref[j].astype(F32)).astype(BF)
            a_ref[j] = ab
            acc = acc + _dot(ab, w_ref[j * FB:(j + 1) * FB, :])
        f_ref[...] = acc.astype(BF)
        x2 = _gated_res(x_ref[...], acc, gn_ref[...], gt_ref[...])
        if target is None:
            x2_ref[...] = x2
        else:
            @pl.when(i == 0)
            def _():
                lacc_ref[...] = jnp.zeros_like(lacc_ref)

            e = x2 - tg_ref[...]
            x2_ref[...] = e * (1.0 / D)
            lacc_ref[...] += _rsum(e * e)

            @pl.when(i == nt - 1)
            def _():
                l_ref[...] = jnp.sum(lacc_ref[...], axis=1, keepdims=True) * (0.5 / D)

    vec = _const((1, D))
    blk = jax.ShapeDtypeStruct((4, t, FB), BF)
    in_specs = [_brows(4, tm, FB, 0), _bprev(4, 16, FB, tm, 0), _brows(4, tm, FB, 1), _const((4, 3, FB)),
                _const((4, 1, FB)), _const((F, D)), _rows(tm, D), vec, vec]
    out_specs = [_rows(tm, D), _rows(tm, D), _brows(4, tm, FB), _brows(4, tm, FB)]
    out_shape = [jax.ShapeDtypeStruct((t, D), F32), jax.ShapeDtypeStruct((t, D), BF), blk, blk]
    args = [up, up, up, convw, convb, w_down, x, gn, gt]
    scratch = []
    if target is not None:
        in_specs.append(_rows(tm, D))
        args.append(target)
        out_specs.append(pl.BlockSpec((1, 1), lambda i: (0, 0)))
        out_shape.append(jax.ShapeDtypeStruct((1, 1), F32))
        scratch.append(pltpu.VMEM((1, D), F32))
    return _call(body, name=name, grid=(nt,), in_specs=in_specs, out_specs=out_specs, out_shape=out_shape, args=args,
                 scratch_shapes=scratch, carry=carry)


def _init_stats(ref):
    @pl.when(pl.program_id(0) == 0)
    def _():
        ref[...] = jnp.zeros_like(ref)


def bwd_out(dxo, m, gn, gt, w, *, tm, name, carry=None):
    t = dxo.shape[0]
    k = w.shape[0]

    def body(dx_ref, m_ref, gn_ref, gt_ref, w_ref, dm_ref, da_ref, st_ref):
        _init_stats(st_ref)
        dm, dgt, dgn = _gated_res_bwd(dx_ref[...], m_ref[...].astype(F32), gn_ref[...], gt_ref[...])
        st_ref[0:1, :] += dgt
        st_ref[1:2, :] += dgn
        st_ref[2:3, :] += _rsum(dm)
        dmb = dm.astype(BF)
        dm_ref[...] = dmb
        da_ref[...] = _dot_nt(dmb, w_ref[...]).astype(BF)

    vec = _const((1, D))
    da_spec, da_shape = _rows(tm, k), jax.ShapeDtypeStruct((t, k), BF)
    return _call(
        body, name=name, grid=(t // tm,), in_specs=[_rows(tm, D), _rows(tm, D), vec, vec, _const((k, D))],
        out_specs=[_rows(tm, D), da_spec, _const((8, D), single=False)],
        out_shape=[jax.ShapeDtypeStruct((t, D), BF), da_shape, jax.ShapeDtypeStruct((8, D), F32)],
        args=[dxo, m, gn, gt, w], carry=carry)


def _adaln_bwd(dh, x_ref, g_ref, sc_ref, dxo_ref, st_ref, row=0):
    x = x_ref[...]
    r = lax.rsqrt(jnp.mean(x * x, axis=-1, keepdims=True) + RMS_EPS)
    xh = x * r
    gv = g_ref[...]
    st_ref[row:row + 1, :] += _rsum(dh)
    st_ref[row + 1:row + 2, :] += _rsum(dh * (xh * gv))
    dn = dh * (1.0 + sc_ref[...])
    st_ref[row + 2:row + 3, :] += _rsum(dn * xh)
    dy = dn * gv
    return dxo_ref[...] + r * (dy - xh * jnp.mean(dy * xh, axis=-1, keepdims=True))


def bwd_in(dp, w, x, g, sc, dxo, *, tm, name):
    t = x.shape[0]
    nb, _, bw = w.shape

    def body(dp_ref, w_ref, x_ref, g_ref, sc_ref, dxo_ref, dx_ref, st_ref):
        _init_stats(st_ref)
        dh = jnp.zeros((tm, D), F32)
        for d in range(nb):
            dh = dh + _dot_nt(dp_ref[:, d * bw:(d + 1) * bw], w_ref[d])
        dx_ref[...] = _adaln_bwd(dh, x_ref, g_ref, sc_ref, dxo_ref, st_ref)

    vec = _const((1, D))
    return pl.pallas_call(
        body, name=name, grid=(t // tm,),
        in_specs=[_rows(tm, nb * bw), _const(w.shape), _rows(tm, D), vec, vec, _rows(tm, D)],
        out_specs=[_rows(tm, D), _const((8, D), single=False)],
        out_shape=[jax.ShapeDtypeStruct((t, D), F32), jax.ShapeDtypeStruct((8, D), F32)],
        compiler_params=_params(1))(dp, w, x, g, sc, dxo)


def sc_bwd_out(dxo, m, gn, gt, w_out, p, convw, *, tm, name, carry=None):
    t = dxo.shape[0]

    def body(dx_ref, m_ref, gn_ref, gt_ref, w_ref, p_ref, ph_ref, cw_ref, dm_ref, dbg_ref, du_ref, st_ref):
        _init_stats(st_ref)
        dm, dgt, dgn = _gated_res_bwd(dx_ref[...], m_ref[...].astype(F32), gn_ref[...], gt_ref[...])
        st_ref[0:1, :] += dgt
        st_ref[1:2, :] += dgn
        dmb = dm.astype(BF)
        dm_ref[...] = dmb
        dq = _dot_nt(dmb, w_ref[...])
        dbg_ref[...] = (dq * _sc_conv(p_ref, ph_ref, cw_ref, pl.program_id(0) == 0, tm)).astype(BF)
        du_ref[...] = (dq * p_ref[:, 0:D].astype(F32)).astype(BF)

    vec = _const((1, D))
    out = jax.ShapeDtypeStruct((t, D), BF)
    return _call(
        body, name=name, grid=(t // tm,),
        in_specs=[_rows(tm, D), _rows(tm, D), vec, vec, _const((D, D)), _rows(tm, 3 * D), _prev(16, 3 * D, tm), _const((3, D))],
        out_specs=[_rows(tm, D)] * 3 + [_const((8, D), single=False)], out_shape=[out, out, out, jax.ShapeDtypeStruct((8, D), F32)],
        args=[dxo, m, gn, gt, w_out, p, p, convw], carry=carry)


def sc_bwd_in(du, dbg, p, convw, w, x, g, sc, dxo, *, tm, name, carry=None):
    t = x.shape[0]

    def body(du_ref, dun_ref, dbg_ref, p_ref, cw_ref, w_ref, x_ref, g_ref, sc_ref, dxo_ref, dx_ref, dp_ref, st_ref, sc2_ref):
        last = pl.program_id(0) == pl.num_programs(0) - 1
        _init_stats(st_ref)
        _init_stats(sc2_ref)
        du = du_ref[...].astype(F32)
        ext = jnp.concatenate([du, jnp.where(last, 0.0, dun_ref[0:8, :].astype(F32))], axis=0)
        e1, e2 = ext[1:tm + 1], ext[2:tm + 2]
        dz = cw_ref[2:3, :] * du + cw_ref[1:2, :] * e1 + cw_ref[0:1, :] * e2
        cg, hi = p_ref[:, D:2 * D].astype(F32), p_ref[:, 2 * D:3 * D].astype(F32)
        dbg, dcg, dhi = dbg_ref[...], (dz * hi).astype(BF), (dz * cg).astype(BF)
        dp_ref[:, 0:D] = dbg
        dp_ref[:, D:2 * D] = dcg
        dp_ref[:, 2 * D:3 * D] = dhi
        dh = _dot_nt(dbg, w_ref[0]) + _dot_nt(dcg, w_ref[1]) + _dot_nt(dhi, w_ref[2])
        z = cg * hi
        sc2_ref[0:1, :] += _rsum(z * e2)
        sc2_ref[1:2, :] += _rsum(z * e1)
        sc2_ref[2:3, :] += _rsum(z * du)
        dx_ref[...] = _adaln_bwd(dh, x_ref, g_ref, sc_ref, dxo_ref, st_ref)

    vec = _const((1, D))
    stat = jax.ShapeDtypeStruct((8, D), F32)
    return _call(
        body, name=name, grid=(t // tm,),
        in_specs=[_rows(tm, D), _next(16, D, tm, t), _rows(tm, D), _rows(tm, 3 * D), _const((3, D)), _const(w.shape),
                  _rows(tm, D), vec, vec, _rows(tm, D)],
        out_specs=[_rows(tm, D), _rows(tm, 3 * D), _const((8, D), single=False), _const((8, D), single=False)],
        out_shape=[jax.ShapeDtypeStruct((t, D), F32), jax.ShapeDtypeStruct((t, 3 * D), BF), stat, stat],
        args=[du, du, dbg, p, convw, w, x, g, sc, dxo], carry=carry)


def cf_bwd_mid(ds, u2, a, w_dw, ln_g, ln_b, *, tm, name, carry=None):
    t = ds.shape[0]
    hb = 32

    def du2_of(dsv, u2v, lg, lb):
        xh, rstd = _layernorm_parts(u2v)
        l = xh * lg + lb
        sg = _sigmoid(l)
        dl = dsv * (sg * (1.0 + l * (1.0 - sg)))
        dxh = dl * lg
        du2 = rstd * (dxh - jnp.mean(dxh, axis=-1, keepdims=True) - xh * jnp.mean(dxh * xh, axis=-1, keepdims=True))
        return du2, dl, xh

    rc, cc = 32, 256

    def body(ds_ref, dsn_ref, u2_ref, u2n_ref, a_ref, wd_ref, lg_ref, lb_ref, da_ref, st_ref, s1_ref, acc_ref):
        i = pl.program_id(0)
        last = i == pl.num_programs(0) - 1
        _init_stats(st_ref)
        _init_stats(acc_ref)
        lg, lb = lg_ref[...], lb_ref[...]
        du2, dl, xh = du2_of(ds_ref[...].astype(F32), u2_ref[...].astype(F32), lg, lb)
        st_ref[32:33, :] += _rsum(dl * xh)
        st_ref[33:34, :] += _rsum(dl)
        st_ref[31:32, :] += _rsum(du2)
        du2n, _, _ = du2_of(dsn_ref[...].astype(F32), u2n_ref[...].astype(F32), lg, lb)
        s1_ref[0, 0:tm, :] = du2
        s1_ref[0, tm:tm + hb, :] = jnp.where(last, 0.0, du2n)
        _row_shifted_copies(s1_ref, tm + hb - 8)

        def taps(r0, _):
            rows = pl.ds(r0, rc)
            for c0 in range(0, D, cc):
                sg = _sigmoid(a_ref[rows, D + c0:D + c0 + cc].astype(F32))
                u = a_ref[rows, c0:c0 + cc].astype(F32) * sg
                du = jnp.zeros((rc, cc), F32)
                for k in range(CFW):
                    o = CFW - 1 - k
                    sh = s1_ref[o % 8, pl.ds(pl.multiple_of(r0 + 8 * (o // 8), 8), rc), c0:c0 + cc]
                    du = du + wd_ref[k:k + 1, c0:c0 + cc] * sh
                    acc_ref[8 * k:8 * k + 8, c0:c0 + cc] += _fold8(u * sh)
                dav = du * sg
                dgv = du * u * (1.0 - sg)
                acc_ref[8 * CFW:8 * CFW + 8, c0:c0 + cc] += _fold8(dav)
                acc_ref[8 * CFW + 8:8 * CFW + 16, c0:c0 + cc] += _fold8(dgv)
                da_ref[rows, c0:c0 + cc] = dav.astype(BF)
                da_ref[rows, D + c0:D + c0 + cc] = dgv.astype(BF)
            return 0

        _chunks(tm, rc, taps)

        @pl.when(last)
        def _():
            for k in range(CFW):
                st_ref[k:k + 1, :] = jnp.sum(acc_ref[8 * k:8 * k + 8, :], axis=0, keepdims=True)
            st_ref[34:35, :] = jnp.sum(acc_ref[8 * CFW:8 * CFW + 8, :], axis=0, keepdims=True)
            st_ref[35:36, :] = jnp.sum(acc_ref[8 * CFW + 8:8 * CFW + 16, :], axis=0, keepdims=True)

    vec = _const((1, D))
    return _call(
        body, name=name, grid=(t // tm,),
        in_specs=[_rows(tm, D), _next(hb, D, tm, t), _rows(tm, D), _next(hb, D, tm, t), _rows(tm, 2 * D),
                  _const((CFW, D)), vec, vec],
        out_specs=[_rows(tm, 2 * D), _const((40, D), single=False)],
        out_shape=[jax.ShapeDtypeStruct((t, 2 * D), BF), jax.ShapeDtypeStruct((40, D), F32)],
        scratch_shapes=[pltpu.VMEM((8, tm + hb, D), F32), pltpu.VMEM((8 * (CFW + 2), D), F32)],
        args=[ds, ds, u2, u2, a, w_dw, ln_g, ln_b], carry=carry)


def pool_bwd(dxo, m, ypre, pw, pscale, gn, gt, x, g, sc, *, tm, name):
    t = dxo.shape[0]
    hb = 16

    def dyp_of(dxv, mv, ypv, ps, gnv, gtv):
        dm, dgt, dgn = _gated_res_bwd(dxv, mv, gnv, gtv)
        return dm * ps, dgt, dgn, _rsum(dm * ypv)

    def body(dx_ref, dxn_ref, m_ref, mn_ref, yp_ref, ypn_ref, pw_ref, ps_ref, gn_ref, gt_ref, x_ref, g_ref, sc_ref,
             dxi_ref, dyp_ref, st_ref, sa_ref, sb_ref):
        i = pl.program_id(0)
        last = i == pl.num_programs(0) - 1
        _init_stats(st_ref)
        ps, gnv, gtv = ps_ref[...], gn_ref[...], gt_ref[...]
        dyp, dgt, dgn, dps = dyp_of(dx_ref[...], m_ref[...].astype(F32), yp_ref[...].astype(F32), ps, gnv, gtv)
        st_ref[0:1, :] += dgt
        st_ref[1:2, :] += dgn
        st_ref[2:3, :] += dps
        st_ref[3:4, :] += _rsum(dyp)
        dypb = dyp.astype(BF)
        dyp_ref[...] = dypb
        dypn, _, _, _ = dyp_of(dxn_ref[...], mn_ref[...].astype(F32), ypn_ref[...].astype(F32), ps, gnv, gtv)
        dypnb = jnp.where(last, 0.0, dypn).astype(BF)
        dpo = []
        for gi, w in enumerate(POOL_WINDOWS):
            c0 = gi * PG
            dp_main = _dot_nt(dypb[:, c0:c0 + PG], pw_ref[gi])
            dp_next = _dot_nt(dypnb[:, c0:c0 + PG], pw_ref[gi])
            dpo.append(dp_main)
            sa_ref[0:tm, c0:c0 + PG] = dp_main / _pool_counts(i, tm, w)
            sa_ref[tm:tm + hb, c0:c0 + PG] = dp_next / float(w)
        zero = jnp.zeros((8, D), F32)
        sa_ref[tm + hb:tm + hb + 8, :] = zero
        sb_ref[tm + hb:tm + hb + 8, :] = zero
        n = tm + hb
        src, dst = sa_ref, sb_ref
        dhs = []
        for gi, w in enumerate(POOL_WINDOWS):
            c0 = gi * PG
            step = w // 2
            dst[pl.ds(0, n), c0:D] = src[pl.ds(0, n), c0:D] + src[pl.ds(step, n), c0:D]
            dhs.append(dst[pl.ds(0, tm), c0:c0 + PG] - dpo[gi])
            src, dst = dst, src
        dxi_ref[...] = _adaln_bwd(jnp.concatenate(dhs, axis=1), x_ref, g_ref, sc_ref, dx_ref, st_ref, row=4)

    vec = _const((1, D))
    return pl.pallas_call(
        body, name=name, grid=(t // tm,),
        in_specs=[_rows(tm, D), _next(hb, D, tm, t), _rows(tm, D), _next(hb, D, tm, t), _rows(tm, D),
                  _next(hb, D, tm, t), _const((4, PG, PG)), vec, vec, vec, _rows(tm, D), vec, vec],
        out_specs=[_rows(tm, D), _rows(tm, D), _const((8, D), single=False)],
        out_shape=[jax.ShapeDtypeStruct((t, D), F32), jax.ShapeDtypeStruct((t, D), BF), jax.ShapeDtypeStruct((8, D), F32)],
        scratch_shapes=[pltpu.VMEM((tm + hb + 8, D), F32)] * 2, compiler_params=_params(1),
    )(dxo, dxo, m, m, ypre, ypre, pw, pscale, gn, gt, x, g, sc)


def ffn_bwd_out(dxo, f, gn, gt, w_down, gc, up, *, tm, name, carry=None):
    t = dxo.shape[0]

    def body(dx_ref, f_ref, gn_ref, gt_ref, w_ref, gc_ref, val_ref, df_ref, dgc_ref, dval_ref, st_ref, sc_ref):
        _init_stats(st_ref)
        _init_stats(sc_ref)
        dm, dgt, dgn = _gated_res_bwd(dx_ref[...], f_ref[...].astype(F32), gn_ref[...], gt_ref[...])
        st_ref[0:1, :] += dgt
        st_ref[1:2, :] += dgn
        dmb = dm.astype(BF)
        df_ref[...] = dmb
        da_all = _dot_nt(dmb, w_ref[...])
        for j in range(4):
            da = da_all[:, j * FB:(j + 1) * FB]
            gcv = gc_ref[j].astype(F32)
            sg = _sigmoid(gcv)
            dval_ref[j] = (da * (gcv * sg)).astype(BF)
            dgc = da * val_ref[j].astype(F32) * (sg * (1.0 + gcv * (1.0 - sg)))
            dgc_ref[j] = dgc.astype(BF)
            sc_ref[j, 0:1, :] += _rsum(dgc)

    vec = _const((1, D))
    return _call(
        body, name=name, grid=(t // tm,),
        in_specs=[_rows(tm, D), _rows(tm, D), vec, vec, _const((F, D)), _brows(4, tm, FB), _brows(4, tm, FB, 1)],
        out_specs=[_rows(tm, D), _brows(4, tm, FB), _brows(4, tm, FB, 1), _const((8, D), single=False),
                   _const((4, 8, FB), single=False)],
        out_shape=[jax.ShapeDtypeStruct((t, D), BF), jax.ShapeDtypeStruct((4, t, FB), BF), jax.ShapeDtypeStruct((8, t, FB), BF),
                   jax.ShapeDtypeStruct((8, D), F32), jax.ShapeDtypeStruct((4, 8, FB), F32)],
        args=[dxo, f, gn, gt, w_down, gc, up], carry=carry)


def ffn_bwd_in(dgc, dup, up, convw, w, x, g, sc, dxo, *, tm, name, carry=None):
    t = x.shape[0]

    def body(dgc_ref, dgcn_ref, dval_ref, gate_ref, cw_ref, w_ref, x_ref, g_ref, sc_ref, dxo_ref,
             dx_ref, dgate_ref, st_ref, sc2_ref):
        last = pl.program_id(0) == pl.num_programs(0) - 1
        _init_stats(st_ref)
        _init_stats(sc2_ref)
        dh = jnp.zeros((tm, D), F32)
        for j in range(4):
            dgc = dgc_ref[j].astype(F32)
            ext = jnp.concatenate([dgc, jnp.where(last, 0.0, dgcn_ref[j, 0:8, :].astype(F32))], axis=0)
            e1, e2 = ext[1:tm + 1], ext[2:tm + 2]
            dgate = (cw_ref[j, 2:3, :] * dgc + cw_ref[j, 1:2, :] * e1 + cw_ref[j, 0:1, :] * e2).astype(BF)
            dgate_ref[j] = dgate
            dh = dh + _dot(dgate, w_ref[j])
            gate = gate_ref[j].astype(F32)
            sc2_ref[j, 1:2, :] += _rsum(gate * e2)
            sc2_ref[j, 2:3, :] += _rsum(gate * e1)
            sc2_ref[j, 3:4, :] += _rsum(gate * dgc)
        for j in range(4):
            dh = dh + _dot(dval_ref[j], w_ref[4 + j])
        dx_ref[...] = _adaln_bwd(dh, x_ref, g_ref, sc_ref, dxo_ref, st_ref)

    vec = _const((1, D))
    return _call(
        body, name=name, grid=(t // tm,),
        in_specs=[_brows(4, tm, FB), _bnext(4, 16, FB, tm, t), _brows(4, tm, FB, 1), _brows(4, tm, FB, 0), _const((4, 3, FB)),
                  _const(w.shape), _rows(tm, D), vec, vec, _rows(tm, D)],
        out_specs=[_rows(tm, D), _brows(4, tm, FB, 0), _const((8, D), single=False), _const((4, 8, FB), single=False)],
        out_shape=[jax.ShapeDtypeStruct((t, D), F32), jax.ShapeDtypeStruct((8, t, FB), BF), jax.ShapeDtypeStruct((8, D), F32),
                   jax.ShapeDtypeStruct((4, 8, FB), F32)],
        args=[dgc, dgc, dup, up, convw, w, x, g, sc, dxo], aliases={2: 1}, carry=carry)


def wgrad(a, b, *, nblk, a_blocked, b_blocked, bk, bn, tt, name, carry=None):
    t = a.shape[1] if a.ndim == 3 else a.shape[0]
    nt = t // tt

    def body(a_ref, b_ref, o_ref, acc_ref):
        s = pl.program_id(1)

        @pl.when(s == 0)
        def _():
            acc_ref[...] = jnp.zeros_like(acc_ref)

        av = a_ref[0] if a.ndim == 3 else a_ref[...]
        bv = b_ref[0] if b.ndim == 3 else b_ref[...]
        acc_ref[...] += _dot_tn(av, bv)

        @pl.when(s == nt - 1)
        def _():
            o_ref[0] = acc_ref[...].astype(BF)

    def spec(arr, blocked, width):
        if arr.ndim == 3:
            return pl.BlockSpec((1, tt, width), lambda j, s: (j, s, 0))
        if blocked:
            return pl.BlockSpec((tt, width), lambda j, s: (s, j))
        return pl.BlockSpec((tt, width), lambda j, s: (s, 0))

    return _call(
        body, name=name, grid=(nblk, nt), in_specs=[spec(a, a_blocked, bk), spec(b, b_blocked, bn)],
        out_specs=[pl.BlockSpec((1, bk, bn), lambda j, s: (j, 0, 0))],
        out_shape=[jax.ShapeDtypeStruct((nblk, bk, bn), BF)],
        scratch_shapes=[pltpu.VMEM((bk, bn), F32)], args=[a, b], carry=carry)


def mod_partial(c_all, w_mod):
    cols = w_mod.shape[2]

    def body(c_ref, w_ref, o_ref):
        c = c_ref[...]
        ca = c * _sigmoid(c)
        o_ref[0] = jnp.dot(ca, w_ref[0], preferred_element_type=F32, precision=lax.Precision.HIGHEST)

    return pl.pallas_call(
        body, name="mod_partial", grid=(DEPTH,),
        in_specs=[pl.BlockSpec((NDEV, D), lambda l: (0, 0)), pl.BlockSpec((1, D, cols), lambda l: (l, 0, 0))],
        out_specs=pl.BlockSpec((1, NDEV, cols), lambda l: (l, 0, 0)),
        out_shape=jax.ShapeDtypeStruct((DEPTH, NDEV, cols), F32), compiler_params=_params(1))(c_all, w_mod)


def mod_finish(parts, b_mod):
    cols = parts.shape[2]

    def body(p_ref, b_ref, o_ref):
        for e in range(NDEV):
            o_ref[:, e * cols:(e + 1) * cols] = p_ref[e] + b_ref[:, e * cols:(e + 1) * cols]

    return pl.pallas_call(
        body, name="mod_finish", out_shape=jax.ShapeDtypeStruct((DEPTH, NDEV * cols), F32))(parts, b_mod)


def sum_parts(parts):
    n, r, c = parts.shape

    def body(p_ref, o_ref):
        acc = p_ref[0]
        for j in range(1, n):
            acc = acc + p_ref[j]
        o_ref[...] = acc

    return pl.pallas_call(body, name="sum_parts", out_shape=jax.ShapeDtypeStruct((r, c), F32))(parts)


def mod_wgrad(c_all_t, gmod_cols):
    cols = gmod_cols.shape[2]

    def body(c_ref, g_ref, o_ref):
        c = c_ref[...]
        ca = c * _sigmoid(c)
        acc = ca[:, 0:1] * g_ref[0, 0:1, :]
        for b in range(1, NDEV):
            acc = acc + ca[:, b:b + 1] * g_ref[0, b:b + 1, :]
        o_ref[0] = acc

    return pl.pallas_call(
        body, name="mod_wgrad", grid=(DEPTH,),
        in_specs=[pl.BlockSpec((D, NDEV), lambda l: (0, 0)), pl.BlockSpec((1, NDEV, cols), lambda l: (l, 0, 0))],
        out_specs=pl.BlockSpec((1, D, cols), lambda l: (l, 0, 0)),
        out_shape=jax.ShapeDtypeStruct((DEPTH, D, cols), F32), compiler_params=_params(1))(c_all_t, gmod_cols)


def _adamw_math(g, w, m, v):
    m2 = B1 * m + (1.0 - B1) * g
    v2 = B2 * v + (1.0 - B2) * (g * g)
    m_hat = m2 / (1.0 - B1 ** STEP)
    v_hat = v2 / (1.0 - B2 ** STEP)
    delta = -LR * (m_hat / (jnp.sqrt(v_hat) + ADAM_EPS) + WD * w)
    return delta, m2, v2


def _row_tile(r, c, budget=1 << 18):
    if r * c <= budget or r % 8:
        return r
    best = 8
    for cand in range(8, r + 1, 8):
        if r % cand == 0 and cand * c <= budget:
            best = cand
    return best


def adamw_sum(parts, w, m, v, *, name):
    n, r, c = parts.shape
    tr = _row_tile(r, c)

    def body(p_ref, w_ref, m_ref, v_ref, g_ref, d_ref, m2_ref, v2_ref):
        g = p_ref[0].astype(F32)
        for j in range(1, n):
            g = g + p_ref[j].astype(F32)
        d, m2, v2 = _adamw_math(g, w_ref[...], m_ref[...], v_ref[...])
        g_ref[...] = g
        d_ref[...] = d
        m2_ref[...] = m2
        v2_ref[...] = v2

    blk = pl.BlockSpec((tr, c), lambda i: (i, 0))
    out = jax.ShapeDtypeStruct((r, c), F32)
    return pl.pallas_call(
        body, name=name, grid=(r // tr,), in_specs=[pl.BlockSpec((n, tr, c), lambda i: (0, i, 0)), blk, blk, blk],
        out_specs=[blk] * 4, out_shape=[out] * 4, compiler_params=_params(1))(parts, w, m, v)


def adamw_layer(parts, w, m, v, prev, layer, *, name):
    n, r, c = parts.shape
    nl = w.shape[0]
    tr = _row_tile(r, c)

    def body(p_ref, w_ref, m_ref, v_ref, *rest):
        g_ref, d_ref, m2_ref, v2_ref = rest[-4:]
        g = p_ref[0].astype(F32)
        for j in range(1, n):
            g = g + p_ref[j].astype(F32)
        d, m2, v2 = _adamw_math(g, w_ref[0], m_ref[0], v_ref[0])
        g_ref[0] = g
        d_ref[0] = d
        m2_ref[0] = m2
        v2_ref[0] = v2

    blk = pl.BlockSpec((1, tr, c), lambda i: (layer, i, 0))
    in_specs = [pl.BlockSpec((n, tr, c), lambda i: (0, i, 0)), blk, blk, blk]
    args = [parts, w, m, v]
    aliases = {}
    if prev is not None:
        in_specs += [ANY] * 4
        args += list(prev)
        aliases = {4 + k: k for k in range(4)}
    out = jax.ShapeDtypeStruct((nl, r, c), F32)
    return pl.pallas_call(
        body, name=name, grid=(r // tr,), in_specs=in_specs, out_specs=[blk] * 4, out_shape=[out] * 4,
        input_output_aliases=aliases, compiler_params=_params(1))(*args)


def _pack(arrays):
    flat, layout, off = [], [], 0
    for a in arrays:
        flat.append(a.reshape(-1))
        layout.append((off, a.shape))
        off += a.size
    pad = (-off) % 1024
    if pad:
        flat.append(jnp.zeros((pad,), F32))
    return jnp.concatenate(flat).reshape(-1, 128), layout


def _unpack(packed, layout, lead=()):
    flat = packed.reshape(lead + (-1,))
    return [flat[..., off:off + _size(shape)].reshape(lead + tuple(shape)) for off, shape in layout]


def _size(shape):
    n = 1
    for s in shape:
        n *= s
    return n


def _join_last(g):
    g = jnp.moveaxis(g, 0, -2)
    return g.reshape(g.shape[:-2] + (g.shape[-2] * g.shape[-1],))


def _my_cols(a, width):
    return lax.dynamic_slice_in_dim(a, _my_id() * width, width, axis=a.ndim - 1)


def _tile(t, pref):
    return min(pref, t)


def kernel(x, c, w_mod, b_mod, norm_g, sc_w_in, sc_conv, sc_w_out, pool_w, pool_b, pool_scale, cf_w_pw1, cf_b_pw1, cf_w_dw, cf_b_dw, cf_ln_g, cf_ln_b, cf_w_pw2, cf_b_pw2, ffn_w_up, ffn_conv, ffn_b_conv, ffn_w_down, loss_target, m_w_mod, m_b_mod, m_norm_g, m_sc_w_in, m_sc_conv, m_sc_w_out, m_pool_w, m_pool_b, m_pool_scale, m_cf_w_pw1, m_cf_b_pw1, m_cf_w_dw, m_cf_b_dw, m_cf_ln_g, m_cf_ln_b, m_cf_w_pw2, m_cf_b_pw2, m_ffn_w_up, m_ffn_conv, m_ffn_b_conv, m_ffn_w_down, v_w_mod, v_b_mod, v_norm_g, v_sc_w_in, v_sc_conv, v_sc_w_out, v_pool_w, v_pool_b, v_pool_scale, v_cf_w_pw1, v_cf_b_pw1, v_cf_w_dw, v_cf_b_dw, v_cf_ln_g, v_cf_ln_b, v_cf_w_pw2, v_cf_b_pw2, v_ffn_w_up, v_ffn_conv, v_ffn_b_conv, v_ffn_w_down):
    env = dict(locals())
    names = ["w_mod", "b_mod", "norm_g", "sc_w_in", "sc_conv", "sc_w_out", "pool_w", "pool_b", "pool_scale", "cf_w_pw1",
             "cf_b_pw1", "cf_w_dw", "cf_b_dw", "cf_ln_g", "cf_ln_b", "cf_w_pw2", "cf_b_pw2", "ffn_w_up", "ffn_conv",
             "ffn_b_conv", "ffn_w_down"]
    t = x.shape[1]
    tm = _tile(t, 512)
    tt = _tile(t, 4096)
    x0, target = x[0], loss_target[0]

    small_names = ["norm_g", "sc_conv", "cf_b_pw1", "cf_w_dw", "cf_b_dw", "cf_ln_g", "cf_ln_b", "cf_b_pw2", "ffn_conv"]
    packed, layout = _pack([c] + [env[n] for n in small_names])

    shard = {"pool": pool_w[0].astype(BF), "pw1": cf_w_pw1[0].astype(BF), "pw2": cf_w_pw2[0].astype(BF)}
    for j in range(2):
        shard[f"in{j}"], shard[f"out{j}"] = sc_w_in[j].astype(BF), sc_w_out[j].astype(BF)
    for l in range(DEPTH):
        shard[f"up{l}"], shard[f"down{l}"] = ffn_w_up[l].T.astype(BF), ffn_w_down[l].astype(BF)
    gathered, g_in0, g_out0 = _Exchange("gather", [packed, shard["in0"], shard["out0"]]).run("gather_first")
    wg = {"in0": g_in0, "out0": g_out0}
    parts = _unpack(gathered, layout, lead=(NDEV,))
    c_all = parts[0].reshape(NDEV, D)
    full = {n: _join_last(p) for n, p in zip(small_names, parts[1:])}
    fwd_plan = {("mix_in", 0): ["up0"], ("mix_out", 0): ["down0"], ("ffn_in", 0): ["pool", "up1"],
                ("ffn_out", 0): ["down1", "pw1", "pw2"], ("ffn_in", 1): ["up2"], ("ffn_out", 1): ["down2", "in1", "out1"],
                ("ffn_in", 2): ["up3"], ("ffn_out", 2): ["down3"]}

    def carrying(plan, kind, store, source, fn, key, *a, **k):
        names = plan.get(key)
        if not names:
            return fn(*a, **k)
        res = fn(*a, carry=_Exchange(kind, [source[n] for n in names]), **k)
        store.update(zip(names, res[-1]))
        return res[:-1]

    fwd = functools.partial(carrying, fwd_plan, "gather", wg, shard)

    mp = mod_partial(c_all, w_mod)
    (mod_parts,) = _Exchange("scatter", [jnp.swapaxes(mp, 0, 1)]).run("exchange_mod")
    mod = mod_finish(mod_parts, b_mod)

    def vec(a):
        return a.reshape(1, -1)

    def col_blocks(g):
        w = jnp.swapaxes(g, 0, 1).reshape(D, -1)
        return jnp.swapaxes(w.reshape(D, -1, D), 0, 1)

    def ffn_blocks(a):
        return jnp.swapaxes(a.reshape(a.shape[0], 4, FB), 0, 1)

    saved = []
    xs = x0
    for l in range(DEPTH):
        sh1, sc1, g1, sh2, sc2, g2 = [mod[l:l + 1, k * D:(k + 1) * D] for k in range(6)]
        ng = [full["norm_g"][l, k:k + 1] for k in range(4)]
        kind, j = l % 3, l // 3
        s = dict(x_in=xs, sc1=sc1, g1=g1, sc2=sc2, g2=g2, ng=ng)
        if kind == 0:
            s["w_in"] = col_blocks(wg[f"in{j}"])
            s["h"], s["p"] = fwd(fwd_in, ("mix_in", l), xs, ng[0], sc1, sh1, s["w_in"], None, blocked=False, tm=tm,
                                 name=f"sc_in_{l}")
            x1, s["m"], s["q"] = fwd(sc_fwd_out, ("mix_out", l), s["p"], full["sc_conv"][j], wg[f"out{j}"].reshape(D, D), xs,
                                     ng[1], g1, tm=tm, name=f"sc_out_{l}")
        elif kind == 1:
            pool_w_f = jnp.swapaxes(wg["pool"], 0, 1).reshape(4, PG, PG)
            x1, s["m"], s["ypre"], s["pooled"] = pool_fwd(xs, ng[0], sc1, sh1, pool_w_f, pool_b, pool_scale, ng[1], g1,
                                                          tm=tm, name=f"pool_{l}")
        else:
            s["w_in"] = col_blocks(wg["pw1"])
            s["h"], s["a"] = fwd_in(xs, ng[0], sc1, sh1, s["w_in"], full["cf_b_pw1"].reshape(2, 1, D), blocked=False, tm=tm,
                                    name=f"cf_in_{l}")
            x1, s["m"], s["s"], s["u2"] = cf_fwd_out(s["a"], full["cf_w_dw"][0], full["cf_b_dw"], full["cf_ln_g"],
                                                     full["cf_ln_b"], wg["pw2"].reshape(D, D), full["cf_b_pw2"], xs, ng[1],
                                                     g1, tm=tm, name=f"cf_out_{l}")
        s["x1"] = x1
        s["cw"] = ffn_blocks(full["ffn_conv"][l])
        s["h2"], s["up"] = fwd(fwd_in, ("ffn_in", l), x1, ng[2], sc2, sh2, wg[f"up{l}"], None, blocked=True, wt=True, tm=tm,
                               name=f"ffn_in_{l}")
        xs, s["f"], s["gc"], s["fa"], *loss_part = fwd(
            ffn_fwd_out, ("ffn_out", l), s["up"], s["cw"], ffn_blocks(ffn_b_conv[l:l + 1]), wg[f"down{l}"].reshape(F, D), x1,
            ng[3], g2, tm=tm, name=f"ffn_out_{l}", target=target if l == DEPTH - 1 else None)
        saved.append(s)

    dx = xs
    loss = lax.psum(loss_part[0][0, 0], ("x", "y", "c"))

    gmod = [None] * DEPTH
    d_norm_g = [None] * DEPTH
    d_ffn_conv = [None] * DEPTH
    d_ffn_b_conv = [None] * DEPTH
    d_sc_conv = [None] * 2
    big = {}
    got = {}
    small_g = {}
    bwd_plan = {("mix_bout", 3): ["down3"], ("mix_bin", 3): ["up3"], ("ffn_bout", 2): ["in1", "out1"],
                ("mix_bmid", 2): ["up2", "down2"], ("ffn_bout", 1): ["pw1", "pw2"], ("ffn_bout", 0): ["pool", "down1"],
                ("ffn_bin", 0): ["up1"], ("mix_bout", 0): ["down0"], ("mix_win", 0): ["up0"], ("mix_wout", 0): ["in0"]}
    bwd = functools.partial(carrying, bwd_plan, "scatter", got, big)
    pool_w_f = jnp.swapaxes(wg["pool"], 0, 1).reshape(4, PG, PG)
    for l in reversed(range(DEPTH)):
        s = saved[l]
        ng = s["ng"]
        kind, j = l % 3, l // 3
        df, dgc, dup, st_o, st_b = bwd(ffn_bwd_out, ("ffn_bout", l), dx, s["f"], ng[3], s["g2"], wg[f"down{l}"].reshape(F, D),
                                       s["gc"], s["up"], tm=tm, name=f"ffn_bout_{l}")
        dx1, dup, st_i, st_c = bwd(ffn_bwd_in, ("ffn_bin", l), dgc, dup, s["up"], s["cw"], wg[f"up{l}"], s["x1"], ng[2],
                                   s["sc2"], dx, tm=tm, name=f"ffn_bin_{l}")
        (big[f"up{l}"],) = wgrad(dup, s["h2"], nblk=NDEV, a_blocked=True, b_blocked=False, bk=FB, bn=D, tt=tt,
                                 name=f"ffn_wup_{l}")
        big[f"down{l}"] = wgrad(s["fa"], df, nblk=4, a_blocked=True, b_blocked=False, bk=FB, bn=D, tt=tt,
                                name=f"ffn_wdown_{l}")[0].reshape(NDEV, F // NDEV, D)
        d_ffn_b_conv[l] = st_b[:, 0, :].reshape(F)
        d_ffn_conv[l] = jnp.swapaxes(st_c[:, 1:4, :], 0, 1).reshape(3, F)
        g_ffn = [st_i[0], st_i[1], st_o[0]]
        dn3, dn2 = st_o[1], st_i[2]
        if kind == 0:
            dm, dbg, du, st_o = bwd(sc_bwd_out, ("mix_bout", l), dx1, s["m"], ng[1], s["g1"], wg[f"out{j}"].reshape(D, D), s["p"],
                                    full["sc_conv"][j], tm=tm, name=f"sc_bout_{l}")
            dx, dp, st_i, st_c = bwd(sc_bwd_in, ("mix_bin", l), du, dbg, s["p"], full["sc_conv"][j], s["w_in"], s["x_in"], ng[0],
                                     s["sc1"], dx1, tm=tm, name=f"sc_bin_{l}")
            (big[f"in{j}"],) = bwd(wgrad, ("mix_win", l), s["h"], dp, nblk=NDEV, a_blocked=False, b_blocked=True, bk=D,
                                   bn=3 * D // NDEV, tt=tt, name=f"sc_win_{l}")
            big[f"out{j}"] = bwd(wgrad, ("mix_wout", l), s["q"], dm, nblk=1, a_blocked=False, b_blocked=False, bk=D, bn=D, tt=tt,
                                 name=f"sc_wout_{l}")[0].reshape(NDEV, D // NDEV, D)
            d_sc_conv[j] = st_c[0:3]
        elif kind == 1:
            dx, dyp, st_o = pool_bwd(dx1, s["m"], s["ypre"], pool_w_f, pool_scale, ng[1], s["g1"], s["x_in"], ng[0], s["sc1"],
                                     tm=tm, name=f"pool_b_{l}")
            st_i = st_o[4:7]
            (dpw,) = wgrad(s["pooled"], dyp, nblk=4, a_blocked=True, b_blocked=True, bk=PG, bn=PG, tt=tt, name=f"pool_w_{l}")
            big["pool"] = jnp.swapaxes(dpw.reshape(4, NDEV, PG // NDEV, PG), 0, 1).reshape(NDEV, 4 * PG // NDEV, PG)
            small_g["pool_scale"], small_g["pool_b"] = st_o[2:3], st_o[3:4]
        else:
            dm, ds, st_o = bwd_out(dx1, s["m"], ng[1], s["g1"], wg["pw2"].reshape(D, D), tm=tm,
                                   name=f"cf_bout_{l}")
            dA, st_c = bwd(cf_bwd_mid, ("mix_bmid", l), ds, s["u2"], s["a"], full["cf_w_dw"][0], full["cf_ln_g"],
                           full["cf_ln_b"], tm=tm, name=f"cf_bmid_{l}")
            dx, st_i = bwd_in(dA, s["w_in"], s["x_in"], ng[0], s["sc1"], dx1, tm=tm, name=f"cf_bin_{l}")
            (big["pw1"],) = wgrad(s["h"], dA, nblk=NDEV, a_blocked=False, b_blocked=True, bk=D, bn=2 * D // NDEV, tt=tt,
                                  name=f"cf_wpw1_{l}")
            big["pw2"] = wgrad(s["s"], dm, nblk=1, a_blocked=False, b_blocked=False, bk=D, bn=D, tt=tt,
                               name=f"cf_wpw2_{l}")[0].reshape(NDEV, D // NDEV, D)
            small_g["cf_w_dw"] = st_c[0:CFW][None]
            small_g["cf_b_dw"], small_g["cf_ln_g"], small_g["cf_ln_b"] = st_c[31:32], st_c[32:33], st_c[33:34]
            small_g["cf_b_pw1"] = st_c[34:36].reshape(1, 2 * D)
            small_g["cf_b_pw2"] = st_o[2:3]
        gmod[l] = jnp.concatenate([st_i[0], st_i[1], st_o[0]] + g_ffn)
        d_norm_g[l] = jnp.stack([st_i[2], st_o[1], dn2, dn3])

    small_g["gmod"] = jnp.stack(gmod)
    small_g["norm_g"] = jnp.stack(d_norm_g)
    small_g["sc_conv"] = jnp.stack(d_sc_conv)
    small_g["ffn_conv"] = jnp.stack(d_ffn_conv)
    small_g["ffn_b_conv"] = jnp.stack(d_ffn_b_conv)
    sg_names = ["gmod", "norm_g", "sc_conv", "pool_b", "pool_scale", "cf_b_pw1", "cf_w_dw", "cf_b_dw", "cf_ln_g", "cf_ln_b",
                "cf_b_pw2", "ffn_conv", "ffn_b_conv"]
    gpacked, glayout = _pack([small_g[n] for n in sg_names])
    (ggath,), (got["out0"],) = _run_exchanges(
        [_Exchange("gather", [gpacked]), _Exchange("scatter", [big["out0"]])], "exchange_last")
    gsum = dict(zip(sg_names, _unpack(sum_parts(ggath), glayout)))
    gmod_all = _unpack(ggath, glayout[:1], lead=(NDEV,))[0]
    grads = {"b_mod": gsum["gmod"], "pool_b": gsum["pool_b"], "pool_scale": gsum["pool_scale"],
             "ffn_b_conv": gsum["ffn_b_conv"]}
    for n in ["norm_g", "sc_conv", "cf_b_pw1", "cf_w_dw", "cf_b_dw", "cf_ln_g", "cf_ln_b", "cf_b_pw2", "ffn_conv"]:
        grads[n] = _my_cols(gsum[n], env[n].shape[-1])
    grads["w_mod"] = mod_wgrad(c_all.T, jnp.swapaxes(_my_cols(gmod_all, w_mod.shape[2]), 0, 1))

    deltas, new_m, new_v = {}, {}, {}
    sp_names = ["b_mod", "norm_g", "sc_conv", "pool_b", "pool_scale", "cf_b_pw1", "cf_w_dw", "cf_b_dw", "cf_ln_g", "cf_ln_b",
                "cf_b_pw2", "ffn_conv", "ffn_b_conv"]
    pg, playout = _pack([grads[n] for n in sp_names])
    pw_, _ = _pack([env[n] for n in sp_names])
    pm_, _ = _pack([env["m_" + n] for n in sp_names])
    pv_, _ = _pack([env["v_" + n] for n in sp_names])
    _, sd, sm, sv = adamw_sum(pg[None], pw_, pm_, pv_, name="adamw_small")
    for n, d_, m_, v_ in zip(sp_names, _unpack(sd, playout), _unpack(sm, playout), _unpack(sv, playout)):
        deltas[n], new_m[n], new_v[n] = d_, m_, v_
    gw = grads["w_mod"].reshape(1, DEPTH * D, -1)
    _, d_, m_, v_ = adamw_sum(gw, w_mod.reshape(gw.shape[1:]), m_w_mod.reshape(gw.shape[1:]), v_w_mod.reshape(gw.shape[1:]),
                              name="adamw_w_mod")
    deltas["w_mod"], new_m["w_mod"], new_v["w_mod"] = [a.reshape(w_mod.shape) for a in (d_, m_, v_)]

    groups = {"sc_w_in": ["in0", "in1"], "sc_w_out": ["out0", "out1"], "pool_w": ["pool"], "cf_w_pw1": ["pw1"],
              "cf_w_pw2": ["pw2"], "ffn_w_up": [f"up{l}" for l in range(DEPTH)], "ffn_w_down": [f"down{l}" for l in range(DEPTH)]}
    for n, layers in groups.items():
        stacked = (len(layers),) + got[layers[0]].shape[1:]
        flip = n == "ffn_w_up"
        w3 = [(jnp.swapaxes(env[p + n], 1, 2) if flip else env[p + n]).reshape(stacked) for p in ("", "m_", "v_")]
        outs = None
        for li, key in enumerate(layers):
            outs = adamw_layer(got[key], *w3, outs, li, name=f"adamw_{n}_{li}")
        grads[n], deltas[n], new_m[n], new_v[n] = [(jnp.swapaxes(a, 1, 2) if flip else a).reshape(env[n].shape) for a in outs]

    return (loss, dx[None], *[grads[n] for n in names], *[deltas[n] for n in names], *[new_m[n] for n in names],
            *[new_v[n] for n in names])
```

```python
import functools

import jax
import jax.numpy as jnp
from jax import lax
from jax.experimental import pallas as pl
from jax.experimental.pallas import tpu as pltpu

D = 1024
F = 2816
NDEV = 8
FB = F // 4
DEPTH = 4
RMS_EPS = 1e-6
LN_EPS = 1e-5
CFW = 31
POOL_WINDOWS = (2, 4, 8, 16)
PG = D // 4
LR, B1, B2, ADAM_EPS, WD, STEP = 0.001, 0.9, 0.999, 1e-08, 0.01, 10

BF = jnp.bfloat16
F32 = jnp.float32
VMEM_LIMIT_V7X = 56 * 1024 * 1024
MXU_COLS_V7X = 256
MESH = pl.DeviceIdType.MESH
ANY = pl.BlockSpec(memory_space=pl.ANY)


def _params(n_axes):
    return pltpu.CompilerParams(dimension_semantics=("arbitrary",) * n_axes, vmem_limit_bytes=VMEM_LIMIT_V7X)


def _const(shape, single=True):
    nd = len(shape)
    if single:
        return pl.BlockSpec(shape, lambda *_: (0,) * nd, pipeline_mode=pl.Buffered(1))
    return pl.BlockSpec(shape, lambda *_: (0,) * nd)


def _rows(tm, c):
    return pl.BlockSpec((tm, c), lambda i: (i, 0))


def _brows(nb, tm, c, b0=0):
    return pl.BlockSpec((nb, tm, c), lambda i: (b0, i, 0))


def _prev(hb, c, tm):
    return pl.BlockSpec((hb, c), lambda i: (jnp.maximum(i * (tm // hb) - 1, 0), 0))


def _next(hb, c, tm, t):
    return pl.BlockSpec((hb, c), lambda i: (jnp.minimum((i + 1) * (tm // hb), t // hb - 1), 0))


def _bprev(nb, hb, c, tm, b0=0):
    return pl.BlockSpec((nb, hb, c), lambda i: (b0, jnp.maximum(i * (tm // hb) - 1, 0), 0))


def _bnext(nb, hb, c, tm, t, b0=0):
    return pl.BlockSpec((nb, hb, c), lambda i: (b0, jnp.minimum((i + 1) * (tm // hb), t // hb - 1), 0))


def _sigmoid(v):
    return 0.5 * jnp.tanh(0.5 * v) + 0.5


def _fold8(v):
    r, c = v.shape
    return jnp.sum(v.reshape(r // 8, 8, c), axis=0)


def _chunks(n_rows, rc, step, init=0, reverse=False):
    n = n_rows // rc

    def it(c, carry):
        idx = (n - 1 - c) if reverse else c
        return step(pl.multiple_of(idx * rc, rc), carry)

    return lax.fori_loop(0, n, it, init)


def _row_shifted_copies(s_ref, n):
    for b in range(1, 8):
        s_ref[b, 0:n, :] = s_ref[0, pl.ds(b, n), :]


def _shifted(s_ref, o, tm):
    return s_ref[o % 8, pl.ds(8 * (o // 8), tm), :]


def _dot(a, b):
    return jnp.dot(a, b, preferred_element_type=F32)


def _dot_nt(a, b):
    return lax.dot_general(a, b, (((1,), (1,)), ((), ())), preferred_element_type=F32)


def _dot_tn(a, b):
    return lax.dot_general(a, b, (((0,), (0,)), ((), ())), preferred_element_type=F32)


def _rsum(v):
    return jnp.sum(v, axis=0, keepdims=True)


def _adaln(x, g, sc, sh):
    r = lax.rsqrt(jnp.mean(x * x, axis=-1, keepdims=True) + RMS_EPS)
    return (x * r * g) * (1.0 + sc) + sh


def _gated_res(x, m, gn, gt):
    r = lax.rsqrt(jnp.mean(m * m, axis=-1, keepdims=True) + RMS_EPS)
    return x + gt * (m * r * gn)


def _gated_res_bwd(dxo, m, gn, gt):
    r = lax.rsqrt(jnp.mean(m * m, axis=-1, keepdims=True) + RMS_EPS)
    mh = m * r
    dgt = _rsum(dxo * (mh * gn))
    dn = dxo * gt
    dgn = _rsum(dn * mh)
    dmh = dn * gn
    dm = r * (dmh - mh * jnp.mean(dmh * mh, axis=-1, keepdims=True))
    return dm, dgt, dgn


def _my_id():
    return 4 * lax.axis_index("x") + 2 * lax.axis_index("y") + lax.axis_index("c")


def _peer(k):
    x, y, c = lax.axis_index("x"), lax.axis_index("y"), lax.axis_index("c")
    px = 1 - x if k & 4 else x
    py = 1 - y if k & 2 else y
    pc = 1 - c if k & 1 else c
    return (px, py, pc), 4 * px + 2 * py + pc


class _Exchange:
    def __init__(self, kind, arrays):
        self.gather = kind == "gather"
        self.arrays = list(arrays)
        n = len(self.arrays)
        if self.gather:
            self.out_shape = [jax.ShapeDtypeStruct((NDEV,) + a.shape, a.dtype) for a in self.arrays]
        else:
            self.out_shape = [jax.ShapeDtypeStruct(a.shape, a.dtype) for a in self.arrays]
        self.scratch = [pltpu.SemaphoreType.DMA((n * NDEV,)), pltpu.SemaphoreType.DMA((n * NDEV,)),
                        pltpu.SemaphoreType.DMA((n,))]

    def _local(self, a, src, dst, sems):
        me = _my_id()
        return pltpu.make_async_copy(src[a] if self.gather else src[a].at[me], dst[a].at[me], sems[2].at[a])

    def _remote(self, a, k, src, dst, sems, incoming):
        to, pid = _peer(k)
        me = _my_id()
        return pltpu.make_async_remote_copy(
            src_ref=src[a] if self.gather else src[a].at[pid], dst_ref=dst[a].at[pid if incoming else me],
            send_sem=sems[0].at[a * NDEV + k], recv_sem=sems[1].at[a * NDEV + k], device_id=to, device_id_type=MESH)

    def start(self, src, dst, sems):
        for a in range(len(self.arrays)):
            self._local(a, src, dst, sems).start()
        for k in range(1, NDEV):
            for a in range(len(self.arrays)):
                self._remote(a, k, src, dst, sems, False).start()

    def wait(self, src, dst, sems):
        for k in range(1, NDEV):
            for a in range(len(self.arrays)):
                self._remote(a, k, src, dst, sems, True).wait_recv()
        for k in range(1, NDEV):
            for a in range(len(self.arrays)):
                self._remote(a, k, src, dst, sems, False).wait_send()
        for a in range(len(self.arrays)):
            self._local(a, src, dst, sems).wait()

    def run(self, name):
        n = len(self.arrays)

        def body(*refs):
            src, dst, sems = refs[:n], refs[n:2 * n], refs[2 * n:]
            self.start(src, dst, sems)
            self.wait(src, dst, sems)

        return pl.pallas_call(body, name=name, in_specs=[ANY] * n, out_specs=[ANY] * n, out_shape=self.out_shape,
                              scratch_shapes=self.scratch)(*self.arrays)


def _run_exchanges(exchanges, name):
    counts = [len(e.arrays) for e in exchanges]
    n = sum(counts)

    def body(*refs):
        src, dst, sems = refs[:n], refs[n:2 * n], refs[2 * n:]
        parts, lo = [], 0
        for ei, (e, c) in enumerate(zip(exchanges, counts)):
            parts.append((e, src[lo:lo + c], dst[lo:lo + c], sems[3 * ei:3 * ei + 3]))
            lo += c
        for e, s, d, m in parts:
            e.start(s, d, m)
        for e, s, d, m in parts:
            e.wait(s, d, m)

    res = pl.pallas_call(
        body, name=name, in_specs=[ANY] * n, out_specs=[ANY] * n, out_shape=[s for e in exchanges for s in e.out_shape],
        scratch_shapes=[s for e in exchanges for s in e.scratch])(*[a for e in exchanges for a in e.arrays])
    out, lo = [], 0
    for c in counts:
        out.append(list(res[lo:lo + c]))
        lo += c
    return out


def _call(body, *, name, grid, in_specs, out_specs, out_shape, args, scratch_shapes=(), carry=None, aliases=None):
    cp = _params(len(grid))
    aliases = aliases or {}
    if carry is None:
        return tuple(pl.pallas_call(body, name=name, grid=grid, in_specs=in_specs, out_specs=out_specs, out_shape=out_shape,
                                    scratch_shapes=list(scratch_shapes), input_output_aliases=aliases,
                                    compiler_params=cp)(*args))
    n_in, n_out, n_sc, n_c = len(in_specs), len(out_specs), len(scratch_shapes), len(carry.arrays)

    def wrapped(*refs):
        ins, src = refs[:n_in], refs[n_in:n_in + n_c]
        outs = refs[n_in + n_c:n_in + n_c + n_out]
        dst = refs[n_in + n_c + n_out:n_in + 2 * n_c + n_out]
        rest = refs[n_in + 2 * n_c + n_out:]
        scr, sems = rest[:n_sc], rest[n_sc:]
        first = pl.program_id(0) == 0
        last = pl.program_id(0) == grid[0] - 1
        for ax in range(1, len(grid)):
            first = jnp.logical_and(first, pl.program_id(ax) == 0)
            last = jnp.logical_and(last, pl.program_id(ax) == grid[ax] - 1)

        @pl.when(first)
        def _():
            carry.start(src, dst, sems)

        body(*ins, *outs, *scr)

        @pl.when(last)
        def _():
            carry.wait(src, dst, sems)

    res = pl.pallas_call(
        wrapped, name=name, grid=grid, in_specs=list(in_specs) + [ANY] * n_c, out_specs=list(out_specs) + [ANY] * n_c,
        out_shape=list(out_shape) + carry.out_shape, scratch_shapes=list(scratch_shapes) + carry.scratch,
        input_output_aliases=aliases, compiler_params=cp)(*args, *carry.arrays)
    return tuple(res[:n_out]) + (list(res[n_out:]),)


def fwd_in(x, g, sc, sh, w, bias, *, blocked, tm, name, carry=None, wt=False):
    t = x.shape[0]
    nb, bw = (w.shape[0], w.shape[1]) if wt else (w.shape[0], w.shape[2])
    per = next(k for k in (1, 2, 4, 8) if (k * bw) % MXU_COLS_V7X == 0)
    assert not wt or (blocked and bias is None and nb % per == 0)

    def body(*refs):
        if bias is None:
            x_ref, g_ref, sc_ref, sh_ref, w_ref, h_ref, p_ref = refs
        else:
            x_ref, g_ref, sc_ref, sh_ref, w_ref, b_ref, h_ref, p_ref = refs
        hb = _adaln(x_ref[...], g_ref[...], sc_ref[...], sh_ref[...]).astype(BF)
        h_ref[...] = hb
        if wt:
            for c in range(nb // per):
                y = _dot_nt(hb, w_ref[c * per:(c + 1) * per].reshape(per * bw, D))
                for d in range(per):
                    p_ref[c * per + d] = y[:, d * bw:(d + 1) * bw].astype(BF)
            return
        for d in range(nb):
            y = _dot(hb, w_ref[d])
            if bias is not None:
                y = y + b_ref[d]
            if blocked:
                p_ref[d] = y.astype(BF)
            else:
                p_ref[:, d * bw:(d + 1) * bw] = y.astype(BF)

    vec = _const((1, D))
    in_specs = [_rows(tm, D), vec, vec, vec, _const(w.shape)]
    args = [x, g, sc, sh, w]
    if bias is not None:
        in_specs.append(_const((nb, 1, bw)))
        args.append(bias)
    if blocked:
        p_spec, p_shape = _brows(nb, tm, bw), jax.ShapeDtypeStruct((nb, t, bw), BF)
    else:
        p_spec, p_shape = _rows(tm, nb * bw), jax.ShapeDtypeStruct((t, nb * bw), BF)
    return _call(body, name=name, grid=(t // tm,), in_specs=in_specs, out_specs=[_rows(tm, D), p_spec],
                 out_shape=[jax.ShapeDtypeStruct((t, D), BF), p_shape], args=args, carry=carry)


def _sc_conv(p_ref, ph_ref, cw_ref, first, tm):
    z = p_ref[:, D:2 * D].astype(F32) * p_ref[:, 2 * D:3 * D].astype(F32)
    zp = jnp.where(first, 0.0, ph_ref[8:16, D:2 * D].astype(F32) * ph_ref[8:16, 2 * D:3 * D].astype(F32))
    ext = jnp.concatenate([zp, z], axis=0)
    return cw_ref[0:1, :] * ext[6:6 + tm] + cw_ref[1:2, :] * ext[7:7 + tm] + cw_ref[2:3, :] * z


def sc_fwd_out(p, convw, w_out, x, gn, gt, *, tm, name, carry=None):
    t = x.shape[0]

    def body(p_ref, ph_ref, cw_ref, w_ref, x_ref, gn_ref, gt_ref, x1_ref, m_ref, q_ref):
        u = _sc_conv(p_ref, ph_ref, cw_ref, pl.program_id(0) == 0, tm)
        qb = (p_ref[:, 0:D].astype(F32) * u).astype(BF)
        q_ref[...] = qb
        m = _dot(qb, w_ref[...])
        m_ref[...] = m.astype(BF)
        x1_ref[...] = _gated_res(x_ref[...], m, gn_ref[...], gt_ref[...])

    vec = _const((1, D))
    return _call(
        body, name=name, grid=(t // tm,),
        in_specs=[_rows(tm, 3 * D), _prev(16, 3 * D, tm), _const((3, D)), _const((D, D)), _rows(tm, D), vec, vec],
        out_specs=[_rows(tm, D)] * 3,
        out_shape=[jax.ShapeDtypeStruct((t, D), F32), jax.ShapeDtypeStruct((t, D), BF), jax.ShapeDtypeStruct((t, D), BF)],
        args=[p, p, convw, w_out, x, gn, gt], carry=carry)


def _layernorm_parts(u2):
    mu = jnp.mean(u2, axis=-1, keepdims=True)
    cen = u2 - mu
    rstd = lax.rsqrt(jnp.mean(cen * cen, axis=-1, keepdims=True) + LN_EPS)
    return cen * rstd, rstd


def cf_fwd_out(a, w_dw, b_dw, ln_g, ln_b, w_pw2, b_pw2, x, gn, gt, *, tm, name):
    t = x.shape[0]
    hb = 32

    def body(a_ref, ah_ref, wd_ref, bd_ref, lg_ref, lb_ref, w_ref, b2_ref, x_ref, gn_ref, gt_ref,
             x1_ref, m_ref, s_out_ref, u2_ref, s_ref):
        i = pl.program_id(0)
        uh = ah_ref[:, 0:D].astype(F32) * _sigmoid(ah_ref[:, D:2 * D].astype(F32))
        s_ref[0, 0:hb, :] = jnp.where(i == 0, 0.0, uh)
        s_ref[0, hb:hb + tm, :] = a_ref[:, 0:D].astype(F32) * _sigmoid(a_ref[:, D:2 * D].astype(F32))
        _row_shifted_copies(s_ref, tm + hb - 8)
        acc = bd_ref[...] + wd_ref[0:1, :] * _shifted(s_ref, hb - CFW + 1, tm)
        for k in range(1, CFW):
            acc = acc + wd_ref[k:k + 1, :] * _shifted(s_ref, hb - CFW + 1 + k, tm)
        u2_ref[...] = acc.astype(BF)
        xh, _ = _layernorm_parts(acc)
        l = xh * lg_ref[...] + lb_ref[...]
        sb = (l * _sigmoid(l)).astype(BF)
        s_out_ref[...] = sb
        m = _dot(sb, w_ref[...]) + b2_ref[...]
        m_ref[...] = m.astype(BF)
        x1_ref[...] = _gated_res(x_ref[...], m, gn_ref[...], gt_ref[...])

    vec = _const((1, D))
    return pl.pallas_call(
        body, name=name, grid=(t // tm,),
        in_specs=[_rows(tm, 2 * D), _prev(hb, 2 * D, tm), _const((CFW, D)), vec, vec, vec, _const((D, D)), vec,
                  _rows(tm, D), vec, vec],
        out_specs=[_rows(tm, D)] * 4,
        out_shape=[jax.ShapeDtypeStruct((t, D), F32)] + [jax.ShapeDtypeStruct((t, D), BF)] * 3,
        scratch_shapes=[pltpu.VMEM((8, tm + hb, D), F32)], compiler_params=_params(1),
    )(a, a, w_dw, b_dw, ln_g, ln_b, w_pw2, b_pw2, x, gn, gt)


def _pool_counts(i, tm, w):
    row = lax.broadcasted_iota(jnp.int32, (tm, 1), 0) + i * tm
    return jnp.minimum(row + 1, w).astype(F32)


def pool_fwd(x, g, sc, sh, pw, pb, pscale, gn, gt, *, tm, name):
    t = x.shape[0]
    pad, hb = 8, 16
    base = pad + hb

    def body(x_ref, xh_ref, g_ref, sc_ref, sh_ref, pw_ref, pb_ref, ps_ref, gn_ref, gt_ref,
             x1_ref, m_ref, yp_ref, po_ref, sa_ref, sb_ref):
        i = pl.program_id(0)
        hh = _adaln(xh_ref[...], g_ref[...], sc_ref[...], sh_ref[...])
        h = _adaln(x_ref[...], g_ref[...], sc_ref[...], sh_ref[...])
        zero = jnp.zeros((pad, D), F32)
        sa_ref[0:pad, :] = zero
        sb_ref[0:pad, :] = zero
        sa_ref[pad:base, :] = jnp.where(i == 0, 0.0, hh)
        sa_ref[base:base + tm, :] = h
        n = hb + tm
        src, dst = sa_ref, sb_ref
        ys = []
        for gi, w in enumerate(POOL_WINDOWS):
            c0 = gi * PG
            step = w // 2
            dst[pl.ds(pad, n), c0:D] = src[pl.ds(pad, n), c0:D] + src[pl.ds(pad - step, n), c0:D]
            mean = dst[pl.ds(base, tm), c0:c0 + PG] / _pool_counts(i, tm, w)
            pooled = (mean - h[:, c0:c0 + PG]).astype(BF)
            po_ref[:, c0:c0 + PG] = pooled
            ys.append(_dot(pooled, pw_ref[gi]))
            src, dst = dst, src
        ypre = jnp.concatenate(ys, axis=1) + pb_ref[...]
        yp_ref[...] = ypre.astype(BF)
        m = ypre * ps_ref[...]
        m_ref[...] = m.astype(BF)
        x1_ref[...] = _gated_res(x_ref[...], m, gn_ref[...], gt_ref[...])

    vec = _const((1, D))
    return pl.pallas_call(
        body, name=name, grid=(t // tm,),
        in_specs=[_rows(tm, D), _prev(hb, D, tm), vec, vec, vec, _const((4, PG, PG)), vec, vec, vec, vec],
        out_specs=[_rows(tm, D)] * 4,
        out_shape=[jax.ShapeDtypeStruct((t, D), F32)] + [jax.ShapeDtypeStruct((t, D), BF)] * 3,
        scratch_shapes=[pltpu.VMEM((tm + base, D), F32)] * 2, compiler_params=_params(1),
    )(x, x, g, sc, sh, pw, pb, pscale, gn, gt)


def ffn_fwd_out(up, convw, convb, w_down, x, gn, gt, *, tm, name, carry=None, target=None):
    t = x.shape[0]
    nt = t // tm

    def body(gate_ref, gh_ref, val_ref, cw_ref, cb_ref, w_ref, x_ref, gn_ref, gt_ref, *rest):
        if target is None:
            x2_ref, f_ref, gc_ref, a_ref = rest
        else:
            tg_ref, x2_ref, f_ref, gc_ref, a_ref, l_ref, lacc_ref = rest
        i = pl.program_id(0)
        acc = jnp.zeros((tm, D), F32)
        for j in range(4):
            gate = gate_ref[j].astype(F32)
            ext = jnp.concatenate([jnp.where(i == 0, 0.0, gh_ref[j, 8:16, :].astype(F32)), gate], axis=0)
            gc = cb_ref[j] + cw_ref[j, 0:1, :] * ext[6:6 + tm] + cw_ref[j, 1:2, :] * ext[7:7 + tm] + cw_ref[j, 2:3, :] * gate
            gc_ref[j] = gc
            ab = (gc * _sigmoid(gc) * val_ref[j].astype(F32)).astype(BF)
            a_ref[j] = ab
            acc = acc + _dot(ab, w_ref[j * FB:(j + 1) * FB, :])
        f_ref[...] = acc.astype(BF)
        x2 = _gated_res(x_ref[...], acc, gn_ref[...], gt_ref[...])
        if target is None:
            x2_ref[...] = x2
        else:
            @pl.when(i == 0)
            def _():
                lacc_ref[...] = jnp.zeros_like(lacc_ref)

            e = x2 - tg_ref[...]
            x2_ref[...] = e * (1.0 / D)
            lacc_ref[...] += _rsum(e * e)

            @pl.when(i == nt - 1)
            def _():
                l_ref[...] = jnp.sum(lacc_ref[...], axis=1, keepdims=True) * (0.5 / D)

    vec = _const((1, D))
    blk = jax.ShapeDtypeStruct((4, t, FB), BF)
    in_specs = [_brows(4, tm, FB, 0), _bprev(4, 16, FB, tm, 0), _brows(4, tm, FB, 1), _const((4, 3, FB)),
                _const((4, 1, FB)), _const((F, D)), _rows(tm, D), vec, vec]
    out_specs = [_rows(tm, D), _rows(tm, D), _brows(4, tm, FB), _brows(4, tm, FB)]
    out_shape = [jax.ShapeDtypeStruct((t, D), F32), jax.ShapeDtypeStruct((t, D), BF), jax.ShapeDtypeStruct((4, t, FB), F32), blk]
    args = [up, up, up, convw, convb, w_down, x, gn, gt]
    scratch = []
    if target is not None:
        in_specs.append(_rows(tm, D))
        args.append(target)
        out_specs.append(pl.BlockSpec((1, 1), lambda i: (0, 0)))
        out_shape.append(jax.ShapeDtypeStruct((1, 1), F32))
        scratch.append(pltpu.VMEM((1, D), F32))
    return _call(body, name=name, grid=(nt,), in_specs=in_specs, out_specs=out_specs, out_shape=out_shape, args=args,
                 scratch_shapes=scratch, carry=carry)


def _init_stats(ref):
    @pl.when(pl.program_id(0) == 0)
    def _():
        ref[...] = jnp.zeros_like(ref)


def bwd_out(dxo, m, gn, gt, w, *, tm, name, carry=None):
    t = dxo.shape[0]
    k = w.shape[0]

    def body(dx_ref, m_ref, gn_ref, gt_ref, w_ref, dm_ref, da_ref, st_ref):
        _init_stats(st_ref)
        dm, dgt, dgn = _gated_res_bwd(dx_ref[...], m_ref[...].astype(F32), gn_ref[...], gt_ref[...])
        st_ref[0:1, :] += dgt
        st_ref[1:2, :] += dgn
        st_ref[2:3, :] += _rsum(dm)
        dmb = dm.astype(BF)
        dm_ref[...] = dmb
        da_ref[...] = _dot_nt(dmb, w_ref[...]).astype(BF)

    vec = _const((1, D))
    da_spec, da_shape = _rows(tm, k), jax.ShapeDtypeStruct((t, k), BF)
    return _call(
        body, name=name, grid=(t // tm,), in_specs=[_rows(tm, D), _rows(tm, D), vec, vec, _const((k, D))],
        out_specs=[_rows(tm, D), da_spec, _const((8, D), single=False)],
        out_shape=[jax.ShapeDtypeStruct((t, D), BF), da_shape, jax.ShapeDtypeStruct((8, D), F32)],
        args=[dxo, m, gn, gt, w], carry=carry)


def _adaln_bwd(dh, x_ref, g_ref, sc_ref, dxo_ref, st_ref, row=0):
    x = x_ref[...]
    r = lax.rsqrt(jnp.mean(x * x, axis=-1, keepdims=True) + RMS_EPS)
    xh = x * r
    gv = g_ref[...]
    st_ref[row:row + 1, :] += _rsum(dh)
    st_ref[row + 1:row + 2, :] += _rsum(dh * (xh * gv))
    dn = dh * (1.0 + sc_ref[...])
    st_ref[row + 2:row + 3, :] += _rsum(dn * xh)
    dy = dn * gv
    return dxo_ref[...] + r * (dy - xh * jnp.mean(dy * xh, axis=-1, keepdims=True))


def bwd_in(dp, w, x, g, sc, dxo, *, tm, name):
    t = x.shape[0]
    nb, _, bw = w.shape

    def body(dp_ref, w_ref, x_ref, g_ref, sc_ref, dxo_ref, dx_ref, st_ref):
        _init_stats(st_ref)
        dh = jnp.zeros((tm, D), F32)
        for d in range(nb):
            dh = dh + _dot_nt(dp_ref[:, d * bw:(d + 1) * bw], w_ref[d])
        dx_ref[...] = _adaln_bwd(dh, x_ref, g_ref, sc_ref, dxo_ref, st_ref)

    vec = _const((1, D))
    return pl.pallas_call(
        body, name=name, grid=(t // tm,),
        in_specs=[_rows(tm, nb * bw), _const(w.shape), _rows(tm, D), vec, vec, _rows(tm, D)],
        out_specs=[_rows(tm, D), _const((8, D), single=False)],
        out_shape=[jax.ShapeDtypeStruct((t, D), F32), jax.ShapeDtypeStruct((8, D), F32)],
        compiler_params=_params(1))(dp, w, x, g, sc, dxo)


def sc_bwd_out(dxo, m, gn, gt, w_out, p, convw, *, tm, name, carry=None):
    t = dxo.shape[0]

    def body(dx_ref, m_ref, gn_ref, gt_ref, w_ref, p_ref, ph_ref, cw_ref, dm_ref, dbg_ref, du_ref, st_ref):
        _init_stats(st_ref)
        dm, dgt, dgn = _gated_res_bwd(dx_ref[...], m_ref[...].astype(F32), gn_ref[...], gt_ref[...])
        st_ref[0:1, :] += dgt
        st_ref[1:2, :] += dgn
        dmb = dm.astype(BF)
        dm_ref[...] = dmb
        dq = _dot_nt(dmb, w_ref[...])
        dbg_ref[...] = (dq * _sc_conv(p_ref, ph_ref, cw_ref, pl.program_id(0) == 0, tm)).astype(BF)
        du_ref[...] = (dq * p_ref[:, 0:D].astype(F32)).astype(BF)

    vec = _const((1, D))
    out = jax.ShapeDtypeStruct((t, D), BF)
    return _call(
        body, name=name, grid=(t // tm,),
        in_specs=[_rows(tm, D), _rows(tm, D), vec, vec, _const((D, D)), _rows(tm, 3 * D), _prev(16, 3 * D, tm), _const((3, D))],
        out_specs=[_rows(tm, D)] * 3 + [_const((8, D), single=False)], out_shape=[out, out, out, jax.ShapeDtypeStruct((8, D), F32)],
        args=[dxo, m, gn, gt, w_out, p, p, convw], carry=carry)


def sc_bwd_in(du, dbg, p, convw, w, x, g, sc, dxo, *, tm, name, carry=None):
    t = x.shape[0]

    def body(du_ref, dun_ref, dbg_ref, p_ref, cw_ref, w_ref, x_ref, g_ref, sc_ref, dxo_ref, dx_ref, dp_ref, st_ref, sc2_ref):
        last = pl.program_id(0) == pl.num_programs(0) - 1
        _init_stats(st_ref)
        _init_stats(sc2_ref)
        du = du_ref[...].astype(F32)
        ext = jnp.concatenate([du, jnp.where(last, 0.0, dun_ref[0:8, :].astype(F32))], axis=0)
        e1, e2 = ext[1:tm + 1], ext[2:tm + 2]
        dz = cw_ref[2:3, :] * du + cw_ref[1:2, :] * e1 + cw_ref[0:1, :] * e2
        cg, hi = p_ref[:, D:2 * D].astype(F32), p_ref[:, 2 * D:3 * D].astype(F32)
        dbg, dcg, dhi = dbg_ref[...], (dz * hi).astype(BF), (dz * cg).astype(BF)
        dp_ref[:, 0:D] = dbg
        dp_ref[:, D:2 * D] = dcg
        dp_ref[:, 2 * D:3 * D] = dhi
        dh = _dot_nt(dbg, w_ref[0]) + _dot_nt(dcg, w_ref[1]) + _dot_nt(dhi, w_ref[2])
        z = cg * hi
        sc2_ref[0:1, :] += _rsum(z * e2)
        sc2_ref[1:2, :] += _rsum(z * e1)
        sc2_ref[2:3, :] += _rsum(z * du)
        dx_ref[...] = _adaln_bwd(dh, x_ref, g_ref, sc_ref, dxo_ref, st_ref)

    vec = _const((1, D))
    stat = jax.ShapeDtypeStruct((8, D), F32)
    return _call(
        body, name=name, grid=(t // tm,),
        in_specs=[_rows(tm, D), _next(16, D, tm, t), _rows(tm, D), _rows(tm, 3 * D), _const((3, D)), _const(w.shape),
                  _rows(tm, D), vec, vec, _rows(tm, D)],
        out_specs=[_rows(tm, D), _rows(tm, 3 * D), _const((8, D), single=False), _const((8, D), single=False)],
        out_shape=[jax.ShapeDtypeStruct((t, D), F32), jax.ShapeDtypeStruct((t, 3 * D), BF), stat, stat],
        args=[du, du, dbg, p, convw, w, x, g, sc, dxo], carry=carry)


def cf_bwd_mid(ds, u2, a, w_dw, ln_g, ln_b, *, tm, name, carry=None):
    t = ds.shape[0]
    hb = 32

    def du2_of(dsv, u2v, lg, lb):
        xh, rstd = _layernorm_parts(u2v)
        l = xh * lg + lb
        sg = _sigmoid(l)
        dl = dsv * (sg * (1.0 + l * (1.0 - sg)))
        dxh = dl * lg
        du2 = rstd * (dxh - jnp.mean(dxh, axis=-1, keepdims=True) - xh * jnp.mean(dxh * xh, axis=-1, keepdims=True))
        return du2, dl, xh

    rc, cc = 32, 256

    def body(ds_ref, dsn_ref, u2_ref, u2n_ref, a_ref, wd_ref, lg_ref, lb_ref, da_ref, st_ref, s1_ref, acc_ref):
        i = pl.program_id(0)
        last = i == pl.num_programs(0) - 1
        _init_stats(st_ref)
        _init_stats(acc_ref)
        lg, lb = lg_ref[...], lb_ref[...]
        du2, dl, xh = du2_of(ds_ref[...].astype(F32), u2_ref[...].astype(F32), lg, lb)
        st_ref[32:33, :] += _rsum(dl * xh)
        st_ref[33:34, :] += _rsum(dl)
        st_ref[31:32, :] += _rsum(du2)
        du2n, _, _ = du2_of(dsn_ref[...].astype(F32), u2n_ref[...].astype(F32), lg, lb)
        s1_ref[0, 0:tm, :] = du2
        s1_ref[0, tm:tm + hb, :] = jnp.where(last, 0.0, du2n)
        _row_shifted_copies(s1_ref, tm + hb - 8)

        def taps(r0, _):
            rows = pl.ds(r0, rc)
            for c0 in range(0, D, cc):
                sg = _sigmoid(a_ref[rows, D + c0:D + c0 + cc].astype(F32))
                u = a_ref[rows, c0:c0 + cc].astype(F32) * sg
                du = jnp.zeros((rc, cc), F32)
                for k in range(CFW):
                    o = CFW - 1 - k
                    sh = s1_ref[o % 8, pl.ds(pl.multiple_of(r0 + 8 * (o // 8), 8), rc), c0:c0 + cc]
                    du = du + wd_ref[k:k + 1, c0:c0 + cc] * sh
                    acc_ref[8 * k:8 * k + 8, c0:c0 + cc] += _fold8(u * sh)
                dav = du * sg
                dgv = du * u * (1.0 - sg)
                acc_ref[8 * CFW:8 * CFW + 8, c0:c0 + cc] += _fold8(dav)
                acc_ref[8 * CFW + 8:8 * CFW + 16, c0:c0 + cc] += _fold8(dgv)
                da_ref[rows, c0:c0 + cc] = dav.astype(BF)
                da_ref[rows, D + c0:D + c0 + cc] = dgv.astype(BF)
            return 0

        _chunks(tm, rc, taps)

        @pl.when(last)
        def _():
            for k in range(CFW):
                st_ref[k:k + 1, :] = jnp.sum(acc_ref[8 * k:8 * k + 8, :], axis=0, keepdims=True)
            st_ref[34:35, :] = jnp.sum(acc_ref[8 * CFW:8 * CFW + 8, :], axis=0, keepdims=True)
            st_ref[35:36, :] = jnp.sum(acc_ref[8 * CFW + 8:8 * CFW + 16, :], axis=0, keepdims=True)

    vec = _const((1, D))
    return _call(
        body, name=name, grid=(t // tm,),
        in_specs=[_rows(tm, D), _next(hb, D, tm, t), _rows(tm, D), _next(hb, D, tm, t), _rows(tm, 2 * D),
                  _const((CFW, D)), vec, vec],
        out_specs=[_rows(tm, 2 * D), _const((40, D), single=False)],
        out_shape=[jax.ShapeDtypeStruct((t, 2 * D), BF), jax.ShapeDtypeStruct((40, D), F32)],
        scratch_shapes=[pltpu.VMEM((8, tm + hb, D), F32), pltpu.VMEM((8 * (CFW + 2), D), F32)],
        args=[ds, ds, u2, u2, a, w_dw, ln_g, ln_b], carry=carry)


def pool_bwd(dxo, m, ypre, pw, pscale, gn, gt, x, g, sc, *, tm, name):
    t = dxo.shape[0]
    hb = 16

    def dyp_of(dxv, mv, ypv, ps, gnv, gtv):
        dm, dgt, dgn = _gated_res_bwd(dxv, mv, gnv, gtv)
        return dm * ps, dgt, dgn, _rsum(dm * ypv)

    def body(dx_ref, dxn_ref, m_ref, mn_ref, yp_ref, ypn_ref, pw_ref, ps_ref, gn_ref, gt_ref, x_ref, g_ref, sc_ref,
             dxi_ref, dyp_ref, st_ref, sa_ref, sb_ref):
        i = pl.program_id(0)
        last = i == pl.num_programs(0) - 1
        _init_stats(st_ref)
        ps, gnv, gtv = ps_ref[...], gn_ref[...], gt_ref[...]
        dyp, dgt, dgn, dps = dyp_of(dx_ref[...], m_ref[...].astype(F32), yp_ref[...].astype(F32), ps, gnv, gtv)
        st_ref[0:1, :] += dgt
        st_ref[1:2, :] += dgn
        st_ref[2:3, :] += dps
        st_ref[3:4, :] += _rsum(dyp)
        dypb = dyp.astype(BF)
        dyp_ref[...] = dypb
        dypn, _, _, _ = dyp_of(dxn_ref[...], mn_ref[...].astype(F32), ypn_ref[...].astype(F32), ps, gnv, gtv)
        dypnb = jnp.where(last, 0.0, dypn).astype(BF)
        dpo = []
        for gi, w in enumerate(POOL_WINDOWS):
            c0 = gi * PG
            dp_main = _dot_nt(dypb[:, c0:c0 + PG], pw_ref[gi])
            dp_next = _dot_nt(dypnb[:, c0:c0 + PG], pw_ref[gi])
            dpo.append(dp_main)
            sa_ref[0:tm, c0:c0 + PG] = dp_main / _pool_counts(i, tm, w)
            sa_ref[tm:tm + hb, c0:c0 + PG] = dp_next / float(w)
        zero = jnp.zeros((8, D), F32)
        sa_ref[tm + hb:tm + hb + 8, :] = zero
        sb_ref[tm + hb:tm + hb + 8, :] = zero
        n = tm + hb
        src, dst = sa_ref, sb_ref
        dhs = []
        for gi, w in enumerate(POOL_WINDOWS):
            c0 = gi * PG
            step = w // 2
            dst[pl.ds(0, n), c0:D] = src[pl.ds(0, n), c0:D] + src[pl.ds(step, n), c0:D]
            dhs.append(dst[pl.ds(0, tm), c0:c0 + PG] - dpo[gi])
            src, dst = dst, src
        dxi_ref[...] = _adaln_bwd(jnp.concatenate(dhs, axis=1), x_ref, g_ref, sc_ref, dx_ref, st_ref, row=4)

    vec = _const((1, D))
    return pl.pallas_call(
        body, name=name, grid=(t // tm,),
        in_specs=[_rows(tm, D), _next(hb, D, tm, t), _rows(tm, D), _next(hb, D, tm, t), _rows(tm, D),
                  _next(hb, D, tm, t), _const((4, PG, PG)), vec, vec, vec, _rows(tm, D), vec, vec],
        out_specs=[_rows(tm, D), _rows(tm, D), _const((8, D), single=False)],
        out_shape=[jax.ShapeDtypeStruct((t, D), F32), jax.ShapeDtypeStruct((t, D), BF), jax.ShapeDtypeStruct((8, D), F32)],
        scratch_shapes=[pltpu.VMEM((tm + hb + 8, D), F32)] * 2, compiler_params=_params(1),
    )(dxo, dxo, m, m, ypre, ypre, pw, pscale, gn, gt, x, g, sc)


def ffn_bwd_out(dxo, f, gn, gt, w_down, gc, up, *, tm, name, carry=None):
    t = dxo.shape[0]

    def body(dx_ref, f_ref, gn_ref, gt_ref, w_ref, gc_ref, val_ref, df_ref, dgc_ref, dval_ref, st_ref, sc_ref):
        _init_stats(st_ref)
        _init_stats(sc_ref)
        dm, dgt, dgn = _gated_res_bwd(dx_ref[...], f_ref[...].astype(F32), gn_ref[...], gt_ref[...])
        st_ref[0:1, :] += dgt
        st_ref[1:2, :] += dgn
        dmb = dm.astype(BF)
        df_ref[...] = dmb
        da_all = _dot_nt(dmb, w_ref[...])
        for j in range(4):
            da = da_all[:, j * FB:(j + 1) * FB]
            gcv = gc_ref[j].astype(F32)
            sg = _sigmoid(gcv)
            dval_ref[j] = (da * (gcv * sg)).astype(BF)
            dgc = da * val_ref[j].astype(F32) * (sg * (1.0 + gcv * (1.0 - sg)))
            dgc_ref[j] = dgc.astype(BF)
            sc_ref[j, 0:1, :] += _rsum(dgc)

    vec = _const((1, D))
    return _call(
        body, name=name, grid=(t // tm,),
        in_specs=[_rows(tm, D), _rows(tm, D), vec, vec, _const((F, D)), _brows(4, tm, FB), _brows(4, tm, FB, 1)],
        out_specs=[_rows(tm, D), _brows(4, tm, FB), _brows(4, tm, FB, 1), _const((8, D), single=False),
                   _const((4, 8, FB), single=False)],
        out_shape=[jax.ShapeDtypeStruct((t, D), BF), jax.ShapeDtypeStruct((4, t, FB), BF), jax.ShapeDtypeStruct((8, t, FB), BF),
                   jax.ShapeDtypeStruct((8, D), F32), jax.ShapeDtypeStruct((4, 8, FB), F32)],
        args=[dxo, f, gn, gt, w_down, gc, up], carry=carry)


def ffn_bwd_in(dgc, dup, up, convw, w, x, g, sc, dxo, *, tm, name, carry=None):
    t = x.shape[0]

    def body(dgc_ref, dgcn_ref, dval_ref, gate_ref, cw_ref, w_ref, x_ref, g_ref, sc_ref, dxo_ref,
             dx_ref, dgate_ref, st_ref, sc2_ref):
        last = pl.program_id(0) == pl.num_programs(0) - 1
        _init_stats(st_ref)
        _init_stats(sc2_ref)
        dh = jnp.zeros((tm, D), F32)
        for j in range(4):
            dgc = dgc_ref[j].astype(F32)
            ext = jnp.concatenate([dgc, jnp.where(last, 0.0, dgcn_ref[j, 0:8, :].astype(F32))], axis=0)
            e1, e2 = ext[1:tm + 1], ext[2:tm + 2]
            dgate = (cw_ref[j, 2:3, :] * dgc + cw_ref[j, 1:2, :] * e1 + cw_ref[j, 0:1, :] * e2).astype(BF)
            dgate_ref[j] = dgate
            dh = dh + _dot(dgate, w_ref[j])
            gate = gate_ref[j].astype(F32)
            sc2_ref[j, 1:2, :] += _rsum(gate * e2)
            sc2_ref[j, 2:3, :] += _rsum(gate * e1)
            sc2_ref[j, 3:4, :] += _rsum(gate * dgc)
        for j in range(4):
            dh = dh + _dot(dval_ref[j], w_ref[4 + j])
        dx_ref[...] = _adaln_bwd(dh, x_ref, g_ref, sc_ref, dxo_ref, st_ref)

    vec = _const((1, D))
    return _call(
        body, name=name, grid=(t // tm,),
        in_specs=[_brows(4, tm, FB), _bnext(4, 16, FB, tm, t), _brows(4, tm, FB, 1), _brows(4, tm, FB, 0), _const((4, 3, FB)),
                  _const(w.shape), _rows(tm, D), vec, vec, _rows(tm, D)],
        out_specs=[_rows(tm, D), _brows(4, tm, FB, 0), _const((8, D), single=False), _const((4, 8, FB), single=False)],
        out_shape=[jax.ShapeDtypeStruct((t, D), F32), jax.ShapeDtypeStruct((8, t, FB), BF), jax.ShapeDtypeStruct((8, D), F32),
                   jax.ShapeDtypeStruct((4, 8, FB), F32)],
        args=[dgc, dgc, dup, up, convw, w, x, g, sc, dxo], aliases={2: 1}, carry=carry)


def wgrad(a, b, *, nblk, a_blocked, b_blocked, bk, bn, tt, name, carry=None):
    t = a.shape[1] if a.ndim == 3 else a.shape[0]
    nt = t // tt

    def body(a_ref, b_ref, o_ref, acc_ref):
        s = pl.program_id(1)

        @pl.when(s == 0)
        def _():
            acc_ref[...] = jnp.zeros_like(acc_ref)

        av = a_ref[0] if a.ndim == 3 else a_ref[...]
        bv = b_ref[0] if b.ndim == 3 else b_ref[...]
        acc_ref[...] += _dot_tn(av, bv)

        @pl.when(s == nt - 1)
        def _():
            o_ref[0] = acc_ref[...].astype(BF)

    def spec(arr, blocked, width):
        if arr.ndim == 3:
            return pl.BlockSpec((1, tt, width), lambda j, s: (j, s, 0))
        if blocked:
            return pl.BlockSpec((tt, width), lambda j, s: (s, j))
        return pl.BlockSpec((tt, width), lambda j, s: (s, 0))

    return _call(
        body, name=name, grid=(nblk, nt), in_specs=[spec(a, a_blocked, bk), spec(b, b_blocked, bn)],
        out_specs=[pl.BlockSpec((1, bk, bn), lambda j, s: (j, 0, 0))],
        out_shape=[jax.ShapeDtypeStruct((nblk, bk, bn), BF)],
        scratch_shapes=[pltpu.VMEM((bk, bn), F32)], args=[a, b], carry=carry)


def mod_partial(c_all, w_mod):
    cols = w_mod.shape[2]

    def body(c_ref, w_ref, o_ref):
        c = c_ref[...]
        ca = c * _sigmoid(c)
        o_ref[0] = jnp.dot(ca, w_ref[0], preferred_element_type=F32, precision=lax.Precision.HIGHEST)

    return pl.pallas_call(
        body, name="mod_partial", grid=(DEPTH,),
        in_specs=[pl.BlockSpec((NDEV, D), lambda l: (0, 0)), pl.BlockSpec((1, D, cols), lambda l: (l, 0, 0))],
        out_specs=pl.BlockSpec((1, NDEV, cols), lambda l: (l, 0, 0)),
        out_shape=jax.ShapeDtypeStruct((DEPTH, NDEV, cols), F32), compiler_params=_params(1))(c_all, w_mod)


def mod_finish(parts, b_mod):
    cols = parts.shape[2]

    def body(p_ref, b_ref, o_ref):
        for e in range(NDEV):
            o_ref[:, e * cols:(e + 1) * cols] = p_ref[e] + b_ref[:, e * cols:(e + 1) * cols]

    return pl.pallas_call(
        body, name="mod_finish", out_shape=jax.ShapeDtypeStruct((DEPTH, NDEV * cols), F32))(parts, b_mod)


def sum_parts(parts):
    n, r, c = parts.shape

    def body(p_ref, o_ref):
        acc = p_ref[0]
        for j in range(1, n):
            acc = acc + p_ref[j]
        o_ref[...] = acc

    return pl.pallas_call(body, name="sum_parts", out_shape=jax.ShapeDtypeStruct((r, c), F32))(parts)


def mod_wgrad(c_all_t, gmod_cols):
    cols = gmod_cols.shape[2]

    def body(c_ref, g_ref, o_ref):
        c = c_ref[...]
        ca = c * _sigmoid(c)
        acc = ca[:, 0:1] * g_ref[0, 0:1, :]
        for b in range(1, NDEV):
            acc = acc + ca[:, b:b + 1] * g_ref[0, b:b + 1, :]
        o_ref[0] = acc

    return pl.pallas_call(
        body, name="mod_wgrad", grid=(DEPTH,),
        in_specs=[pl.BlockSpec((D, NDEV), lambda l: (0, 0)), pl.BlockSpec((1, NDEV, cols), lambda l: (l, 0, 0))],
        out_specs=pl.BlockSpec((1, D, cols), lambda l: (l, 0, 0)),
        out_shape=jax.ShapeDtypeStruct((DEPTH, D, cols), F32), compiler_params=_params(1))(c_all_t, gmod_cols)


def _adamw_math(g, w, m, v):
    m2 = B1 * m + (1.0 - B1) * g
    v2 = B2 * v + (1.0 - B2) * (g * g)
    m_hat = m2 / (1.0 - B1 ** STEP)
    v_hat = v2 / (1.0 - B2 ** STEP)
    delta = -LR * (m_hat / (jnp.sqrt(v_hat) + ADAM_EPS) + WD * w)
    return delta, m2, v2


def _row_tile(r, c, budget=1 << 18):
    if r * c <= budget or r % 8:
        return r
    best = 8
    for cand in range(8, r + 1, 8):
        if r % cand == 0 and cand * c <= budget:
            best = cand
    return best


def adamw_sum(parts, w, m, v, *, name):
    n, r, c = parts.shape
    tr = _row_tile(r, c)

    def body(p_ref, w_ref, m_ref, v_ref, g_ref, d_ref, m2_ref, v2_ref):
        g = p_ref[0].astype(F32)
        for j in range(1, n):
            g = g + p_ref[j].astype(F32)
        d, m2, v2 = _adamw_math(g, w_ref[...], m_ref[...], v_ref[...])
        g_ref[...] = g
        d_ref[...] = d
        m2_ref[...] = m2
        v2_ref[...] = v2

    blk = pl.BlockSpec((tr, c), lambda i: (i, 0))
    out = jax.ShapeDtypeStruct((r, c), F32)
    return pl.pallas_call(
        body, name=name, grid=(r // tr,), in_specs=[pl.BlockSpec((n, tr, c), lambda i: (0, i, 0)), blk, blk, blk],
        out_specs=[blk] * 4, out_shape=[out] * 4, compiler_params=_params(1))(parts, w, m, v)


def adamw_layer(parts, w, m, v, prev, layer, *, name):
    n, r, c = parts.shape
    nl = w.shape[0]
    tr = _row_tile(r, c)

    def body(p_ref, w_ref, m_ref, v_ref, *rest):
        g_ref, d_ref, m2_ref, v2_ref = rest[-4:]
        g = p_ref[0].astype(F32)
        for j in range(1, n):
            g = g + p_ref[j].astype(F32)
        d, m2, v2 = _adamw_math(g, w_ref[0], m_ref[0], v_ref[0])
        g_ref[0] = g
        d_ref[0] = d
        m2_ref[0] = m2
        v2_ref[0] = v2

    blk = pl.BlockSpec((1, tr, c), lambda i: (layer, i, 0))
    in_specs = [pl.BlockSpec((n, tr, c), lambda i: (0, i, 0)), blk, blk, blk]
    args = [parts, w, m, v]
    aliases = {}
    if prev is not None:
        in_specs += [ANY] * 4
        args += list(prev)
        aliases = {4 + k: k for k in range(4)}
    out = jax.ShapeDtypeStruct((nl, r, c), F32)
    return pl.pallas_call(
        body, name=name, grid=(r // tr,), in_specs=in_specs, out_specs=[blk] * 4, out_shape=[out] * 4,
        input_output_aliases=aliases, compiler_params=_params(1))(*args)


def _pack(arrays):
    flat, layout, off = [], [], 0
    for a in arrays:
        flat.append(a.reshape(-1))
        layout.append((off, a.shape))
        off += a.size
    pad = (-off) % 1024
    if pad:
        flat.append(jnp.zeros((pad,), F32))
    return jnp.concatenate(flat).reshape(-1, 128), layout


def _unpack(packed, layout, lead=()):
    flat = packed.reshape(lead + (-1,))
    return [flat[..., off:off + _size(shape)].reshape(lead + tuple(shape)) for off, shape in layout]


def _size(shape):
    n = 1
    for s in shape:
        n *= s
    return n


def _join_last(g):
    g = jnp.moveaxis(g, 0, -2)
    return g.reshape(g.shape[:-2] + (g.shape[-2] * g.shape[-1],))


def _my_cols(a, width):
    return lax.dynamic_slice_in_dim(a, _my_id() * width, width, axis=a.ndim - 1)


def _tile(t, pref):
    return min(pref, t)


def kernel(x, c, w_mod, b_mod, norm_g, sc_w_in, sc_conv, sc_w_out, pool_w, pool_b, pool_scale, cf_w_pw1, cf_b_pw1, cf_w_dw, cf_b_dw, cf_ln_g, cf_ln_b, cf_w_pw2, cf_b_pw2, ffn_w_up, ffn_conv, ffn_b_conv, ffn_w_down, loss_target, m_w_mod, m_b_mod, m_norm_g, m_sc_w_in, m_sc_conv, m_sc_w_out, m_pool_w, m_pool_b, m_pool_scale, m_cf_w_pw1, m_cf_b_pw1, m_cf_w_dw, m_cf_b_dw, m_cf_ln_g, m_cf_ln_b, m_cf_w_pw2, m_cf_b_pw2, m_ffn_w_up, m_ffn_conv, m_ffn_b_conv, m_ffn_w_down, v_w_mod, v_b_mod, v_norm_g, v_sc_w_in, v_sc_conv, v_sc_w_out, v_pool_w, v_pool_b, v_pool_scale, v_cf_w_pw1, v_cf_b_pw1, v_cf_w_dw, v_cf_b_dw, v_cf_ln_g, v_cf_ln_b, v_cf_w_pw2, v_cf_b_pw2, v_ffn_w_up, v_ffn_conv, v_ffn_b_conv, v_ffn_w_down):
    env = dict(locals())
    names = ["w_mod", "b_mod", "norm_g", "sc_w_in", "sc_conv", "sc_w_out", "pool_w", "pool_b", "pool_scale", "cf_w_pw1",
             "cf_b_pw1", "cf_w_dw", "cf_b_dw", "cf_ln_g", "cf_ln_b", "cf_w_pw2", "cf_b_pw2", "ffn_w_up", "ffn_conv",
             "ffn_b_conv", "ffn_w_down"]
    t = x.shape[1]
    tm = _tile(t, 512)
    tt = _tile(t, 4096)
    x0, target = x[0], loss_target[0]

    small_names = ["norm_g", "sc_conv", "cf_b_pw1", "cf_w_dw", "cf_b_dw", "cf_ln_g", "cf_ln_b", "cf_b_pw2", "ffn_conv"]
    packed, layout = _pack([c] + [env[n] for n in small_names])

    shard = {"pool": pool_w[0].astype(BF), "pw1": cf_w_pw1[0].astype(BF), "pw2": cf_w_pw2[0].astype(BF)}
    for j in range(2):
        shard[f"in{j}"], shard[f"out{j}"] = sc_w_in[j].astype(BF), sc_w_out[j].astype(BF)
    for l in range(DEPTH):
        shard[f"up{l}"], shard[f"down{l}"] = ffn_w_up[l].T.astype(BF), ffn_w_down[l].astype(BF)
    gathered, g_in0, g_out0 = _Exchange("gather", [packed, shard["in0"], shard["out0"]]).run("gather_first")
    wg = {"in0": g_in0, "out0": g_out0}
    parts = _unpack(gathered, layout, lead=(NDEV,))
    c_all = parts[0].reshape(NDEV, D)
    full = {n: _join_last(p) for n, p in zip(small_names, parts[1:])}
    fwd_plan = {("mix_in", 0): ["up0"], ("mix_out", 0): ["down0"], ("ffn_in", 0): ["pool", "up1"],
                ("ffn_out", 0): ["down1", "pw1", "pw2"], ("ffn_in", 1): ["up2"], ("ffn_out", 1): ["down2", "in1", "out1"],
                ("ffn_in", 2): ["up3"], ("ffn_out", 2): ["down3"]}

    def carrying(plan, kind, store, source, fn, key, *a, **k):
        names = plan.get(key)
        if not names:
            return fn(*a, **k)
        res = fn(*a, carry=_Exchange(kind, [source[n] for n in names]), **k)
        store.update(zip(names, res[-1]))
        return res[:-1]

    fwd = functools.partial(carrying, fwd_plan, "gather", wg, shard)

    mp = mod_partial(c_all, w_mod)
    (mod_parts,) = _Exchange("scatter", [jnp.swapaxes(mp, 0, 1)]).run("exchange_mod")
    mod = mod_finish(mod_parts, b_mod)

    def vec(a):
        return a.reshape(1, -1)

    def col_blocks(g):
        w = jnp.swapaxes(g, 0, 1).reshape(D, -1)
        return jnp.swapaxes(w.reshape(D, -1, D), 0, 1)

    def ffn_blocks(a):
        return jnp.swapaxes(a.reshape(a.shape[0], 4, FB), 0, 1)

    saved = []
    xs = x0
    for l in range(DEPTH):
        sh1, sc1, g1, sh2, sc2, g2 = [mod[l:l + 1, k * D:(k + 1) * D] for k in range(6)]
        ng = [full["norm_g"][l, k:k + 1] for k in range(4)]
        kind, j = l % 3, l // 3
        s = dict(x_in=xs, sc1=sc1, g1=g1, sc2=sc2, g2=g2, ng=ng)
        if kind == 0:
            s["w_in"] = col_blocks(wg[f"in{j}"])
            s["h"], s["p"] = fwd(fwd_in, ("mix_in", l), xs, ng[0], sc1, sh1, s["w_in"], None, blocked=False, tm=tm,
                                 name=f"sc_in_{l}")
            x1, s["m"], s["q"] = fwd(sc_fwd_out, ("mix_out", l), s["p"], full["sc_conv"][j], wg[f"out{j}"].reshape(D, D), xs,
                                     ng[1], g1, tm=tm, name=f"sc_out_{l}")
        elif kind == 1:
            pool_w_f = jnp.swapaxes(wg["pool"], 0, 1).reshape(4, PG, PG)
            x1, s["m"], s["ypre"], s["pooled"] = pool_fwd(xs, ng[0], sc1, sh1, pool_w_f, pool_b, pool_scale, ng[1], g1,
                                                          tm=tm, name=f"pool_{l}")
        else:
            s["w_in"] = col_blocks(wg["pw1"])
            s["h"], s["a"] = fwd_in(xs, ng[0], sc1, sh1, s["w_in"], full["cf_b_pw1"].reshape(2, 1, D), blocked=False, tm=tm,
                                    name=f"cf_in_{l}")
            x1, s["m"], s["s"], s["u2"] = cf_fwd_out(s["a"], full["cf_w_dw"][0], full["cf_b_dw"], full["cf_ln_g"],
                                                     full["cf_ln_b"], wg["pw2"].reshape(D, D), full["cf_b_pw2"], xs, ng[1],
                                                     g1, tm=tm, name=f"cf_out_{l}")
        s["x1"] = x1
        s["cw"] = ffn_blocks(full["ffn_conv"][l])
        s["h2"], s["up"] = fwd(fwd_in, ("ffn_in", l), x1, ng[2], sc2, sh2, wg[f"up{l}"], None, blocked=True, wt=True, tm=tm,
                               name=f"ffn_in_{l}")
        xs, s["f"], s["gc"], s["fa"], *loss_part = fwd(
            ffn_fwd_out, ("ffn_out", l), s["up"], s["cw"], ffn_blocks(ffn_b_conv[l:l + 1]), wg[f"down{l}"].reshape(F, D), x1,
            ng[3], g2, tm=tm, name=f"ffn_out_{l}", target=target if l == DEPTH - 1 else None)
        saved.append(s)

    dx = xs
    loss = lax.psum(loss_part[0][0, 0], ("x", "y", "c"))

    gmod = [None] * DEPTH
    d_norm_g = [None] * DEPTH
    d_ffn_conv = [None] * DEPTH
    d_ffn_b_conv = [None] * DEPTH
    d_sc_conv = [None] * 2
    big = {}
    got = {}
    small_g = {}
    bwd_plan = {("mix_bout", 3): ["down3"], ("mix_bin", 3): ["up3"], ("ffn_bout", 2): ["in1", "out1"],
                ("mix_bmid", 2): ["up2", "down2"], ("ffn_bout", 1): ["pw1", "pw2"], ("ffn_bout", 0): ["pool", "down1"],
                ("ffn_bin", 0): ["up1"], ("mix_bout", 0): ["down0"], ("mix_win", 0): ["up0"], ("mix_wout", 0): ["in0"]}
    bwd = functools.partial(carrying, bwd_plan, "scatter", got, big)
    pool_w_f = jnp.swapaxes(wg["pool"], 0, 1).reshape(4, PG, PG)
    for l in reversed(range(DEPTH)):
        s = saved[l]
        ng = s["ng"]
        kind, j = l % 3, l // 3
        df, dgc, dup, st_o, st_b = bwd(ffn_bwd_out, ("ffn_bout", l), dx, s["f"], ng[3], s["g2"], wg[f"down{l}"].reshape(F, D),
                                       s["gc"], s["up"], tm=tm, name=f"ffn_bout_{l}")
        dx1, dup, st_i, st_c = bwd(ffn_bwd_in, ("ffn_bin", l), dgc, dup, s["up"], s["cw"], wg[f"up{l}"], s["x1"], ng[2],
                                   s["sc2"], dx, tm=tm, name=f"ffn_bin_{l}")
        (big[f"up{l}"],) = wgrad(dup, s["h2"], nblk=NDEV, a_blocked=True, b_blocked=False, bk=FB, bn=D, tt=tt,
                                 name=f"ffn_wup_{l}")
        big[f"down{l}"] = wgrad(s["fa"], df, nblk=4, a_blocked=True, b_blocked=False, bk=FB, bn=D, tt=tt,
                                name=f"ffn_wdown_{l}")[0].reshape(NDEV, F // NDEV, D)
        d_ffn_b_conv[l] = st_b[:, 0, :].reshape(F)
        d_ffn_conv[l] = jnp.swapaxes(st_c[:, 1:4, :], 0, 1).reshape(3, F)
        g_ffn = [st_i[0], st_i[1], st_o[0]]
        dn3, dn2 = st_o[1], st_i[2]
        if kind == 0:
            dm, dbg, du, st_o = bwd(sc_bwd_out, ("mix_bout", l), dx1, s["m"], ng[1], s["g1"], wg[f"out{j}"].reshape(D, D), s["p"],
                                    full["sc_conv"][j], tm=tm, name=f"sc_bout_{l}")
            dx, dp, st_i, st_c = bwd(sc_bwd_in, ("mix_bin", l), du, dbg, s["p"], full["sc_conv"][j], s["w_in"], s["x_in"], ng[0],
                                     s["sc1"], dx1, tm=tm, name=f"sc_bin_{l}")
            (big[f"in{j}"],) = bwd(wgrad, ("mix_win", l), s["h"], dp, nblk=NDEV, a_blocked=False, b_blocked=True, bk=D,
                                   bn=3 * D // NDEV, tt=tt, name=f"sc_win_{l}")
            big[f"out{j}"] = bwd(wgrad, ("mix_wout", l), s["q"], dm, nblk=1, a_blocked=False, b_blocked=False, bk=D, bn=D, tt=tt,
                                 name=f"sc_wout_{l}")[0].reshape(NDEV, D // NDEV, D)
            d_sc_conv[j] = st_c[0:3]
        elif kind == 1:
            dx, dyp, st_o = pool_bwd(dx1, s["m"], s["ypre"], pool_w_f, pool_scale, ng[1], s["g1"], s["x_in"], ng[0], s["sc1"],
                                     tm=tm, name=f"pool_b_{l}")
            st_i = st_o[4:7]
            (dpw,) = wgrad(s["pooled"], dyp, nblk=4, a_blocked=True, b_blocked=True, bk=PG, bn=PG, tt=tt, name=f"pool_w_{l}")
            big["pool"] = jnp.swapaxes(dpw.reshape(4, NDEV, PG // NDEV, PG), 0, 1).reshape(NDEV, 4 * PG // NDEV, PG)
            small_g["pool_scale"], small_g["pool_b"] = st_o[2:3], st_o[3:4]
        else:
            dm, ds, st_o = bwd_out(dx1, s["m"], ng[1], s["g1"], wg["pw2"].reshape(D, D), tm=tm,
                                   name=f"cf_bout_{l}")
            dA, st_c = bwd(cf_bwd_mid, ("mix_bmid", l), ds, s["u2"], s["a"], full["cf_w_dw"][0], full["cf_ln_g"],
                           full["cf_ln_b"], tm=tm, name=f"cf_bmid_{l}")
            dx, st_i = bwd_in(dA, s["w_in"], s["x_in"], ng[0], s["sc1"], dx1, tm=tm, name=f"cf_bin_{l}")
            (big["pw1"],) = wgrad(s["h"], dA, nblk=NDEV, a_blocked=False, b_blocked=True, bk=D, bn=2 * D // NDEV, tt=tt,
                                  name=f"cf_wpw1_{l}")
            big["pw2"] = wgrad(s["s"], dm, nblk=1, a_blocked=False, b_blocked=False, bk=D, bn=D, tt=tt,
                               name=f"cf_wpw2_{l}")[0].reshape(NDEV, D // NDEV, D)
            small_g["cf_w_dw"] = st_c[0:CFW][None]
            small_g["cf_b_dw"], small_g["cf_ln_g"], small_g["cf_ln_b"] = st_c[31:32], st_c[32:33], st_c[33:34]
            small_g["cf_b_pw1"] = st_c[34:36].reshape(1, 2 * D)
            small_g["cf_b_pw2"] = st_o[2:3]
        gmod[l] = jnp.concatenate([st_i[0], st_i[1], st_o[0]] + g_ffn)
        d_norm_g[l] = jnp.stack([st_i[2], st_o[1], dn2, dn3])

    small_g["gmod"] = jnp.stack(gmod)
    small_g["norm_g"] = jnp.stack(d_norm_g)
    small_g["sc_conv"] = jnp.stack(d_sc_conv)
    small_g["ffn_conv"] = jnp.stack(d_ffn_conv)
    small_g["ffn_b_conv"] = jnp.stack(d_ffn_b_conv)
    sg_names = ["gmod", "norm_g", "sc_conv", "pool_b", "pool_scale", "cf_b_pw1", "cf_w_dw", "cf_b_dw", "cf_ln_g", "cf_ln_b",
                "cf_b_pw2", "ffn_conv", "ffn_b_conv"]
    gpacked, glayout = _pack([small_g[n] for n in sg_names])
    (ggath,), (got["out0"],) = _run_exchanges(
        [_Exchange("gather", [gpacked]), _Exchange("scatter", [big["out0"]])], "exchange_last")
    gsum = dict(zip(sg_names, _unpack(sum_parts(ggath), glayout)))
    gmod_all = _unpack(ggath, glayout[:1], lead=(NDEV,))[0]
    grads = {"b_mod": gsum["gmod"], "pool_b": gsum["pool_b"], "pool_scale": gsum["pool_scale"],
             "ffn_b_conv": gsum["ffn_b_conv"]}
    for n in ["norm_g", "sc_conv", "cf_b_pw1", "cf_w_dw", "cf_b_dw", "cf_ln_g", "cf_ln_b", "cf_b_pw2", "ffn_conv"]:
        grads[n] = _my_cols(gsum[n], env[n].shape[-1])
    grads["w_mod"] = mod_wgrad(c_all.T, jnp.swapaxes(_my_cols(gmod_all, w_mod.shape[2]), 0, 1))

    deltas, new_m, new_v = {}, {}, {}
    sp_names = ["b_mod", "norm_g", "sc_conv", "pool_b", "pool_scale", "cf_b_pw1", "cf_w_dw", "cf_b_dw", "cf_ln_g", "cf_ln_b",
                "cf_b_pw2", "ffn_conv", "ffn_b_conv"]
    pg, playout = _pack([grads[n] for n in sp_names])
    pw_, _ = _pack([env[n] for n in sp_names])
    pm_, _ = _pack([env["m_" + n] for n in sp_names])
    pv_, _ = _pack([env["v_" + n] for n in sp_names])
    _, sd, sm, sv = adamw_sum(pg[None], pw_, pm_, pv_, name="adamw_small")
    for n, d_, m_, v_ in zip(sp_names, _unpack(sd, playout), _unpack(sm, playout), _unpack(sv, playout)):
        deltas[n], new_m[n], new_v[n] = d_, m_, v_
    gw = grads["w_mod"].reshape(1, DEPTH * D, -1)
    _, d_, m_, v_ = adamw_sum(gw, w_mod.reshape(gw.shape[1:]), m_w_mod.reshape(gw.shape[1:]), v_w_mod.reshape(gw.shape[1:]),
                              name="adamw_w_mod")
    deltas["w_mod"], new_m["w_mod"], new_v["w_mod"] = [a.reshape(w_mod.shape) for a in (d_, m_, v_)]

    groups = {"sc_w_in": ["in0", "in1"], "sc_w_out": ["out0", "out1"], "pool_w": ["pool"], "cf_w_pw1": ["pw1"],
              "cf_w_pw2": ["pw2"], "ffn_w_up": [f"up{l}" for l in range(DEPTH)], "ffn_w_down": [f"down{l}" for l in range(DEPTH)]}
    for n, layers in groups.items():
        stacked = (len(layers),) + got[layers[0]].shape[1:]
        flip = n == "ffn_w_up"
        w3 = [(jnp.swapaxes(env[p + n], 1, 2) if flip else env[p + n]).reshape(stacked) for p in ("", "m_", "v_")]
        outs = None
        for li, key in enumerate(layers):
            outs = adamw_layer(got[key], *w3, outs, li, name=f"adamw_{n}_{li}")
        grads[n], deltas[n], new_m[n], new_v[n] = [(jnp.swapaxes(a, 1, 2) if flip else a).reshape(env[n].shape) for a in outs]

    return (loss, dx[None], *[grads[n] for n in names], *[deltas[n] for n in names], *[new_m[n] for n in names],
            *[new_v[n] for n in names])
```

```python
import functools

import jax
import jax.numpy as jnp
from jax import lax
from jax.experimental import pallas as pl
from jax.experimental.pallas import tpu as pltpu

D = 1024
F = 2816
NDEV = 8
FB = F // 4
DEPTH = 4
RMS_EPS = 1e-6
LN_EPS = 1e-5
CFW = 31
POOL_WINDOWS = (2, 4, 8, 16)
PG = D // 4
LR, B1, B2, ADAM_EPS, WD, STEP = 0.001, 0.9, 0.999, 1e-08, 0.01, 10

BF = jnp.bfloat16
F32 = jnp.float32
VMEM_LIMIT_V7X = 56 * 1024 * 1024
MXU_COLS_V7X = 256
MESH = pl.DeviceIdType.MESH
ANY = pl.BlockSpec(memory_space=pl.ANY)


def _params(n_axes):
    return pltpu.CompilerParams(dimension_semantics=("arbitrary",) * n_axes, vmem_limit_bytes=VMEM_LIMIT_V7X)


def _const(shape, single=True):
    nd = len(shape)
    if single:
        return pl.BlockSpec(shape, lambda *_: (0,) * nd, pipeline_mode=pl.Buffered(1))
    return pl.BlockSpec(shape, lambda *_: (0,) * nd)


def _rows(tm, c):
    return pl.BlockSpec((tm, c), lambda i: (i, 0))


def _brows(nb, tm, c, b0=0):
    return pl.BlockSpec((nb, tm, c), lambda i: (b0, i, 0))


def _prev(hb, c, tm):
    return pl.BlockSpec((hb, c), lambda i: (jnp.maximum(i * (tm // hb) - 1, 0), 0))


def _next(hb, c, tm, t):
    return pl.BlockSpec((hb, c), lambda i: (jnp.minimum((i + 1) * (tm // hb), t // hb - 1), 0))


def _bprev(nb, hb, c, tm, b0=0):
    return pl.BlockSpec((nb, hb, c), lambda i: (b0, jnp.maximum(i * (tm // hb) - 1, 0), 0))


def _bnext(nb, hb, c, tm, t, b0=0):
    return pl.BlockSpec((nb, hb, c), lambda i: (b0, jnp.minimum((i + 1) * (tm // hb), t // hb - 1), 0))


def _sigmoid(v):
    return 0.5 * jnp.tanh(0.5 * v) + 0.5


def _fold8(v):
    r, c = v.shape
    return jnp.sum(v.reshape(r // 8, 8, c), axis=0)


def _chunks(n_rows, rc, step, init=0, reverse=False):
    n = n_rows // rc

    def it(c, carry):
        idx = (n - 1 - c) if reverse else c
        return step(pl.multiple_of(idx * rc, rc), carry)

    return lax.fori_loop(0, n, it, init)


def _row_shifted_copies(s_ref, n):
    for b in range(1, 8):
        s_ref[b, 0:n, :] = s_ref[0, pl.ds(b, n), :]


def _shifted(s_ref, o, tm):
    return s_ref[o % 8, pl.ds(8 * (o // 8), tm), :]


def _dot(a, b):
    return jnp.dot(a, b, preferred_element_type=F32)


def _dot_nt(a, b):
    return lax.dot_general(a, b, (((1,), (1,)), ((), ())), preferred_element_type=F32)


def _dot_tn(a, b):
    return lax.dot_general(a, b, (((0,), (0,)), ((), ())), preferred_element_type=F32)


def _rsum(v):
    return jnp.sum(v, axis=0, keepdims=True)


def _adaln(x, g, sc, sh):
    r = lax.rsqrt(jnp.mean(x * x, axis=-1, keepdims=True) + RMS_EPS)
    return (x * r * g) * (1.0 + sc) + sh


def _gated_res(x, m, gn, gt):
    r = lax.rsqrt(jnp.mean(m * m, axis=-1, keepdims=True) + RMS_EPS)
    return x + gt * (m * r * gn)


def _gated_res_bwd(dxo, m, gn, gt):
    r = lax.rsqrt(jnp.mean(m * m, axis=-1, keepdims=True) + RMS_EPS)
    mh = m * r
    dgt = _rsum(dxo * (mh * gn))
    dn = dxo * gt
    dgn = _rsum(dn * mh)
    dmh = dn * gn
    dm = r * (dmh - mh * jnp.mean(dmh * mh, axis=-1, keepdims=True))
    return dm, dgt, dgn


def _my_id():
    return 4 * lax.axis_index("x") + 2 * lax.axis_index("y") + lax.axis_index("c")


def _peer(k):
    x, y, c = lax.axis_index("x"), lax.axis_index("y"), lax.axis_index("c")
    px = 1 - x if k & 4 else x
    py = 1 - y if k & 2 else y
    pc = 1 - c if k & 1 else c
    return (px, py, pc), 4 * px + 2 * py + pc


class _Exchange:
    def __init__(self, kind, arrays):
        self.gather = kind == "gather"
        self.arrays = list(arrays)
        n = len(self.arrays)
        if self.gather:
            self.out_shape = [jax.ShapeDtypeStruct((NDEV,) + a.shape, a.dtype) for a in self.arrays]
        else:
            self.out_shape = [jax.ShapeDtypeStruct(a.shape, a.dtype) for a in self.arrays]
        self.scratch = [pltpu.SemaphoreType.DMA((n * NDEV,)), pltpu.SemaphoreType.DMA((n * NDEV,)),
                        pltpu.SemaphoreType.DMA((n,))]

    def _local(self, a, src, dst, sems):
        me = _my_id()
        return pltpu.make_async_copy(src[a] if self.gather else src[a].at[me], dst[a].at[me], sems[2].at[a])

    def _remote(self, a, k, src, dst, sems, incoming):
        to, pid = _peer(k)
        me = _my_id()
        return pltpu.make_async_remote_copy(
            src_ref=src[a] if self.gather else src[a].at[pid], dst_ref=dst[a].at[pid if incoming else me],
            send_sem=sems[0].at[a * NDEV + k], recv_sem=sems[1].at[a * NDEV + k], device_id=to, device_id_type=MESH)

    def start(self, src, dst, sems):
        for a in range(len(self.arrays)):
            self._local(a, src, dst, sems).start()
        for k in range(1, NDEV):
            for a in range(len(self.arrays)):
                self._remote(a, k, src, dst, sems, False).start()

    def wait(self, src, dst, sems):
        for k in range(1, NDEV):
            for a in range(len(self.arrays)):
                self._remote(a, k, src, dst, sems, True).wait_recv()
        for k in range(1, NDEV):
            for a in range(len(self.arrays)):
                self._remote(a, k, src, dst, sems, False).wait_send()
        for a in range(len(self.arrays)):
            self._local(a, src, dst, sems).wait()

    def run(self, name):
        n = len(self.arrays)

        def body(*refs):
            src, dst, sems = refs[:n], refs[n:2 * n], refs[2 * n:]
            self.start(src, dst, sems)
            self.wait(src, dst, sems)

        return pl.pallas_call(body, name=name, in_specs=[ANY] * n, out_specs=[ANY] * n, out_shape=self.out_shape,
                              scratch_shapes=self.scratch)(*self.arrays)


def _run_exchanges(exchanges, name):
    counts = [len(e.arrays) for e in exchanges]
    n = sum(counts)

    def body(*refs):
        src, dst, sems = refs[:n], refs[n:2 * n], refs[2 * n:]
        parts, lo = [], 0
        for ei, (e, c) in enumerate(zip(exchanges, counts)):
            parts.append((e, src[lo:lo + c], dst[lo:lo + c], sems[3 * ei:3 * ei + 3]))
            lo += c
        for e, s, d, m in parts:
            e.start(s, d, m)
        for e, s, d, m in parts:
            e.wait(s, d, m)

    res = pl.pallas_call(
        body, name=name, in_specs=[ANY] * n, out_specs=[ANY] * n, out_shape=[s for e in exchanges for s in e.out_shape],
        scratch_shapes=[s for e in exchanges for s in e.scratch])(*[a for e in exchanges for a in e.arrays])
    out, lo = [], 0
    for c in counts:
        out.append(list(res[lo:lo + c]))
        lo += c
    return out


def _call(body, *, name, grid, in_specs, out_specs, out_shape, args, scratch_shapes=(), carry=None, aliases=None):
    cp = _params(len(grid))
    aliases = aliases or {}
    if carry is None:
        return tuple(pl.pallas_call(body, name=name, grid=grid, in_specs=in_specs, out_specs=out_specs, out_shape=out_shape,
                                    scratch_shapes=list(scratch_shapes), input_output_aliases=aliases,
                                    compiler_params=cp)(*args))
    n_in, n_out, n_sc, n_c = len(in_specs), len(out_specs), len(scratch_shapes), len(carry.arrays)

    def wrapped(*refs):
        ins, src = refs[:n_in], refs[n_in:n_in + n_c]
        outs = refs[n_in + n_c:n_in + n_c + n_out]
        dst = refs[n_in + n_c + n_out:n_in + 2 * n_c + n_out]
        rest = refs[n_in + 2 * n_c + n_out:]
        scr, sems = rest[:n_sc], rest[n_sc:]
        first = pl.program_id(0) == 0
        last = pl.program_id(0) == grid[0] - 1
        for ax in range(1, len(grid)):
            first = jnp.logical_and(first, pl.program_id(ax) == 0)
            last = jnp.logical_and(last, pl.program_id(ax) == grid[ax] - 1)

        @pl.when(first)
        def _():
            carry.start(src, dst, sems)

        body(*ins, *outs, *scr)

        @pl.when(last)
        def _():
            carry.wait(src, dst, sems)

    res = pl.pallas_call(
        wrapped, name=name, grid=grid, in_specs=list(in_specs) + [ANY] * n_c, out_specs=list(out_specs) + [ANY] * n_c,
        out_shape=list(out_shape) + carry.out_shape, scratch_shapes=list(scratch_shapes) + carry.scratch,
        input_output_aliases=aliases, compiler_params=cp)(*args, *carry.arrays)
    return tuple(res[:n_out]) + (list(res[n_out:]),)


def fwd_in(x, g, sc, sh, w, bias, *, blocked, tm, name, carry=None, wt=False):
    t = x.shape[0]
    nb, bw = (w.shape[0], w.shape[1]) if wt else (w.shape[0], w.shape[2])
    per = next(k for k in (1, 2, 4, 8) if (k * bw) % MXU_COLS_V7X == 0)
    assert not wt or (blocked and bias is None and nb % per == 0)

    def body(*refs):
        if bias is None:
            x_ref, g_ref, sc_ref, sh_ref, w_ref, h_ref, p_ref = refs
        else:
            x_ref, g_ref, sc_ref, sh_ref, w_ref, b_ref, h_ref, p_ref = refs
        hb = _adaln(x_ref[...], g_ref[...], sc_ref[...], sh_ref[...]).astype(BF)
        h_ref[...] = hb
        if wt:
            for c in range(nb // per):
                y = _dot_nt(hb, w_ref[c * per:(c + 1) * per].reshape(per * bw, D))
                for d in range(per):
                    p_ref[c * per + d] = y[:, d * bw:(d + 1) * bw].astype(BF)
            return
        for d in range(nb):
            y = _dot(hb, w_ref[d])
            if bias is not None:
                y = y + b_ref[d]
            if blocked:
                p_ref[d] = y.astype(BF)
            else:
                p_ref[:, d * bw:(d + 1) * bw] = y.astype(BF)

    vec = _const((1, D))
    in_specs = [_rows(tm, D), vec, vec, vec, _const(w.shape)]
    args = [x, g, sc, sh, w]
    if bias is not None:
        in_specs.append(_const((nb, 1, bw)))
        args.append(bias)
    if blocked:
        p_spec, p_shape = _brows(nb, tm, bw), jax.ShapeDtypeStruct((nb, t, bw), BF)
    else:
        p_spec, p_shape = _rows(tm, nb * bw), jax.ShapeDtypeStruct((t, nb * bw), BF)
    return _call(body, name=name, grid=(t // tm,), in_specs=in_specs, out_specs=[_rows(tm, D), p_spec],
                 out_shape=[jax.ShapeDtypeStruct((t, D), BF), p_shape], args=args, carry=carry)


def _sc_conv(p_ref, ph_ref, cw_ref, first, tm):
    z = p_ref[:, D:2 * D].astype(F32) * p_ref[:, 2 * D:3 * D].astype(F32)
    zp = jnp.where(first, 0.0, ph_ref[8:16, D:2 * D].astype(F32) * ph_ref[8:16, 2 * D:3 * D].astype(F32))
    ext = jnp.concatenate([zp, z], axis=0)
    return cw_ref[0:1, :] * ext[6:6 + tm] + cw_ref[1:2, :] * ext[7:7 + tm] + cw_ref[2:3, :] * z


def sc_fwd_out(p, convw, w_out, x, gn, gt, *, tm, name, carry=None):
    t = x.shape[0]

    def body(p_ref, ph_ref, cw_ref, w_ref, x_ref, gn_ref, gt_ref, x1_ref, m_ref, q_ref):
        u = _sc_conv(p_ref, ph_ref, cw_ref, pl.program_id(0) == 0, tm)
        qb = (p_ref[:, 0:D].astype(F32) * u).astype(BF)
        q_ref[...] = qb
        m = _dot(qb, w_ref[...])
        m_ref[...] = m.astype(BF)
        x1_ref[...] = _gated_res(x_ref[...], m, gn_ref[...], gt_ref[...])

    vec = _const((1, D))
    return _call(
        body, name=name, grid=(t // tm,),
        in_specs=[_rows(tm, 3 * D), _prev(16, 3 * D, tm), _const((3, D)), _const((D, D)), _rows(tm, D), vec, vec],
        out_specs=[_rows(tm, D)] * 3,
        out_shape=[jax.ShapeDtypeStruct((t, D), F32), jax.ShapeDtypeStruct((t, D), BF), jax.ShapeDtypeStruct((t, D), BF)],
        args=[p, p, convw, w_out, x, gn, gt], carry=carry)


def _layernorm_parts(u2):
    mu = jnp.mean(u2, axis=-1, keepdims=True)
    cen = u2 - mu
    rstd = lax.rsqrt(jnp.mean(cen * cen, axis=-1, keepdims=True) + LN_EPS)
    return cen * rstd, rstd


def cf_fwd_out(a, w_dw, b_dw, ln_g, ln_b, w_pw2, b_pw2, x, gn, gt, *, tm, name):
    t = x.shape[0]
    hb = 32

    def body(a_ref, ah_ref, wd_ref, bd_ref, lg_ref, lb_ref, w_ref, b2_ref, x_ref, gn_ref, gt_ref,
             x1_ref, m_ref, s_out_ref, u2_ref, s_ref):
        i = pl.program_id(0)
        uh = ah_ref[:, 0:D].astype(F32) * _sigmoid(ah_ref[:, D:2 * D].astype(F32))
        s_ref[0, 0:hb, :] = jnp.where(i == 0, 0.0, uh)
        s_ref[0, hb:hb + tm, :] = a_ref[:, 0:D].astype(F32) * _sigmoid(a_ref[:, D:2 * D].astype(F32))
        _row_shifted_copies(s_ref, tm + hb - 8)
        acc = bd_ref[...] + wd_ref[0:1, :] * _shifted(s_ref, hb - CFW + 1, tm)
        for k in range(1, CFW):
            acc = acc + wd_ref[k:k + 1, :] * _shifted(s_ref, hb - CFW + 1 + k, tm)
        u2_ref[...] = acc.astype(BF)
        xh, _ = _layernorm_parts(acc)
        l = xh * lg_ref[...] + lb_ref[...]
        sb = (l * _sigmoid(l)).astype(BF)
        s_out_ref[...] = sb
        m = _dot(sb, w_ref[...]) + b2_ref[...]
        m_ref[...] = m.astype(BF)
        x1_ref[...] = _gated_res(x_ref[...], m, gn_ref[...], gt_ref[...])

    vec = _const((1, D))
    return pl.pallas_call(
        body, name=name, grid=(t // tm,),
        in_specs=[_rows(tm, 2 * D), _prev(hb, 2 * D, tm), _const((CFW, D)), vec, vec, vec, _const((D, D)), vec,
                  _rows(tm, D), vec, vec],
        out_specs=[_rows(tm, D)] * 4,
        out_shape=[jax.ShapeDtypeStruct((t, D), F32)] + [jax.ShapeDtypeStruct((t, D), BF)] * 3,
        scratch_shapes=[pltpu.VMEM((8, tm + hb, D), F32)], compiler_params=_params(1),
    )(a, a, w_dw, b_dw, ln_g, ln_b, w_pw2, b_pw2, x, gn, gt)


def _pool_counts(i, tm, w):
    row = lax.broadcasted_iota(jnp.int32, (tm, 1), 0) + i * tm
    return jnp.minimum(row + 1, w).astype(F32)


def pool_fwd(x, g, sc, sh, pw, pb, pscale, gn, gt, *, tm, name):
    t = x.shape[0]
    pad, hb = 8, 16
    base = pad + hb

    def body(x_ref, xh_ref, g_ref, sc_ref, sh_ref, pw_ref, pb_ref, ps_ref, gn_ref, gt_ref,
             x1_ref, m_ref, yp_ref, po_ref, sa_ref, sb_ref):
        i = pl.program_id(0)
        hh = _adaln(xh_ref[...], g_ref[...], sc_ref[...], sh_ref[...])
        h = _adaln(x_ref[...], g_ref[...], sc_ref[...], sh_ref[...])
        zero = jnp.zeros((pad, D), F32)
        sa_ref[0:pad, :] = zero
        sb_ref[0:pad, :] = zero
        sa_ref[pad:base, :] = jnp.where(i == 0, 0.0, hh)
        sa_ref[base:base + tm, :] = h
        n = hb + tm
        src, dst = sa_ref, sb_ref
        ys = []
        for gi, w in enumerate(POOL_WINDOWS):
            c0 = gi * PG
            step = w // 2
            dst[pl.ds(pad, n), c0:D] = src[pl.ds(pad, n), c0:D] + src[pl.ds(pad - step, n), c0:D]
            mean = dst[pl.ds(base, tm), c0:c0 + PG] / _pool_counts(i, tm, w)
            pooled = (mean - h[:, c0:c0 + PG]).astype(BF)
            po_ref[:, c0:c0 + PG] = pooled
            ys.append(_dot(pooled, pw_ref[gi]))
            src, dst = dst, src
        ypre = jnp.concatenate(ys, axis=1) + pb_ref[...]
        yp_ref[...] = ypre.astype(BF)
        m = ypre * ps_ref[...]
        m_ref[...] = m.astype(BF)
        x1_ref[...] = _gated_res(x_ref[...], m, gn_ref[...], gt_ref[...])

    vec = _const((1, D))
    return pl.pallas_call(
        body, name=name, grid=(t // tm,),
        in_specs=[_rows(tm, D), _prev(hb, D, tm), vec, vec, vec, _const((4, PG, PG)), vec, vec, vec, vec],
        out_specs=[_rows(tm, D)] * 4,
        out_shape=[jax.ShapeDtypeStruct((t, D), F32)] + [jax.ShapeDtypeStruct((t, D), BF)] * 3,
        scratch_shapes=[pltpu.VMEM((tm + base, D), F32)] * 2, compiler_params=_params(1),
    )(x, x, g, sc, sh, pw, pb, pscale, gn, gt)


def ffn_fwd_out(up, convw, convb, w_down, x, gn, gt, *, tm, name, carry=None, target=None):
    t = x.shape[0]
    nt = t // tm

    def body(gate_ref, gh_ref, val_ref, cw_ref, cb_ref, w_ref, x_ref, gn_ref, gt_ref, *rest):
        if target is None:
            x2_ref, f_ref, gc_ref, a_ref = rest
        else:
            tg_ref, x2_ref, f_ref, gc_ref, a_ref, l_ref, lacc_ref = rest
        i = pl.program_id(0)
        acc = jnp.zeros((tm, D), F32)
        for j in range(4):
            gate = gate_ref[j].astype(F32)
            ext = jnp.concatenate([jnp.where(i == 0, 0.0, gh_ref[j, 8:16, :].astype(F32)), gate], axis=0)
            gc = cb_ref[j] + cw_ref[j, 0:1, :] * ext[6:6 + tm] + cw_ref[j, 1:2, :] * ext[7:7 + tm] + cw_ref[j, 2:3, :] * gate
            gc_ref[j] = gc.astype(BF)
            ab = (gc * _sigmoid(gc) * val_ref[j].astype(F32)).astype(BF)
            a_ref[j] = ab
            acc = acc + _dot(ab, w_ref[j * FB:(j + 1) * FB, :])
        f_ref[...] = acc.astype(BF)
        x2 = _gated_res(x_ref[...], acc, gn_ref[...], gt_ref[...])
        if target is None:
            x2_ref[...] = x2
        else:
            @pl.when(i == 0)
            def _():
                lacc_ref[...] = jnp.zeros_like(lacc_ref)

            e = x2 - tg_ref[...]
            x2_ref[...] = e * (1.0 / D)
            lacc_ref[...] += _rsum(e * e)

            @pl.when(i == nt - 1)
            def _():
                l_ref[...] = jnp.sum(lacc_ref[...], axis=1, keepdims=True) * (0.5 / D)

    vec = _const((1, D))
    blk = jax.ShapeDtypeStruct((4, t, FB), BF)
    in_specs = [_brows(4, tm, FB, 0), _bprev(4, 16, FB, tm, 0), _brows(4, tm, FB, 1), _const((4, 3, FB)),
                _const((4, 1, FB)), _const((F, D)), _rows(tm, D), vec, vec]
    out_specs = [_rows(tm, D), _rows(tm, D), _brows(4, tm, FB), _brows(4, tm, FB)]
    out_shape = [jax.ShapeDtypeStruct((t, D), F32), jax.ShapeDtypeStruct((t, D), BF), blk, blk]
    args = [up, up, up, convw, convb, w_down, x, gn, gt]
    scratch = []
    if target is not None:
        in_specs.append(_rows(tm, D))
        args.append(target)
        out_specs.append(pl.BlockSpec((1, 1), lambda i: (0, 0)))
        out_shape.append(jax.ShapeDtypeStruct((1, 1), F32))
        scratch.append(pltpu.VMEM((1, D), F32))
    return _call(body, name=name, grid=(nt,), in_specs=in_specs, out_specs=out_specs, out_shape=out_shape, args=args,
                 scratch_shapes=scratch, carry=carry)


def _init_stats(ref):
    @pl.when(pl.program_id(0) == 0)
    def _():
        ref[...] = jnp.zeros_like(ref)


def bwd_out(dxo, m, gn, gt, w, *, tm, name, carry=None):
    t = dxo.shape[0]
    k = w.shape[0]

    def body(dx_ref, m_ref, gn_ref, gt_ref, w_ref, dm_ref, da_ref, st_ref):
        _init_stats(st_ref)
        dm, dgt, dgn = _gated_res_bwd(dx_ref[...], m_ref[...].astype(F32), gn_ref[...], gt_ref[...])
        st_ref[0:1, :] += dgt
        st_ref[1:2, :] += dgn
        st_ref[2:3, :] += _rsum(dm)
        dmb = dm.astype(BF)
        dm_ref[...] = dmb
        da_ref[...] = _dot_nt(dmb, w_ref[...]).astype(BF)

    vec = _const((1, D))
    da_spec, da_shape = _rows(tm, k), jax.ShapeDtypeStruct((t, k), BF)
    return _call(
        body, name=name, grid=(t // tm,), in_specs=[_rows(tm, D), _rows(tm, D), vec, vec, _const((k, D))],
        out_specs=[_rows(tm, D), da_spec, _const((8, D), single=False)],
        out_shape=[jax.ShapeDtypeStruct((t, D), BF), da_shape, jax.ShapeDtypeStruct((8, D), F32)],
        args=[dxo, m, gn, gt, w], carry=carry)


def _adaln_bwd(dh, x_ref, g_ref, sc_ref, dxo_ref, st_ref, row=0):
    x = x_ref[...]
    r = lax.rsqrt(jnp.mean(x * x, axis=-1, keepdims=True) + RMS_EPS)
    xh = x * r
    gv = g_ref[...]
    st_ref[row:row + 1, :] += _rsum(dh)
    st_ref[row + 1:row + 2, :] += _rsum(dh * (xh * gv))
    dn = dh * (1.0 + sc_ref[...])
    st_ref[row + 2:row + 3, :] += _rsum(dn * xh)
    dy = dn * gv
    return dxo_ref[...] + r * (dy - xh * jnp.mean(dy * xh, axis=-1, keepdims=True))


def bwd_in(dp, w, x, g, sc, dxo, *, tm, name):
    t = x.shape[0]
    nb, _, bw = w.shape

    def body(dp_ref, w_ref, x_ref, g_ref, sc_ref, dxo_ref, dx_ref, st_ref):
        _init_stats(st_ref)
        dh = jnp.zeros((tm, D), F32)
        for d in range(nb):
            dh = dh + _dot_nt(dp_ref[:, d * bw:(d + 1) * bw], w_ref[d])
        dx_ref[...] = _adaln_bwd(dh, x_ref, g_ref, sc_ref, dxo_ref, st_ref)

    vec = _const((1, D))
    return pl.pallas_call(
        body, name=name, grid=(t // tm,),
        in_specs=[_rows(tm, nb * bw), _const(w.shape), _rows(tm, D), vec, vec, _rows(tm, D)],
        out_specs=[_rows(tm, D), _const((8, D), single=False)],
        out_shape=[jax.ShapeDtypeStruct((t, D), F32), jax.ShapeDtypeStruct((8, D), F32)],
        compiler_params=_params(1))(dp, w, x, g, sc, dxo)


def sc_bwd_out(dxo, m, gn, gt, w_out, p, convw, *, tm, name, carry=None):
    t = dxo.shape[0]

    def body(dx_ref, m_ref, gn_ref, gt_ref, w_ref, p_ref, ph_ref, cw_ref, dm_ref, dbg_ref, du_ref, st_ref):
        _init_stats(st_ref)
        dm, dgt, dgn = _gated_res_bwd(dx_ref[...], m_ref[...].astype(F32), gn_ref[...], gt_ref[...])
        st_ref[0:1, :] += dgt
        st_ref[1:2, :] += dgn
        dmb = dm.astype(BF)
        dm_ref[...] = dmb
        dq = _dot_nt(dmb, w_ref[...])
        dbg_ref[...] = (dq * _sc_conv(p_ref, ph_ref, cw_ref, pl.program_id(0) == 0, tm)).astype(BF)
        du_ref[...] = (dq * p_ref[:, 0:D].astype(F32)).astype(BF)

    vec = _const((1, D))
    out = jax.ShapeDtypeStruct((t, D), BF)
    return _call(
        body, name=name, grid=(t // tm,),
        in_specs=[_rows(tm, D), _rows(tm, D), vec, vec, _const((D, D)), _rows(tm, 3 * D), _prev(16, 3 * D, tm), _const((3, D))],
        out_specs=[_rows(tm, D)] * 3 + [_const((8, D), single=False)], out_shape=[out, out, out, jax.ShapeDtypeStruct((8, D), F32)],
        args=[dxo, m, gn, gt, w_out, p, p, convw], carry=carry)


def sc_bwd_in(du, dbg, p, convw, w, x, g, sc, dxo, *, tm, name, carry=None):
    t = x.shape[0]

    def body(du_ref, dun_ref, dbg_ref, p_ref, cw_ref, w_ref, x_ref, g_ref, sc_ref, dxo_ref, dx_ref, dp_ref, st_ref, sc2_ref):
        last = pl.program_id(0) == pl.num_programs(0) - 1
        _init_stats(st_ref)
        _init_stats(sc2_ref)
        du = du_ref[...].astype(F32)
        ext = jnp.concatenate([du, jnp.where(last, 0.0, dun_ref[0:8, :].astype(F32))], axis=0)
        e1, e2 = ext[1:tm + 1], ext[2:tm + 2]
        dz = cw_ref[2:3, :] * du + cw_ref[1:2, :] * e1 + cw_ref[0:1, :] * e2
        cg, hi = p_ref[:, D:2 * D].astype(F32), p_ref[:, 2 * D:3 * D].astype(F32)
        dbg, dcg, dhi = dbg_ref[...], (dz * hi).astype(BF), (dz * cg).astype(BF)
        dp_ref[:, 0:D] = dbg
        dp_ref[:, D:2 * D] = dcg
        dp_ref[:, 2 * D:3 * D] = dhi
        dh = _dot_nt(dbg, w_ref[0]) + _dot_nt(dcg, w_ref[1]) + _dot_nt(dhi, w_ref[2])
        z = cg * hi
        sc2_ref[0:1, :] += _rsum(z * e2)
        sc2_ref[1:2, :] += _rsum(z * e1)
        sc2_ref[2:3, :] += _rsum(z * du)
        dx_ref[...] = _adaln_bwd(dh, x_ref, g_ref, sc_ref, dxo_ref, st_ref)

    vec = _const((1, D))
    stat = jax.ShapeDtypeStruct((8, D), F32)
    return _call(
        body, name=name, grid=(t // tm,),
        in_specs=[_rows(tm, D), _next(16, D, tm, t), _rows(tm, D), _rows(tm, 3 * D), _const((3, D)), _const(w.shape),
                  _rows(tm, D), vec, vec, _rows(tm, D)],
        out_specs=[_rows(tm, D), _rows(tm, 3 * D), _const((8, D), single=False), _const((8, D), single=False)],
        out_shape=[jax.ShapeDtypeStruct((t, D), F32), jax.ShapeDtypeStruct((t, 3 * D), BF), stat, stat],
        args=[du, du, dbg, p, convw, w, x, g, sc, dxo], carry=carry)


def cf_bwd_mid(ds, u2, a, w_dw, ln_g, ln_b, *, tm, name, carry=None):
    t = ds.shape[0]
    hb = 32

    def du2_of(dsv, u2v, lg, lb):
        xh, rstd = _layernorm_parts(u2v)
        l = xh * lg + lb
        sg = _sigmoid(l)
        dl = dsv * (sg * (1.0 + l * (1.0 - sg)))
        dxh = dl * lg
        du2 = rstd * (dxh - jnp.mean(dxh, axis=-1, keepdims=True) - xh * jnp.mean(dxh * xh, axis=-1, keepdims=True))
        return du2, dl, xh

    rc, cc = 32, 256

    def body(ds_ref, dsn_ref, u2_ref, u2n_ref, a_ref, wd_ref, lg_ref, lb_ref, da_ref, st_ref, s1_ref, acc_ref):
        i = pl.program_id(0)
        last = i == pl.num_programs(0) - 1
        _init_stats(st_ref)
        _init_stats(acc_ref)
        lg, lb = lg_ref[...], lb_ref[...]
        du2, dl, xh = du2_of(ds_ref[...].astype(F32), u2_ref[...].astype(F32), lg, lb)
        st_ref[32:33, :] += _rsum(dl * xh)
        st_ref[33:34, :] += _rsum(dl)
        st_ref[31:32, :] += _rsum(du2)
        du2n, _, _ = du2_of(dsn_ref[...].astype(F32), u2n_ref[...].astype(F32), lg, lb)
        s1_ref[0, 0:tm, :] = du2
        s1_ref[0, tm:tm + hb, :] = jnp.where(last, 0.0, du2n)
        _row_shifted_copies(s1_ref, tm + hb - 8)

        def taps(r0, _):
            rows = pl.ds(r0, rc)
            for c0 in range(0, D, cc):
                sg = _sigmoid(a_ref[rows, D + c0:D + c0 + cc].astype(F32))
                u = a_ref[rows, c0:c0 + cc].astype(F32) * sg
                du = jnp.zeros((rc, cc), F32)
                for k in range(CFW):
                    o = CFW - 1 - k
                    sh = s1_ref[o % 8, pl.ds(pl.multiple_of(r0 + 8 * (o // 8), 8), rc), c0:c0 + cc]
                    du = du + wd_ref[k:k + 1, c0:c0 + cc] * sh
                    acc_ref[8 * k:8 * k + 8, c0:c0 + cc] += _fold8(u * sh)
                dav = du * sg
                dgv = du * u * (1.0 - sg)
                acc_ref[8 * CFW:8 * CFW + 8, c0:c0 + cc] += _fold8(dav)
                acc_ref[8 * CFW + 8:8 * CFW + 16, c0:c0 + cc] += _fold8(dgv)
                da_ref[rows, c0:c0 + cc] = dav.astype(BF)
                da_ref[rows, D + c0:D + c0 + cc] = dgv.astype(BF)
            return 0

        _chunks(tm, rc, taps)

        @pl.when(last)
        def _():
            for k in range(CFW):
                st_ref[k:k + 1, :] = jnp.sum(acc_ref[8 * k:8 * k + 8, :], axis=0, keepdims=True)
            st_ref[34:35, :] = jnp.sum(acc_ref[8 * CFW:8 * CFW + 8, :], axis=0, keepdims=True)
            st_ref[35:36, :] = jnp.sum(acc_ref[8 * CFW + 8:8 * CFW + 16, :], axis=0, keepdims=True)

    vec = _const((1, D))
    return _call(
        body, name=name, grid=(t // tm,),
        in_specs=[_rows(tm, D), _next(hb, D, tm, t), _rows(tm, D), _next(hb, D, tm, t), _rows(tm, 2 * D),
                  _const((CFW, D)), vec, vec],
        out_specs=[_rows(tm, 2 * D), _const((40, D), single=False)],
        out_shape=[jax.ShapeDtypeStruct((t, 2 * D), BF), jax.ShapeDtypeStruct((40, D), F32)],
        scratch_shapes=[pltpu.VMEM((8, tm + hb, D), F32), pltpu.VMEM((8 * (CFW + 2), D), F32)],
        args=[ds, ds, u2, u2, a, w_dw, ln_g, ln_b], carry=carry)


def pool_bwd(dxo, m, ypre, pw, pscale, gn, gt, x, g, sc, *, tm, name):
    t = dxo.shape[0]
    hb = 16

    def dyp_of(dxv, mv, ypv, ps, gnv, gtv):
        dm, dgt, dgn = _gated_res_bwd(dxv, mv, gnv, gtv)
        return dm * ps, dgt, dgn, _rsum(dm * ypv)

    def body(dx_ref, dxn_ref, m_ref, mn_ref, yp_ref, ypn_ref, pw_ref, ps_ref, gn_ref, gt_ref, x_ref, g_ref, sc_ref,
             dxi_ref, dyp_ref, st_ref, sa_ref, sb_ref):
        i = pl.program_id(0)
        last = i == pl.num_programs(0) - 1
        _init_stats(st_ref)
        ps, gnv, gtv = ps_ref[...], gn_ref[...], gt_ref[...]
        dyp, dgt, dgn, dps = dyp_of(dx_ref[...], m_ref[...].astype(F32), yp_ref[...].astype(F32), ps, gnv, gtv)
        st_ref[0:1, :] += dgt
        st_ref[1:2, :] += dgn
        st_ref[2:3, :] += dps
        st_ref[3:4, :] += _rsum(dyp)
        dypb = dyp.astype(BF)
        dyp_ref[...] = dypb
        dypn, _, _, _ = dyp_of(dxn_ref[...], mn_ref[...].astype(F32), ypn_ref[...].astype(F32), ps, gnv, gtv)
        dypnb = jnp.where(last, 0.0, dypn).astype(BF)
        dpo = []
        for gi, w in enumerate(POOL_WINDOWS):
            c0 = gi * PG
            dp_main = _dot_nt(dypb[:, c0:c0 + PG], pw_ref[gi])
            dp_next = _dot_nt(dypnb[:, c0:c0 + PG], pw_ref[gi])
            dpo.append(dp_main)
            sa_ref[0:tm, c0:c0 + PG] = dp_main / _pool_counts(i, tm, w)
            sa_ref[tm:tm + hb, c0:c0 + PG] = dp_next / float(w)
        zero = jnp.zeros((8, D), F32)
        sa_ref[tm + hb:tm + hb + 8, :] = zero
        sb_ref[tm + hb:tm + hb + 8, :] = zero
        n = tm + hb
        src, dst = sa_ref, sb_ref
        dhs = []
        for gi, w in enumerate(POOL_WINDOWS):
            c0 = gi * PG
            step = w // 2
            dst[pl.ds(0, n), c0:D] = src[pl.ds(0, n), c0:D] + src[pl.ds(step, n), c0:D]
            dhs.append(dst[pl.ds(0, tm), c0:c0 + PG] - dpo[gi])
            src, dst = dst, src
        dxi_ref[...] = _adaln_bwd(jnp.concatenate(dhs, axis=1), x_ref, g_ref, sc_ref, dx_ref, st_ref, row=4)

    vec = _const((1, D))
    return pl.pallas_call(
        body, name=name, grid=(t // tm,),
        in_specs=[_rows(tm, D), _next(hb, D, tm, t), _rows(tm, D), _next(hb, D, tm, t), _rows(tm, D),
                  _next(hb, D, tm, t), _const((4, PG, PG)), vec, vec, vec, _rows(tm, D), vec, vec],
        out_specs=[_rows(tm, D), _rows(tm, D), _const((8, D), single=False)],
        out_shape=[jax.ShapeDtypeStruct((t, D), F32), jax.ShapeDtypeStruct((t, D), BF), jax.ShapeDtypeStruct((8, D), F32)],
        scratch_shapes=[pltpu.VMEM((tm + hb + 8, D), F32)] * 2, compiler_params=_params(1),
    )(dxo, dxo, m, m, ypre, ypre, pw, pscale, gn, gt, x, g, sc)


def ffn_bwd_out(dxo, f, gn, gt, w_down, gc, up, *, tm, name, carry=None):
    t = dxo.shape[0]

    def body(dx_ref, f_ref, gn_ref, gt_ref, w_ref, gc_ref, val_ref, df_ref, dgc_ref, dval_ref, st_ref, sc_ref):
        _init_stats(st_ref)
        _init_stats(sc_ref)
        dm, dgt, dgn = _gated_res_bwd(dx_ref[...], f_ref[...].astype(F32), gn_ref[...], gt_ref[...])
        st_ref[0:1, :] += dgt
        st_ref[1:2, :] += dgn
        dmb = dm.astype(BF)
        df_ref[...] = dmb
        da_all = _dot_nt(dmb, w_ref[...])
        for j in range(4):
            da = da_all[:, j * FB:(j + 1) * FB]
            gcv = gc_ref[j].astype(F32)
            sg = _sigmoid(gcv)
            dval_ref[j] = (da * (gcv * sg)).astype(BF)
            dgc = da * val_ref[j].astype(F32) * (sg * (1.0 + gcv * (1.0 - sg)))
            dgc_ref[j] = dgc.astype(BF)
            sc_ref[j, 0:1, :] += _rsum(dgc)

    vec = _const((1, D))
    return _call(
        body, name=name, grid=(t // tm,),
        in_specs=[_rows(tm, D), _rows(tm, D), vec, vec, _const((F, D)), _brows(4, tm, FB), _brows(4, tm, FB, 1)],
        out_specs=[_rows(tm, D), _brows(4, tm, FB), _brows(4, tm, FB, 1), _const((8, D), single=False),
                   _const((4, 8, FB), single=False)],
        out_shape=[jax.ShapeDtypeStruct((t, D), BF), jax.ShapeDtypeStruct((4, t, FB), BF), jax.ShapeDtypeStruct((8, t, FB), BF),
                   jax.ShapeDtypeStruct((8, D), F32), jax.ShapeDtypeStruct((4, 8, FB), F32)],
        args=[dxo, f, gn, gt, w_down, gc, up], carry=carry)


def ffn_bwd_in(dgc, dup, up, convw, w, x, g, sc, dxo, *, tm, name, carry=None):
    t = x.shape[0]

    def body(dgc_ref, dgcn_ref, dval_ref, gate_ref, cw_ref, w_ref, x_ref, g_ref, sc_ref, dxo_ref,
             dx_ref, dgate_ref, st_ref, sc2_ref):
        last = pl.program_id(0) == pl.num_programs(0) - 1
        _init_stats(st_ref)
        _init_stats(sc2_ref)
        dh = jnp.zeros((tm, D), F32)
        for j in range(4):
            dgc = dgc_ref[j].astype(F32)
            ext = jnp.concatenate([dgc, jnp.where(last, 0.0, dgcn_ref[j, 0:8, :].astype(F32))], axis=0)
            e1, e2 = ext[1:tm + 1], ext[2:tm + 2]
            dgate = (cw_ref[j, 2:3, :] * dgc + cw_ref[j, 1:2, :] * e1 + cw_ref[j, 0:1, :] * e2).astype(BF)
            dgate_ref[j] = dgate
            dh = dh + _dot(dgate, w_ref[j])
            gate = gate_ref[j].astype(F32)
            sc2_ref[j, 1:2, :] += _rsum(gate * e2)
            sc2_ref[j, 2:3, :] += _rsum(gate * e1)
            sc2_ref[j, 3:4, :] += _rsum(gate * dgc)
        for j in range(4):
            dh = dh + _dot(dval_ref[j], w_ref[4 + j])
        dx_ref[...] = _adaln_bwd(dh, x_ref, g_ref, sc_ref, dxo_ref, st_ref)

    vec = _const((1, D))
    return _call(
        body, name=name, grid=(t // tm,),
        in_specs=[_brows(4, tm, FB), _bnext(4, 16, FB, tm, t), _brows(4, tm, FB, 1), _brows(4, tm, FB, 0), _const((4, 3, FB)),
                  _const(w.shape), _rows(tm, D), vec, vec, _rows(tm, D)],
        out_specs=[_rows(tm, D), _brows(4, tm, FB, 0), _const((8, D), single=False), _const((4, 8, FB), single=False)],
        out_shape=[jax.ShapeDtypeStruct((t, D), F32), jax.ShapeDtypeStruct((8, t, FB), BF), jax.ShapeDtypeStruct((8, D), F32),
                   jax.ShapeDtypeStruct((4, 8, FB), F32)],
        args=[dgc, dgc, dup, up, convw, w, x, g, sc, dxo], aliases={2: 1}, carry=carry)


def wgrad(a, b, *, nblk, a_blocked, b_blocked, bk, bn, tt, name, carry=None):
    t = a.shape[1] if a.ndim == 3 else a.shape[0]
    nt = t // tt

    def body(a_ref, b_ref, o_ref, acc_ref):
        s = pl.program_id(1)

        @pl.when(s == 0)
        def _():
            acc_ref[...] = jnp.zeros_like(acc_ref)

        av = a_ref[0] if a.ndim == 3 else a_ref[...]
        bv = b_ref[0] if b.ndim == 3 else b_ref[...]
        acc_ref[...] += _dot_tn(av, bv)

        @pl.when(s == nt - 1)
        def _():
            o_ref[0] = acc_ref[...].astype(BF)

    def spec(arr, blocked, width):
        if arr.ndim == 3:
            return pl.BlockSpec((1, tt, width), lambda j, s: (j, s, 0))
        if blocked:
            return pl.BlockSpec((tt, width), lambda j, s: (s, j))
        return pl.BlockSpec((tt, width), lambda j, s: (s, 0))

    return _call(
        body, name=name, grid=(nblk, nt), in_specs=[spec(a, a_blocked, bk), spec(b, b_blocked, bn)],
        out_specs=[pl.BlockSpec((1, bk, bn), lambda j, s: (j, 0, 0))],
        out_shape=[jax.ShapeDtypeStruct((nblk, bk, bn), BF)],
        scratch_shapes=[pltpu.VMEM((bk, bn), F32)], args=[a, b], carry=carry)


def mod_partial(c_all, w_mod):
    cols = w_mod.shape[2]

    def body(c_ref, w_ref, o_ref):
        c = c_ref[...]
        ca = c * _sigmoid(c)
        o_ref[0] = jnp.dot(ca, w_ref[0], preferred_element_type=F32, precision=lax.Precision.HIGHEST)

    return pl.pallas_call(
        body, name="mod_partial", grid=(DEPTH,),
        in_specs=[pl.BlockSpec((NDEV, D), lambda l: (0, 0)), pl.BlockSpec((1, D, cols), lambda l: (l, 0, 0))],
        out_specs=pl.BlockSpec((1, NDEV, cols), lambda l: (l, 0, 0)),
        out_shape=jax.ShapeDtypeStruct((DEPTH, NDEV, cols), F32), compiler_params=_params(1))(c_all, w_mod)


def mod_finish(parts, b_mod):
    cols = parts.shape[2]

    def body(p_ref, b_ref, o_ref):
        for e in range(NDEV):
            o_ref[:, e * cols:(e + 1) * cols] = p_ref[e] + b_ref[:, e * cols:(e + 1) * cols]

    return pl.pallas_call(
        body, name="mod_finish", out_shape=jax.ShapeDtypeStruct((DEPTH, NDEV * cols), F32))(parts, b_mod)


def sum_parts(parts):
    n, r, c = parts.shape

    def body(p_ref, o_ref):
        acc = p_ref[0]
        for j in range(1, n):
            acc = acc + p_ref[j]
        o_ref[...] = acc

    return pl.pallas_call(body, name="sum_parts", out_shape=jax.ShapeDtypeStruct((r, c), F32))(parts)


def mod_wgrad(c_all_t, gmod_cols):
    cols = gmod_cols.shape[2]

    def body(c_ref, g_ref, o_ref):
        c = c_ref[...]
        ca = c * _sigmoid(c)
        acc = ca[:, 0:1] * g_ref[0, 0:1, :]
        for b in range(1, NDEV):
            acc = acc + ca[:, b:b + 1] * g_ref[0, b:b + 1, :]
        o_ref[0] = acc

    return pl.pallas_call(
        body, name="mod_wgrad", grid=(DEPTH,),
        in_specs=[pl.BlockSpec((D, NDEV), lambda l: (0, 0)), pl.BlockSpec((1, NDEV, cols), lambda l: (l, 0, 0))],
        out_specs=pl.BlockSpec((1, D, cols), lambda l: (l, 0, 0)),
        out_shape=jax.ShapeDtypeStruct((DEPTH, D, cols), F32), compiler_params=_params(1))(c_all_t, gmod_cols)


def _adamw_math(g, w, m, v):
    m2 = B1 * m + (1.0 - B1) * g
    v2 = B2 * v + (1.0 - B2) * (g * g)
    m_hat = m2 / (1.0 - B1 ** STEP)
    v_hat = v2 / (1.0 - B2 ** STEP)
    delta = -LR * (m_hat / (jnp.sqrt(v_hat) + ADAM_EPS) + WD * w)
    return delta, m2, v2


def _row_tile(r, c, budget=1 << 18):
    if r * c <= budget or r % 8:
        return r
    best = 8
    for cand in range(8, r + 1, 8):
        if r % cand == 0 and cand * c <= budget:
            best = cand
    return best


def adamw_sum(parts, w, m, v, *, name):
    n, r, c = parts.shape
    tr = _row_tile(r, c)

    def body(p_ref, w_ref, m_ref, v_ref, g_ref, d_ref, m2_ref, v2_ref):
        g = p_ref[0].astype(F32)
        for j in range(1, n):
            g = g + p_ref[j].astype(F32)
        d, m2, v2 = _adamw_math(g, w_ref[...], m_ref[...], v_ref[...])
        g_ref[...] = g
        d_ref[...] = d
        m2_ref[...] = m2
        v2_ref[...] = v2

    blk = pl.BlockSpec((tr, c), lambda i: (i, 0))
    out = jax.ShapeDtypeStruct((r, c), F32)
    return pl.pallas_call(
        body, name=name, grid=(r // tr,), in_specs=[pl.BlockSpec((n, tr, c), lambda i: (0, i, 0)), blk, blk, blk],
        out_specs=[blk] * 4, out_shape=[out] * 4, compiler_params=_params(1))(parts, w, m, v)


def adamw_layer(parts, w, m, v, prev, layer, *, name):
    n, r, c = parts.shape
    nl = w.shape[0]
    tr = _row_tile(r, c)

    def body(p_ref, w_ref, m_ref, v_ref, *rest):
        g_ref, d_ref, m2_ref, v2_ref = rest[-4:]
        g = p_ref[0].astype(F32)
        for j in range(1, n):
            g = g + p_ref[j].astype(F32)
        d, m2, v2 = _adamw_math(g, w_ref[0], m_ref[0], v_ref[0])
        g_ref[0] = g
        d_ref[0] = d
        m2_ref[0] = m2
        v2_ref[0] = v2

    blk = pl.BlockSpec((1, tr, c), lambda i: (layer, i, 0))
    in_specs = [pl.BlockSpec((n, tr, c), lambda i: (0, i, 0)), blk, blk, blk]
    args = [parts, w, m, v]
    aliases = {}
    if prev is not None:
        in_specs += [ANY] * 4
        args += list(prev)
        aliases = {4 + k: k for k in range(4)}
    out = jax.ShapeDtypeStruct((nl, r, c), F32)
    return pl.pallas_call(
        body, name=name, grid=(r // tr,), in_specs=in_specs, out_specs=[blk] * 4, out_shape=[out] * 4,
        input_output_aliases=aliases, compiler_params=_params(1))(*args)


def _pack(arrays):
    flat, layout, off = [], [], 0
    for a in arrays:
        flat.append(a.reshape(-1))
        layout.append((off, a.shape))
        off += a.size
    pad = (-off) % 1024
    if pad:
        flat.append(jnp.zeros((pad,), F32))
    return jnp.concatenate(flat).reshape(-1, 128), layout


def _unpack(packed, layout, lead=()):
    flat = packed.reshape(lead + (-1,))
    return [flat[..., off:off + _size(shape)].reshape(lead + tuple(shape)) for off, shape in layout]


def _size(shape):
    n = 1
    for s in shape:
        n *= s
    return n


def _join_last(g):
    g = jnp.moveaxis(g, 0, -2)
    return g.reshape(g.shape[:-2] + (g.shape[-2] * g.shape[-1],))


def _my_cols(a, width):
    return lax.dynamic_slice_in_dim(a, _my_id() * width, width, axis=a.ndim - 1)


def _tile(t, pref):
    return min(pref, t)


def kernel(x, c, w_mod, b_mod, norm_g, sc_w_in, sc_conv, sc_w_out, pool_w, pool_b, pool_scale, cf_w_pw1, cf_b_pw1, cf_w_dw, cf_b_dw, cf_ln_g, cf_ln_b, cf_w_pw2, cf_b_pw2, ffn_w_up, ffn_conv, ffn_b_conv, ffn_w_down, loss_target, m_w_mod, m_b_mod, m_norm_g, m_sc_w_in, m_sc_conv, m_sc_w_out, m_pool_w, m_pool_b, m_pool_scale, m_cf_w_pw1, m_cf_b_pw1, m_cf_w_dw, m_cf_b_dw, m_cf_ln_g, m_cf_ln_b, m_cf_w_pw2, m_cf_b_pw2, m_ffn_w_up, m_ffn_conv, m_ffn_b_conv, m_ffn_w_down, v_w_mod, v_b_mod, v_norm_g, v_sc_w_in, v_sc_conv, v_sc_w_out, v_pool_w, v_pool_b, v_pool_scale, v_cf_w_pw1, v_cf_b_pw1, v_cf_w_dw, v_cf_b_dw, v_cf_ln_g, v_cf_ln_b, v_cf_w_pw2, v_cf_b_pw2, v_ffn_w_up, v_ffn_conv, v_ffn_b_conv, v_ffn_w_down):
    env = dict(locals())
    names = ["w_mod", "b_mod", "norm_g", "sc_w_in", "sc_conv", "sc_w_out", "pool_w", "pool_b", "pool_scale", "cf_w_pw1",
             "cf_b_pw1", "cf_w_dw", "cf_b_dw", "cf_ln_g", "cf_ln_b", "cf_w_pw2", "cf_b_pw2", "ffn_w_up", "ffn_conv",
             "ffn_b_conv", "ffn_w_down"]
    t = x.shape[1]
    tm = _tile(t, 512)
    tt = _tile(t, 4096)
    x0, target = x[0], loss_target[0]

    small_names = ["norm_g", "sc_conv", "cf_b_pw1", "cf_w_dw", "cf_b_dw", "cf_ln_g", "cf_ln_b", "cf_b_pw2", "ffn_conv"]
    packed, layout = _pack([c] + [env[n] for n in small_names])

    shard = {"pool": pool_w[0].astype(BF), "pw1": cf_w_pw1[0].astype(BF), "pw2": cf_w_pw2[0].astype(BF)}
    for j in range(2):
        shard[f"in{j}"], shard[f"out{j}"] = sc_w_in[j].astype(BF), sc_w_out[j].astype(BF)
    for l in range(DEPTH):
        shard[f"up{l}"], shard[f"down{l}"] = ffn_w_up[l].T.astype(BF), ffn_w_down[l].astype(BF)
    gathered, g_in0, g_out0 = _Exchange("gather", [packed, shard["in0"], shard["out0"]]).run("gather_first")
    wg = {"in0": g_in0, "out0": g_out0}
    parts = _unpack(gathered, layout, lead=(NDEV,))
    c_all = parts[0].reshape(NDEV, D)
    full = {n: _join_last(p) for n, p in zip(small_names, parts[1:])}
    fwd_plan = {("mix_in", 0): ["up0"], ("mix_out", 0): ["down0"], ("ffn_in", 0): ["pool", "up1"],
                ("ffn_out", 0): ["down1", "pw1", "pw2"], ("ffn_in", 1): ["up2"], ("ffn_out", 1): ["down2", "in1", "out1"],
                ("ffn_in", 2): ["up3"], ("ffn_out", 2): ["down3"]}

    def carrying(plan, kind, store, source, fn, key, *a, **k):
        names = plan.get(key)
        if not names:
            return fn(*a, **k)
        res = fn(*a, carry=_Exchange(kind, [source[n] for n in names]), **k)
        store.update(zip(names, res[-1]))
        return res[:-1]

    fwd = functools.partial(carrying, fwd_plan, "gather", wg, shard)

    mp = mod_partial(c_all, w_mod)
    (mod_parts,) = _Exchange("scatter", [jnp.swapaxes(mp, 0, 1)]).run("exchange_mod")
    mod = mod_finish(mod_parts, b_mod)

    def vec(a):
        return a.reshape(1, -1)

    def col_blocks(g):
        w = jnp.swapaxes(g, 0, 1).reshape(D, -1)
        return jnp.swapaxes(w.reshape(D, -1, D), 0, 1)

    def ffn_blocks(a):
        return jnp.swapaxes(a.reshape(a.shape[0], 4, FB), 0, 1)

    saved = []
    xs = x0
    for l in range(DEPTH):
        sh1, sc1, g1, sh2, sc2, g2 = [mod[l:l + 1, k * D:(k + 1) * D] for k in range(6)]
        ng = [full["norm_g"][l, k:k + 1] for k in range(4)]
        kind, j = l % 3, l // 3
        s = dict(x_in=xs, sc1=sc1, g1=g1, sc2=sc2, g2=g2, ng=ng)
        if kind == 0:
            s["w_in"] = col_blocks(wg[f"in{j}"])
            s["h"], s["p"] = fwd(fwd_in, ("mix_in", l), xs, ng[0], sc1, sh1, s["w_in"], None, blocked=False, tm=tm,
                                 name=f"sc_in_{l}")
            x1, s["m"], s["q"] = fwd(sc_fwd_out, ("mix_out", l), s["p"], full["sc_conv"][j], wg[f"out{j}"].reshape(D, D), xs,
                                     ng[1], g1, tm=tm, name=f"sc_out_{l}")
        elif kind == 1:
            pool_w_f = jnp.swapaxes(wg["pool"], 0, 1).reshape(4, PG, PG)
            x1, s["m"], s["ypre"], s["pooled"] = pool_fwd(xs, ng[0], sc1, sh1, pool_w_f, pool_b, pool_scale, ng[1], g1,
                                                          tm=tm, name=f"pool_{l}")
        else:
            s["w_in"] = col_blocks(wg["pw1"])
            s["h"], s["a"] = fwd_in(xs, ng[0], sc1, sh1, s["w_in"], full["cf_b_pw1"].reshape(2, 1, D), blocked=False, tm=tm,
                                    name=f"cf_in_{l}")
            x1, s["m"], s["s"], s["u2"] = cf_fwd_out(s["a"], full["cf_w_dw"][0], full["cf_b_dw"], full["cf_ln_g"],
                                                     full["cf_ln_b"], wg["pw2"].reshape(D, D), full["cf_b_pw2"], xs, ng[1],
                                                     g1, tm=tm, name=f"cf_out_{l}")
        s["x1"] = x1
        s["cw"] = ffn_blocks(full["ffn_conv"][l])
        s["h2"], s["up"] = fwd(fwd_in, ("ffn_in", l), x1, ng[2], sc2, sh2, wg[f"up{l}"], None, blocked=True, wt=True, tm=tm,
                               name=f"ffn_in_{l}")
        xs, s["f"], s["gc"], s["fa"], *loss_part = fwd(
            ffn_fwd_out, ("ffn_out", l), s["up"], s["cw"], ffn_blocks(ffn_b_conv[l:l + 1]), wg[f"down{l}"].reshape(F, D), x1,
            ng[3], g2, tm=tm, name=f"ffn_out_{l}", target=target if l == DEPTH - 1 else None)
        saved.append(s)

    dx = xs
    loss = lax.psum(loss_part[0][0, 0], ("x", "y", "c"))

    gmod = [None] * DEPTH
    d_norm_g = [None] * DEPTH
    d_ffn_conv = [None] * DEPTH
    d_ffn_b_conv = [None] * DEPTH
    d_sc_conv = [None] * 2
    big = {}
    got = {}
    small_g = {}
    bwd_plan = {("mix_bout", 3): ["down3"], ("mix_bin", 3): ["up3"], ("ffn_bout", 2): ["in1", "out1"],
                ("mix_bmid", 2): ["up2", "down2"], ("ffn_bout", 1): ["pw1", "pw2"], ("ffn_bout", 0): ["pool", "down1"],
                ("ffn_bin", 0): ["up1"], ("mix_bout", 0): ["down0"], ("mix_win", 0): ["up0"], ("mix_wout", 0): ["in0"]}
    bwd = functools.partial(carrying, bwd_plan, "scatter", got, big)
    pool_w_f = jnp.swapaxes(wg["pool"], 0, 1).reshape(4, PG, PG)
    for l in reversed(range(DEPTH)):
        s = saved[l]
        ng = s["ng"]
        kind, j = l % 3, l // 3
        df, dgc, dup, st_o, st_b = bwd(ffn_bwd_out, ("ffn_bout", l), dx, s["f"], ng[3], s["g2"], wg[f"down{l}"].reshape(F, D),
                                       s["gc"], s["up"], tm=tm, name=f"ffn_bout_{l}")
        dx1, dup, st_i, st_c = bwd(ffn_bwd_in, ("ffn_bin", l), dgc, dup, s["up"], s["cw"], wg[f"up{l}"], s["x1"], ng[2],
                                   s["sc2"], dx, tm=tm, name=f"ffn_bin_{l}")
        (big[f"up{l}"],) = wgrad(dup, s["h2"], nblk=NDEV, a_blocked=True, b_blocked=False, bk=FB, bn=D, tt=tt,
                                 name=f"ffn_wup_{l}")
        big[f"down{l}"] = wgrad(s["fa"], df, nblk=4, a_blocked=True, b_blocked=False, bk=FB, bn=D, tt=tt,
                                name=f"ffn_wdown_{l}")[0].reshape(NDEV, F // NDEV, D)
        d_ffn_b_conv[l] = st_b[:, 0, :].reshape(F)
        d_ffn_conv[l] = jnp.swapaxes(st_c[:, 1:4, :], 0, 1).reshape(3, F)
        g_ffn = [st_i[0], st_i[1], st_o[0]]
        dn3, dn2 = st_o[1], st_i[2]
        if kind == 0:
            dm, dbg, du, st_o = bwd(sc_bwd_out, ("mix_bout", l), dx1, s["m"], ng[1], s["g1"], wg[f"out{j}"].reshape(D, D), s["p"],
                                    full["sc_conv"][j], tm=tm, name=f"sc_bout_{l}")
            dx, dp, st_i, st_c = bwd(sc_bwd_in, ("mix_bin", l), du, dbg, s["p"], full["sc_conv"][j], s["w_in"], s["x_in"], ng[0],
                                     s["sc1"], dx1, tm=tm, name=f"sc_bin_{l}")
            (w_pairs,) = bwd(wgrad, ("mix_win", l), s["h"], dp, nblk=NDEV // 2, a_blocked=False, b_blocked=True, bk=D,
                             bn=6 * D // NDEV, tt=tt, name=f"sc_win_{l}")
            big[f"in{j}"] = jnp.swapaxes(w_pairs.reshape(NDEV // 2, D, 2, 3 * D // NDEV), 1, 2).reshape(NDEV, D, 3 * D // NDEV)
            big[f"out{j}"] = bwd(wgrad, ("mix_wout", l), s["q"], dm, nblk=1, a_blocked=False, b_blocked=False, bk=D, bn=D, tt=tt,
                                 name=f"sc_wout_{l}")[0].reshape(NDEV, D // NDEV, D)
            d_sc_conv[j] = st_c[0:3]
        elif kind == 1:
            dx, dyp, st_o = pool_bwd(dx1, s["m"], s["ypre"], pool_w_f, pool_scale, ng[1], s["g1"], s["x_in"], ng[0], s["sc1"],
                                     tm=tm, name=f"pool_b_{l}")
            st_i = st_o[4:7]
            (dpw,) = wgrad(s["pooled"], dyp, nblk=4, a_blocked=True, b_blocked=True, bk=PG, bn=PG, tt=tt, name=f"pool_w_{l}")
            big["pool"] = jnp.swapaxes(dpw.reshape(4, NDEV, PG // NDEV, PG), 0, 1).reshape(NDEV, 4 * PG // NDEV, PG)
            small_g["pool_scale"], small_g["pool_b"] = st_o[2:3], st_o[3:4]
        else:
            dm, ds, st_o = bwd_out(dx1, s["m"], ng[1], s["g1"], wg["pw2"].reshape(D, D), tm=tm,
                                   name=f"cf_bout_{l}")
            dA, st_c = bwd(cf_bwd_mid, ("mix_bmid", l), ds, s["u2"], s["a"], full["cf_w_dw"][0], full["cf_ln_g"],
                           full["cf_ln_b"], tm=tm, name=f"cf_bmid_{l}")
            dx, st_i = bwd_in(dA, s["w_in"], s["x_in"], ng[0], s["sc1"], dx1, tm=tm, name=f"cf_bin_{l}")
            (w_quads,) = wgrad(s["h"], dA, nblk=2, a_blocked=False, b_blocked=True, bk=D, bn=D, tt=tt, name=f"cf_wpw1_{l}")
            big["pw1"] = jnp.swapaxes(w_quads.reshape(2, D, 4, 2 * D // NDEV), 1, 2).reshape(NDEV, D, 2 * D // NDEV)
            big["pw2"] = wgrad(s["s"], dm, nblk=1, a_blocked=False, b_blocked=False, bk=D, bn=D, tt=tt,
                               name=f"cf_wpw2_{l}")[0].reshape(NDEV, D // NDEV, D)
            small_g["cf_w_dw"] = st_c[0:CFW][None]
            small_g["cf_b_dw"], small_g["cf_ln_g"], small_g["cf_ln_b"] = st_c[31:32], st_c[32:33], st_c[33:34]
            small_g["cf_b_pw1"] = st_c[34:36].reshape(1, 2 * D)
            small_g["cf_b_pw2"] = st_o[2:3]
        gmod[l] = jnp.concatenate([st_i[0], st_i[1], st_o[0]] + g_ffn)
        d_norm_g[l] = jnp.stack([st_i[2], st_o[1], dn2, dn3])

    small_g["gmod"] = jnp.stack(gmod)
    small_g["norm_g"] = jnp.stack(d_norm_g)
    small_g["sc_conv"] = jnp.stack(d_sc_conv)
    small_g["ffn_conv"] = jnp.stack(d_ffn_conv)
    small_g["ffn_b_conv"] = jnp.stack(d_ffn_b_conv)
    sg_names = ["gmod", "norm_g", "sc_conv", "pool_b", "pool_scale", "cf_b_pw1", "cf_w_dw", "cf_b_dw", "cf_ln_g", "cf_ln_b",
                "cf_b_pw2", "ffn_conv", "ffn_b_conv"]
    gpacked, glayout = _pack([small_g[n] for n in sg_names])
    (ggath,), (got["out0"],) = _run_exchanges(
        [_Exchange("gather", [gpacked]), _Exchange("scatter", [big["out0"]])], "exchange_last")
    gsum = dict(zip(sg_names, _unpack(sum_parts(ggath), glayout)))
    gmod_all = _unpack(ggath, glayout[:1], lead=(NDEV,))[0]
    grads = {"b_mod": gsum["gmod"], "pool_b": gsum["pool_b"], "pool_scale": gsum["pool_scale"],
             "ffn_b_conv": gsum["ffn_b_conv"]}
    for n in ["norm_g", "sc_conv", "cf_b_pw1", "cf_w_dw", "cf_b_dw", "cf_ln_g", "cf_ln_b", "cf_b_pw2", "ffn_conv"]:
        grads[n] = _my_cols(gsum[n], env[n].shape[-1])
    grads["w_mod"] = mod_wgrad(c_all.T, jnp.swapaxes(_my_cols(gmod_all, w_mod.shape[2]), 0, 1))

    deltas, new_m, new_v = {}, {}, {}
    sp_names = ["b_mod", "norm_g", "sc_conv", "pool_b", "pool_scale", "cf_b_pw1", "cf_w_dw", "cf_b_dw", "cf_ln_g", "cf_ln_b",
                "cf_b_pw2", "ffn_conv", "ffn_b_conv"]
    pg, playout = _pack([grads[n] for n in sp_names])
    pw_, _ = _pack([env[n] for n in sp_names])
    pm_, _ = _pack([env["m_" + n] for n in sp_names])
    pv_, _ = _pack([env["v_" + n] for n in sp_names])
    _, sd, sm, sv = adamw_sum(pg[None], pw_, pm_, pv_, name="adamw_small")
    for n, d_, m_, v_ in zip(sp_names, _unpack(sd, playout), _unpack(sm, playout), _unpack(sv, playout)):
        deltas[n], new_m[n], new_v[n] = d_, m_, v_
    gw = grads["w_mod"].reshape(1, DEPTH * D, -1)
    _, d_, m_, v_ = adamw_sum(gw, w_mod.reshape(gw.shape[1:]), m_w_mod.reshape(gw.shape[1:]), v_w_mod.reshape(gw.shape[1:]),
                              name="adamw_w_mod")
    deltas["w_mod"], new_m["w_mod"], new_v["w_mod"] = [a.reshape(w_mod.shape) for a in (d_, m_, v_)]

    groups = {"sc_w_in": ["in0", "in1"], "sc_w_out": ["out0", "out1"], "pool_w": ["pool"], "cf_w_pw1": ["pw1"],
              "cf_w_pw2": ["pw2"], "ffn_w_up": [f"up{l}" for l in range(DEPTH)], "ffn_w_down": [f"down{l}" for l in range(DEPTH)]}
    for n, layers in groups.items():
        stacked = (len(layers),) + got[layers[0]].shape[1:]
        flip = n == "ffn_w_up"
        w3 = [(jnp.swapaxes(env[p + n], 1, 2) if flip else env[p + n]).reshape(stacked) for p in ("", "m_", "v_")]
        outs = None
        for li, key in enumerate(layers):
            outs = adamw_layer(got[key], *w3, outs, li, name=f"adamw_{n}_{li}")
        grads[n], deltas[n], new_m[n], new_v[n] = [(jnp.swapaxes(a, 1, 2) if flip else a).reshape(env[n].shape) for a in outs]

    return (loss, dx[None], *[grads[n] for n in names], *[deltas[n] for n in names], *[new_m[n] for n in names],
            *[new_v[n] for n in names])
```

```python
import functools

import jax
import jax.numpy as jnp
from jax import lax
from jax.experimental import pallas as pl
from jax.experimental.pallas import tpu as pltpu

D = 1024
F = 2816
NDEV = 8
FB = F // 4
DEPTH = 4
RMS_EPS = 1e-6
LN_EPS = 1e-5
CFW = 31
POOL_WINDOWS = (2, 4, 8, 16)
PG = D // 4
LR, B1, B2, ADAM_EPS, WD, STEP = 0.001, 0.9, 0.999, 1e-08, 0.01, 10

BF = jnp.bfloat16
F32 = jnp.float32
VMEM_LIMIT_V7X = 56 * 1024 * 1024
MXU_COLS_V7X = 256
MESH = pl.DeviceIdType.MESH
ANY = pl.BlockSpec(memory_space=pl.ANY)


def _params(n_axes):
    return pltpu.CompilerParams(dimension_semantics=("arbitrary",) * n_axes, vmem_limit_bytes=VMEM_LIMIT_V7X)


def _const(shape, single=True):
    nd = len(shape)
    if single:
        return pl.BlockSpec(shape, lambda *_: (0,) * nd, pipeline_mode=pl.Buffered(1))
    return pl.BlockSpec(shape, lambda *_: (0,) * nd)


def _rows(tm, c):
    return pl.BlockSpec((tm, c), lambda i: (i, 0))


def _brows(nb, tm, c, b0=0):
    return pl.BlockSpec((nb, tm, c), lambda i: (b0, i, 0))


def _prev(hb, c, tm):
    return pl.BlockSpec((hb, c), lambda i: (jnp.maximum(i * (tm // hb) - 1, 0), 0))


def _next(hb, c, tm, t):
    return pl.BlockSpec((hb, c), lambda i: (jnp.minimum((i + 1) * (tm // hb), t // hb - 1), 0))


def _bprev(nb, hb, c, tm, b0=0):
    return pl.BlockSpec((nb, hb, c), lambda i: (b0, jnp.maximum(i * (tm // hb) - 1, 0), 0))


def _bnext(nb, hb, c, tm, t, b0=0):
    return pl.BlockSpec((nb, hb, c), lambda i: (b0, jnp.minimum((i + 1) * (tm // hb), t // hb - 1), 0))


def _sigmoid(v):
    return 0.5 * jnp.tanh(0.5 * v) + 0.5


def _fold8(v):
    r, c = v.shape
    return jnp.sum(v.reshape(r // 8, 8, c), axis=0)


def _chunks(n_rows, rc, step, init=0, reverse=False):
    n = n_rows // rc

    def it(c, carry):
        idx = (n - 1 - c) if reverse else c
        return step(pl.multiple_of(idx * rc, rc), carry)

    return lax.fori_loop(0, n, it, init)


def _row_shifted_copies(s_ref, n):
    for b in range(1, 8):
        s_ref[b, 0:n, :] = s_ref[0, pl.ds(b, n), :]


def _shifted(s_ref, o, tm):
    return s_ref[o % 8, pl.ds(8 * (o // 8), tm), :]


def _dot(a, b):
    return jnp.dot(a, b, preferred_element_type=F32)


def _dot_nt(a, b):
    return lax.dot_general(a, b, (((1,), (1,)), ((), ())), preferred_element_type=F32)


def _dot_tn(a, b):
    return lax.dot_general(a, b, (((0,), (0,)), ((), ())), preferred_element_type=F32)


def _rsum(v):
    return jnp.sum(v, axis=0, keepdims=True)


def _adaln(x, g, sc, sh):
    r = lax.rsqrt(jnp.mean(x * x, axis=-1, keepdims=True) + RMS_EPS)
    return (x * r * g) * (1.0 + sc) + sh


def _gated_res(x, m, gn, gt):
    r = lax.rsqrt(jnp.mean(m * m, axis=-1, keepdims=True) + RMS_EPS)
    return x + gt * (m * r * gn)


def _gated_res_bwd(dxo, m, gn, gt):
    r = lax.rsqrt(jnp.mean(m * m, axis=-1, keepdims=True) + RMS_EPS)
    mh = m * r
    dgt = _rsum(dxo * (mh * gn))
    dn = dxo * gt
    dgn = _rsum(dn * mh)
    dmh = dn * gn
    dm = r * (dmh - mh * jnp.mean(dmh * mh, axis=-1, keepdims=True))
    return dm, dgt, dgn


def _my_id():
    return 4 * lax.axis_index("x") + 2 * lax.axis_index("y") + lax.axis_index("c")


def _peer(k):
    x, y, c = lax.axis_index("x"), lax.axis_index("y"), lax.axis_index("c")
    px = 1 - x if k & 4 else x
    py = 1 - y if k & 2 else y
    pc = 1 - c if k & 1 else c
    return (px, py, pc), 4 * px + 2 * py + pc


class _Exchange:
    def __init__(self, kind, arrays):
        self.gather = kind == "gather"
        self.arrays = list(arrays)
        n = len(self.arrays)
        if self.gather:
            self.out_shape = [jax.ShapeDtypeStruct((NDEV,) + a.shape, a.dtype) for a in self.arrays]
        else:
            self.out_shape = [jax.ShapeDtypeStruct(a.shape, a.dtype) for a in self.arrays]
        self.scratch = [pltpu.SemaphoreType.DMA((n * NDEV,)), pltpu.SemaphoreType.DMA((n * NDEV,)),
                        pltpu.SemaphoreType.DMA((n,))]

    def _local(self, a, src, dst, sems):
        me = _my_id()
        return pltpu.make_async_copy(src[a] if self.gather else src[a].at[me], dst[a].at[me], sems[2].at[a])

    def _remote(self, a, k, src, dst, sems, incoming):
        to, pid = _peer(k)
        me = _my_id()
        return pltpu.make_async_remote_copy(
            src_ref=src[a] if self.gather else src[a].at[pid], dst_ref=dst[a].at[pid if incoming else me],
            send_sem=sems[0].at[a * NDEV + k], recv_sem=sems[1].at[a * NDEV + k], device_id=to, device_id_type=MESH)

    def start(self, src, dst, sems):
        for a in range(len(self.arrays)):
            self._local(a, src, dst, sems).start()
        for k in range(1, NDEV):
            for a in range(len(self.arrays)):
                self._remote(a, k, src, dst, sems, False).start()

    def wait(self, src, dst, sems):
        for k in range(1, NDEV):
            for a in range(len(self.arrays)):
                self._remote(a, k, src, dst, sems, True).wait_recv()
        for k in range(1, NDEV):
            for a in range(len(self.arrays)):
                self._remote(a, k, src, dst, sems, False).wait_send()
        for a in range(len(self.arrays)):
            self._local(a, src, dst, sems).wait()

    def run_two_level(self, name):
        assert self.gather
        n = len(self.arrays)

        def body(*refs):
            src, dst, sems = refs[:n], refs[n:2 * n], refs[2 * n:]
            me = _my_id()
            sibling, _ = _peer(1)

            def passed_on(a, k, incoming):
                _, pid = _peer(k + 1 if incoming else k)
                return pltpu.make_async_remote_copy(
                    src_ref=dst[a].at[pid], dst_ref=dst[a].at[pid], send_sem=sems[0].at[a * NDEV + k + 1],
                    recv_sem=sems[1].at[a * NDEV + k + 1], device_id=sibling, device_id_type=MESH)

            for a in range(n):
                self._local(a, src, dst, sems).start()
            for k in (1, 2, 4, 6):
                for a in range(n):
                    self._remote(a, k, src, dst, sems, False).start()
            for k in (2, 4, 6):
                for a in range(n):
                    self._remote(a, k, src, dst, sems, True).wait_recv()
                    passed_on(a, k, False).start()
            for a in range(n):
                self._remote(a, 1, src, dst, sems, True).wait_recv()
                for k in (2, 4, 6):
                    passed_on(a, k, True).wait_recv()
            for a in range(n):
                for k in (1, 2, 4, 6):
                    self._remote(a, k, src, dst, sems, False).wait_send()
                for k in (2, 4, 6):
                    passed_on(a, k, False).wait_send()
                self._local(a, src, dst, sems).wait()

        return pl.pallas_call(body, name=name, in_specs=[ANY] * n, out_specs=[ANY] * n, out_shape=self.out_shape,
                              scratch_shapes=self.scratch)(*self.arrays)

    def run(self, name):
        n = len(self.arrays)

        def body(*refs):
            src, dst, sems = refs[:n], refs[n:2 * n], refs[2 * n:]
            self.start(src, dst, sems)
            self.wait(src, dst, sems)

        return pl.pallas_call(body, name=name, in_specs=[ANY] * n, out_specs=[ANY] * n, out_shape=self.out_shape,
                              scratch_shapes=self.scratch)(*self.arrays)


def _run_exchanges(exchanges, name):
    counts = [len(e.arrays) for e in exchanges]
    n = sum(counts)

    def body(*refs):
        src, dst, sems = refs[:n], refs[n:2 * n], refs[2 * n:]
        parts, lo = [], 0
        for ei, (e, c) in enumerate(zip(exchanges, counts)):
            parts.append((e, src[lo:lo + c], dst[lo:lo + c], sems[3 * ei:3 * ei + 3]))
            lo += c
        for e, s, d, m in parts:
            e.start(s, d, m)
        for e, s, d, m in parts:
            e.wait(s, d, m)

    res = pl.pallas_call(
        body, name=name, in_specs=[ANY] * n, out_specs=[ANY] * n, out_shape=[s for e in exchanges for s in e.out_shape],
        scratch_shapes=[s for e in exchanges for s in e.scratch])(*[a for e in exchanges for a in e.arrays])
    out, lo = [], 0
    for c in counts:
        out.append(list(res[lo:lo + c]))
        lo += c
    return out


def _call(body, *, name, grid, in_specs, out_specs, out_shape, args, scratch_shapes=(), carry=None, aliases=None):
    cp = _params(len(grid))
    aliases = aliases or {}
    if carry is None:
        return tuple(pl.pallas_call(body, name=name, grid=grid, in_specs=in_specs, out_specs=out_specs, out_shape=out_shape,
                                    scratch_shapes=list(scratch_shapes), input_output_aliases=aliases,
                                    compiler_params=cp)(*args))
    n_in, n_out, n_sc, n_c = len(in_specs), len(out_specs), len(scratch_shapes), len(carry.arrays)

    def wrapped(*refs):
        ins, src = refs[:n_in], refs[n_in:n_in + n_c]
        outs = refs[n_in + n_c:n_in + n_c + n_out]
        dst = refs[n_in + n_c + n_out:n_in + 2 * n_c + n_out]
        rest = refs[n_in + 2 * n_c + n_out:]
        scr, sems = rest[:n_sc], rest[n_sc:]
        first = pl.program_id(0) == 0
        last = pl.program_id(0) == grid[0] - 1
        for ax in range(1, len(grid)):
            first = jnp.logical_and(first, pl.program_id(ax) == 0)
            last = jnp.logical_and(last, pl.program_id(ax) == grid[ax] - 1)

        @pl.when(first)
        def _():
            carry.start(src, dst, sems)

        body(*ins, *outs, *scr)

        @pl.when(last)
        def _():
            carry.wait(src, dst, sems)

    res = pl.pallas_call(
        wrapped, name=name, grid=grid, in_specs=list(in_specs) + [ANY] * n_c, out_specs=list(out_specs) + [ANY] * n_c,
        out_shape=list(out_shape) + carry.out_shape, scratch_shapes=list(scratch_shapes) + carry.scratch,
        input_output_aliases=aliases, compiler_params=cp)(*args, *carry.arrays)
    return tuple(res[:n_out]) + (list(res[n_out:]),)


def fwd_in(x, g, sc, sh, w, bias, *, blocked, tm, name, carry=None, wt=False):
    t = x.shape[0]
    nb, bw = (w.shape[0], w.shape[1]) if wt else (w.shape[0], w.shape[2])
    per = next(k for k in (1, 2, 4, 8) if (k * bw) % MXU_COLS_V7X == 0)
    assert not wt or (blocked and bias is None and nb % per == 0)

    def body(*refs):
        if bias is None:
            x_ref, g_ref, sc_ref, sh_ref, w_ref, h_ref, p_ref = refs
        else:
            x_ref, g_ref, sc_ref, sh_ref, w_ref, b_ref, h_ref, p_ref = refs
        hb = _adaln(x_ref[...], g_ref[...], sc_ref[...], sh_ref[...]).astype(BF)
        h_ref[...] = hb
        if wt:
            for c in range(nb // per):
                y = _dot_nt(hb, w_ref[c * per:(c + 1) * per].reshape(per * bw, D))
                for d in range(per):
                    p_ref[c * per + d] = y[:, d * bw:(d + 1) * bw].astype(BF)
            return
        for d in range(nb):
            y = _dot(hb, w_ref[d])
            if bias is not None:
                y = y + b_ref[d]
            if blocked:
                p_ref[d] = y.astype(BF)
            else:
                p_ref[:, d * bw:(d + 1) * bw] = y.astype(BF)

    vec = _const((1, D))
    in_specs = [_rows(tm, D), vec, vec, vec, _const(w.shape)]
    args = [x, g, sc, sh, w]
    if bias is not None:
        in_specs.append(_const((nb, 1, bw)))
        args.append(bias)
    if blocked:
        p_spec, p_shape = _brows(nb, tm, bw), jax.ShapeDtypeStruct((nb, t, bw), BF)
    else:
        p_spec, p_shape = _rows(tm, nb * bw), jax.ShapeDtypeStruct((t, nb * bw), BF)
    return _call(body, name=name, grid=(t // tm,), in_specs=in_specs, out_specs=[_rows(tm, D), p_spec],
                 out_shape=[jax.ShapeDtypeStruct((t, D), BF), p_shape], args=args, carry=carry)


def _sc_conv(p_ref, ph_ref, cw_ref, first, tm):
    z = p_ref[:, D:2 * D].astype(F32) * p_ref[:, 2 * D:3 * D].astype(F32)
    zp = jnp.where(first, 0.0, ph_ref[8:16, D:2 * D].astype(F32) * ph_ref[8:16, 2 * D:3 * D].astype(F32))
    ext = jnp.concatenate([zp, z], axis=0)
    return cw_ref[0:1, :] * ext[6:6 + tm] + cw_ref[1:2, :] * ext[7:7 + tm] + cw_ref[2:3, :] * z


def sc_fwd_out(p, convw, w_out, x, gn, gt, *, tm, name, carry=None):
    t = x.shape[0]

    def body(p_ref, ph_ref, cw_ref, w_ref, x_ref, gn_ref, gt_ref, x1_ref, m_ref, q_ref):
        u = _sc_conv(p_ref, ph_ref, cw_ref, pl.program_id(0) == 0, tm)
        qb = (p_ref[:, 0:D].astype(F32) * u).astype(BF)
        q_ref[...] = qb
        m = _dot(qb, w_ref[...])
        m_ref[...] = m.astype(BF)
        x1_ref[...] = _gated_res(x_ref[...], m, gn_ref[...], gt_ref[...])

    vec = _const((1, D))
    return _call(
        body, name=name, grid=(t // tm,),
        in_specs=[_rows(tm, 3 * D), _prev(16, 3 * D, tm), _const((3, D)), _const((D, D)), _rows(tm, D), vec, vec],
        out_specs=[_rows(tm, D)] * 3,
        out_shape=[jax.ShapeDtypeStruct((t, D), F32), jax.ShapeDtypeStruct((t, D), BF), jax.ShapeDtypeStruct((t, D), BF)],
        args=[p, p, convw, w_out, x, gn, gt], carry=carry)


def _layernorm_parts(u2):
    mu = jnp.mean(u2, axis=-1, keepdims=True)
    cen = u2 - mu
    rstd = lax.rsqrt(jnp.mean(cen * cen, axis=-1, keepdims=True) + LN_EPS)
    return cen * rstd, rstd


def cf_fwd_out(a, w_dw, b_dw, ln_g, ln_b, w_pw2, b_pw2, x, gn, gt, *, tm, name):
    t = x.shape[0]
    hb = 32

    def body(a_ref, ah_ref, wd_ref, bd_ref, lg_ref, lb_ref, w_ref, b2_ref, x_ref, gn_ref, gt_ref,
             x1_ref, m_ref, s_out_ref, u2_ref, s_ref):
        i = pl.program_id(0)
        uh = ah_ref[:, 0:D].astype(F32) * _sigmoid(ah_ref[:, D:2 * D].astype(F32))
        s_ref[0, 0:hb, :] = jnp.where(i == 0, 0.0, uh)
        s_ref[0, hb:hb + tm, :] = a_ref[:, 0:D].astype(F32) * _sigmoid(a_ref[:, D:2 * D].astype(F32))
        _row_shifted_copies(s_ref, tm + hb - 8)
        acc = bd_ref[...] + wd_ref[0:1, :] * _shifted(s_ref, hb - CFW + 1, tm)
        for k in range(1, CFW):
            acc = acc + wd_ref[k:k + 1, :] * _shifted(s_ref, hb - CFW + 1 + k, tm)
        u2_ref[...] = acc.astype(BF)
        xh, _ = _layernorm_parts(acc)
        l = xh * lg_ref[...] + lb_ref[...]
        sb = (l * _sigmoid(l)).astype(BF)
        s_out_ref[...] = sb
        m = _dot(sb, w_ref[...]) + b2_ref[...]
        m_ref[...] = m.astype(BF)
        x1_ref[...] = _gated_res(x_ref[...], m, gn_ref[...], gt_ref[...])

    vec = _const((1, D))
    return pl.pallas_call(
        body, name=name, grid=(t // tm,),
        in_specs=[_rows(tm, 2 * D), _prev(hb, 2 * D, tm), _const((CFW, D)), vec, vec, vec, _const((D, D)), vec,
                  _rows(tm, D), vec, vec],
        out_specs=[_rows(tm, D)] * 4,
        out_shape=[jax.ShapeDtypeStruct((t, D), F32)] + [jax.ShapeDtypeStruct((t, D), BF)] * 3,
        scratch_shapes=[pltpu.VMEM((8, tm + hb, D), F32)], compiler_params=_params(1),
    )(a, a, w_dw, b_dw, ln_g, ln_b, w_pw2, b_pw2, x, gn, gt)


def _pool_counts(i, tm, w):
    row = lax.broadcasted_iota(jnp.int32, (tm, 1), 0) + i * tm
    return jnp.minimum(row + 1, w).astype(F32)


def pool_fwd(x, g, sc, sh, pw, pb, pscale, gn, gt, *, tm, name):
    t = x.shape[0]
    pad, hb = 8, 16
    base = pad + hb

    def body(x_ref, xh_ref, g_ref, sc_ref, sh_ref, pw_ref, pb_ref, ps_ref, gn_ref, gt_ref,
             x1_ref, m_ref, yp_ref, po_ref, sa_ref, sb_ref):
        i = pl.program_id(0)
        hh = _adaln(xh_ref[...], g_ref[...], sc_ref[...], sh_ref[...])
        h = _adaln(x_ref[...], g_ref[...], sc_ref[...], sh_ref[...])
        zero = jnp.zeros((pad, D), F32)
        sa_ref[0:pad, :] = zero
        sb_ref[0:pad, :] = zero
        sa_ref[pad:base, :] = jnp.where(i == 0, 0.0, hh)
        sa_ref[base:base + tm, :] = h
        n = hb + tm
        src, dst = sa_ref, sb_ref
        ys = []
        for gi, w in enumerate(POOL_WINDOWS):
            c0 = gi * PG
            step = w // 2
            dst[pl.ds(pad, n), c0:D] = src[pl.ds(pad, n), c0:D] + src[pl.ds(pad - step, n), c0:D]
            mean = dst[pl.ds(base, tm), c0:c0 + PG] / _pool_counts(i, tm, w)
            pooled = (mean - h[:, c0:c0 + PG]).astype(BF)
            po_ref[:, c0:c0 + PG] = pooled
            ys.append(_dot(pooled, pw_ref[gi]))
            src, dst = dst, src
        ypre = jnp.concatenate(ys, axis=1) + pb_ref[...]
        yp_ref[...] = ypre.astype(BF)
        m = ypre * ps_ref[...]
        m_ref[...] = m.astype(BF)
        x1_ref[...] = _gated_res(x_ref[...], m, gn_ref[...], gt_ref[...])

    vec = _const((1, D))
    return pl.pallas_call(
        body, name=name, grid=(t // tm,),
        in_specs=[_rows(tm, D), _prev(hb, D, tm), vec, vec, vec, _const((4, PG, PG)), vec, vec, vec, vec],
        out_specs=[_rows(tm, D)] * 4,
        out_shape=[jax.ShapeDtypeStruct((t, D), F32)] + [jax.ShapeDtypeStruct((t, D), BF)] * 3,
        scratch_shapes=[pltpu.VMEM((tm + base, D), F32)] * 2, compiler_params=_params(1),
    )(x, x, g, sc, sh, pw, pb, pscale, gn, gt)


def ffn_fwd_out(up, convw, convb, w_down, x, gn, gt, *, tm, name, carry=None, target=None):
    t = x.shape[0]
    nt = t // tm

    def body(gate_ref, gh_ref, val_ref, cw_ref, cb_ref, w_ref, x_ref, gn_ref, gt_ref, *rest):
        if target is None:
            x2_ref, f_ref, gc_ref, a_ref = rest
        else:
            tg_ref, x2_ref, f_ref, gc_ref, a_ref, l_ref, lacc_ref = rest
        i = pl.program_id(0)
        acc = jnp.zeros((tm, D), F32)
        for j in range(4):
            gate = gate_ref[j].astype(F32)
            ext = jnp.concatenate([jnp.where(i == 0, 0.0, gh_ref[j, 8:16, :].astype(F32)), gate], axis=0)
            gc = cb_ref[j] + cw_ref[j, 0:1, :] * ext[6:6 + tm] + cw_ref[j, 1:2, :] * ext[7:7 + tm] + cw_ref[j, 2:3, :] * gate
            gc_ref[j] = gc.astype(BF)
            ab = (gc * _sigmoid(gc) * val_ref[j].astype(F32)).astype(BF)
            a_ref[j] = ab
            acc = acc + _dot(ab, w_ref[j * FB:(j + 1) * FB, :])
        f_ref[...] = acc.astype(BF)
        x2 = _gated_res(x_ref[...], acc, gn_ref[...], gt_ref[...])
        if target is None:
            x2_ref[...] = x2
        else:
            @pl.when(i == 0)
            def _():
                lacc_ref[...] = jnp.zeros_like(lacc_ref)

            e = x2 - tg_ref[...]
            x2_ref[...] = e * (1.0 / D)
            lacc_ref[...] += _rsum(e * e)

            @pl.when(i == nt - 1)
            def _():
                l_ref[...] = jnp.sum(lacc_ref[...], axis=1, keepdims=True) * (0.5 / D)

    vec = _const((1, D))
    blk = jax.ShapeDtypeStruct((4, t, FB), BF)
    in_specs = [_brows(4, tm, FB, 0), _bprev(4, 16, FB, tm, 0), _brows(4, tm, FB, 1), _const((4, 3, FB)),
                _const((4, 1, FB)), _const((F, D)), _rows(tm, D), vec, vec]
    out_specs = [_rows(tm, D), _rows(tm, D), _brows(4, tm, FB), _brows(4, tm, FB)]
    out_shape = [jax.ShapeDtypeStruct((t, D), F32), jax.ShapeDtypeStruct((t, D), BF), blk, blk]
    args = [up, up, up, convw, convb, w_down, x, gn, gt]
    scratch = []
    if target is not None:
        in_specs.append(_rows(tm, D))
        args.append(target)
        out_specs.append(pl.BlockSpec((1, 1), lambda i: (0, 0)))
        out_shape.append(jax.ShapeDtypeStruct((1, 1), F32))
        scratch.append(pltpu.VMEM((1, D), F32))
    return _call(body, name=name, grid=(nt,), in_specs=in_specs, out_specs=out_specs, out_shape=out_shape, args=args,
                 scratch_shapes=scratch, carry=carry)


def _init_stats(ref):
    @pl.when(pl.program_id(0) == 0)
    def _():
        ref[...] = jnp.zeros_like(ref)


def bwd_out(dxo, m, gn, gt, w, *, tm, name, carry=None):
    t = dxo.shape[0]
    k = w.shape[0]

    def body(dx_ref, m_ref, gn_ref, gt_ref, w_ref, dm_ref, da_ref, st_ref):
        _init_stats(st_ref)
        dm, dgt, dgn = _gated_res_bwd(dx_ref[...], m_ref[...].astype(F32), gn_ref[...], gt_ref[...])
        st_ref[0:1, :] += dgt
        st_ref[1:2, :] += dgn
        st_ref[2:3, :] += _rsum(dm)
        dmb = dm.astype(BF)
        dm_ref[...] = dmb
        da_ref[...] = _dot_nt(dmb, w_ref[...]).astype(BF)

    vec = _const((1, D))
    da_spec, da_shape = _rows(tm, k), jax.ShapeDtypeStruct((t, k), BF)
    return _call(
        body, name=name, grid=(t // tm,), in_specs=[_rows(tm, D), _rows(tm, D), vec, vec, _const((k, D))],
        out_specs=[_rows(tm, D), da_spec, _const((8, D), single=False)],
        out_shape=[jax.ShapeDtypeStruct((t, D), BF), da_shape, jax.ShapeDtypeStruct((8, D), F32)],
        args=[dxo, m, gn, gt, w], carry=carry)


def _adaln_bwd(dh, x_ref, g_ref, sc_ref, dxo_ref, st_ref, row=0):
    x = x_ref[...]
    r = lax.rsqrt(jnp.mean(x * x, axis=-1, keepdims=True) + RMS_EPS)
    xh = x * r
    gv = g_ref[...]
    st_ref[row:row + 1, :] += _rsum(dh)
    st_ref[row + 1:row + 2, :] += _rsum(dh * (xh * gv))
    dn = dh * (1.0 + sc_ref[...])
    st_ref[row + 2:row + 3, :] += _rsum(dn * xh)
    dy = dn * gv
    return dxo_ref[...] + r * (dy - xh * jnp.mean(dy * xh, axis=-1, keepdims=True))


def bwd_in(dp, w, x, g, sc, dxo, *, tm, name):
    t = x.shape[0]
    nb, _, bw = w.shape

    def body(dp_ref, w_ref, x_ref, g_ref, sc_ref, dxo_ref, dx_ref, st_ref):
        _init_stats(st_ref)
        dh = jnp.zeros((tm, D), F32)
        for d in range(nb):
            dh = dh + _dot_nt(dp_ref[:, d * bw:(d + 1) * bw], w_ref[d])
        dx_ref[...] = _adaln_bwd(dh, x_ref, g_ref, sc_ref, dxo_ref, st_ref)

    vec = _const((1, D))
    return pl.pallas_call(
        body, name=name, grid=(t // tm,),
        in_specs=[_rows(tm, nb * bw), _const(w.shape), _rows(tm, D), vec, vec, _rows(tm, D)],
        out_specs=[_rows(tm, D), _const((8, D), single=False)],
        out_shape=[jax.ShapeDtypeStruct((t, D), F32), jax.ShapeDtypeStruct((8, D), F32)],
        compiler_params=_params(1))(dp, w, x, g, sc, dxo)


def sc_bwd_out(dxo, m, gn, gt, w_out, p, convw, *, tm, name, carry=None):
    t = dxo.shape[0]

    def body(dx_ref, m_ref, gn_ref, gt_ref, w_ref, p_ref, ph_ref, cw_ref, dm_ref, dbg_ref, du_ref, st_ref):
        _init_stats(st_ref)
        dm, dgt, dgn = _gated_res_bwd(dx_ref[...], m_ref[...].astype(F32), gn_ref[...], gt_ref[...])
        st_ref[0:1, :] += dgt
        st_ref[1:2, :] += dgn
        dmb = dm.astype(BF)
        dm_ref[...] = dmb
        dq = _dot_nt(dmb, w_ref[...])
        dbg_ref[...] = (dq * _sc_conv(p_ref, ph_ref, cw_ref, pl.program_id(0) == 0, tm)).astype(BF)
        du_ref[...] = (dq * p_ref[:, 0:D].astype(F32)).astype(BF)

    vec = _const((1, D))
    out = jax.ShapeDtypeStruct((t, D), BF)
    return _call(
        body, name=name, grid=(t // tm,),
        in_specs=[_rows(tm, D), _rows(tm, D), vec, vec, _const((D, D)), _rows(tm, 3 * D), _prev(16, 3 * D, tm), _const((3, D))],
        out_specs=[_rows(tm, D)] * 3 + [_const((8, D), single=False)], out_shape=[out, out, out, jax.ShapeDtypeStruct((8, D), F32)],
        args=[dxo, m, gn, gt, w_out, p, p, convw], carry=carry)


def sc_bwd_in(du, dbg, p, convw, w, x, g, sc, dxo, *, tm, name, carry=None):
    t = x.shape[0]

    def body(du_ref, dun_ref, dbg_ref, p_ref, cw_ref, w_ref, x_ref, g_ref, sc_ref, dxo_ref, dx_ref, dp_ref, st_ref, sc2_ref):
        last = pl.program_id(0) == pl.num_programs(0) - 1
        _init_stats(st_ref)
        _init_stats(sc2_ref)
        du = du_ref[...].astype(F32)
        ext = jnp.concatenate([du, jnp.where(last, 0.0, dun_ref[0:8, :].astype(F32))], axis=0)
        e1, e2 = ext[1:tm + 1], ext[2:tm + 2]
        dz = cw_ref[2:3, :] * du + cw_ref[1:2, :] * e1 + cw_ref[0:1, :] * e2
        cg, hi = p_ref[:, D:2 * D].astype(F32), p_ref[:, 2 * D:3 * D].astype(F32)
        dbg, dcg, dhi = dbg_ref[...], (dz * hi).astype(BF), (dz * cg).astype(BF)
        dp_ref[:, 0:D] = dbg
        dp_ref[:, D:2 * D] = dcg
        dp_ref[:, 2 * D:3 * D] = dhi
        dh = _dot_nt(dbg, w_ref[0]) + _dot_nt(dcg, w_ref[1]) + _dot_nt(dhi, w_ref[2])
        z = cg * hi
        sc2_ref[0:1, :] += _rsum(z * e2)
        sc2_ref[1:2, :] += _rsum(z * e1)
        sc2_ref[2:3, :] += _rsum(z * du)
        dx_ref[...] = _adaln_bwd(dh, x_ref, g_ref, sc_ref, dxo_ref, st_ref)

    vec = _const((1, D))
    stat = jax.ShapeDtypeStruct((8, D), F32)
    return _call(
        body, name=name, grid=(t // tm,),
        in_specs=[_rows(tm, D), _next(16, D, tm, t), _rows(tm, D), _rows(tm, 3 * D), _const((3, D)), _const(w.shape),
                  _rows(tm, D), vec, vec, _rows(tm, D)],
        out_specs=[_rows(tm, D), _rows(tm, 3 * D), _const((8, D), single=False), _const((8, D), single=False)],
        out_shape=[jax.ShapeDtypeStruct((t, D), F32), jax.ShapeDtypeStruct((t, 3 * D), BF), stat, stat],
        args=[du, du, dbg, p, convw, w, x, g, sc, dxo], carry=carry)


def cf_bwd_mid(ds, u2, a, w_dw, ln_g, ln_b, *, tm, name, carry=None):
    t = ds.shape[0]
    hb = 32

    def du2_of(dsv, u2v, lg, lb):
        xh, rstd = _layernorm_parts(u2v)
        l = xh * lg + lb
        sg = _sigmoid(l)
        dl = dsv * (sg * (1.0 + l * (1.0 - sg)))
        dxh = dl * lg
        du2 = rstd * (dxh - jnp.mean(dxh, axis=-1, keepdims=True) - xh * jnp.mean(dxh * xh, axis=-1, keepdims=True))
        return du2, dl, xh

    rc, cc = 32, 256

    def body(ds_ref, dsn_ref, u2_ref, u2n_ref, a_ref, wd_ref, lg_ref, lb_ref, da_ref, st_ref, s1_ref, acc_ref):
        i = pl.program_id(0)
        last = i == pl.num_programs(0) - 1
        _init_stats(st_ref)
        _init_stats(acc_ref)
        lg, lb = lg_ref[...], lb_ref[...]
        du2, dl, xh = du2_of(ds_ref[...].astype(F32), u2_ref[...].astype(F32), lg, lb)
        st_ref[32:33, :] += _rsum(dl * xh)
        st_ref[33:34, :] += _rsum(dl)
        st_ref[31:32, :] += _rsum(du2)
        du2n, _, _ = du2_of(dsn_ref[...].astype(F32), u2n_ref[...].astype(F32), lg, lb)
        s1_ref[0, 0:tm, :] = du2
        s1_ref[0, tm:tm + hb, :] = jnp.where(last, 0.0, du2n)
        _row_shifted_copies(s1_ref, tm + hb - 8)

        def taps(r0, _):
            rows = pl.ds(r0, rc)
            for c0 in range(0, D, cc):
                sg = _sigmoid(a_ref[rows, D + c0:D + c0 + cc].astype(F32))
                u = a_ref[rows, c0:c0 + cc].astype(F32) * sg
                du = jnp.zeros((rc, cc), F32)
                for k in range(CFW):
                    o = CFW - 1 - k
                    sh = s1_ref[o % 8, pl.ds(pl.multiple_of(r0 + 8 * (o // 8), 8), rc), c0:c0 + cc]
                    du = du + wd_ref[k:k + 1, c0:c0 + cc] * sh
                    acc_ref[8 * k:8 * k + 8, c0:c0 + cc] += _fold8(u * sh)
                dav = du * sg
                dgv = du * u * (1.0 - sg)
                acc_ref[8 * CFW:8 * CFW + 8, c0:c0 + cc] += _fold8(dav)
                acc_ref[8 * CFW + 8:8 * CFW + 16, c0:c0 + cc] += _fold8(dgv)
                da_ref[rows, c0:c0 + cc] = dav.astype(BF)
                da_ref[rows, D + c0:D + c0 + cc] = dgv.astype(BF)
            return 0

        _chunks(tm, rc, taps)

        @pl.when(last)
        def _():
            for k in range(CFW):
                st_ref[k:k + 1, :] = jnp.sum(acc_ref[8 * k:8 * k + 8, :], axis=0, keepdims=True)
            st_ref[34:35, :] = jnp.sum(acc_ref[8 * CFW:8 * CFW + 8, :], axis=0, keepdims=True)
            st_ref[35:36, :] = jnp.sum(acc_ref[8 * CFW + 8:8 * CFW + 16, :], axis=0, keepdims=True)

    vec = _const((1, D))
    return _call(
        body, name=name, grid=(t // tm,),
        in_specs=[_rows(tm, D), _next(hb, D, tm, t), _rows(tm, D), _next(hb, D, tm, t), _rows(tm, 2 * D),
                  _const((CFW, D)), vec, vec],
        out_specs=[_rows(tm, 2 * D), _const((40, D), single=False)],
        out_shape=[jax.ShapeDtypeStruct((t, 2 * D), BF), jax.ShapeDtypeStruct((40, D), F32)],
        scratch_shapes=[pltpu.VMEM((8, tm + hb, D), F32), pltpu.VMEM((8 * (CFW + 2), D), F32)],
        args=[ds, ds, u2, u2, a, w_dw, ln_g, ln_b], carry=carry)


def pool_bwd(dxo, m, ypre, pw, pscale, gn, gt, x, g, sc, *, tm, name):
    t = dxo.shape[0]
    hb = 16

    def dyp_of(dxv, mv, ypv, ps, gnv, gtv):
        dm, dgt, dgn = _gated_res_bwd(dxv, mv, gnv, gtv)
        return dm * ps, dgt, dgn, _rsum(dm * ypv)

    def body(dx_ref, dxn_ref, m_ref, mn_ref, yp_ref, ypn_ref, pw_ref, ps_ref, gn_ref, gt_ref, x_ref, g_ref, sc_ref,
             dxi_ref, dyp_ref, st_ref, sa_ref, sb_ref):
        i = pl.program_id(0)
        last = i == pl.num_programs(0) - 1
        _init_stats(st_ref)
        ps, gnv, gtv = ps_ref[...], gn_ref[...], gt_ref[...]
        dyp, dgt, dgn, dps = dyp_of(dx_ref[...], m_ref[...].astype(F32), yp_ref[...].astype(F32), ps, gnv, gtv)
        st_ref[0:1, :] += dgt
        st_ref[1:2, :] += dgn
        st_ref[2:3, :] += dps
        st_ref[3:4, :] += _rsum(dyp)
        dypb = dyp.astype(BF)
        dyp_ref[...] = dypb
        dypn, _, _, _ = dyp_of(dxn_ref[...], mn_ref[...].astype(F32), ypn_ref[...].astype(F32), ps, gnv, gtv)
        dypnb = jnp.where(last, 0.0, dypn).astype(BF)
        dpo = []
        for gi, w in enumerate(POOL_WINDOWS):
            c0 = gi * PG
            dp_main = _dot_nt(dypb[:, c0:c0 + PG], pw_ref[gi])
            dp_next = _dot_nt(dypnb[:, c0:c0 + PG], pw_ref[gi])
            dpo.append(dp_main)
            sa_ref[0:tm, c0:c0 + PG] = dp_main / _pool_counts(i, tm, w)
            sa_ref[tm:tm + hb, c0:c0 + PG] = dp_next / float(w)
        zero = jnp.zeros((8, D), F32)
        sa_ref[tm + hb:tm + hb + 8, :] = zero
        sb_ref[tm + hb:tm + hb + 8, :] = zero
        n = tm + hb
        src, dst = sa_ref, sb_ref
        dhs = []
        for gi, w in enumerate(POOL_WINDOWS):
            c0 = gi * PG
            step = w // 2
            dst[pl.ds(0, n), c0:D] = src[pl.ds(0, n), c0:D] + src[pl.ds(step, n), c0:D]
            dhs.append(dst[pl.ds(0, tm), c0:c0 + PG] - dpo[gi])
            src, dst = dst, src
        dxi_ref[...] = _adaln_bwd(jnp.concatenate(dhs, axis=1), x_ref, g_ref, sc_ref, dx_ref, st_ref, row=4)

    vec = _const((1, D))
    return pl.pallas_call(
        body, name=name, grid=(t // tm,),
        in_specs=[_rows(tm, D), _next(hb, D, tm, t), _rows(tm, D), _next(hb, D, tm, t), _rows(tm, D),
                  _next(hb, D, tm, t), _const((4, PG, PG)), vec, vec, vec, _rows(tm, D), vec, vec],
        out_specs=[_rows(tm, D), _rows(tm, D), _const((8, D), single=False)],
        out_shape=[jax.ShapeDtypeStruct((t, D), F32), jax.ShapeDtypeStruct((t, D), BF), jax.ShapeDtypeStruct((8, D), F32)],
        scratch_shapes=[pltpu.VMEM((tm + hb + 8, D), F32)] * 2, compiler_params=_params(1),
    )(dxo, dxo, m, m, ypre, ypre, pw, pscale, gn, gt, x, g, sc)


def ffn_bwd_out(dxo, f, gn, gt, w_down, gc, up, *, tm, name, carry=None):
    t = dxo.shape[0]

    def body(dx_ref, f_ref, gn_ref, gt_ref, w_ref, gc_ref, val_ref, df_ref, dgc_ref, dval_ref, st_ref, sc_ref):
        _init_stats(st_ref)
        _init_stats(sc_ref)
        dm, dgt, dgn = _gated_res_bwd(dx_ref[...], f_ref[...].astype(F32), gn_ref[...], gt_ref[...])
        st_ref[0:1, :] += dgt
        st_ref[1:2, :] += dgn
        dmb = dm.astype(BF)
        df_ref[...] = dmb
        da_all = _dot_nt(dmb, w_ref[...])
        for j in range(4):
            da = da_all[:, j * FB:(j + 1) * FB]
            gcv = gc_ref[j].astype(F32)
            sg = _sigmoid(gcv)
            dval_ref[j] = (da * (gcv * sg)).astype(BF)
            dgc = da * val_ref[j].astype(F32) * (sg * (1.0 + gcv * (1.0 - sg)))
            dgc_ref[j] = dgc.astype(BF)
            sc_ref[j, 0:1, :] += _rsum(dgc)

    vec = _const((1, D))
    return _call(
        body, name=name, grid=(t // tm,),
        in_specs=[_rows(tm, D), _rows(tm, D), vec, vec, _const((F, D)), _brows(4, tm, FB), _brows(4, tm, FB, 1)],
        out_specs=[_rows(tm, D), _brows(4, tm, FB), _brows(4, tm, FB, 1), _const((8, D), single=False),
                   _const((4, 8, FB), single=False)],
        out_shape=[jax.ShapeDtypeStruct((t, D), BF), jax.ShapeDtypeStruct((4, t, FB), BF), jax.ShapeDtypeStruct((8, t, FB), BF),
                   jax.ShapeDtypeStruct((8, D), F32), jax.ShapeDtypeStruct((4, 8, FB), F32)],
        args=[dxo, f, gn, gt, w_down, gc, up], carry=carry)


def ffn_bwd_in(dgc, dup, up, convw, w, x, g, sc, dxo, *, tm, name, carry=None):
    t = x.shape[0]

    def body(dgc_ref, dgcn_ref, dval_ref, gate_ref, cw_ref, w_ref, x_ref, g_ref, sc_ref, dxo_ref,
             dx_ref, dgate_ref, st_ref, sc2_ref):
        last = pl.program_id(0) == pl.num_programs(0) - 1
        _init_stats(st_ref)
        _init_stats(sc2_ref)
        dh = jnp.zeros((tm, D), F32)
        for j in range(4):
            dgc = dgc_ref[j].astype(F32)
            ext = jnp.concatenate([dgc, jnp.where(last, 0.0, dgcn_ref[j, 0:8, :].astype(F32))], axis=0)
            e1, e2 = ext[1:tm + 1], ext[2:tm + 2]
            dgate = (cw_ref[j, 2:3, :] * dgc + cw_ref[j, 1:2, :] * e1 + cw_ref[j, 0:1, :] * e2).astype(BF)
            dgate_ref[j] = dgate
            dh = dh + _dot(dgate, w_ref[j])
            gate = gate_ref[j].astype(F32)
            sc2_ref[j, 1:2, :] += _rsum(gate * e2)
            sc2_ref[j, 2:3, :] += _rsum(gate * e1)
            sc2_ref[j, 3:4, :] += _rsum(gate * dgc)
        for j in range(4):
            dh = dh + _dot(dval_ref[j], w_ref[4 + j])
        dx_ref[...] = _adaln_bwd(dh, x_ref, g_ref, sc_ref, dxo_ref, st_ref)

    vec = _const((1, D))
    return _call(
        body, name=name, grid=(t // tm,),
        in_specs=[_brows(4, tm, FB), _bnext(4, 16, FB, tm, t), _brows(4, tm, FB, 1), _brows(4, tm, FB, 0), _const((4, 3, FB)),
                  _const(w.shape), _rows(tm, D), vec, vec, _rows(tm, D)],
        out_specs=[_rows(tm, D), _brows(4, tm, FB, 0), _const((8, D), single=False), _const((4, 8, FB), single=False)],
        out_shape=[jax.ShapeDtypeStruct((t, D), F32), jax.ShapeDtypeStruct((8, t, FB), BF), jax.ShapeDtypeStruct((8, D), F32),
                   jax.ShapeDtypeStruct((4, 8, FB), F32)],
        args=[dgc, dgc, dup, up, convw, w, x, g, sc, dxo], aliases={2: 1}, carry=carry)


def wgrad(a, b, *, nblk, a_blocked, b_blocked, bk, bn, tt, name, carry=None):
    t = a.shape[1] if a.ndim == 3 else a.shape[0]
    nt = t // tt

    def body(a_ref, b_ref, o_ref, acc_ref):
        s = pl.program_id(1)

        @pl.when(s == 0)
        def _():
            acc_ref[...] = jnp.zeros_like(acc_ref)

        av = a_ref[0] if a.ndim == 3 else a_ref[...]
        bv = b_ref[0] if b.ndim == 3 else b_ref[...]
        acc_ref[...] += _dot_tn(av, bv)

        @pl.when(s == nt - 1)
        def _():
            o_ref[0] = acc_ref[...].astype(BF)

    def spec(arr, blocked, width):
        if arr.ndim == 3:
            return pl.BlockSpec((1, tt, width), lambda j, s: (j, s, 0))
        if blocked:
            return pl.BlockSpec((tt, width), lambda j, s: (s, j))
        return pl.BlockSpec((tt, width), lambda j, s: (s, 0))

    return _call(
        body, name=name, grid=(nblk, nt), in_specs=[spec(a, a_blocked, bk), spec(b, b_blocked, bn)],
        out_specs=[pl.BlockSpec((1, bk, bn), lambda j, s: (j, 0, 0))],
        out_shape=[jax.ShapeDtypeStruct((nblk, bk, bn), BF)],
        scratch_shapes=[pltpu.VMEM((bk, bn), F32)], args=[a, b], carry=carry)


def mod_partial(c_all, w_mod):
    cols = w_mod.shape[2]

    def body(c_ref, w_ref, o_ref):
        c = c_ref[...]
        ca = c * _sigmoid(c)
        o_ref[0] = jnp.dot(ca, w_ref[0], preferred_element_type=F32, precision=lax.Precision.HIGHEST)

    return pl.pallas_call(
        body, name="mod_partial", grid=(DEPTH,),
        in_specs=[pl.BlockSpec((NDEV, D), lambda l: (0, 0)), pl.BlockSpec((1, D, cols), lambda l: (l, 0, 0))],
        out_specs=pl.BlockSpec((1, NDEV, cols), lambda l: (l, 0, 0)),
        out_shape=jax.ShapeDtypeStruct((DEPTH, NDEV, cols), F32), compiler_params=_params(1))(c_all, w_mod)


def mod_finish(parts, b_mod):
    cols = parts.shape[2]

    def body(p_ref, b_ref, o_ref):
        for e in range(NDEV):
            o_ref[:, e * cols:(e + 1) * cols] = p_ref[e] + b_ref[:, e * cols:(e + 1) * cols]

    return pl.pallas_call(
        body, name="mod_finish", out_shape=jax.ShapeDtypeStruct((DEPTH, NDEV * cols), F32))(parts, b_mod)


def sum_parts(parts):
    n, r, c = parts.shape

    def body(p_ref, o_ref):
        acc = p_ref[0]
        for j in range(1, n):
            acc = acc + p_ref[j]
        o_ref[...] = acc

    return pl.pallas_call(body, name="sum_parts", out_shape=jax.ShapeDtypeStruct((r, c), F32))(parts)


def mod_wgrad(c_all_t, gmod_cols):
    cols = gmod_cols.shape[2]

    def body(c_ref, g_ref, o_ref):
        c = c_ref[...]
        ca = c * _sigmoid(c)
        acc = ca[:, 0:1] * g_ref[0, 0:1, :]
        for b in range(1, NDEV):
            acc = acc + ca[:, b:b + 1] * g_ref[0, b:b + 1, :]
        o_ref[0] = acc

    return pl.pallas_call(
        body, name="mod_wgrad", grid=(DEPTH,),
        in_specs=[pl.BlockSpec((D, NDEV), lambda l: (0, 0)), pl.BlockSpec((1, NDEV, cols), lambda l: (l, 0, 0))],
        out_specs=pl.BlockSpec((1, D, cols), lambda l: (l, 0, 0)),
        out_shape=jax.ShapeDtypeStruct((DEPTH, D, cols), F32), compiler_params=_params(1))(c_all_t, gmod_cols)


def _adamw_math(g, w, m, v):
    m2 = B1 * m + (1.0 - B1) * g
    v2 = B2 * v + (1.0 - B2) * (g * g)
    m_hat = m2 / (1.0 - B1 ** STEP)
    v_hat = v2 / (1.0 - B2 ** STEP)
    delta = -LR * (m_hat / (jnp.sqrt(v_hat) + ADAM_EPS) + WD * w)
    return delta, m2, v2


def _row_tile(r, c, budget=1 << 18):
    if r * c <= budget or r % 8:
        return r
    best = 8
    for cand in range(8, r + 1, 8):
        if r % cand == 0 and cand * c <= budget:
            best = cand
    return best


def adamw_sum(parts, w, m, v, *, name):
    n, r, c = parts.shape
    tr = _row_tile(r, c)

    def body(p_ref, w_ref, m_ref, v_ref, g_ref, d_ref, m2_ref, v2_ref):
        g = p_ref[0].astype(F32)
        for j in range(1, n):
            g = g + p_ref[j].astype(F32)
        d, m2, v2 = _adamw_math(g, w_ref[...], m_ref[...], v_ref[...])
        g_ref[...] = g
        d_ref[...] = d
        m2_ref[...] = m2
        v2_ref[...] = v2

    blk = pl.BlockSpec((tr, c), lambda i: (i, 0))
    out = jax.ShapeDtypeStruct((r, c), F32)
    return pl.pallas_call(
        body, name=name, grid=(r // tr,), in_specs=[pl.BlockSpec((n, tr, c), lambda i: (0, i, 0)), blk, blk, blk],
        out_specs=[blk] * 4, out_shape=[out] * 4, compiler_params=_params(1))(parts, w, m, v)


def adamw_layer(parts, w, m, v, prev, layer, *, name):
    n, r, c = parts.shape
    nl = w.shape[0]
    tr = _row_tile(r, c)

    def body(p_ref, w_ref, m_ref, v_ref, *rest):
        g_ref, d_ref, m2_ref, v2_ref = rest[-4:]
        g = p_ref[0].astype(F32)
        for j in range(1, n):
            g = g + p_ref[j].astype(F32)
        d, m2, v2 = _adamw_math(g, w_ref[0], m_ref[0], v_ref[0])
        g_ref[0] = g
        d_ref[0] = d
        m2_ref[0] = m2
        v2_ref[0] = v2

    blk = pl.BlockSpec((1, tr, c), lambda i: (layer, i, 0))
    in_specs = [pl.BlockSpec((n, tr, c), lambda i: (0, i, 0)), blk, blk, blk]
    args = [parts, w, m, v]
    aliases = {}
    if prev is not None:
        in_specs += [ANY] * 4
        args += list(prev)
        aliases = {4 + k: k for k in range(4)}
    out = jax.ShapeDtypeStruct((nl, r, c), F32)
    return pl.pallas_call(
        body, name=name, grid=(r // tr,), in_specs=in_specs, out_specs=[blk] * 4, out_shape=[out] * 4,
        input_output_aliases=aliases, compiler_params=_params(1))(*args)


def _pack(arrays):
    flat, layout, off = [], [], 0
    for a in arrays:
        flat.append(a.reshape(-1))
        layout.append((off, a.shape))
        off += a.size
    pad = (-off) % 1024
    if pad:
        flat.append(jnp.zeros((pad,), F32))
    return jnp.concatenate(flat).reshape(-1, 128), layout


def _unpack(packed, layout, lead=()):
    flat = packed.reshape(lead + (-1,))
    return [flat[..., off:off + _size(shape)].reshape(lead + tuple(shape)) for off, shape in layout]


def _size(shape):
    n = 1
    for s in shape:
        n *= s
    return n


def _join_last(g):
    g = jnp.moveaxis(g, 0, -2)
    return g.reshape(g.shape[:-2] + (g.shape[-2] * g.shape[-1],))


def _my_cols(a, width):
    return lax.dynamic_slice_in_dim(a, _my_id() * width, width, axis=a.ndim - 1)


def _tile(t, pref):
    return min(pref, t)


def kernel(x, c, w_mod, b_mod, norm_g, sc_w_in, sc_conv, sc_w_out, pool_w, pool_b, pool_scale, cf_w_pw1, cf_b_pw1, cf_w_dw, cf_b_dw, cf_ln_g, cf_ln_b, cf_w_pw2, cf_b_pw2, ffn_w_up, ffn_conv, ffn_b_conv, ffn_w_down, loss_target, m_w_mod, m_b_mod, m_norm_g, m_sc_w_in, m_sc_conv, m_sc_w_out, m_pool_w, m_pool_b, m_pool_scale, m_cf_w_pw1, m_cf_b_pw1, m_cf_w_dw, m_cf_b_dw, m_cf_ln_g, m_cf_ln_b, m_cf_w_pw2, m_cf_b_pw2, m_ffn_w_up, m_ffn_conv, m_ffn_b_conv, m_ffn_w_down, v_w_mod, v_b_mod, v_norm_g, v_sc_w_in, v_sc_conv, v_sc_w_out, v_pool_w, v_pool_b, v_pool_scale, v_cf_w_pw1, v_cf_b_pw1, v_cf_w_dw, v_cf_b_dw, v_cf_ln_g, v_cf_ln_b, v_cf_w_pw2, v_cf_b_pw2, v_ffn_w_up, v_ffn_conv, v_ffn_b_conv, v_ffn_w_down):
    env = dict(locals())
    names = ["w_mod", "b_mod", "norm_g", "sc_w_in", "sc_conv", "sc_w_out", "pool_w", "pool_b", "pool_scale", "cf_w_pw1",
             "cf_b_pw1", "cf_w_dw", "cf_b_dw", "cf_ln_g", "cf_ln_b", "cf_w_pw2", "cf_b_pw2", "ffn_w_up", "ffn_conv",
             "ffn_b_conv", "ffn_w_down"]
    t = x.shape[1]
    tm = _tile(t, 512)
    tt = _tile(t, 4096)
    x0, target = x[0], loss_target[0]

    small_names = ["norm_g", "sc_conv", "cf_b_pw1", "cf_w_dw", "cf_b_dw", "cf_ln_g", "cf_ln_b", "cf_b_pw2", "ffn_conv"]
    packed, layout = _pack([c] + [env[n] for n in small_names])

    shard = {"pool": pool_w[0].astype(BF), "pw1": cf_w_pw1[0].astype(BF), "pw2": cf_w_pw2[0].astype(BF)}
    for j in range(2):
        shard[f"in{j}"], shard[f"out{j}"] = sc_w_in[j].astype(BF), sc_w_out[j].astype(BF)
    for l in range(DEPTH):
        shard[f"up{l}"], shard[f"down{l}"] = ffn_w_up[l].T.astype(BF), ffn_w_down[l].astype(BF)
    gathered, g_in0, g_out0 = _Exchange("gather", [packed, shard["in0"], shard["out0"]]).run_two_level("gather_first")
    wg = {"in0": g_in0, "out0": g_out0}
    parts = _unpack(gathered, layout, lead=(NDEV,))
    c_all = parts[0].reshape(NDEV, D)
    full = {n: _join_last(p) for n, p in zip(small_names, parts[1:])}
    fwd_plan = {("mix_in", 0): ["up0"], ("mix_out", 0): ["down0"], ("ffn_in", 0): ["pool", "up1"],
                ("ffn_out", 0): ["down1", "pw1", "pw2"], ("ffn_in", 1): ["up2"], ("ffn_out", 1): ["down2", "in1", "out1"],
                ("ffn_in", 2): ["up3"], ("ffn_out", 2): ["down3"]}

    def carrying(plan, kind, store, source, fn, key, *a, **k):
        names = plan.get(key)
        if not names:
            return fn(*a, **k)
        res = fn(*a, carry=_Exchange(kind, [source[n] for n in names]), **k)
        store.update(zip(names, res[-1]))
        return res[:-1]

    fwd = functools.partial(carrying, fwd_plan, "gather", wg, shard)

    mp = mod_partial(c_all, w_mod)
    (mod_parts,) = _Exchange("scatter", [jnp.swapaxes(mp, 0, 1)]).run("exchange_mod")
    mod = mod_finish(mod_parts, b_mod)

    def vec(a):
        return a.reshape(1, -1)

    def col_blocks(g):
        w = jnp.swapaxes(g, 0, 1).reshape(D, -1)
        return jnp.swapaxes(w.reshape(D, -1, D), 0, 1)

    def ffn_blocks(a):
        return jnp.swapaxes(a.reshape(a.shape[0], 4, FB), 0, 1)

    saved = []
    xs = x0
    for l in range(DEPTH):
        sh1, sc1, g1, sh2, sc2, g2 = [mod[l:l + 1, k * D:(k + 1) * D] for k in range(6)]
        ng = [full["norm_g"][l, k:k + 1] for k in range(4)]
        kind, j = l % 3, l // 3
        s = dict(x_in=xs, sc1=sc1, g1=g1, sc2=sc2, g2=g2, ng=ng)
        if kind == 0:
            s["w_in"] = col_blocks(wg[f"in{j}"])
            s["h"], s["p"] = fwd(fwd_in, ("mix_in", l), xs, ng[0], sc1, sh1, s["w_in"], None, blocked=False, tm=tm,
                                 name=f"sc_in_{l}")
            x1, s["m"], s["q"] = fwd(sc_fwd_out, ("mix_out", l), s["p"], full["sc_conv"][j], wg[f"out{j}"].reshape(D, D), xs,
                                     ng[1], g1, tm=tm, name=f"sc_out_{l}")
        elif kind == 1:
            pool_w_f = jnp.swapaxes(wg["pool"], 0, 1).reshape(4, PG, PG)
            x1, s["m"], s["ypre"], s["pooled"] = pool_fwd(xs, ng[0], sc1, sh1, pool_w_f, pool_b, pool_scale, ng[1], g1,
                                                          tm=tm, name=f"pool_{l}")
        else:
            s["w_in"] = col_blocks(wg["pw1"])
            s["h"], s["a"] = fwd_in(xs, ng[0], sc1, sh1, s["w_in"], full["cf_b_pw1"].reshape(2, 1, D), blocked=False, tm=tm,
                                    name=f"cf_in_{l}")
            x1, s["m"], s["s"], s["u2"] = cf_fwd_out(s["a"], full["cf_w_dw"][0], full["cf_b_dw"], full["cf_ln_g"],
                                                     full["cf_ln_b"], wg["pw2"].reshape(D, D), full["cf_b_pw2"], xs, ng[1],
                                                     g1, tm=tm, name=f"cf_out_{l}")
        s["x1"] = x1
        s["cw"] = ffn_blocks(full["ffn_conv"][l])
        s["h2"], s["up"] = fwd(fwd_in, ("ffn_in", l), x1, ng[2], sc2, sh2, wg[f"up{l}"], None, blocked=True, wt=True, tm=tm,
                               name=f"ffn_in_{l}")
        xs, s["f"], s["gc"], s["fa"], *loss_part = fwd(
            ffn_fwd_out, ("ffn_out", l), s["up"], s["cw"], ffn_blocks(ffn_b_conv[l:l + 1]), wg[f"down{l}"].reshape(F, D), x1,
            ng[3], g2, tm=tm, name=f"ffn_out_{l}", target=target if l == DEPTH - 1 else None)
        saved.append(s)

    dx = xs
    loss = lax.psum(loss_part[0][0, 0], ("x", "y", "c"))

    gmod = [None] * DEPTH
    d_norm_g = [None] * DEPTH
    d_ffn_conv = [None] * DEPTH
    d_ffn_b_conv = [None] * DEPTH
    d_sc_conv = [None] * 2
    big = {}
    got = {}
    small_g = {}
    bwd_plan = {("mix_bout", 3): ["down3"], ("mix_bin", 3): ["up3"], ("ffn_bout", 2): ["in1", "out1"],
                ("mix_bmid", 2): ["up2", "down2"], ("ffn_bout", 1): ["pw1", "pw2"], ("ffn_bout", 0): ["pool", "down1"],
                ("ffn_bin", 0): ["up1"], ("mix_bout", 0): ["down0"], ("mix_win", 0): ["up0"], ("mix_wout", 0): ["in0"]}
    bwd = functools.partial(carrying, bwd_plan, "scatter", got, big)
    pool_w_f = jnp.swapaxes(wg["pool"], 0, 1).reshape(4, PG, PG)
    for l in reversed(range(DEPTH)):
        s = saved[l]
        ng = s["ng"]
        kind, j = l % 3, l // 3
        df, dgc, dup, st_o, st_b = bwd(ffn_bwd_out, ("ffn_bout", l), dx, s["f"], ng[3], s["g2"], wg[f"down{l}"].reshape(F, D),
                                       s["gc"], s["up"], tm=tm, name=f"ffn_bout_{l}")
        dx1, dup, st_i, st_c = bwd(ffn_bwd_in, ("ffn_bin", l), dgc, dup, s["up"], s["cw"], wg[f"up{l}"], s["x1"], ng[2],
                                   s["sc2"], dx, tm=tm, name=f"ffn_bin_{l}")
        (big[f"up{l}"],) = wgrad(dup, s["h2"], nblk=NDEV, a_blocked=True, b_blocked=False, bk=FB, bn=D, tt=tt,
                                 name=f"ffn_wup_{l}")
        big[f"down{l}"] = wgrad(s["fa"], df, nblk=4, a_blocked=True, b_blocked=False, bk=FB, bn=D, tt=tt,
                                name=f"ffn_wdown_{l}")[0].reshape(NDEV, F // NDEV, D)
        d_ffn_b_conv[l] = st_b[:, 0, :].reshape(F)
        d_ffn_conv[l] = jnp.swapaxes(st_c[:, 1:4, :], 0, 1).reshape(3, F)
        g_ffn = [st_i[0], st_i[1], st_o[0]]
        dn3, dn2 = st_o[1], st_i[2]
        if kind == 0:
            dm, dbg, du, st_o = bwd(sc_bwd_out, ("mix_bout", l), dx1, s["m"], ng[1], s["g1"], wg[f"out{j}"].reshape(D, D), s["p"],
                                    full["sc_conv"][j], tm=tm, name=f"sc_bout_{l}")
            dx, dp, st_i, st_c = bwd(sc_bwd_in, ("mix_bin", l), du, dbg, s["p"], full["sc_conv"][j], s["w_in"], s["x_in"], ng[0],
                                     s["sc1"], dx1, tm=tm, name=f"sc_bin_{l}")
            (w_pairs,) = bwd(wgrad, ("mix_win", l), s["h"], dp, nblk=NDEV // 2, a_blocked=False, b_blocked=True, bk=D,
                             bn=6 * D // NDEV, tt=tt, name=f"sc_win_{l}")
            big[f"in{j}"] = jnp.swapaxes(w_pairs.reshape(NDEV // 2, D, 2, 3 * D // NDEV), 1, 2).reshape(NDEV, D, 3 * D // NDEV)
            big[f"out{j}"] = bwd(wgrad, ("mix_wout", l), s["q"], dm, nblk=1, a_blocked=False, b_blocked=False, bk=D, bn=D, tt=tt,
                                 name=f"sc_wout_{l}")[0].reshape(NDEV, D // NDEV, D)
            d_sc_conv[j] = st_c[0:3]
        elif kind == 1:
            dx, dyp, st_o = pool_bwd(dx1, s["m"], s["ypre"], pool_w_f, pool_scale, ng[1], s["g1"], s["x_in"], ng[0], s["sc1"],
                                     tm=tm, name=f"pool_b_{l}")
            st_i = st_o[4:7]
            (dpw,) = wgrad(s["pooled"], dyp, nblk=4, a_blocked=True, b_blocked=True, bk=PG, bn=PG, tt=tt, name=f"pool_w_{l}")
            big["pool"] = jnp.swapaxes(dpw.reshape(4, NDEV, PG // NDEV, PG), 0, 1).reshape(NDEV, 4 * PG // NDEV, PG)
            small_g["pool_scale"], small_g["pool_b"] = st_o[2:3], st_o[3:4]
        else:
            dm, ds, st_o = bwd_out(dx1, s["m"], ng[1], s["g1"], wg["pw2"].reshape(D, D), tm=tm,
                                   name=f"cf_bout_{l}")
            dA, st_c = bwd(cf_bwd_mid, ("mix_bmid", l), ds, s["u2"], s["a"], full["cf_w_dw"][0], full["cf_ln_g"],
                           full["cf_ln_b"], tm=tm, name=f"cf_bmid_{l}")
            dx, st_i = bwd_in(dA, s["w_in"], s["x_in"], ng[0], s["sc1"], dx1, tm=tm, name=f"cf_bin_{l}")
            (w_quads,) = wgrad(s["h"], dA, nblk=2, a_blocked=False, b_blocked=True, bk=D, bn=D, tt=tt, name=f"cf_wpw1_{l}")
            big["pw1"] = jnp.swapaxes(w_quads.reshape(2, D, 4, 2 * D // NDEV), 1, 2).reshape(NDEV, D, 2 * D // NDEV)
            big["pw2"] = wgrad(s["s"], dm, nblk=1, a_blocked=False, b_blocked=False, bk=D, bn=D, tt=tt,
                               name=f"cf_wpw2_{l}")[0].reshape(NDEV, D // NDEV, D)
            small_g["cf_w_dw"] = st_c[0:CFW][None]
            small_g["cf_b_dw"], small_g["cf_ln_g"], small_g["cf_ln_b"] = st_c[31:32], st_c[32:33], st_c[33:34]
            small_g["cf_b_pw1"] = st_c[34:36].reshape(1, 2 * D)
            small_g["cf_b_pw2"] = st_o[2:3]
        gmod[l] = jnp.concatenate([st_i[0], st_i[1], st_o[0]] + g_ffn)
        d_norm_g[l] = jnp.stack([st_i[2], st_o[1], dn2, dn3])

    small_g["gmod"] = jnp.stack(gmod)
    small_g["norm_g"] = jnp.stack(d_norm_g)
    small_g["sc_conv"] = jnp.stack(d_sc_conv)
    small_g["ffn_conv"] = jnp.stack(d_ffn_conv)
    small_g["ffn_b_conv"] = jnp.stack(d_ffn_b_conv)
    sg_names = ["gmod", "norm_g", "sc_conv", "pool_b", "pool_scale", "cf_b_pw1", "cf_w_dw", "cf_b_dw", "cf_ln_g", "cf_ln_b",
                "cf_b_pw2", "ffn_conv", "ffn_b_conv"]
    gpacked, glayout = _pack([small_g[n] for n in sg_names])
    (ggath,), (got["out0"],) = _run_exchanges(
        [_Exchange("gather", [gpacked]), _Exchange("scatter", [big["out0"]])], "exchange_last")
    gsum = dict(zip(sg_names, _unpack(sum_parts(ggath), glayout)))
    gmod_all = _unpack(ggath, glayout[:1], lead=(NDEV,))[0]
    grads = {"b_mod": gsum["gmod"], "pool_b": gsum["pool_b"], "pool_scale": gsum["pool_scale"],
             "ffn_b_conv": gsum["ffn_b_conv"]}
    for n in ["norm_g", "sc_conv", "cf_b_pw1", "cf_w_dw", "cf_b_dw", "cf_ln_g", "cf_ln_b", "cf_b_pw2", "ffn_conv"]:
        grads[n] = _my_cols(gsum[n], env[n].shape[-1])
    grads["w_mod"] = mod_wgrad(c_all.T, jnp.swapaxes(_my_cols(gmod_all, w_mod.shape[2]), 0, 1))

    deltas, new_m, new_v = {}, {}, {}
    sp_names = ["b_mod", "norm_g", "sc_conv", "pool_b", "pool_scale", "cf_b_pw1", "cf_w_dw", "cf_b_dw", "cf_ln_g", "cf_ln_b",
                "cf_b_pw2", "ffn_conv", "ffn_b_conv"]
    pg, playout = _pack([grads[n] for n in sp_names])
    pw_, _ = _pack([env[n] for n in sp_names])
    pm_, _ = _pack([env["m_" + n] for n in sp_names])
    pv_, _ = _pack([env["v_" + n] for n in sp_names])
    _, sd, sm, sv = adamw_sum(pg[None], pw_, pm_, pv_, name="adamw_small")
    for n, d_, m_, v_ in zip(sp_names, _unpack(sd, playout), _unpack(sm, playout), _unpack(sv, playout)):
        deltas[n], new_m[n], new_v[n] = d_, m_, v_
    gw = grads["w_mod"].reshape(1, DEPTH * D, -1)
    _, d_, m_, v_ = adamw_sum(gw, w_mod.reshape(gw.shape[1:]), m_w_mod.reshape(gw.shape[1:]), v_w_mod.reshape(gw.shape[1:]),
                              name="adamw_w_mod")
    deltas["w_mod"], new_m["w_mod"], new_v["w_mod"] = [a.reshape(w_mod.shape) for a in (d_, m_, v_)]

    groups = {"sc_w_in": ["in0", "in1"], "sc_w_out": ["out0", "out1"], "pool_w": ["pool"], "cf_w_pw1": ["pw1"],
              "cf_w_pw2": ["pw2"], "ffn_w_up": [f"up{l}" for l in range(DEPTH)], "ffn_w_down": [f"down{l}" for l in range(DEPTH)]}
    for n, layers in groups.items():
        stacked = (len(layers),) + got[layers[0]].shape[1:]
        flip = n == "ffn_w_up"
        w3 = [(jnp.swapaxes(env[p + n], 1, 2) if flip else env[p + n]).reshape(stacked) for p in ("", "m_", "v_")]
        outs = None
        for li, key in enumerate(layers):
            outs = adamw_layer(got[key], *w3, outs, li, name=f"adamw_{n}_{li}")
        grads[n], deltas[n], new_m[n], new_v[n] = [(jnp.swapaxes(a, 1, 2) if flip else a).reshape(env[n].shape) for a in outs]

    return (loss, dx[None], *[grads[n] for n in names], *[deltas[n] for n in names], *[new_m[n] for n in names],
            *[new_v[n] for n in names])
```

```python
import functools

import jax
import jax.numpy as jnp
from jax import lax
from jax.experimental import pallas as pl
from jax.experimental.pallas import tpu as pltpu

D = 1024
F = 2816
NDEV = 8
FB = F // 4
DEPTH = 4
RMS_EPS = 1e-6
LN_EPS = 1e-5
CFW = 31
POOL_WINDOWS = (2, 4, 8, 16)
PG = D // 4
LR, B1, B2, ADAM_EPS, WD, STEP = 0.001, 0.9, 0.999, 1e-08, 0.01, 10

BF = jnp.bfloat16
F32 = jnp.float32
VMEM_LIMIT_V7X = 56 * 1024 * 1024
MXU_COLS_V7X = 256
MESH = pl.DeviceIdType.MESH
ANY = pl.BlockSpec(memory_space=pl.ANY)


def _params(n_axes):
    return pltpu.CompilerParams(dimension_semantics=("arbitrary",) * n_axes, vmem_limit_bytes=VMEM_LIMIT_V7X)


def _const(shape, single=True):
    nd = len(shape)
    if single:
        return pl.BlockSpec(shape, lambda *_: (0,) * nd, pipeline_mode=pl.Buffered(1))
    return pl.BlockSpec(shape, lambda *_: (0,) * nd)


def _rows(tm, c):
    return pl.BlockSpec((tm, c), lambda i: (i, 0))


def _brows(nb, tm, c, b0=0):
    return pl.BlockSpec((nb, tm, c), lambda i: (b0, i, 0))


def _prev(hb, c, tm):
    return pl.BlockSpec((hb, c), lambda i: (jnp.maximum(i * (tm // hb) - 1, 0), 0))


def _next(hb, c, tm, t):
    return pl.BlockSpec((hb, c), lambda i: (jnp.minimum((i + 1) * (tm // hb), t // hb - 1), 0))


def _bprev(nb, hb, c, tm, b0=0):
    return pl.BlockSpec((nb, hb, c), lambda i: (b0, jnp.maximum(i * (tm // hb) - 1, 0), 0))


def _bnext(nb, hb, c, tm, t, b0=0):
    return pl.BlockSpec((nb, hb, c), lambda i: (b0, jnp.minimum((i + 1) * (tm // hb), t // hb - 1), 0))


def _sigmoid(v):
    return 0.5 * jnp.tanh(0.5 * v) + 0.5


def _fold8(v):
    r, c = v.shape
    return jnp.sum(v.reshape(r // 8, 8, c), axis=0)


def _chunks(n_rows, rc, step, init=0, reverse=False):
    n = n_rows // rc

    def it(c, carry):
        idx = (n - 1 - c) if reverse else c
        return step(pl.multiple_of(idx * rc, rc), carry)

    return lax.fori_loop(0, n, it, init)


def _row_shifted_copies(s_ref, n):
    for b in range(1, 8):
        s_ref[b, 0:n, :] = s_ref[0, pl.ds(b, n), :]


def _shifted(s_ref, o, tm):
    return s_ref[o % 8, pl.ds(8 * (o // 8), tm), :]


def _dot(a, b):
    return jnp.dot(a, b, preferred_element_type=F32)


def _dot_nt(a, b):
    return lax.dot_general(a, b, (((1,), (1,)), ((), ())), preferred_element_type=F32)


def _dot_tn(a, b):
    return lax.dot_general(a, b, (((0,), (0,)), ((), ())), preferred_element_type=F32)


def _rsum(v):
    return jnp.sum(v, axis=0, keepdims=True)


def _adaln(x, g, sc, sh):
    r = lax.rsqrt(jnp.mean(x * x, axis=-1, keepdims=True) + RMS_EPS)
    return (x * r * g) * (1.0 + sc) + sh


def _gated_res(x, m, gn, gt):
    r = lax.rsqrt(jnp.mean(m * m, axis=-1, keepdims=True) + RMS_EPS)
    return x + gt * (m * r * gn)


def _gated_res_bwd(dxo, m, gn, gt):
    r = lax.rsqrt(jnp.mean(m * m, axis=-1, keepdims=True) + RMS_EPS)
    mh = m * r
    dgt = _rsum(dxo * (mh * gn))
    dn = dxo * gt
    dgn = _rsum(dn * mh)
    dmh = dn * gn
    dm = r * (dmh - mh * jnp.mean(dmh * mh, axis=-1, keepdims=True))
    return dm, dgt, dgn


def _my_id():
    return 4 * lax.axis_index("x") + 2 * lax.axis_index("y") + lax.axis_index("c")


def _peer(k):
    x, y, c = lax.axis_index("x"), lax.axis_index("y"), lax.axis_index("c")
    px = 1 - x if k & 4 else x
    py = 1 - y if k & 2 else y
    pc = 1 - c if k & 1 else c
    return (px, py, pc), 4 * px + 2 * py + pc


class _Exchange:
    def __init__(self, kind, arrays):
        self.gather = kind == "gather"
        self.arrays = list(arrays)
        n = len(self.arrays)
        if self.gather:
            self.out_shape = [jax.ShapeDtypeStruct((NDEV,) + a.shape, a.dtype) for a in self.arrays]
        else:
            self.out_shape = [jax.ShapeDtypeStruct(a.shape, a.dtype) for a in self.arrays]
        self.scratch = [pltpu.SemaphoreType.DMA((n * NDEV,)), pltpu.SemaphoreType.DMA((n * NDEV,)),
                        pltpu.SemaphoreType.DMA((n,))]

    def _local(self, a, src, dst, sems):
        me = _my_id()
        return pltpu.make_async_copy(src[a] if self.gather else src[a].at[me], dst[a].at[me], sems[2].at[a])

    def _remote(self, a, k, src, dst, sems, incoming):
        to, pid = _peer(k)
        me = _my_id()
        return pltpu.make_async_remote_copy(
            src_ref=src[a] if self.gather else src[a].at[pid], dst_ref=dst[a].at[pid if incoming else me],
            send_sem=sems[0].at[a * NDEV + k], recv_sem=sems[1].at[a * NDEV + k], device_id=to, device_id_type=MESH)

    def start(self, src, dst, sems):
        for a in range(len(self.arrays)):
            self._local(a, src, dst, sems).start()
        for k in range(1, NDEV):
            for a in range(len(self.arrays)):
                self._remote(a, k, src, dst, sems, False).start()

    def wait(self, src, dst, sems):
        for k in range(1, NDEV):
            for a in range(len(self.arrays)):
                self._remote(a, k, src, dst, sems, True).wait_recv()
        for k in range(1, NDEV):
            for a in range(len(self.arrays)):
                self._remote(a, k, src, dst, sems, False).wait_send()
        for a in range(len(self.arrays)):
            self._local(a, src, dst, sems).wait()

    def run_two_level(self, name):
        assert self.gather
        n = len(self.arrays)

        def body(*refs):
            src, dst, sems = refs[:n], refs[n:2 * n], refs[2 * n:]
            me = _my_id()
            sibling, _ = _peer(1)

            def passed_on(a, k, incoming):
                _, pid = _peer(k + 1 if incoming else k)
                return pltpu.make_async_remote_copy(
                    src_ref=dst[a].at[pid], dst_ref=dst[a].at[pid], send_sem=sems[0].at[a * NDEV + k + 1],
                    recv_sem=sems[1].at[a * NDEV + k + 1], device_id=sibling, device_id_type=MESH)

            for a in range(n):
                self._local(a, src, dst, sems).start()
            for k in (1, 2, 4, 6):
                for a in range(n):
                    self._remote(a, k, src, dst, sems, False).start()
            for k in (2, 4, 6):
                for a in range(n):
                    self._remote(a, k, src, dst, sems, True).wait_recv()
                    passed_on(a, k, False).start()
            for a in range(n):
                self._remote(a, 1, src, dst, sems, True).wait_recv()
                for k in (2, 4, 6):
                    passed_on(a, k, True).wait_recv()
            for a in range(n):
                for k in (1, 2, 4, 6):
                    self._remote(a, k, src, dst, sems, False).wait_send()
                for k in (2, 4, 6):
                    passed_on(a, k, False).wait_send()
                self._local(a, src, dst, sems).wait()

        return pl.pallas_call(body, name=name, in_specs=[ANY] * n, out_specs=[ANY] * n, out_shape=self.out_shape,
                              scratch_shapes=self.scratch)(*self.arrays)

    def run(self, name):
        n = len(self.arrays)

        def body(*refs):
            src, dst, sems = refs[:n], refs[n:2 * n], refs[2 * n:]
            self.start(src, dst, sems)
            self.wait(src, dst, sems)

        return pl.pallas_call(body, name=name, in_specs=[ANY] * n, out_specs=[ANY] * n, out_shape=self.out_shape,
                              scratch_shapes=self.scratch)(*self.arrays)


def _run_exchanges(exchanges, name):
    counts = [len(e.arrays) for e in exchanges]
    n = sum(counts)

    def body(*refs):
        src, dst, sems = refs[:n], refs[n:2 * n], refs[2 * n:]
        parts, lo = [], 0
        for ei, (e, c) in enumerate(zip(exchanges, counts)):
            parts.append((e, src[lo:lo + c], dst[lo:lo + c], sems[3 * ei:3 * ei + 3]))
            lo += c
        for e, s, d, m in parts:
            e.start(s, d, m)
        for e, s, d, m in parts:
            e.wait(s, d, m)

    res = pl.pallas_call(
        body, name=name, in_specs=[ANY] * n, out_specs=[ANY] * n, out_shape=[s for e in exchanges for s in e.out_shape],
        scratch_shapes=[s for e in exchanges for s in e.scratch])(*[a for e in exchanges for a in e.arrays])
    out, lo = [], 0
    for c in counts:
        out.append(list(res[lo:lo + c]))
        lo += c
    return out


def _call(body, *, name, grid, in_specs, out_specs, out_shape, args, scratch_shapes=(), carry=None, aliases=None):
    cp = _params(len(grid))
    aliases = aliases or {}
    if carry is None:
        return tuple(pl.pallas_call(body, name=name, grid=grid, in_specs=in_specs, out_specs=out_specs, out_shape=out_shape,
                                    scratch_shapes=list(scratch_shapes), input_output_aliases=aliases,
                                    compiler_params=cp)(*args))
    n_in, n_out, n_sc, n_c = len(in_specs), len(out_specs), len(scratch_shapes), len(carry.arrays)

    def wrapped(*refs):
        ins, src = refs[:n_in], refs[n_in:n_in + n_c]
        outs = refs[n_in + n_c:n_in + n_c + n_out]
        dst = refs[n_in + n_c + n_out:n_in + 2 * n_c + n_out]
        rest = refs[n_in + 2 * n_c + n_out:]
        scr, sems = rest[:n_sc], rest[n_sc:]
        first = pl.program_id(0) == 0
        last = pl.program_id(0) == grid[0] - 1
        for ax in range(1, len(grid)):
            first = jnp.logical_and(first, pl.program_id(ax) == 0)
            last = jnp.logical_and(last, pl.program_id(ax) == grid[ax] - 1)

        @pl.when(first)
        def _():
            carry.start(src, dst, sems)

        body(*ins, *outs, *scr)

        @pl.when(last)
        def _():
            carry.wait(src, dst, sems)

    res = pl.pallas_call(
        wrapped, name=name, grid=grid, in_specs=list(in_specs) + [ANY] * n_c, out_specs=list(out_specs) + [ANY] * n_c,
        out_shape=list(out_shape) + carry.out_shape, scratch_shapes=list(scratch_shapes) + carry.scratch,
        input_output_aliases=aliases, compiler_params=cp)(*args, *carry.arrays)
    return tuple(res[:n_out]) + (list(res[n_out:]),)


def fwd_in(x, g, sc, sh, w, bias, *, blocked, tm, name, carry=None, wt=False):
    t = x.shape[0]
    nb, bw = (w.shape[0], w.shape[1]) if wt else (w.shape[0], w.shape[2])
    per = next(k for k in (1, 2, 4, 8) if (k * bw) % MXU_COLS_V7X == 0)
    assert not wt or (blocked and bias is None and nb % per == 0)

    def body(*refs):
        if bias is None:
            x_ref, g_ref, sc_ref, sh_ref, w_ref, h_ref, p_ref = refs
        else:
            x_ref, g_ref, sc_ref, sh_ref, w_ref, b_ref, h_ref, p_ref = refs
        hb = _adaln(x_ref[...], g_ref[...], sc_ref[...], sh_ref[...]).astype(BF)
        h_ref[...] = hb
        if wt:
            for c in range(nb // per):
                y = _dot_nt(hb, w_ref[c * per:(c + 1) * per].reshape(per * bw, D))
                for d in range(per):
                    p_ref[c * per + d] = y[:, d * bw:(d + 1) * bw].astype(BF)
            return
        for d in range(nb):
            y = _dot(hb, w_ref[d])
            if bias is not None:
                y = y + b_ref[d]
            if blocked:
                p_ref[d] = y.astype(BF)
            else:
                p_ref[:, d * bw:(d + 1) * bw] = y.astype(BF)

    vec = _const((1, D))
    in_specs = [_rows(tm, D), vec, vec, vec, _const(w.shape)]
    args = [x, g, sc, sh, w]
    if bias is not None:
        in_specs.append(_const((nb, 1, bw)))
        args.append(bias)
    if blocked:
        p_spec, p_shape = _brows(nb, tm, bw), jax.ShapeDtypeStruct((nb, t, bw), BF)
    else:
        p_spec, p_shape = _rows(tm, nb * bw), jax.ShapeDtypeStruct((t, nb * bw), BF)
    return _call(body, name=name, grid=(t // tm,), in_specs=in_specs, out_specs=[_rows(tm, D), p_spec],
                 out_shape=[jax.ShapeDtypeStruct((t, D), BF), p_shape], args=args, carry=carry)


def _sc_conv(p_ref, ph_ref, cw_ref, first, tm):
    z = p_ref[:, D:2 * D].astype(F32) * p_ref[:, 2 * D:3 * D].astype(F32)
    zp = jnp.where(first, 0.0, ph_ref[8:16, D:2 * D].astype(F32) * ph_ref[8:16, 2 * D:3 * D].astype(F32))
    ext = jnp.concatenate([zp, z], axis=0)
    return cw_ref[0:1, :] * ext[6:6 + tm] + cw_ref[1:2, :] * ext[7:7 + tm] + cw_ref[2:3, :] * z


def sc_fwd_out(p, convw, w_out, x, gn, gt, *, tm, name, carry=None):
    t = x.shape[0]

    def body(p_ref, ph_ref, cw_ref, w_ref, x_ref, gn_ref, gt_ref, x1_ref, m_ref, q_ref):
        u = _sc_conv(p_ref, ph_ref, cw_ref, pl.program_id(0) == 0, tm)
        qb = (p_ref[:, 0:D].astype(F32) * u).astype(BF)
        q_ref[...] = qb
        m = _dot(qb, w_ref[...])
        m_ref[...] = m.astype(BF)
        x1_ref[...] = _gated_res(x_ref[...], m, gn_ref[...], gt_ref[...])

    vec = _const((1, D))
    return _call(
        body, name=name, grid=(t // tm,),
        in_specs=[_rows(tm, 3 * D), _prev(16, 3 * D, tm), _const((3, D)), _const((D, D)), _rows(tm, D), vec, vec],
        out_specs=[_rows(tm, D)] * 3,
        out_shape=[jax.ShapeDtypeStruct((t, D), F32), jax.ShapeDtypeStruct((t, D), BF), jax.ShapeDtypeStruct((t, D), BF)],
        args=[p, p, convw, w_out, x, gn, gt], carry=carry)


def _layernorm_parts(u2):
    mu = jnp.mean(u2, axis=-1, keepdims=True)
    cen = u2 - mu
    rstd = lax.rsqrt(jnp.mean(cen * cen, axis=-1, keepdims=True) + LN_EPS)
    return cen * rstd, rstd


def cf_fwd_out(a, w_dw, b_dw, ln_g, ln_b, w_pw2, b_pw2, x, gn, gt, *, tm, name):
    t = x.shape[0]
    hb = 32

    def body(a_ref, ah_ref, wd_ref, bd_ref, lg_ref, lb_ref, w_ref, b2_ref, x_ref, gn_ref, gt_ref,
             x1_ref, m_ref, s_out_ref, u2_ref, s_ref):
        i = pl.program_id(0)
        uh = ah_ref[:, 0:D].astype(F32) * _sigmoid(ah_ref[:, D:2 * D].astype(F32))
        s_ref[0, 0:hb, :] = jnp.where(i == 0, 0.0, uh)
        s_ref[0, hb:hb + tm, :] = a_ref[:, 0:D].astype(F32) * _sigmoid(a_ref[:, D:2 * D].astype(F32))
        _row_shifted_copies(s_ref, tm + hb - 8)
        acc = bd_ref[...] + wd_ref[0:1, :] * _shifted(s_ref, hb - CFW + 1, tm)
        for k in range(1, CFW):
            acc = acc + wd_ref[k:k + 1, :] * _shifted(s_ref, hb - CFW + 1 + k, tm)
        u2_ref[...] = acc.astype(BF)
        xh, _ = _layernorm_parts(acc)
        l = xh * lg_ref[...] + lb_ref[...]
        sb = (l * _sigmoid(l)).astype(BF)
        s_out_ref[...] = sb
        m = _dot(sb, w_ref[...]) + b2_ref[...]
        m_ref[...] = m.astype(BF)
        x1_ref[...] = _gated_res(x_ref[...], m, gn_ref[...], gt_ref[...])

    vec = _const((1, D))
    return pl.pallas_call(
        body, name=name, grid=(t // tm,),
        in_specs=[_rows(tm, 2 * D), _prev(hb, 2 * D, tm), _const((CFW, D)), vec, vec, vec, _const((D, D)), vec,
                  _rows(tm, D), vec, vec],
        out_specs=[_rows(tm, D)] * 4,
        out_shape=[jax.ShapeDtypeStruct((t, D), F32)] + [jax.ShapeDtypeStruct((t, D), BF)] * 3,
        scratch_shapes=[pltpu.VMEM((8, tm + hb, D), F32)], compiler_params=_params(1),
    )(a, a, w_dw, b_dw, ln_g, ln_b, w_pw2, b_pw2, x, gn, gt)


def _pool_counts(i, tm, w):
    row = lax.broadcasted_iota(jnp.int32, (tm, 1), 0) + i * tm
    return jnp.minimum(row + 1, w).astype(F32)


def pool_fwd(x, g, sc, sh, pw, pb, pscale, gn, gt, *, tm, name):
    t = x.shape[0]
    pad, hb = 8, 16
    base = pad + hb

    def body(x_ref, xh_ref, g_ref, sc_ref, sh_ref, pw_ref, pb_ref, ps_ref, gn_ref, gt_ref,
             x1_ref, m_ref, yp_ref, po_ref, sa_ref, sb_ref):
        i = pl.program_id(0)
        hh = _adaln(xh_ref[...], g_ref[...], sc_ref[...], sh_ref[...])
        h = _adaln(x_ref[...], g_ref[...], sc_ref[...], sh_ref[...])
        zero = jnp.zeros((pad, D), F32)
        sa_ref[0:pad, :] = zero
        sb_ref[0:pad, :] = zero
        sa_ref[pad:base, :] = jnp.where(i == 0, 0.0, hh)
        sa_ref[base:base + tm, :] = h
        n = hb + tm
        src, dst = sa_ref, sb_ref
        ys = []
        for gi, w in enumerate(POOL_WINDOWS):
            c0 = gi * PG
            step = w // 2
            dst[pl.ds(pad, n), c0:D] = src[pl.ds(pad, n), c0:D] + src[pl.ds(pad - step, n), c0:D]
            mean = dst[pl.ds(base, tm), c0:c0 + PG] / _pool_counts(i, tm, w)
            pooled = (mean - h[:, c0:c0 + PG]).astype(BF)
            po_ref[:, c0:c0 + PG] = pooled
            ys.append(_dot(pooled, pw_ref[gi]))
            src, dst = dst, src
        ypre = jnp.concatenate(ys, axis=1) + pb_ref[...]
        yp_ref[...] = ypre.astype(BF)
        m = ypre * ps_ref[...]
        m_ref[...] = m.astype(BF)
        x1_ref[...] = _gated_res(x_ref[...], m, gn_ref[...], gt_ref[...])

    vec = _const((1, D))
    return pl.pallas_call(
        body, name=name, grid=(t // tm,),
        in_specs=[_rows(tm, D), _prev(hb, D, tm), vec, vec, vec, _const((4, PG, PG)), vec, vec, vec, vec],
        out_specs=[_rows(tm, D)] * 4,
        out_shape=[jax.ShapeDtypeStruct((t, D), F32)] + [jax.ShapeDtypeStruct((t, D), BF)] * 3,
        scratch_shapes=[pltpu.VMEM((tm + base, D), F32)] * 2, compiler_params=_params(1),
    )(x, x, g, sc, sh, pw, pb, pscale, gn, gt)


def ffn_fwd_out(up, convw, convb, w_down, x, gn, gt, *, tm, name, carry=None, target=None):
    t = x.shape[0]
    nt = t // tm

    def body(gate_ref, gh_ref, val_ref, cw_ref, cb_ref, w_ref, x_ref, gn_ref, gt_ref, *rest):
        if target is None:
            x2_ref, f_ref, gc_ref, a_ref = rest
        else:
            tg_ref, x2_ref, f_ref, gc_ref, a_ref, l_ref, lacc_ref = rest
        i = pl.program_id(0)
        acc = jnp.zeros((tm, D), F32)
        for j in range(4):
            gate = gate_ref[j].astype(F32)
            ext = jnp.concatenate([jnp.where(i == 0, 0.0, gh_ref[j, 8:16, :].astype(F32)), gate], axis=0)
            gc = cb_ref[j] + cw_ref[j, 0:1, :] * ext[6:6 + tm] + cw_ref[j, 1:2, :] * ext[7:7 + tm] + cw_ref[j, 2:3, :] * gate
            gc_ref[j] = gc.astype(BF)
            ab = (gc * _sigmoid(gc) * val_ref[j].astype(F32)).astype(BF)
            a_ref[j] = ab
            acc = acc + _dot(ab, w_ref[j * FB:(j + 1) * FB, :])
        f_ref[...] = acc.astype(BF)
        x2 = _gated_res(x_ref[...], acc, gn_ref[...], gt_ref[...])
        if target is None:
            x2_ref[...] = x2
        else:
            @pl.when(i == 0)
            def _():
                lacc_ref[...] = jnp.zeros_like(lacc_ref)

            e = x2 - tg_ref[...]
            x2_ref[...] = e * (1.0 / D)
            lacc_ref[...] += _rsum(e * e)

            @pl.when(i == nt - 1)
            def _():
                l_ref[...] = jnp.sum(lacc_ref[...], axis=1, keepdims=True) * (0.5 / D)

    vec = _const((1, D))
    blk = jax.ShapeDtypeStruct((4, t, FB), BF)
    in_specs = [_brows(4, tm, FB, 0), _bprev(4, 16, FB, tm, 0), _brows(4, tm, FB, 1), _const((4, 3, FB)),
                _const((4, 1, FB)), _const((F, D)), _rows(tm, D), vec, vec]
    out_specs = [_rows(tm, D), _rows(tm, D), _brows(4, tm, FB), _brows(4, tm, FB)]
    out_shape = [jax.ShapeDtypeStruct((t, D), F32), jax.ShapeDtypeStruct((t, D), BF), blk, blk]
    args = [up, up, up, convw, convb, w_down, x, gn, gt]
    scratch = []
    if target is not None:
        in_specs.append(_rows(tm, D))
        args.append(target)
        out_specs.append(pl.BlockSpec((1, 1), lambda i: (0, 0)))
        out_shape.append(jax.ShapeDtypeStruct((1, 1), F32))
        scratch.append(pltpu.VMEM((1, D), F32))
    return _call(body, name=name, grid=(nt,), in_specs=in_specs, out_specs=out_specs, out_shape=out_shape, args=args,
                 scratch_shapes=scratch, carry=carry)


def _init_stats(ref):
    @pl.when(pl.program_id(0) == 0)
    def _():
        ref[...] = jnp.zeros_like(ref)


def bwd_out(dxo, m, gn, gt, w, *, tm, name, carry=None):
    t = dxo.shape[0]
    k = w.shape[0]

    def body(dx_ref, m_ref, gn_ref, gt_ref, w_ref, dm_ref, da_ref, st_ref):
        _init_stats(st_ref)
        dm, dgt, dgn = _gated_res_bwd(dx_ref[...], m_ref[...].astype(F32), gn_ref[...], gt_ref[...])
        st_ref[0:1, :] += dgt
        st_ref[1:2, :] += dgn
        st_ref[2:3, :] += _rsum(dm)
        dmb = dm.astype(BF)
        dm_ref[...] = dmb
        da_ref[...] = _dot_nt(dmb, w_ref[...]).astype(BF)

    vec = _const((1, D))
    da_spec, da_shape = _rows(tm, k), jax.ShapeDtypeStruct((t, k), BF)
    return _call(
        body, name=name, grid=(t // tm,), in_specs=[_rows(tm, D), _rows(tm, D), vec, vec, _const((k, D))],
        out_specs=[_rows(tm, D), da_spec, _const((8, D), single=False)],
        out_shape=[jax.ShapeDtypeStruct((t, D), BF), da_shape, jax.ShapeDtypeStruct((8, D), F32)],
        args=[dxo, m, gn, gt, w], carry=carry)


def _adaln_bwd(dh, x_ref, g_ref, sc_ref, dxo_ref, st_ref, row=0):
    x = x_ref[...]
    r = lax.rsqrt(jnp.mean(x * x, axis=-1, keepdims=True) + RMS_EPS)
    xh = x * r
    gv = g_ref[...]
    st_ref[row:row + 1, :] += _rsum(dh)
    st_ref[row + 1:row + 2, :] += _rsum(dh * (xh * gv))
    dn = dh * (1.0 + sc_ref[...])
    st_ref[row + 2:row + 3, :] += _rsum(dn * xh)
    dy = dn * gv
    return dxo_ref[...] + r * (dy - xh * jnp.mean(dy * xh, axis=-1, keepdims=True))


def bwd_in(dp, w, x, g, sc, dxo, *, tm, name):
    t = x.shape[0]
    nb, _, bw = w.shape

    def body(dp_ref, w_ref, x_ref, g_ref, sc_ref, dxo_ref, dx_ref, st_ref):
        _init_stats(st_ref)
        dh = jnp.zeros((tm, D), F32)
        for d in range(nb):
            dh = dh + _dot_nt(dp_ref[:, d * bw:(d + 1) * bw], w_ref[d])
        dx_ref[...] = _adaln_bwd(dh, x_ref, g_ref, sc_ref, dxo_ref, st_ref)

    vec = _const((1, D))
    return pl.pallas_call(
        body, name=name, grid=(t // tm,),
        in_specs=[_rows(tm, nb * bw), _const(w.shape), _rows(tm, D), vec, vec, _rows(tm, D)],
        out_specs=[_rows(tm, D), _const((8, D), single=False)],
        out_shape=[jax.ShapeDtypeStruct((t, D), F32), jax.ShapeDtypeStruct((8, D), F32)],
        compiler_params=_params(1))(dp, w, x, g, sc, dxo)


def sc_bwd_out(dxo, m, gn, gt, w_out, p, convw, *, tm, name, carry=None):
    t = dxo.shape[0]

    def body(dx_ref, m_ref, gn_ref, gt_ref, w_ref, p_ref, ph_ref, cw_ref, dm_ref, dbg_ref, du_ref, st_ref):
        _init_stats(st_ref)
        dm, dgt, dgn = _gated_res_bwd(dx_ref[...], m_ref[...].astype(F32), gn_ref[...], gt_ref[...])
        st_ref[0:1, :] += dgt
        st_ref[1:2, :] += dgn
        dmb = dm.astype(BF)
        dm_ref[...] = dmb
        dq = _dot_nt(dmb, w_ref[...])
        dbg_ref[...] = (dq * _sc_conv(p_ref, ph_ref, cw_ref, pl.program_id(0) == 0, tm)).astype(BF)
        du_ref[...] = (dq * p_ref[:, 0:D].astype(F32)).astype(BF)

    vec = _const((1, D))
    out = jax.ShapeDtypeStruct((t, D), BF)
    return _call(
        body, name=name, grid=(t // tm,),
        in_specs=[_rows(tm, D), _rows(tm, D), vec, vec, _const((D, D)), _rows(tm, 3 * D), _prev(16, 3 * D, tm), _const((3, D))],
        out_specs=[_rows(tm, D)] * 3 + [_const((8, D), single=False)], out_shape=[out, out, out, jax.ShapeDtypeStruct((8, D), F32)],
        args=[dxo, m, gn, gt, w_out, p, p, convw], carry=carry)


def sc_bwd_in(du, dbg, p, convw, w, x, g, sc, dxo, *, tm, name, carry=None):
    t = x.shape[0]

    def body(du_ref, dun_ref, dbg_ref, p_ref, cw_ref, w_ref, x_ref, g_ref, sc_ref, dxo_ref, dx_ref, dp_ref, st_ref, sc2_ref):
        last = pl.program_id(0) == pl.num_programs(0) - 1
        _init_stats(st_ref)
        _init_stats(sc2_ref)
        du = du_ref[...].astype(F32)
        ext = jnp.concatenate([du, jnp.where(last, 0.0, dun_ref[0:8, :].astype(F32))], axis=0)
        e1, e2 = ext[1:tm + 1], ext[2:tm + 2]
        dz = cw_ref[2:3, :] * du + cw_ref[1:2, :] * e1 + cw_ref[0:1, :] * e2
        cg, hi = p_ref[:, D:2 * D].astype(F32), p_ref[:, 2 * D:3 * D].astype(F32)
        dbg, dcg, dhi = dbg_ref[...], (dz * hi).astype(BF), (dz * cg).astype(BF)
        dp_ref[:, 0:D] = dbg
        dp_ref[:, D:2 * D] = dcg
        dp_ref[:, 2 * D:3 * D] = dhi
        dh = _dot_nt(dbg, w_ref[0]) + _dot_nt(dcg, w_ref[1]) + _dot_nt(dhi, w_ref[2])
        z = cg * hi
        sc2_ref[0:1, :] += _rsum(z * e2)
        sc2_ref[1:2, :] += _rsum(z * e1)
        sc2_ref[2:3, :] += _rsum(z * du)
        dx_ref[...] = _adaln_bwd(dh, x_ref, g_ref, sc_ref, dxo_ref, st_ref)

    vec = _const((1, D))
    stat = jax.ShapeDtypeStruct((8, D), F32)
    return _call(
        body, name=name, grid=(t // tm,),
        in_specs=[_rows(tm, D), _next(16, D, tm, t), _rows(tm, D), _rows(tm, 3 * D), _const((3, D)), _const(w.shape),
                  _rows(tm, D), vec, vec, _rows(tm, D)],
        out_specs=[_rows(tm, D), _rows(tm, 3 * D), _const((8, D), single=False), _const((8, D), single=False)],
        out_shape=[jax.ShapeDtypeStruct((t, D), F32), jax.ShapeDtypeStruct((t, 3 * D), BF), stat, stat],
        args=[du, du, dbg, p, convw, w, x, g, sc, dxo], carry=carry)


def cf_bwd_mid(ds, u2, a, w_dw, ln_g, ln_b, *, tm, name, carry=None):
    t = ds.shape[0]
    hb = 32

    def du2_of(dsv, u2v, lg, lb):
        xh, rstd = _layernorm_parts(u2v)
        l = xh * lg + lb
        sg = _sigmoid(l)
        dl = dsv * (sg * (1.0 + l * (1.0 - sg)))
        dxh = dl * lg
        du2 = rstd * (dxh - jnp.mean(dxh, axis=-1, keepdims=True) - xh * jnp.mean(dxh * xh, axis=-1, keepdims=True))
        return du2, dl, xh

    rc, cc = 32, 256

    def body(ds_ref, dsn_ref, u2_ref, u2n_ref, a_ref, wd_ref, lg_ref, lb_ref, da_ref, st_ref, s1_ref, acc_ref):
        i = pl.program_id(0)
        last = i == pl.num_programs(0) - 1
        _init_stats(st_ref)
        _init_stats(acc_ref)
        lg, lb = lg_ref[...], lb_ref[...]
        du2, dl, xh = du2_of(ds_ref[...].astype(F32), u2_ref[...].astype(F32), lg, lb)
        st_ref[32:33, :] += _rsum(dl * xh)
        st_ref[33:34, :] += _rsum(dl)
        st_ref[31:32, :] += _rsum(du2)
        du2n, _, _ = du2_of(dsn_ref[...].astype(F32), u2n_ref[...].astype(F32), lg, lb)
        s1_ref[0, 0:tm, :] = du2
        s1_ref[0, tm:tm + hb, :] = jnp.where(last, 0.0, du2n)
        _row_shifted_copies(s1_ref, tm + hb - 8)

        def taps(r0, _):
            rows = pl.ds(r0, rc)
            for c0 in range(0, D, cc):
                sg = _sigmoid(a_ref[rows, D + c0:D + c0 + cc].astype(F32))
                u = a_ref[rows, c0:c0 + cc].astype(F32) * sg
                du = jnp.zeros((rc, cc), F32)
                for k in range(CFW):
                    o = CFW - 1 - k
                    sh = s1_ref[o % 8, pl.ds(pl.multiple_of(r0 + 8 * (o // 8), 8), rc), c0:c0 + cc]
                    du = du + wd_ref[k:k + 1, c0:c0 + cc] * sh
                    acc_ref[8 * k:8 * k + 8, c0:c0 + cc] += _fold8(u * sh)
                dav = du * sg
                dgv = du * u * (1.0 - sg)
                acc_ref[8 * CFW:8 * CFW + 8, c0:c0 + cc] += _fold8(dav)
                acc_ref[8 * CFW + 8:8 * CFW + 16, c0:c0 + cc] += _fold8(dgv)
                da_ref[rows, c0:c0 + cc] = dav.astype(BF)
                da_ref[rows, D + c0:D + c0 + cc] = dgv.astype(BF)
            return 0

        _chunks(tm, rc, taps)

        @pl.when(last)
        def _():
            for k in range(CFW):
                st_ref[k:k + 1, :] = jnp.sum(acc_ref[8 * k:8 * k + 8, :], axis=0, keepdims=True)
            st_ref[34:35, :] = jnp.sum(acc_ref[8 * CFW:8 * CFW + 8, :], axis=0, keepdims=True)
            st_ref[35:36, :] = jnp.sum(acc_ref[8 * CFW + 8:8 * CFW + 16, :], axis=0, keepdims=True)

    vec = _const((1, D))
    return _call(
        body, name=name, grid=(t // tm,),
        in_specs=[_rows(tm, D), _next(hb, D, tm, t), _rows(tm, D), _next(hb, D, tm, t), _rows(tm, 2 * D),
                  _const((CFW, D)), vec, vec],
        out_specs=[_rows(tm, 2 * D), _const((40, D), single=False)],
        out_shape=[jax.ShapeDtypeStruct((t, 2 * D), BF), jax.ShapeDtypeStruct((40, D), F32)],
        scratch_shapes=[pltpu.VMEM((8, tm + hb, D), F32), pltpu.VMEM((8 * (CFW + 2), D), F32)],
        args=[ds, ds, u2, u2, a, w_dw, ln_g, ln_b], carry=carry)


def pool_bwd(dxo, m, ypre, pw, pscale, gn, gt, x, g, sc, *, tm, name):
    t = dxo.shape[0]
    hb = 16

    def dyp_of(dxv, mv, ypv, ps, gnv, gtv):
        dm, dgt, dgn = _gated_res_bwd(dxv, mv, gnv, gtv)
        return dm * ps, dgt, dgn, _rsum(dm * ypv)

    def body(dx_ref, dxn_ref, m_ref, mn_ref, yp_ref, ypn_ref, pw_ref, ps_ref, gn_ref, gt_ref, x_ref, g_ref, sc_ref,
             dxi_ref, dyp_ref, st_ref, sa_ref, sb_ref):
        i = pl.program_id(0)
        last = i == pl.num_programs(0) - 1
        _init_stats(st_ref)
        ps, gnv, gtv = ps_ref[...], gn_ref[...], gt_ref[...]
        dyp, dgt, dgn, dps = dyp_of(dx_ref[...], m_ref[...].astype(F32), yp_ref[...].astype(F32), ps, gnv, gtv)
        st_ref[0:1, :] += dgt
        st_ref[1:2, :] += dgn
        st_ref[2:3, :] += dps
        st_ref[3:4, :] += _rsum(dyp)
        dypb = dyp.astype(BF)
        dyp_ref[...] = dypb
        dypn, _, _, _ = dyp_of(dxn_ref[...], mn_ref[...].astype(F32), ypn_ref[...].astype(F32), ps, gnv, gtv)
        dypnb = jnp.where(last, 0.0, dypn).astype(BF)
        dpo = []
        for gi, w in enumerate(POOL_WINDOWS):
            c0 = gi * PG
            dp_main = _dot_nt(dypb[:, c0:c0 + PG], pw_ref[gi])
            dp_next = _dot_nt(dypnb[:, c0:c0 + PG], pw_ref[gi])
            dpo.append(dp_main)
            sa_ref[0:tm, c0:c0 + PG] = dp_main / _pool_counts(i, tm, w)
            sa_ref[tm:tm + hb, c0:c0 + PG] = dp_next / float(w)
        zero = jnp.zeros((8, D), F32)
        sa_ref[tm + hb:tm + hb + 8, :] = zero
        sb_ref[tm + hb:tm + hb + 8, :] = zero
        n = tm + hb
        src, dst = sa_ref, sb_ref
        dhs = []
        for gi, w in enumerate(POOL_WINDOWS):
            c0 = gi * PG
            step = w // 2
            dst[pl.ds(0, n), c0:D] = src[pl.ds(0, n), c0:D] + src[pl.ds(step, n), c0:D]
            dhs.append(dst[pl.ds(0, tm), c0:c0 + PG] - dpo[gi])
            src, dst = dst, src
        dxi_ref[...] = _adaln_bwd(jnp.concatenate(dhs, axis=1), x_ref, g_ref, sc_ref, dx_ref, st_ref, row=4)

    vec = _const((1, D))
    return pl.pallas_call(
        body, name=name, grid=(t // tm,),
        in_specs=[_rows(tm, D), _next(hb, D, tm, t), _rows(tm, D), _next(hb, D, tm, t), _rows(tm, D),
                  _next(hb, D, tm, t), _const((4, PG, PG)), vec, vec, vec, _rows(tm, D), vec, vec],
        out_specs=[_rows(tm, D), _rows(tm, D), _const((8, D), single=False)],
        out_shape=[jax.ShapeDtypeStruct((t, D), F32), jax.ShapeDtypeStruct((t, D), BF), jax.ShapeDtypeStruct((8, D), F32)],
        scratch_shapes=[pltpu.VMEM((tm + hb + 8, D), F32)] * 2, compiler_params=_params(1),
    )(dxo, dxo, m, m, ypre, ypre, pw, pscale, gn, gt, x, g, sc)


def ffn_bwd_out(dxo, f, gn, gt, w_down, gc, up, *, tm, name, carry=None):
    t = dxo.shape[0]

    def body(dx_ref, f_ref, gn_ref, gt_ref, w_ref, gc_ref, val_ref, df_ref, dgc_ref, dval_ref, st_ref, sc_ref):
        _init_stats(st_ref)
        _init_stats(sc_ref)
        dm, dgt, dgn = _gated_res_bwd(dx_ref[...], f_ref[...].astype(F32), gn_ref[...], gt_ref[...])
        st_ref[0:1, :] += dgt
        st_ref[1:2, :] += dgn
        dmb = dm.astype(BF)
        df_ref[...] = dmb
        da_all = _dot_nt(dmb, w_ref[...])
        for j in range(4):
            da = da_all[:, j * FB:(j + 1) * FB]
            gcv = gc_ref[j].astype(F32)
            sg = _sigmoid(gcv)
            dval_ref[j] = (da * (gcv * sg)).astype(BF)
            dgc = da * val_ref[j].astype(F32) * (sg * (1.0 + gcv * (1.0 - sg)))
            dgc_ref[j] = dgc.astype(BF)
            sc_ref[j, 0:1, :] += _rsum(dgc)

    vec = _const((1, D))
    return _call(
        body, name=name, grid=(t // tm,),
        in_specs=[_rows(tm, D), _rows(tm, D), vec, vec, _const((F, D)), _brows(4, tm, FB), _brows(4, tm, FB, 1)],
        out_specs=[_rows(tm, D), _brows(4, tm, FB), _brows(4, tm, FB, 1), _const((8, D), single=False),
                   _const((4, 8, FB), single=False)],
        out_shape=[jax.ShapeDtypeStruct((t, D), BF), jax.ShapeDtypeStruct((4, t, FB), BF), jax.ShapeDtypeStruct((8, t, FB), BF),
                   jax.ShapeDtypeStruct((8, D), F32), jax.ShapeDtypeStruct((4, 8, FB), F32)],
        args=[dxo, f, gn, gt, w_down, gc, up], carry=carry)


def ffn_bwd_in(dgc, dup, up, convw, w, x, g, sc, dxo, *, tm, name, carry=None):
    t = x.shape[0]

    def body(dgc_ref, dgcn_ref, dval_ref, gate_ref, cw_ref, w_ref, x_ref, g_ref, sc_ref, dxo_ref,
             dx_ref, dgate_ref, st_ref, sc2_ref):
        last = pl.program_id(0) == pl.num_programs(0) - 1
        _init_stats(st_ref)
        _init_stats(sc2_ref)
        dh = jnp.zeros((tm, D), F32)
        for j in range(4):
            dgc = dgc_ref[j].astype(F32)
            ext = jnp.concatenate([dgc, jnp.where(last, 0.0, dgcn_ref[j, 0:8, :].astype(F32))], axis=0)
            e1, e2 = ext[1:tm + 1], ext[2:tm + 2]
            dgate = (cw_ref[j, 2:3, :] * dgc + cw_ref[j, 1:2, :] * e1 + cw_ref[j, 0:1, :] * e2).astype(BF)
            dgate_ref[j] = dgate
            dh = dh + _dot(dgate, w_ref[j])
            gate = gate_ref[j].astype(F32)
            sc2_ref[j, 1:2, :] += _rsum(gate * e2)
            sc2_ref[j, 2:3, :] += _rsum(gate * e1)
            sc2_ref[j, 3:4, :] += _rsum(gate * dgc)
        for j in range(4):
            dh = dh + _dot(dval_ref[j], w_ref[4 + j])
        dx_ref[...] = _adaln_bwd(dh, x_ref, g_ref, sc_ref, dxo_ref, st_ref)

    vec = _const((1, D))
    return _call(
        body, name=name, grid=(t // tm,),
        in_specs=[_brows(4, tm, FB), _bnext(4, 16, FB, tm, t), _brows(4, tm, FB, 1), _brows(4, tm, FB, 0), _const((4, 3, FB)),
                  _const(w.shape), _rows(tm, D), vec, vec, _rows(tm, D)],
        out_specs=[_rows(tm, D), _brows(4, tm, FB, 0), _const((8, D), single=False), _const((4, 8, FB), single=False)],
        out_shape=[jax.ShapeDtypeStruct((t, D), F32), jax.ShapeDtypeStruct((8, t, FB), BF), jax.ShapeDtypeStruct((8, D), F32),
                   jax.ShapeDtypeStruct((4, 8, FB), F32)],
        args=[dgc, dgc, dup, up, convw, w, x, g, sc, dxo], aliases={2: 1}, carry=carry)


def wgrad(a, b, *, nblk, a_blocked, b_blocked, bk, bn, tt, name, carry=None):
    t = a.shape[1] if a.ndim == 3 else a.shape[0]
    nt = t // tt

    def body(a_ref, b_ref, o_ref, acc_ref):
        s = pl.program_id(1)

        @pl.when(s == 0)
        def _():
            acc_ref[...] = jnp.zeros_like(acc_ref)

        av = a_ref[0] if a.ndim == 3 else a_ref[...]
        bv = b_ref[0] if b.ndim == 3 else b_ref[...]
        acc_ref[...] += _dot_tn(av, bv)

        @pl.when(s == nt - 1)
        def _():
            o_ref[0] = acc_ref[...].astype(BF)

    def spec(arr, blocked, width):
        if arr.ndim == 3:
            return pl.BlockSpec((1, tt, width), lambda j, s: (j, s, 0))
        if blocked:
            return pl.BlockSpec((tt, width), lambda j, s: (s, j))
        return pl.BlockSpec((tt, width), lambda j, s: (s, 0))

    return _call(
        body, name=name, grid=(nblk, nt), in_specs=[spec(a, a_blocked, bk), spec(b, b_blocked, bn)],
        out_specs=[pl.BlockSpec((1, bk, bn), lambda j, s: (j, 0, 0))],
        out_shape=[jax.ShapeDtypeStruct((nblk, bk, bn), BF)],
        scratch_shapes=[pltpu.VMEM((bk, bn), F32)], args=[a, b], carry=carry)


def mod_partial(c_all, w_mod):
    cols = w_mod.shape[2]

    def body(c_ref, w_ref, o_ref):
        c = c_ref[...]
        ca = c * _sigmoid(c)
        o_ref[0] = jnp.dot(ca, w_ref[0], preferred_element_type=F32, precision=lax.Precision.HIGHEST)

    return pl.pallas_call(
        body, name="mod_partial", grid=(DEPTH,),
        in_specs=[pl.BlockSpec((NDEV, D), lambda l: (0, 0)), pl.BlockSpec((1, D, cols), lambda l: (l, 0, 0))],
        out_specs=pl.BlockSpec((1, NDEV, cols), lambda l: (l, 0, 0)),
        out_shape=jax.ShapeDtypeStruct((DEPTH, NDEV, cols), F32), compiler_params=_params(1))(c_all, w_mod)


def mod_finish(parts, b_mod):
    cols = parts.shape[2]

    def body(p_ref, b_ref, o_ref):
        for e in range(NDEV):
            o_ref[:, e * cols:(e + 1) * cols] = p_ref[e] + b_ref[:, e * cols:(e + 1) * cols]

    return pl.pallas_call(
        body, name="mod_finish", out_shape=jax.ShapeDtypeStruct((DEPTH, NDEV * cols), F32))(parts, b_mod)


def sum_parts(parts):
    n, r, c = parts.shape

    def body(p_ref, o_ref):
        acc = p_ref[0]
        for j in range(1, n):
            acc = acc + p_ref[j]
        o_ref[...] = acc

    return pl.pallas_call(body, name="sum_parts", out_shape=jax.ShapeDtypeStruct((r, c), F32))(parts)


def mod_wgrad(c_all_t, gmod_cols):
    cols = gmod_cols.shape[2]

    def body(c_ref, g_ref, o_ref):
        c = c_ref[...]
        ca = c * _sigmoid(c)
        acc = ca[:, 0:1] * g_ref[0, 0:1, :]
        for b in range(1, NDEV):
            acc = acc + ca[:, b:b + 1] * g_ref[0, b:b + 1, :]
        o_ref[0] = acc

    return pl.pallas_call(
        body, name="mod_wgrad", grid=(DEPTH,),
        in_specs=[pl.BlockSpec((D, NDEV), lambda l: (0, 0)), pl.BlockSpec((1, NDEV, cols), lambda l: (l, 0, 0))],
        out_specs=pl.BlockSpec((1, D, cols), lambda l: (l, 0, 0)),
        out_shape=jax.ShapeDtypeStruct((DEPTH, D, cols), F32), compiler_params=_params(1))(c_all_t, gmod_cols)


def _adamw_math(g, w, m, v):
    m2 = B1 * m + (1.0 - B1) * g
    v2 = B2 * v + (1.0 - B2) * (g * g)
    m_hat = m2 / (1.0 - B1 ** STEP)
    v_hat = v2 / (1.0 - B2 ** STEP)
    delta = -LR * (m_hat / (jnp.sqrt(v_hat) + ADAM_EPS) + WD * w)
    return delta, m2, v2


def _row_tile(r, c, budget=1 << 18):
    if r * c <= budget or r % 8:
        return r
    best = 8
    for cand in range(8, r + 1, 8):
        if r % cand == 0 and cand * c <= budget:
            best = cand
    return best


def adamw_sum(parts, w, m, v, *, name):
    n, r, c = parts.shape
    tr = _row_tile(r, c)

    def body(p_ref, w_ref, m_ref, v_ref, g_ref, d_ref, m2_ref, v2_ref):
        g = p_ref[0].astype(F32)
        for j in range(1, n):
            g = g + p_ref[j].astype(F32)
        d, m2, v2 = _adamw_math(g, w_ref[...], m_ref[...], v_ref[...])
        g_ref[...] = g
        d_ref[...] = d
        m2_ref[...] = m2
        v2_ref[...] = v2

    blk = pl.BlockSpec((tr, c), lambda i: (i, 0))
    out = jax.ShapeDtypeStruct((r, c), F32)
    return pl.pallas_call(
        body, name=name, grid=(r // tr,), in_specs=[pl.BlockSpec((n, tr, c), lambda i: (0, i, 0)), blk, blk, blk],
        out_specs=[blk] * 4, out_shape=[out] * 4, compiler_params=_params(1))(parts, w, m, v)


def adamw_layer(parts, w, m, v, prev, layer, *, name):
    n, r, c = parts.shape
    nl = w.shape[0]
    tr = _row_tile(r, c)

    def body(p_ref, w_ref, m_ref, v_ref, *rest):
        g_ref, d_ref, m2_ref, v2_ref = rest[-4:]
        g = p_ref[0].astype(F32)
        for j in range(1, n):
            g = g + p_ref[j].astype(F32)
        d, m2, v2 = _adamw_math(g, w_ref[0], m_ref[0], v_ref[0])
        g_ref[0] = g
        d_ref[0] = d
        m2_ref[0] = m2
        v2_ref[0] = v2

    blk = pl.BlockSpec((1, tr, c), lambda i: (layer, i, 0))
    in_specs = [pl.BlockSpec((n, tr, c), lambda i: (0, i, 0)), blk, blk, blk]
    args = [parts, w, m, v]
    aliases = {}
    if prev is not None:
        in_specs += [ANY] * 4
        args += list(prev)
        aliases = {4 + k: k for k in range(4)}
    out = jax.ShapeDtypeStruct((nl, r, c), F32)
    return pl.pallas_call(
        body, name=name, grid=(r // tr,), in_specs=in_specs, out_specs=[blk] * 4, out_shape=[out] * 4,
        input_output_aliases=aliases, compiler_params=_params(1))(*args)


def _pack(arrays):
    flat, layout, off = [], [], 0
    for a in arrays:
        flat.append(a.reshape(-1))
        layout.append((off, a.shape))
        off += a.size
    pad = (-off) % 1024
    if pad:
        flat.append(jnp.zeros((pad,), F32))
    return jnp.concatenate(flat).reshape(-1, 128), layout


def _unpack(packed, layout, lead=()):
    flat = packed.reshape(lead + (-1,))
    return [flat[..., off:off + _size(shape)].reshape(lead + tuple(shape)) for off, shape in layout]


def _size(shape):
    n = 1
    for s in shape:
        n *= s
    return n


def _join_last(g):
    g = jnp.moveaxis(g, 0, -2)
    return g.reshape(g.shape[:-2] + (g.shape[-2] * g.shape[-1],))


def _my_cols(a, width):
    return lax.dynamic_slice_in_dim(a, _my_id() * width, width, axis=a.ndim - 1)


def _tile(t, pref):
    return min(pref, t)


def kernel(x, c, w_mod, b_mod, norm_g, sc_w_in, sc_conv, sc_w_out, pool_w, pool_b, pool_scale, cf_w_pw1, cf_b_pw1, cf_w_dw, cf_b_dw, cf_ln_g, cf_ln_b, cf_w_pw2, cf_b_pw2, ffn_w_up, ffn_conv, ffn_b_conv, ffn_w_down, loss_target, m_w_mod, m_b_mod, m_norm_g, m_sc_w_in, m_sc_conv, m_sc_w_out, m_pool_w, m_pool_b, m_pool_scale, m_cf_w_pw1, m_cf_b_pw1, m_cf_w_dw, m_cf_b_dw, m_cf_ln_g, m_cf_ln_b, m_cf_w_pw2, m_cf_b_pw2, m_ffn_w_up, m_ffn_conv, m_ffn_b_conv, m_ffn_w_down, v_w_mod, v_b_mod, v_norm_g, v_sc_w_in, v_sc_conv, v_sc_w_out, v_pool_w, v_pool_b, v_pool_scale, v_cf_w_pw1, v_cf_b_pw1, v_cf_w_dw, v_cf_b_dw, v_cf_ln_g, v_cf_ln_b, v_cf_w_pw2, v_cf_b_pw2, v_ffn_w_up, v_ffn_conv, v_ffn_b_conv, v_ffn_w_down):
    env = dict(locals())
    names = ["w_mod", "b_mod", "norm_g", "sc_w_in", "sc_conv", "sc_w_out", "pool_w", "pool_b", "pool_scale", "cf_w_pw1",
             "cf_b_pw1", "cf_w_dw", "cf_b_dw", "cf_ln_g", "cf_ln_b", "cf_w_pw2", "cf_b_pw2", "ffn_w_up", "ffn_conv",
             "ffn_b_conv", "ffn_w_down"]
    t = x.shape[1]
    tm = _tile(t, 512)
    tt = _tile(t, 4096)
    x0, target = x[0], loss_target[0]

    small_names = ["norm_g", "sc_conv", "cf_b_pw1", "cf_w_dw", "cf_b_dw", "cf_ln_g", "cf_ln_b", "cf_b_pw2", "ffn_conv"]
    packed, layout = _pack([c] + [env[n] for n in small_names])

    shard = {"pool": pool_w[0].astype(BF), "pw1": cf_w_pw1[0].astype(BF), "pw2": cf_w_pw2[0].astype(BF)}
    for j in range(2):
        shard[f"in{j}"], shard[f"out{j}"] = sc_w_in[j].astype(BF), sc_w_out[j].astype(BF)
    for l in range(DEPTH):
        shard[f"up{l}"], shard[f"down{l}"] = ffn_w_up[l].T.astype(BF), ffn_w_down[l].astype(BF)
    gathered, g_in0, g_out0 = _Exchange("gather", [packed, shard["in0"], shard["out0"]]).run_two_level("gather_first")
    wg = {"in0": g_in0, "out0": g_out0}
    parts = _unpack(gathered, layout, lead=(NDEV,))
    c_all = parts[0].reshape(NDEV, D)
    full = {n: _join_last(p) for n, p in zip(small_names, parts[1:])}
    fwd_plan = {("mix_in", 0): ["up0"], ("mix_out", 0): ["down0"], ("ffn_in", 0): ["pool", "up1"],
                ("ffn_out", 0): ["down1", "pw1", "pw2"], ("ffn_in", 1): ["up2"], ("ffn_out", 1): ["down2", "in1", "out1"],
                ("ffn_in", 2): ["up3"], ("ffn_out", 2): ["down3"]}

    def carrying(plan, kind, store, source, fn, key, *a, **k):
        names = plan.get(key)
        if not names:
            return fn(*a, **k)
        res = fn(*a, carry=_Exchange(kind, [source[n] for n in names]), **k)
        store.update(zip(names, res[-1]))
        return res[:-1]

    fwd = functools.partial(carrying, fwd_plan, "gather", wg, shard)

    mp = mod_partial(c_all, w_mod)
    (mod_parts,) = _Exchange("scatter", [jnp.swapaxes(mp, 0, 1)]).run("exchange_mod")
    mod = mod_finish(mod_parts, b_mod)

    def vec(a):
        return a.reshape(1, -1)

    def col_blocks(g):
        w = jnp.swapaxes(g, 0, 1).reshape(D, -1)
        return jnp.swapaxes(w.reshape(D, -1, D), 0, 1)

    def ffn_blocks(a):
        return jnp.swapaxes(a.reshape(a.shape[0], 4, FB), 0, 1)

    saved = []
    xs = x0
    for l in range(DEPTH):
        sh1, sc1, g1, sh2, sc2, g2 = [mod[l:l + 1, k * D:(k + 1) * D] for k in range(6)]
        ng = [full["norm_g"][l, k:k + 1] for k in range(4)]
        kind, j = l % 3, l // 3
        s = dict(x_in=xs, sc1=sc1, g1=g1, sc2=sc2, g2=g2, ng=ng)
        if kind == 0:
            s["w_in"] = col_blocks(wg[f"in{j}"])
            s["h"], s["p"] = fwd(fwd_in, ("mix_in", l), xs, ng[0], sc1, sh1, s["w_in"], None, blocked=False, tm=tm,
                                 name=f"sc_in_{l}")
            x1, s["m"], s["q"] = fwd(sc_fwd_out, ("mix_out", l), s["p"], full["sc_conv"][j], wg[f"out{j}"].reshape(D, D), xs,
                                     ng[1], g1, tm=tm, name=f"sc_out_{l}")
        elif kind == 1:
            pool_w_f = jnp.swapaxes(wg["pool"], 0, 1).reshape(4, PG, PG)
            x1, s["m"], s["ypre"], s["pooled"] = pool_fwd(xs, ng[0], sc1, sh1, pool_w_f, pool_b, pool_scale, ng[1], g1,
                                                          tm=tm, name=f"pool_{l}")
        else:
            s["w_in"] = col_blocks(wg["pw1"])
            s["h"], s["a"] = fwd_in(xs, ng[0], sc1, sh1, s["w_in"], full["cf_b_pw1"].reshape(2, 1, D), blocked=False, tm=tm,
                                    name=f"cf_in_{l}")
            x1, s["m"], s["s"], s["u2"] = cf_fwd_out(s["a"], full["cf_w_dw"][0], full["cf_b_dw"], full["cf_ln_g"],
                                                     full["cf_ln_b"], wg["pw2"].reshape(D, D), full["cf_b_pw2"], xs, ng[1],
                                                     g1, tm=tm, name=f"cf_out_{l}")
        s["x1"] = x1
        s["cw"] = ffn_blocks(full["ffn_conv"][l])
        s["h2"], s["up"] = fwd(fwd_in, ("ffn_in", l), x1, ng[2], sc2, sh2, wg[f"up{l}"], None, blocked=True, wt=True, tm=tm,
                               name=f"ffn_in_{l}")
        xs, s["f"], s["gc"], s["fa"], *loss_part = fwd(
            ffn_fwd_out, ("ffn_out", l), s["up"], s["cw"], ffn_blocks(ffn_b_conv[l:l + 1]), wg[f"down{l}"].reshape(F, D), x1,
            ng[3], g2, tm=tm, name=f"ffn_out_{l}", target=target if l == DEPTH - 1 else None)
        saved.append(s)

    dx = xs
    loss = lax.psum(loss_part[0][0, 0], ("x", "y", "c"))

    gmod = [None] * DEPTH
    d_norm_g = [None] * DEPTH
    d_ffn_conv = [None] * DEPTH
    d_ffn_b_conv = [None] * DEPTH
    d_sc_conv = [None] * 2
    big = {}
    got = {}
    small_g = {}
    bwd_plan = {("mix_bout", 3): ["down3"], ("mix_bin", 3): ["up3"], ("ffn_bout", 2): ["in1", "out1"],
                ("mix_bmid", 2): ["up2", "down2"], ("ffn_bout", 1): ["pw1", "pw2"], ("ffn_bout", 0): ["pool", "down1"],
                ("ffn_bin", 0): ["up1"], ("mix_bout", 0): ["down0"], ("mix_win", 0): ["up0"], ("mix_wout", 0): ["in0"]}
    bwd = functools.partial(carrying, bwd_plan, "scatter", got, big)
    pool_w_f = jnp.swapaxes(wg["pool"], 0, 1).reshape(4, PG, PG)
    for l in reversed(range(DEPTH)):
        s = saved[l]
        ng = s["ng"]
        kind, j = l % 3, l // 3
        df, dgc, dup, st_o, st_b = bwd(ffn_bwd_out, ("ffn_bout", l), dx, s["f"], ng[3], s["g2"], wg[f"down{l}"].reshape(F, D),
                                       s["gc"], s["up"], tm=tm, name=f"ffn_bout_{l}")
        dx1, dup, st_i, st_c = bwd(ffn_bwd_in, ("ffn_bin", l), dgc, dup, s["up"], s["cw"], wg[f"up{l}"], s["x1"], ng[2],
                                   s["sc2"], dx, tm=tm, name=f"ffn_bin_{l}")
        (big[f"up{l}"],) = wgrad(dup, s["h2"], nblk=NDEV, a_blocked=True, b_blocked=False, bk=FB, bn=D, tt=tt,
                                 name=f"ffn_wup_{l}")
        big[f"down{l}"] = wgrad(s["fa"], df, nblk=4, a_blocked=True, b_blocked=False, bk=FB, bn=D, tt=tt,
                                name=f"ffn_wdown_{l}")[0].reshape(NDEV, F // NDEV, D)
        d_ffn_b_conv[l] = st_b[:, 0, :].reshape(F)
        d_ffn_conv[l] = jnp.swapaxes(st_c[:, 1:4, :], 0, 1).reshape(3, F)
        g_ffn = [st_i[0], st_i[1], st_o[0]]
        dn3, dn2 = st_o[1], st_i[2]
        if kind == 0:
            dm, dbg, du, st_o = bwd(sc_bwd_out, ("mix_bout", l), dx1, s["m"], ng[1], s["g1"], wg[f"out{j}"].reshape(D, D), s["p"],
                                    full["sc_conv"][j], tm=tm, name=f"sc_bout_{l}")
            dx, dp, st_i, st_c = bwd(sc_bwd_in, ("mix_bin", l), du, dbg, s["p"], full["sc_conv"][j], s["w_in"], s["x_in"], ng[0],
                                     s["sc1"], dx1, tm=tm, name=f"sc_bin_{l}")
            (w_pairs,) = bwd(wgrad, ("mix_win", l), s["h"], dp, nblk=NDEV // 2, a_blocked=False, b_blocked=True, bk=D,
                             bn=6 * D // NDEV, tt=tt, name=f"sc_win_{l}")
            big[f"in{j}"] = jnp.swapaxes(w_pairs.reshape(NDEV // 2, D, 2, 3 * D // NDEV), 1, 2).reshape(NDEV, D, 3 * D // NDEV)
            big[f"out{j}"] = bwd(wgrad, ("mix_wout", l), s["q"], dm, nblk=1, a_blocked=False, b_blocked=False, bk=D, bn=D, tt=tt,
                                 name=f"sc_wout_{l}")[0].reshape(NDEV, D // NDEV, D)
            d_sc_conv[j] = st_c[0:3]
        elif kind == 1:
            dx, dyp, st_o = pool_bwd(dx1, s["m"], s["ypre"], pool_w_f, pool_scale, ng[1], s["g1"], s["x_in"], ng[0], s["sc1"],
                                     tm=tm, name=f"pool_b_{l}")
            st_i = st_o[4:7]
            (dpw,) = wgrad(s["pooled"], dyp, nblk=4, a_blocked=True, b_blocked=True, bk=PG, bn=PG, tt=tt, name=f"pool_w_{l}")
            big["pool"] = jnp.swapaxes(dpw.reshape(4, NDEV, PG // NDEV, PG), 0, 1).reshape(NDEV, 4 * PG // NDEV, PG)
            small_g["pool_scale"], small_g["pool_b"] = st_o[2:3], st_o[3:4]
        else:
            dm, ds, st_o = bwd_out(dx1, s["m"], ng[1], s["g1"], wg["pw2"].reshape(D, D), tm=tm,
                                   name=f"cf_bout_{l}")
            dA, st_c = bwd(cf_bwd_mid, ("mix_bmid", l), ds, s["u2"], s["a"], full["cf_w_dw"][0], full["cf_ln_g"],
                           full["cf_ln_b"], tm=tm, name=f"cf_bmid_{l}")
            dx, st_i = bwd_in(dA, s["w_in"], s["x_in"], ng[0], s["sc1"], dx1, tm=tm, name=f"cf_bin_{l}")
            (w_quads,) = wgrad(s["h"], dA, nblk=2, a_blocked=False, b_blocked=True, bk=D, bn=D, tt=tt, name=f"cf_wpw1_{l}")
            big["pw1"] = jnp.swapaxes(w_quads.reshape(2, D, 4, 2 * D // NDEV), 1, 2).reshape(NDEV, D, 2 * D // NDEV)
            big["pw2"] = wgrad(s["s"], dm, nblk=1, a_blocked=False, b_blocked=False, bk=D, bn=D, tt=tt,
                               name=f"cf_wpw2_{l}")[0].reshape(NDEV, D // NDEV, D)
            small_g["cf_w_dw"] = st_c[0:CFW][None]
            small_g["cf_b_dw"], small_g["cf_ln_g"], small_g["cf_ln_b"] = st_c[31:32], st_c[32:33], st_c[33:34]
            small_g["cf_b_pw1"] = st_c[34:36].reshape(1, 2 * D)
            small_g["cf_b_pw2"] = st_o[2:3]
        gmod[l] = jnp.concatenate([st_i[0], st_i[1], st_o[0]] + g_ffn)
        d_norm_g[l] = jnp.stack([st_i[2], st_o[1], dn2, dn3])

    small_g["gmod"] = jnp.stack(gmod)
    small_g["norm_g"] = jnp.stack(d_norm_g)
    small_g["sc_conv"] = jnp.stack(d_sc_conv)
    small_g["ffn_conv"] = jnp.stack(d_ffn_conv)
    small_g["ffn_b_conv"] = jnp.stack(d_ffn_b_conv)
    sg_names = ["gmod", "norm_g", "sc_conv", "pool_b", "pool_scale", "cf_b_pw1", "cf_w_dw", "cf_b_dw", "cf_ln_g", "cf_ln_b",
                "cf_b_pw2", "ffn_conv", "ffn_b_conv"]
    gpacked, glayout = _pack([small_g[n] for n in sg_names])
    (got["out0"],) = _Exchange("scatter", [big["out0"]]).run("scatter_last")
    (ggath,) = _Exchange("gather", [gpacked]).run_two_level("gather_small_grads")
    gsum = dict(zip(sg_names, _unpack(sum_parts(ggath), glayout)))
    gmod_all = _unpack(ggath, glayout[:1], lead=(NDEV,))[0]
    grads = {"b_mod": gsum["gmod"], "pool_b": gsum["pool_b"], "pool_scale": gsum["pool_scale"],
             "ffn_b_conv": gsum["ffn_b_conv"]}
    for n in ["norm_g", "sc_conv", "cf_b_pw1", "cf_w_dw", "cf_b_dw", "cf_ln_g", "cf_ln_b", "cf_b_pw2", "ffn_conv"]:
        grads[n] = _my_cols(gsum[n], env[n].shape[-1])
    grads["w_mod"] = mod_wgrad(c_all.T, jnp.swapaxes(_my_cols(gmod_all, w_mod.shape[2]), 0, 1))

    deltas, new_m, new_v = {}, {}, {}
    sp_names = ["b_mod", "norm_g", "sc_conv", "pool_b", "pool_scale", "cf_b_pw1", "cf_w_dw", "cf_b_dw", "cf_ln_g", "cf_ln_b",
                "cf_b_pw2", "ffn_conv", "ffn_b_conv"]
    pg, playout = _pack([grads[n] for n in sp_names])
    pw_, _ = _pack([env[n] for n in sp_names])
    pm_, _ = _pack([env["m_" + n] for n in sp_names])
    pv_, _ = _pack([env["v_" + n] for n in sp_names])
    _, sd, sm, sv = adamw_sum(pg[None], pw_, pm_, pv_, name="adamw_small")
    for n, d_, m_, v_ in zip(sp_names, _unpack(sd, playout), _unpack(sm, playout), _unpack(sv, playout)):
        deltas[n], new_m[n], new_v[n] = d_, m_, v_
    gw = grads["w_mod"].reshape(1, DEPTH * D, -1)
    _, d_, m_, v_ = adamw_sum(gw, w_mod.reshape(gw.shape[1:]), m_w_mod.reshape(gw.shape[1:]), v_w_mod.reshape(gw.shape[1:]),
                              name="adamw_w_mod")
    deltas["w_mod"], new_m["w_mod"], new_v["w_mod"] = [a.reshape(w_mod.shape) for a in (d_, m_, v_)]

    groups = {"sc_w_in": ["in0", "in1"], "sc_w_out": ["out0", "out1"], "pool_w": ["pool"], "cf_w_pw1": ["pw1"],
              "cf_w_pw2": ["pw2"], "ffn_w_up": [f"up{l}" for l in range(DEPTH)], "ffn_w_down": [f"down{l}" for l in range(DEPTH)]}
    for n, layers in groups.items():
        stacked = (len(layers),) + got[layers[0]].shape[1:]
        flip = n == "ffn_w_up"
        w3 = [(jnp.swapaxes(env[p + n], 1, 2) if flip else env[p + n]).reshape(stacked) for p in ("", "m_", "v_")]
        outs = None
        for li, key in enumerate(layers):
            outs = adamw_layer(got[key], *w3, outs, li, name=f"adamw_{n}_{li}")
        grads[n], deltas[n], new_m[n], new_v[n] = [(jnp.swapaxes(a, 1, 2) if flip else a).reshape(env[n].shape) for a in outs]

    return (loss, dx[None], *[grads[n] for n in names], *[deltas[n] for n in names], *[new_m[n] for n in names],
            *[new_v[n] for n in names])
```
